```python
import math
import jax, jax.numpy as jnp
from jax import lax
import numpy as np

D_MODEL = 1024
BATCH = 8
SEQ = 4096
DEPTH = 4

D_CONV = D_MODEL // 2
CONV_WIDTH = 3
D_SSM = D_MODEL // 2
SSM_GROUP = 16
N_SSM_GROUPS = D_SSM // SSM_GROUP
SSM_STATE = 64
D_POOL = D_MODEL // 2
POOL_WINDOWS = (2, 4, 8, 16)
POOL_GROUP = D_POOL // len(POOL_WINDOWS)
D_SGU = D_MODEL // 2
SGU_HEADS = 4
SGU_HEAD_DIM = D_SGU // SGU_HEADS
CHUNK = 128
D_FF = ((8 * D_MODEL // 3 + 127) // 128) * 128
N_EVEN = (DEPTH + 1) // 2
N_ODD = DEPTH // 2
EPS = 1e-6

kernel_name = "hybrid_conv_s5_pool_sgu_trunk"


def rmsnorm(x, g):
    xf = x.astype(jnp.float32)
    y = xf * lax.rsqrt(jnp.mean(xf * xf, axis=-1, keepdims=True) + EPS)
    return (y * g.astype(jnp.float32)).astype(x.dtype)


def causal_dwconv(x, w):
    L = x.shape[1]
    K = w.shape[0]
    xp = jnp.pad(x, ((0, 0), (K - 1, 0), (0, 0)))
    y = xp[:, 0:L] * w[0]
    for k in range(1, K):
        y = y + xp[:, k:k + L] * w[k]
    return y


def short_conv_mixer(xa, ba, ca, conv_w):
    return ba * causal_dwconv(ca * xa, conv_w)


def s5_mixer(u, log_step, a_re, a_im, b_re, b_im, c_re, c_im, d_skip, glu_w, glu_b):
    f32 = jnp.float32
    Bsz, L, _ = u.shape
    uf = u.astype(f32).reshape(Bsz, L, N_SSM_GROUPS, SSM_GROUP)
    lam = lax.complex(a_re.astype(f32), a_im.astype(f32))
    step = jnp.exp(log_step.astype(f32))[:, None]
    lam_bar = jnp.exp(lam * step)
    b_tilde = lax.complex(b_re.astype(f32), b_im.astype(f32))
    b_bar = ((lam_bar - 1.0) / lam)[..., None] * b_tilde
    bu = jnp.einsum('gph,blgh->blgp', b_bar, uf.astype(jnp.complex64))
    a_elems = jnp.broadcast_to(lam_bar, bu.shape)

    def combine(left, right):
        a_l, b_l = left
        a_r, b_r = right
        return a_r * a_l, a_r * b_l + b_r

    _, states = lax.associative_scan(combine, (a_elems, bu), axis=1)
    c_tilde = lax.complex(c_re.astype(f32), c_im.astype(f32))
    y = jnp.real(jnp.einsum('ghp,blgp->blgh', c_tilde, states))
    y = y + d_skip.astype(f32).reshape(N_SSM_GROUPS, SSM_GROUP) * uf
    y = jax.nn.gelu(y.reshape(Bsz, L, D_SSM))
    y = y * jax.nn.sigmoid(y @ glu_w.astype(f32) + glu_b.astype(f32))
    return y.astype(u.dtype)


def pool_mixer(z, pool_w, pool_scale):
    f32 = jnp.float32
    Bsz, L, _ = z.shape
    zf = z.astype(f32).reshape(Bsz, L, len(POOL_WINDOWS), POOL_GROUP)
    csum = lax.cumsum(zf, axis=1)
    count = jnp.arange(1, L + 1, dtype=f32)[None, :, None]
    outs = []
    for g, w in enumerate(POOL_WINDOWS):
        s = csum[:, :, g]
        lower = jnp.pad(s, ((0, 0), (w, 0), (0, 0)))[:, :L]
        mean = (s - lower) / jnp.minimum(count, w)
        outs.append(mean - zf[:, :, g])
    pooled = jnp.stack(outs, axis=2)
    y = jnp.einsum('blgc,gcd->blgd', pooled, pool_w.astype(f32)).reshape(Bsz, L, D_POOL)
    return (y * pool_scale.astype(f32)).astype(z.dtype)


def sgu_mixer(su, sv, norm_g, sgu_w, sgu_b):
    Bsz, L, _ = su.shape
    v = rmsnorm(sv, norm_g)
    vr = v.reshape(Bsz, L // CHUNK, CHUNK, SGU_HEADS, SGU_HEAD_DIM)
    mask = jnp.tril(jnp.ones((CHUNK, CHUNK), dtype=bool))
    w_s = jnp.where(mask, sgu_w, 0)
    mixed = jnp.einsum('hts,bnshd->bnthd', w_s, vr) + jnp.swapaxes(sgu_b, 0, 1)[:, :, None]
    return su * mixed.reshape(Bsz, L, D_SGU)


def conv_ffn(h, w_up, conv_w, conv_b, w_down):
    up = causal_dwconv(h @ w_up, conv_w) + conv_b
    g, v = jnp.split(up, 2, axis=-1)
    return (jax.nn.silu(g) * v) @ w_down


def _fwd_setup_inputs(seed: int = 0) -> dict:
    key = jax.random.key(seed)
    ks = jax.random.split(key, 32)
    f32 = jnp.float32

    def nrm(k, shape, scale):
        return jax.random.normal(k, shape, f32) * scale

    G, P, Hg = N_SSM_GROUPS, SSM_STATE, SSM_GROUP
    d_even_in = 3 * D_CONV + D_SSM
    d_odd_in = D_POOL + 2 * D_SGU
    a_im_base = jnp.pi * jnp.arange(P, dtype=f32)
    return {
        "x": nrm(ks[0], (BATCH, SEQ, D_MODEL), 1.0),
        "norm_mix_g": 1.0 + nrm(ks[1], (DEPTH, D_MODEL), 0.05),
        "even_w_in": nrm(ks[2], (N_EVEN, D_MODEL, d_even_in), D_MODEL ** -0.5),
        "even_conv_w": nrm(ks[3], (N_EVEN, CONV_WIDTH, D_CONV), CONV_WIDTH ** -0.5),
        "ssm_log_step": jax.random.uniform(ks[4], (N_EVEN, G), f32, math.log(1e-3), math.log(1e-1)),
        "ssm_a_re": -0.5 * (1.0 + nrm(ks[5], (N_EVEN, G, P), 0.01)),
        "ssm_a_im": a_im_base + nrm(ks[6], (N_EVEN, G, P), 0.01),
        "ssm_b_re": nrm(ks[7], (N_EVEN, G, P, Hg), (2 * Hg) ** -0.5),
        "ssm_b_im": nrm(ks[8], (N_EVEN, G, P, Hg), (2 * Hg) ** -0.5),
        "ssm_c_re": nrm(ks[9], (N_EVEN, G, Hg, P), (2 * P) ** -0.5),
        "ssm_c_im": nrm(ks[10], (N_EVEN, G, Hg, P), (2 * P) ** -0.5),
        "ssm_d": nrm(ks[11], (N_EVEN, D_SSM), 1.0),
        "ssm_glu_w": nrm(ks[12], (N_EVEN, D_SSM, D_SSM), D_SSM ** -0.5),
        "ssm_glu_b": nrm(ks[13], (N_EVEN, D_SSM), 0.02),
        "even_w_out": nrm(ks[14], (N_EVEN, D_CONV + D_SSM, D_MODEL), (D_CONV + D_SSM) ** -0.5),
        "odd_w_in": nrm(ks[15], (N_ODD, D_MODEL, d_odd_in), D_MODEL ** -0.5),
        "pool_w": nrm(ks[16], (N_ODD, len(POOL_WINDOWS), POOL_GROUP, POOL_GROUP), POOL_GROUP ** -0.5),
        "pool_scale": 1.0 + nrm(ks[17], (N_ODD, D_POOL), 0.1),
        "sgu_norm_g": 1.0 + nrm(ks[18], (N_ODD, D_SGU), 0.05),
        "sgu_w": nrm(ks[19], (N_ODD, SGU_HEADS, CHUNK, CHUNK), CHUNK ** -0.5),
        "sgu_b": 1.0 + nrm(ks[20], (N_ODD, SGU_HEADS, CHUNK), 0.1),
        "odd_w_out": nrm(ks[21], (N_ODD, D_POOL + D_SGU, D_MODEL), (D_POOL + D_SGU) ** -0.5),
        "norm_ffn_g": 1.0 + nrm(ks[22], (DEPTH, D_MODEL), 0.05),
        "ffn_w_up": nrm(ks[23], (DEPTH, D_MODEL, 2 * D_FF), D_MODEL ** -0.5),
        "ffn_conv_w": nrm(ks[24], (DEPTH, CONV_WIDTH, 2 * D_FF), CONV_WIDTH ** -0.5),
        "ffn_conv_b": nrm(ks[25], (DEPTH, 2 * D_FF), 0.02),
        "ffn_w_down": nrm(ks[26], (DEPTH, D_FF, D_MODEL), D_FF ** -0.5),
        "norm_final_g": 1.0 + nrm(ks[27], (D_MODEL,), 0.05),
    }


def _fwd_reference(x, norm_mix_g, even_w_in, even_conv_w, ssm_log_step, ssm_a_re, ssm_a_im,
              ssm_b_re, ssm_b_im, ssm_c_re, ssm_c_im, ssm_d, ssm_glu_w, ssm_glu_b,
              even_w_out, odd_w_in, pool_w, pool_scale, sgu_norm_g, sgu_w, sgu_b,
              odd_w_out, norm_ffn_g, ffn_w_up, ffn_conv_w, ffn_conv_b, ffn_w_down,
              norm_final_g):
    for i in range(DEPTH):
        h = rmsnorm(x, norm_mix_g[i])
        j = i // 2
        if i % 2 == 0:
            proj = h @ even_w_in[j]
            xa = proj[..., :D_CONV]
            ba = proj[..., D_CONV:2 * D_CONV]
            ca = proj[..., 2 * D_CONV:3 * D_CONV]
            u = proj[..., 3 * D_CONV:]
            ya = short_conv_mixer(xa, ba, ca, even_conv_w[j])
            yb = s5_mixer(u, ssm_log_step[j], ssm_a_re[j], ssm_a_im[j], ssm_b_re[j], ssm_b_im[j],
                          ssm_c_re[j], ssm_c_im[j], ssm_d[j], ssm_glu_w[j], ssm_glu_b[j])
            mix = jnp.concatenate([ya, yb], axis=-1) @ even_w_out[j]
        else:
            proj = h @ odd_w_in[j]
            z = proj[..., :D_POOL]
            uv = jax.nn.gelu(proj[..., D_POOL:])
            su = uv[..., :D_SGU]
            sv = uv[..., D_SGU:]
            yc = pool_mixer(z, pool_w[j], pool_scale[j])
            yd = sgu_mixer(su, sv, sgu_norm_g[j], sgu_w[j], sgu_b[j])
            mix = jnp.concatenate([yc, yd], axis=-1) @ odd_w_out[j]
        x = x + mix
        x = x + conv_ffn(rmsnorm(x, norm_ffn_g[i]), ffn_w_up[i], ffn_conv_w[i], ffn_conv_b[i], ffn_w_down[i])
    return rmsnorm(x, norm_final_g)


import jax as _jax
import jax.numpy as _jnp

TWIN_FORMAT = 'train_step'
FWD_PARAMS = ['x', 'norm_mix_g', 'even_w_in', 'even_conv_w', 'ssm_log_step', 'ssm_a_re', 'ssm_a_im', 'ssm_b_re', 'ssm_b_im', 'ssm_c_re', 'ssm_c_im', 'ssm_d', 'ssm_glu_w', 'ssm_glu_b', 'even_w_out', 'odd_w_in', 'pool_w', 'pool_scale', 'sgu_norm_g', 'sgu_w', 'sgu_b', 'odd_w_out', 'norm_ffn_g', 'ffn_w_up', 'ffn_conv_w', 'ffn_conv_b', 'ffn_w_down', 'norm_final_g']
TWIN_WEIGHTS = ['norm_mix_g', 'even_w_in', 'even_conv_w', 'ssm_log_step', 'ssm_a_re', 'ssm_a_im', 'ssm_b_re', 'ssm_b_im', 'ssm_c_re', 'ssm_c_im', 'ssm_d', 'ssm_glu_w', 'ssm_glu_b', 'even_w_out', 'odd_w_in', 'pool_w', 'pool_scale', 'sgu_norm_g', 'sgu_w', 'sgu_b', 'odd_w_out', 'norm_ffn_g', 'ffn_w_up', 'ffn_conv_w', 'ffn_conv_b', 'ffn_w_down', 'norm_final_g']
TWIN_DIFF_INPUT = 'x'
TWIN_INPUTS = ['x', 'norm_mix_g', 'even_w_in', 'even_conv_w', 'ssm_log_step', 'ssm_a_re', 'ssm_a_im', 'ssm_b_re', 'ssm_b_im', 'ssm_c_re', 'ssm_c_im', 'ssm_d', 'ssm_glu_w', 'ssm_glu_b', 'even_w_out', 'odd_w_in', 'pool_w', 'pool_scale', 'sgu_norm_g', 'sgu_w', 'sgu_b', 'odd_w_out', 'norm_ffn_g', 'ffn_w_up', 'ffn_conv_w', 'ffn_conv_b', 'ffn_w_down', 'norm_final_g', 'loss_target', 'm_norm_mix_g', 'm_even_w_in', 'm_even_conv_w', 'm_ssm_log_step', 'm_ssm_a_re', 'm_ssm_a_im', 'm_ssm_b_re', 'm_ssm_b_im', 'm_ssm_c_re', 'm_ssm_c_im', 'm_ssm_d', 'm_ssm_glu_w', 'm_ssm_glu_b', 'm_even_w_out', 'm_odd_w_in', 'm_pool_w', 'm_pool_scale', 'm_sgu_norm_g', 'm_sgu_w', 'm_sgu_b', 'm_odd_w_out', 'm_norm_ffn_g', 'm_ffn_w_up', 'm_ffn_conv_w', 'm_ffn_conv_b', 'm_ffn_w_down', 'm_norm_final_g', 'v_norm_mix_g', 'v_even_w_in', 'v_even_conv_w', 'v_ssm_log_step', 'v_ssm_a_re', 'v_ssm_a_im', 'v_ssm_b_re', 'v_ssm_b_im', 'v_ssm_c_re', 'v_ssm_c_im', 'v_ssm_d', 'v_ssm_glu_w', 'v_ssm_glu_b', 'v_even_w_out', 'v_odd_w_in', 'v_pool_w', 'v_pool_scale', 'v_sgu_norm_g', 'v_sgu_w', 'v_sgu_b', 'v_odd_w_out', 'v_norm_ffn_g', 'v_ffn_w_up', 'v_ffn_conv_w', 'v_ffn_conv_b', 'v_ffn_w_down', 'v_norm_final_g']
TWIN_OUTPUTS = ['loss', 'grad_x', 'grad_norm_mix_g', 'grad_even_w_in', 'grad_even_conv_w', 'grad_ssm_log_step', 'grad_ssm_a_re', 'grad_ssm_a_im', 'grad_ssm_b_re', 'grad_ssm_b_im', 'grad_ssm_c_re', 'grad_ssm_c_im', 'grad_ssm_d', 'grad_ssm_glu_w', 'grad_ssm_glu_b', 'grad_even_w_out', 'grad_odd_w_in', 'grad_pool_w', 'grad_pool_scale', 'grad_sgu_norm_g', 'grad_sgu_w', 'grad_sgu_b', 'grad_odd_w_out', 'grad_norm_ffn_g', 'grad_ffn_w_up', 'grad_ffn_conv_w', 'grad_ffn_conv_b', 'grad_ffn_w_down', 'grad_norm_final_g', 'delta_norm_mix_g', 'delta_even_w_in', 'delta_even_conv_w', 'delta_ssm_log_step', 'delta_ssm_a_re', 'delta_ssm_a_im', 'delta_ssm_b_re', 'delta_ssm_b_im', 'delta_ssm_c_re', 'delta_ssm_c_im', 'delta_ssm_d', 'delta_ssm_glu_w', 'delta_ssm_glu_b', 'delta_even_w_out', 'delta_odd_w_in', 'delta_pool_w', 'delta_pool_scale', 'delta_sgu_norm_g', 'delta_sgu_w', 'delta_sgu_b', 'delta_odd_w_out', 'delta_norm_ffn_g', 'delta_ffn_w_up', 'delta_ffn_conv_w', 'delta_ffn_conv_b', 'delta_ffn_w_down', 'delta_norm_final_g', 'new_m_norm_mix_g', 'new_m_even_w_in', 'new_m_even_conv_w', 'new_m_ssm_log_step', 'new_m_ssm_a_re', 'new_m_ssm_a_im', 'new_m_ssm_b_re', 'new_m_ssm_b_im', 'new_m_ssm_c_re', 'new_m_ssm_c_im', 'new_m_ssm_d', 'new_m_ssm_glu_w', 'new_m_ssm_glu_b', 'new_m_even_w_out', 'new_m_odd_w_in', 'new_m_pool_w', 'new_m_pool_scale', 'new_m_sgu_norm_g', 'new_m_sgu_w', 'new_m_sgu_b', 'new_m_odd_w_out', 'new_m_norm_ffn_g', 'new_m_ffn_w_up', 'new_m_ffn_conv_w', 'new_m_ffn_conv_b', 'new_m_ffn_w_down', 'new_m_norm_final_g', 'new_v_norm_mix_g', 'new_v_even_w_in', 'new_v_even_conv_w', 'new_v_ssm_log_step', 'new_v_ssm_a_re', 'new_v_ssm_a_im', 'new_v_ssm_b_re', 'new_v_ssm_b_im', 'new_v_ssm_c_re', 'new_v_ssm_c_im', 'new_v_ssm_d', 'new_v_ssm_glu_w', 'new_v_ssm_glu_b', 'new_v_even_w_out', 'new_v_odd_w_in', 'new_v_pool_w', 'new_v_pool_scale', 'new_v_sgu_norm_g', 'new_v_sgu_w', 'new_v_sgu_b', 'new_v_odd_w_out', 'new_v_norm_ffn_g', 'new_v_ffn_w_up', 'new_v_ffn_conv_w', 'new_v_ffn_conv_b', 'new_v_ffn_w_down', 'new_v_norm_final_g']
TWIN_LEAF_KINDS = {'loss': 'loss', 'grad_x': 'grad_x', 'grad_norm_mix_g': 'grad_w', 'grad_even_w_in': 'grad_w', 'grad_even_conv_w': 'grad_w', 'grad_ssm_log_step': 'grad_w', 'grad_ssm_a_re': 'grad_w', 'grad_ssm_a_im': 'grad_w', 'grad_ssm_b_re': 'grad_w', 'grad_ssm_b_im': 'grad_w', 'grad_ssm_c_re': 'grad_w', 'grad_ssm_c_im': 'grad_w', 'grad_ssm_d': 'grad_w', 'grad_ssm_glu_w': 'grad_w', 'grad_ssm_glu_b': 'grad_w', 'grad_even_w_out': 'grad_w', 'grad_odd_w_in': 'grad_w', 'grad_pool_w': 'grad_w', 'grad_pool_scale': 'grad_w', 'grad_sgu_norm_g': 'grad_w', 'grad_sgu_w': 'grad_w', 'grad_sgu_b': 'grad_w', 'grad_odd_w_out': 'grad_w', 'grad_norm_ffn_g': 'grad_w', 'grad_ffn_w_up': 'grad_w', 'grad_ffn_conv_w': 'grad_w', 'grad_ffn_conv_b': 'grad_w', 'grad_ffn_w_down': 'grad_w', 'grad_norm_final_g': 'grad_w', 'delta_norm_mix_g': 'delta_w', 'delta_even_w_in': 'delta_w', 'delta_even_conv_w': 'delta_w', 'delta_ssm_log_step': 'delta_w', 'delta_ssm_a_re': 'delta_w', 'delta_ssm_a_im': 'delta_w', 'delta_ssm_b_re': 'delta_w', 'delta_ssm_b_im': 'delta_w', 'delta_ssm_c_re': 'delta_w', 'delta_ssm_c_im': 'delta_w', 'delta_ssm_d': 'delta_w', 'delta_ssm_glu_w': 'delta_w', 'delta_ssm_glu_b': 'delta_w', 'delta_even_w_out': 'delta_w', 'delta_odd_w_in': 'delta_w', 'delta_pool_w': 'delta_w', 'delta_pool_scale': 'delta_w', 'delta_sgu_norm_g': 'delta_w', 'delta_sgu_w': 'delta_w', 'delta_sgu_b': 'delta_w', 'delta_odd_w_out': 'delta_w', 'delta_norm_ffn_g': 'delta_w', 'delta_ffn_w_up': 'delta_w', 'delta_ffn_conv_w': 'delta_w', 'delta_ffn_conv_b': 'delta_w', 'delta_ffn_w_down': 'delta_w', 'delta_norm_final_g': 'delta_w', 'new_m_norm_mix_g': 'new_m', 'new_m_even_w_in': 'new_m', 'new_m_even_conv_w': 'new_m', 'new_m_ssm_log_step': 'new_m', 'new_m_ssm_a_re': 'new_m', 'new_m_ssm_a_im': 'new_m', 'new_m_ssm_b_re': 'new_m', 'new_m_ssm_b_im': 'new_m', 'new_m_ssm_c_re': 'new_m', 'new_m_ssm_c_im': 'new_m', 'new_m_ssm_d': 'new_m', 'new_m_ssm_glu_w': 'new_m', 'new_m_ssm_glu_b': 'new_m', 'new_m_even_w_out': 'new_m', 'new_m_odd_w_in': 'new_m', 'new_m_pool_w': 'new_m', 'new_m_pool_scale': 'new_m', 'new_m_sgu_norm_g': 'new_m', 'new_m_sgu_w': 'new_m', 'new_m_sgu_b': 'new_m', 'new_m_odd_w_out': 'new_m', 'new_m_norm_ffn_g': 'new_m', 'new_m_ffn_w_up': 'new_m', 'new_m_ffn_conv_w': 'new_m', 'new_m_ffn_conv_b': 'new_m', 'new_m_ffn_w_down': 'new_m', 'new_m_norm_final_g': 'new_m', 'new_v_norm_mix_g': 'new_v', 'new_v_even_w_in': 'new_v', 'new_v_even_conv_w': 'new_v', 'new_v_ssm_log_step': 'new_v', 'new_v_ssm_a_re': 'new_v', 'new_v_ssm_a_im': 'new_v', 'new_v_ssm_b_re': 'new_v', 'new_v_ssm_b_im': 'new_v', 'new_v_ssm_c_re': 'new_v', 'new_v_ssm_c_im': 'new_v', 'new_v_ssm_d': 'new_v', 'new_v_ssm_glu_w': 'new_v', 'new_v_ssm_glu_b': 'new_v', 'new_v_even_w_out': 'new_v', 'new_v_odd_w_in': 'new_v', 'new_v_pool_w': 'new_v', 'new_v_pool_scale': 'new_v', 'new_v_sgu_norm_g': 'new_v', 'new_v_sgu_w': 'new_v', 'new_v_sgu_b': 'new_v', 'new_v_odd_w_out': 'new_v', 'new_v_norm_ffn_g': 'new_v', 'new_v_ffn_w_up': 'new_v', 'new_v_ffn_conv_w': 'new_v', 'new_v_ffn_conv_b': 'new_v', 'new_v_ffn_w_down': 'new_v', 'new_v_norm_final_g': 'new_v'}


def _forward(args):
    return _fwd_reference(*[args[k] for k in FWD_PARAMS])


def _output_shape():
    out = _jax.eval_shape(lambda: _forward(_fwd_setup_inputs(0)))
    return out.shape, out.dtype

N_MICROBATCH = 1
ADAM_LR = 0.001
ADAM_B1 = 0.9
ADAM_B2 = 0.999
ADAM_EPS = 1e-08
ADAM_WD = 0.01
ADAM_STEP = 10
PER_EXAMPLE_BATCH_AXIS = {'x': 0, 'loss_target': 0}
SHARED_INPUTS = []
_WEIGHT_DTYPES = {'norm_mix_g': _jnp.float32, 'even_w_in': _jnp.float32, 'even_conv_w': _jnp.float32, 'ssm_log_step': _jnp.float32, 'ssm_a_re': _jnp.float32, 'ssm_a_im': _jnp.float32, 'ssm_b_re': _jnp.float32, 'ssm_b_im': _jnp.float32, 'ssm_c_re': _jnp.float32, 'ssm_c_im': _jnp.float32, 'ssm_d': _jnp.float32, 'ssm_glu_w': _jnp.float32, 'ssm_glu_b': _jnp.float32, 'even_w_out': _jnp.float32, 'odd_w_in': _jnp.float32, 'pool_w': _jnp.float32, 'pool_scale': _jnp.float32, 'sgu_norm_g': _jnp.float32, 'sgu_w': _jnp.float32, 'sgu_b': _jnp.float32, 'odd_w_out': _jnp.float32, 'norm_ffn_g': _jnp.float32, 'ffn_w_up': _jnp.float32, 'ffn_conv_w': _jnp.float32, 'ffn_conv_b': _jnp.float32, 'ffn_w_down': _jnp.float32, 'norm_final_g': _jnp.float32}
MOMENT_SCALE = {'norm_mix_g': 1.649402e-01, 'even_w_in': 1.481058e-01, 'even_conv_w': 1.724439e-01, 'ssm_log_step': 3.676624e+00, 'ssm_a_re': 3.389794e-03, 'ssm_a_im': 3.528689e-03, 'ssm_b_re': 2.128420e-03, 'ssm_b_im': 2.097336e-03, 'ssm_c_re': 4.302892e-03, 'ssm_c_im': 4.299436e-03, 'ssm_d': 7.480461e-02, 'ssm_glu_w': 1.865660e-02, 'ssm_glu_b': 2.982143e-02, 'even_w_out': 1.270576e-01, 'odd_w_in': 9.162134e-02, 'pool_w': 1.094666e-01, 'pool_scale': 1.194193e-01, 'sgu_norm_g': 5.503268e-02, 'sgu_w': 5.633510e-02, 'sgu_b': 8.035076e-02, 'odd_w_out': 1.051857e-01, 'norm_ffn_g': 1.134057e-01, 'ffn_w_up': 4.771096e-02, 'ffn_conv_w': 4.745788e-02, 'ffn_conv_b': 4.758466e-02, 'ffn_w_down': 7.804445e-02, 'norm_final_g': 3.199318e+01}


def _to_microbatches(a, axis):
    t = _jnp.moveaxis(a, axis, 0)
    t = t.reshape((N_MICROBATCH, t.shape[0] // N_MICROBATCH) + t.shape[1:])
    return _jnp.moveaxis(t, 1, axis + 1)


def setup_inputs(seed: int = 0) -> dict:
    inp = _fwd_setup_inputs(seed)
    key = _jax.random.fold_in(_jax.random.key(seed), 7919)
    shape, _ = _output_shape()
    out = dict(inp)
    out["loss_target"] = _jax.random.normal(_jax.random.fold_in(key, 0), shape, _jnp.float32)
    for i, name in enumerate(TWIN_WEIGHTS):
        w = inp[name].astype(_jnp.float32)
        if MOMENT_SCALE is None:
            s = _jnp.sqrt(_jnp.mean(_jnp.square(w)) + 1e-30)
        else:
            s = MOMENT_SCALE[name]
        km, kv = _jax.random.split(_jax.random.fold_in(key, i + 1))
        out[name] = w
        out["m_" + name] = s * _jax.random.normal(km, w.shape, _jnp.float32)
        out["v_" + name] = (s * s) * _jax.random.uniform(kv, w.shape, _jnp.float32, 0.5, 1.5)
    if N_MICROBATCH > 1:
        for name, axis in PER_EXAMPLE_BATCH_AXIS.items():
            out[name] = _to_microbatches(out[name], axis)
    return {'x': out['x'], 'norm_mix_g': out['norm_mix_g'], 'even_w_in': out['even_w_in'], 'even_conv_w': out['even_conv_w'], 'ssm_log_step': out['ssm_log_step'], 'ssm_a_re': out['ssm_a_re'], 'ssm_a_im': out['ssm_a_im'], 'ssm_b_re': out['ssm_b_re'], 'ssm_b_im': out['ssm_b_im'], 'ssm_c_re': out['ssm_c_re'], 'ssm_c_im': out['ssm_c_im'], 'ssm_d': out['ssm_d'], 'ssm_glu_w': out['ssm_glu_w'], 'ssm_glu_b': out['ssm_glu_b'], 'even_w_out': out['even_w_out'], 'odd_w_in': out['odd_w_in'], 'pool_w': out['pool_w'], 'pool_scale': out['pool_scale'], 'sgu_norm_g': out['sgu_norm_g'], 'sgu_w': out['sgu_w'], 'sgu_b': out['sgu_b'], 'odd_w_out': out['odd_w_out'], 'norm_ffn_g': out['norm_ffn_g'], 'ffn_w_up': out['ffn_w_up'], 'ffn_conv_w': out['ffn_conv_w'], 'ffn_conv_b': out['ffn_conv_b'], 'ffn_w_down': out['ffn_w_down'], 'norm_final_g': out['norm_final_g'], 'loss_target': out['loss_target'], 'm_norm_mix_g': out['m_norm_mix_g'], 'm_even_w_in': out['m_even_w_in'], 'm_even_conv_w': out['m_even_conv_w'], 'm_ssm_log_step': out['m_ssm_log_step'], 'm_ssm_a_re': out['m_ssm_a_re'], 'm_ssm_a_im': out['m_ssm_a_im'], 'm_ssm_b_re': out['m_ssm_b_re'], 'm_ssm_b_im': out['m_ssm_b_im'], 'm_ssm_c_re': out['m_ssm_c_re'], 'm_ssm_c_im': out['m_ssm_c_im'], 'm_ssm_d': out['m_ssm_d'], 'm_ssm_glu_w': out['m_ssm_glu_w'], 'm_ssm_glu_b': out['m_ssm_glu_b'], 'm_even_w_out': out['m_even_w_out'], 'm_odd_w_in': out['m_odd_w_in'], 'm_pool_w': out['m_pool_w'], 'm_pool_scale': out['m_pool_scale'], 'm_sgu_norm_g': out['m_sgu_norm_g'], 'm_sgu_w': out['m_sgu_w'], 'm_sgu_b': out['m_sgu_b'], 'm_odd_w_out': out['m_odd_w_out'], 'm_norm_ffn_g': out['m_norm_ffn_g'], 'm_ffn_w_up': out['m_ffn_w_up'], 'm_ffn_conv_w': out['m_ffn_conv_w'], 'm_ffn_conv_b': out['m_ffn_conv_b'], 'm_ffn_w_down': out['m_ffn_w_down'], 'm_norm_final_g': out['m_norm_final_g'], 'v_norm_mix_g': out['v_norm_mix_g'], 'v_even_w_in': out['v_even_w_in'], 'v_even_conv_w': out['v_even_conv_w'], 'v_ssm_log_step': out['v_ssm_log_step'], 'v_ssm_a_re': out['v_ssm_a_re'], 'v_ssm_a_im': out['v_ssm_a_im'], 'v_ssm_b_re': out['v_ssm_b_re'], 'v_ssm_b_im': out['v_ssm_b_im'], 'v_ssm_c_re': out['v_ssm_c_re'], 'v_ssm_c_im': out['v_ssm_c_im'], 'v_ssm_d': out['v_ssm_d'], 'v_ssm_glu_w': out['v_ssm_glu_w'], 'v_ssm_glu_b': out['v_ssm_glu_b'], 'v_even_w_out': out['v_even_w_out'], 'v_odd_w_in': out['v_odd_w_in'], 'v_pool_w': out['v_pool_w'], 'v_pool_scale': out['v_pool_scale'], 'v_sgu_norm_g': out['v_sgu_norm_g'], 'v_sgu_w': out['v_sgu_w'], 'v_sgu_b': out['v_sgu_b'], 'v_odd_w_out': out['v_odd_w_out'], 'v_norm_ffn_g': out['v_norm_ffn_g'], 'v_ffn_w_up': out['v_ffn_w_up'], 'v_ffn_conv_w': out['v_ffn_conv_w'], 'v_ffn_conv_b': out['v_ffn_conv_b'], 'v_ffn_w_down': out['v_ffn_w_down'], 'v_norm_final_g': out['v_norm_final_g']}


def _loss(weights, diff, rest, loss_target):
    with _jax.named_scope("forward"):
        args = {**rest, TWIN_DIFF_INPUT: diff, **{k: w.astype(_WEIGHT_DTYPES[k]) for k, w in weights.items()}}
        y = _forward(args)
    with _jax.named_scope("loss_head"):
        err = _jnp.square(y.astype(_jnp.float32) - loss_target)
        return 0.5 * _jnp.sum(_jnp.mean(err, axis=-1)) if err.ndim else 0.5 * err


def _adamw(w, g, m, v):
    m = ADAM_B1 * m + (1.0 - ADAM_B1) * g
    v = ADAM_B2 * v + (1.0 - ADAM_B2) * _jnp.square(g)
    m_hat = m / (1.0 - ADAM_B1 ** ADAM_STEP)
    v_hat = v / (1.0 - ADAM_B2 ** ADAM_STEP)
    delta = -ADAM_LR * (m_hat / (_jnp.sqrt(v_hat) + ADAM_EPS) + ADAM_WD * w)
    return delta, m, v


def reference(x, norm_mix_g, even_w_in, even_conv_w, ssm_log_step, ssm_a_re, ssm_a_im, ssm_b_re, ssm_b_im, ssm_c_re, ssm_c_im, ssm_d, ssm_glu_w, ssm_glu_b, even_w_out, odd_w_in, pool_w, pool_scale, sgu_norm_g, sgu_w, sgu_b, odd_w_out, norm_ffn_g, ffn_w_up, ffn_conv_w, ffn_conv_b, ffn_w_down, norm_final_g, loss_target, m_norm_mix_g, m_even_w_in, m_even_conv_w, m_ssm_log_step, m_ssm_a_re, m_ssm_a_im, m_ssm_b_re, m_ssm_b_im, m_ssm_c_re, m_ssm_c_im, m_ssm_d, m_ssm_glu_w, m_ssm_glu_b, m_even_w_out, m_odd_w_in, m_pool_w, m_pool_scale, m_sgu_norm_g, m_sgu_w, m_sgu_b, m_odd_w_out, m_norm_ffn_g, m_ffn_w_up, m_ffn_conv_w, m_ffn_conv_b, m_ffn_w_down, m_norm_final_g, v_norm_mix_g, v_even_w_in, v_even_conv_w, v_ssm_log_step, v_ssm_a_re, v_ssm_a_im, v_ssm_b_re, v_ssm_b_im, v_ssm_c_re, v_ssm_c_im, v_ssm_d, v_ssm_glu_w, v_ssm_glu_b, v_even_w_out, v_odd_w_in, v_pool_w, v_pool_scale, v_sgu_norm_g, v_sgu_w, v_sgu_b, v_odd_w_out, v_norm_ffn_g, v_ffn_w_up, v_ffn_conv_w, v_ffn_conv_b, v_ffn_w_down, v_norm_final_g):
    given = dict(x=x, norm_mix_g=norm_mix_g, even_w_in=even_w_in, even_conv_w=even_conv_w, ssm_log_step=ssm_log_step, ssm_a_re=ssm_a_re, ssm_a_im=ssm_a_im, ssm_b_re=ssm_b_re, ssm_b_im=ssm_b_im, ssm_c_re=ssm_c_re, ssm_c_im=ssm_c_im, ssm_d=ssm_d, ssm_glu_w=ssm_glu_w, ssm_glu_b=ssm_glu_b, even_w_out=even_w_out, odd_w_in=odd_w_in, pool_w=pool_w, pool_scale=pool_scale, sgu_norm_g=sgu_norm_g, sgu_w=sgu_w, sgu_b=sgu_b, odd_w_out=odd_w_out, norm_ffn_g=norm_ffn_g, ffn_w_up=ffn_w_up, ffn_conv_w=ffn_conv_w, ffn_conv_b=ffn_conv_b, ffn_w_down=ffn_w_down, norm_final_g=norm_final_g, loss_target=loss_target, m_norm_mix_g=m_norm_mix_g, m_even_w_in=m_even_w_in, m_even_conv_w=m_even_conv_w, m_ssm_log_step=m_ssm_log_step, m_ssm_a_re=m_ssm_a_re, m_ssm_a_im=m_ssm_a_im, m_ssm_b_re=m_ssm_b_re, m_ssm_b_im=m_ssm_b_im, m_ssm_c_re=m_ssm_c_re, m_ssm_c_im=m_ssm_c_im, m_ssm_d=m_ssm_d, m_ssm_glu_w=m_ssm_glu_w, m_ssm_glu_b=m_ssm_glu_b, m_even_w_out=m_even_w_out, m_odd_w_in=m_odd_w_in, m_pool_w=m_pool_w, m_pool_scale=m_pool_scale, m_sgu_norm_g=m_sgu_norm_g, m_sgu_w=m_sgu_w, m_sgu_b=m_sgu_b, m_odd_w_out=m_odd_w_out, m_norm_ffn_g=m_norm_ffn_g, m_ffn_w_up=m_ffn_w_up, m_ffn_conv_w=m_ffn_conv_w, m_ffn_conv_b=m_ffn_conv_b, m_ffn_w_down=m_ffn_w_down, m_norm_final_g=m_norm_final_g, v_norm_mix_g=v_norm_mix_g, v_even_w_in=v_even_w_in, v_even_conv_w=v_even_conv_w, v_ssm_log_step=v_ssm_log_step, v_ssm_a_re=v_ssm_a_re, v_ssm_a_im=v_ssm_a_im, v_ssm_b_re=v_ssm_b_re, v_ssm_b_im=v_ssm_b_im, v_ssm_c_re=v_ssm_c_re, v_ssm_c_im=v_ssm_c_im, v_ssm_d=v_ssm_d, v_ssm_glu_w=v_ssm_glu_w, v_ssm_glu_b=v_ssm_glu_b, v_even_w_out=v_even_w_out, v_odd_w_in=v_odd_w_in, v_pool_w=v_pool_w, v_pool_scale=v_pool_scale, v_sgu_norm_g=v_sgu_norm_g, v_sgu_w=v_sgu_w, v_sgu_b=v_sgu_b, v_odd_w_out=v_odd_w_out, v_norm_ffn_g=v_norm_ffn_g, v_ffn_w_up=v_ffn_w_up, v_ffn_conv_w=v_ffn_conv_w, v_ffn_conv_b=v_ffn_conv_b, v_ffn_w_down=v_ffn_w_down, v_norm_final_g=v_norm_final_g)
    weights = {n: given[n] for n in TWIN_WEIGHTS}
    shared = {n: given[n] for n in SHARED_INPUTS}
    per_example = {n: given[n] for n in ['x']}
    grad_fn = _jax.value_and_grad(_loss, argnums=(0, 1))

    def one_microbatch(ex, loss_target):
        ex = dict(ex)
        diff = ex.pop(TWIN_DIFF_INPUT)
        return grad_fn(weights, diff, {**shared, **ex}, loss_target)

    if N_MICROBATCH == 1:
        loss, (grad_w, grad_x) = one_microbatch(per_example, given["loss_target"])
    else:
        def body(carry, xs):
            loss_sum, grad_sum = carry
            l_k, (gw_k, gx_k) = one_microbatch(xs[0], xs[1])
            with _jax.named_scope("update"):
                return (loss_sum + l_k, _jax.tree.map(_jnp.add, grad_sum, gw_k)), gx_k

        init = (_jnp.zeros((), _jnp.float32), _jax.tree.map(_jnp.zeros_like, weights))
        (loss, grad_w), grad_x = _jax.lax.scan(body, init, (per_example, given["loss_target"]))
    with _jax.named_scope("update"):
        delta_w, new_m, new_v = {}, {}, {}
        for n in TWIN_WEIGHTS:
            delta_w[n], new_m[n], new_v[n] = _adamw(weights[n], grad_w[n], given["m_" + n], given["v_" + n])
    return (loss, grad_x, *[grad_w[n] for n in TWIN_WEIGHTS], *[delta_w[n] for n in TWIN_WEIGHTS],
            *[new_m[n] for n in TWIN_WEIGHTS], *[new_v[n] for n in TWIN_WEIGHTS])
```

```python
import functools
import math

import jax
import jax.numpy as jnp
from jax import lax
from jax.experimental import pallas as pl
from jax.experimental.pallas import tpu as pltpu

F32 = jnp.float32
BF16 = jnp.bfloat16

D_MODEL = 1024
DEPTH = 4
D_HALF = D_MODEL // 2
SSM_GROUP = 16
N_SSM_GROUPS = D_HALF // SSM_GROUP
SSM_STATE = 64
POOL_WINDOWS = (2, 4, 8, 16)
SGU_HEADS = 4
CHUNK = 128
D_FF = 2816
CONV_WIDTH = 3
EPS = 1e-6
N_DEV = 8

ADAM_LR = 0.001
ADAM_B1 = 0.9
ADAM_B2 = 0.999
ADAM_EPS = 1e-08
ADAM_WD = 0.01
ADAM_STEP = 10

LANE = 128
SUBLANE = 8
S5_LANE_BLOCKS = D_HALF // LANE
S5_STATE_LANES = (N_SSM_GROUPS // S5_LANE_BLOCKS) * SSM_STATE
S5_TIME_CHUNK = 512
EXCHANGE_BLOCK_ELEMS = 1 << 20

GELU_K = math.sqrt(2.0 / math.pi)
GELU_C = 0.044715

_ARB = pltpu.CompilerParams(dimension_semantics=("arbitrary",))
_ARB2 = pltpu.CompilerParams(dimension_semantics=("arbitrary", "arbitrary"))
_PAR = pltpu.CompilerParams(dimension_semantics=("parallel",))
_PAR2 = pltpu.CompilerParams(dimension_semantics=("parallel", "parallel"))


def _pick_tile(n, cap, mult):
    if n <= cap:
        return n
    best = None
    for t in range(mult, cap + 1, mult):
        if n % t == 0:
            best = t
    assert best is not None, (n, cap, mult)
    return best


def _shift_dn(x, d):
    rolled = pltpu.roll(x, d, 0)
    row = lax.broadcasted_iota(jnp.int32, x.shape, 0)
    return jnp.where(row >= d, rolled, 0.0)


def _shift_up(x, d):
    n = x.shape[0]
    rolled = pltpu.roll(x, n - d, 0)
    row = lax.broadcasted_iota(jnp.int32, x.shape, 0)
    return jnp.where(row < n - d, rolled, 0.0)


def _gelu(x):
    return 0.5 * x * (1.0 + jnp.tanh(GELU_K * (x + GELU_C * x * x * x)))


def _gelu_grad(x):
    t = jnp.tanh(GELU_K * (x + GELU_C * x * x * x))
    return 0.5 * (1.0 + t) + 0.5 * x * (1.0 - t * t) * (GELU_K * (1.0 + 3.0 * GELU_C * x * x))


def _sigmoid(x):
    return 1.0 / (1.0 + jnp.exp(-x))


def _conv3(x, w_ref):
    return w_ref[0:1, :] * _shift_dn(x, 2) + w_ref[1:2, :] * _shift_dn(x, 1) + w_ref[2:3, :] * x


def _conv3_bwd_x(dy, w_ref):
    return w_ref[2:3, :] * dy + w_ref[1:2, :] * _shift_up(dy, 1) + w_ref[0:1, :] * _shift_up(dy, 2)


def _conv3_bwd_w(dy, x):
    return jnp.concatenate([
        jnp.sum(dy * _shift_dn(x, 2), axis=0, keepdims=True),
        jnp.sum(dy * _shift_dn(x, 1), axis=0, keepdims=True),
        jnp.sum(dy * x, axis=0, keepdims=True)], axis=0)


def _dot(a, b, dims):
    return lax.dot_general(a.astype(BF16), b.astype(BF16), (dims, ((), ())), preferred_element_type=F32)


_NN = ((1,), (0,))
_NT = ((1,), (1,))
_TN = ((0,), (0,))


def _mm(a, b, mode, *, name, out_dtype=F32, add=None, tm_cap=512, tn_cap=1536):
    if mode == "tn":
        r, m = a.shape
        n = b.shape[1]
        tm, tn = _pick_tile(m, 512, LANE), _pick_tile(n, 512, LANE)
        in_specs = [pl.BlockSpec((r, tm), lambda i, j: (0, i)), pl.BlockSpec((r, tn), lambda i, j: (0, j))]
        dims = _TN
    elif mode == "nn":
        m, k = a.shape
        n = b.shape[1]
        tm, tn = _pick_tile(m, tm_cap, SUBLANE), _pick_tile(n, tn_cap, LANE)
        in_specs = [pl.BlockSpec((tm, k), lambda i, j: (i, 0)), pl.BlockSpec((k, tn), lambda i, j: (0, j))]
        dims = _NN
    else:
        m, k = a.shape
        n = b.shape[0]
        tm, tn = _pick_tile(m, tm_cap, SUBLANE), _pick_tile(n, tn_cap, LANE)
        in_specs = [pl.BlockSpec((tm, k), lambda i, j: (i, 0)), pl.BlockSpec((tn, k), lambda i, j: (j, 0))]
        dims = _NT
    args = [a, b]
    if add is not None:
        in_specs.append(pl.BlockSpec((tm, tn), lambda i, j: (i, j)))
        args.append(add)

    def body(*refs):
        acc = _dot(refs[0][...], refs[1][...], dims)
        if add is not None:
            acc = acc + refs[2][...]
        refs[-1][...] = acc.astype(out_dtype)

    return pl.pallas_call(
        body, name=name, grid=(m // tm, n // tn), in_specs=in_specs,
        out_specs=pl.BlockSpec((tm, tn), lambda i, j: (i, j)),
        out_shape=jax.ShapeDtypeStruct((m, n), out_dtype), compiler_params=_PAR2)(*args)


def _rms_fwd(x, g, *, name):
    l, d = x.shape
    tl = _pick_tile(l, 512, SUBLANE)

    def body(x_ref, g_ref, h_ref):
        xv = x_ref[...]
        r = lax.rsqrt(jnp.mean(xv * xv, axis=-1, keepdims=True) + EPS)
        h_ref[...] = (xv * r * g_ref[...]).astype(BF16)

    return pl.pallas_call(
        body, name=name, grid=(l // tl,),
        in_specs=[pl.BlockSpec((tl, d), lambda i: (i, 0)), pl.BlockSpec((1, d), lambda i: (0, 0))],
        out_specs=pl.BlockSpec((tl, d), lambda i: (i, 0)),
        out_shape=jax.ShapeDtypeStruct((l, d), BF16), compiler_params=_PAR)(x, g)


def _rms_bwd_rows(dh, xv, g):
    r = lax.rsqrt(jnp.mean(xv * xv, axis=-1, keepdims=True) + EPS)
    a = dh * g
    m = jnp.mean(a * xv, axis=-1, keepdims=True)
    return r * a - xv * (r * r * r) * m, xv * r


def _rms_bwd(dh, x, g, res, *, name):
    l, d = x.shape
    tl = _pick_tile(l, 512, SUBLANE)

    def body(dh_ref, x_ref, g_ref, res_ref, dx_ref, dg_ref):
        @pl.when(pl.program_id(0) == 0)
        def _():
            dg_ref[...] = jnp.zeros_like(dg_ref)
        dhv = dh_ref[...]
        dx, xn = _rms_bwd_rows(dhv, x_ref[...], g_ref[...])
        dx_ref[...] = dx + res_ref[...]
        dg_ref[...] += jnp.sum(dhv * xn, axis=0, keepdims=True)

    row = pl.BlockSpec((tl, d), lambda i: (i, 0))
    vec = pl.BlockSpec((1, d), lambda i: (0, 0))
    return pl.pallas_call(
        body, name=name, grid=(l // tl,), in_specs=[row, row, vec, row], out_specs=[row, vec],
        out_shape=[jax.ShapeDtypeStruct((l, d), F32), jax.ShapeDtypeStruct((1, d), F32)],
        compiler_params=_ARB)(dh, x, g, res)


def _loss_head(x, g, tgt, *, name):
    l, d = x.shape
    tl = _pick_tile(l, 512, SUBLANE)

    def body(x_ref, g_ref, t_ref, loss_ref, dx_ref, dg_ref):
        @pl.when(pl.program_id(0) == 0)
        def _():
            dg_ref[...] = jnp.zeros_like(dg_ref)
            loss_ref[...] = jnp.zeros_like(loss_ref)
        xv = x_ref[...]
        gv = g_ref[...]
        r = lax.rsqrt(jnp.mean(xv * xv, axis=-1, keepdims=True) + EPS)
        err = xv * r * gv - t_ref[...]
        row_loss = jnp.sum(err * err, axis=-1, keepdims=True) * (0.5 / d)
        loss_ref[...] += jnp.sum(row_loss, axis=0, keepdims=True)
        dy = err * (1.0 / d)
        dx, xn = _rms_bwd_rows(dy, xv, gv)
        dx_ref[...] = dx
        dg_ref[...] += jnp.sum(dy * xn, axis=0, keepdims=True)

    row = pl.BlockSpec((tl, d), lambda i: (i, 0))
    vec = pl.BlockSpec((1, d), lambda i: (0, 0))
    one = pl.BlockSpec((1, 1), lambda i: (0, 0))
    return pl.pallas_call(
        body, name=name, grid=(l // tl,), in_specs=[row, vec, row], out_specs=[one, row, vec],
        out_shape=[jax.ShapeDtypeStruct((1, 1), F32), jax.ShapeDtypeStruct((l, d), F32),
                   jax.ShapeDtypeStruct((1, d), F32)],
        compiler_params=_ARB)(x, g, tgt)


def _ffn_act(up, cw, cb, *, name):
    l = up.shape[0]
    nb = D_FF // LANE

    def body(ug_ref, uv_ref, wg_ref, wv_ref, bg_ref, bv_ref, o_ref):
        gc = _conv3(ug_ref[...], wg_ref) + bg_ref[...]
        vc = _conv3(uv_ref[...], wv_ref) + bv_ref[...]
        o_ref[...] = (gc * _sigmoid(gc) * vc).astype(BF16)

    col = lambda off: pl.BlockSpec((l, LANE), lambda j: (0, j + off))
    w3 = lambda off: pl.BlockSpec((CONV_WIDTH, LANE), lambda j: (0, j + off))
    b1 = lambda off: pl.BlockSpec((1, LANE), lambda j: (0, j + off))
    return pl.pallas_call(
        body, name=name, grid=(nb,), in_specs=[col(0), col(nb), w3(0), w3(nb), b1(0), b1(nb)],
        out_specs=col(0), out_shape=jax.ShapeDtypeStruct((l, D_FF), BF16),
        compiler_params=_PAR)(up, up, cw, cw, cb, cb)


def _ffn_act_bwd(up, dact, cw, cb, *, name):
    l = up.shape[0]
    nb = D_FF // LANE

    def body(ug_ref, uv_ref, own_ref, da_ref, wg_ref, wv_ref, wo_ref, bg_ref, bv_ref,
             dup_ref, dcw_ref, dcb_ref, dc_scr):
        is_gate = pl.program_id(0) < nb
        gc = _conv3(ug_ref[...], wg_ref) + bg_ref[...]
        sg = _sigmoid(gc)

        @pl.when(is_gate)
        def _():
            vc = _conv3(uv_ref[...], wv_ref) + bv_ref[...]
            dc_scr[...] = da_ref[...] * vc * (sg * (1.0 + gc * (1.0 - sg)))

        @pl.when(jnp.logical_not(is_gate))
        def _():
            dc_scr[...] = da_ref[...] * (gc * sg)

        dc = dc_scr[...]
        dcb_ref[...] = jnp.sum(dc, axis=0, keepdims=True)
        dcw_ref[...] = _conv3_bwd_w(dc, own_ref[...])
        dup_ref[...] = _conv3_bwd_x(dc, wo_ref).astype(BF16)

    half = lambda j: j % nb
    colg = pl.BlockSpec((l, LANE), lambda j: (0, half(j)))
    colv = pl.BlockSpec((l, LANE), lambda j: (0, half(j) + nb))
    colo = pl.BlockSpec((l, LANE), lambda j: (0, j))
    w3g = pl.BlockSpec((CONV_WIDTH, LANE), lambda j: (0, half(j)))
    w3v = pl.BlockSpec((CONV_WIDTH, LANE), lambda j: (0, half(j) + nb))
    w3o = pl.BlockSpec((CONV_WIDTH, LANE), lambda j: (0, j))
    b1g = pl.BlockSpec((1, LANE), lambda j: (0, half(j)))
    b1v = pl.BlockSpec((1, LANE), lambda j: (0, half(j) + nb))
    b1o = pl.BlockSpec((1, LANE), lambda j: (0, j))
    return pl.pallas_call(
        body, name=name, grid=(2 * nb,),
        in_specs=[colg, colv, colo, colg, w3g, w3v, w3o, b1g, b1v],
        out_specs=[colo, w3o, b1o],
        out_shape=[jax.ShapeDtypeStruct((l, 2 * D_FF), BF16), jax.ShapeDtypeStruct((CONV_WIDTH, 2 * D_FF), F32),
                   jax.ShapeDtypeStruct((1, 2 * D_FF), F32)],
        scratch_shapes=[pltpu.VMEM((l, LANE), F32)],
        compiler_params=_PAR)(up, up, up, dact, cw, cw, cw, cb, cb)


def _sconv_fwd(proj, cw, *, name):
    l = proj.shape[0]
    nb = D_HALF // LANE

    def body(xa_ref, ba_ref, ca_ref, w_ref, o_ref):
        o_ref[...] = (ba_ref[...] * _conv3(ca_ref[...] * xa_ref[...], w_ref)).astype(BF16)

    col = lambda off: pl.BlockSpec((l, LANE), lambda j: (0, j + off))
    return pl.pallas_call(
        body, name=name, grid=(nb,),
        in_specs=[col(0), col(nb), col(2 * nb), pl.BlockSpec((CONV_WIDTH, LANE), lambda j: (0, j))],
        out_specs=col(0), out_shape=jax.ShapeDtypeStruct((l, D_HALF), BF16),
        compiler_params=_PAR)(proj, proj, proj, cw)


def _sconv_bwd(proj, dcat, cw, *, name):
    l = proj.shape[0]
    nb = D_HALF // LANE

    def body(xa_ref, ba_ref, ca_ref, dy_ref, w_ref, dxa_ref, dba_ref, dca_ref, dw_ref):
        xa, ba, ca, dy = xa_ref[...], ba_ref[...], ca_ref[...], dy_ref[...]
        q = ca * xa
        dba_ref[...] = (dy * _conv3(q, w_ref)).astype(BF16)
        dconv = dy * ba
        dw_ref[...] = _conv3_bwd_w(dconv, q)
        dq = _conv3_bwd_x(dconv, w_ref)
        dxa_ref[...] = (dq * ca).astype(BF16)
        dca_ref[...] = (dq * xa).astype(BF16)

    col = lambda off: pl.BlockSpec((l, LANE), lambda j: (0, j + off))
    w3 = pl.BlockSpec((CONV_WIDTH, LANE), lambda j: (0, j))
    piece = jax.ShapeDtypeStruct((l, D_HALF), BF16)
    return pl.pallas_call(
        body, name=name, grid=(nb,),
        in_specs=[col(0), col(nb), col(2 * nb), col(0), w3],
        out_specs=[col(0), col(0), col(0), w3],
        out_shape=[piece, piece, piece, jax.ShapeDtypeStruct((CONV_WIDTH, D_HALF), F32)],
        compiler_params=_PAR)(proj, proj, proj, dcat, cw)


def _s5_prep(log_step, a_re, a_im, b_re, b_im):
    step = jnp.exp(log_step)[:, None]
    mag = jnp.exp(a_re * step)
    lr = mag * jnp.cos(a_im * step)
    li = mag * jnp.sin(a_im * step)
    nr = lr - 1.0
    den = a_re * a_re + a_im * a_im
    qr = (nr * a_re + li * a_im) / den
    qi = (li * a_re - nr * a_im) / den
    br = qr[..., None] * b_re - qi[..., None] * b_im
    bi = qr[..., None] * b_im + qi[..., None] * b_re
    return lr, li, br, bi


def _block_diag(m):
    nb, ng, r, c = m.shape
    eye = jnp.eye(ng, dtype=m.dtype)
    return jnp.einsum("bgrc,gh->bgrhc", m, eye).reshape(nb, ng * r, ng * c)


def _block_diag_extract(w, r, c):
    nb = w.shape[0]
    ng = w.shape[1] // r
    w5 = w.reshape(nb, ng, r, ng, c)
    return jnp.einsum("bgrhc,gh->bgrc", w5, jnp.eye(ng, dtype=w.dtype))


def _s5_mats(br, bi, c_re, c_im):
    g8 = N_SSM_GROUPS // S5_LANE_BLOCKS
    to_blk = lambda m: m.reshape(S5_LANE_BLOCKS, g8, m.shape[1], m.shape[2])
    wb = jnp.concatenate([_block_diag(to_blk(jnp.swapaxes(br, 1, 2))),
                          _block_diag(to_blk(jnp.swapaxes(bi, 1, 2)))], axis=2)
    wc = jnp.concatenate([_block_diag(to_blk(jnp.swapaxes(c_re, 1, 2))),
                          _block_diag(to_blk(jnp.swapaxes(-c_im, 1, 2)))], axis=1)
    return wb, wc


def _s5_mats_bwd(dwb, dwc):
    g, p, h = N_SSM_GROUPS, SSM_STATE, SSM_GROUP
    half = S5_STATE_LANES
    dbr = jnp.swapaxes(_block_diag_extract(dwb[:, :, :half], h, p).reshape(g, h, p), 1, 2)
    dbi = jnp.swapaxes(_block_diag_extract(dwb[:, :, half:], h, p).reshape(g, h, p), 1, 2)
    dcr = jnp.swapaxes(_block_diag_extract(dwc[:, :half, :], p, h).reshape(g, p, h), 1, 2)
    dci = -jnp.swapaxes(_block_diag_extract(dwc[:, half:, :], p, h).reshape(g, p, h), 1, 2)
    return dbr, dbi, dcr, dci


def _s5_scan_consts(log_step, a_re, a_im, reverse):
    step = jnp.exp(log_step)[:, None]
    xr = (a_re * step).reshape(S5_LANE_BLOCKS, 1, S5_STATE_LANES)
    xi = (a_im * step).reshape(S5_LANE_BLOCKS, 1, S5_STATE_LANES)
    if reverse:
        xi = -xi
    row = jnp.arange(SUBLANE, dtype=F32).reshape(1, SUBLANE, 1)

    def power(n):
        mag = jnp.exp(n * xr)
        return jnp.concatenate([mag * jnp.cos(n * xi), mag * jnp.sin(n * xi)], axis=-1)

    kinds = []
    for d in (1, 2, 4):
        keep = (row <= SUBLANE - 1 - d) if reverse else (row >= d)
        kinds.append(jnp.where(keep, power(jnp.full_like(row, float(d))), 0.0))
    kinds.append(power((SUBLANE - row) if reverse else (row + 1.0)))
    return jnp.stack(kinds, axis=1)


def _scan_rows(s_ref, sc_ref, carry_ref, n_rows, reverse):
    n_grp = n_rows // SUBLANE
    n_col = S5_STATE_LANES // LANE
    half = S5_STATE_LANES

    def step(i, carry):
        grp = (n_grp - 1 - i) if reverse else i
        r0 = pl.multiple_of(grp * SUBLANE, SUBLANE)
        out = []
        for cb in range(n_col):
            lo, hi = cb * LANE, half + cb * LANE
            re = s_ref[pl.ds(r0, SUBLANE), lo:lo + LANE]
            im = s_ref[pl.ds(r0, SUBLANE), hi:hi + LANE]
            for k, d in enumerate((1, 2, 4)):
                sh = (SUBLANE - d) if reverse else d
                rr, ri = pltpu.roll(re, sh, 0), pltpu.roll(im, sh, 0)
                ar, ai = sc_ref[k, :, lo:lo + LANE], sc_ref[k, :, hi:hi + LANE]
                re, im = re + (ar * rr - ai * ri), im + (ar * ri + ai * rr)
            pr, pi = sc_ref[3, :, lo:lo + LANE], sc_ref[3, :, hi:hi + LANE]
            cr, ci = carry[2 * cb], carry[2 * cb + 1]
            re, im = re + (pr * cr - pi * ci), im + (pr * ci + pi * cr)
            s_ref[pl.ds(r0, SUBLANE), lo:lo + LANE] = re
            s_ref[pl.ds(r0, SUBLANE), hi:hi + LANE] = im
            edge = 0 if reverse else SUBLANE - 1
            out.append(jnp.broadcast_to(re[edge:edge + 1, :], (SUBLANE, LANE)))
            out.append(jnp.broadcast_to(im[edge:edge + 1, :], (SUBLANE, LANE)))
        return tuple(out)

    init = []
    for cb in range(n_col):
        init.append(carry_ref[:, cb * LANE:(cb + 1) * LANE])
        init.append(carry_ref[:, half + cb * LANE:half + (cb + 1) * LANE])
    fin = lax.fori_loop(0, n_grp, step, tuple(init))
    for cb in range(n_col):
        carry_ref[:, cb * LANE:(cb + 1) * LANE] = fin[2 * cb]
        carry_ref[:, half + cb * LANE:half + (cb + 1) * LANE] = fin[2 * cb + 1]


def _s5_fwd(proj, wb, wc, d_skip, sc, *, name):
    l = proj.shape[0]
    tt = _pick_tile(l, S5_TIME_CHUNK, SUBLANE)
    u_off = (proj.shape[1] - D_HALF) // LANE
    w2 = 2 * S5_STATE_LANES

    def body(u_ref, wb_ref, wc_ref, d_ref, sc_ref, s_ref, y_ref, carry_ref):
        @pl.when(pl.program_id(1) == 0)
        def _():
            carry_ref[...] = jnp.zeros_like(carry_ref)
        u = u_ref[...]
        s_ref[...] = _dot(u, wb_ref[0], _NN)
        _scan_rows(s_ref, sc_ref.at[0], carry_ref, tt, False)
        y_ref[...] = _dot(s_ref[...], wc_ref[0], _NN) + d_ref[...] * u

    return pl.pallas_call(
        body, name=name, grid=(S5_LANE_BLOCKS, l // tt),
        in_specs=[pl.BlockSpec((tt, LANE), lambda b, t: (t, b + u_off)),
                  pl.BlockSpec((1, LANE, w2), lambda b, t: (b, 0, 0)),
                  pl.BlockSpec((1, w2, LANE), lambda b, t: (b, 0, 0)),
                  pl.BlockSpec((1, LANE), lambda b, t: (0, b)),
                  pl.BlockSpec((1, 4, SUBLANE, w2), lambda b, t: (b, 0, 0, 0))],
        out_specs=[pl.BlockSpec((tt, w2), lambda b, t: (t, b)), pl.BlockSpec((tt, LANE), lambda b, t: (t, b))],
        out_shape=[jax.ShapeDtypeStruct((l, S5_LANE_BLOCKS * w2), F32), jax.ShapeDtypeStruct((l, D_HALF), F32)],
        scratch_shapes=[pltpu.VMEM((SUBLANE, w2), F32)],
        compiler_params=_ARB2)(proj, wb, wc, d_skip, sc)


def _s5_bwd(proj, dy, states, wb, wc, d_skip, sc_rev, *, name):
    l = proj.shape[0]
    tt = _pick_tile(l, S5_TIME_CHUNK, SUBLANE)
    nt = l // tt
    u_off = (proj.shape[1] - D_HALF) // LANE
    w2 = 2 * S5_STATE_LANES
    half = S5_STATE_LANES
    grp_per_chunk = tt // SUBLANE

    def body(u_ref, dy_ref, s_ref, halo_ref, wb_ref, wc_ref, d_ref, sc_ref,
             du_ref, dwb_ref, dwc_ref, dlam_ref, dd_ref, g_scr, carry_ref):
        t = pl.program_id(1)

        @pl.when(t == 0)
        def _():
            carry_ref[...] = jnp.zeros_like(carry_ref)
            dwb_ref[...] = jnp.zeros_like(dwb_ref)
            dwc_ref[...] = jnp.zeros_like(dwc_ref)
            dlam_ref[...] = jnp.zeros_like(dlam_ref)
            dd_ref[...] = jnp.zeros_like(dd_ref)

        u = u_ref[...]
        dyv = dy_ref[...]
        g_scr[...] = _dot(dyv, wc_ref[0], _NT)
        _scan_rows(g_scr, sc_ref.at[0], carry_ref, tt, True)
        gv = g_scr[...]
        du_ref[...] = (_dot(gv, wb_ref[0], _NT) + d_ref[...] * dyv).astype(BF16)
        dwb_ref[0] += _dot(u, gv, _TN)
        sv = s_ref[...]
        dwc_ref[0] += _dot(sv, dyv, _TN)
        dd_ref[...] += jnp.sum(dyv * u, axis=0, keepdims=True)
        first_chunk = t == nt - 1
        halo = jnp.where(first_chunk, 0.0, halo_ref[SUBLANE - 1:SUBLANE, :])
        row = lax.broadcasted_iota(jnp.int32, sv.shape, 0)
        sp = jnp.where(row == 0, jnp.broadcast_to(halo, sv.shape), pltpu.roll(sv, 1, 0))
        gr, gi = gv[:, :half], gv[:, half:]
        sr, si = sp[:, :half], sp[:, half:]
        dlr = jnp.sum(gr * sr + gi * si, axis=0, keepdims=True)
        dli = jnp.sum(gi * sr - gr * si, axis=0, keepdims=True)
        dlam_ref[0] += jnp.concatenate([dlr, dli], axis=1)

    rev = lambda t: nt - 1 - t
    return pl.pallas_call(
        body, name=name, grid=(S5_LANE_BLOCKS, nt),
        in_specs=[pl.BlockSpec((tt, LANE), lambda b, t: (rev(t), b + u_off)),
                  pl.BlockSpec((tt, LANE), lambda b, t: (rev(t), b)),
                  pl.BlockSpec((tt, w2), lambda b, t: (rev(t), b)),
                  pl.BlockSpec((SUBLANE, w2), lambda b, t: (jnp.maximum(rev(t) * grp_per_chunk - 1, 0), b)),
                  pl.BlockSpec((1, LANE, w2), lambda b, t: (b, 0, 0)),
                  pl.BlockSpec((1, w2, LANE), lambda b, t: (b, 0, 0)),
                  pl.BlockSpec((1, LANE), lambda b, t: (0, b)),
                  pl.BlockSpec((1, 4, SUBLANE, w2), lambda b, t: (b, 0, 0, 0))],
        out_specs=[pl.BlockSpec((tt, LANE), lambda b, t: (rev(t), b)),
                   pl.BlockSpec((1, LANE, w2), lambda b, t: (b, 0, 0)),
                   pl.BlockSpec((1, w2, LANE), lambda b, t: (b, 0, 0)),
                   pl.BlockSpec((1, 1, w2), lambda b, t: (b, 0, 0)),
                   pl.BlockSpec((1, LANE), lambda b, t: (0, b))],
        out_shape=[jax.ShapeDtypeStruct((l, D_HALF), BF16),
                   jax.ShapeDtypeStruct((S5_LANE_BLOCKS, LANE, w2), F32),
                   jax.ShapeDtypeStruct((S5_LANE_BLOCKS, w2, LANE), F32),
                   jax.ShapeDtypeStruct((S5_LANE_BLOCKS, 1, w2), F32),
                   jax.ShapeDtypeStruct((1, D_HALF), F32)],
        scratch_shapes=[pltpu.VMEM((tt, w2), F32), pltpu.VMEM((SUBLANE, w2), F32)],
        compiler_params=_ARB2)(proj, dy, states, states, wb, wc, d_skip, sc_rev)


def _glu_fwd(ypre, wg, bg, *, name):
    l, d = ypre.shape
    tl = _pick_tile(l, 512, SUBLANE)

    def body(y_ref, w_ref, b_ref, o_ref):
        yg = _gelu(y_ref[...])
        o_ref[...] = (yg * _sigmoid(_dot(yg, w_ref[...], _NN) + b_ref[...])).astype(BF16)

    row = pl.BlockSpec((tl, d), lambda i: (i, 0))
    return pl.pallas_call(
        body, name=name, grid=(l // tl,),
        in_specs=[row, pl.BlockSpec((d, d), lambda i: (0, 0)), pl.BlockSpec((1, d), lambda i: (0, 0))],
        out_specs=row, out_shape=jax.ShapeDtypeStruct((l, d), BF16), compiler_params=_PAR)(ypre, wg, bg)


def _glu_bwd(dcat, ypre, wg, bg, *, name):
    l, d = ypre.shape
    tl = _pick_tile(l, 512, SUBLANE)

    def body(dy_ref, y_ref, w_ref, b_ref, dyp_ref, dw_ref, db_ref):
        @pl.when(pl.program_id(0) == 0)
        def _():
            dw_ref[...] = jnp.zeros_like(dw_ref)
            db_ref[...] = jnp.zeros_like(db_ref)
        yp = y_ref[...]
        dyb = dy_ref[...]
        yg = _gelu(yp)
        sg = _sigmoid(_dot(yg, w_ref[...], _NN) + b_ref[...])
        dz = dyb * yg * sg * (1.0 - sg)
        dyg = dyb * sg + _dot(dz, w_ref[...], _NT)
        dyp_ref[...] = dyg * _gelu_grad(yp)
        dw_ref[...] += _dot(yg, dz, _TN)
        db_ref[...] += jnp.sum(dz, axis=0, keepdims=True)

    row = pl.BlockSpec((tl, d), lambda i: (i, 0))
    mat = pl.BlockSpec((d, d), lambda i: (0, 0))
    vec = pl.BlockSpec((1, d), lambda i: (0, 0))
    return pl.pallas_call(
        body, name=name, grid=(l // tl,),
        in_specs=[pl.BlockSpec((tl, d), lambda i: (i, 1)), row, mat, vec], out_specs=[row, mat, vec],
        out_shape=[jax.ShapeDtypeStruct((l, d), F32), jax.ShapeDtypeStruct((d, d), F32),
                   jax.ShapeDtypeStruct((1, d), F32)],
        compiler_params=_ARB)(dcat, ypre, wg, bg)


def _window_sum(x, w, trailing):
    s, d = x, 1
    while d < w:
        s = s + (_shift_dn(s, d) if trailing else _shift_up(s, d))
        d *= 2
    return s


def _window_count(shape, w):
    row = lax.broadcasted_iota(jnp.int32, shape, 0)
    return jnp.minimum(row + 1, w).astype(F32)


def _pool_fwd(proj, pw, scale, *, name):
    l = proj.shape[0]
    ng = len(POOL_WINDOWS)

    def body(z_ref, w_ref, sc_ref, o_ref):
        z = z_ref[...]
        for k, w in enumerate(POOL_WINDOWS):
            @pl.when(pl.program_id(0) == k)
            def _():
                pooled = _window_sum(z, w, True) / _window_count(z.shape, w) - z
                o_ref[...] = (_dot(pooled, w_ref[0], _NN) * sc_ref[...]).astype(BF16)

    col = pl.BlockSpec((l, LANE), lambda g: (0, g))
    return pl.pallas_call(
        body, name=name, grid=(ng,),
        in_specs=[col, pl.BlockSpec((1, LANE, LANE), lambda g: (g, 0, 0)), pl.BlockSpec((1, LANE), lambda g: (0, g))],
        out_specs=col, out_shape=jax.ShapeDtypeStruct((l, D_HALF), BF16), compiler_params=_PAR)(proj, pw, scale)


def _pool_bwd(proj, dcat, pw, scale, *, name):
    l = proj.shape[0]
    ng = len(POOL_WINDOWS)

    def body(z_ref, dy_ref, w_ref, sc_ref, dz_ref, dw_ref, dsc_ref):
        z = z_ref[...]
        dy = dy_ref[...]
        for k, w in enumerate(POOL_WINDOWS):
            @pl.when(pl.program_id(0) == k)
            def _():
                cnt = _window_count(z.shape, w)
                pooled = _window_sum(z, w, True) / cnt - z
                ypre = _dot(pooled, w_ref[0], _NN)
                dsc_ref[...] = jnp.sum(dy * ypre, axis=0, keepdims=True)
                dyp = dy * sc_ref[...]
                dw_ref[0] = _dot(pooled, dyp, _TN)
                dpool = _dot(dyp, w_ref[0], _NT)
                dz_ref[...] = (_window_sum(dpool / cnt, w, False) - dpool).astype(BF16)

    col = pl.BlockSpec((l, LANE), lambda g: (0, g))
    mat = pl.BlockSpec((1, LANE, LANE), lambda g: (g, 0, 0))
    vec = pl.BlockSpec((1, LANE), lambda g: (0, g))
    return pl.pallas_call(
        body, name=name, grid=(ng,), in_specs=[col, col, mat, vec], out_specs=[col, mat, vec],
        out_shape=[jax.ShapeDtypeStruct((l, D_HALF), BF16), jax.ShapeDtypeStruct((ng, LANE, LANE), F32),
                   jax.ShapeDtypeStruct((1, D_HALF), F32)],
        compiler_params=_PAR)(proj, dcat, pw, scale)


def _tril_mask():
    r = lax.broadcasted_iota(jnp.int32, (CHUNK, CHUNK), 0)
    c = lax.broadcasted_iota(jnp.int32, (CHUNK, CHUNK), 1)
    return r >= c


def _sgu_fwd(proj, ng, sw, sb_t, *, name):
    l = proj.shape[0]
    tl = _pick_tile(l, 512, CHUNK)

    def body(su_ref, sv_ref, g_ref, w_ref, b_ref, o_ref):
        su = _gelu(su_ref[...])
        sv = _gelu(sv_ref[...])
        r = lax.rsqrt(jnp.mean(sv * sv, axis=-1, keepdims=True) + EPS)
        v = sv * r * g_ref[...]
        mask = _tril_mask()
        for h in range(SGU_HEADS):
            wm = jnp.where(mask, w_ref[h], 0.0)
            cs = slice(h * LANE, (h + 1) * LANE)
            for n in range(tl // CHUNK):
                rs = slice(n * CHUNK, (n + 1) * CHUNK)
                mixed = _dot(wm, v[rs, cs], _NN) + b_ref[:, h:h + 1]
                o_ref[rs, cs] = (su[rs, cs] * mixed).astype(BF16)

    blk = lambda c: pl.BlockSpec((tl, D_HALF), lambda i: (i, c))
    return pl.pallas_call(
        body, name=name, grid=(l // tl,),
        in_specs=[blk(1), blk(2), pl.BlockSpec((1, D_HALF), lambda i: (0, 0)),
                  pl.BlockSpec((SGU_HEADS, CHUNK, CHUNK), lambda i: (0, 0, 0)),
                  pl.BlockSpec((CHUNK, SGU_HEADS), lambda i: (0, 0))],
        out_specs=blk(0), out_shape=jax.ShapeDtypeStruct((l, D_HALF), BF16),
        compiler_params=_PAR)(proj, proj, ng, sw, sb_t)


def _sgu_bwd(proj, dcat, ng, sw, sb_t, *, name):
    l = proj.shape[0]
    tl = _pick_tile(l, 512, CHUNK)

    def body(su_ref, sv_ref, dy_ref, g_ref, w_ref, b_ref, dsu_ref, dsv_ref, dw_ref, dbm_ref, dng_ref, dv_scr):
        @pl.when(pl.program_id(0) == 0)
        def _():
            dw_ref[...] = jnp.zeros_like(dw_ref)
            dbm_ref[...] = jnp.zeros_like(dbm_ref)
            dng_ref[...] = jnp.zeros_like(dng_ref)
        su_pre = su_ref[...]
        sv_pre = sv_ref[...]
        su = _gelu(su_pre)
        sv = _gelu(sv_pre)
        gsu = _gelu_grad(su_pre)
        gv = g_ref[...]
        r = lax.rsqrt(jnp.mean(sv * sv, axis=-1, keepdims=True) + EPS)
        v = sv * r * gv
        dy = dy_ref[...]
        mask = _tril_mask()
        for h in range(SGU_HEADS):
            wm = jnp.where(mask, w_ref[h], 0.0)
            cs = slice(h * LANE, (h + 1) * LANE)
            dw_acc = jnp.zeros((CHUNK, CHUNK), F32)
            db_acc = jnp.zeros((CHUNK, LANE), F32)
            for n in range(tl // CHUNK):
                rs = slice(n * CHUNK, (n + 1) * CHUNK)
                vb = v[rs, cs]
                mixed = _dot(wm, vb, _NN) + b_ref[:, h:h + 1]
                dyb = dy[rs, cs]
                dsu_ref[rs, cs] = (dyb * mixed * gsu[rs, cs]).astype(BF16)
                dmix = dyb * su[rs, cs]
                db_acc = db_acc + dmix
                dw_acc = dw_acc + _dot(dmix, vb, _NT)
                dv_scr[rs, cs] = _dot(wm, dmix, _TN)
            dw_ref[h] += jnp.where(mask, dw_acc, 0.0)
            dbm_ref[h] += db_acc
        dv = dv_scr[...]
        a = dv * gv
        m = jnp.mean(a * sv, axis=-1, keepdims=True)
        dsv = r * a - sv * (r * r * r) * m
        dng_ref[...] += jnp.sum(dv * sv * r, axis=0, keepdims=True)
        dsv_ref[...] = (dsv * _gelu_grad(sv_pre)).astype(BF16)

    blk = lambda c: pl.BlockSpec((tl, D_HALF), lambda i: (i, c))
    mats = pl.BlockSpec((SGU_HEADS, CHUNK, CHUNK), lambda i: (0, 0, 0))
    vec = pl.BlockSpec((1, D_HALF), lambda i: (0, 0))
    piece = jax.ShapeDtypeStruct((l, D_HALF), BF16)
    mshape = jax.ShapeDtypeStruct((SGU_HEADS, CHUNK, CHUNK), F32)
    return pl.pallas_call(
        body, name=name, grid=(l // tl,),
        in_specs=[blk(1), blk(2), blk(1), vec, mats, pl.BlockSpec((CHUNK, SGU_HEADS), lambda i: (0, 0))],
        out_specs=[blk(0), blk(0), mats, mats, vec],
        out_shape=[piece, piece, mshape, mshape, jax.ShapeDtypeStruct((1, D_HALF), F32)],
        scratch_shapes=[pltpu.VMEM((tl, D_HALF), F32)],
        compiler_params=_ARB)(proj, proj, dcat, ng, sw, sb_t)


def _s5_layer_params(w, j):
    prep_args = (w["ssm_log_step"][j], w["ssm_a_re"][j], w["ssm_a_im"][j], w["ssm_b_re"][j], w["ssm_b_im"][j])
    (lr, li, br, bi), prep_vjp = jax.vjp(_s5_prep, *prep_args)
    wb, wc = _s5_mats(br, bi, w["ssm_c_re"][j], w["ssm_c_im"][j])
    return dict(
        wb=wb.astype(BF16), wc=wc.astype(BF16), d=w["ssm_d"][j][None, :],
        sc=_s5_scan_consts(*prep_args[:3], reverse=False), sc_rev=_s5_scan_consts(*prep_args[:3], reverse=True),
        prep_vjp=prep_vjp)


def _device_step(x, tgt, w):
    saved = []
    for i in range(DEPTH):
        j = i // 2
        h = _rms_fwd(x, w["norm_mix_g"][i:i + 1], name="rms_fwd")
        if i % 2 == 0:
            proj = _mm(h, w["even_w_in"][j], "nn", name="mm_even_in")
            ya = _sconv_fwd(proj, w["even_conv_w"][j], name="sconv_fwd")
            sp = _s5_layer_params(w, j)
            states, ypre = _s5_fwd(proj, sp["wb"], sp["wc"], sp["d"], sp["sc"], name="s5_fwd")
            yb = _glu_fwd(ypre, w["ssm_glu_w"][j], w["ssm_glu_b"][j][None, :], name="glu_fwd")
            cat = jnp.concatenate([ya, yb], axis=1)
            x1 = _mm(cat, w["even_w_out"][j], "nn", add=x, name="mm_mix_out")
            mix = (states, ypre, sp)
        else:
            proj = _mm(h, w["odd_w_in"][j], "nn", name="mm_odd_in")
            yc = _pool_fwd(proj, w["pool_w"][j], w["pool_scale"][j][None, :], name="pool_fwd")
            sb_t = jnp.transpose(w["sgu_b"][j])
            yd = _sgu_fwd(proj, w["sgu_norm_g"][j][None, :], w["sgu_w"][j], sb_t, name="sgu_fwd")
            cat = jnp.concatenate([yc, yd], axis=1)
            x1 = _mm(cat, w["odd_w_out"][j], "nn", add=x, name="mm_mix_out")
            mix = (sb_t,)
        h2 = _rms_fwd(x1, w["norm_ffn_g"][i:i + 1], name="rms_fwd")
        up = _mm(h2, w["ffn_w_up"][i], "nn", name="mm_up")
        act = _ffn_act(up, w["ffn_conv_w"][i], w["ffn_conv_b"][i:i + 1], name="ffn_act")
        x2 = _mm(act, w["ffn_w_down"][i], "nn", add=x1, name="mm_down")
        saved.append((x, h, proj, cat, x1, h2, up, act, mix))
        x = x2

    loss, dx, dgf = _loss_head(x, w["norm_final_g"][None, :], tgt, name="loss_head")
    per_layer = {}

    def put(name, idx, val):
        per_layer.setdefault(name, {})[idx] = val

    for i in reversed(range(DEPTH)):
        j = i // 2
        x0, h, proj, cat, x1, h2, up, act, mix = saved[i]
        dact = _mm(dx, w["ffn_w_down"][i], "nt", name="mm_down_dx")
        put("ffn_w_down", i, _mm(act, dx, "tn", name="mm_down_dw"))
        dup, dcw, dcb = _ffn_act_bwd(up, dact, w["ffn_conv_w"][i], w["ffn_conv_b"][i:i + 1], name="ffn_act_bwd")
        put("ffn_conv_w", i, dcw)
        put("ffn_conv_b", i, dcb[0])
        dh2 = _mm(dup, w["ffn_w_up"][i], "nt", tn_cap=512, name="mm_up_dx")
        put("ffn_w_up", i, _mm(h2, dup, "tn", name="mm_up_dw"))
        dx1, dg2 = _rms_bwd(dh2, x1, w["norm_ffn_g"][i:i + 1], dx, name="rms_bwd")
        put("norm_ffn_g", i, dg2[0])
        if i % 2 == 0:
            states, ypre, sp = mix
            dcat = _mm(dx1, w["even_w_out"][j], "nt", name="mm_mix_out_dx")
            put("even_w_out", j, _mm(cat, dx1, "tn", name="mm_mix_out_dw"))
            dxa, dba, dca, dcw_a = _sconv_bwd(proj, dcat, w["even_conv_w"][j], name="sconv_bwd")
            put("even_conv_w", j, dcw_a)
            dypre, dwg, dbg = _glu_bwd(dcat, ypre, w["ssm_glu_w"][j], w["ssm_glu_b"][j][None, :], name="glu_bwd")
            put("ssm_glu_w", j, dwg)
            put("ssm_glu_b", j, dbg[0])
            du, dwb, dwc, dlam, dd = _s5_bwd(proj, dypre, states, sp["wb"], sp["wc"], sp["d"], sp["sc_rev"],
                                             name="s5_bwd")
            dbr, dbi, dcr, dci = _s5_mats_bwd(dwb, dwc)
            dlr = dlam[:, 0, :S5_STATE_LANES].reshape(N_SSM_GROUPS, SSM_STATE)
            dli = dlam[:, 0, S5_STATE_LANES:].reshape(N_SSM_GROUPS, SSM_STATE)
            dls, dar, dai, db_re, db_im = sp["prep_vjp"]((dlr, dli, dbr, dbi))
            for nm, val in (("ssm_log_step", dls), ("ssm_a_re", dar), ("ssm_a_im", dai), ("ssm_b_re", db_re),
                            ("ssm_b_im", db_im), ("ssm_c_re", dcr), ("ssm_c_im", dci), ("ssm_d", dd[0])):
                put(nm, j, val)
            dproj = jnp.concatenate([dxa, dba, dca, du], axis=1)
            dh = _mm(dproj, w["even_w_in"][j], "nt", name="mm_even_in_dx")
            put("even_w_in", j, _mm(h, dproj, "tn", name="mm_even_in_dw"))
        else:
            (sb_t,) = mix
            dcat = _mm(dx1, w["odd_w_out"][j], "nt", name="mm_mix_out_dx")
            put("odd_w_out", j, _mm(cat, dx1, "tn", name="mm_mix_out_dw"))
            dz, dpw, dps = _pool_bwd(proj, dcat, w["pool_w"][j], w["pool_scale"][j][None, :], name="pool_bwd")
            put("pool_w", j, dpw)
            put("pool_scale", j, dps[0])
            dsu, dsv, dsw, dbm, dng = _sgu_bwd(proj, dcat, w["sgu_norm_g"][j][None, :], w["sgu_w"][j], sb_t,
                                               name="sgu_bwd")
            put("sgu_w", j, dsw)
            put("sgu_b", j, jnp.sum(dbm, axis=-1))
            put("sgu_norm_g", j, dng[0])
            dproj = jnp.concatenate([dz, dsu, dsv], axis=1)
            dh = _mm(dproj, w["odd_w_in"][j], "nt", name="mm_odd_in_dx")
            put("odd_w_in", j, _mm(h, dproj, "tn", name="mm_odd_in_dw"))
        dx, dg1 = _rms_bwd(dh, x0, w["norm_mix_g"][i:i + 1], dx1, name="rms_bwd")
        put("norm_mix_g", i, dg1[0])

    grads = {}
    for nm, vals in per_layer.items():
        layers = [vals[k] for k in sorted(vals)]
        grads[nm] = layers if nm in BIG else jnp.stack(layers)
    grads["norm_final_g"] = dgf[0]
    return loss, dx, grads


_MESH = pl.DeviceIdType.MESH
_ANY = pl.BlockSpec(memory_space=pl.ANY)


def _all_gather(shard, *, name):
    r, c = shard.shape

    def body(x_ref, out_ref, send_sems, recv_sems, local_sem):
        x, y, cc = lax.axis_index("x"), lax.axis_index("y"), lax.axis_index("c")
        me, sibling = (x, y, cc), (x, y, 1 - cc)
        chips = [(1 - x, y), (x, 1 - y), (1 - x, 1 - y)]

        def rows(px, py, pc):
            return out_ref.at[4 * px + 2 * py + pc]

        def copy(k, block, to, src=None):
            return pltpu.make_async_remote_copy(
                src_ref=rows(*block) if src is None else src, dst_ref=rows(*block),
                send_sem=send_sems.at[k], recv_sem=recv_sems.at[k], device_id=to, device_id_type=_MESH)

        mine = pltpu.make_async_copy(x_ref, rows(*me), local_sem)
        mine.start()
        first = [copy(0, me, sibling, src=x_ref)]
        first += [copy(1 + k, me, (*chip, cc), src=x_ref) for k, chip in enumerate(chips)]
        for cp in first:
            cp.start()
        passed = [copy(4 + k, (*chip, cc), sibling) for k, chip in enumerate(chips)]
        for k, chip in enumerate(chips):
            copy(1 + k, (*chip, cc), me).wait_recv()
            passed[k].start()
        copy(0, sibling, me).wait_recv()
        for k, chip in enumerate(chips):
            copy(4 + k, (*chip, 1 - cc), me).wait_recv()
        for cp in first + passed:
            cp.wait_send()
        mine.wait()

    return pl.pallas_call(
        body, name=name, out_shape=jax.ShapeDtypeStruct((N_DEV, r, c), shard.dtype),
        in_specs=[_ANY], out_specs=_ANY,
        scratch_shapes=[pltpu.SemaphoreType.DMA((7,)), pltpu.SemaphoreType.DMA((7,)), pltpu.SemaphoreType.DMA],
    )(shard)


def _sibling_exchange(p, *, name):
    _, r, c = p.shape

    def body(p_ref, out_ref, send_sems, recv_sems):
        x, y, cc = lax.axis_index("x"), lax.axis_index("y"), lax.axis_index("c")
        copies = [pltpu.make_async_remote_copy(
            src_ref=p_ref.at[2 * k + (1 - cc)], dst_ref=out_ref.at[k], send_sem=send_sems.at[k],
            recv_sem=recv_sems.at[k], device_id=(x, y, 1 - cc), device_id_type=_MESH) for k in range(4)]
        for cp in copies:
            cp.start()
        for cp in copies:
            cp.wait()

    return pl.pallas_call(
        body, name=name, out_shape=jax.ShapeDtypeStruct((4, r, c), p.dtype), in_specs=[_ANY], out_specs=_ANY,
        scratch_shapes=[pltpu.SemaphoreType.DMA((4,)), pltpu.SemaphoreType.DMA((4,))],
    )(p)


def _chip_exchange(q, *, name):
    _, r, c = q.shape

    def body(q_ref, out_ref, send_sems, recv_sems, local_sem):
        x, y, cc = lax.axis_index("x"), lax.axis_index("y"), lax.axis_index("c")
        my_chip = 2 * x + y
        chips = [(1 - x, y), (x, 1 - y), (1 - x, 1 - y)]
        local = pltpu.make_async_copy(q_ref.at[my_chip], out_ref.at[my_chip], local_sem)
        local.start()

        def copy(k, src_chip, dst_chip, to):
            return pltpu.make_async_remote_copy(
                src_ref=q_ref.at[dst_chip], dst_ref=out_ref.at[src_chip], send_sem=send_sems.at[k],
                recv_sem=recv_sems.at[k], device_id=to, device_id_type=_MESH)

        sends = [copy(k, my_chip, 2 * px + py, (px, py, cc)) for k, (px, py) in enumerate(chips)]
        for cp in sends:
            cp.start()
        for k, (px, py) in enumerate(chips):
            copy(k, 2 * px + py, my_chip, (px, py, cc)).wait_recv()
        for cp in sends:
            cp.wait_send()
        local.wait()

    return pl.pallas_call(
        body, name=name, out_shape=jax.ShapeDtypeStruct((4, r, c), q.dtype), in_specs=[_ANY], out_specs=_ANY,
        scratch_shapes=[pltpu.SemaphoreType.DMA((3,)), pltpu.SemaphoreType.DMA((3,)), pltpu.SemaphoreType.DMA],
    )(q)


def _parity_add(p, got, *, name):
    _, r, c = p.shape
    tr = _pick_tile(r, max(16, EXCHANGE_BLOCK_ELEMS // c), 16)
    my_c = lax.axis_index("c").astype(jnp.int32).reshape(1)

    def body(c_ref, p_ref, g_ref, o_ref):
        o_ref[...] = (p_ref[...].astype(F32) + g_ref[...].astype(F32)).astype(o_ref.dtype)

    grid_spec = pltpu.PrefetchScalarGridSpec(
        num_scalar_prefetch=1, grid=(4, r // tr),
        in_specs=[pl.BlockSpec((1, tr, c), lambda k, i, c_ref: (2 * k + c_ref[0], i, 0)),
                  pl.BlockSpec((1, tr, c), lambda k, i, c_ref: (k, i, 0))],
        out_specs=pl.BlockSpec((1, tr, c), lambda k, i, c_ref: (k, i, 0)))
    return pl.pallas_call(
        body, name=name, grid_spec=grid_spec, out_shape=jax.ShapeDtypeStruct((4, r, c), p.dtype),
        compiler_params=_PAR2)(my_c, p, got)


def _sum_parts(parts, *, name):
    g, r, c = parts.shape
    tr = _pick_tile(r, max(16, EXCHANGE_BLOCK_ELEMS // c), 16)

    def body(p_ref, o_ref):
        acc = p_ref[0].astype(F32)
        for k in range(1, g):
            acc = acc + p_ref[k].astype(F32)
        o_ref[...] = acc

    return pl.pallas_call(
        body, name=name, grid=(r // tr,), in_specs=[pl.BlockSpec((g, tr, c), lambda i: (0, i, 0))],
        out_specs=pl.BlockSpec((tr, c), lambda i: (i, 0)), out_shape=jax.ShapeDtypeStruct((r, c), F32),
        compiler_params=_PAR)(parts)


def _reduce_scatter(p, *, tag):
    got = _sibling_exchange(p, name="rs_sibling_" + tag)
    q = _parity_add(p, got, name="rs_add_" + tag)
    return _chip_exchange(q, name="rs_chips_" + tag)


def _adamw(w, m, v, g_parts, *, name):
    r, c = w.shape
    g = g_parts.shape[0]
    tc = _pick_tile(c, 8192, LANE)
    tr = _pick_tile(r, max(16, (1 << 18) // tc), 16)
    c1 = 1.0 - ADAM_B1 ** ADAM_STEP
    c2 = 1.0 - ADAM_B2 ** ADAM_STEP

    def body(w_ref, m_ref, v_ref, g_ref, go_ref, d_ref, mo_ref, vo_ref):
        grad = g_ref[0].astype(F32)
        for k in range(1, g):
            grad = grad + g_ref[k].astype(F32)
        m_new = ADAM_B1 * m_ref[...] + (1.0 - ADAM_B1) * grad
        v_new = ADAM_B2 * v_ref[...] + (1.0 - ADAM_B2) * (grad * grad)
        go_ref[...] = grad
        mo_ref[...] = m_new
        vo_ref[...] = v_new
        d_ref[...] = -ADAM_LR * ((m_new / c1) / (jnp.sqrt(v_new / c2) + ADAM_EPS) + ADAM_WD * w_ref[...])

    blk = pl.BlockSpec((tr, tc), lambda i, j: (i, j))
    out = jax.ShapeDtypeStruct((r, c), F32)
    return pl.pallas_call(
        body, name=name, grid=(r // tr, c // tc),
        in_specs=[blk, blk, blk, pl.BlockSpec((g, tr, tc), lambda i, j: (0, i, j))],
        out_specs=[blk, blk, blk, blk], out_shape=[out, out, out, out], compiler_params=_PAR2)(w, m, v, g_parts)


WEIGHT_NAMES = ['norm_mix_g', 'even_w_in', 'even_conv_w', 'ssm_log_step', 'ssm_a_re', 'ssm_a_im', 'ssm_b_re',
                'ssm_b_im', 'ssm_c_re', 'ssm_c_im', 'ssm_d', 'ssm_glu_w', 'ssm_glu_b', 'even_w_out', 'odd_w_in',
                'pool_w', 'pool_scale', 'sgu_norm_g', 'sgu_w', 'sgu_b', 'odd_w_out', 'norm_ffn_g', 'ffn_w_up',
                'ffn_conv_w', 'ffn_conv_b', 'ffn_w_down', 'norm_final_g']
BIG = {'even_w_in': 2, 'ssm_glu_w': 1, 'even_w_out': 1, 'odd_w_in': 2, 'odd_w_out': 1, 'ffn_w_up': 2,
       'ffn_w_down': 1}
SMALL = {'even_conv_w': 2, 'pool_scale': 1, 'sgu_norm_g': 1, 'ffn_conv_w': 2}
BIG_ROWS = 512
SMALL_ROWS = 16


def _pad_to(n, q):
    return -(-n // q) * q


def _pack(arrays, dtype, rows, lead=()):
    flat = [a.reshape(lead + (-1,)).astype(dtype) for a in arrays]
    n = sum(f.shape[-1] for f in flat)
    pad = _pad_to(n, rows * LANE) - n
    if pad:
        flat.append(jnp.zeros(lead + (pad,), dtype))
    return jnp.concatenate(flat, axis=-1).reshape(lead + (rows, -1))


def _unpack(buf, shapes, lead=()):
    flat = buf.reshape(lead + (-1,))
    out, off = [], 0
    for shp in shapes:
        n = math.prod(shp)
        out.append(flat[..., off:off + n].reshape(lead + tuple(shp)))
        off += n
    return out


def _to_dest_major(full, axis):
    shp = full.shape
    split = full.reshape(shp[:axis] + (N_DEV, shp[axis] // N_DEV) + shp[axis + 1:])
    return jnp.moveaxis(split, axis, 0)


def _from_dest_major(blocks, axis):
    moved = jnp.moveaxis(blocks, 0, axis)
    shp = moved.shape
    return moved.reshape(shp[:axis] + (shp[axis] * shp[axis + 1],) + shp[axis + 2:])


def _rows_2d(a):
    return a.reshape(-1, a.shape[-1])


def _rows_2d_lead(a):
    return a.reshape(a.shape[0], -1, a.shape[-1])


def kernel(x, norm_mix_g, even_w_in, even_conv_w, ssm_log_step, ssm_a_re, ssm_a_im, ssm_b_re, ssm_b_im, ssm_c_re, ssm_c_im, ssm_d, ssm_glu_w, ssm_glu_b, even_w_out, odd_w_in, pool_w, pool_scale, sgu_norm_g, sgu_w, sgu_b, odd_w_out, norm_ffn_g, ffn_w_up, ffn_conv_w, ffn_conv_b, ffn_w_down, norm_final_g, loss_target, m_norm_mix_g, m_even_w_in, m_even_conv_w, m_ssm_log_step, m_ssm_a_re, m_ssm_a_im, m_ssm_b_re, m_ssm_b_im, m_ssm_c_re, m_ssm_c_im, m_ssm_d, m_ssm_glu_w, m_ssm_glu_b, m_even_w_out, m_odd_w_in, m_pool_w, m_pool_scale, m_sgu_norm_g, m_sgu_w, m_sgu_b, m_odd_w_out, m_norm_ffn_g, m_ffn_w_up, m_ffn_conv_w, m_ffn_conv_b, m_ffn_w_down, m_norm_final_g, v_norm_mix_g, v_even_w_in, v_even_conv_w, v_ssm_log_step, v_ssm_a_re, v_ssm_a_im, v_ssm_b_re, v_ssm_b_im, v_ssm_c_re, v_ssm_c_im, v_ssm_d, v_ssm_glu_w, v_ssm_glu_b, v_even_w_out, v_odd_w_in, v_pool_w, v_pool_scale, v_sgu_norm_g, v_sgu_w, v_sgu_b, v_odd_w_out, v_norm_ffn_g, v_ffn_w_up, v_ffn_conv_w, v_ffn_conv_b, v_ffn_w_down, v_norm_final_g):
    given = dict(locals())
    wts = {n: given[n] for n in WEIGHT_NAMES}
    mom = {n: given["m_" + n] for n in WEIGHT_NAMES}
    var = {n: given["v_" + n] for n in WEIGHT_NAMES}
    repl = [n for n in WEIGHT_NAMES if n not in BIG and n not in SMALL]

    small_shapes = [wts[n].shape for n in SMALL]
    small_all = _all_gather(_pack([wts[n] for n in SMALL], F32, SMALL_ROWS), name="ag_small")
    full = dict(wts)
    for n in BIG:
        shp = wts[n].shape
        got = _all_gather(_rows_2d(wts[n]).astype(BF16), name="ag_" + n)
        full[n] = _from_dest_major(got.reshape((N_DEV,) + shp), BIG[n])
    for n, blocks in zip(SMALL, _unpack(small_all, small_shapes, lead=(N_DEV,))):
        full[n] = _from_dest_major(blocks, SMALL[n])

    loss, dx, grads = _device_step(x[0], loss_target[0], full)

    big_rs = {}
    for n in BIG:
        parts = [_rows_2d_lead(_to_dest_major(g, BIG[n] - 1).astype(BF16)) for g in grads[n]]
        big_rs[n] = _reduce_scatter(jnp.concatenate(parts, axis=1), tag=n)
    repl_shapes = [wts[n].shape for n in repl]
    repl_flat = jnp.concatenate([grads[n].reshape(-1) for n in repl] + [loss.reshape(-1)])
    n_repl = repl_flat.shape[0]
    chunk = _pad_to(-(-n_repl // N_DEV), SMALL_ROWS * LANE)
    repl_flat = jnp.pad(repl_flat, (0, N_DEV * chunk - n_repl))
    small_part = _pack([_to_dest_major(grads[n], SMALL[n]) for n in SMALL], F32, SMALL_ROWS, lead=(N_DEV,))
    small_cols = small_part.shape[2]
    small_rs = _reduce_scatter(
        jnp.concatenate([small_part, repl_flat.reshape(N_DEV, SMALL_ROWS, chunk // SMALL_ROWS)], axis=2), tag="small")
    small_sum = _sum_parts(small_rs, name="rs_sum_small")
    repl_sum = _all_gather(small_sum[:, small_cols:], name="ag_repl").reshape(-1)
    total_loss = repl_sum[n_repl - 1]

    out = {}
    for n in BIG:
        shp = wts[n].shape
        res = _adamw(_rows_2d(wts[n]), _rows_2d(mom[n]), _rows_2d(var[n]), big_rs[n], name="adamw_" + n)
        out[n] = [r.reshape(shp) for r in res]

    def small_vec(shard_part, repl_part):
        flat = jnp.concatenate([shard_part.reshape(-1), repl_part])
        return flat.reshape(SMALL_ROWS, -1)

    def small_tree(tree):
        tail = jnp.concatenate([tree[n].reshape(-1) for n in repl])
        tail = jnp.pad(tail, (0, N_DEV * chunk - tail.shape[0]))
        return small_vec(_pack([tree[n] for n in SMALL], F32, SMALL_ROWS), tail)

    res = _adamw(small_tree(wts), small_tree(mom), small_tree(var), small_vec(small_sum[:, :small_cols], repl_sum)[None],
                 name="adamw_small")
    n_small = SMALL_ROWS * small_cols
    for k, r in enumerate(res):
        flat = r.reshape(-1)
        shard = _unpack(flat[:n_small], small_shapes)
        rest = _unpack(flat[n_small:], repl_shapes)
        for n, val in zip(SMALL, shard):
            out.setdefault(n, [None] * 4)[k] = val
        for n, val in zip(repl, rest):
            out.setdefault(n, [None] * 4)[k] = val

    grad_x = dx[None]
    return (total_loss, grad_x, *[out[n][0] for n in WEIGHT_NAMES], *[out[n][1] for n in WEIGHT_NAMES],
            *[out[n][2] for n in WEIGHT_NAMES], *[out[n][3] for n in WEIGHT_NAMES])
```

```python
import functools
import math

import jax
import jax.numpy as jnp
from jax import lax
from jax.experimental import pallas as pl
from jax.experimental.pallas import tpu as pltpu

F32 = jnp.float32
BF16 = jnp.bfloat16

D_MODEL = 1024
DEPTH = 4
D_HALF = D_MODEL // 2
SSM_GROUP = 16
N_SSM_GROUPS = D_HALF // SSM_GROUP
SSM_STATE = 64
POOL_WINDOWS = (2, 4, 8, 16)
SGU_HEADS = 4
CHUNK = 128
D_FF = 2816
CONV_WIDTH = 3
EPS = 1e-6
N_DEV = 8

ADAM_LR = 0.001
ADAM_B1 = 0.9
ADAM_B2 = 0.999
ADAM_EPS = 1e-08
ADAM_WD = 0.01
ADAM_STEP = 10

LANE = 128
SUBLANE = 8
S5_LANE_BLOCKS = D_HALF // LANE
S5_STATE_LANES = (N_SSM_GROUPS // S5_LANE_BLOCKS) * SSM_STATE
S5_TIME_CHUNK = 512
EXCHANGE_BLOCK_ELEMS = 1 << 20

GELU_K = math.sqrt(2.0 / math.pi)
GELU_C = 0.044715

_ARB = pltpu.CompilerParams(dimension_semantics=("arbitrary",))
_ARB2 = pltpu.CompilerParams(dimension_semantics=("arbitrary", "arbitrary"))
_PAR = pltpu.CompilerParams(dimension_semantics=("parallel",))
_PAR2 = pltpu.CompilerParams(dimension_semantics=("parallel", "parallel"))


def _pick_tile(n, cap, mult):
    if n <= cap:
        return n
    best = None
    for t in range(mult, cap + 1, mult):
        if n % t == 0:
            best = t
    assert best is not None, (n, cap, mult)
    return best


_MESH = pl.DeviceIdType.MESH
_ANY = pl.BlockSpec(memory_space=pl.ANY)
SEMS_PER_EXCHANGE = N_DEV - 1


class _Gather:
    def __init__(self, src):
        self.src = src
        self.out_shape = jax.ShapeDtypeStruct((N_DEV,) + src.shape, src.dtype)

    def copies(self, x_ref, out_ref, send_sems, recv_sems, local_sem):
        x, y, cc = lax.axis_index("x"), lax.axis_index("y"), lax.axis_index("c")
        me, sibling = (x, y, cc), (x, y, 1 - cc)
        chips = [(1 - x, y), (x, 1 - y), (1 - x, 1 - y)]

        def rows(px, py, pc):
            return out_ref.at[4 * px + 2 * py + pc]

        def copy(k, block, to, src=None):
            return pltpu.make_async_remote_copy(
                src_ref=rows(*block) if src is None else src, dst_ref=rows(*block),
                send_sem=send_sems.at[k], recv_sem=recv_sems.at[k], device_id=to, device_id_type=_MESH)

        return dict(
            mine=pltpu.make_async_copy(x_ref, rows(*me), local_sem),
            first=[copy(0, me, sibling, src=x_ref)] + [copy(1 + k, me, (*chip, cc), src=x_ref)
                                                       for k, chip in enumerate(chips)],
            passed=[copy(4 + k, (*chip, cc), sibling) for k, chip in enumerate(chips)],
            over_ici=[copy(1 + k, (*chip, cc), me) for k, chip in enumerate(chips)],
            from_sibling=[copy(0, sibling, me)] + [copy(4 + k, (*chip, 1 - cc), me) for k, chip in enumerate(chips)])

    def start(self, *refs):
        cps = self.copies(*refs)
        cps["mine"].start()
        for cp in cps["first"]:
            cp.start()

    def finish(self, *refs):
        cps = self.copies(*refs)
        for arrived, onward in zip(cps["over_ici"], cps["passed"]):
            arrived.wait_recv()
            onward.start()
        for arrived in cps["from_sibling"]:
            arrived.wait_recv()
        for cp in cps["first"] + cps["passed"]:
            cp.wait_send()
        cps["mine"].wait()


class _Scatter:
    def __init__(self, src):
        self.src = src
        self.out_shape = jax.ShapeDtypeStruct(src.shape, src.dtype)

    def copies(self, p_ref, out_ref, send_sems, recv_sems, local_sem):
        x, y, cc = lax.axis_index("x"), lax.axis_index("y"), lax.axis_index("c")
        me = 4 * x + 2 * y + cc
        sends, arrivals = [], []
        for k in range(1, N_DEV):
            px = (1 - x) if k & 4 else x
            py = (1 - y) if k & 2 else y
            pc = (1 - cc) if k & 1 else cc
            peer = 4 * px + 2 * py + pc
            kw = dict(send_sem=send_sems.at[k - 1], recv_sem=recv_sems.at[k - 1], device_id=(px, py, pc),
                      device_id_type=_MESH)
            sends.append(pltpu.make_async_remote_copy(src_ref=p_ref.at[peer], dst_ref=out_ref.at[me], **kw))
            arrivals.append(pltpu.make_async_remote_copy(src_ref=p_ref.at[me], dst_ref=out_ref.at[peer], **kw))
        return dict(mine=pltpu.make_async_copy(p_ref.at[me], out_ref.at[me], local_sem), sends=sends,
                    arrivals=arrivals)

    def start(self, *refs):
        cps = self.copies(*refs)
        cps["mine"].start()
        for cp in cps["sends"]:
            cp.start()

    def finish(self, *refs):
        cps = self.copies(*refs)
        for cp in cps["arrivals"]:
            cp.wait_recv()
        for cp in cps["sends"]:
            cp.wait_send()
        cps["mine"].wait()


class _SemView:
    def __init__(self, ref, lo):
        self.ref, self.lo = ref, lo

    @property
    def at(self):
        return self

    def __getitem__(self, k):
        return self.ref.at[self.lo + k]


def _call(body, *, name, grid, in_specs, out_specs, out_shape, args, scratch_shapes=(), parallel=True, exchanges=()):
    n_axes = len(grid)
    if not exchanges:
        sem = ("parallel" if parallel else "arbitrary",) * n_axes
        return pl.pallas_call(
            body, name=name, grid=grid, in_specs=in_specs, out_specs=out_specs, out_shape=out_shape,
            scratch_shapes=scratch_shapes, compiler_params=pltpu.CompilerParams(dimension_semantics=sem))(*args)
    single = not isinstance(out_shape, (list, tuple))
    out_specs = [out_specs] if single else list(out_specs)
    out_shape = [out_shape] if single else list(out_shape)
    n_in, n_out, n_scr, n_x = len(in_specs), len(out_specs), len(scratch_shapes), len(exchanges)

    def wrapped(*refs):
        ins, refs = refs[:n_in], refs[n_in:]
        x_in, refs = refs[:n_x], refs[n_x:]
        outs, refs = refs[:n_out], refs[n_out:]
        x_out, refs = refs[:n_x], refs[n_x:]
        scr, (send_sems, recv_sems, local_sems) = refs[:n_scr], refs[n_scr:]
        ids = [pl.program_id(k) for k in range(n_axes)]
        first = functools.reduce(jnp.logical_and, [i == 0 for i in ids])
        last = functools.reduce(jnp.logical_and, [i == g - 1 for i, g in zip(ids, grid)])

        def sems(e):
            lo = e * SEMS_PER_EXCHANGE
            return _SemView(send_sems, lo), _SemView(recv_sems, lo), local_sems.at[e]

        @pl.when(first)
        def _():
            for e, ex in enumerate(exchanges):
                ex.start(x_in[e], x_out[e], *sems(e))

        body(*ins, *outs, *scr)

        @pl.when(last)
        def _():
            for e, ex in enumerate(exchanges):
                ex.finish(x_in[e], x_out[e], *sems(e))

    res = pl.pallas_call(
        wrapped, name=name, grid=grid, in_specs=list(in_specs) + [_ANY] * n_x, out_specs=out_specs + [_ANY] * n_x,
        out_shape=out_shape + [ex.out_shape for ex in exchanges],
        scratch_shapes=list(scratch_shapes) + [pltpu.SemaphoreType.DMA((n_x * SEMS_PER_EXCHANGE,)),
                                               pltpu.SemaphoreType.DMA((n_x * SEMS_PER_EXCHANGE,)),
                                               pltpu.SemaphoreType.DMA((n_x,))],
        compiler_params=pltpu.CompilerParams(dimension_semantics=("arbitrary",) * n_axes),
    )(*args, *[ex.src for ex in exchanges])
    outs, x_outs = res[:n_out], res[n_out:]
    return (outs[0] if single else outs), x_outs


def _exchange_only(exchanges, *, name):
    def body():
        pass

    return _call(body, name=name, grid=(1,), in_specs=[], out_specs=[], out_shape=[], args=[],
                 exchanges=exchanges)[1]


def _shift_dn(x, d):
    rolled = pltpu.roll(x, d, 0)
    row = lax.broadcasted_iota(jnp.int32, x.shape, 0)
    return jnp.where(row >= d, rolled, 0.0)


def _shift_up(x, d):
    n = x.shape[0]
    rolled = pltpu.roll(x, n - d, 0)
    row = lax.broadcasted_iota(jnp.int32, x.shape, 0)
    return jnp.where(row < n - d, rolled, 0.0)


def _gelu(x):
    return 0.5 * x * (1.0 + jnp.tanh(GELU_K * (x + GELU_C * x * x * x)))


def _gelu_grad(x):
    t = jnp.tanh(GELU_K * (x + GELU_C * x * x * x))
    return 0.5 * (1.0 + t) + 0.5 * x * (1.0 - t * t) * (GELU_K * (1.0 + 3.0 * GELU_C * x * x))


def _sigmoid(x):
    return 1.0 / (1.0 + jnp.exp(-x))


def _conv3(x, w_ref):
    return w_ref[0:1, :] * _shift_dn(x, 2) + w_ref[1:2, :] * _shift_dn(x, 1) + w_ref[2:3, :] * x


def _conv3_bwd_x(dy, w_ref):
    return w_ref[2:3, :] * dy + w_ref[1:2, :] * _shift_up(dy, 1) + w_ref[0:1, :] * _shift_up(dy, 2)


def _conv3_bwd_w(dy, x):
    return jnp.concatenate([
        jnp.sum(dy * _shift_dn(x, 2), axis=0, keepdims=True),
        jnp.sum(dy * _shift_dn(x, 1), axis=0, keepdims=True),
        jnp.sum(dy * x, axis=0, keepdims=True)], axis=0)


def _dot(a, b, dims):
    return lax.dot_general(a.astype(BF16), b.astype(BF16), (dims, ((), ())), preferred_element_type=F32)


_NN = ((1,), (0,))
_NT = ((1,), (1,))
_TN = ((0,), (0,))


def _mm(a, b, mode, *, name, out_dtype=F32, add=None, tm_cap=512, tn_cap=1536, exchanges=()):
    if mode == "tn":
        r, m = a.shape
        n = b.shape[1]
        tm, tn = _pick_tile(m, 256, LANE), _pick_tile(n, tn_cap, LANE)
        in_specs = [pl.BlockSpec((r, tm), lambda i, j: (0, i)), pl.BlockSpec((r, tn), lambda i, j: (0, j))]
        dims = _TN
    elif mode == "nn":
        m, k = a.shape
        n = b.shape[1]
        tm, tn = _pick_tile(m, tm_cap, SUBLANE), _pick_tile(n, tn_cap, LANE)
        in_specs = [pl.BlockSpec((tm, k), lambda i, j: (i, 0)), pl.BlockSpec((k, tn), lambda i, j: (0, j))]
        dims = _NN
    else:
        m, k = a.shape
        n = b.shape[0]
        tm, tn = _pick_tile(m, tm_cap, SUBLANE), _pick_tile(n, tn_cap, LANE)
        in_specs = [pl.BlockSpec((tm, k), lambda i, j: (i, 0)), pl.BlockSpec((tn, k), lambda i, j: (j, 0))]
        dims = _NT
    args = [a, b]
    if add is not None:
        in_specs.append(pl.BlockSpec((tm, tn), lambda i, j: (i, j)))
        args.append(add)

    def body(*refs):
        acc = _dot(refs[0][...], refs[1][...], dims)
        if add is not None:
            acc = acc + refs[2][...]
        refs[-1][...] = acc.astype(out_dtype)

    return _call(body, name=name, grid=(m // tm, n // tn), in_specs=in_specs,
                 out_specs=pl.BlockSpec((tm, tn), lambda i, j: (i, j)),
                 out_shape=jax.ShapeDtypeStruct((m, n), out_dtype), args=args, exchanges=exchanges)


def _rms_fwd(x, g, *, name):
    l, d = x.shape
    tl = _pick_tile(l, 512, SUBLANE)

    def body(x_ref, g_ref, h_ref):
        xv = x_ref[...]
        r = lax.rsqrt(jnp.mean(xv * xv, axis=-1, keepdims=True) + EPS)
        h_ref[...] = (xv * r * g_ref[...]).astype(BF16)

    return pl.pallas_call(
        body, name=name, grid=(l // tl,),
        in_specs=[pl.BlockSpec((tl, d), lambda i: (i, 0)), pl.BlockSpec((1, d), lambda i: (0, 0))],
        out_specs=pl.BlockSpec((tl, d), lambda i: (i, 0)),
        out_shape=jax.ShapeDtypeStruct((l, d), BF16), compiler_params=_PAR)(x, g)


def _rms_bwd_rows(dh, xv, g):
    r = lax.rsqrt(jnp.mean(xv * xv, axis=-1, keepdims=True) + EPS)
    a = dh * g
    m = jnp.mean(a * xv, axis=-1, keepdims=True)
    return r * a - xv * (r * r * r) * m, xv * r


def _rms_bwd(dh, x, g, res, *, name):
    l, d = x.shape
    tl = _pick_tile(l, 512, SUBLANE)

    def body(dh_ref, x_ref, g_ref, res_ref, dx_ref, dxb_ref, dg_ref):
        @pl.when(pl.program_id(0) == 0)
        def _():
            dg_ref[...] = jnp.zeros_like(dg_ref)
        dhv = dh_ref[...]
        dx, xn = _rms_bwd_rows(dhv, x_ref[...], g_ref[...])
        dx = dx + res_ref[...]
        dx_ref[...] = dx
        dxb_ref[...] = dx.astype(BF16)
        dg_ref[...] += jnp.sum(dhv * xn, axis=0, keepdims=True)

    row = pl.BlockSpec((tl, d), lambda i: (i, 0))
    vec = pl.BlockSpec((1, d), lambda i: (0, 0))
    return pl.pallas_call(
        body, name=name, grid=(l // tl,), in_specs=[row, row, vec, row], out_specs=[row, row, vec],
        out_shape=[jax.ShapeDtypeStruct((l, d), F32), jax.ShapeDtypeStruct((l, d), BF16),
                   jax.ShapeDtypeStruct((1, d), F32)],
        compiler_params=_ARB)(dh, x, g, res)


def _loss_head(x, g, tgt, *, name):
    l, d = x.shape
    tl = _pick_tile(l, 512, SUBLANE)

    def body(x_ref, g_ref, t_ref, loss_ref, dx_ref, dxb_ref, dg_ref):
        @pl.when(pl.program_id(0) == 0)
        def _():
            dg_ref[...] = jnp.zeros_like(dg_ref)
            loss_ref[...] = jnp.zeros_like(loss_ref)
        xv = x_ref[...]
        gv = g_ref[...]
        r = lax.rsqrt(jnp.mean(xv * xv, axis=-1, keepdims=True) + EPS)
        err = xv * r * gv - t_ref[...]
        row_loss = jnp.sum(err * err, axis=-1, keepdims=True) * (0.5 / d)
        loss_ref[...] += jnp.sum(row_loss, axis=0, keepdims=True)
        dy = err * (1.0 / d)
        dx, xn = _rms_bwd_rows(dy, xv, gv)
        dx_ref[...] = dx
        dxb_ref[...] = dx.astype(BF16)
        dg_ref[...] += jnp.sum(dy * xn, axis=0, keepdims=True)

    row = pl.BlockSpec((tl, d), lambda i: (i, 0))
    vec = pl.BlockSpec((1, d), lambda i: (0, 0))
    one = pl.BlockSpec((1, 1), lambda i: (0, 0))
    return pl.pallas_call(
        body, name=name, grid=(l // tl,), in_specs=[row, vec, row], out_specs=[one, row, row, vec],
        out_shape=[jax.ShapeDtypeStruct((1, 1), F32), jax.ShapeDtypeStruct((l, d), F32),
                   jax.ShapeDtypeStruct((l, d), BF16), jax.ShapeDtypeStruct((1, d), F32)],
        compiler_params=_ARB)(x, g, tgt)


def _ffn_act(up, cw, cb, *, name, exchanges=()):
    l = up.shape[0]
    nb = D_FF // LANE

    def body(ug_ref, uv_ref, wg_ref, wv_ref, bg_ref, bv_ref, o_ref):
        gc = _conv3(ug_ref[...], wg_ref) + bg_ref[...]
        vc = _conv3(uv_ref[...], wv_ref) + bv_ref[...]
        o_ref[...] = (gc * _sigmoid(gc) * vc).astype(BF16)

    col = lambda off: pl.BlockSpec((l, LANE), lambda j: (0, j + off))
    w3 = lambda off: pl.BlockSpec((CONV_WIDTH, LANE), lambda j: (0, j + off))
    b1 = lambda off: pl.BlockSpec((1, LANE), lambda j: (0, j + off))
    return _call(body, name=name, grid=(nb,), in_specs=[col(0), col(nb), w3(0), w3(nb), b1(0), b1(nb)],
                 out_specs=col(0), out_shape=jax.ShapeDtypeStruct((l, D_FF), BF16),
                 args=[up, up, cw, cw, cb, cb], exchanges=exchanges)


def _ffn_act_bwd(up, dact, cw, cb, *, name, exchanges=()):
    l = up.shape[0]
    nb = D_FF // LANE

    def body(ug_ref, uv_ref, own_ref, da_ref, wg_ref, wv_ref, wo_ref, bg_ref, bv_ref,
             dup_ref, dcw_ref, dcb_ref, dc_scr):
        is_gate = pl.program_id(0) < nb
        gc = _conv3(ug_ref[...], wg_ref) + bg_ref[...]
        sg = _sigmoid(gc)

        @pl.when(is_gate)
        def _():
            vc = _conv3(uv_ref[...], wv_ref) + bv_ref[...]
            dc_scr[...] = da_ref[...] * vc * (sg * (1.0 + gc * (1.0 - sg)))

        @pl.when(jnp.logical_not(is_gate))
        def _():
            dc_scr[...] = da_ref[...] * (gc * sg)

        dc = dc_scr[...]
        dcb_ref[...] = jnp.sum(dc, axis=0, keepdims=True)
        dcw_ref[...] = _conv3_bwd_w(dc, own_ref[...])
        dup_ref[...] = _conv3_bwd_x(dc, wo_ref).astype(BF16)

    half = lambda j: j % nb
    colg = pl.BlockSpec((l, LANE), lambda j: (0, half(j)))
    colv = pl.BlockSpec((l, LANE), lambda j: (0, half(j) + nb))
    colo = pl.BlockSpec((l, LANE), lambda j: (0, j))
    w3g = pl.BlockSpec((CONV_WIDTH, LANE), lambda j: (0, half(j)))
    w3v = pl.BlockSpec((CONV_WIDTH, LANE), lambda j: (0, half(j) + nb))
    w3o = pl.BlockSpec((CONV_WIDTH, LANE), lambda j: (0, j))
    b1g = pl.BlockSpec((1, LANE), lambda j: (0, half(j)))
    b1v = pl.BlockSpec((1, LANE), lambda j: (0, half(j) + nb))
    b1o = pl.BlockSpec((1, LANE), lambda j: (0, j))
    return _call(
        body, name=name, grid=(2 * nb,),
        in_specs=[colg, colv, colo, colg, w3g, w3v, w3o, b1g, b1v],
        out_specs=[colo, w3o, b1o],
        out_shape=[jax.ShapeDtypeStruct((l, 2 * D_FF), BF16), jax.ShapeDtypeStruct((CONV_WIDTH, 2 * D_FF), F32),
                   jax.ShapeDtypeStruct((1, 2 * D_FF), F32)],
        scratch_shapes=[pltpu.VMEM((l, LANE), F32)],
        args=[up, up, up, dact, cw, cw, cw, cb, cb], exchanges=exchanges)


def _sconv_fwd(proj, cw, *, name):
    l = proj.shape[0]
    nb = D_HALF // LANE

    def body(xa_ref, ba_ref, ca_ref, w_ref, o_ref):
        o_ref[...] = (ba_ref[...] * _conv3(ca_ref[...] * xa_ref[...], w_ref)).astype(BF16)

    col = lambda off: pl.BlockSpec((l, LANE), lambda j: (0, j + off))
    return pl.pallas_call(
        body, name=name, grid=(nb,),
        in_specs=[col(0), col(nb), col(2 * nb), pl.BlockSpec((CONV_WIDTH, LANE), lambda j: (0, j))],
        out_specs=col(0), out_shape=jax.ShapeDtypeStruct((l, D_HALF), BF16),
        compiler_params=_PAR)(proj, proj, proj, cw)


def _sconv_bwd(proj, dcat, cw, *, name):
    l = proj.shape[0]
    nb = D_HALF // LANE

    def body(xa_ref, ba_ref, ca_ref, dy_ref, w_ref, dxa_ref, dba_ref, dca_ref, dw_ref):
        xa, ba, ca, dy = xa_ref[...], ba_ref[...], ca_ref[...], dy_ref[...]
        q = ca * xa
        dba_ref[...] = (dy * _conv3(q, w_ref)).astype(BF16)
        dconv = dy * ba
        dw_ref[...] = _conv3_bwd_w(dconv, q)
        dq = _conv3_bwd_x(dconv, w_ref)
        dxa_ref[...] = (dq * ca).astype(BF16)
        dca_ref[...] = (dq * xa).astype(BF16)

    col = lambda off: pl.BlockSpec((l, LANE), lambda j: (0, j + off))
    w3 = pl.BlockSpec((CONV_WIDTH, LANE), lambda j: (0, j))
    piece = jax.ShapeDtypeStruct((l, D_HALF), BF16)
    return pl.pallas_call(
        body, name=name, grid=(nb,),
        in_specs=[col(0), col(nb), col(2 * nb), col(0), w3],
        out_specs=[col(0), col(0), col(0), w3],
        out_shape=[piece, piece, piece, jax.ShapeDtypeStruct((CONV_WIDTH, D_HALF), F32)],
        compiler_params=_PAR)(proj, proj, proj, dcat, cw)


def _s5_prep(log_step, a_re, a_im, b_re, b_im):
    step = jnp.exp(log_step)[:, None]
    mag = jnp.exp(a_re * step)
    lr = mag * jnp.cos(a_im * step)
    li = mag * jnp.sin(a_im * step)
    nr = lr - 1.0
    den = a_re * a_re + a_im * a_im
    qr = (nr * a_re + li * a_im) / den
    qi = (li * a_re - nr * a_im) / den
    br = qr[..., None] * b_re - qi[..., None] * b_im
    bi = qr[..., None] * b_im + qi[..., None] * b_re
    return lr, li, br, bi


def _block_diag(m):
    nb, ng, r, c = m.shape
    eye = jnp.eye(ng, dtype=m.dtype)
    return jnp.einsum("bgrc,gh->bgrhc", m, eye).reshape(nb, ng * r, ng * c)


def _block_diag_extract(w, r, c):
    nb = w.shape[0]
    ng = w.shape[1] // r
    w5 = w.reshape(nb, ng, r, ng, c)
    return jnp.einsum("bgrhc,gh->bgrc", w5, jnp.eye(ng, dtype=w.dtype))


def _s5_mats(br, bi, c_re, c_im):
    g8 = N_SSM_GROUPS // S5_LANE_BLOCKS
    to_blk = lambda m: m.reshape(S5_LANE_BLOCKS, g8, m.shape[1], m.shape[2])
    wb = jnp.concatenate([_block_diag(to_blk(jnp.swapaxes(br, 1, 2))),
                          _block_diag(to_blk(jnp.swapaxes(bi, 1, 2)))], axis=2)
    wc = jnp.concatenate([_block_diag(to_blk(jnp.swapaxes(c_re, 1, 2))),
                          _block_diag(to_blk(jnp.swapaxes(-c_im, 1, 2)))], axis=1)
    return wb, wc


def _s5_mats_bwd(dwb, dwc):
    g, p, h = N_SSM_GROUPS, SSM_STATE, SSM_GROUP
    half = S5_STATE_LANES
    dbr = jnp.swapaxes(_block_diag_extract(dwb[:, :, :half], h, p).reshape(g, h, p), 1, 2)
    dbi = jnp.swapaxes(_block_diag_extract(dwb[:, :, half:], h, p).reshape(g, h, p), 1, 2)
    dcr = jnp.swapaxes(_block_diag_extract(dwc[:, :half, :], p, h).reshape(g, p, h), 1, 2)
    dci = -jnp.swapaxes(_block_diag_extract(dwc[:, half:, :], p, h).reshape(g, p, h), 1, 2)
    return dbr, dbi, dcr, dci


def _s5_scan_consts(log_step, a_re, a_im, reverse):
    step = jnp.exp(log_step)[:, None]
    xr = (a_re * step).reshape(S5_LANE_BLOCKS, 1, S5_STATE_LANES)
    xi = (a_im * step).reshape(S5_LANE_BLOCKS, 1, S5_STATE_LANES)
    if reverse:
        xi = -xi
    row = jnp.arange(SUBLANE, dtype=F32).reshape(1, SUBLANE, 1)

    def power(n):
        mag = jnp.exp(n * xr)
        return jnp.concatenate([mag * jnp.cos(n * xi), mag * jnp.sin(n * xi)], axis=-1)

    kinds = []
    for d in (1, 2, 4):
        keep = (row <= SUBLANE - 1 - d) if reverse else (row >= d)
        kinds.append(jnp.where(keep, power(jnp.full_like(row, float(d))), 0.0))
    kinds.append(power((SUBLANE - row) if reverse else (row + 1.0)))
    return jnp.stack(kinds, axis=1)


def _scan_rows(s_ref, sc_ref, carry_ref, n_rows, reverse):
    n_grp = n_rows // SUBLANE
    n_col = S5_STATE_LANES // LANE
    half = S5_STATE_LANES

    def step(i, carry):
        grp = (n_grp - 1 - i) if reverse else i
        r0 = pl.multiple_of(grp * SUBLANE, SUBLANE)
        out = []
        for cb in range(n_col):
            lo, hi = cb * LANE, half + cb * LANE
            re = s_ref[pl.ds(r0, SUBLANE), lo:lo + LANE]
            im = s_ref[pl.ds(r0, SUBLANE), hi:hi + LANE]
            for k, d in enumerate((1, 2, 4)):
                sh = (SUBLANE - d) if reverse else d
                rr, ri = pltpu.roll(re, sh, 0), pltpu.roll(im, sh, 0)
                ar, ai = sc_ref[k, :, lo:lo + LANE], sc_ref[k, :, hi:hi + LANE]
                re, im = re + (ar * rr - ai * ri), im + (ar * ri + ai * rr)
            pr, pi = sc_ref[3, :, lo:lo + LANE], sc_ref[3, :, hi:hi + LANE]
            cr, ci = carry[2 * cb], carry[2 * cb + 1]
            re, im = re + (pr * cr - pi * ci), im + (pr * ci + pi * cr)
            s_ref[pl.ds(r0, SUBLANE), lo:lo + LANE] = re
            s_ref[pl.ds(r0, SUBLANE), hi:hi + LANE] = im
            edge = 0 if reverse else SUBLANE - 1
            out.append(jnp.broadcast_to(re[edge:edge + 1, :], (SUBLANE, LANE)))
            out.append(jnp.broadcast_to(im[edge:edge + 1, :], (SUBLANE, LANE)))
        return tuple(out)

    init = []
    for cb in range(n_col):
        init.append(carry_ref[:, cb * LANE:(cb + 1) * LANE])
        init.append(carry_ref[:, half + cb * LANE:half + (cb + 1) * LANE])
    fin = lax.fori_loop(0, n_grp, step, tuple(init))
    for cb in range(n_col):
        carry_ref[:, cb * LANE:(cb + 1) * LANE] = fin[2 * cb]
        carry_ref[:, half + cb * LANE:half + (cb + 1) * LANE] = fin[2 * cb + 1]


def _s5_fwd(proj, wb, wc, d_skip, sc, *, name):
    l = proj.shape[0]
    tt = _pick_tile(l, S5_TIME_CHUNK, SUBLANE)
    u_off = (proj.shape[1] - D_HALF) // LANE
    w2 = 2 * S5_STATE_LANES

    def body(u_ref, wb_ref, wc_ref, d_ref, sc_ref, s_ref, y_ref, carry_ref):
        @pl.when(pl.program_id(1) == 0)
        def _():
            carry_ref[...] = jnp.zeros_like(carry_ref)
        u = u_ref[...]
        s_ref[...] = _dot(u, wb_ref[0], _NN)
        _scan_rows(s_ref, sc_ref.at[0], carry_ref, tt, False)
        y_ref[...] = _dot(s_ref[...], wc_ref[0], _NN) + d_ref[...] * u

    return pl.pallas_call(
        body, name=name, grid=(S5_LANE_BLOCKS, l // tt),
        in_specs=[pl.BlockSpec((tt, LANE), lambda b, t: (t, b + u_off)),
                  pl.BlockSpec((1, LANE, w2), lambda b, t: (b, 0, 0)),
                  pl.BlockSpec((1, w2, LANE), lambda b, t: (b, 0, 0)),
                  pl.BlockSpec((1, LANE), lambda b, t: (0, b)),
                  pl.BlockSpec((1, 4, SUBLANE, w2), lambda b, t: (b, 0, 0, 0))],
        out_specs=[pl.BlockSpec((tt, w2), lambda b, t: (t, b)), pl.BlockSpec((tt, LANE), lambda b, t: (t, b))],
        out_shape=[jax.ShapeDtypeStruct((l, S5_LANE_BLOCKS * w2), F32), jax.ShapeDtypeStruct((l, D_HALF), F32)],
        scratch_shapes=[pltpu.VMEM((SUBLANE, w2), F32)],
        compiler_params=_ARB2)(proj, wb, wc, d_skip, sc)


def _s5_bwd(proj, dy, states, wb, wc, d_skip, sc_rev, *, name):
    l = proj.shape[0]
    tt = _pick_tile(l, S5_TIME_CHUNK, SUBLANE)
    nt = l // tt
    u_off = (proj.shape[1] - D_HALF) // LANE
    w2 = 2 * S5_STATE_LANES
    half = S5_STATE_LANES
    grp_per_chunk = tt // SUBLANE

    def body(u_ref, dy_ref, s_ref, halo_ref, wb_ref, wc_ref, d_ref, sc_ref,
             du_ref, dwb_ref, dwc_ref, dlam_ref, dd_ref, g_scr, carry_ref):
        t = pl.program_id(1)

        @pl.when(t == 0)
        def _():
            carry_ref[...] = jnp.zeros_like(carry_ref)
            dwb_ref[...] = jnp.zeros_like(dwb_ref)
            dwc_ref[...] = jnp.zeros_like(dwc_ref)
            dlam_ref[...] = jnp.zeros_like(dlam_ref)
            dd_ref[...] = jnp.zeros_like(dd_ref)

        u = u_ref[...]
        dyv = dy_ref[...]
        g_scr[...] = _dot(dyv, wc_ref[0], _NT)
        _scan_rows(g_scr, sc_ref.at[0], carry_ref, tt, True)
        gv = g_scr[...]
        du_ref[...] = (_dot(gv, wb_ref[0], _NT) + d_ref[...] * dyv).astype(BF16)
        dwb_ref[0] += _dot(u, gv, _TN)
        sv = s_ref[...]
        dwc_ref[0] += _dot(sv, dyv, _TN)
        dd_ref[...] += jnp.sum(dyv * u, axis=0, keepdims=True)
        first_chunk = t == nt - 1
        halo = jnp.where(first_chunk, 0.0, halo_ref[SUBLANE - 1:SUBLANE, :])
        row = lax.broadcasted_iota(jnp.int32, sv.shape, 0)
        sp = jnp.where(row == 0, jnp.broadcast_to(halo, sv.shape), pltpu.roll(sv, 1, 0))
        gr, gi = gv[:, :half], gv[:, half:]
        sr, si = sp[:, :half], sp[:, half:]
        dlr = jnp.sum(gr * sr + gi * si, axis=0, keepdims=True)
        dli = jnp.sum(gi * sr - gr * si, axis=0, keepdims=True)
        dlam_ref[0] += jnp.concatenate([dlr, dli], axis=1)

    rev = lambda t: nt - 1 - t
    return pl.pallas_call(
        body, name=name, grid=(S5_LANE_BLOCKS, nt),
        in_specs=[pl.BlockSpec((tt, LANE), lambda b, t: (rev(t), b + u_off)),
                  pl.BlockSpec((tt, LANE), lambda b, t: (rev(t), b)),
                  pl.BlockSpec((tt, w2), lambda b, t: (rev(t), b)),
                  pl.BlockSpec((SUBLANE, w2), lambda b, t: (jnp.maximum(rev(t) * grp_per_chunk - 1, 0), b)),
                  pl.BlockSpec((1, LANE, w2), lambda b, t: (b, 0, 0)),
                  pl.BlockSpec((1, w2, LANE), lambda b, t: (b, 0, 0)),
                  pl.BlockSpec((1, LANE), lambda b, t: (0, b)),
                  pl.BlockSpec((1, 4, SUBLANE, w2), lambda b, t: (b, 0, 0, 0))],
        out_specs=[pl.BlockSpec((tt, LANE), lambda b, t: (rev(t), b)),
                   pl.BlockSpec((1, LANE, w2), lambda b, t: (b, 0, 0)),
                   pl.BlockSpec((1, w2, LANE), lambda b, t: (b, 0, 0)),
                   pl.BlockSpec((1, 1, w2), lambda b, t: (b, 0, 0)),
                   pl.BlockSpec((1, LANE), lambda b, t: (0, b))],
        out_shape=[jax.ShapeDtypeStruct((l, D_HALF), BF16),
                   jax.ShapeDtypeStruct((S5_LANE_BLOCKS, LANE, w2), F32),
                   jax.ShapeDtypeStruct((S5_LANE_BLOCKS, w2, LANE), F32),
                   jax.ShapeDtypeStruct((S5_LANE_BLOCKS, 1, w2), F32),
                   jax.ShapeDtypeStruct((1, D_HALF), F32)],
        scratch_shapes=[pltpu.VMEM((tt, w2), F32), pltpu.VMEM((SUBLANE, w2), F32)],
        compiler_params=_ARB2)(proj, dy, states, states, wb, wc, d_skip, sc_rev)


def _glu_fwd(ypre, wg, bg, *, name):
    l, d = ypre.shape
    tl = _pick_tile(l, 512, SUBLANE)

    def body(y_ref, w_ref, b_ref, o_ref):
        yg = _gelu(y_ref[...])
        o_ref[...] = (yg * _sigmoid(_dot(yg, w_ref[...], _NN) + b_ref[...])).astype(BF16)

    row = pl.BlockSpec((tl, d), lambda i: (i, 0))
    return pl.pallas_call(
        body, name=name, grid=(l // tl,),
        in_specs=[row, pl.BlockSpec((d, d), lambda i: (0, 0)), pl.BlockSpec((1, d), lambda i: (0, 0))],
        out_specs=row, out_shape=jax.ShapeDtypeStruct((l, d), BF16), compiler_params=_PAR)(ypre, wg, bg)


def _glu_bwd(dcat, ypre, wg, bg, *, name):
    l, d = ypre.shape
    tl = _pick_tile(l, 512, SUBLANE)

    def body(dy_ref, y_ref, w_ref, b_ref, dyp_ref, dw_ref, db_ref):
        @pl.when(pl.program_id(0) == 0)
        def _():
            dw_ref[...] = jnp.zeros_like(dw_ref)
            db_ref[...] = jnp.zeros_like(db_ref)
        yp = y_ref[...]
        dyb = dy_ref[...]
        yg = _gelu(yp)
        sg = _sigmoid(_dot(yg, w_ref[...], _NN) + b_ref[...])
        dz = dyb * yg * sg * (1.0 - sg)
        dyg = dyb * sg + _dot(dz, w_ref[...], _NT)
        dyp_ref[...] = dyg * _gelu_grad(yp)
        dw_ref[...] += _dot(yg, dz, _TN)
        db_ref[...] += jnp.sum(dz, axis=0, keepdims=True)

    row = pl.BlockSpec((tl, d), lambda i: (i, 0))
    mat = pl.BlockSpec((d, d), lambda i: (0, 0))
    vec = pl.BlockSpec((1, d), lambda i: (0, 0))
    return pl.pallas_call(
        body, name=name, grid=(l // tl,),
        in_specs=[pl.BlockSpec((tl, d), lambda i: (i, 1)), row, mat, vec], out_specs=[row, mat, vec],
        out_shape=[jax.ShapeDtypeStruct((l, d), F32), jax.ShapeDtypeStruct((d, d), F32),
                   jax.ShapeDtypeStruct((1, d), F32)],
        compiler_params=_ARB)(dcat, ypre, wg, bg)


def _window_sum(x, w, trailing):
    s, d = x, 1
    while d < w:
        s = s + (_shift_dn(s, d) if trailing else _shift_up(s, d))
        d *= 2
    return s


def _window_count(shape, w):
    row = lax.broadcasted_iota(jnp.int32, shape, 0)
    return jnp.minimum(row + 1, w).astype(F32)


def _pool_fwd(proj, pw, scale, *, name):
    l = proj.shape[0]
    ng = len(POOL_WINDOWS)

    def body(z_ref, w_ref, sc_ref, o_ref):
        z = z_ref[...]
        for k, w in enumerate(POOL_WINDOWS):
            @pl.when(pl.program_id(0) == k)
            def _():
                pooled = _window_sum(z, w, True) / _window_count(z.shape, w) - z
                o_ref[...] = (_dot(pooled, w_ref[0], _NN) * sc_ref[...]).astype(BF16)

    col = pl.BlockSpec((l, LANE), lambda g: (0, g))
    return pl.pallas_call(
        body, name=name, grid=(ng,),
        in_specs=[col, pl.BlockSpec((1, LANE, LANE), lambda g: (g, 0, 0)), pl.BlockSpec((1, LANE), lambda g: (0, g))],
        out_specs=col, out_shape=jax.ShapeDtypeStruct((l, D_HALF), BF16), compiler_params=_PAR)(proj, pw, scale)


def _pool_bwd(proj, dcat, pw, scale, *, name):
    l = proj.shape[0]
    ng = len(POOL_WINDOWS)

    def body(z_ref, dy_ref, w_ref, sc_ref, dz_ref, dw_ref, dsc_ref):
        z = z_ref[...]
        dy = dy_ref[...]
        for k, w in enumerate(POOL_WINDOWS):
            @pl.when(pl.program_id(0) == k)
            def _():
                cnt = _window_count(z.shape, w)
                pooled = _window_sum(z, w, True) / cnt - z
                ypre = _dot(pooled, w_ref[0], _NN)
                dsc_ref[...] = jnp.sum(dy * ypre, axis=0, keepdims=True)
                dyp = dy * sc_ref[...]
                dw_ref[0] = _dot(pooled, dyp, _TN)
                dpool = _dot(dyp, w_ref[0], _NT)
                dz_ref[...] = (_window_sum(dpool / cnt, w, False) - dpool).astype(BF16)

    col = pl.BlockSpec((l, LANE), lambda g: (0, g))
    mat = pl.BlockSpec((1, LANE, LANE), lambda g: (g, 0, 0))
    vec = pl.BlockSpec((1, LANE), lambda g: (0, g))
    return pl.pallas_call(
        body, name=name, grid=(ng,), in_specs=[col, col, mat, vec], out_specs=[col, mat, vec],
        out_shape=[jax.ShapeDtypeStruct((l, D_HALF), BF16), jax.ShapeDtypeStruct((ng, LANE, LANE), F32),
                   jax.ShapeDtypeStruct((1, D_HALF), F32)],
        compiler_params=_PAR)(proj, dcat, pw, scale)


def _tril_mask():
    r = lax.broadcasted_iota(jnp.int32, (CHUNK, CHUNK), 0)
    c = lax.broadcasted_iota(jnp.int32, (CHUNK, CHUNK), 1)
    return r >= c


def _sgu_fwd(proj, ng, sw, sb_t, *, name):
    l = proj.shape[0]
    tl = _pick_tile(l, 512, CHUNK)

    def body(su_ref, sv_ref, g_ref, w_ref, b_ref, o_ref):
        su = _gelu(su_ref[...])
        sv = _gelu(sv_ref[...])
        r = lax.rsqrt(jnp.mean(sv * sv, axis=-1, keepdims=True) + EPS)
        v = sv * r * g_ref[...]
        mask = _tril_mask()
        for h in range(SGU_HEADS):
            wm = jnp.where(mask, w_ref[h], 0.0)
            cs = slice(h * LANE, (h + 1) * LANE)
            for n in range(tl // CHUNK):
                rs = slice(n * CHUNK, (n + 1) * CHUNK)
                mixed = _dot(wm, v[rs, cs], _NN) + b_ref[:, h:h + 1]
                o_ref[rs, cs] = (su[rs, cs] * mixed).astype(BF16)

    blk = lambda c: pl.BlockSpec((tl, D_HALF), lambda i: (i, c))
    return pl.pallas_call(
        body, name=name, grid=(l // tl,),
        in_specs=[blk(1), blk(2), pl.BlockSpec((1, D_HALF), lambda i: (0, 0)),
                  pl.BlockSpec((SGU_HEADS, CHUNK, CHUNK), lambda i: (0, 0, 0)),
                  pl.BlockSpec((CHUNK, SGU_HEADS), lambda i: (0, 0))],
        out_specs=blk(0), out_shape=jax.ShapeDtypeStruct((l, D_HALF), BF16),
        compiler_params=_PAR)(proj, proj, ng, sw, sb_t)


def _sgu_bwd(proj, dcat, ng, sw, sb_t, *, name):
    l = proj.shape[0]
    tl = _pick_tile(l, 512, CHUNK)

    def body(su_ref, sv_ref, dy_ref, g_ref, w_ref, b_ref, dsu_ref, dsv_ref, dw_ref, dbm_ref, dng_ref, dv_scr):
        @pl.when(pl.program_id(0) == 0)
        def _():
            dw_ref[...] = jnp.zeros_like(dw_ref)
            dbm_ref[...] = jnp.zeros_like(dbm_ref)
            dng_ref[...] = jnp.zeros_like(dng_ref)
        su_pre = su_ref[...]
        sv_pre = sv_ref[...]
        su = _gelu(su_pre)
        sv = _gelu(sv_pre)
        gsu = _gelu_grad(su_pre)
        gv = g_ref[...]
        r = lax.rsqrt(jnp.mean(sv * sv, axis=-1, keepdims=True) + EPS)
        v = sv * r * gv
        dy = dy_ref[...]
        mask = _tril_mask()
        for h in range(SGU_HEADS):
            wm = jnp.where(mask, w_ref[h], 0.0)
            cs = slice(h * LANE, (h + 1) * LANE)
            dw_acc = jnp.zeros((CHUNK, CHUNK), F32)
            db_acc = jnp.zeros((CHUNK, LANE), F32)
            for n in range(tl // CHUNK):
                rs = slice(n * CHUNK, (n + 1) * CHUNK)
                vb = v[rs, cs]
                mixed = _dot(wm, vb, _NN) + b_ref[:, h:h + 1]
                dyb = dy[rs, cs]
                dsu_ref[rs, cs] = (dyb * mixed * gsu[rs, cs]).astype(BF16)
                dmix = dyb * su[rs, cs]
                db_acc = db_acc + dmix
                dw_acc = dw_acc + _dot(dmix, vb, _NT)
                dv_scr[rs, cs] = _dot(wm, dmix, _TN)
            dw_ref[h] += jnp.where(mask, dw_acc, 0.0)
            dbm_ref[h] += db_acc
        dv = dv_scr[...]
        a = dv * gv
        m = jnp.mean(a * sv, axis=-1, keepdims=True)
        dsv = r * a - sv * (r * r * r) * m
        dng_ref[...] += jnp.sum(dv * sv * r, axis=0, keepdims=True)
        dsv_ref[...] = (dsv * _gelu_grad(sv_pre)).astype(BF16)

    blk = lambda c: pl.BlockSpec((tl, D_HALF), lambda i: (i, c))
    mats = pl.BlockSpec((SGU_HEADS, CHUNK, CHUNK), lambda i: (0, 0, 0))
    vec = pl.BlockSpec((1, D_HALF), lambda i: (0, 0))
    piece = jax.ShapeDtypeStruct((l, D_HALF), BF16)
    mshape = jax.ShapeDtypeStruct((SGU_HEADS, CHUNK, CHUNK), F32)
    return pl.pallas_call(
        body, name=name, grid=(l // tl,),
        in_specs=[blk(1), blk(2), blk(1), vec, mats, pl.BlockSpec((CHUNK, SGU_HEADS), lambda i: (0, 0))],
        out_specs=[blk(0), blk(0), mats, mats, vec],
        out_shape=[piece, piece, mshape, mshape, jax.ShapeDtypeStruct((1, D_HALF), F32)],
        scratch_shapes=[pltpu.VMEM((tl, D_HALF), F32)],
        compiler_params=_ARB)(proj, proj, dcat, ng, sw, sb_t)


def _s5_layer_params(w, j):
    prep_args = (w["ssm_log_step"][j], w["ssm_a_re"][j], w["ssm_a_im"][j], w["ssm_b_re"][j], w["ssm_b_im"][j])
    (lr, li, br, bi), prep_vjp = jax.vjp(_s5_prep, *prep_args)
    wb, wc = _s5_mats(br, bi, w["ssm_c_re"][j], w["ssm_c_im"][j])
    return dict(
        wb=wb.astype(BF16), wc=wc.astype(BF16), d=w["ssm_d"][j][None, :],
        sc=_s5_scan_consts(*prep_args[:3], reverse=False), sc_rev=_s5_scan_consts(*prep_args[:3], reverse=True),
        prep_vjp=prep_vjp)


def _layer_weights(i):
    j = i // 2
    mixer = [("even_w_in", j), ("even_w_out", j), ("ssm_glu_w", j)] if i % 2 == 0 else [("odd_w_in", j),
                                                                                         ("odd_w_out", j)]
    return dict(w_in=mixer[0], w_out=mixer[1], glu=mixer[2:], up=("ffn_w_up", i), down=("ffn_w_down", i))


class _LocalWeights:
    def __init__(self, w):
        self.w, self.grads = w, {}

    def carried_by(self, stage, i):
        return []

    def delivered(self, stage, i, outs):
        pass

    def weight(self, key):
        return self.w[key[0]][key[1]]

    def grad(self, key, dw):
        self.grads[key] = dw


class _ShardedWeights:
    def __init__(self, shards):
        self.shards = shards
        self.full, self.pending, self.scattered = {}, {}, {}
        first = _layer_weights(0)
        keys = [first["w_in"], first["w_out"], *first["glu"], first["up"]]
        self._take(keys, _exchange_only([self._gather(k) for k in keys], name="ag_first_layer"))

    def _gather(self, key):
        return _Gather(self.shards[key[0]][key[1]])

    def _take(self, keys, outs):
        for key, got in zip(keys, outs):
            if BIG[key[0]] == 2:
                self.full[key] = jnp.swapaxes(got, 0, 1).reshape(got.shape[1], -1)
            else:
                self.full[key] = got.reshape(-1, got.shape[2])

    def _plan(self, stage, i):
        nxt = _layer_weights(i + 1) if i + 1 < DEPTH else None
        if stage == "mm_up":
            return [_layer_weights(i)["down"]] + ([nxt["w_in"], nxt["w_out"], *nxt["glu"]] if nxt else []), []
        if stage == "ffn_act":
            return ([nxt["up"]] if nxt else []), []
        if stage == "ffn_act_bwd":
            return [], [_layer_weights(i)["down"]] + ([nxt["up"]] if nxt else [])
        if stage == "mm_up_dx":
            return [], ([nxt["w_in"], *nxt["glu"]] if nxt else [])
        if stage == "mm_up_dw":
            return [], ([nxt["w_out"]] if nxt else [])
        raise ValueError(stage)

    def carried_by(self, stage, i):
        gather, scatter = self._plan(stage, i)
        return [self._gather(k) for k in gather] + [_Scatter(self.pending.pop(k)) for k in scatter]

    def delivered(self, stage, i, outs):
        gather, scatter = self._plan(stage, i)
        self._take(gather, outs[:len(gather)])
        self.scattered.update(zip(scatter, outs[len(gather):]))

    def weight(self, key):
        return self.full[key]

    def grad(self, key, dw):
        self.pending[key] = _to_dest_major(dw, BIG[key[0]] - 1).astype(BF16)

    def finish(self):
        keys = list(self.pending)
        outs = _exchange_only([_Scatter(self.pending.pop(k)) for k in keys], name="rs_last_layer")
        self.scattered.update(zip(keys, outs))


def _device_step(x, tgt, w, comm):
    saved = []
    for i in range(DEPTH):
        j = i // 2
        lw = _layer_weights(i)
        h = _rms_fwd(x, w["norm_mix_g"][i:i + 1], name="rms_fwd")
        if i % 2 == 0:
            proj = _mm(h, comm.weight(lw["w_in"]), "nn", name="mm_even_in")
            ya = _sconv_fwd(proj, w["even_conv_w"][j], name="sconv_fwd")
            sp = _s5_layer_params(w, j)
            states, ypre = _s5_fwd(proj, sp["wb"], sp["wc"], sp["d"], sp["sc"], name="s5_fwd")
            yb = _glu_fwd(ypre, comm.weight(lw["glu"][0]), w["ssm_glu_b"][j][None, :], name="glu_fwd")
            cat = jnp.concatenate([ya, yb], axis=1)
            mix = (states, ypre, sp)
        else:
            proj = _mm(h, comm.weight(lw["w_in"]), "nn", name="mm_odd_in")
            yc = _pool_fwd(proj, w["pool_w"][j], w["pool_scale"][j][None, :], name="pool_fwd")
            sb_t = jnp.transpose(w["sgu_b"][j])
            yd = _sgu_fwd(proj, w["sgu_norm_g"][j][None, :], w["sgu_w"][j], sb_t, name="sgu_fwd")
            cat = jnp.concatenate([yc, yd], axis=1)
            mix = (sb_t,)
        x1 = _mm(cat, comm.weight(lw["w_out"]), "nn", add=x, name="mm_mix_out")
        h2 = _rms_fwd(x1, w["norm_ffn_g"][i:i + 1], name="rms_fwd")
        up = _carry(comm, "mm_up", i, _mm, h2, comm.weight(lw["up"]), "nn", name="mm_up")
        act = _carry(comm, "ffn_act", i, _ffn_act, up, w["ffn_conv_w"][i], w["ffn_conv_b"][i:i + 1], name="ffn_act")
        x2 = _mm(act, comm.weight(lw["down"]), "nn", add=x1, name="mm_down")
        saved.append((x, h, proj, cat, x1, h2, up, act, mix))
        x = x2

    loss, dx, dxb, dgf = _loss_head(x, w["norm_final_g"][None, :], tgt, name="loss_head")
    per_layer = {}

    def put(name, idx, val):
        per_layer.setdefault(name, {})[idx] = val

    for i in reversed(range(DEPTH)):
        j = i // 2
        lw = _layer_weights(i)
        x0, h, proj, cat, x1, h2, up, act, mix = saved[i]
        dact = _mm(dxb, comm.weight(lw["down"]), "nt", name="mm_down_dx")
        comm.grad(lw["down"], _mm(act, dxb, "tn", name="mm_down_dw"))
        dup, dcw, dcb = _carry(comm, "ffn_act_bwd", i, _ffn_act_bwd, up, dact, w["ffn_conv_w"][i],
                               w["ffn_conv_b"][i:i + 1], name="ffn_act_bwd")
        put("ffn_conv_w", i, dcw)
        put("ffn_conv_b", i, dcb[0])
        dh2 = _carry(comm, "mm_up_dx", i, _mm, dup, comm.weight(lw["up"]), "nt", tn_cap=512, name="mm_up_dx")
        comm.grad(lw["up"], _carry(comm, "mm_up_dw", i, _mm, h2, dup, "tn", name="mm_up_dw"))
        dx1, dx1b, dg2 = _rms_bwd(dh2, x1, w["norm_ffn_g"][i:i + 1], dx, name="rms_bwd")
        put("norm_ffn_g", i, dg2[0])
        dcat = _mm(dx1b, comm.weight(lw["w_out"]), "nt", name="mm_mix_out_dx")
        comm.grad(lw["w_out"], _mm(cat, dx1b, "tn", name="mm_mix_out_dw"))
        if i % 2 == 0:
            states, ypre, sp = mix
            dxa, dba, dca, dcw_a = _sconv_bwd(proj, dcat, w["even_conv_w"][j], name="sconv_bwd")
            put("even_conv_w", j, dcw_a)
            dypre, dwg, dbg = _glu_bwd(dcat, ypre, comm.weight(lw["glu"][0]), w["ssm_glu_b"][j][None, :],
                                       name="glu_bwd")
            comm.grad(lw["glu"][0], dwg)
            put("ssm_glu_b", j, dbg[0])
            du, dwb, dwc, dlam, dd = _s5_bwd(proj, dypre, states, sp["wb"], sp["wc"], sp["d"], sp["sc_rev"],
                                             name="s5_bwd")
            dbr, dbi, dcr, dci = _s5_mats_bwd(dwb, dwc)
            dlr = dlam[:, 0, :S5_STATE_LANES].reshape(N_SSM_GROUPS, SSM_STATE)
            dli = dlam[:, 0, S5_STATE_LANES:].reshape(N_SSM_GROUPS, SSM_STATE)
            dls, dar, dai, db_re, db_im = sp["prep_vjp"]((dlr, dli, dbr, dbi))
            for nm, val in (("ssm_log_step", dls), ("ssm_a_re", dar), ("ssm_a_im", dai), ("ssm_b_re", db_re),
                            ("ssm_b_im", db_im), ("ssm_c_re", dcr), ("ssm_c_im", dci), ("ssm_d", dd[0])):
                put(nm, j, val)
            dproj = jnp.concatenate([dxa, dba, dca, du], axis=1)
            dh = _mm(dproj, comm.weight(lw["w_in"]), "nt", name="mm_even_in_dx")
            comm.grad(lw["w_in"], _mm(h, dproj, "tn", name="mm_even_in_dw"))
        else:
            (sb_t,) = mix
            dz, dpw, dps = _pool_bwd(proj, dcat, w["pool_w"][j], w["pool_scale"][j][None, :], name="pool_bwd")
            put("pool_w", j, dpw)
            put("pool_scale", j, dps[0])
            dsu, dsv, dsw, dbm, dng = _sgu_bwd(proj, dcat, w["sgu_norm_g"][j][None, :], w["sgu_w"][j], sb_t,
                                               name="sgu_bwd")
            put("sgu_w", j, dsw)
            put("sgu_b", j, jnp.sum(dbm, axis=-1))
            put("sgu_norm_g", j, dng[0])
            dproj = jnp.concatenate([dz, dsu, dsv], axis=1)
            dh = _mm(dproj, comm.weight(lw["w_in"]), "nt", name="mm_odd_in_dx")
            comm.grad(lw["w_in"], _mm(h, dproj, "tn", name="mm_odd_in_dw"))
        dx, dxb, dg1 = _rms_bwd(dh, x0, w["norm_mix_g"][i:i + 1], dx1, name="rms_bwd")
        put("norm_mix_g", i, dg1[0])

    grads = {nm: jnp.stack([vals[k] for k in sorted(vals)]) for nm, vals in per_layer.items()}
    grads["norm_final_g"] = dgf[0]
    return loss, dx, grads


def _carry(comm, stage, i, fn, *args, **kwargs):
    exchanges = comm.carried_by(stage, i)
    if not exchanges:
        return fn(*args, **kwargs)
    out, moved = fn(*args, exchanges=exchanges, **kwargs)
    comm.delivered(stage, i, moved)
    return out


def _sum_parts(parts, *, name):
    g, r, c = parts.shape
    tr = _pick_tile(r, max(16, EXCHANGE_BLOCK_ELEMS // c), 16)

    def body(p_ref, o_ref):
        acc = p_ref[0].astype(F32)
        for k in range(1, g):
            acc = acc + p_ref[k].astype(F32)
        o_ref[...] = acc

    return pl.pallas_call(
        body, name=name, grid=(r // tr,), in_specs=[pl.BlockSpec((g, tr, c), lambda i: (0, i, 0))],
        out_specs=pl.BlockSpec((tr, c), lambda i: (i, 0)), out_shape=jax.ShapeDtypeStruct((r, c), F32),
        compiler_params=_PAR)(parts)


def _adamw(w, m, v, g_parts, *, name):
    r, c = w.shape
    g = g_parts.shape[0]
    tc = _pick_tile(c, 8192, LANE)
    tr = _pick_tile(r, max(16, (1 << 18) // tc), 16)
    c1 = 1.0 - ADAM_B1 ** ADAM_STEP
    c2 = 1.0 - ADAM_B2 ** ADAM_STEP

    def body(w_ref, m_ref, v_ref, g_ref, go_ref, d_ref, mo_ref, vo_ref):
        grad = g_ref[0].astype(F32)
        for k in range(1, g):
            grad = grad + g_ref[k].astype(F32)
        m_new = ADAM_B1 * m_ref[...] + (1.0 - ADAM_B1) * grad
        v_new = ADAM_B2 * v_ref[...] + (1.0 - ADAM_B2) * (grad * grad)
        go_ref[...] = grad
        mo_ref[...] = m_new
        vo_ref[...] = v_new
        d_ref[...] = -ADAM_LR * ((m_new / c1) / (jnp.sqrt(v_new / c2) + ADAM_EPS) + ADAM_WD * w_ref[...])

    blk = pl.BlockSpec((tr, tc), lambda i, j: (i, j))
    out = jax.ShapeDtypeStruct((r, c), F32)
    return pl.pallas_call(
        body, name=name, grid=(r // tr, c // tc),
        in_specs=[blk, blk, blk, pl.BlockSpec((g, tr, tc), lambda i, j: (0, i, j))],
        out_specs=[blk, blk, blk, blk], out_shape=[out, out, out, out], compiler_params=_PAR2)(w, m, v, g_parts)


WEIGHT_NAMES = ['norm_mix_g', 'even_w_in', 'even_conv_w', 'ssm_log_step', 'ssm_a_re', 'ssm_a_im', 'ssm_b_re',
                'ssm_b_im', 'ssm_c_re', 'ssm_c_im', 'ssm_d', 'ssm_glu_w', 'ssm_glu_b', 'even_w_out', 'odd_w_in',
                'pool_w', 'pool_scale', 'sgu_norm_g', 'sgu_w', 'sgu_b', 'odd_w_out', 'norm_ffn_g', 'ffn_w_up',
                'ffn_conv_w', 'ffn_conv_b', 'ffn_w_down', 'norm_final_g']
BIG = {'even_w_in': 2, 'ssm_glu_w': 1, 'even_w_out': 1, 'odd_w_in': 2, 'odd_w_out': 1, 'ffn_w_up': 2,
       'ffn_w_down': 1}
SMALL = {'even_conv_w': 2, 'pool_scale': 1, 'sgu_norm_g': 1, 'ffn_conv_w': 2}
BIG_ROWS = 512
SMALL_ROWS = 16


def _pad_to(n, q):
    return -(-n // q) * q


def _pack(arrays, dtype, rows, lead=()):
    flat = [a.reshape(lead + (-1,)).astype(dtype) for a in arrays]
    n = sum(f.shape[-1] for f in flat)
    pad = _pad_to(n, rows * LANE) - n
    if pad:
        flat.append(jnp.zeros(lead + (pad,), dtype))
    return jnp.concatenate(flat, axis=-1).reshape(lead + (rows, -1))


def _unpack(buf, shapes, lead=()):
    flat = buf.reshape(lead + (-1,))
    out, off = [], 0
    for shp in shapes:
        n = math.prod(shp)
        out.append(flat[..., off:off + n].reshape(lead + tuple(shp)))
        off += n
    return out


def _to_dest_major(full, axis):
    shp = full.shape
    split = full.reshape(shp[:axis] + (N_DEV, shp[axis] // N_DEV) + shp[axis + 1:])
    return jnp.moveaxis(split, axis, 0)


def _from_dest_major(blocks, axis):
    moved = jnp.moveaxis(blocks, 0, axis)
    shp = moved.shape
    return moved.reshape(shp[:axis] + (shp[axis] * shp[axis + 1],) + shp[axis + 2:])


def _rows_2d(a):
    return a.reshape(-1, a.shape[-1])


def _rows_2d_lead(a):
    return a.reshape(a.shape[0], -1, a.shape[-1])


def kernel(x, norm_mix_g, even_w_in, even_conv_w, ssm_log_step, ssm_a_re, ssm_a_im, ssm_b_re, ssm_b_im, ssm_c_re, ssm_c_im, ssm_d, ssm_glu_w, ssm_glu_b, even_w_out, odd_w_in, pool_w, pool_scale, sgu_norm_g, sgu_w, sgu_b, odd_w_out, norm_ffn_g, ffn_w_up, ffn_conv_w, ffn_conv_b, ffn_w_down, norm_final_g, loss_target, m_norm_mix_g, m_even_w_in, m_even_conv_w, m_ssm_log_step, m_ssm_a_re, m_ssm_a_im, m_ssm_b_re, m_ssm_b_im, m_ssm_c_re, m_ssm_c_im, m_ssm_d, m_ssm_glu_w, m_ssm_glu_b, m_even_w_out, m_odd_w_in, m_pool_w, m_pool_scale, m_sgu_norm_g, m_sgu_w, m_sgu_b, m_odd_w_out, m_norm_ffn_g, m_ffn_w_up, m_ffn_conv_w, m_ffn_conv_b, m_ffn_w_down, m_norm_final_g, v_norm_mix_g, v_even_w_in, v_even_conv_w, v_ssm_log_step, v_ssm_a_re, v_ssm_a_im, v_ssm_b_re, v_ssm_b_im, v_ssm_c_re, v_ssm_c_im, v_ssm_d, v_ssm_glu_w, v_ssm_glu_b, v_even_w_out, v_odd_w_in, v_pool_w, v_pool_scale, v_sgu_norm_g, v_sgu_w, v_sgu_b, v_odd_w_out, v_norm_ffn_g, v_ffn_w_up, v_ffn_conv_w, v_ffn_conv_b, v_ffn_w_down, v_norm_final_g):
    given = dict(locals())
    wts = {n: given[n] for n in WEIGHT_NAMES}
    mom = {n: given["m_" + n] for n in WEIGHT_NAMES}
    var = {n: given["v_" + n] for n in WEIGHT_NAMES}
    repl = [n for n in WEIGHT_NAMES if n not in BIG and n not in SMALL]

    small_shapes = [wts[n].shape for n in SMALL]
    (small_all,) = _exchange_only([_Gather(_pack([wts[n] for n in SMALL], F32, SMALL_ROWS))], name="ag_small")
    full = {n: wts[n] for n in repl}
    for n, blocks in zip(SMALL, _unpack(small_all, small_shapes, lead=(N_DEV,))):
        full[n] = _from_dest_major(blocks, SMALL[n])
    comm = _ShardedWeights({n: wts[n].astype(BF16) for n in BIG})

    loss, dx, grads = _device_step(x[0], loss_target[0], full, comm)
    comm.finish()

    repl_shapes = [wts[n].shape for n in repl]
    repl_flat = jnp.concatenate([grads[n].reshape(-1) for n in repl] + [loss.reshape(-1)])
    n_repl = repl_flat.shape[0]
    chunk = _pad_to(-(-n_repl // N_DEV), SMALL_ROWS * LANE)
    repl_flat = jnp.pad(repl_flat, (0, N_DEV * chunk - n_repl))
    small_part = _pack([_to_dest_major(grads[n], SMALL[n]) for n in SMALL], F32, SMALL_ROWS, lead=(N_DEV,))
    small_cols = small_part.shape[2]
    (small_rs,) = _exchange_only([_Scatter(
        jnp.concatenate([small_part, repl_flat.reshape(N_DEV, SMALL_ROWS, chunk // SMALL_ROWS)], axis=2))],
        name="rs_small")
    small_sum = _sum_parts(small_rs, name="rs_sum_small")
    (repl_all,) = _exchange_only([_Gather(small_sum[:, small_cols:])], name="ag_repl")
    repl_sum = repl_all.reshape(-1)
    total_loss = repl_sum[n_repl - 1]

    out = {}
    for n in BIG:
        shp = wts[n].shape
        parts = jnp.concatenate([comm.scattered[(n, k)] for k in range(shp[0])], axis=1)
        res = _adamw(_rows_2d(wts[n]), _rows_2d(mom[n]), _rows_2d(var[n]), parts, name="adamw_" + n)
        out[n] = [r.reshape(shp) for r in res]

    def small_vec(shard_part, repl_part):
        flat = jnp.concatenate([shard_part.reshape(-1), repl_part])
        return flat.reshape(SMALL_ROWS, -1)

    def small_tree(tree):
        tail = jnp.concatenate([tree[n].reshape(-1) for n in repl])
        tail = jnp.pad(tail, (0, N_DEV * chunk - tail.shape[0]))
        return small_vec(_pack([tree[n] for n in SMALL], F32, SMALL_ROWS), tail)

    res = _adamw(small_tree(wts), small_tree(mom), small_tree(var), small_vec(small_sum[:, :small_cols], repl_sum)[None],
                 name="adamw_small")
    n_small = SMALL_ROWS * small_cols
    for k, r in enumerate(res):
        flat = r.reshape(-1)
        shard = _unpack(flat[:n_small], small_shapes)
        rest = _unpack(flat[n_small:], repl_shapes)
        for n, val in zip(SMALL, shard):
            out.setdefault(n, [None] * 4)[k] = val
        for n, val in zip(repl, rest):
            out.setdefault(n, [None] * 4)[k] = val

    grad_x = dx[None]
    return (total_loss, grad_x, *[out[n][0] for n in WEIGHT_NAMES], *[out[n][1] for n in WEIGHT_NAMES],
            *[out[n][2] for n in WEIGHT_NAMES], *[out[n][3] for n in WEIGHT_NAMES])
```

```python
import functools
import math

import jax
import jax.numpy as jnp
from jax import lax
from jax.experimental import pallas as pl
from jax.experimental.pallas import tpu as pltpu

F32 = jnp.float32
BF16 = jnp.bfloat16

D_MODEL = 1024
DEPTH = 4
D_HALF = D_MODEL // 2
SSM_GROUP = 16
N_SSM_GROUPS = D_HALF // SSM_GROUP
SSM_STATE = 64
POOL_WINDOWS = (2, 4, 8, 16)
SGU_HEADS = 4
CHUNK = 128
D_FF = 2816
CONV_WIDTH = 3
EPS = 1e-6
N_DEV = 8

ADAM_LR = 0.001
ADAM_B1 = 0.9
ADAM_B2 = 0.999
ADAM_EPS = 1e-08
ADAM_WD = 0.01
ADAM_STEP = 10

LANE = 128
SUBLANE = 8
S5_LANE_BLOCKS = D_HALF // LANE
S5_STATE_LANES = (N_SSM_GROUPS // S5_LANE_BLOCKS) * SSM_STATE
S5_TIME_CHUNK = 512
EXCHANGE_BLOCK_ELEMS = 1 << 20

GELU_K = math.sqrt(2.0 / math.pi)
GELU_C = 0.044715

_ARB = pltpu.CompilerParams(dimension_semantics=("arbitrary",))
_ARB2 = pltpu.CompilerParams(dimension_semantics=("arbitrary", "arbitrary"))
_PAR = pltpu.CompilerParams(dimension_semantics=("parallel",))
_PAR2 = pltpu.CompilerParams(dimension_semantics=("parallel", "parallel"))


def _pick_tile(n, cap, mult):
    if n <= cap:
        return n
    best = None
    for t in range(mult, cap + 1, mult):
        if n % t == 0:
            best = t
    assert best is not None, (n, cap, mult)
    return best


_MESH = pl.DeviceIdType.MESH
_ANY = pl.BlockSpec(memory_space=pl.ANY)
SEMS_PER_EXCHANGE = N_DEV - 1


class _Gather:
    into = None

    def __init__(self, src):
        self.src = src
        self.out_shape = jax.ShapeDtypeStruct((N_DEV,) + src.shape, src.dtype)

    def copies(self, x_ref, out_ref, send_sems, recv_sems, local_sem):
        x, y, cc = lax.axis_index("x"), lax.axis_index("y"), lax.axis_index("c")
        me, sibling = (x, y, cc), (x, y, 1 - cc)
        chips = [(1 - x, y), (x, 1 - y), (1 - x, 1 - y)]

        def rows(px, py, pc):
            return out_ref.at[4 * px + 2 * py + pc]

        def copy(k, block, to, src=None):
            return pltpu.make_async_remote_copy(
                src_ref=rows(*block) if src is None else src, dst_ref=rows(*block),
                send_sem=send_sems.at[k], recv_sem=recv_sems.at[k], device_id=to, device_id_type=_MESH)

        return dict(
            mine=pltpu.make_async_copy(x_ref, rows(*me), local_sem),
            first=[copy(0, me, sibling, src=x_ref)] + [copy(1 + k, me, (*chip, cc), src=x_ref)
                                                       for k, chip in enumerate(chips)],
            passed=[copy(4 + k, (*chip, cc), sibling) for k, chip in enumerate(chips)],
            over_ici=[copy(1 + k, (*chip, cc), me) for k, chip in enumerate(chips)],
            from_sibling=[copy(0, sibling, me)] + [copy(4 + k, (*chip, 1 - cc), me) for k, chip in enumerate(chips)])

    def start(self, *refs):
        cps = self.copies(*refs)
        cps["mine"].start()
        for cp in cps["first"]:
            cp.start()

    def finish(self, *refs):
        cps = self.copies(*refs)
        for arrived, onward in zip(cps["over_ici"], cps["passed"]):
            arrived.wait_recv()
            onward.start()
        for arrived in cps["from_sibling"]:
            arrived.wait_recv()
        for cp in cps["first"] + cps["passed"]:
            cp.wait_send()
        cps["mine"].wait()


class _Scatter:
    def __init__(self, src, into=None, row0=0):
        self.src, self.into, self.row0 = src, into, row0
        whole = src if into is None else into
        self.out_shape = jax.ShapeDtypeStruct(whole.shape, whole.dtype)

    def copies(self, p_ref, whole_ref, send_sems, recv_sems, local_sem):
        x, y, cc = lax.axis_index("x"), lax.axis_index("y"), lax.axis_index("c")
        me = 4 * x + 2 * y + cc
        rows = self.src.shape[1]
        out_ref = whole_ref if self.into is None else whole_ref.at[:, pl.ds(self.row0, rows)]
        sends, arrivals = [], []
        for k in range(1, N_DEV):
            px = (1 - x) if k & 4 else x
            py = (1 - y) if k & 2 else y
            pc = (1 - cc) if k & 1 else cc
            peer = 4 * px + 2 * py + pc
            kw = dict(send_sem=send_sems.at[k - 1], recv_sem=recv_sems.at[k - 1], device_id=(px, py, pc),
                      device_id_type=_MESH)
            sends.append(pltpu.make_async_remote_copy(src_ref=p_ref.at[peer], dst_ref=out_ref.at[me], **kw))
            arrivals.append(pltpu.make_async_remote_copy(src_ref=p_ref.at[me], dst_ref=out_ref.at[peer], **kw))
        return dict(mine=pltpu.make_async_copy(p_ref.at[me], out_ref.at[me], local_sem), sends=sends,
                    arrivals=arrivals)

    def start(self, *refs):
        cps = self.copies(*refs)
        cps["mine"].start()
        for cp in cps["sends"]:
            cp.start()

    def finish(self, *refs):
        cps = self.copies(*refs)
        for cp in cps["arrivals"]:
            cp.wait_recv()
        for cp in cps["sends"]:
            cp.wait_send()
        cps["mine"].wait()


class _SemView:
    def __init__(self, ref, lo):
        self.ref, self.lo = ref, lo

    @property
    def at(self):
        return self

    def __getitem__(self, k):
        return self.ref.at[self.lo + k]


def _call(body, *, name, grid, in_specs, out_specs, out_shape, args, scratch_shapes=(), parallel=True, exchanges=()):
    n_axes = len(grid)
    if not exchanges:
        sem = ("parallel" if parallel else "arbitrary",) * n_axes
        return pl.pallas_call(
            body, name=name, grid=grid, in_specs=in_specs, out_specs=out_specs, out_shape=out_shape,
            scratch_shapes=scratch_shapes, compiler_params=pltpu.CompilerParams(dimension_semantics=sem))(*args)
    single = not isinstance(out_shape, (list, tuple))
    out_specs = [out_specs] if single else list(out_specs)
    out_shape = [out_shape] if single else list(out_shape)
    n_in, n_out, n_scr, n_x = len(in_specs), len(out_specs), len(scratch_shapes), len(exchanges)
    landing = [(e, ex.into) for e, ex in enumerate(exchanges) if ex.into is not None]
    aliases = {n_in + n_x + pos: n_out + e for pos, (e, _) in enumerate(landing)}

    def wrapped(*refs):
        ins, refs = refs[:n_in], refs[n_in:]
        x_in, refs = refs[:n_x], refs[n_x + len(landing):]
        outs, refs = refs[:n_out], refs[n_out:]
        x_out, refs = refs[:n_x], refs[n_x:]
        scr, (send_sems, recv_sems, local_sems) = refs[:n_scr], refs[n_scr:]
        ids = [pl.program_id(k) for k in range(n_axes)]
        first = functools.reduce(jnp.logical_and, [i == 0 for i in ids])
        last = functools.reduce(jnp.logical_and, [i == g - 1 for i, g in zip(ids, grid)])

        def sems(e):
            lo = e * SEMS_PER_EXCHANGE
            return _SemView(send_sems, lo), _SemView(recv_sems, lo), local_sems.at[e]

        @pl.when(first)
        def _():
            for e, ex in enumerate(exchanges):
                ex.start(x_in[e], x_out[e], *sems(e))

        body(*ins, *outs, *scr)

        @pl.when(last)
        def _():
            for e, ex in enumerate(exchanges):
                ex.finish(x_in[e], x_out[e], *sems(e))

    res = pl.pallas_call(
        wrapped, name=name, grid=grid, in_specs=list(in_specs) + [_ANY] * (n_x + len(landing)),
        out_specs=out_specs + [_ANY] * n_x, out_shape=out_shape + [ex.out_shape for ex in exchanges],
        input_output_aliases=aliases,
        scratch_shapes=list(scratch_shapes) + [pltpu.SemaphoreType.DMA((n_x * SEMS_PER_EXCHANGE,)),
                                               pltpu.SemaphoreType.DMA((n_x * SEMS_PER_EXCHANGE,)),
                                               pltpu.SemaphoreType.DMA((n_x,))],
        compiler_params=pltpu.CompilerParams(dimension_semantics=("arbitrary",) * n_axes),
    )(*args, *[ex.src for ex in exchanges], *[buf for _, buf in landing])
    outs, x_outs = res[:n_out], res[n_out:]
    return (outs[0] if single else outs), x_outs


def _exchange_only(exchanges, *, name):
    def body():
        pass

    return _call(body, name=name, grid=(1,), in_specs=[], out_specs=[], out_shape=[], args=[],
                 exchanges=exchanges)[1]


def _shift_dn(x, d):
    rolled = pltpu.roll(x, d, 0)
    if x.shape[0] <= SUBLANE or d >= SUBLANE:
        row = lax.broadcasted_iota(jnp.int32, x.shape, 0)
        return jnp.where(row >= d, rolled, 0.0)
    row = lax.broadcasted_iota(jnp.int32, (SUBLANE, x.shape[1]), 0)
    return jnp.concatenate([jnp.where(row >= d, rolled[:SUBLANE], 0.0), rolled[SUBLANE:]], axis=0)


def _shift_up(x, d):
    n = x.shape[0]
    rolled = pltpu.roll(x, n - d, 0)
    if n <= SUBLANE or d >= SUBLANE:
        row = lax.broadcasted_iota(jnp.int32, x.shape, 0)
        return jnp.where(row < n - d, rolled, 0.0)
    row = lax.broadcasted_iota(jnp.int32, (SUBLANE, x.shape[1]), 0)
    return jnp.concatenate([rolled[:n - SUBLANE], jnp.where(row < SUBLANE - d, rolled[n - SUBLANE:], 0.0)], axis=0)


def _gelu(x):
    return 0.5 * x * (1.0 + jnp.tanh(GELU_K * (x + GELU_C * x * x * x)))


def _gelu_grad(x):
    t = jnp.tanh(GELU_K * (x + GELU_C * x * x * x))
    return 0.5 * (1.0 + t) + 0.5 * x * (1.0 - t * t) * (GELU_K * (1.0 + 3.0 * GELU_C * x * x))


def _sigmoid(x):
    return 1.0 / (1.0 + jnp.exp(-x))


def _conv3(x, w_ref):
    return w_ref[0:1, :] * _shift_dn(x, 2) + w_ref[1:2, :] * _shift_dn(x, 1) + w_ref[2:3, :] * x


def _conv3_bwd_x(dy, w_ref):
    return w_ref[2:3, :] * dy + w_ref[1:2, :] * _shift_up(dy, 1) + w_ref[0:1, :] * _shift_up(dy, 2)


def _conv3_bwd_w(dy, x):
    return jnp.concatenate([
        jnp.sum(dy * _shift_dn(x, 2), axis=0, keepdims=True),
        jnp.sum(dy * _shift_dn(x, 1), axis=0, keepdims=True),
        jnp.sum(dy * x, axis=0, keepdims=True)], axis=0)


def _dot(a, b, dims):
    return lax.dot_general(a.astype(BF16), b.astype(BF16), (dims, ((), ())), preferred_element_type=F32)


_NN = ((1,), (0,))
_NT = ((1,), (1,))
_TN = ((0,), (0,))


def _mm(a, b, mode, *, name, out_dtype=F32, add=None, norm=None, tm_cap=512, tn_cap=1536, exchanges=()):
    if mode == "tn":
        r, m = a.shape
        n = b.shape[1]
        tm, tn = _pick_tile(m, 256, LANE), _pick_tile(n, tn_cap, LANE)
        in_specs = [pl.BlockSpec((r, tm), lambda i, j: (0, i)), pl.BlockSpec((r, tn), lambda i, j: (0, j))]
        dims = _TN
    elif mode == "nn":
        m, k = a.shape
        n = b.shape[1]
        tm, tn = _pick_tile(m, tm_cap, SUBLANE), _pick_tile(n, tn_cap, LANE)
        in_specs = [pl.BlockSpec((tm, k), lambda i, j: (i, 0)), pl.BlockSpec((k, tn), lambda i, j: (0, j))]
        dims = _NN
    else:
        m, k = a.shape
        n = b.shape[0]
        tm, tn = _pick_tile(m, tm_cap, SUBLANE), _pick_tile(n, tn_cap, LANE)
        in_specs = [pl.BlockSpec((tm, k), lambda i, j: (i, 0)), pl.BlockSpec((tn, k), lambda i, j: (j, 0))]
        dims = _NT
    args = [a, b]
    tile = pl.BlockSpec((tm, tn), lambda i, j: (i, j))
    if add is not None:
        in_specs.append(tile)
        args.append(add)
    out_specs, out_shape = tile, jax.ShapeDtypeStruct((m, n), out_dtype)
    if norm is not None:
        gains, layer = norm
        assert tn == n
        in_specs.append(pl.BlockSpec((None, 1, n), lambda i, j: (layer, 0, 0)))
        args.append(gains.reshape(gains.shape[0], 1, n))
        out_specs, out_shape = [tile, tile], [out_shape, jax.ShapeDtypeStruct((m, n), BF16)]

    def body(*refs):
        acc = _dot(refs[0][...], refs[1][...], dims)
        if add is not None:
            acc = acc + refs[2][...]
        if norm is None:
            refs[-1][...] = acc.astype(out_dtype)
        else:
            refs[-2][...] = acc.astype(out_dtype)
            r = lax.rsqrt(jnp.mean(acc * acc, axis=-1, keepdims=True) + EPS)
            refs[-1][...] = (acc * r * refs[-3][...]).astype(BF16)

    return _call(body, name=name, grid=(m // tm, n // tn), in_specs=in_specs, out_specs=out_specs,
                 out_shape=out_shape, args=args, exchanges=exchanges)


def _rms_fwd(x, g, *, name):
    l, d = x.shape
    tl = _pick_tile(l, 512, SUBLANE)

    def body(x_ref, g_ref, h_ref):
        xv = x_ref[...]
        r = lax.rsqrt(jnp.mean(xv * xv, axis=-1, keepdims=True) + EPS)
        h_ref[...] = (xv * r * g_ref[...]).astype(BF16)

    return pl.pallas_call(
        body, name=name, grid=(l // tl,),
        in_specs=[pl.BlockSpec((tl, d), lambda i: (i, 0)), pl.BlockSpec((1, d), lambda i: (0, 0))],
        out_specs=pl.BlockSpec((tl, d), lambda i: (i, 0)),
        out_shape=jax.ShapeDtypeStruct((l, d), BF16), compiler_params=_PAR)(x, g)


def _rms_bwd_rows(dh, xv, g):
    r = lax.rsqrt(jnp.mean(xv * xv, axis=-1, keepdims=True) + EPS)
    a = dh * g
    m = jnp.mean(a * xv, axis=-1, keepdims=True)
    return r * a - xv * (r * r * r) * m, xv * r


def _rms_bwd(dh, x, g, res, *, name):
    l, d = x.shape
    tl = _pick_tile(l, 512, SUBLANE)

    def body(dh_ref, x_ref, g_ref, res_ref, dx_ref, dxb_ref, dg_ref):
        @pl.when(pl.program_id(0) == 0)
        def _():
            dg_ref[...] = jnp.zeros_like(dg_ref)
        dhv = dh_ref[...]
        dx, xn = _rms_bwd_rows(dhv, x_ref[...], g_ref[...])
        dx = dx + res_ref[...]
        dx_ref[...] = dx
        dxb_ref[...] = dx.astype(BF16)
        dg_ref[...] += jnp.sum(dhv * xn, axis=0, keepdims=True)

    row = pl.BlockSpec((tl, d), lambda i: (i, 0))
    vec = pl.BlockSpec((1, d), lambda i: (0, 0))
    return pl.pallas_call(
        body, name=name, grid=(l // tl,), in_specs=[row, row, vec, row], out_specs=[row, row, vec],
        out_shape=[jax.ShapeDtypeStruct((l, d), F32), jax.ShapeDtypeStruct((l, d), BF16),
                   jax.ShapeDtypeStruct((1, d), F32)],
        compiler_params=_ARB)(dh, x, g, res)


def _loss_head(x, g, tgt, *, name):
    l, d = x.shape
    tl = _pick_tile(l, 512, SUBLANE)

    def body(x_ref, g_ref, t_ref, loss_ref, dx_ref, dxb_ref, dg_ref):
        @pl.when(pl.program_id(0) == 0)
        def _():
            dg_ref[...] = jnp.zeros_like(dg_ref)
            loss_ref[...] = jnp.zeros_like(loss_ref)
        xv = x_ref[...]
        gv = g_ref[...]
        r = lax.rsqrt(jnp.mean(xv * xv, axis=-1, keepdims=True) + EPS)
        err = xv * r * gv - t_ref[...]
        row_loss = jnp.sum(err * err, axis=-1, keepdims=True) * (0.5 / d)
        loss_ref[...] += jnp.sum(row_loss, axis=0, keepdims=True)
        dy = err * (1.0 / d)
        dx, xn = _rms_bwd_rows(dy, xv, gv)
        dx_ref[...] = dx
        dxb_ref[...] = dx.astype(BF16)
        dg_ref[...] += jnp.sum(dy * xn, axis=0, keepdims=True)

    row = pl.BlockSpec((tl, d), lambda i: (i, 0))
    vec = pl.BlockSpec((1, d), lambda i: (0, 0))
    one = pl.BlockSpec((1, 1), lambda i: (0, 0))
    return pl.pallas_call(
        body, name=name, grid=(l // tl,), in_specs=[row, vec, row], out_specs=[one, row, row, vec],
        out_shape=[jax.ShapeDtypeStruct((1, 1), F32), jax.ShapeDtypeStruct((l, d), F32),
                   jax.ShapeDtypeStruct((l, d), BF16), jax.ShapeDtypeStruct((1, d), F32)],
        compiler_params=_ARB)(x, g, tgt)


def _ffn_act(up, cw, cb, *, name, exchanges=()):
    l = up.shape[0]
    nb = D_FF // LANE

    def body(ug_ref, uv_ref, wg_ref, wv_ref, bg_ref, bv_ref, o_ref):
        gc = _conv3(ug_ref[...], wg_ref) + bg_ref[...]
        vc = _conv3(uv_ref[...], wv_ref) + bv_ref[...]
        o_ref[...] = (gc * _sigmoid(gc) * vc).astype(BF16)

    col = lambda off: pl.BlockSpec((l, LANE), lambda j: (0, j + off))
    w3 = lambda off: pl.BlockSpec((CONV_WIDTH, LANE), lambda j: (0, j + off))
    b1 = lambda off: pl.BlockSpec((1, LANE), lambda j: (0, j + off))
    return _call(body, name=name, grid=(nb,), in_specs=[col(0), col(nb), w3(0), w3(nb), b1(0), b1(nb)],
                 out_specs=col(0), out_shape=jax.ShapeDtypeStruct((l, D_FF), BF16),
                 args=[up, up, cw, cw, cb, cb], exchanges=exchanges)


def _ffn_act_bwd(up, dact, cw, cb, *, name, exchanges=()):
    l = up.shape[0]
    nb = D_FF // LANE

    def body(ug_ref, uv_ref, da_ref, wg_ref, wv_ref, bg_ref, bv_ref, dup_ref, dcw_ref, dcb_ref):
        is_gate = pl.program_id(0) < nb
        ug = ug_ref[...]
        ug1, ug2 = _shift_dn(ug, 1), _shift_dn(ug, 2)
        gc = wg_ref[0:1, :] * ug2 + wg_ref[1:2, :] * ug1 + wg_ref[2:3, :] * ug + bg_ref[...]
        sg = _sigmoid(gc)

        def conv_bwd(dc, x0, x1, x2, w_ref):
            dcb_ref[...] = jnp.sum(dc, axis=0, keepdims=True)
            dcw_ref[...] = jnp.concatenate([jnp.sum(dc * x2, axis=0, keepdims=True),
                                            jnp.sum(dc * x1, axis=0, keepdims=True),
                                            jnp.sum(dc * x0, axis=0, keepdims=True)], axis=0)
            dup_ref[...] = _conv3_bwd_x(dc, w_ref).astype(BF16)

        @pl.when(is_gate)
        def _():
            vc = _conv3(uv_ref[...], wv_ref) + bv_ref[...]
            conv_bwd(da_ref[...] * vc * (sg * (1.0 + gc * (1.0 - sg))), ug, ug1, ug2, wg_ref)

        @pl.when(jnp.logical_not(is_gate))
        def _():
            uv = uv_ref[...]
            conv_bwd(da_ref[...] * (gc * sg), uv, _shift_dn(uv, 1), _shift_dn(uv, 2), wv_ref)

    half = lambda j: j % nb
    colg = pl.BlockSpec((l, LANE), lambda j: (0, half(j)))
    colv = pl.BlockSpec((l, LANE), lambda j: (0, half(j) + nb))
    colo = pl.BlockSpec((l, LANE), lambda j: (0, j))
    w3g = pl.BlockSpec((CONV_WIDTH, LANE), lambda j: (0, half(j)))
    w3v = pl.BlockSpec((CONV_WIDTH, LANE), lambda j: (0, half(j) + nb))
    w3o = pl.BlockSpec((CONV_WIDTH, LANE), lambda j: (0, j))
    b1g = pl.BlockSpec((1, LANE), lambda j: (0, half(j)))
    b1v = pl.BlockSpec((1, LANE), lambda j: (0, half(j) + nb))
    b1o = pl.BlockSpec((1, LANE), lambda j: (0, j))
    return _call(
        body, name=name, grid=(2 * nb,),
        in_specs=[colg, colv, colg, w3g, w3v, b1g, b1v],
        out_specs=[colo, w3o, b1o],
        out_shape=[jax.ShapeDtypeStruct((l, 2 * D_FF), BF16), jax.ShapeDtypeStruct((CONV_WIDTH, 2 * D_FF), F32),
                   jax.ShapeDtypeStruct((1, 2 * D_FF), F32)],
        args=[up, up, dact, cw, cw, cb, cb], exchanges=exchanges)


def _sconv_fwd(proj, cw, *, name):
    l = proj.shape[0]
    nb = D_HALF // LANE

    def body(xa_ref, ba_ref, ca_ref, w_ref, o_ref):
        o_ref[...] = (ba_ref[...] * _conv3(ca_ref[...] * xa_ref[...], w_ref)).astype(BF16)

    col = lambda off: pl.BlockSpec((l, LANE), lambda j: (0, j + off))
    return pl.pallas_call(
        body, name=name, grid=(nb,),
        in_specs=[col(0), col(nb), col(2 * nb), pl.BlockSpec((CONV_WIDTH, LANE), lambda j: (0, j))],
        out_specs=col(0), out_shape=jax.ShapeDtypeStruct((l, D_HALF), BF16),
        compiler_params=_PAR)(proj, proj, proj, cw)


def _sconv_bwd(proj, dcat, cw, *, name):
    l = proj.shape[0]
    nb = D_HALF // LANE

    def body(xa_ref, ba_ref, ca_ref, dy_ref, w_ref, dxa_ref, dba_ref, dca_ref, dw_ref):
        xa, ba, ca, dy = xa_ref[...], ba_ref[...], ca_ref[...], dy_ref[...]
        q = ca * xa
        dba_ref[...] = (dy * _conv3(q, w_ref)).astype(BF16)
        dconv = dy * ba
        dw_ref[...] = _conv3_bwd_w(dconv, q)
        dq = _conv3_bwd_x(dconv, w_ref)
        dxa_ref[...] = (dq * ca).astype(BF16)
        dca_ref[...] = (dq * xa).astype(BF16)

    col = lambda off: pl.BlockSpec((l, LANE), lambda j: (0, j + off))
    w3 = pl.BlockSpec((CONV_WIDTH, LANE), lambda j: (0, j))
    piece = jax.ShapeDtypeStruct((l, D_HALF), BF16)
    return pl.pallas_call(
        body, name=name, grid=(nb,),
        in_specs=[col(0), col(nb), col(2 * nb), col(0), w3],
        out_specs=[col(0), col(0), col(0), w3],
        out_shape=[piece, piece, piece, jax.ShapeDtypeStruct((CONV_WIDTH, D_HALF), F32)],
        compiler_params=_PAR)(proj, proj, proj, dcat, cw)


def _s5_prep(log_step, a_re, a_im, b_re, b_im):
    step = jnp.exp(log_step)[:, None]
    mag = jnp.exp(a_re * step)
    lr = mag * jnp.cos(a_im * step)
    li = mag * jnp.sin(a_im * step)
    nr = lr - 1.0
    den = a_re * a_re + a_im * a_im
    qr = (nr * a_re + li * a_im) / den
    qi = (li * a_re - nr * a_im) / den
    br = qr[..., None] * b_re - qi[..., None] * b_im
    bi = qr[..., None] * b_im + qi[..., None] * b_re
    return lr, li, br, bi


def _block_diag(m):
    nb, ng, r, c = m.shape
    eye = jnp.eye(ng, dtype=m.dtype)
    return jnp.einsum("bgrc,gh->bgrhc", m, eye).reshape(nb, ng * r, ng * c)


def _block_diag_extract(w, r, c):
    nb = w.shape[0]
    ng = w.shape[1] // r
    w5 = w.reshape(nb, ng, r, ng, c)
    return jnp.einsum("bgrhc,gh->bgrc", w5, jnp.eye(ng, dtype=w.dtype))


def _s5_mats(br, bi, c_re, c_im):
    g8 = N_SSM_GROUPS // S5_LANE_BLOCKS
    to_blk = lambda m: m.reshape(S5_LANE_BLOCKS, g8, m.shape[1], m.shape[2])
    wb = jnp.concatenate([_block_diag(to_blk(jnp.swapaxes(br, 1, 2))),
                          _block_diag(to_blk(jnp.swapaxes(bi, 1, 2)))], axis=2)
    wc = jnp.concatenate([_block_diag(to_blk(jnp.swapaxes(c_re, 1, 2))),
                          _block_diag(to_blk(jnp.swapaxes(-c_im, 1, 2)))], axis=1)
    return wb, wc


def _s5_mats_bwd(dwb, dwc):
    g, p, h = N_SSM_GROUPS, SSM_STATE, SSM_GROUP
    half = S5_STATE_LANES
    dbr = jnp.swapaxes(_block_diag_extract(dwb[:, :, :half], h, p).reshape(g, h, p), 1, 2)
    dbi = jnp.swapaxes(_block_diag_extract(dwb[:, :, half:], h, p).reshape(g, h, p), 1, 2)
    dcr = jnp.swapaxes(_block_diag_extract(dwc[:, :half, :], p, h).reshape(g, p, h), 1, 2)
    dci = -jnp.swapaxes(_block_diag_extract(dwc[:, half:, :], p, h).reshape(g, p, h), 1, 2)
    return dbr, dbi, dcr, dci


def _s5_scan_consts(log_step, a_re, a_im, reverse):
    step = jnp.exp(log_step)[:, None]
    xr = (a_re * step).reshape(S5_LANE_BLOCKS, 1, S5_STATE_LANES)
    xi = (a_im * step).reshape(S5_LANE_BLOCKS, 1, S5_STATE_LANES)
    if reverse:
        xi = -xi
    row = jnp.arange(SUBLANE, dtype=F32).reshape(1, SUBLANE, 1)

    def power(n):
        mag = jnp.exp(n * xr)
        return jnp.concatenate([mag * jnp.cos(n * xi), mag * jnp.sin(n * xi)], axis=-1)

    kinds = []
    for d in (1, 2, 4):
        keep = (row <= SUBLANE - 1 - d) if reverse else (row >= d)
        kinds.append(jnp.where(keep, power(jnp.full_like(row, float(d))), 0.0))
    kinds.append(power((SUBLANE - row) if reverse else (row + 1.0)))
    return jnp.stack(kinds, axis=1)


def _scan_rows(s_ref, sc_ref, carry_ref, n_rows, reverse):
    n_grp = n_rows // SUBLANE
    n_col = S5_STATE_LANES // LANE
    half = S5_STATE_LANES

    def step(i, carry):
        grp = (n_grp - 1 - i) if reverse else i
        r0 = pl.multiple_of(grp * SUBLANE, SUBLANE)
        out = []
        for cb in range(n_col):
            lo, hi = cb * LANE, half + cb * LANE
            re = s_ref[pl.ds(r0, SUBLANE), lo:lo + LANE]
            im = s_ref[pl.ds(r0, SUBLANE), hi:hi + LANE]
            for k, d in enumerate((1, 2, 4)):
                sh = (SUBLANE - d) if reverse else d
                rr, ri = pltpu.roll(re, sh, 0), pltpu.roll(im, sh, 0)
                ar, ai = sc_ref[k, :, lo:lo + LANE], sc_ref[k, :, hi:hi + LANE]
                re, im = re + (ar * rr - ai * ri), im + (ar * ri + ai * rr)
            pr, pi = sc_ref[3, :, lo:lo + LANE], sc_ref[3, :, hi:hi + LANE]
            cr, ci = carry[2 * cb], carry[2 * cb + 1]
            re, im = re + (pr * cr - pi * ci), im + (pr * ci + pi * cr)
            s_ref[pl.ds(r0, SUBLANE), lo:lo + LANE] = re
            s_ref[pl.ds(r0, SUBLANE), hi:hi + LANE] = im
            edge = 0 if reverse else SUBLANE - 1
            out.append(jnp.broadcast_to(re[edge:edge + 1, :], (SUBLANE, LANE)))
            out.append(jnp.broadcast_to(im[edge:edge + 1, :], (SUBLANE, LANE)))
        return tuple(out)

    init = []
    for cb in range(n_col):
        init.append(carry_ref[:, cb * LANE:(cb + 1) * LANE])
        init.append(carry_ref[:, half + cb * LANE:half + (cb + 1) * LANE])
    fin = lax.fori_loop(0, n_grp, step, tuple(init))
    for cb in range(n_col):
        carry_ref[:, cb * LANE:(cb + 1) * LANE] = fin[2 * cb]
        carry_ref[:, half + cb * LANE:half + (cb + 1) * LANE] = fin[2 * cb + 1]


def _s5_fwd(proj, wb, wc, d_skip, sc, *, name, exchanges=()):
    l = proj.shape[0]
    tt = _pick_tile(l, S5_TIME_CHUNK, SUBLANE)
    u_off = (proj.shape[1] - D_HALF) // LANE
    w2 = 2 * S5_STATE_LANES

    def body(u_ref, wb_ref, wc_ref, d_ref, sc_ref, s_ref, y_ref, carry_ref):
        @pl.when(pl.program_id(1) == 0)
        def _():
            carry_ref[...] = jnp.zeros_like(carry_ref)
        u = u_ref[...]
        s_ref[...] = _dot(u, wb_ref[0], _NN)
        _scan_rows(s_ref, sc_ref.at[0], carry_ref, tt, False)
        y_ref[...] = _dot(s_ref[...], wc_ref[0], _NN) + d_ref[...] * u

    return _call(
        body, name=name, grid=(S5_LANE_BLOCKS, l // tt),
        in_specs=[pl.BlockSpec((tt, LANE), lambda b, t: (t, b + u_off)),
                  pl.BlockSpec((1, LANE, w2), lambda b, t: (b, 0, 0)),
                  pl.BlockSpec((1, w2, LANE), lambda b, t: (b, 0, 0)),
                  pl.BlockSpec((1, LANE), lambda b, t: (0, b)),
                  pl.BlockSpec((1, 4, SUBLANE, w2), lambda b, t: (b, 0, 0, 0))],
        out_specs=[pl.BlockSpec((tt, w2), lambda b, t: (t, b)), pl.BlockSpec((tt, LANE), lambda b, t: (t, b))],
        out_shape=[jax.ShapeDtypeStruct((l, S5_LANE_BLOCKS * w2), F32), jax.ShapeDtypeStruct((l, D_HALF), F32)],
        scratch_shapes=[pltpu.VMEM((SUBLANE, w2), F32)],
        args=[proj, wb, wc, d_skip, sc], parallel=False, exchanges=exchanges)


def _s5_bwd(proj, dy, states, wb, wc, d_skip, sc_rev, *, name, exchanges=()):
    l = proj.shape[0]
    tt = _pick_tile(l, S5_TIME_CHUNK, SUBLANE)
    nt = l // tt
    u_off = (proj.shape[1] - D_HALF) // LANE
    w2 = 2 * S5_STATE_LANES
    half = S5_STATE_LANES
    grp_per_chunk = tt // SUBLANE

    def body(u_ref, dy_ref, s_ref, halo_ref, wb_ref, wc_ref, d_ref, sc_ref,
             du_ref, dwb_ref, dwc_ref, dlam_ref, dd_ref, g_scr, carry_ref):
        t = pl.program_id(1)

        @pl.when(t == 0)
        def _():
            carry_ref[...] = jnp.zeros_like(carry_ref)
            dwb_ref[...] = jnp.zeros_like(dwb_ref)
            dwc_ref[...] = jnp.zeros_like(dwc_ref)
            dlam_ref[...] = jnp.zeros_like(dlam_ref)
            dd_ref[...] = jnp.zeros_like(dd_ref)

        u = u_ref[...]
        dyv = dy_ref[...]
        g_scr[...] = _dot(dyv, wc_ref[0], _NT)
        _scan_rows(g_scr, sc_ref.at[0], carry_ref, tt, True)
        gv = g_scr[...]
        du_ref[...] = (_dot(gv, wb_ref[0], _NT) + d_ref[...] * dyv).astype(BF16)
        dwb_ref[0] += _dot(u, gv, _TN)
        sv = s_ref[...]
        dwc_ref[0] += _dot(sv, dyv, _TN)
        dd_ref[...] += jnp.sum(dyv * u, axis=0, keepdims=True)
        first_chunk = t == nt - 1
        halo = jnp.where(first_chunk, 0.0, halo_ref[SUBLANE - 1:SUBLANE, :])
        row = lax.broadcasted_iota(jnp.int32, sv.shape, 0)
        sp = jnp.where(row == 0, jnp.broadcast_to(halo, sv.shape), pltpu.roll(sv, 1, 0))
        gr, gi = gv[:, :half], gv[:, half:]
        sr, si = sp[:, :half], sp[:, half:]
        dlr = jnp.sum(gr * sr + gi * si, axis=0, keepdims=True)
        dli = jnp.sum(gi * sr - gr * si, axis=0, keepdims=True)
        dlam_ref[0] += jnp.concatenate([dlr, dli], axis=1)

    rev = lambda t: nt - 1 - t
    return _call(
        body, name=name, grid=(S5_LANE_BLOCKS, nt),
        in_specs=[pl.BlockSpec((tt, LANE), lambda b, t: (rev(t), b + u_off)),
                  pl.BlockSpec((tt, LANE), lambda b, t: (rev(t), b)),
                  pl.BlockSpec((tt, w2), lambda b, t: (rev(t), b)),
                  pl.BlockSpec((SUBLANE, w2), lambda b, t: (jnp.maximum(rev(t) * grp_per_chunk - 1, 0), b)),
                  pl.BlockSpec((1, LANE, w2), lambda b, t: (b, 0, 0)),
                  pl.BlockSpec((1, w2, LANE), lambda b, t: (b, 0, 0)),
                  pl.BlockSpec((1, LANE), lambda b, t: (0, b)),
                  pl.BlockSpec((1, 4, SUBLANE, w2), lambda b, t: (b, 0, 0, 0))],
        out_specs=[pl.BlockSpec((tt, LANE), lambda b, t: (rev(t), b)),
                   pl.BlockSpec((1, LANE, w2), lambda b, t: (b, 0, 0)),
                   pl.BlockSpec((1, w2, LANE), lambda b, t: (b, 0, 0)),
                   pl.BlockSpec((1, 1, w2), lambda b, t: (b, 0, 0)),
                   pl.BlockSpec((1, LANE), lambda b, t: (0, b))],
        out_shape=[jax.ShapeDtypeStruct((l, D_HALF), BF16),
                   jax.ShapeDtypeStruct((S5_LANE_BLOCKS, LANE, w2), F32),
                   jax.ShapeDtypeStruct((S5_LANE_BLOCKS, w2, LANE), F32),
                   jax.ShapeDtypeStruct((S5_LANE_BLOCKS, 1, w2), F32),
                   jax.ShapeDtypeStruct((1, D_HALF), F32)],
        scratch_shapes=[pltpu.VMEM((tt, w2), F32), pltpu.VMEM((SUBLANE, w2), F32)],
        args=[proj, dy, states, states, wb, wc, d_skip, sc_rev], parallel=False, exchanges=exchanges)


def _glu_fwd(ypre, wg, bg, *, name):
    l, d = ypre.shape
    tl = _pick_tile(l, 512, SUBLANE)

    def body(y_ref, w_ref, b_ref, o_ref):
        yg = _gelu(y_ref[...])
        o_ref[...] = (yg * _sigmoid(_dot(yg, w_ref[...], _NN) + b_ref[...])).astype(BF16)

    row = pl.BlockSpec((tl, d), lambda i: (i, 0))
    return pl.pallas_call(
        body, name=name, grid=(l // tl,),
        in_specs=[row, pl.BlockSpec((d, d), lambda i: (0, 0)), pl.BlockSpec((1, d), lambda i: (0, 0))],
        out_specs=row, out_shape=jax.ShapeDtypeStruct((l, d), BF16), compiler_params=_PAR)(ypre, wg, bg)


def _glu_bwd(dcat, ypre, wg, bg, *, name):
    l, d = ypre.shape
    tl = _pick_tile(l, 512, SUBLANE)

    def body(dy_ref, y_ref, w_ref, b_ref, dyp_ref, dw_ref, db_ref):
        @pl.when(pl.program_id(0) == 0)
        def _():
            dw_ref[...] = jnp.zeros_like(dw_ref)
            db_ref[...] = jnp.zeros_like(db_ref)
        yp = y_ref[...]
        dyb = dy_ref[...]
        yg = _gelu(yp)
        sg = _sigmoid(_dot(yg, w_ref[...], _NN) + b_ref[...])
        dz = dyb * yg * sg * (1.0 - sg)
        dyg = dyb * sg + _dot(dz, w_ref[...], _NT)
        dyp_ref[...] = dyg * _gelu_grad(yp)
        dw_ref[...] += _dot(yg, dz, _TN)
        db_ref[...] += jnp.sum(dz, axis=0, keepdims=True)

    row = pl.BlockSpec((tl, d), lambda i: (i, 0))
    mat = pl.BlockSpec((d, d), lambda i: (0, 0))
    vec = pl.BlockSpec((1, d), lambda i: (0, 0))
    return pl.pallas_call(
        body, name=name, grid=(l // tl,),
        in_specs=[pl.BlockSpec((tl, d), lambda i: (i, 1)), row, mat, vec], out_specs=[row, mat, vec],
        out_shape=[jax.ShapeDtypeStruct((l, d), F32), jax.ShapeDtypeStruct((d, d), F32),
                   jax.ShapeDtypeStruct((1, d), F32)],
        compiler_params=_ARB)(dcat, ypre, wg, bg)


def _window_sum(x, w, trailing):
    s, d = x, 1
    while d < w:
        s = s + (_shift_dn(s, d) if trailing else _shift_up(s, d))
        d *= 2
    return s


def _window_count(shape, w):
    row = lax.broadcasted_iota(jnp.int32, shape, 0)
    return jnp.minimum(row + 1, w).astype(F32)


def _pool_fwd(proj, pw, scale, *, name):
    l = proj.shape[0]
    ng = len(POOL_WINDOWS)

    def body(z_ref, w_ref, sc_ref, o_ref):
        z = z_ref[...]
        for k, w in enumerate(POOL_WINDOWS):
            @pl.when(pl.program_id(0) == k)
            def _():
                pooled = _window_sum(z, w, True) / _window_count(z.shape, w) - z
                o_ref[...] = (_dot(pooled, w_ref[0], _NN) * sc_ref[...]).astype(BF16)

    col = pl.BlockSpec((l, LANE), lambda g: (0, g))
    return pl.pallas_call(
        body, name=name, grid=(ng,),
        in_specs=[col, pl.BlockSpec((1, LANE, LANE), lambda g: (g, 0, 0)), pl.BlockSpec((1, LANE), lambda g: (0, g))],
        out_specs=col, out_shape=jax.ShapeDtypeStruct((l, D_HALF), BF16), compiler_params=_PAR)(proj, pw, scale)


def _pool_bwd(proj, dcat, pw, scale, *, name):
    l = proj.shape[0]
    ng = len(POOL_WINDOWS)

    def body(z_ref, dy_ref, w_ref, sc_ref, dz_ref, dw_ref, dsc_ref):
        z = z_ref[...]
        dy = dy_ref[...]
        for k, w in enumerate(POOL_WINDOWS):
            @pl.when(pl.program_id(0) == k)
            def _():
                cnt = _window_count(z.shape, w)
                pooled = _window_sum(z, w, True) / cnt - z
                ypre = _dot(pooled, w_ref[0], _NN)
                dsc_ref[...] = jnp.sum(dy * ypre, axis=0, keepdims=True)
                dyp = dy * sc_ref[...]
                dw_ref[0] = _dot(pooled, dyp, _TN)
                dpool = _dot(dyp, w_ref[0], _NT)
                dz_ref[...] = (_window_sum(dpool / cnt, w, False) - dpool).astype(BF16)

    col = pl.BlockSpec((l, LANE), lambda g: (0, g))
    mat = pl.BlockSpec((1, LANE, LANE), lambda g: (g, 0, 0))
    vec = pl.BlockSpec((1, LANE), lambda g: (0, g))
    return pl.pallas_call(
        body, name=name, grid=(ng,), in_specs=[col, col, mat, vec], out_specs=[col, mat, vec],
        out_shape=[jax.ShapeDtypeStruct((l, D_HALF), BF16), jax.ShapeDtypeStruct((ng, LANE, LANE), F32),
                   jax.ShapeDtypeStruct((1, D_HALF), F32)],
        compiler_params=_PAR)(proj, dcat, pw, scale)


def _tril_mask():
    r = lax.broadcasted_iota(jnp.int32, (CHUNK, CHUNK), 0)
    c = lax.broadcasted_iota(jnp.int32, (CHUNK, CHUNK), 1)
    return r >= c


def _sgu_fwd(proj, ng, sw, sb_t, *, name):
    l = proj.shape[0]
    tl = _pick_tile(l, 512, CHUNK)

    def body(su_ref, sv_ref, g_ref, w_ref, b_ref, o_ref):
        su = _gelu(su_ref[...])
        sv = _gelu(sv_ref[...])
        r = lax.rsqrt(jnp.mean(sv * sv, axis=-1, keepdims=True) + EPS)
        v = sv * r * g_ref[...]
        mask = _tril_mask()
        for h in range(SGU_HEADS):
            wm = jnp.where(mask, w_ref[h], 0.0)
            cs = slice(h * LANE, (h + 1) * LANE)
            for n in range(tl // CHUNK):
                rs = slice(n * CHUNK, (n + 1) * CHUNK)
                mixed = _dot(wm, v[rs, cs], _NN) + b_ref[:, h:h + 1]
                o_ref[rs, cs] = (su[rs, cs] * mixed).astype(BF16)

    blk = lambda c: pl.BlockSpec((tl, D_HALF), lambda i: (i, c))
    return pl.pallas_call(
        body, name=name, grid=(l // tl,),
        in_specs=[blk(1), blk(2), pl.BlockSpec((1, D_HALF), lambda i: (0, 0)),
                  pl.BlockSpec((SGU_HEADS, CHUNK, CHUNK), lambda i: (0, 0, 0)),
                  pl.BlockSpec((CHUNK, SGU_HEADS), lambda i: (0, 0))],
        out_specs=blk(0), out_shape=jax.ShapeDtypeStruct((l, D_HALF), BF16),
        compiler_params=_PAR)(proj, proj, ng, sw, sb_t)


def _sgu_bwd(proj, dcat, ng, sw, sb_t, *, name):
    l = proj.shape[0]
    tl = _pick_tile(l, 512, CHUNK)

    def body(su_ref, sv_ref, dy_ref, g_ref, w_ref, b_ref, dsu_ref, dsv_ref, dw_ref, dbm_ref, dng_ref, dv_scr):
        @pl.when(pl.program_id(0) == 0)
        def _():
            dw_ref[...] = jnp.zeros_like(dw_ref)
            dbm_ref[...] = jnp.zeros_like(dbm_ref)
            dng_ref[...] = jnp.zeros_like(dng_ref)
        su_pre = su_ref[...]
        sv_pre = sv_ref[...]
        su = _gelu(su_pre)
        sv = _gelu(sv_pre)
        gsu = _gelu_grad(su_pre)
        gv = g_ref[...]
        r = lax.rsqrt(jnp.mean(sv * sv, axis=-1, keepdims=True) + EPS)
        v = sv * r * gv
        dy = dy_ref[...]
        mask = _tril_mask()
        for h in range(SGU_HEADS):
            wm = jnp.where(mask, w_ref[h], 0.0)
            cs = slice(h * LANE, (h + 1) * LANE)
            dw_acc = jnp.zeros((CHUNK, CHUNK), F32)
            db_acc = jnp.zeros((CHUNK, LANE), F32)
            for n in range(tl // CHUNK):
                rs = slice(n * CHUNK, (n + 1) * CHUNK)
                vb = v[rs, cs]
                mixed = _dot(wm, vb, _NN) + b_ref[:, h:h + 1]
                dyb = dy[rs, cs]
                dsu_ref[rs, cs] = (dyb * mixed * gsu[rs, cs]).astype(BF16)
                dmix = dyb * su[rs, cs]
                db_acc = db_acc + dmix
                dw_acc = dw_acc + _dot(dmix, vb, _NT)
                dv_scr[rs, cs] = _dot(wm, dmix, _TN)
            dw_ref[h] += jnp.where(mask, dw_acc, 0.0)
            dbm_ref[h] += db_acc
        dv = dv_scr[...]
        a = dv * gv
        m = jnp.mean(a * sv, axis=-1, keepdims=True)
        dsv = r * a - sv * (r * r * r) * m
        dng_ref[...] += jnp.sum(dv * sv * r, axis=0, keepdims=True)
        dsv_ref[...] = (dsv * _gelu_grad(sv_pre)).astype(BF16)

    blk = lambda c: pl.BlockSpec((tl, D_HALF), lambda i: (i, c))
    mats = pl.BlockSpec((SGU_HEADS, CHUNK, CHUNK), lambda i: (0, 0, 0))
    vec = pl.BlockSpec((1, D_HALF), lambda i: (0, 0))
    piece = jax.ShapeDtypeStruct((l, D_HALF), BF16)
    mshape = jax.ShapeDtypeStruct((SGU_HEADS, CHUNK, CHUNK), F32)
    return pl.pallas_call(
        body, name=name, grid=(l // tl,),
        in_specs=[blk(1), blk(2), blk(1), vec, mats, pl.BlockSpec((CHUNK, SGU_HEADS), lambda i: (0, 0))],
        out_specs=[blk(0), blk(0), mats, mats, vec],
        out_shape=[piece, piece, mshape, mshape, jax.ShapeDtypeStruct((1, D_HALF), F32)],
        scratch_shapes=[pltpu.VMEM((tl, D_HALF), F32)],
        compiler_params=_ARB)(proj, proj, dcat, ng, sw, sb_t)


def _s5_params(w):
    prep_args = (w["ssm_log_step"], w["ssm_a_re"], w["ssm_a_im"], w["ssm_b_re"], w["ssm_b_im"])
    (lr, li, br, bi), prep_vjp = jax.vjp(jax.vmap(_s5_prep), *prep_args)
    wb, wc = jax.vmap(_s5_mats)(br, bi, w["ssm_c_re"], w["ssm_c_im"])
    consts = lambda reverse: jax.vmap(functools.partial(_s5_scan_consts, reverse=reverse))(*prep_args[:3])
    return dict(wb=wb.astype(BF16), wc=wc.astype(BF16), d=w["ssm_d"][:, None, :], sc=consts(False),
                sc_rev=consts(True), prep_vjp=prep_vjp)


def _s5_param_grads(s5, dwb, dwc, dlam, dd):
    dbr, dbi, dcr, dci = jax.vmap(_s5_mats_bwd)(dwb, dwc)
    n = dlam.shape[0]
    dlr = dlam[:, :, 0, :S5_STATE_LANES].reshape(n, N_SSM_GROUPS, SSM_STATE)
    dli = dlam[:, :, 0, S5_STATE_LANES:].reshape(n, N_SSM_GROUPS, SSM_STATE)
    dls, dar, dai, db_re, db_im = s5["prep_vjp"]((dlr, dli, dbr, dbi))
    return dict(ssm_log_step=dls, ssm_a_re=dar, ssm_a_im=dai, ssm_b_re=db_re, ssm_b_im=db_im, ssm_c_re=dcr,
                ssm_c_im=dci, ssm_d=dd[:, 0, :])


def _layer_weights(i):
    j = i // 2
    mixer = [("even_w_in", j), ("even_w_out", j), ("ssm_glu_w", j)] if i % 2 == 0 else [("odd_w_in", j),
                                                                                         ("odd_w_out", j)]
    return dict(w_in=mixer[0], w_out=mixer[1], glu=mixer[2:], up=("ffn_w_up", i), down=("ffn_w_down", i))


class _LocalWeights:
    def __init__(self, w):
        self.w, self.grads = w, {}

    def carried_by(self, stage, i):
        return []

    def delivered(self, stage, i, outs):
        pass

    def weight(self, key):
        return self.w[key[0]][key[1]]

    def grad(self, key, dw):
        self.grads[key] = dw


class _ShardedWeights:
    def __init__(self, shards):
        self.shards = shards
        self.full, self.pending, self.scattered = {}, {}, {}

    def start(self, others):
        keys = [_layer_weights(0)["w_in"]]
        outs = _exchange_only(others + [self._gather(k) for k in keys], name="ag_first")
        self._take(keys, outs[len(others):])
        return outs[:len(others)]

    def _gather(self, key):
        return _Gather(self.shards[key[0]][key[1]])

    def _take(self, keys, outs):
        for key, got in zip(keys, outs):
            if BIG[key[0]] == 2:
                self.full[key] = jnp.swapaxes(got, 0, 1).reshape(got.shape[1], -1)
            else:
                self.full[key] = got.reshape(-1, got.shape[2])

    def _plan(self, stage, i):
        cur = _layer_weights(i)
        nxt = _layer_weights(i + 1) if i + 1 < DEPTH else None
        has_scan = lambda k: k % 2 == 0
        none = ([], [])
        return {
            "mm_in": ([cur["w_out"], *cur["glu"]], []) if i == 0 else none,
            "s5_fwd": ([cur["up"]], []),
            "mm_up": ([cur["down"]] + ([nxt["w_in"], nxt["w_out"], *nxt["glu"]] if nxt else []), []),
            "ffn_act": ([nxt["up"]], []) if nxt and not has_scan(i + 1) else none,
            "ffn_act_bwd": ([], [cur["down"]] + ([nxt["up"]] if nxt and not has_scan(i + 1) else [])),
            "s5_bwd": ([], [cur["up"]]),
            "mm_up_dx": ([], [nxt["w_in"], *nxt["glu"]]) if nxt else none,
            "mm_up_dw": ([], [nxt["w_out"]]) if nxt else none,
            "mm_in_dx": ([], [cur["w_out"]]) if i == 0 else none,
            "mm_in_dw": ([], list(cur["glu"])) if i == 0 else none,
        }[stage]

    def _scatter(self, key):
        name, layer = key
        src = self.pending.pop(key)
        layers, rows, cols = self.shards[name].shape
        if name not in self.scattered:
            self.scattered[name] = lax.empty((N_DEV, layers * rows, cols), src.dtype)
        return _Scatter(src, into=self.scattered[name], row0=layer * rows)

    def carried_by(self, stage, i):
        gather, scatter = self._plan(stage, i)
        return [self._gather(k) for k in gather] + [self._scatter(k) for k in scatter]

    def delivered(self, stage, i, outs):
        gather, scatter = self._plan(stage, i)
        self._take(gather, outs[:len(gather)])
        for (name, _), buf in zip(scatter, outs[len(gather):]):
            self.scattered[name] = buf

    def weight(self, key):
        return self.full[key]

    def grad(self, key, dw):
        self.pending[key] = _to_dest_major(dw, BIG[key[0]] - 1).astype(BF16)

    def finish(self, others):
        keys = list(self.pending)
        outs = _exchange_only(others + [self._scatter(k) for k in keys], name="rs_last")
        for (name, _), buf in zip(keys, outs[len(others):]):
            self.scattered[name] = buf
        return outs[:len(others)]


def _device_step(x, tgt, w, comm):
    saved = []
    s5 = _s5_params(w)
    h = _rms_fwd(x, w["norm_mix_g"][0:1], name="rms_fwd")
    for i in range(DEPTH):
        j = i // 2
        lw = _layer_weights(i)
        if i % 2 == 0:
            proj = _carry(comm, "mm_in", i, _mm, h, comm.weight(lw["w_in"]), "nn", name="mm_even_in")
            ya = _sconv_fwd(proj, w["even_conv_w"][j], name="sconv_fwd")
            states, ypre = _carry(comm, "s5_fwd", i, _s5_fwd, proj, s5["wb"][j], s5["wc"][j], s5["d"][j],
                                  s5["sc"][j], name="s5_fwd")
            yb = _glu_fwd(ypre, comm.weight(lw["glu"][0]), w["ssm_glu_b"][j][None, :], name="glu_fwd")
            cat = jnp.concatenate([ya, yb], axis=1)
            mix = (states, ypre)
        else:
            proj = _carry(comm, "mm_in", i, _mm, h, comm.weight(lw["w_in"]), "nn", name="mm_odd_in")
            yc = _pool_fwd(proj, w["pool_w"][j], w["pool_scale"][j][None, :], name="pool_fwd")
            sb_t = jnp.transpose(w["sgu_b"][j])
            yd = _sgu_fwd(proj, w["sgu_norm_g"][j][None, :], w["sgu_w"][j], sb_t, name="sgu_fwd")
            cat = jnp.concatenate([yc, yd], axis=1)
            mix = (sb_t,)
        x1, h2 = _mm(cat, comm.weight(lw["w_out"]), "nn", add=x, norm=(w["norm_ffn_g"], i), name="mm_mix_out")
        up = _carry(comm, "mm_up", i, _mm, h2, comm.weight(lw["up"]), "nn", name="mm_up")
        act = _carry(comm, "ffn_act", i, _ffn_act, up, w["ffn_conv_w"][i], w["ffn_conv_b"][i:i + 1], name="ffn_act")
        if i + 1 < DEPTH:
            x2, h_next = _mm(act, comm.weight(lw["down"]), "nn", add=x1, norm=(w["norm_mix_g"], i + 1),
                             name="mm_down")
        else:
            x2, h_next = _mm(act, comm.weight(lw["down"]), "nn", add=x1, name="mm_down_last"), None
        saved.append((x, h, proj, cat, x1, h2, up, act, mix))
        x, h = x2, h_next

    loss, dx, dxb, dgf = _loss_head(x, w["norm_final_g"][None, :], tgt, name="loss_head")
    per_layer = {}
    s5_grads = []

    def put(name, idx, val):
        per_layer.setdefault(name, {})[idx] = val

    for i in reversed(range(DEPTH)):
        j = i // 2
        lw = _layer_weights(i)
        x0, h, proj, cat, x1, h2, up, act, mix = saved[i]
        dact = _mm(dxb, comm.weight(lw["down"]), "nt", name="mm_down_dx")
        comm.grad(lw["down"], _mm(act, dxb, "tn", out_dtype=BF16, name="mm_down_dw"))
        dup, dcw, dcb = _carry(comm, "ffn_act_bwd", i, _ffn_act_bwd, up, dact, w["ffn_conv_w"][i],
                               w["ffn_conv_b"][i:i + 1], name="ffn_act_bwd")
        put("ffn_conv_w", i, dcw)
        put("ffn_conv_b", i, dcb[0])
        dh2 = _carry(comm, "mm_up_dx", i, _mm, dup, comm.weight(lw["up"]), "nt", tn_cap=512, name="mm_up_dx")
        comm.grad(lw["up"], _carry(comm, "mm_up_dw", i, _mm, h2, dup, "tn", out_dtype=BF16, name="mm_up_dw"))
        dx1, dx1b, dg2 = _rms_bwd(dh2, x1, w["norm_ffn_g"][i:i + 1], dx, name="rms_bwd")
        put("norm_ffn_g", i, dg2[0])
        dcat = _mm(dx1b, comm.weight(lw["w_out"]), "nt", name="mm_mix_out_dx")
        comm.grad(lw["w_out"], _mm(cat, dx1b, "tn", out_dtype=BF16, name="mm_mix_out_dw"))
        if i % 2 == 0:
            states, ypre = mix
            dxa, dba, dca, dcw_a = _sconv_bwd(proj, dcat, w["even_conv_w"][j], name="sconv_bwd")
            put("even_conv_w", j, dcw_a)
            dypre, dwg, dbg = _glu_bwd(dcat, ypre, comm.weight(lw["glu"][0]), w["ssm_glu_b"][j][None, :],
                                       name="glu_bwd")
            comm.grad(lw["glu"][0], dwg)
            put("ssm_glu_b", j, dbg[0])
            du, dwb, dwc, dlam, dd = _carry(comm, "s5_bwd", i, _s5_bwd, proj, dypre, states, s5["wb"][j], s5["wc"][j],
                                            s5["d"][j], s5["sc_rev"][j], name="s5_bwd")
            s5_grads.insert(0, (dwb, dwc, dlam, dd))
            dproj = jnp.concatenate([dxa, dba, dca, du], axis=1)
            dh = _carry(comm, "mm_in_dx", i, _mm, dproj, comm.weight(lw["w_in"]), "nt", name="mm_even_in_dx")
            comm.grad(lw["w_in"], _carry(comm, "mm_in_dw", i, _mm, h, dproj, "tn", out_dtype=BF16,
                                         name="mm_even_in_dw"))
        else:
            (sb_t,) = mix
            dz, dpw, dps = _pool_bwd(proj, dcat, w["pool_w"][j], w["pool_scale"][j][None, :], name="pool_bwd")
            put("pool_w", j, dpw)
            put("pool_scale", j, dps[0])
            dsu, dsv, dsw, dbm, dng = _sgu_bwd(proj, dcat, w["sgu_norm_g"][j][None, :], w["sgu_w"][j], sb_t,
                                               name="sgu_bwd")
            put("sgu_w", j, dsw)
            put("sgu_b", j, jnp.sum(dbm, axis=-1))
            put("sgu_norm_g", j, dng[0])
            dproj = jnp.concatenate([dz, dsu, dsv], axis=1)
            dh = _carry(comm, "mm_in_dx", i, _mm, dproj, comm.weight(lw["w_in"]), "nt", name="mm_odd_in_dx")
            comm.grad(lw["w_in"], _carry(comm, "mm_in_dw", i, _mm, h, dproj, "tn", out_dtype=BF16,
                                         name="mm_odd_in_dw"))
        dx, dxb, dg1 = _rms_bwd(dh, x0, w["norm_mix_g"][i:i + 1], dx1, name="rms_bwd")
        put("norm_mix_g", i, dg1[0])

    grads = {nm: jnp.stack([vals[k] for k in sorted(vals)]) for nm, vals in per_layer.items()}
    grads["norm_final_g"] = dgf[0]
    grads.update(_s5_param_grads(s5, *[jnp.stack(parts) for parts in zip(*s5_grads)]))
    return loss, dx, grads


def _carry(comm, stage, i, fn, *args, **kwargs):
    exchanges = comm.carried_by(stage, i)
    if not exchanges:
        return fn(*args, **kwargs)
    out, moved = fn(*args, exchanges=exchanges, **kwargs)
    comm.delivered(stage, i, moved)
    return out


def _sum_parts(parts, *, name):
    g, r, c = parts.shape
    tr = _pick_tile(r, max(16, EXCHANGE_BLOCK_ELEMS // c), 16)

    def body(p_ref, o_ref):
        acc = p_ref[0].astype(F32)
        for k in range(1, g):
            acc = acc + p_ref[k].astype(F32)
        o_ref[...] = acc

    return pl.pallas_call(
        body, name=name, grid=(r // tr,), in_specs=[pl.BlockSpec((g, tr, c), lambda i: (0, i, 0))],
        out_specs=pl.BlockSpec((tr, c), lambda i: (i, 0)), out_shape=jax.ShapeDtypeStruct((r, c), F32),
        compiler_params=_PAR)(parts)


def _adamw(w, m, v, g_parts, *, name):
    r, c = w.shape
    g = g_parts.shape[0]
    tc = _pick_tile(c, 8192, LANE)
    tr = _pick_tile(r, max(16, (1 << 18) // tc), 16)
    c1 = 1.0 - ADAM_B1 ** ADAM_STEP
    c2 = 1.0 - ADAM_B2 ** ADAM_STEP

    def body(w_ref, m_ref, v_ref, g_ref, go_ref, d_ref, mo_ref, vo_ref):
        grad = g_ref[0].astype(F32)
        for k in range(1, g):
            grad = grad + g_ref[k].astype(F32)
        m_new = ADAM_B1 * m_ref[...] + (1.0 - ADAM_B1) * grad
        v_new = ADAM_B2 * v_ref[...] + (1.0 - ADAM_B2) * (grad * grad)
        go_ref[...] = grad
        mo_ref[...] = m_new
        vo_ref[...] = v_new
        d_ref[...] = -ADAM_LR * ((m_new / c1) / (jnp.sqrt(v_new / c2) + ADAM_EPS) + ADAM_WD * w_ref[...])

    blk = pl.BlockSpec((tr, tc), lambda i, j: (i, j))
    out = jax.ShapeDtypeStruct((r, c), F32)
    return pl.pallas_call(
        body, name=name, grid=(r // tr, c // tc),
        in_specs=[blk, blk, blk, pl.BlockSpec((g, tr, tc), lambda i, j: (0, i, j))],
        out_specs=[blk, blk, blk, blk], out_shape=[out, out, out, out], compiler_params=_PAR2)(w, m, v, g_parts)


WEIGHT_NAMES = ['norm_mix_g', 'even_w_in', 'even_conv_w', 'ssm_log_step', 'ssm_a_re', 'ssm_a_im', 'ssm_b_re',
                'ssm_b_im', 'ssm_c_re', 'ssm_c_im', 'ssm_d', 'ssm_glu_w', 'ssm_glu_b', 'even_w_out', 'odd_w_in',
                'pool_w', 'pool_scale', 'sgu_norm_g', 'sgu_w', 'sgu_b', 'odd_w_out', 'norm_ffn_g', 'ffn_w_up',
                'ffn_conv_w', 'ffn_conv_b', 'ffn_w_down', 'norm_final_g']
BIG = {'even_w_in': 2, 'ssm_glu_w': 1, 'even_w_out': 1, 'odd_w_in': 2, 'odd_w_out': 1, 'ffn_w_up': 2,
       'ffn_w_down': 1}
SMALL = {'even_conv_w': 2, 'pool_scale': 1, 'sgu_norm_g': 1, 'ffn_conv_w': 2}
BIG_ROWS = 512
SMALL_ROWS = 16


def _pad_to(n, q):
    return -(-n // q) * q


def _pack(arrays, dtype, rows, lead=()):
    flat = [a.reshape(lead + (-1,)).astype(dtype) for a in arrays]
    n = sum(f.shape[-1] for f in flat)
    pad = _pad_to(n, rows * LANE) - n
    if pad:
        flat.append(jnp.zeros(lead + (pad,), dtype))
    return jnp.concatenate(flat, axis=-1).reshape(lead + (rows, -1))


def _unpack(buf, shapes, lead=()):
    flat = buf.reshape(lead + (-1,))
    out, off = [], 0
    for shp in shapes:
        n = math.prod(shp)
        out.append(flat[..., off:off + n].reshape(lead + tuple(shp)))
        off += n
    return out


def _to_dest_major(full, axis):
    shp = full.shape
    split = full.reshape(shp[:axis] + (N_DEV, shp[axis] // N_DEV) + shp[axis + 1:])
    return jnp.moveaxis(split, axis, 0)


def _from_dest_major(blocks, axis):
    moved = jnp.moveaxis(blocks, 0, axis)
    shp = moved.shape
    return moved.reshape(shp[:axis] + (shp[axis] * shp[axis + 1],) + shp[axis + 2:])


def _rows_2d(a):
    return a.reshape(-1, a.shape[-1])


def _rows_2d_lead(a):
    return a.reshape(a.shape[0], -1, a.shape[-1])


def kernel(x, norm_mix_g, even_w_in, even_conv_w, ssm_log_step, ssm_a_re, ssm_a_im, ssm_b_re, ssm_b_im, ssm_c_re, ssm_c_im, ssm_d, ssm_glu_w, ssm_glu_b, even_w_out, odd_w_in, pool_w, pool_scale, sgu_norm_g, sgu_w, sgu_b, odd_w_out, norm_ffn_g, ffn_w_up, ffn_conv_w, ffn_conv_b, ffn_w_down, norm_final_g, loss_target, m_norm_mix_g, m_even_w_in, m_even_conv_w, m_ssm_log_step, m_ssm_a_re, m_ssm_a_im, m_ssm_b_re, m_ssm_b_im, m_ssm_c_re, m_ssm_c_im, m_ssm_d, m_ssm_glu_w, m_ssm_glu_b, m_even_w_out, m_odd_w_in, m_pool_w, m_pool_scale, m_sgu_norm_g, m_sgu_w, m_sgu_b, m_odd_w_out, m_norm_ffn_g, m_ffn_w_up, m_ffn_conv_w, m_ffn_conv_b, m_ffn_w_down, m_norm_final_g, v_norm_mix_g, v_even_w_in, v_even_conv_w, v_ssm_log_step, v_ssm_a_re, v_ssm_a_im, v_ssm_b_re, v_ssm_b_im, v_ssm_c_re, v_ssm_c_im, v_ssm_d, v_ssm_glu_w, v_ssm_glu_b, v_even_w_out, v_odd_w_in, v_pool_w, v_pool_scale, v_sgu_norm_g, v_sgu_w, v_sgu_b, v_odd_w_out, v_norm_ffn_g, v_ffn_w_up, v_ffn_conv_w, v_ffn_conv_b, v_ffn_w_down, v_norm_final_g):
    given = dict(locals())
    wts = {n: given[n] for n in WEIGHT_NAMES}
    mom = {n: given["m_" + n] for n in WEIGHT_NAMES}
    var = {n: given["v_" + n] for n in WEIGHT_NAMES}
    repl = [n for n in WEIGHT_NAMES if n not in BIG and n not in SMALL]

    small_shapes = [wts[n].shape for n in SMALL]
    comm = _ShardedWeights({n: wts[n].astype(BF16) for n in BIG})
    (small_all,) = comm.start([_Gather(_pack([wts[n] for n in SMALL], F32, SMALL_ROWS))])
    full = {n: wts[n] for n in repl}
    for n, blocks in zip(SMALL, _unpack(small_all, small_shapes, lead=(N_DEV,))):
        full[n] = _from_dest_major(blocks, SMALL[n])

    loss, dx, grads = _device_step(x[0], loss_target[0], full, comm)

    repl_shapes = [wts[n].shape for n in repl]
    repl_flat = jnp.concatenate([grads[n].reshape(-1) for n in repl] + [loss.reshape(-1)])
    n_repl = repl_flat.shape[0]
    chunk = _pad_to(-(-n_repl // N_DEV), SMALL_ROWS * LANE)
    repl_flat = jnp.pad(repl_flat, (0, N_DEV * chunk - n_repl))
    small_part = _pack([_to_dest_major(grads[n], SMALL[n]) for n in SMALL], F32, SMALL_ROWS, lead=(N_DEV,))
    small_cols = small_part.shape[2]
    (small_rs,) = comm.finish([_Scatter(
        jnp.concatenate([small_part, repl_flat.reshape(N_DEV, SMALL_ROWS, chunk // SMALL_ROWS)], axis=2))])
    small_sum = _sum_parts(small_rs, name="rs_sum_small")
    (repl_all,) = _exchange_only([_Gather(small_sum[:, small_cols:])], name="ag_repl")
    repl_sum = repl_all.reshape(-1)
    total_loss = repl_sum[n_repl - 1]

    out = {}
    for n in BIG:
        shp = wts[n].shape
        res = _adamw(_rows_2d(wts[n]), _rows_2d(mom[n]), _rows_2d(var[n]), comm.scattered[n], name="adamw_" + n)
        out[n] = [r.reshape(shp) for r in res]

    def small_vec(shard_part, repl_part):
        flat = jnp.concatenate([shard_part.reshape(-1), repl_part])
        return flat.reshape(SMALL_ROWS, -1)

    def small_tree(tree):
        tail = jnp.concatenate([tree[n].reshape(-1) for n in repl])
        tail = jnp.pad(tail, (0, N_DEV * chunk - tail.shape[0]))
        return small_vec(_pack([tree[n] for n in SMALL], F32, SMALL_ROWS), tail)

    res = _adamw(small_tree(wts), small_tree(mom), small_tree(var), small_vec(small_sum[:, :small_cols], repl_sum)[None],
                 name="adamw_small")
    n_small = SMALL_ROWS * small_cols
    for k, r in enumerate(res):
        flat = r.reshape(-1)
        shard = _unpack(flat[:n_small], small_shapes)
        rest = _unpack(flat[n_small:], repl_shapes)
        for n, val in zip(SMALL, shard):
            out.setdefault(n, [None] * 4)[k] = val
        for n, val in zip(repl, rest):
            out.setdefault(n, [None] * 4)[k] = val

    grad_x = dx[None]
    return (total_loss, grad_x, *[out[n][0] for n in WEIGHT_NAMES], *[out[n][1] for n in WEIGHT_NAMES],
            *[out[n][2] for n in WEIGHT_NAMES], *[out[n][3] for n in WEIGHT_NAMES])
```

```python
import functools
import math

import jax
import jax.numpy as jnp
from jax import lax
from jax.experimental import pallas as pl
from jax.experimental.pallas import tpu as pltpu

F32 = jnp.float32
BF16 = jnp.bfloat16

D_MODEL = 1024
DEPTH = 4
D_HALF = D_MODEL // 2
SSM_GROUP = 16
N_SSM_GROUPS = D_HALF // SSM_GROUP
SSM_STATE = 64
POOL_WINDOWS = (2, 4, 8, 16)
SGU_HEADS = 4
CHUNK = 128
D_FF = 2816
CONV_WIDTH = 3
EPS = 1e-6
N_DEV = 8

ADAM_LR = 0.001
ADAM_B1 = 0.9
ADAM_B2 = 0.999
ADAM_EPS = 1e-08
ADAM_WD = 0.01
ADAM_STEP = 10

LANE = 128
SUBLANE = 8
S5_LANE_BLOCKS = D_HALF // LANE
S5_STATE_LANES = (N_SSM_GROUPS // S5_LANE_BLOCKS) * SSM_STATE
S5_TIME_CHUNK = 512
EXCHANGE_BLOCK_ELEMS = 1 << 20

GELU_K = math.sqrt(2.0 / math.pi)
GELU_C = 0.044715

_ARB = pltpu.CompilerParams(dimension_semantics=("arbitrary",))
_ARB2 = pltpu.CompilerParams(dimension_semantics=("arbitrary", "arbitrary"))
_PAR = pltpu.CompilerParams(dimension_semantics=("parallel",))
_PAR2 = pltpu.CompilerParams(dimension_semantics=("parallel", "parallel"))


def _pick_tile(n, cap, mult):
    if n <= cap:
        return n
    best = None
    for t in range(mult, cap + 1, mult):
        if n % t == 0:
            best = t
    assert best is not None, (n, cap, mult)
    return best


_MESH = pl.DeviceIdType.MESH
_ANY = pl.BlockSpec(memory_space=pl.ANY)
SEMS_PER_EXCHANGE = N_DEV - 1


class _Gather:
    into = None

    def __init__(self, src):
        self.src = src
        self.out_shape = jax.ShapeDtypeStruct((N_DEV,) + src.shape, src.dtype)

    def copies(self, x_ref, out_ref, send_sems, recv_sems, local_sem):
        x, y, cc = lax.axis_index("x"), lax.axis_index("y"), lax.axis_index("c")
        me, sibling = (x, y, cc), (x, y, 1 - cc)
        chips = [(1 - x, y), (x, 1 - y), (1 - x, 1 - y)]

        def rows(px, py, pc):
            return out_ref.at[4 * px + 2 * py + pc]

        def copy(k, block, to, src=None):
            return pltpu.make_async_remote_copy(
                src_ref=rows(*block) if src is None else src, dst_ref=rows(*block),
                send_sem=send_sems.at[k], recv_sem=recv_sems.at[k], device_id=to, device_id_type=_MESH)

        return dict(
            mine=pltpu.make_async_copy(x_ref, rows(*me), local_sem),
            first=[copy(0, me, sibling, src=x_ref)] + [copy(1 + k, me, (*chip, cc), src=x_ref)
                                                       for k, chip in enumerate(chips)],
            passed=[copy(4 + k, (*chip, cc), sibling) for k, chip in enumerate(chips)],
            over_ici=[copy(1 + k, (*chip, cc), me) for k, chip in enumerate(chips)],
            from_sibling=[copy(0, sibling, me)] + [copy(4 + k, (*chip, 1 - cc), me) for k, chip in enumerate(chips)])

    def start(self, *refs):
        cps = self.copies(*refs)
        cps["mine"].start()
        for cp in cps["first"]:
            cp.start()

    def finish(self, *refs):
        cps = self.copies(*refs)
        for arrived, onward in zip(cps["over_ici"], cps["passed"]):
            arrived.wait_recv()
            onward.start()
        for arrived in cps["from_sibling"]:
            arrived.wait_recv()
        for cp in cps["first"] + cps["passed"]:
            cp.wait_send()
        cps["mine"].wait()


class _Scatter:
    def __init__(self, src, into=None, row0=0):
        self.src, self.into, self.row0 = src, into, row0
        whole = src if into is None else into
        self.out_shape = jax.ShapeDtypeStruct(whole.shape, whole.dtype)

    def copies(self, p_ref, whole_ref, send_sems, recv_sems, local_sem):
        x, y, cc = lax.axis_index("x"), lax.axis_index("y"), lax.axis_index("c")
        me = 4 * x + 2 * y + cc
        rows = self.src.shape[1]
        out_ref = whole_ref if self.into is None else whole_ref.at[:, pl.ds(self.row0, rows)]
        sends, arrivals = [], []
        for k in range(1, N_DEV):
            px = (1 - x) if k & 4 else x
            py = (1 - y) if k & 2 else y
            pc = (1 - cc) if k & 1 else cc
            peer = 4 * px + 2 * py + pc
            kw = dict(send_sem=send_sems.at[k - 1], recv_sem=recv_sems.at[k - 1], device_id=(px, py, pc),
                      device_id_type=_MESH)
            sends.append(pltpu.make_async_remote_copy(src_ref=p_ref.at[peer], dst_ref=out_ref.at[me], **kw))
            arrivals.append(pltpu.make_async_remote_copy(src_ref=p_ref.at[me], dst_ref=out_ref.at[peer], **kw))
        return dict(mine=pltpu.make_async_copy(p_ref.at[me], out_ref.at[me], local_sem), sends=sends,
                    arrivals=arrivals)

    def start(self, *refs):
        cps = self.copies(*refs)
        cps["mine"].start()
        for cp in cps["sends"]:
            cp.start()

    def finish(self, *refs):
        cps = self.copies(*refs)
        for cp in cps["arrivals"]:
            cp.wait_recv()
        for cp in cps["sends"]:
            cp.wait_send()
        cps["mine"].wait()


class _SemView:
    def __init__(self, ref, lo):
        self.ref, self.lo = ref, lo

    @property
    def at(self):
        return self

    def __getitem__(self, k):
        return self.ref.at[self.lo + k]


def _call(body, *, name, grid, in_specs, out_specs, out_shape, args, scratch_shapes=(), parallel=True, exchanges=()):
    n_axes = len(grid)
    if not exchanges:
        sem = ("parallel" if parallel else "arbitrary",) * n_axes
        return pl.pallas_call(
            body, name=name, grid=grid, in_specs=in_specs, out_specs=out_specs, out_shape=out_shape,
            scratch_shapes=scratch_shapes, compiler_params=pltpu.CompilerParams(dimension_semantics=sem))(*args)
    single = not isinstance(out_shape, (list, tuple))
    out_specs = [out_specs] if single else list(out_specs)
    out_shape = [out_shape] if single else list(out_shape)
    n_in, n_out, n_scr, n_x = len(in_specs), len(out_specs), len(scratch_shapes), len(exchanges)
    landing = [(e, ex.into) for e, ex in enumerate(exchanges) if ex.into is not None]
    aliases = {n_in + n_x + pos: n_out + e for pos, (e, _) in enumerate(landing)}

    def wrapped(*refs):
        ins, refs = refs[:n_in], refs[n_in:]
        x_in, refs = refs[:n_x], refs[n_x + len(landing):]
        outs, refs = refs[:n_out], refs[n_out:]
        x_out, refs = refs[:n_x], refs[n_x:]
        scr, (send_sems, recv_sems, local_sems) = refs[:n_scr], refs[n_scr:]
        ids = [pl.program_id(k) for k in range(n_axes)]
        first = functools.reduce(jnp.logical_and, [i == 0 for i in ids])
        last = functools.reduce(jnp.logical_and, [i == g - 1 for i, g in zip(ids, grid)])

        def sems(e):
            lo = e * SEMS_PER_EXCHANGE
            return _SemView(send_sems, lo), _SemView(recv_sems, lo), local_sems.at[e]

        @pl.when(first)
        def _():
            for e, ex in enumerate(exchanges):
                ex.start(x_in[e], x_out[e], *sems(e))

        body(*ins, *outs, *scr)

        @pl.when(last)
        def _():
            for e, ex in enumerate(exchanges):
                ex.finish(x_in[e], x_out[e], *sems(e))

    res = pl.pallas_call(
        wrapped, name=name, grid=grid, in_specs=list(in_specs) + [_ANY] * (n_x + len(landing)),
        out_specs=out_specs + [_ANY] * n_x, out_shape=out_shape + [ex.out_shape for ex in exchanges],
        input_output_aliases=aliases,
        scratch_shapes=list(scratch_shapes) + [pltpu.SemaphoreType.DMA((n_x * SEMS_PER_EXCHANGE,)),
                                               pltpu.SemaphoreType.DMA((n_x * SEMS_PER_EXCHANGE,)),
                                               pltpu.SemaphoreType.DMA((n_x,))],
        compiler_params=pltpu.CompilerParams(dimension_semantics=("arbitrary",) * n_axes),
    )(*args, *[ex.src for ex in exchanges], *[buf for _, buf in landing])
    outs, x_outs = res[:n_out], res[n_out:]
    return (outs[0] if single else outs), x_outs


def _exchange_only(exchanges, *, name):
    def body():
        pass

    return _call(body, name=name, grid=(1,), in_specs=[], out_specs=[], out_shape=[], args=[],
                 exchanges=exchanges)[1]


def _shift_dn(x, d):
    rolled = pltpu.roll(x, d, 0)
    if x.shape[0] <= SUBLANE or d >= SUBLANE:
        row = lax.broadcasted_iota(jnp.int32, x.shape, 0)
        return jnp.where(row >= d, rolled, 0.0)
    row = lax.broadcasted_iota(jnp.int32, (SUBLANE, x.shape[1]), 0)
    return jnp.concatenate([jnp.where(row >= d, rolled[:SUBLANE], 0.0), rolled[SUBLANE:]], axis=0)


def _shift_up(x, d):
    n = x.shape[0]
    rolled = pltpu.roll(x, n - d, 0)
    if n <= SUBLANE or d >= SUBLANE:
        row = lax.broadcasted_iota(jnp.int32, x.shape, 0)
        return jnp.where(row < n - d, rolled, 0.0)
    row = lax.broadcasted_iota(jnp.int32, (SUBLANE, x.shape[1]), 0)
    return jnp.concatenate([rolled[:n - SUBLANE], jnp.where(row < SUBLANE - d, rolled[n - SUBLANE:], 0.0)], axis=0)


def _gelu(x):
    return 0.5 * x * (1.0 + jnp.tanh(GELU_K * (x + GELU_C * x * x * x)))


def _gelu_grad(x):
    t = jnp.tanh(GELU_K * (x + GELU_C * x * x * x))
    return 0.5 * (1.0 + t) + 0.5 * x * (1.0 - t * t) * (GELU_K * (1.0 + 3.0 * GELU_C * x * x))


def _sigmoid(x):
    return 1.0 / (1.0 + jnp.exp(-x))


def _conv3(x, w_ref):
    return w_ref[0:1, :] * _shift_dn(x, 2) + w_ref[1:2, :] * _shift_dn(x, 1) + w_ref[2:3, :] * x


def _conv3_bwd_x(dy, w_ref):
    return w_ref[2:3, :] * dy + w_ref[1:2, :] * _shift_up(dy, 1) + w_ref[0:1, :] * _shift_up(dy, 2)


def _conv3_bwd_w(dy, x):
    return jnp.concatenate([
        jnp.sum(dy * _shift_dn(x, 2), axis=0, keepdims=True),
        jnp.sum(dy * _shift_dn(x, 1), axis=0, keepdims=True),
        jnp.sum(dy * x, axis=0, keepdims=True)], axis=0)


def _dot(a, b, dims):
    return lax.dot_general(a.astype(BF16), b.astype(BF16), (dims, ((), ())), preferred_element_type=F32)


_NN = ((1,), (0,))
_NT = ((1,), (1,))
_TN = ((0,), (0,))


def _mm(a, b, mode, *, name, out_dtype=F32, add=None, norm=None, norm_bwd=None, tm_cap=512, tn_cap=1536,
        exchanges=()):
    if mode == "tn":
        r, m = a.shape
        n = b.shape[1]
        tm, tn = _pick_tile(m, 256, LANE), _pick_tile(n, tn_cap, LANE)
        in_specs = [pl.BlockSpec((r, tm), lambda i, j: (0, i)), pl.BlockSpec((r, tn), lambda i, j: (0, j))]
        dims = _TN
    elif mode == "nn":
        m, k = a.shape
        n = b.shape[1]
        tm, tn = _pick_tile(m, tm_cap, SUBLANE), _pick_tile(n, tn_cap, LANE)
        in_specs = [pl.BlockSpec((tm, k), lambda i, j: (i, 0)), pl.BlockSpec((k, tn), lambda i, j: (0, j))]
        dims = _NN
    else:
        m, k = a.shape
        n = b.shape[0]
        tm, tn = _pick_tile(m, tm_cap, SUBLANE), _pick_tile(n, tn_cap, LANE)
        in_specs = [pl.BlockSpec((tm, k), lambda i, j: (i, 0)), pl.BlockSpec((tn, k), lambda i, j: (j, 0))]
        dims = _NT
    args = [a, b]
    tile = pl.BlockSpec((tm, tn), lambda i, j: (i, j))
    if add is not None:
        in_specs.append(tile)
        args.append(add)
    out_specs, out_shape = tile, jax.ShapeDtypeStruct((m, n), out_dtype)
    if norm is not None:
        gains, layer = norm
        assert tn == n
        in_specs.append(pl.BlockSpec((None, 1, n), lambda i, j: (layer, 0, 0)))
        args.append(gains.reshape(gains.shape[0], 1, n))
        out_specs, out_shape = [tile, tile], [out_shape, jax.ShapeDtypeStruct((m, n), BF16)]
    if norm_bwd is not None:
        x_in, gains, layer, res = norm_bwd
        assert tn == n and add is None and norm is None
        vec = pl.BlockSpec((None, 1, n), lambda i, j: (layer, 0, 0))
        in_specs += [tile, vec, tile]
        args += [x_in, gains.reshape(gains.shape[0], 1, n), res]
        out_specs = [tile, tile, pl.BlockSpec((1, n), lambda i, j: (0, 0))]
        out_shape = [jax.ShapeDtypeStruct((m, n), F32), jax.ShapeDtypeStruct((m, n), BF16),
                     jax.ShapeDtypeStruct((1, n), F32)]

    def body(*refs):
        acc = _dot(refs[0][...], refs[1][...], dims)
        if add is not None:
            acc = acc + refs[2][...]
        if norm_bwd is not None:
            x_ref, g_ref, res_ref, dx_ref, dxb_ref, dg_ref = refs[2:]

            @pl.when(pl.program_id(0) == 0)
            def _():
                dg_ref[...] = jnp.zeros_like(dg_ref)
            dx, xn = _rms_bwd_rows(acc, x_ref[...], g_ref[...])
            dx = dx + res_ref[...]
            dx_ref[...] = dx
            dxb_ref[...] = dx.astype(BF16)
            dg_ref[...] += jnp.sum(acc * xn, axis=0, keepdims=True)
        elif norm is None:
            refs[-1][...] = acc.astype(out_dtype)
        else:
            refs[-2][...] = acc.astype(out_dtype)
            r = lax.rsqrt(jnp.mean(acc * acc, axis=-1, keepdims=True) + EPS)
            refs[-1][...] = (acc * r * refs[-3][...]).astype(BF16)

    return _call(body, name=name, grid=(m // tm, n // tn), in_specs=in_specs, out_specs=out_specs,
                 out_shape=out_shape, args=args, parallel=norm_bwd is None, exchanges=exchanges)


def _rms_fwd(x, g, *, name):
    l, d = x.shape
    tl = _pick_tile(l, 512, SUBLANE)

    def body(x_ref, g_ref, h_ref):
        xv = x_ref[...]
        r = lax.rsqrt(jnp.mean(xv * xv, axis=-1, keepdims=True) + EPS)
        h_ref[...] = (xv * r * g_ref[...]).astype(BF16)

    return pl.pallas_call(
        body, name=name, grid=(l // tl,),
        in_specs=[pl.BlockSpec((tl, d), lambda i: (i, 0)), pl.BlockSpec((1, d), lambda i: (0, 0))],
        out_specs=pl.BlockSpec((tl, d), lambda i: (i, 0)),
        out_shape=jax.ShapeDtypeStruct((l, d), BF16), compiler_params=_PAR)(x, g)


def _rms_bwd_rows(dh, xv, g):
    r = lax.rsqrt(jnp.mean(xv * xv, axis=-1, keepdims=True) + EPS)
    a = dh * g
    m = jnp.mean(a * xv, axis=-1, keepdims=True)
    return r * a - xv * (r * r * r) * m, xv * r


def _rms_bwd(dh, x, g, res, *, name):
    l, d = x.shape
    tl = _pick_tile(l, 512, SUBLANE)

    def body(dh_ref, x_ref, g_ref, res_ref, dx_ref, dxb_ref, dg_ref):
        @pl.when(pl.program_id(0) == 0)
        def _():
            dg_ref[...] = jnp.zeros_like(dg_ref)
        dhv = dh_ref[...]
        dx, xn = _rms_bwd_rows(dhv, x_ref[...], g_ref[...])
        dx = dx + res_ref[...]
        dx_ref[...] = dx
        dxb_ref[...] = dx.astype(BF16)
        dg_ref[...] += jnp.sum(dhv * xn, axis=0, keepdims=True)

    row = pl.BlockSpec((tl, d), lambda i: (i, 0))
    vec = pl.BlockSpec((1, d), lambda i: (0, 0))
    return pl.pallas_call(
        body, name=name, grid=(l // tl,), in_specs=[row, row, vec, row], out_specs=[row, row, vec],
        out_shape=[jax.ShapeDtypeStruct((l, d), F32), jax.ShapeDtypeStruct((l, d), BF16),
                   jax.ShapeDtypeStruct((1, d), F32)],
        compiler_params=_ARB)(dh, x, g, res)


def _loss_head(x, g, tgt, *, name):
    l, d = x.shape
    tl = _pick_tile(l, 512, SUBLANE)

    def body(x_ref, g_ref, t_ref, loss_ref, dx_ref, dxb_ref, dg_ref):
        @pl.when(pl.program_id(0) == 0)
        def _():
            dg_ref[...] = jnp.zeros_like(dg_ref)
            loss_ref[...] = jnp.zeros_like(loss_ref)
        xv = x_ref[...]
        gv = g_ref[...]
        r = lax.rsqrt(jnp.mean(xv * xv, axis=-1, keepdims=True) + EPS)
        err = xv * r * gv - t_ref[...]
        row_loss = jnp.sum(err * err, axis=-1, keepdims=True) * (0.5 / d)
        loss_ref[...] += jnp.sum(row_loss, axis=0, keepdims=True)
        dy = err * (1.0 / d)
        dx, xn = _rms_bwd_rows(dy, xv, gv)
        dx_ref[...] = dx
        dxb_ref[...] = dx.astype(BF16)
        dg_ref[...] += jnp.sum(dy * xn, axis=0, keepdims=True)

    row = pl.BlockSpec((tl, d), lambda i: (i, 0))
    vec = pl.BlockSpec((1, d), lambda i: (0, 0))
    one = pl.BlockSpec((1, 1), lambda i: (0, 0))
    return pl.pallas_call(
        body, name=name, grid=(l // tl,), in_specs=[row, vec, row], out_specs=[one, row, row, vec],
        out_shape=[jax.ShapeDtypeStruct((1, 1), F32), jax.ShapeDtypeStruct((l, d), F32),
                   jax.ShapeDtypeStruct((l, d), BF16), jax.ShapeDtypeStruct((1, d), F32)],
        compiler_params=_ARB)(x, g, tgt)


def _ffn_act(up, cw, cb, *, name, exchanges=()):
    l = up.shape[0]
    nb = D_FF // LANE

    def body(ug_ref, uv_ref, wg_ref, wv_ref, bg_ref, bv_ref, o_ref):
        gc = _conv3(ug_ref[...], wg_ref) + bg_ref[...]
        vc = _conv3(uv_ref[...], wv_ref) + bv_ref[...]
        o_ref[...] = (gc * _sigmoid(gc) * vc).astype(BF16)

    col = lambda off: pl.BlockSpec((l, LANE), lambda j: (0, j + off))
    w3 = lambda off: pl.BlockSpec((CONV_WIDTH, LANE), lambda j: (0, j + off))
    b1 = lambda off: pl.BlockSpec((1, LANE), lambda j: (0, j + off))
    return _call(body, name=name, grid=(nb,), in_specs=[col(0), col(nb), w3(0), w3(nb), b1(0), b1(nb)],
                 out_specs=col(0), out_shape=jax.ShapeDtypeStruct((l, D_FF), BF16),
                 args=[up, up, cw, cw, cb, cb], exchanges=exchanges)


def _ffn_act_bwd(up, dx, w_down, cw, cb, *, name, exchanges=()):
    l, d = dx.shape
    nb = D_FF // LANE

    def body(ug_ref, uv_ref, own_ref, dx_ref, wd_ref, wg_ref, wv_ref, wo_ref, bg_ref, bv_ref,
             dup_ref, dcw_ref, dcb_ref, dc_scr):
        is_gate = pl.program_id(0) < nb
        dact = _dot(dx_ref[...], wd_ref[...], _NT)
        gc = _conv3(ug_ref[...], wg_ref) + bg_ref[...]
        sg = _sigmoid(gc)

        @pl.when(is_gate)
        def _():
            vc = _conv3(uv_ref[...], wv_ref) + bv_ref[...]
            dc_scr[...] = dact * vc * (sg * (1.0 + gc * (1.0 - sg)))

        @pl.when(jnp.logical_not(is_gate))
        def _():
            dc_scr[...] = dact * (gc * sg)

        dc = dc_scr[...]
        dcb_ref[...] = jnp.sum(dc, axis=0, keepdims=True)
        dcw_ref[...] = _conv3_bwd_w(dc, own_ref[...])
        dup_ref[...] = _conv3_bwd_x(dc, wo_ref).astype(BF16)

    half = lambda j: j % nb
    colg = pl.BlockSpec((l, LANE), lambda j: (0, half(j)))
    colv = pl.BlockSpec((l, LANE), lambda j: (0, half(j) + nb))
    colo = pl.BlockSpec((l, LANE), lambda j: (0, j))
    w3g = pl.BlockSpec((CONV_WIDTH, LANE), lambda j: (0, half(j)))
    w3v = pl.BlockSpec((CONV_WIDTH, LANE), lambda j: (0, half(j) + nb))
    w3o = pl.BlockSpec((CONV_WIDTH, LANE), lambda j: (0, j))
    b1g = pl.BlockSpec((1, LANE), lambda j: (0, half(j)))
    b1v = pl.BlockSpec((1, LANE), lambda j: (0, half(j) + nb))
    b1o = pl.BlockSpec((1, LANE), lambda j: (0, j))
    return _call(
        body, name=name, grid=(2 * nb,),
        in_specs=[colg, colv, colo, pl.BlockSpec((l, d), lambda j: (0, 0)),
                  pl.BlockSpec((LANE, d), lambda j: (half(j), 0)), w3g, w3v, w3o, b1g, b1v],
        out_specs=[colo, w3o, b1o],
        out_shape=[jax.ShapeDtypeStruct((l, 2 * D_FF), BF16), jax.ShapeDtypeStruct((CONV_WIDTH, 2 * D_FF), F32),
                   jax.ShapeDtypeStruct((1, 2 * D_FF), F32)],
        scratch_shapes=[pltpu.VMEM((l, LANE), F32)],
        args=[up, up, up, dx, w_down, cw, cw, cw, cb, cb], exchanges=exchanges)


def _sconv_fwd(proj, cw, *, name):
    l = proj.shape[0]
    nb = D_HALF // LANE

    def body(xa_ref, ba_ref, ca_ref, w_ref, o_ref):
        o_ref[...] = (ba_ref[...] * _conv3(ca_ref[...] * xa_ref[...], w_ref)).astype(BF16)

    col = lambda off: pl.BlockSpec((l, LANE), lambda j: (0, j + off))
    return pl.pallas_call(
        body, name=name, grid=(nb,),
        in_specs=[col(0), col(nb), col(2 * nb), pl.BlockSpec((CONV_WIDTH, LANE), lambda j: (0, j))],
        out_specs=col(0), out_shape=jax.ShapeDtypeStruct((l, D_HALF), BF16),
        compiler_params=_PAR)(proj, proj, proj, cw)


def _sconv_bwd(proj, dcat, cw, *, name):
    l = proj.shape[0]
    nb = D_HALF // LANE

    def body(xa_ref, ba_ref, ca_ref, dy_ref, w_ref, dxa_ref, dba_ref, dca_ref, dw_ref):
        xa, ba, ca, dy = xa_ref[...], ba_ref[...], ca_ref[...], dy_ref[...]
        q = ca * xa
        dba_ref[...] = (dy * _conv3(q, w_ref)).astype(BF16)
        dconv = dy * ba
        dw_ref[...] = _conv3_bwd_w(dconv, q)
        dq = _conv3_bwd_x(dconv, w_ref)
        dxa_ref[...] = (dq * ca).astype(BF16)
        dca_ref[...] = (dq * xa).astype(BF16)

    col = lambda off: pl.BlockSpec((l, LANE), lambda j: (0, j + off))
    w3 = pl.BlockSpec((CONV_WIDTH, LANE), lambda j: (0, j))
    piece = jax.ShapeDtypeStruct((l, D_HALF), BF16)
    return pl.pallas_call(
        body, name=name, grid=(nb,),
        in_specs=[col(0), col(nb), col(2 * nb), col(0), w3],
        out_specs=[col(0), col(0), col(0), w3],
        out_shape=[piece, piece, piece, jax.ShapeDtypeStruct((CONV_WIDTH, D_HALF), F32)],
        compiler_params=_PAR)(proj, proj, proj, dcat, cw)


def _s5_prep(log_step, a_re, a_im, b_re, b_im):
    step = jnp.exp(log_step)[:, None]
    mag = jnp.exp(a_re * step)
    lr = mag * jnp.cos(a_im * step)
    li = mag * jnp.sin(a_im * step)
    nr = lr - 1.0
    den = a_re * a_re + a_im * a_im
    qr = (nr * a_re + li * a_im) / den
    qi = (li * a_re - nr * a_im) / den
    br = qr[..., None] * b_re - qi[..., None] * b_im
    bi = qr[..., None] * b_im + qi[..., None] * b_re
    return lr, li, br, bi


def _block_diag(m):
    nb, ng, r, c = m.shape
    eye = jnp.eye(ng, dtype=m.dtype)
    return jnp.einsum("bgrc,gh->bgrhc", m, eye).reshape(nb, ng * r, ng * c)


def _block_diag_extract(w, r, c):
    nb = w.shape[0]
    ng = w.shape[1] // r
    w5 = w.reshape(nb, ng, r, ng, c)
    return jnp.einsum("bgrhc,gh->bgrc", w5, jnp.eye(ng, dtype=w.dtype))


def _s5_mats(br, bi, c_re, c_im):
    g8 = N_SSM_GROUPS // S5_LANE_BLOCKS
    to_blk = lambda m: m.reshape(S5_LANE_BLOCKS, g8, m.shape[1], m.shape[2])
    wb = jnp.concatenate([_block_diag(to_blk(jnp.swapaxes(br, 1, 2))),
                          _block_diag(to_blk(jnp.swapaxes(bi, 1, 2)))], axis=2)
    wc = jnp.concatenate([_block_diag(to_blk(jnp.swapaxes(c_re, 1, 2))),
                          _block_diag(to_blk(jnp.swapaxes(-c_im, 1, 2)))], axis=1)
    return wb, wc


def _s5_mats_bwd(dwb, dwc):
    g, p, h = N_SSM_GROUPS, SSM_STATE, SSM_GROUP
    half = S5_STATE_LANES
    dbr = jnp.swapaxes(_block_diag_extract(dwb[:, :, :half], h, p).reshape(g, h, p), 1, 2)
    dbi = jnp.swapaxes(_block_diag_extract(dwb[:, :, half:], h, p).reshape(g, h, p), 1, 2)
    dcr = jnp.swapaxes(_block_diag_extract(dwc[:, :half, :], p, h).reshape(g, p, h), 1, 2)
    dci = -jnp.swapaxes(_block_diag_extract(dwc[:, half:, :], p, h).reshape(g, p, h), 1, 2)
    return dbr, dbi, dcr, dci


def _s5_scan_consts(log_step, a_re, a_im, reverse):
    step = jnp.exp(log_step)[:, None]
    xr = (a_re * step).reshape(S5_LANE_BLOCKS, 1, S5_STATE_LANES)
    xi = (a_im * step).reshape(S5_LANE_BLOCKS, 1, S5_STATE_LANES)
    if reverse:
        xi = -xi
    row = jnp.arange(SUBLANE, dtype=F32).reshape(1, SUBLANE, 1)

    def power(n):
        mag = jnp.exp(n * xr)
        return jnp.concatenate([mag * jnp.cos(n * xi), mag * jnp.sin(n * xi)], axis=-1)

    kinds = []
    for d in (1, 2, 4):
        keep = (row <= SUBLANE - 1 - d) if reverse else (row >= d)
        kinds.append(jnp.where(keep, power(jnp.full_like(row, float(d))), 0.0))
    kinds.append(power((SUBLANE - row) if reverse else (row + 1.0)))
    return jnp.stack(kinds, axis=1)


def _scan_rows(s_ref, sc_ref, carry_ref, n_rows, reverse):
    n_grp = n_rows // SUBLANE
    n_col = S5_STATE_LANES // LANE
    half = S5_STATE_LANES

    def step(i, carry):
        grp = (n_grp - 1 - i) if reverse else i
        r0 = pl.multiple_of(grp * SUBLANE, SUBLANE)
        out = []
        for cb in range(n_col):
            lo, hi = cb * LANE, half + cb * LANE
            re = s_ref[pl.ds(r0, SUBLANE), lo:lo + LANE]
            im = s_ref[pl.ds(r0, SUBLANE), hi:hi + LANE]
            for k, d in enumerate((1, 2, 4)):
                sh = (SUBLANE - d) if reverse else d
                rr, ri = pltpu.roll(re, sh, 0), pltpu.roll(im, sh, 0)
                ar, ai = sc_ref[k, :, lo:lo + LANE], sc_ref[k, :, hi:hi + LANE]
                re, im = re + (ar * rr - ai * ri), im + (ar * ri + ai * rr)
            pr, pi = sc_ref[3, :, lo:lo + LANE], sc_ref[3, :, hi:hi + LANE]
            cr, ci = carry[2 * cb], carry[2 * cb + 1]
            re, im = re + (pr * cr - pi * ci), im + (pr * ci + pi * cr)
            s_ref[pl.ds(r0, SUBLANE), lo:lo + LANE] = re
            s_ref[pl.ds(r0, SUBLANE), hi:hi + LANE] = im
            edge = 0 if reverse else SUBLANE - 1
            out.append(jnp.broadcast_to(re[edge:edge + 1, :], (SUBLANE, LANE)))
            out.append(jnp.broadcast_to(im[edge:edge + 1, :], (SUBLANE, LANE)))
        return tuple(out)

    init = []
    for cb in range(n_col):
        init.append(carry_ref[:, cb * LANE:(cb + 1) * LANE])
        init.append(carry_ref[:, half + cb * LANE:half + (cb + 1) * LANE])
    fin = lax.fori_loop(0, n_grp, step, tuple(init))
    for cb in range(n_col):
        carry_ref[:, cb * LANE:(cb + 1) * LANE] = fin[2 * cb]
        carry_ref[:, half + cb * LANE:half + (cb + 1) * LANE] = fin[2 * cb + 1]


def _s5_fwd(proj, wb, wc, d_skip, sc, *, name, exchanges=()):
    l = proj.shape[0]
    tt = _pick_tile(l, S5_TIME_CHUNK, SUBLANE)
    u_off = (proj.shape[1] - D_HALF) // LANE
    w2 = 2 * S5_STATE_LANES

    def body(u_ref, wb_ref, wc_ref, d_ref, sc_ref, s_ref, y_ref, carry_ref):
        @pl.when(pl.program_id(1) == 0)
        def _():
            carry_ref[...] = jnp.zeros_like(carry_ref)
        u = u_ref[...]
        s_ref[...] = _dot(u, wb_ref[0], _NN)
        _scan_rows(s_ref, sc_ref.at[0], carry_ref, tt, False)
        y_ref[...] = _dot(s_ref[...], wc_ref[0], _NN) + d_ref[...] * u

    return _call(
        body, name=name, grid=(S5_LANE_BLOCKS, l // tt),
        in_specs=[pl.BlockSpec((tt, LANE), lambda b, t: (t, b + u_off)),
                  pl.BlockSpec((1, LANE, w2), lambda b, t: (b, 0, 0)),
                  pl.BlockSpec((1, w2, LANE), lambda b, t: (b, 0, 0)),
                  pl.BlockSpec((1, LANE), lambda b, t: (0, b)),
                  pl.BlockSpec((1, 4, SUBLANE, w2), lambda b, t: (b, 0, 0, 0))],
        out_specs=[pl.BlockSpec((tt, w2), lambda b, t: (t, b)), pl.BlockSpec((tt, LANE), lambda b, t: (t, b))],
        out_shape=[jax.ShapeDtypeStruct((l, S5_LANE_BLOCKS * w2), F32), jax.ShapeDtypeStruct((l, D_HALF), F32)],
        scratch_shapes=[pltpu.VMEM((SUBLANE, w2), F32)],
        args=[proj, wb, wc, d_skip, sc], parallel=False, exchanges=exchanges)


def _s5_bwd(proj, dy, states, wb, wc, d_skip, sc_rev, *, name, exchanges=()):
    l = proj.shape[0]
    tt = _pick_tile(l, S5_TIME_CHUNK, SUBLANE)
    nt = l // tt
    u_off = (proj.shape[1] - D_HALF) // LANE
    w2 = 2 * S5_STATE_LANES
    half = S5_STATE_LANES
    grp_per_chunk = tt // SUBLANE

    def body(u_ref, dy_ref, s_ref, halo_ref, wb_ref, wc_ref, d_ref, sc_ref,
             du_ref, dwb_ref, dwc_ref, dlam_ref, dd_ref, g_scr, carry_ref):
        t = pl.program_id(1)

        @pl.when(t == 0)
        def _():
            carry_ref[...] = jnp.zeros_like(carry_ref)
            dwb_ref[...] = jnp.zeros_like(dwb_ref)
            dwc_ref[...] = jnp.zeros_like(dwc_ref)
            dlam_ref[...] = jnp.zeros_like(dlam_ref)
            dd_ref[...] = jnp.zeros_like(dd_ref)

        u = u_ref[...]
        dyv = dy_ref[...]
        g_scr[...] = _dot(dyv, wc_ref[0], _NT)
        _scan_rows(g_scr, sc_ref.at[0], carry_ref, tt, True)
        gv = g_scr[...]
        du_ref[...] = (_dot(gv, wb_ref[0], _NT) + d_ref[...] * dyv).astype(BF16)
        dwb_ref[0] += _dot(u, gv, _TN)
        sv = s_ref[...]
        dwc_ref[0] += _dot(sv, dyv, _TN)
        dd_ref[...] += jnp.sum(dyv * u, axis=0, keepdims=True)
        first_chunk = t == nt - 1
        halo = jnp.where(first_chunk, 0.0, halo_ref[SUBLANE - 1:SUBLANE, :])
        row = lax.broadcasted_iota(jnp.int32, sv.shape, 0)
        sp = jnp.where(row == 0, jnp.broadcast_to(halo, sv.shape), pltpu.roll(sv, 1, 0))
        gr, gi = gv[:, :half], gv[:, half:]
        sr, si = sp[:, :half], sp[:, half:]
        dlr = jnp.sum(gr * sr + gi * si, axis=0, keepdims=True)
        dli = jnp.sum(gi * sr - gr * si, axis=0, keepdims=True)
        dlam_ref[0] += jnp.concatenate([dlr, dli], axis=1)

    rev = lambda t: nt - 1 - t
    return _call(
        body, name=name, grid=(S5_LANE_BLOCKS, nt),
        in_specs=[pl.BlockSpec((tt, LANE), lambda b, t: (rev(t), b + u_off)),
                  pl.BlockSpec((tt, LANE), lambda b, t: (rev(t), b)),
                  pl.BlockSpec((tt, w2), lambda b, t: (rev(t), b)),
                  pl.BlockSpec((SUBLANE, w2), lambda b, t: (jnp.maximum(rev(t) * grp_per_chunk - 1, 0), b)),
                  pl.BlockSpec((1, LANE, w2), lambda b, t: (b, 0, 0)),
                  pl.BlockSpec((1, w2, LANE), lambda b, t: (b, 0, 0)),
                  pl.BlockSpec((1, LANE), lambda b, t: (0, b)),
                  pl.BlockSpec((1, 4, SUBLANE, w2), lambda b, t: (b, 0, 0, 0))],
        out_specs=[pl.BlockSpec((tt, LANE), lambda b, t: (rev(t), b)),
                   pl.BlockSpec((1, LANE, w2), lambda b, t: (b, 0, 0)),
                   pl.BlockSpec((1, w2, LANE), lambda b, t: (b, 0, 0)),
                   pl.BlockSpec((1, 1, w2), lambda b, t: (b, 0, 0)),
                   pl.BlockSpec((1, LANE), lambda b, t: (0, b))],
        out_shape=[jax.ShapeDtypeStruct((l, D_HALF), BF16),
                   jax.ShapeDtypeStruct((S5_LANE_BLOCKS, LANE, w2), F32),
                   jax.ShapeDtypeStruct((S5_LANE_BLOCKS, w2, LANE), F32),
                   jax.ShapeDtypeStruct((S5_LANE_BLOCKS, 1, w2), F32),
                   jax.ShapeDtypeStruct((1, D_HALF), F32)],
        scratch_shapes=[pltpu.VMEM((tt, w2), F32), pltpu.VMEM((SUBLANE, w2), F32)],
        args=[proj, dy, states, states, wb, wc, d_skip, sc_rev], parallel=False, exchanges=exchanges)


def _glu_fwd(ypre, wg, bg, *, name):
    l, d = ypre.shape
    tl = _pick_tile(l, 512, SUBLANE)

    def body(y_ref, w_ref, b_ref, o_ref):
        yg = _gelu(y_ref[...])
        o_ref[...] = (yg * _sigmoid(_dot(yg, w_ref[...], _NN) + b_ref[...])).astype(BF16)

    row = pl.BlockSpec((tl, d), lambda i: (i, 0))
    return pl.pallas_call(
        body, name=name, grid=(l // tl,),
        in_specs=[row, pl.BlockSpec((d, d), lambda i: (0, 0)), pl.BlockSpec((1, d), lambda i: (0, 0))],
        out_specs=row, out_shape=jax.ShapeDtypeStruct((l, d), BF16), compiler_params=_PAR)(ypre, wg, bg)


def _glu_bwd(dcat, ypre, wg, bg, *, name):
    l, d = ypre.shape
    tl = _pick_tile(l, 512, SUBLANE)

    def body(dy_ref, y_ref, w_ref, b_ref, dyp_ref, dw_ref, db_ref):
        @pl.when(pl.program_id(0) == 0)
        def _():
            dw_ref[...] = jnp.zeros_like(dw_ref)
            db_ref[...] = jnp.zeros_like(db_ref)
        yp = y_ref[...]
        dyb = dy_ref[...]
        yg = _gelu(yp)
        sg = _sigmoid(_dot(yg, w_ref[...], _NN) + b_ref[...])
        dz = dyb * yg * sg * (1.0 - sg)
        dyg = dyb * sg + _dot(dz, w_ref[...], _NT)
        dyp_ref[...] = dyg * _gelu_grad(yp)
        dw_ref[...] += _dot(yg, dz, _TN)
        db_ref[...] += jnp.sum(dz, axis=0, keepdims=True)

    row = pl.BlockSpec((tl, d), lambda i: (i, 0))
    mat = pl.BlockSpec((d, d), lambda i: (0, 0))
    vec = pl.BlockSpec((1, d), lambda i: (0, 0))
    return pl.pallas_call(
        body, name=name, grid=(l // tl,),
        in_specs=[pl.BlockSpec((tl, d), lambda i: (i, 1)), row, mat, vec], out_specs=[row, mat, vec],
        out_shape=[jax.ShapeDtypeStruct((l, d), F32), jax.ShapeDtypeStruct((d, d), F32),
                   jax.ShapeDtypeStruct((1, d), F32)],
        compiler_params=_ARB)(dcat, ypre, wg, bg)


def _window_sum(x, w, trailing):
    s, d = x, 1
    while d < w:
        s = s + (_shift_dn(s, d) if trailing else _shift_up(s, d))
        d *= 2
    return s


def _window_count(shape, w):
    row = lax.broadcasted_iota(jnp.int32, shape, 0)
    return jnp.minimum(row + 1, w).astype(F32)


def _pool_fwd(proj, pw, scale, *, name):
    l = proj.shape[0]
    ng = len(POOL_WINDOWS)

    def body(z_ref, w_ref, sc_ref, o_ref):
        z = z_ref[...]
        for k, w in enumerate(POOL_WINDOWS):
            @pl.when(pl.program_id(0) == k)
            def _():
                pooled = _window_sum(z, w, True) / _window_count(z.shape, w) - z
                o_ref[...] = (_dot(pooled, w_ref[0], _NN) * sc_ref[...]).astype(BF16)

    col = pl.BlockSpec((l, LANE), lambda g: (0, g))
    return pl.pallas_call(
        body, name=name, grid=(ng,),
        in_specs=[col, pl.BlockSpec((1, LANE, LANE), lambda g: (g, 0, 0)), pl.BlockSpec((1, LANE), lambda g: (0, g))],
        out_specs=col, out_shape=jax.ShapeDtypeStruct((l, D_HALF), BF16), compiler_params=_PAR)(proj, pw, scale)


def _pool_bwd(proj, dcat, pw, scale, *, name):
    l = proj.shape[0]
    ng = len(POOL_WINDOWS)

    def body(z_ref, dy_ref, w_ref, sc_ref, dz_ref, dw_ref, dsc_ref):
        z = z_ref[...]
        dy = dy_ref[...]
        for k, w in enumerate(POOL_WINDOWS):
            @pl.when(pl.program_id(0) == k)
            def _():
                cnt = _window_count(z.shape, w)
                pooled = _window_sum(z, w, True) / cnt - z
                ypre = _dot(pooled, w_ref[0], _NN)
                dsc_ref[...] = jnp.sum(dy * ypre, axis=0, keepdims=True)
                dyp = dy * sc_ref[...]
                dw_ref[0] = _dot(pooled, dyp, _TN)
                dpool = _dot(dyp, w_ref[0], _NT)
                dz_ref[...] = (_window_sum(dpool / cnt, w, False) - dpool).astype(BF16)

    col = pl.BlockSpec((l, LANE), lambda g: (0, g))
    mat = pl.BlockSpec((1, LANE, LANE), lambda g: (g, 0, 0))
    vec = pl.BlockSpec((1, LANE), lambda g: (0, g))
    return pl.pallas_call(
        body, name=name, grid=(ng,), in_specs=[col, col, mat, vec], out_specs=[col, mat, vec],
        out_shape=[jax.ShapeDtypeStruct((l, D_HALF), BF16), jax.ShapeDtypeStruct((ng, LANE, LANE), F32),
                   jax.ShapeDtypeStruct((1, D_HALF), F32)],
        compiler_params=_PAR)(proj, dcat, pw, scale)


def _tril_mask():
    r = lax.broadcasted_iota(jnp.int32, (CHUNK, CHUNK), 0)
    c = lax.broadcasted_iota(jnp.int32, (CHUNK, CHUNK), 1)
    return r >= c


def _sgu_fwd(proj, ng, sw, sb_t, *, name):
    l = proj.shape[0]
    tl = _pick_tile(l, 512, CHUNK)

    def body(su_ref, sv_ref, g_ref, w_ref, b_ref, o_ref):
        su = _gelu(su_ref[...])
        sv = _gelu(sv_ref[...])
        r = lax.rsqrt(jnp.mean(sv * sv, axis=-1, keepdims=True) + EPS)
        v = sv * r * g_ref[...]
        mask = _tril_mask()
        for h in range(SGU_HEADS):
            wm = jnp.where(mask, w_ref[h], 0.0)
            cs = slice(h * LANE, (h + 1) * LANE)
            for n in range(tl // CHUNK):
                rs = slice(n * CHUNK, (n + 1) * CHUNK)
                mixed = _dot(wm, v[rs, cs], _NN) + b_ref[:, h:h + 1]
                o_ref[rs, cs] = (su[rs, cs] * mixed).astype(BF16)

    blk = lambda c: pl.BlockSpec((tl, D_HALF), lambda i: (i, c))
    return pl.pallas_call(
        body, name=name, grid=(l // tl,),
        in_specs=[blk(1), blk(2), pl.BlockSpec((1, D_HALF), lambda i: (0, 0)),
                  pl.BlockSpec((SGU_HEADS, CHUNK, CHUNK), lambda i: (0, 0, 0)),
                  pl.BlockSpec((CHUNK, SGU_HEADS), lambda i: (0, 0))],
        out_specs=blk(0), out_shape=jax.ShapeDtypeStruct((l, D_HALF), BF16),
        compiler_params=_PAR)(proj, proj, ng, sw, sb_t)


def _sgu_bwd(proj, dcat, ng, sw, sb_t, *, name):
    l = proj.shape[0]
    tl = _pick_tile(l, 512, CHUNK)

    def body(su_ref, sv_ref, dy_ref, g_ref, w_ref, b_ref, dsu_ref, dsv_ref, dw_ref, dbm_ref, dng_ref, dv_scr):
        @pl.when(pl.program_id(0) == 0)
        def _():
            dw_ref[...] = jnp.zeros_like(dw_ref)
            dbm_ref[...] = jnp.zeros_like(dbm_ref)
            dng_ref[...] = jnp.zeros_like(dng_ref)
        su_pre = su_ref[...]
        sv_pre = sv_ref[...]
        su = _gelu(su_pre)
        sv = _gelu(sv_pre)
        gsu = _gelu_grad(su_pre)
        gv = g_ref[...]
        r = lax.rsqrt(jnp.mean(sv * sv, axis=-1, keepdims=True) + EPS)
        v = sv * r * gv
        dy = dy_ref[...]
        mask = _tril_mask()
        for h in range(SGU_HEADS):
            wm = jnp.where(mask, w_ref[h], 0.0)
            cs = slice(h * LANE, (h + 1) * LANE)
            dw_acc = jnp.zeros((CHUNK, CHUNK), F32)
            db_acc = jnp.zeros((CHUNK, LANE), F32)
            for n in range(tl // CHUNK):
                rs = slice(n * CHUNK, (n + 1) * CHUNK)
                vb = v[rs, cs]
                mixed = _dot(wm, vb, _NN) + b_ref[:, h:h + 1]
                dyb = dy[rs, cs]
                dsu_ref[rs, cs] = (dyb * mixed * gsu[rs, cs]).astype(BF16)
                dmix = dyb * su[rs, cs]
                db_acc = db_acc + dmix
                dw_acc = dw_acc + _dot(dmix, vb, _NT)
                dv_scr[rs, cs] = _dot(wm, dmix, _TN)
            dw_ref[h] += jnp.where(mask, dw_acc, 0.0)
            dbm_ref[h] += db_acc
        dv = dv_scr[...]
        a = dv * gv
        m = jnp.mean(a * sv, axis=-1, keepdims=True)
        dsv = r * a - sv * (r * r * r) * m
        dng_ref[...] += jnp.sum(dv * sv * r, axis=0, keepdims=True)
        dsv_ref[...] = (dsv * _gelu_grad(sv_pre)).astype(BF16)

    blk = lambda c: pl.BlockSpec((tl, D_HALF), lambda i: (i, c))
    mats = pl.BlockSpec((SGU_HEADS, CHUNK, CHUNK), lambda i: (0, 0, 0))
    vec = pl.BlockSpec((1, D_HALF), lambda i: (0, 0))
    piece = jax.ShapeDtypeStruct((l, D_HALF), BF16)
    mshape = jax.ShapeDtypeStruct((SGU_HEADS, CHUNK, CHUNK), F32)
    return pl.pallas_call(
        body, name=name, grid=(l // tl,),
        in_specs=[blk(1), blk(2), blk(1), vec, mats, pl.BlockSpec((CHUNK, SGU_HEADS), lambda i: (0, 0))],
        out_specs=[blk(0), blk(0), mats, mats, vec],
        out_shape=[piece, piece, mshape, mshape, jax.ShapeDtypeStruct((1, D_HALF), F32)],
        scratch_shapes=[pltpu.VMEM((tl, D_HALF), F32)],
        compiler_params=_ARB)(proj, proj, dcat, ng, sw, sb_t)


def _s5_params(w):
    prep_args = (w["ssm_log_step"], w["ssm_a_re"], w["ssm_a_im"], w["ssm_b_re"], w["ssm_b_im"])
    (lr, li, br, bi), prep_vjp = jax.vjp(jax.vmap(_s5_prep), *prep_args)
    wb, wc = jax.vmap(_s5_mats)(br, bi, w["ssm_c_re"], w["ssm_c_im"])
    consts = lambda reverse: jax.vmap(functools.partial(_s5_scan_consts, reverse=reverse))(*prep_args[:3])
    return dict(wb=wb.astype(BF16), wc=wc.astype(BF16), d=w["ssm_d"][:, None, :], sc=consts(False),
                sc_rev=consts(True), prep_vjp=prep_vjp)


def _s5_param_grads(s5, dwb, dwc, dlam, dd):
    dbr, dbi, dcr, dci = jax.vmap(_s5_mats_bwd)(dwb, dwc)
    n = dlam.shape[0]
    dlr = dlam[:, :, 0, :S5_STATE_LANES].reshape(n, N_SSM_GROUPS, SSM_STATE)
    dli = dlam[:, :, 0, S5_STATE_LANES:].reshape(n, N_SSM_GROUPS, SSM_STATE)
    dls, dar, dai, db_re, db_im = s5["prep_vjp"]((dlr, dli, dbr, dbi))
    return dict(ssm_log_step=dls, ssm_a_re=dar, ssm_a_im=dai, ssm_b_re=db_re, ssm_b_im=db_im, ssm_c_re=dcr,
                ssm_c_im=dci, ssm_d=dd[:, 0, :])


def _layer_weights(i):
    j = i // 2
    mixer = [("even_w_in", j), ("even_w_out", j), ("ssm_glu_w", j)] if i % 2 == 0 else [("odd_w_in", j),
                                                                                         ("odd_w_out", j)]
    return dict(w_in=mixer[0], w_out=mixer[1], glu=mixer[2:], up=("ffn_w_up", i), down=("ffn_w_down", i))


class _LocalWeights:
    def __init__(self, w):
        self.w, self.grads = w, {}

    def carried_by(self, stage, i):
        return []

    def delivered(self, stage, i, outs):
        pass

    def weight(self, key):
        return self.w[key[0]][key[1]]

    def grad(self, key, dw):
        self.grads[key] = dw


class _ShardedWeights:
    def __init__(self, shards):
        self.shards = shards
        self.full, self.pending, self.scattered = {}, {}, {}

    def start(self, others):
        keys = [_layer_weights(0)["w_in"]]
        outs = _exchange_only(others + [self._gather(k) for k in keys], name="ag_first")
        self._take(keys, outs[len(others):])
        return outs[:len(others)]

    def _gather(self, key):
        return _Gather(self.shards[key[0]][key[1]])

    def _take(self, keys, outs):
        for key, got in zip(keys, outs):
            if BIG[key[0]] == 2:
                self.full[key] = jnp.swapaxes(got, 0, 1).reshape(got.shape[1], -1)
            else:
                self.full[key] = got.reshape(-1, got.shape[2])

    def _plan(self, stage, i):
        cur = _layer_weights(i)
        nxt = _layer_weights(i + 1) if i + 1 < DEPTH else None
        has_scan = lambda k: k % 2 == 0
        none = ([], [])
        return {
            "mm_in": ([cur["w_out"], *cur["glu"]], []) if i == 0 else none,
            "s5_fwd": ([cur["up"]], []),
            "mm_up": ([cur["down"]] + ([nxt["w_in"], nxt["w_out"], *nxt["glu"]] if nxt else []), []),
            "ffn_act": ([nxt["up"]], []) if nxt and not has_scan(i + 1) else none,
            "ffn_act_bwd": ([], [cur["down"]] + ([nxt["up"]] if nxt and not has_scan(i + 1) else [])),
            "s5_bwd": ([], [cur["up"]]),
            "mm_up_dx": ([], [nxt["w_in"], *nxt["glu"]]) if nxt else none,
            "mm_up_dw": ([], [nxt["w_out"]]) if nxt else none,
            "mm_in_dx": ([], [cur["w_out"]]) if i == 0 else none,
            "mm_in_dw": ([], list(cur["glu"])) if i == 0 else none,
        }[stage]

    def _scatter(self, key):
        name, layer = key
        src = self.pending.pop(key)
        layers, rows, cols = self.shards[name].shape
        if name not in self.scattered:
            self.scattered[name] = lax.empty((N_DEV, layers * rows, cols), src.dtype)
        return _Scatter(src, into=self.scattered[name], row0=layer * rows)

    def carried_by(self, stage, i):
        gather, scatter = self._plan(stage, i)
        return [self._gather(k) for k in gather] + [self._scatter(k) for k in scatter]

    def delivered(self, stage, i, outs):
        gather, scatter = self._plan(stage, i)
        self._take(gather, outs[:len(gather)])
        for (name, _), buf in zip(scatter, outs[len(gather):]):
            self.scattered[name] = buf

    def weight(self, key):
        return self.full[key]

    def grad(self, key, dw):
        self.pending[key] = _to_dest_major(dw, BIG[key[0]] - 1).astype(BF16)

    def finish(self, others):
        keys = list(self.pending)
        outs = _exchange_only(others + [self._scatter(k) for k in keys], name="rs_last")
        for (name, _), buf in zip(keys, outs[len(others):]):
            self.scattered[name] = buf
        return outs[:len(others)]


def _device_step(x, tgt, w, comm):
    saved = []
    s5 = _s5_params(w)
    h = _rms_fwd(x, w["norm_mix_g"][0:1], name="rms_fwd")
    for i in range(DEPTH):
        j = i // 2
        lw = _layer_weights(i)
        if i % 2 == 0:
            proj = _carry(comm, "mm_in", i, _mm, h, comm.weight(lw["w_in"]), "nn", name="mm_even_in")
            ya = _sconv_fwd(proj, w["even_conv_w"][j], name="sconv_fwd")
            states, ypre = _carry(comm, "s5_fwd", i, _s5_fwd, proj, s5["wb"][j], s5["wc"][j], s5["d"][j],
                                  s5["sc"][j], name="s5_fwd")
            yb = _glu_fwd(ypre, comm.weight(lw["glu"][0]), w["ssm_glu_b"][j][None, :], name="glu_fwd")
            cat = jnp.concatenate([ya, yb], axis=1)
            mix = (states, ypre)
        else:
            proj = _carry(comm, "mm_in", i, _mm, h, comm.weight(lw["w_in"]), "nn", name="mm_odd_in")
            yc = _pool_fwd(proj, w["pool_w"][j], w["pool_scale"][j][None, :], name="pool_fwd")
            sb_t = jnp.transpose(w["sgu_b"][j])
            yd = _sgu_fwd(proj, w["sgu_norm_g"][j][None, :], w["sgu_w"][j], sb_t, name="sgu_fwd")
            cat = jnp.concatenate([yc, yd], axis=1)
            mix = (sb_t,)
        x1, h2 = _mm(cat, comm.weight(lw["w_out"]), "nn", add=x, norm=(w["norm_ffn_g"], i), name="mm_mix_out")
        up = _carry(comm, "mm_up", i, _mm, h2, comm.weight(lw["up"]), "nn", name="mm_up")
        act = _carry(comm, "ffn_act", i, _ffn_act, up, w["ffn_conv_w"][i], w["ffn_conv_b"][i:i + 1], name="ffn_act")
        if i + 1 < DEPTH:
            x2, h_next = _mm(act, comm.weight(lw["down"]), "nn", add=x1, norm=(w["norm_mix_g"], i + 1),
                             name="mm_down")
        else:
            x2, h_next = _mm(act, comm.weight(lw["down"]), "nn", add=x1, name="mm_down_last"), None
        saved.append((x, h, proj, cat, x1, h2, up, act, mix))
        x, h = x2, h_next

    loss, dx, dxb, dgf = _loss_head(x, w["norm_final_g"][None, :], tgt, name="loss_head")
    per_layer = {}
    s5_grads = []

    def put(name, idx, val):
        per_layer.setdefault(name, {})[idx] = val

    for i in reversed(range(DEPTH)):
        j = i // 2
        lw = _layer_weights(i)
        x0, h, proj, cat, x1, h2, up, act, mix = saved[i]
        comm.grad(lw["down"], _mm(act, dxb, "tn", out_dtype=BF16, name="mm_down_dw"))
        dup, dcw, dcb = _carry(comm, "ffn_act_bwd", i, _ffn_act_bwd, up, dxb, comm.weight(lw["down"]),
                               w["ffn_conv_w"][i], w["ffn_conv_b"][i:i + 1], name="ffn_act_bwd")
        put("ffn_conv_w", i, dcw)
        put("ffn_conv_b", i, dcb[0])
        dx1, dx1b, dg2 = _carry(comm, "mm_up_dx", i, _mm, dup, comm.weight(lw["up"]), "nt", tm_cap=256, tn_cap=D_MODEL,
                                norm_bwd=(x1, w["norm_ffn_g"], i, dx), name="mm_up_dx")
        comm.grad(lw["up"], _carry(comm, "mm_up_dw", i, _mm, h2, dup, "tn", out_dtype=BF16, name="mm_up_dw"))
        put("norm_ffn_g", i, dg2[0])
        dcat = _mm(dx1b, comm.weight(lw["w_out"]), "nt", name="mm_mix_out_dx")
        comm.grad(lw["w_out"], _mm(cat, dx1b, "tn", out_dtype=BF16, name="mm_mix_out_dw"))
        if i % 2 == 0:
            states, ypre = mix
            dxa, dba, dca, dcw_a = _sconv_bwd(proj, dcat, w["even_conv_w"][j], name="sconv_bwd")
            put("even_conv_w", j, dcw_a)
            dypre, dwg, dbg = _glu_bwd(dcat, ypre, comm.weight(lw["glu"][0]), w["ssm_glu_b"][j][None, :],
                                       name="glu_bwd")
            comm.grad(lw["glu"][0], dwg)
            put("ssm_glu_b", j, dbg[0])
            du, dwb, dwc, dlam, dd = _carry(comm, "s5_bwd", i, _s5_bwd, proj, dypre, states, s5["wb"][j], s5["wc"][j],
                                            s5["d"][j], s5["sc_rev"][j], name="s5_bwd")
            s5_grads.insert(0, (dwb, dwc, dlam, dd))
            dproj = jnp.concatenate([dxa, dba, dca, du], axis=1)
            in_name = "mm_even_in"
        else:
            (sb_t,) = mix
            dz, dpw, dps = _pool_bwd(proj, dcat, w["pool_w"][j], w["pool_scale"][j][None, :], name="pool_bwd")
            put("pool_w", j, dpw)
            put("pool_scale", j, dps[0])
            dsu, dsv, dsw, dbm, dng = _sgu_bwd(proj, dcat, w["sgu_norm_g"][j][None, :], w["sgu_w"][j], sb_t,
                                               name="sgu_bwd")
            put("sgu_w", j, dsw)
            put("sgu_b", j, jnp.sum(dbm, axis=-1))
            put("sgu_norm_g", j, dng[0])
            dproj = jnp.concatenate([dz, dsu, dsv], axis=1)
            in_name = "mm_odd_in"
        dx, dxb, dg1 = _carry(comm, "mm_in_dx", i, _mm, dproj, comm.weight(lw["w_in"]), "nt", tn_cap=D_MODEL,
                              norm_bwd=(x0, w["norm_mix_g"], i, dx1), name=in_name + "_dx")
        comm.grad(lw["w_in"], _carry(comm, "mm_in_dw", i, _mm, h, dproj, "tn", out_dtype=BF16, name=in_name + "_dw"))
        put("norm_mix_g", i, dg1[0])

    grads = {nm: jnp.stack([vals[k] for k in sorted(vals)]) for nm, vals in per_layer.items()}
    grads["norm_final_g"] = dgf[0]
    grads.update(_s5_param_grads(s5, *[jnp.stack(parts) for parts in zip(*s5_grads)]))
    return loss, dx, grads


def _carry(comm, stage, i, fn, *args, **kwargs):
    exchanges = comm.carried_by(stage, i)
    if not exchanges:
        return fn(*args, **kwargs)
    out, moved = fn(*args, exchanges=exchanges, **kwargs)
    comm.delivered(stage, i, moved)
    return out


def _sum_parts(parts, *, name):
    g, r, c = parts.shape
    tr = _pick_tile(r, max(16, EXCHANGE_BLOCK_ELEMS // c), 16)

    def body(p_ref, o_ref):
        acc = p_ref[0].astype(F32)
        for k in range(1, g):
            acc = acc + p_ref[k].astype(F32)
        o_ref[...] = acc

    return pl.pallas_call(
        body, name=name, grid=(r // tr,), in_specs=[pl.BlockSpec((g, tr, c), lambda i: (0, i, 0))],
        out_specs=pl.BlockSpec((tr, c), lambda i: (i, 0)), out_shape=jax.ShapeDtypeStruct((r, c), F32),
        compiler_params=_PAR)(parts)


def _adamw(w, m, v, g_parts, *, name):
    r, c = w.shape
    g = g_parts.shape[0]
    tc = _pick_tile(c, 8192, LANE)
    tr = _pick_tile(r, max(16, (1 << 18) // tc), 16)
    c1 = 1.0 - ADAM_B1 ** ADAM_STEP
    c2 = 1.0 - ADAM_B2 ** ADAM_STEP

    def body(w_ref, m_ref, v_ref, g_ref, go_ref, d_ref, mo_ref, vo_ref):
        grad = g_ref[0].astype(F32)
        for k in range(1, g):
            grad = grad + g_ref[k].astype(F32)
        m_new = ADAM_B1 * m_ref[...] + (1.0 - ADAM_B1) * grad
        v_new = ADAM_B2 * v_ref[...] + (1.0 - ADAM_B2) * (grad * grad)
        go_ref[...] = grad
        mo_ref[...] = m_new
        vo_ref[...] = v_new
        d_ref[...] = -ADAM_LR * ((m_new / c1) / (jnp.sqrt(v_new / c2) + ADAM_EPS) + ADAM_WD * w_ref[...])

    blk = pl.BlockSpec((tr, tc), lambda i, j: (i, j))
    out = jax.ShapeDtypeStruct((r, c), F32)
    return pl.pallas_call(
        body, name=name, grid=(r // tr, c // tc),
        in_specs=[blk, blk, blk, pl.BlockSpec((g, tr, tc), lambda i, j: (0, i, j))],
        out_specs=[blk, blk, blk, blk], out_shape=[out, out, out, out], compiler_params=_PAR2)(w, m, v, g_parts)


WEIGHT_NAMES = ['norm_mix_g', 'even_w_in', 'even_conv_w', 'ssm_log_step', 'ssm_a_re', 'ssm_a_im', 'ssm_b_re',
                'ssm_b_im', 'ssm_c_re', 'ssm_c_im', 'ssm_d', 'ssm_glu_w', 'ssm_glu_b', 'even_w_out', 'odd_w_in',
                'pool_w', 'pool_scale', 'sgu_norm_g', 'sgu_w', 'sgu_b', 'odd_w_out', 'norm_ffn_g', 'ffn_w_up',
                'ffn_conv_w', 'ffn_conv_b', 'ffn_w_down', 'norm_final_g']
BIG = {'even_w_in': 2, 'ssm_glu_w': 1, 'even_w_out': 1, 'odd_w_in': 2, 'odd_w_out': 1, 'ffn_w_up': 2,
       'ffn_w_down': 1}
SMALL = {'even_conv_w': 2, 'pool_scale': 1, 'sgu_norm_g': 1, 'ffn_conv_w': 2}
BIG_ROWS = 512
SMALL_ROWS = 16


def _pad_to(n, q):
    return -(-n // q) * q


def _pack(arrays, dtype, rows, lead=()):
    flat = [a.reshape(lead + (-1,)).astype(dtype) for a in arrays]
    n = sum(f.shape[-1] for f in flat)
    pad = _pad_to(n, rows * LANE) - n
    if pad:
        flat.append(jnp.zeros(lead + (pad,), dtype))
    return jnp.concatenate(flat, axis=-1).reshape(lead + (rows, -1))


def _unpack(buf, shapes, lead=()):
    flat = buf.reshape(lead + (-1,))
    out, off = [], 0
    for shp in shapes:
        n = math.prod(shp)
        out.append(flat[..., off:off + n].reshape(lead + tuple(shp)))
        off += n
    return out


def _to_dest_major(full, axis):
    shp = full.shape
    split = full.reshape(shp[:axis] + (N_DEV, shp[axis] // N_DEV) + shp[axis + 1:])
    return jnp.moveaxis(split, axis, 0)


def _from_dest_major(blocks, axis):
    moved = jnp.moveaxis(blocks, 0, axis)
    shp = moved.shape
    return moved.reshape(shp[:axis] + (shp[axis] * shp[axis + 1],) + shp[axis + 2:])


def _rows_2d(a):
    return a.reshape(-1, a.shape[-1])


def _rows_2d_lead(a):
    return a.reshape(a.shape[0], -1, a.shape[-1])


def kernel(x, norm_mix_g, even_w_in, even_conv_w, ssm_log_step, ssm_a_re, ssm_a_im, ssm_b_re, ssm_b_im, ssm_c_re, ssm_c_im, ssm_d, ssm_glu_w, ssm_glu_b, even_w_out, odd_w_in, pool_w, pool_scale, sgu_norm_g, sgu_w, sgu_b, odd_w_out, norm_ffn_g, ffn_w_up, ffn_conv_w, ffn_conv_b, ffn_w_down, norm_final_g, loss_target, m_norm_mix_g, m_even_w_in, m_even_conv_w, m_ssm_log_step, m_ssm_a_re, m_ssm_a_im, m_ssm_b_re, m_ssm_b_im, m_ssm_c_re, m_ssm_c_im, m_ssm_d, m_ssm_glu_w, m_ssm_glu_b, m_even_w_out, m_odd_w_in, m_pool_w, m_pool_scale, m_sgu_norm_g, m_sgu_w, m_sgu_b, m_odd_w_out, m_norm_ffn_g, m_ffn_w_up, m_ffn_conv_w, m_ffn_conv_b, m_ffn_w_down, m_norm_final_g, v_norm_mix_g, v_even_w_in, v_even_conv_w, v_ssm_log_step, v_ssm_a_re, v_ssm_a_im, v_ssm_b_re, v_ssm_b_im, v_ssm_c_re, v_ssm_c_im, v_ssm_d, v_ssm_glu_w, v_ssm_glu_b, v_even_w_out, v_odd_w_in, v_pool_w, v_pool_scale, v_sgu_norm_g, v_sgu_w, v_sgu_b, v_odd_w_out, v_norm_ffn_g, v_ffn_w_up, v_ffn_conv_w, v_ffn_conv_b, v_ffn_w_down, v_norm_final_g):
    given = dict(locals())
    wts = {n: given[n] for n in WEIGHT_NAMES}
    mom = {n: given["m_" + n] for n in WEIGHT_NAMES}
    var = {n: given["v_" + n] for n in WEIGHT_NAMES}
    repl = [n for n in WEIGHT_NAMES if n not in BIG and n not in SMALL]

    small_shapes = [wts[n].shape for n in SMALL]
    comm = _ShardedWeights({n: wts[n].astype(BF16) for n in BIG})
    (small_all,) = comm.start([_Gather(_pack([wts[n] for n in SMALL], F32, SMALL_ROWS))])
    full = {n: wts[n] for n in repl}
    for n, blocks in zip(SMALL, _unpack(small_all, small_shapes, lead=(N_DEV,))):
        full[n] = _from_dest_major(blocks, SMALL[n])

    loss, dx, grads = _device_step(x[0], loss_target[0], full, comm)

    repl_shapes = [wts[n].shape for n in repl]
    repl_flat = jnp.concatenate([grads[n].reshape(-1) for n in repl] + [loss.reshape(-1)])
    n_repl = repl_flat.shape[0]
    chunk = _pad_to(-(-n_repl // N_DEV), SMALL_ROWS * LANE)
    repl_flat = jnp.pad(repl_flat, (0, N_DEV * chunk - n_repl))
    small_part = _pack([_to_dest_major(grads[n], SMALL[n]) for n in SMALL], F32, SMALL_ROWS, lead=(N_DEV,))
    small_cols = small_part.shape[2]
    (small_rs,) = comm.finish([_Scatter(
        jnp.concatenate([small_part, repl_flat.reshape(N_DEV, SMALL_ROWS, chunk // SMALL_ROWS)], axis=2))])
    small_sum = _sum_parts(small_rs, name="rs_sum_small")
    (repl_all,) = _exchange_only([_Gather(small_sum[:, small_cols:])], name="ag_repl")
    repl_sum = repl_all.reshape(-1)
    total_loss = repl_sum[n_repl - 1]

    out = {}
    for n in BIG:
        shp = wts[n].shape
        res = _adamw(_rows_2d(wts[n]), _rows_2d(mom[n]), _rows_2d(var[n]), comm.scattered[n], name="adamw_" + n)
        out[n] = [r.reshape(shp) for r in res]

    def small_vec(shard_part, repl_part):
        flat = jnp.concatenate([shard_part.reshape(-1), repl_part])
        return flat.reshape(SMALL_ROWS, -1)

    def small_tree(tree):
        tail = jnp.concatenate([tree[n].reshape(-1) for n in repl])
        tail = jnp.pad(tail, (0, N_DEV * chunk - tail.shape[0]))
        return small_vec(_pack([tree[n] for n in SMALL], F32, SMALL_ROWS), tail)

    res = _adamw(small_tree(wts), small_tree(mom), small_tree(var), small_vec(small_sum[:, :small_cols], repl_sum)[None],
                 name="adamw_small")
    n_small = SMALL_ROWS * small_cols
    for k, r in enumerate(res):
        flat = r.reshape(-1)
        shard = _unpack(flat[:n_small], small_shapes)
        rest = _unpack(flat[n_small:], repl_shapes)
        for n, val in zip(SMALL, shard):
            out.setdefault(n, [None] * 4)[k] = val
        for n, val in zip(repl, rest):
            out.setdefault(n, [None] * 4)[k] = val

    grad_x = dx[None]
    return (total_loss, grad_x, *[out[n][0] for n in WEIGHT_NAMES], *[out[n][1] for n in WEIGHT_NAMES],
            *[out[n][2] for n in WEIGHT_NAMES], *[out[n][3] for n in WEIGHT_NAMES])
```

```python
import functools
import math

import jax
import jax.numpy as jnp
from jax import lax
from jax.experimental import pallas as pl
from jax.experimental.pallas import tpu as pltpu

F32 = jnp.float32
BF16 = jnp.bfloat16

D_MODEL = 1024
DEPTH = 4
D_HALF = D_MODEL // 2
SSM_GROUP = 16
N_SSM_GROUPS = D_HALF // SSM_GROUP
SSM_STATE = 64
POOL_WINDOWS = (2, 4, 8, 16)
SGU_HEADS = 4
CHUNK = 128
D_FF = 2816
CONV_WIDTH = 3
EPS = 1e-6
N_DEV = 8

ADAM_LR = 0.001
ADAM_B1 = 0.9
ADAM_B2 = 0.999
ADAM_EPS = 1e-08
ADAM_WD = 0.01
ADAM_STEP = 10

LANE = 128
SUBLANE = 8
S5_LANE_BLOCKS = D_HALF // LANE
S5_STATE_LANES = (N_SSM_GROUPS // S5_LANE_BLOCKS) * SSM_STATE
S5_TIME_CHUNK = 512
EXCHANGE_BLOCK_ELEMS = 1 << 20

GELU_K = math.sqrt(2.0 / math.pi)
GELU_C = 0.044715

_ARB = pltpu.CompilerParams(dimension_semantics=("arbitrary",))
_ARB2 = pltpu.CompilerParams(dimension_semantics=("arbitrary", "arbitrary"))
_PAR = pltpu.CompilerParams(dimension_semantics=("parallel",))
_PAR2 = pltpu.CompilerParams(dimension_semantics=("parallel", "parallel"))


def _pick_tile(n, cap, mult):
    if n <= cap:
        return n
    best = None
    for t in range(mult, cap + 1, mult):
        if n % t == 0:
            best = t
    assert best is not None, (n, cap, mult)
    return best


_MESH = pl.DeviceIdType.MESH
_ANY = pl.BlockSpec(memory_space=pl.ANY)
SEMS_PER_EXCHANGE = N_DEV - 1


class _Gather:
    into = None

    def __init__(self, src):
        self.src = src
        self.out_shape = jax.ShapeDtypeStruct((N_DEV,) + src.shape, src.dtype)

    def copies(self, x_ref, out_ref, send_sems, recv_sems, local_sem):
        x, y, cc = lax.axis_index("x"), lax.axis_index("y"), lax.axis_index("c")
        me, sibling = (x, y, cc), (x, y, 1 - cc)
        chips = [(1 - x, y), (x, 1 - y), (1 - x, 1 - y)]

        def rows(px, py, pc):
            return out_ref.at[4 * px + 2 * py + pc]

        def copy(k, block, to, src=None):
            return pltpu.make_async_remote_copy(
                src_ref=rows(*block) if src is None else src, dst_ref=rows(*block),
                send_sem=send_sems.at[k], recv_sem=recv_sems.at[k], device_id=to, device_id_type=_MESH)

        return dict(
            mine=pltpu.make_async_copy(x_ref, rows(*me), local_sem),
            first=[copy(0, me, sibling, src=x_ref)] + [copy(1 + k, me, (*chip, cc), src=x_ref)
                                                       for k, chip in enumerate(chips)],
            passed=[copy(4 + k, (*chip, cc), sibling) for k, chip in enumerate(chips)],
            over_ici=[copy(1 + k, (*chip, cc), me) for k, chip in enumerate(chips)],
            from_sibling=[copy(0, sibling, me)] + [copy(4 + k, (*chip, 1 - cc), me) for k, chip in enumerate(chips)])

    def start(self, *refs):
        cps = self.copies(*refs)
        cps["mine"].start()
        for cp in cps["first"]:
            cp.start()

    def finish(self, *refs):
        cps = self.copies(*refs)
        for arrived, onward in zip(cps["over_ici"], cps["passed"]):
            arrived.wait_recv()
            onward.start()
        for arrived in cps["from_sibling"]:
            arrived.wait_recv()
        for cp in cps["first"] + cps["passed"]:
            cp.wait_send()
        cps["mine"].wait()


class _Scatter:
    def __init__(self, src, into=None, row0=0):
        self.src, self.into, self.row0 = src, into, row0
        whole = src if into is None else into
        self.out_shape = jax.ShapeDtypeStruct(whole.shape, whole.dtype)

    def copies(self, p_ref, whole_ref, send_sems, recv_sems, local_sem):
        x, y, cc = lax.axis_index("x"), lax.axis_index("y"), lax.axis_index("c")
        me = 4 * x + 2 * y + cc
        rows = self.src.shape[1]
        out_ref = whole_ref if self.into is None else whole_ref.at[:, pl.ds(self.row0, rows)]
        sends, arrivals = [], []
        for k in range(1, N_DEV):
            px = (1 - x) if k & 4 else x
            py = (1 - y) if k & 2 else y
            pc = (1 - cc) if k & 1 else cc
            peer = 4 * px + 2 * py + pc
            kw = dict(send_sem=send_sems.at[k - 1], recv_sem=recv_sems.at[k - 1], device_id=(px, py, pc),
                      device_id_type=_MESH)
            sends.append(pltpu.make_async_remote_copy(src_ref=p_ref.at[peer], dst_ref=out_ref.at[me], **kw))
            arrivals.append(pltpu.make_async_remote_copy(src_ref=p_ref.at[me], dst_ref=out_ref.at[peer], **kw))
        return dict(mine=pltpu.make_async_copy(p_ref.at[me], out_ref.at[me], local_sem), sends=sends,
                    arrivals=arrivals)

    def start(self, *refs):
        cps = self.copies(*refs)
        cps["mine"].start()
        for cp in cps["sends"]:
            cp.start()

    def finish(self, *refs):
        cps = self.copies(*refs)
        for cp in cps["arrivals"]:
            cp.wait_recv()
        for cp in cps["sends"]:
            cp.wait_send()
        cps["mine"].wait()


class _SemView:
    def __init__(self, ref, lo):
        self.ref, self.lo = ref, lo

    @property
    def at(self):
        return self

    def __getitem__(self, k):
        return self.ref.at[self.lo + k]


def _call(body, *, name, grid, in_specs, out_specs, out_shape, args, scratch_shapes=(), parallel=True, exchanges=()):
    n_axes = len(grid)
    if not exchanges:
        sem = ("parallel" if parallel else "arbitrary",) * n_axes
        return pl.pallas_call(
            body, name=name, grid=grid, in_specs=in_specs, out_specs=out_specs, out_shape=out_shape,
            scratch_shapes=scratch_shapes, compiler_params=pltpu.CompilerParams(dimension_semantics=sem))(*args)
    single = not isinstance(out_shape, (list, tuple))
    out_specs = [out_specs] if single else list(out_specs)
    out_shape = [out_shape] if single else list(out_shape)
    n_in, n_out, n_scr, n_x = len(in_specs), len(out_specs), len(scratch_shapes), len(exchanges)
    landing = [(e, ex.into) for e, ex in enumerate(exchanges) if ex.into is not None]
    aliases = {n_in + n_x + pos: n_out + e for pos, (e, _) in enumerate(landing)}

    def wrapped(*refs):
        ins, refs = refs[:n_in], refs[n_in:]
        x_in, refs = refs[:n_x], refs[n_x + len(landing):]
        outs, refs = refs[:n_out], refs[n_out:]
        x_out, refs = refs[:n_x], refs[n_x:]
        scr, (send_sems, recv_sems, local_sems) = refs[:n_scr], refs[n_scr:]
        ids = [pl.program_id(k) for k in range(n_axes)]
        first = functools.reduce(jnp.logical_and, [i == 0 for i in ids])
        last = functools.reduce(jnp.logical_and, [i == g - 1 for i, g in zip(ids, grid)])

        def sems(e):
            lo = e * SEMS_PER_EXCHANGE
            return _SemView(send_sems, lo), _SemView(recv_sems, lo), local_sems.at[e]

        @pl.when(first)
        def _():
            for e, ex in enumerate(exchanges):
                ex.start(x_in[e], x_out[e], *sems(e))

        body(*ins, *outs, *scr)

        @pl.when(last)
        def _():
            for e, ex in enumerate(exchanges):
                ex.finish(x_in[e], x_out[e], *sems(e))

    res = pl.pallas_call(
        wrapped, name=name, grid=grid, in_specs=list(in_specs) + [_ANY] * (n_x + len(landing)),
        out_specs=out_specs + [_ANY] * n_x, out_shape=out_shape + [ex.out_shape for ex in exchanges],
        input_output_aliases=aliases,
        scratch_shapes=list(scratch_shapes) + [pltpu.SemaphoreType.DMA((n_x * SEMS_PER_EXCHANGE,)),
                                               pltpu.SemaphoreType.DMA((n_x * SEMS_PER_EXCHANGE,)),
                                               pltpu.SemaphoreType.DMA((n_x,))],
        compiler_params=pltpu.CompilerParams(dimension_semantics=("arbitrary",) * n_axes),
    )(*args, *[ex.src for ex in exchanges], *[buf for _, buf in landing])
    outs, x_outs = res[:n_out], res[n_out:]
    return (outs[0] if single else outs), x_outs


def _exchange_only(exchanges, *, name):
    def body():
        pass

    return _call(body, name=name, grid=(1,), in_specs=[], out_specs=[], out_shape=[], args=[],
                 exchanges=exchanges)[1]


def _shift_dn(x, d):
    rolled = pltpu.roll(x, d, 0)
    if x.shape[0] <= SUBLANE or d >= SUBLANE:
        row = lax.broadcasted_iota(jnp.int32, x.shape, 0)
        return jnp.where(row >= d, rolled, 0.0)
    row = lax.broadcasted_iota(jnp.int32, (SUBLANE, x.shape[1]), 0)
    return jnp.concatenate([jnp.where(row >= d, rolled[:SUBLANE], 0.0), rolled[SUBLANE:]], axis=0)


def _shift_up(x, d):
    n = x.shape[0]
    rolled = pltpu.roll(x, n - d, 0)
    if n <= SUBLANE or d >= SUBLANE:
        row = lax.broadcasted_iota(jnp.int32, x.shape, 0)
        return jnp.where(row < n - d, rolled, 0.0)
    row = lax.broadcasted_iota(jnp.int32, (SUBLANE, x.shape[1]), 0)
    return jnp.concatenate([rolled[:n - SUBLANE], jnp.where(row < SUBLANE - d, rolled[n - SUBLANE:], 0.0)], axis=0)


def _gelu(x):
    return 0.5 * x * (1.0 + jnp.tanh(GELU_K * (x + GELU_C * x * x * x)))


def _gelu_grad(x):
    t = jnp.tanh(GELU_K * (x + GELU_C * x * x * x))
    return 0.5 * (1.0 + t) + 0.5 * x * (1.0 - t * t) * (GELU_K * (1.0 + 3.0 * GELU_C * x * x))


def _sigmoid(x):
    return 1.0 / (1.0 + jnp.exp(-x))


def _conv3(x, w_ref):
    return w_ref[0:1, :] * _shift_dn(x, 2) + w_ref[1:2, :] * _shift_dn(x, 1) + w_ref[2:3, :] * x


def _conv3_bwd_x(dy, w_ref):
    return w_ref[2:3, :] * dy + w_ref[1:2, :] * _shift_up(dy, 1) + w_ref[0:1, :] * _shift_up(dy, 2)


def _conv3_bwd_w(dy, x):
    return jnp.concatenate([
        jnp.sum(dy * _shift_dn(x, 2), axis=0, keepdims=True),
        jnp.sum(dy * _shift_dn(x, 1), axis=0, keepdims=True),
        jnp.sum(dy * x, axis=0, keepdims=True)], axis=0)


def _dot(a, b, dims):
    return lax.dot_general(a.astype(BF16), b.astype(BF16), (dims, ((), ())), preferred_element_type=F32)


_NN = ((1,), (0,))
_NT = ((1,), (1,))
_TN = ((0,), (0,))


def _mm(a, b, mode, *, name, out_dtype=F32, add=None, norm=None, norm_bwd=None, tm_cap=512, tn_cap=1536,
        tiles=None, exchanges=()):
    if mode == "tn":
        r, m = a.shape
        n = b.shape[1]
        tm, tn = tiles or (_pick_tile(m, 256, LANE), _pick_tile(n, tn_cap, LANE))
        in_specs = [pl.BlockSpec((r, tm), lambda i, j: (0, i)), pl.BlockSpec((r, tn), lambda i, j: (0, j))]
        dims = _TN
    elif mode == "nn":
        m, k = a.shape
        n = b.shape[1]
        tm, tn = tiles or (_pick_tile(m, tm_cap, SUBLANE), _pick_tile(n, tn_cap, LANE))
        in_specs = [pl.BlockSpec((tm, k), lambda i, j: (i, 0)), pl.BlockSpec((k, tn), lambda i, j: (0, j))]
        dims = _NN
    else:
        m, k = a.shape
        n = b.shape[0]
        tm, tn = tiles or (_pick_tile(m, tm_cap, SUBLANE), _pick_tile(n, tn_cap, LANE))
        in_specs = [pl.BlockSpec((tm, k), lambda i, j: (i, 0)), pl.BlockSpec((tn, k), lambda i, j: (j, 0))]
        dims = _NT
    assert m % tm == 0 and n % tn == 0, (name, m, n, tm, tn)
    args = [a, b]
    tile = pl.BlockSpec((tm, tn), lambda i, j: (i, j))
    if add is not None:
        in_specs.append(tile)
        args.append(add)
    out_specs, out_shape = tile, jax.ShapeDtypeStruct((m, n), out_dtype)
    if norm is not None:
        gains, layer = norm
        assert tn == n
        in_specs.append(pl.BlockSpec((None, 1, n), lambda i, j: (layer, 0, 0)))
        args.append(gains.reshape(gains.shape[0], 1, n))
        out_specs, out_shape = [tile, tile], [out_shape, jax.ShapeDtypeStruct((m, n), BF16)]
    if norm_bwd is not None:
        x_in, gains, layer, res = norm_bwd
        assert tn == n and add is None and norm is None
        vec = pl.BlockSpec((None, 1, n), lambda i, j: (layer, 0, 0))
        in_specs += [tile, vec, tile]
        args += [x_in, gains.reshape(gains.shape[0], 1, n), res]
        out_specs = [tile, tile, pl.BlockSpec((1, n), lambda i, j: (0, 0))]
        out_shape = [jax.ShapeDtypeStruct((m, n), F32), jax.ShapeDtypeStruct((m, n), BF16),
                     jax.ShapeDtypeStruct((1, n), F32)]

    def body(*refs):
        acc = _dot(refs[0][...], refs[1][...], dims)
        if add is not None:
            acc = acc + refs[2][...]
        if norm_bwd is not None:
            x_ref, g_ref, res_ref, dx_ref, dxb_ref, dg_ref = refs[2:]

            @pl.when(pl.program_id(0) == 0)
            def _():
                dg_ref[...] = jnp.zeros_like(dg_ref)
            dx, xn = _rms_bwd_rows(acc, x_ref[...], g_ref[...])
            dx = dx + res_ref[...]
            dx_ref[...] = dx
            dxb_ref[...] = dx.astype(BF16)
            dg_ref[...] += jnp.sum(acc * xn, axis=0, keepdims=True)
        elif norm is None:
            refs[-1][...] = acc.astype(out_dtype)
        else:
            refs[-2][...] = acc.astype(out_dtype)
            r = lax.rsqrt(jnp.mean(acc * acc, axis=-1, keepdims=True) + EPS)
            refs[-1][...] = (acc * r * refs[-3][...]).astype(BF16)

    return _call(body, name=name, grid=(m // tm, n // tn), in_specs=in_specs, out_specs=out_specs,
                 out_shape=out_shape, args=args, parallel=norm_bwd is None, exchanges=exchanges)


def _rms_fwd(x, g, *, name):
    l, d = x.shape
    tl = _pick_tile(l, 512, SUBLANE)

    def body(x_ref, g_ref, h_ref):
        xv = x_ref[...]
        r = lax.rsqrt(jnp.mean(xv * xv, axis=-1, keepdims=True) + EPS)
        h_ref[...] = (xv * r * g_ref[...]).astype(BF16)

    return pl.pallas_call(
        body, name=name, grid=(l // tl,),
        in_specs=[pl.BlockSpec((tl, d), lambda i: (i, 0)), pl.BlockSpec((1, d), lambda i: (0, 0))],
        out_specs=pl.BlockSpec((tl, d), lambda i: (i, 0)),
        out_shape=jax.ShapeDtypeStruct((l, d), BF16), compiler_params=_PAR)(x, g)


def _rms_bwd_rows(dh, xv, g):
    r = lax.rsqrt(jnp.mean(xv * xv, axis=-1, keepdims=True) + EPS)
    a = dh * g
    m = jnp.mean(a * xv, axis=-1, keepdims=True)
    return r * a - xv * (r * r * r) * m, xv * r


def _rms_bwd(dh, x, g, res, *, name):
    l, d = x.shape
    tl = _pick_tile(l, 512, SUBLANE)

    def body(dh_ref, x_ref, g_ref, res_ref, dx_ref, dxb_ref, dg_ref):
        @pl.when(pl.program_id(0) == 0)
        def _():
            dg_ref[...] = jnp.zeros_like(dg_ref)
        dhv = dh_ref[...]
        dx, xn = _rms_bwd_rows(dhv, x_ref[...], g_ref[...])
        dx = dx + res_ref[...]
        dx_ref[...] = dx
        dxb_ref[...] = dx.astype(BF16)
        dg_ref[...] += jnp.sum(dhv * xn, axis=0, keepdims=True)

    row = pl.BlockSpec((tl, d), lambda i: (i, 0))
    vec = pl.BlockSpec((1, d), lambda i: (0, 0))
    return pl.pallas_call(
        body, name=name, grid=(l // tl,), in_specs=[row, row, vec, row], out_specs=[row, row, vec],
        out_shape=[jax.ShapeDtypeStruct((l, d), F32), jax.ShapeDtypeStruct((l, d), BF16),
                   jax.ShapeDtypeStruct((1, d), F32)],
        compiler_params=_ARB)(dh, x, g, res)


def _loss_head(x, g, tgt, *, name):
    l, d = x.shape
    tl = _pick_tile(l, 512, SUBLANE)

    def body(x_ref, g_ref, t_ref, loss_ref, dx_ref, dxb_ref, dg_ref):
        @pl.when(pl.program_id(0) == 0)
        def _():
            dg_ref[...] = jnp.zeros_like(dg_ref)
            loss_ref[...] = jnp.zeros_like(loss_ref)
        xv = x_ref[...]
        gv = g_ref[...]
        r = lax.rsqrt(jnp.mean(xv * xv, axis=-1, keepdims=True) + EPS)
        err = xv * r * gv - t_ref[...]
        row_loss = jnp.sum(err * err, axis=-1, keepdims=True) * (0.5 / d)
        loss_ref[...] += jnp.sum(row_loss, axis=0, keepdims=True)
        dy = err * (1.0 / d)
        dx, xn = _rms_bwd_rows(dy, xv, gv)
        dx_ref[...] = dx
        dxb_ref[...] = dx.astype(BF16)
        dg_ref[...] += jnp.sum(dy * xn, axis=0, keepdims=True)

    row = pl.BlockSpec((tl, d), lambda i: (i, 0))
    vec = pl.BlockSpec((1, d), lambda i: (0, 0))
    one = pl.BlockSpec((1, 1), lambda i: (0, 0))
    return pl.pallas_call(
        body, name=name, grid=(l // tl,), in_specs=[row, vec, row], out_specs=[one, row, row, vec],
        out_shape=[jax.ShapeDtypeStruct((1, 1), F32), jax.ShapeDtypeStruct((l, d), F32),
                   jax.ShapeDtypeStruct((l, d), BF16), jax.ShapeDtypeStruct((1, d), F32)],
        compiler_params=_ARB)(x, g, tgt)


def _ffn_act(up, cw, cb, *, name, exchanges=()):
    l = up.shape[0]
    nb = D_FF // LANE

    def body(ug_ref, uv_ref, wg_ref, wv_ref, bg_ref, bv_ref, o_ref):
        gc = _conv3(ug_ref[...], wg_ref) + bg_ref[...]
        vc = _conv3(uv_ref[...], wv_ref) + bv_ref[...]
        o_ref[...] = (gc * _sigmoid(gc) * vc).astype(BF16)

    col = lambda off: pl.BlockSpec((l, LANE), lambda j: (0, j + off))
    w3 = lambda off: pl.BlockSpec((CONV_WIDTH, LANE), lambda j: (0, j + off))
    b1 = lambda off: pl.BlockSpec((1, LANE), lambda j: (0, j + off))
    return _call(body, name=name, grid=(nb,), in_specs=[col(0), col(nb), w3(0), w3(nb), b1(0), b1(nb)],
                 out_specs=col(0), out_shape=jax.ShapeDtypeStruct((l, D_FF), BF16),
                 args=[up, up, cw, cw, cb, cb], exchanges=exchanges)


def _ffn_act_bwd(up, dact, cw, cb, *, name, exchanges=()):
    l = up.shape[0]
    nb = D_FF // LANE

    def body(ug_ref, uv_ref, own_ref, da_ref, wg_ref, wv_ref, wo_ref, bg_ref, bv_ref,
             dup_ref, dcw_ref, dcb_ref, dc_scr):
        is_gate = pl.program_id(0) < nb
        gc = _conv3(ug_ref[...], wg_ref) + bg_ref[...]
        sg = _sigmoid(gc)

        @pl.when(is_gate)
        def _():
            vc = _conv3(uv_ref[...], wv_ref) + bv_ref[...]
            dc_scr[...] = da_ref[...] * vc * (sg * (1.0 + gc * (1.0 - sg)))

        @pl.when(jnp.logical_not(is_gate))
        def _():
            dc_scr[...] = da_ref[...] * (gc * sg)

        dc = dc_scr[...]
        dcb_ref[...] = jnp.sum(dc, axis=0, keepdims=True)
        dcw_ref[...] = _conv3_bwd_w(dc, own_ref[...])
        dup_ref[...] = _conv3_bwd_x(dc, wo_ref).astype(BF16)

    half = lambda j: j % nb
    colg = pl.BlockSpec((l, LANE), lambda j: (0, half(j)))
    colv = pl.BlockSpec((l, LANE), lambda j: (0, half(j) + nb))
    colo = pl.BlockSpec((l, LANE), lambda j: (0, j))
    w3g = pl.BlockSpec((CONV_WIDTH, LANE), lambda j: (0, half(j)))
    w3v = pl.BlockSpec((CONV_WIDTH, LANE), lambda j: (0, half(j) + nb))
    w3o = pl.BlockSpec((CONV_WIDTH, LANE), lambda j: (0, j))
    b1g = pl.BlockSpec((1, LANE), lambda j: (0, half(j)))
    b1v = pl.BlockSpec((1, LANE), lambda j: (0, half(j) + nb))
    b1o = pl.BlockSpec((1, LANE), lambda j: (0, j))
    return _call(
        body, name=name, grid=(2 * nb,),
        in_specs=[colg, colv, colo, colg, w3g, w3v, w3o, b1g, b1v],
        out_specs=[colo, w3o, b1o],
        out_shape=[jax.ShapeDtypeStruct((l, 2 * D_FF), BF16), jax.ShapeDtypeStruct((CONV_WIDTH, 2 * D_FF), F32),
                   jax.ShapeDtypeStruct((1, 2 * D_FF), F32)],
        scratch_shapes=[pltpu.VMEM((l, LANE), F32)],
        args=[up, up, up, dact, cw, cw, cw, cb, cb], exchanges=exchanges)


def _sconv_fwd(proj, cw, *, name):
    l = proj.shape[0]
    nb = D_HALF // LANE

    def body(xa_ref, ba_ref, ca_ref, w_ref, o_ref):
        o_ref[...] = (ba_ref[...] * _conv3(ca_ref[...] * xa_ref[...], w_ref)).astype(BF16)

    col = lambda off: pl.BlockSpec((l, LANE), lambda j: (0, j + off))
    return pl.pallas_call(
        body, name=name, grid=(nb,),
        in_specs=[col(0), col(nb), col(2 * nb), pl.BlockSpec((CONV_WIDTH, LANE), lambda j: (0, j))],
        out_specs=col(0), out_shape=jax.ShapeDtypeStruct((l, D_HALF), BF16),
        compiler_params=_PAR)(proj, proj, proj, cw)


def _sconv_bwd(proj, dcat, cw, *, name):
    l = proj.shape[0]
    nb = D_HALF // LANE

    def body(xa_ref, ba_ref, ca_ref, dy_ref, w_ref, dxa_ref, dba_ref, dca_ref, dw_ref):
        xa, ba, ca, dy = xa_ref[...], ba_ref[...], ca_ref[...], dy_ref[...]
        q = ca * xa
        dba_ref[...] = (dy * _conv3(q, w_ref)).astype(BF16)
        dconv = dy * ba
        dw_ref[...] = _conv3_bwd_w(dconv, q)
        dq = _conv3_bwd_x(dconv, w_ref)
        dxa_ref[...] = (dq * ca).astype(BF16)
        dca_ref[...] = (dq * xa).astype(BF16)

    col = lambda off: pl.BlockSpec((l, LANE), lambda j: (0, j + off))
    w3 = pl.BlockSpec((CONV_WIDTH, LANE), lambda j: (0, j))
    piece = jax.ShapeDtypeStruct((l, D_HALF), BF16)
    return pl.pallas_call(
        body, name=name, grid=(nb,),
        in_specs=[col(0), col(nb), col(2 * nb), col(0), w3],
        out_specs=[col(0), col(0), col(0), w3],
        out_shape=[piece, piece, piece, jax.ShapeDtypeStruct((CONV_WIDTH, D_HALF), F32)],
        compiler_params=_PAR)(proj, proj, proj, dcat, cw)


def _s5_prep(log_step, a_re, a_im, b_re, b_im):
    step = jnp.exp(log_step)[:, None]
    mag = jnp.exp(a_re * step)
    lr = mag * jnp.cos(a_im * step)
    li = mag * jnp.sin(a_im * step)
    nr = lr - 1.0
    den = a_re * a_re + a_im * a_im
    qr = (nr * a_re + li * a_im) / den
    qi = (li * a_re - nr * a_im) / den
    br = qr[..., None] * b_re - qi[..., None] * b_im
    bi = qr[..., None] * b_im + qi[..., None] * b_re
    return lr, li, br, bi


def _block_diag(m):
    nb, ng, r, c = m.shape
    eye = jnp.eye(ng, dtype=m.dtype)
    return jnp.einsum("bgrc,gh->bgrhc", m, eye).reshape(nb, ng * r, ng * c)


def _block_diag_extract(w, r, c):
    nb = w.shape[0]
    ng = w.shape[1] // r
    w5 = w.reshape(nb, ng, r, ng, c)
    return jnp.einsum("bgrhc,gh->bgrc", w5, jnp.eye(ng, dtype=w.dtype))


def _s5_mats(br, bi, c_re, c_im):
    g8 = N_SSM_GROUPS // S5_LANE_BLOCKS
    to_blk = lambda m: m.reshape(S5_LANE_BLOCKS, g8, m.shape[1], m.shape[2])
    wb = jnp.concatenate([_block_diag(to_blk(jnp.swapaxes(br, 1, 2))),
                          _block_diag(to_blk(jnp.swapaxes(bi, 1, 2)))], axis=2)
    wc = jnp.concatenate([_block_diag(to_blk(jnp.swapaxes(c_re, 1, 2))),
                          _block_diag(to_blk(jnp.swapaxes(-c_im, 1, 2)))], axis=1)
    return wb, wc


def _s5_mats_bwd(dwb, dwc):
    g, p, h = N_SSM_GROUPS, SSM_STATE, SSM_GROUP
    half = S5_STATE_LANES
    dbr = jnp.swapaxes(_block_diag_extract(dwb[:, :, :half], h, p).reshape(g, h, p), 1, 2)
    dbi = jnp.swapaxes(_block_diag_extract(dwb[:, :, half:], h, p).reshape(g, h, p), 1, 2)
    dcr = jnp.swapaxes(_block_diag_extract(dwc[:, :half, :], p, h).reshape(g, p, h), 1, 2)
    dci = -jnp.swapaxes(_block_diag_extract(dwc[:, half:, :], p, h).reshape(g, p, h), 1, 2)
    return dbr, dbi, dcr, dci


def _s5_scan_consts(log_step, a_re, a_im, reverse):
    step = jnp.exp(log_step)[:, None]
    xr = (a_re * step).reshape(S5_LANE_BLOCKS, 1, S5_STATE_LANES)
    xi = (a_im * step).reshape(S5_LANE_BLOCKS, 1, S5_STATE_LANES)
    if reverse:
        xi = -xi
    row = jnp.arange(SUBLANE, dtype=F32).reshape(1, SUBLANE, 1)

    def power(n):
        mag = jnp.exp(n * xr)
        return jnp.concatenate([mag * jnp.cos(n * xi), mag * jnp.sin(n * xi)], axis=-1)

    kinds = []
    for d in (1, 2, 4):
        keep = (row <= SUBLANE - 1 - d) if reverse else (row >= d)
        kinds.append(jnp.where(keep, power(jnp.full_like(row, float(d))), 0.0))
    kinds.append(power((SUBLANE - row) if reverse else (row + 1.0)))
    return jnp.stack(kinds, axis=1)


def _scan_rows(s_ref, sc_ref, carry_ref, n_rows, reverse):
    n_grp = n_rows // SUBLANE
    n_col = S5_STATE_LANES // LANE
    half = S5_STATE_LANES

    def step(i, carry):
        grp = (n_grp - 1 - i) if reverse else i
        r0 = pl.multiple_of(grp * SUBLANE, SUBLANE)
        out = []
        for cb in range(n_col):
            lo, hi = cb * LANE, half + cb * LANE
            re = s_ref[pl.ds(r0, SUBLANE), lo:lo + LANE]
            im = s_ref[pl.ds(r0, SUBLANE), hi:hi + LANE]
            for k, d in enumerate((1, 2, 4)):
                sh = (SUBLANE - d) if reverse else d
                rr, ri = pltpu.roll(re, sh, 0), pltpu.roll(im, sh, 0)
                ar, ai = sc_ref[k, :, lo:lo + LANE], sc_ref[k, :, hi:hi + LANE]
                re, im = re + (ar * rr - ai * ri), im + (ar * ri + ai * rr)
            pr, pi = sc_ref[3, :, lo:lo + LANE], sc_ref[3, :, hi:hi + LANE]
            cr, ci = carry[2 * cb], carry[2 * cb + 1]
            re, im = re + (pr * cr - pi * ci), im + (pr * ci + pi * cr)
            s_ref[pl.ds(r0, SUBLANE), lo:lo + LANE] = re
            s_ref[pl.ds(r0, SUBLANE), hi:hi + LANE] = im
            edge = 0 if reverse else SUBLANE - 1
            out.append(jnp.broadcast_to(re[edge:edge + 1, :], (SUBLANE, LANE)))
            out.append(jnp.broadcast_to(im[edge:edge + 1, :], (SUBLANE, LANE)))
        return tuple(out)

    init = []
    for cb in range(n_col):
        init.append(carry_ref[:, cb * LANE:(cb + 1) * LANE])
        init.append(carry_ref[:, half + cb * LANE:half + (cb + 1) * LANE])
    fin = lax.fori_loop(0, n_grp, step, tuple(init))
    for cb in range(n_col):
        carry_ref[:, cb * LANE:(cb + 1) * LANE] = fin[2 * cb]
        carry_ref[:, half + cb * LANE:half + (cb + 1) * LANE] = fin[2 * cb + 1]


def _s5_fwd(proj, wb, wc, d_skip, sc, *, name, exchanges=()):
    l = proj.shape[0]
    tt = _pick_tile(l, S5_TIME_CHUNK, SUBLANE)
    u_off = (proj.shape[1] - D_HALF) // LANE
    w2 = 2 * S5_STATE_LANES

    def body(u_ref, wb_ref, wc_ref, d_ref, sc_ref, s_ref, y_ref, carry_ref):
        @pl.when(pl.program_id(1) == 0)
        def _():
            carry_ref[...] = jnp.zeros_like(carry_ref)
        u = u_ref[...]
        s_ref[...] = _dot(u, wb_ref[0], _NN)
        _scan_rows(s_ref, sc_ref.at[0], carry_ref, tt, False)
        y_ref[...] = _dot(s_ref[...], wc_ref[0], _NN) + d_ref[...] * u

    return _call(
        body, name=name, grid=(S5_LANE_BLOCKS, l // tt),
        in_specs=[pl.BlockSpec((tt, LANE), lambda b, t: (t, b + u_off)),
                  pl.BlockSpec((1, LANE, w2), lambda b, t: (b, 0, 0)),
                  pl.BlockSpec((1, w2, LANE), lambda b, t: (b, 0, 0)),
                  pl.BlockSpec((1, LANE), lambda b, t: (0, b)),
                  pl.BlockSpec((1, 4, SUBLANE, w2), lambda b, t: (b, 0, 0, 0))],
        out_specs=[pl.BlockSpec((tt, w2), lambda b, t: (t, b)), pl.BlockSpec((tt, LANE), lambda b, t: (t, b))],
        out_shape=[jax.ShapeDtypeStruct((l, S5_LANE_BLOCKS * w2), F32), jax.ShapeDtypeStruct((l, D_HALF), F32)],
        scratch_shapes=[pltpu.VMEM((SUBLANE, w2), F32)],
        args=[proj, wb, wc, d_skip, sc], parallel=False, exchanges=exchanges)


def _s5_bwd(proj, dy, states, wb, wc, d_skip, sc_rev, *, name, exchanges=()):
    l = proj.shape[0]
    tt = _pick_tile(l, S5_TIME_CHUNK, SUBLANE)
    nt = l // tt
    u_off = (proj.shape[1] - D_HALF) // LANE
    w2 = 2 * S5_STATE_LANES
    half = S5_STATE_LANES
    grp_per_chunk = tt // SUBLANE

    def body(u_ref, dy_ref, s_ref, halo_ref, wb_ref, wc_ref, d_ref, sc_ref,
             du_ref, dwb_ref, dwc_ref, dlam_ref, dd_ref, g_scr, carry_ref):
        t = pl.program_id(1)

        @pl.when(t == 0)
        def _():
            carry_ref[...] = jnp.zeros_like(carry_ref)
            dwb_ref[...] = jnp.zeros_like(dwb_ref)
            dwc_ref[...] = jnp.zeros_like(dwc_ref)
            dlam_ref[...] = jnp.zeros_like(dlam_ref)
            dd_ref[...] = jnp.zeros_like(dd_ref)

        u = u_ref[...]
        dyv = dy_ref[...]
        g_scr[...] = _dot(dyv, wc_ref[0], _NT)
        _scan_rows(g_scr, sc_ref.at[0], carry_ref, tt, True)
        gv = g_scr[...]
        du_ref[...] = (_dot(gv, wb_ref[0], _NT) + d_ref[...] * dyv).astype(BF16)
        dwb_ref[0] += _dot(u, gv, _TN)
        sv = s_ref[...]
        dwc_ref[0] += _dot(sv, dyv, _TN)
        dd_ref[...] += jnp.sum(dyv * u, axis=0, keepdims=True)
        first_chunk = t == nt - 1
        halo = jnp.where(first_chunk, 0.0, halo_ref[SUBLANE - 1:SUBLANE, :])
        row = lax.broadcasted_iota(jnp.int32, sv.shape, 0)
        sp = jnp.where(row == 0, jnp.broadcast_to(halo, sv.shape), pltpu.roll(sv, 1, 0))
        gr, gi = gv[:, :half], gv[:, half:]
        sr, si = sp[:, :half], sp[:, half:]
        dlr = jnp.sum(gr * sr + gi * si, axis=0, keepdims=True)
        dli = jnp.sum(gi * sr - gr * si, axis=0, keepdims=True)
        dlam_ref[0] += jnp.concatenate([dlr, dli], axis=1)

    rev = lambda t: nt - 1 - t
    return _call(
        body, name=name, grid=(S5_LANE_BLOCKS, nt),
        in_specs=[pl.BlockSpec((tt, LANE), lambda b, t: (rev(t), b + u_off)),
                  pl.BlockSpec((tt, LANE), lambda b, t: (rev(t), b)),
                  pl.BlockSpec((tt, w2), lambda b, t: (rev(t), b)),
                  pl.BlockSpec((SUBLANE, w2), lambda b, t: (jnp.maximum(rev(t) * grp_per_chunk - 1, 0), b)),
                  pl.BlockSpec((1, LANE, w2), lambda b, t: (b, 0, 0)),
                  pl.BlockSpec((1, w2, LANE), lambda b, t: (b, 0, 0)),
                  pl.BlockSpec((1, LANE), lambda b, t: (0, b)),
                  pl.BlockSpec((1, 4, SUBLANE, w2), lambda b, t: (b, 0, 0, 0))],
        out_specs=[pl.BlockSpec((tt, LANE), lambda b, t: (rev(t), b)),
                   pl.BlockSpec((1, LANE, w2), lambda b, t: (b, 0, 0)),
                   pl.BlockSpec((1, w2, LANE), lambda b, t: (b, 0, 0)),
                   pl.BlockSpec((1, 1, w2), lambda b, t: (b, 0, 0)),
                   pl.BlockSpec((1, LANE), lambda b, t: (0, b))],
        out_shape=[jax.ShapeDtypeStruct((l, D_HALF), BF16),
                   jax.ShapeDtypeStruct((S5_LANE_BLOCKS, LANE, w2), F32),
                   jax.ShapeDtypeStruct((S5_LANE_BLOCKS, w2, LANE), F32),
                   jax.ShapeDtypeStruct((S5_LANE_BLOCKS, 1, w2), F32),
                   jax.ShapeDtypeStruct((1, D_HALF), F32)],
        scratch_shapes=[pltpu.VMEM((tt, w2), F32), pltpu.VMEM((SUBLANE, w2), F32)],
        args=[proj, dy, states, states, wb, wc, d_skip, sc_rev], parallel=False, exchanges=exchanges)


def _glu_fwd(ypre, wg, bg, *, name):
    l, d = ypre.shape
    tl = _pick_tile(l, 512, SUBLANE)

    def body(y_ref, w_ref, b_ref, o_ref):
        yg = _gelu(y_ref[...])
        o_ref[...] = (yg * _sigmoid(_dot(yg, w_ref[...], _NN) + b_ref[...])).astype(BF16)

    row = pl.BlockSpec((tl, d), lambda i: (i, 0))
    return pl.pallas_call(
        body, name=name, grid=(l // tl,),
        in_specs=[row, pl.BlockSpec((d, d), lambda i: (0, 0)), pl.BlockSpec((1, d), lambda i: (0, 0))],
        out_specs=row, out_shape=jax.ShapeDtypeStruct((l, d), BF16), compiler_params=_PAR)(ypre, wg, bg)


def _glu_bwd(dcat, ypre, wg, bg, *, name):
    l, d = ypre.shape
    tl = _pick_tile(l, 512, SUBLANE)

    def body(dy_ref, y_ref, w_ref, b_ref, dyp_ref, dw_ref, db_ref):
        @pl.when(pl.program_id(0) == 0)
        def _():
            dw_ref[...] = jnp.zeros_like(dw_ref)
            db_ref[...] = jnp.zeros_like(db_ref)
        yp = y_ref[...]
        dyb = dy_ref[...]
        yg = _gelu(yp)
        sg = _sigmoid(_dot(yg, w_ref[...], _NN) + b_ref[...])
        dz = dyb * yg * sg * (1.0 - sg)
        dyg = dyb * sg + _dot(dz, w_ref[...], _NT)
        dyp_ref[...] = dyg * _gelu_grad(yp)
        dw_ref[...] += _dot(yg, dz, _TN)
        db_ref[...] += jnp.sum(dz, axis=0, keepdims=True)

    row = pl.BlockSpec((tl, d), lambda i: (i, 0))
    mat = pl.BlockSpec((d, d), lambda i: (0, 0))
    vec = pl.BlockSpec((1, d), lambda i: (0, 0))
    return pl.pallas_call(
        body, name=name, grid=(l // tl,),
        in_specs=[pl.BlockSpec((tl, d), lambda i: (i, 1)), row, mat, vec], out_specs=[row, mat, vec],
        out_shape=[jax.ShapeDtypeStruct((l, d), F32), jax.ShapeDtypeStruct((d, d), F32),
                   jax.ShapeDtypeStruct((1, d), F32)],
        compiler_params=_ARB)(dcat, ypre, wg, bg)


def _window_sum(x, w, trailing):
    s, d = x, 1
    while d < w:
        s = s + (_shift_dn(s, d) if trailing else _shift_up(s, d))
        d *= 2
    return s


def _window_count(shape, w):
    row = lax.broadcasted_iota(jnp.int32, shape, 0)
    return jnp.minimum(row + 1, w).astype(F32)


def _pool_fwd(proj, pw, scale, *, name):
    l = proj.shape[0]
    ng = len(POOL_WINDOWS)

    def body(z_ref, w_ref, sc_ref, o_ref):
        z = z_ref[...]
        for k, w in enumerate(POOL_WINDOWS):
            @pl.when(pl.program_id(0) == k)
            def _():
                pooled = _window_sum(z, w, True) / _window_count(z.shape, w) - z
                o_ref[...] = (_dot(pooled, w_ref[0], _NN) * sc_ref[...]).astype(BF16)

    col = pl.BlockSpec((l, LANE), lambda g: (0, g))
    return pl.pallas_call(
        body, name=name, grid=(ng,),
        in_specs=[col, pl.BlockSpec((1, LANE, LANE), lambda g: (g, 0, 0)), pl.BlockSpec((1, LANE), lambda g: (0, g))],
        out_specs=col, out_shape=jax.ShapeDtypeStruct((l, D_HALF), BF16), compiler_params=_PAR)(proj, pw, scale)


def _pool_bwd(proj, dcat, pw, scale, *, name):
    l = proj.shape[0]
    ng = len(POOL_WINDOWS)

    def body(z_ref, dy_ref, w_ref, sc_ref, dz_ref, dw_ref, dsc_ref):
        z = z_ref[...]
        dy = dy_ref[...]
        for k, w in enumerate(POOL_WINDOWS):
            @pl.when(pl.program_id(0) == k)
            def _():
                cnt = _window_count(z.shape, w)
                pooled = _window_sum(z, w, True) / cnt - z
                ypre = _dot(pooled, w_ref[0], _NN)
                dsc_ref[...] = jnp.sum(dy * ypre, axis=0, keepdims=True)
                dyp = dy * sc_ref[...]
                dw_ref[0] = _dot(pooled, dyp, _TN)
                dpool = _dot(dyp, w_ref[0], _NT)
                dz_ref[...] = (_window_sum(dpool / cnt, w, False) - dpool).astype(BF16)

    col = pl.BlockSpec((l, LANE), lambda g: (0, g))
    mat = pl.BlockSpec((1, LANE, LANE), lambda g: (g, 0, 0))
    vec = pl.BlockSpec((1, LANE), lambda g: (0, g))
    return pl.pallas_call(
        body, name=name, grid=(ng,), in_specs=[col, col, mat, vec], out_specs=[col, mat, vec],
        out_shape=[jax.ShapeDtypeStruct((l, D_HALF), BF16), jax.ShapeDtypeStruct((ng, LANE, LANE), F32),
                   jax.ShapeDtypeStruct((1, D_HALF), F32)],
        compiler_params=_PAR)(proj, dcat, pw, scale)


def _tril_mask():
    r = lax.broadcasted_iota(jnp.int32, (CHUNK, CHUNK), 0)
    c = lax.broadcasted_iota(jnp.int32, (CHUNK, CHUNK), 1)
    return r >= c


def _sgu_fwd(proj, ng, sw, sb_t, *, name):
    l = proj.shape[0]
    tl = _pick_tile(l, 512, CHUNK)

    def body(su_ref, sv_ref, g_ref, w_ref, b_ref, o_ref):
        su = _gelu(su_ref[...])
        sv = _gelu(sv_ref[...])
        r = lax.rsqrt(jnp.mean(sv * sv, axis=-1, keepdims=True) + EPS)
        v = sv * r * g_ref[...]
        mask = _tril_mask()
        for h in range(SGU_HEADS):
            wm = jnp.where(mask, w_ref[h], 0.0)
            cs = slice(h * LANE, (h + 1) * LANE)
            for n in range(tl // CHUNK):
                rs = slice(n * CHUNK, (n + 1) * CHUNK)
                mixed = _dot(wm, v[rs, cs], _NN) + b_ref[:, h:h + 1]
                o_ref[rs, cs] = (su[rs, cs] * mixed).astype(BF16)

    blk = lambda c: pl.BlockSpec((tl, D_HALF), lambda i: (i, c))
    return pl.pallas_call(
        body, name=name, grid=(l // tl,),
        in_specs=[blk(1), blk(2), pl.BlockSpec((1, D_HALF), lambda i: (0, 0)),
                  pl.BlockSpec((SGU_HEADS, CHUNK, CHUNK), lambda i: (0, 0, 0)),
                  pl.BlockSpec((CHUNK, SGU_HEADS), lambda i: (0, 0))],
        out_specs=blk(0), out_shape=jax.ShapeDtypeStruct((l, D_HALF), BF16),
        compiler_params=_PAR)(proj, proj, ng, sw, sb_t)


def _sgu_bwd(proj, dcat, ng, sw, sb_t, *, name):
    l = proj.shape[0]
    tl = _pick_tile(l, 512, CHUNK)

    def body(su_ref, sv_ref, dy_ref, g_ref, w_ref, b_ref, dsu_ref, dsv_ref, dw_ref, dbm_ref, dng_ref, dv_scr):
        @pl.when(pl.program_id(0) == 0)
        def _():
            dw_ref[...] = jnp.zeros_like(dw_ref)
            dbm_ref[...] = jnp.zeros_like(dbm_ref)
            dng_ref[...] = jnp.zeros_like(dng_ref)
        su_pre = su_ref[...]
        sv_pre = sv_ref[...]
        su = _gelu(su_pre)
        sv = _gelu(sv_pre)
        gsu = _gelu_grad(su_pre)
        gv = g_ref[...]
        r = lax.rsqrt(jnp.mean(sv * sv, axis=-1, keepdims=True) + EPS)
        v = sv * r * gv
        dy = dy_ref[...]
        mask = _tril_mask()
        for h in range(SGU_HEADS):
            wm = jnp.where(mask, w_ref[h], 0.0)
            cs = slice(h * LANE, (h + 1) * LANE)
            dw_acc = jnp.zeros((CHUNK, CHUNK), F32)
            db_acc = jnp.zeros((CHUNK, LANE), F32)
            for n in range(tl // CHUNK):
                rs = slice(n * CHUNK, (n + 1) * CHUNK)
                vb = v[rs, cs]
                mixed = _dot(wm, vb, _NN) + b_ref[:, h:h + 1]
                dyb = dy[rs, cs]
                dsu_ref[rs, cs] = (dyb * mixed * gsu[rs, cs]).astype(BF16)
                dmix = dyb * su[rs, cs]
                db_acc = db_acc + dmix
                dw_acc = dw_acc + _dot(dmix, vb, _NT)
                dv_scr[rs, cs] = _dot(wm, dmix, _TN)
            dw_ref[h] += jnp.where(mask, dw_acc, 0.0)
            dbm_ref[h] += db_acc
        dv = dv_scr[...]
        a = dv * gv
        m = jnp.mean(a * sv, axis=-1, keepdims=True)
        dsv = r * a - sv * (r * r * r) * m
        dng_ref[...] += jnp.sum(dv * sv * r, axis=0, keepdims=True)
        dsv_ref[...] = (dsv * _gelu_grad(sv_pre)).astype(BF16)

    blk = lambda c: pl.BlockSpec((tl, D_HALF), lambda i: (i, c))
    mats = pl.BlockSpec((SGU_HEADS, CHUNK, CHUNK), lambda i: (0, 0, 0))
    vec = pl.BlockSpec((1, D_HALF), lambda i: (0, 0))
    piece = jax.ShapeDtypeStruct((l, D_HALF), BF16)
    mshape = jax.ShapeDtypeStruct((SGU_HEADS, CHUNK, CHUNK), F32)
    return pl.pallas_call(
        body, name=name, grid=(l // tl,),
        in_specs=[blk(1), blk(2), blk(1), vec, mats, pl.BlockSpec((CHUNK, SGU_HEADS), lambda i: (0, 0))],
        out_specs=[blk(0), blk(0), mats, mats, vec],
        out_shape=[piece, piece, mshape, mshape, jax.ShapeDtypeStruct((1, D_HALF), F32)],
        scratch_shapes=[pltpu.VMEM((tl, D_HALF), F32)],
        compiler_params=_ARB)(proj, proj, dcat, ng, sw, sb_t)


def _s5_params(w):
    prep_args = (w["ssm_log_step"], w["ssm_a_re"], w["ssm_a_im"], w["ssm_b_re"], w["ssm_b_im"])
    (lr, li, br, bi), prep_vjp = jax.vjp(jax.vmap(_s5_prep), *prep_args)
    wb, wc = jax.vmap(_s5_mats)(br, bi, w["ssm_c_re"], w["ssm_c_im"])
    consts = lambda reverse: jax.vmap(functools.partial(_s5_scan_consts, reverse=reverse))(*prep_args[:3])
    return dict(wb=wb.astype(BF16), wc=wc.astype(BF16), d=w["ssm_d"][:, None, :], sc=consts(False),
                sc_rev=consts(True), prep_vjp=prep_vjp)


def _s5_param_grads(s5, dwb, dwc, dlam, dd):
    dbr, dbi, dcr, dci = jax.vmap(_s5_mats_bwd)(dwb, dwc)
    n = dlam.shape[0]
    dlr = dlam[:, :, 0, :S5_STATE_LANES].reshape(n, N_SSM_GROUPS, SSM_STATE)
    dli = dlam[:, :, 0, S5_STATE_LANES:].reshape(n, N_SSM_GROUPS, SSM_STATE)
    dls, dar, dai, db_re, db_im = s5["prep_vjp"]((dlr, dli, dbr, dbi))
    return dict(ssm_log_step=dls, ssm_a_re=dar, ssm_a_im=dai, ssm_b_re=db_re, ssm_b_im=db_im, ssm_c_re=dcr,
                ssm_c_im=dci, ssm_d=dd[:, 0, :])


FFN_TILES = {
    "mm_up": [(512, 1408), (1024, 1408), (1024, 512), (2048, 512)],
    "mm_up_dw": [(256, 1408), (512, 512), (512, 1408), (1024, 512)],
    "mm_down_dx": [(512, 1408), (1024, 1408), (1024, 2816), (2048, 1408)],
    "mm_down": [(512, 1024), (256, 1024), (512, 1024), (512, 1024)],
    "mm_down_dw": [(256, 1024), (256, 512), (128, 1024), (256, 1024)],
}


def _layer_weights(i):
    j = i // 2
    mixer = [("even_w_in", j), ("even_w_out", j), ("ssm_glu_w", j)] if i % 2 == 0 else [("odd_w_in", j),
                                                                                         ("odd_w_out", j)]
    return dict(w_in=mixer[0], w_out=mixer[1], glu=mixer[2:], up=("ffn_w_up", i), down=("ffn_w_down", i))


class _LocalWeights:
    def __init__(self, w):
        self.w, self.grads = w, {}

    def carried_by(self, stage, i):
        return []

    def delivered(self, stage, i, outs):
        pass

    def weight(self, key):
        return self.w[key[0]][key[1]]

    def grad(self, key, dw):
        self.grads[key] = dw


class _ShardedWeights:
    def __init__(self, shards):
        self.shards = shards
        self.full, self.pending, self.scattered = {}, {}, {}

    def start(self, others):
        keys = [_layer_weights(0)["w_in"]]
        outs = _exchange_only(others + [self._gather(k) for k in keys], name="ag_first")
        self._take(keys, outs[len(others):])
        return outs[:len(others)]

    def _gather(self, key):
        return _Gather(self.shards[key[0]][key[1]])

    def _take(self, keys, outs):
        for key, got in zip(keys, outs):
            if BIG[key[0]] == 2:
                self.full[key] = jnp.swapaxes(got, 0, 1).reshape(got.shape[1], -1)
            else:
                self.full[key] = got.reshape(-1, got.shape[2])

    def _plan(self, stage, i):
        cur = _layer_weights(i)
        nxt = _layer_weights(i + 1) if i + 1 < DEPTH else None
        has_scan = lambda k: k % 2 == 0
        none = ([], [])
        return {
            "mm_in": ([cur["w_out"], *cur["glu"]], []) if i == 0 else none,
            "s5_fwd": ([cur["up"]], []),
            "mm_up": ([cur["down"]] + ([nxt["w_in"], nxt["w_out"], *nxt["glu"]] if nxt else []), []),
            "ffn_act": ([nxt["up"]], []) if nxt and not has_scan(i + 1) else none,
            "ffn_act_bwd": ([], [cur["down"]] + ([nxt["up"]] if nxt and not has_scan(i + 1) else [])),
            "s5_bwd": ([], [cur["up"]]),
            "mm_up_dx": ([], [nxt["w_in"], *nxt["glu"]]) if nxt else none,
            "mm_up_dw": ([], [nxt["w_out"]]) if nxt else none,
            "mm_in_dx": ([], [cur["w_out"]]) if i == 0 else none,
            "mm_in_dw": ([], list(cur["glu"])) if i == 0 else none,
        }[stage]

    def _scatter(self, key):
        name, layer = key
        src = self.pending.pop(key)
        layers, rows, cols = self.shards[name].shape
        if name not in self.scattered:
            self.scattered[name] = lax.empty((N_DEV, layers * rows, cols), src.dtype)
        return _Scatter(src, into=self.scattered[name], row0=layer * rows)

    def carried_by(self, stage, i):
        gather, scatter = self._plan(stage, i)
        return [self._gather(k) for k in gather] + [self._scatter(k) for k in scatter]

    def delivered(self, stage, i, outs):
        gather, scatter = self._plan(stage, i)
        self._take(gather, outs[:len(gather)])
        for (name, _), buf in zip(scatter, outs[len(gather):]):
            self.scattered[name] = buf

    def weight(self, key):
        return self.full[key]

    def grad(self, key, dw):
        self.pending[key] = _to_dest_major(dw, BIG[key[0]] - 1).astype(BF16)

    def finish(self, others):
        keys = list(self.pending)
        outs = _exchange_only(others + [self._scatter(k) for k in keys], name="rs_last")
        for (name, _), buf in zip(keys, outs[len(others):]):
            self.scattered[name] = buf
        return outs[:len(others)]


def _device_step(x, tgt, w, comm):
    saved = []
    s5 = _s5_params(w)
    h = _rms_fwd(x, w["norm_mix_g"][0:1], name="rms_fwd")
    for i in range(DEPTH):
        j = i // 2
        lw = _layer_weights(i)
        if i % 2 == 0:
            proj = _carry(comm, "mm_in", i, _mm, h, comm.weight(lw["w_in"]), "nn", name="mm_even_in")
            ya = _sconv_fwd(proj, w["even_conv_w"][j], name="sconv_fwd")
            states, ypre = _carry(comm, "s5_fwd", i, _s5_fwd, proj, s5["wb"][j], s5["wc"][j], s5["d"][j],
                                  s5["sc"][j], name="s5_fwd")
            yb = _glu_fwd(ypre, comm.weight(lw["glu"][0]), w["ssm_glu_b"][j][None, :], name="glu_fwd")
            cat = jnp.concatenate([ya, yb], axis=1)
            mix = (states, ypre)
        else:
            proj = _carry(comm, "mm_in", i, _mm, h, comm.weight(lw["w_in"]), "nn", name="mm_odd_in")
            yc = _pool_fwd(proj, w["pool_w"][j], w["pool_scale"][j][None, :], name="pool_fwd")
            sb_t = jnp.transpose(w["sgu_b"][j])
            yd = _sgu_fwd(proj, w["sgu_norm_g"][j][None, :], w["sgu_w"][j], sb_t, name="sgu_fwd")
            cat = jnp.concatenate([yc, yd], axis=1)
            mix = (sb_t,)
        x1, h2 = _mm(cat, comm.weight(lw["w_out"]), "nn", add=x, norm=(w["norm_ffn_g"], i), name="mm_mix_out")
        up = _carry(comm, "mm_up", i, _mm, h2, comm.weight(lw["up"]), "nn", tiles=FFN_TILES["mm_up"][i],
                    name="mm_up_L%d" % i)
        act = _carry(comm, "ffn_act", i, _ffn_act, up, w["ffn_conv_w"][i], w["ffn_conv_b"][i:i + 1], name="ffn_act")
        if i + 1 < DEPTH:
            x2, h_next = _mm(act, comm.weight(lw["down"]), "nn", add=x1, norm=(w["norm_mix_g"], i + 1),
                             tiles=FFN_TILES["mm_down"][i], name="mm_down_L%d" % i)
        else:
            x2, h_next = _mm(act, comm.weight(lw["down"]), "nn", add=x1, name="mm_down_last"), None
        saved.append((x, h, proj, cat, x1, h2, up, act, mix))
        x, h = x2, h_next

    loss, dx, dxb, dgf = _loss_head(x, w["norm_final_g"][None, :], tgt, name="loss_head")
    per_layer = {}
    s5_grads = []

    def put(name, idx, val):
        per_layer.setdefault(name, {})[idx] = val

    for i in reversed(range(DEPTH)):
        j = i // 2
        lw = _layer_weights(i)
        x0, h, proj, cat, x1, h2, up, act, mix = saved[i]
        dact = _mm(dxb, comm.weight(lw["down"]), "nt", tiles=FFN_TILES["mm_down_dx"][i], name="mm_down_dx_L%d" % i)
        comm.grad(lw["down"], _mm(act, dxb, "tn", out_dtype=BF16, tiles=FFN_TILES["mm_down_dw"][i],
                                  name="mm_down_dw_L%d" % i))
        dup, dcw, dcb = _carry(comm, "ffn_act_bwd", i, _ffn_act_bwd, up, dact, w["ffn_conv_w"][i],
                               w["ffn_conv_b"][i:i + 1], name="ffn_act_bwd")
        put("ffn_conv_w", i, dcw)
        put("ffn_conv_b", i, dcb[0])
        dx1, dx1b, dg2 = _carry(comm, "mm_up_dx", i, _mm, dup, comm.weight(lw["up"]), "nt", tm_cap=256, tn_cap=D_MODEL,
                                norm_bwd=(x1, w["norm_ffn_g"], i, dx), name="mm_up_dx")
        comm.grad(lw["up"], _carry(comm, "mm_up_dw", i, _mm, h2, dup, "tn", out_dtype=BF16,
                                   tiles=FFN_TILES["mm_up_dw"][i], name="mm_up_dw_L%d" % i))
        put("norm_ffn_g", i, dg2[0])
        dcat = _mm(dx1b, comm.weight(lw["w_out"]), "nt", name="mm_mix_out_dx")
        comm.grad(lw["w_out"], _mm(cat, dx1b, "tn", out_dtype=BF16, name="mm_mix_out_dw"))
        if i % 2 == 0:
            states, ypre = mix
            dxa, dba, dca, dcw_a = _sconv_bwd(proj, dcat, w["even_conv_w"][j], name="sconv_bwd")
            put("even_conv_w", j, dcw_a)
            dypre, dwg, dbg = _glu_bwd(dcat, ypre, comm.weight(lw["glu"][0]), w["ssm_glu_b"][j][None, :],
                                       name="glu_bwd")
            comm.grad(lw["glu"][0], dwg)
            put("ssm_glu_b", j, dbg[0])
            du, dwb, dwc, dlam, dd = _carry(comm, "s5_bwd", i, _s5_bwd, proj, dypre, states, s5["wb"][j], s5["wc"][j],
                                            s5["d"][j], s5["sc_rev"][j], name="s5_bwd")
            s5_grads.insert(0, (dwb, dwc, dlam, dd))
            dproj = jnp.concatenate([dxa, dba, dca, du], axis=1)
            in_name = "mm_even_in"
        else:
            (sb_t,) = mix
            dz, dpw, dps = _pool_bwd(proj, dcat, w["pool_w"][j], w["pool_scale"][j][None, :], name="pool_bwd")
            put("pool_w", j, dpw)
            put("pool_scale", j, dps[0])
            dsu, dsv, dsw, dbm, dng = _sgu_bwd(proj, dcat, w["sgu_norm_g"][j][None, :], w["sgu_w"][j], sb_t,
                                               name="sgu_bwd")
            put("sgu_w", j, dsw)
            put("sgu_b", j, jnp.sum(dbm, axis=-1))
            put("sgu_norm_g", j, dng[0])
            dproj = jnp.concatenate([dz, dsu, dsv], axis=1)
            in_name = "mm_odd_in"
        dx, dxb, dg1 = _carry(comm, "mm_in_dx", i, _mm, dproj, comm.weight(lw["w_in"]), "nt", tn_cap=D_MODEL,
                              norm_bwd=(x0, w["norm_mix_g"], i, dx1), name=in_name + "_dx")
        comm.grad(lw["w_in"], _carry(comm, "mm_in_dw", i, _mm, h, dproj, "tn", out_dtype=BF16, name=in_name + "_dw"))
        put("norm_mix_g", i, dg1[0])

    grads = {nm: jnp.stack([vals[k] for k in sorted(vals)]) for nm, vals in per_layer.items()}
    grads["norm_final_g"] = dgf[0]
    grads.update(_s5_param_grads(s5, *[jnp.stack(parts) for parts in zip(*s5_grads)]))
    return loss, dx, grads


def _carry(comm, stage, i, fn, *args, **kwargs):
    exchanges = comm.carried_by(stage, i)
    if not exchanges:
        return fn(*args, **kwargs)
    out, moved = fn(*args, exchanges=exchanges, **kwargs)
    comm.delivered(stage, i, moved)
    return out


def _sum_parts(parts, *, name):
    g, r, c = parts.shape
    tr = _pick_tile(r, max(16, EXCHANGE_BLOCK_ELEMS // c), 16)

    def body(p_ref, o_ref):
        acc = p_ref[0].astype(F32)
        for k in range(1, g):
            acc = acc + p_ref[k].astype(F32)
        o_ref[...] = acc

    return pl.pallas_call(
        body, name=name, grid=(r // tr,), in_specs=[pl.BlockSpec((g, tr, c), lambda i: (0, i, 0))],
        out_specs=pl.BlockSpec((tr, c), lambda i: (i, 0)), out_shape=jax.ShapeDtypeStruct((r, c), F32),
        compiler_params=_PAR)(parts)


def _adamw(w, m, v, g_parts, *, name):
    r, c = w.shape
    g = g_parts.shape[0]
    tc = _pick_tile(c, 8192, LANE)
    tr = _pick_tile(r, max(16, (1 << 18) // tc), 16)
    c1 = 1.0 - ADAM_B1 ** ADAM_STEP
    c2 = 1.0 - ADAM_B2 ** ADAM_STEP

    def body(w_ref, m_ref, v_ref, g_ref, go_ref, d_ref, mo_ref, vo_ref):
        grad = g_ref[0].astype(F32)
        for k in range(1, g):
            grad = grad + g_ref[k].astype(F32)
        m_new = ADAM_B1 * m_ref[...] + (1.0 - ADAM_B1) * grad
        v_new = ADAM_B2 * v_ref[...] + (1.0 - ADAM_B2) * (grad * grad)
        go_ref[...] = grad
        mo_ref[...] = m_new
        vo_ref[...] = v_new
        d_ref[...] = -ADAM_LR * ((m_new / c1) / (jnp.sqrt(v_new / c2) + ADAM_EPS) + ADAM_WD * w_ref[...])

    blk = pl.BlockSpec((tr, tc), lambda i, j: (i, j))
    out = jax.ShapeDtypeStruct((r, c), F32)
    return pl.pallas_call(
        body, name=name, grid=(r // tr, c // tc),
        in_specs=[blk, blk, blk, pl.BlockSpec((g, tr, tc), lambda i, j: (0, i, j))],
        out_specs=[blk, blk, blk, blk], out_shape=[out, out, out, out], compiler_params=_PAR2)(w, m, v, g_parts)


WEIGHT_NAMES = ['norm_mix_g', 'even_w_in', 'even_conv_w', 'ssm_log_step', 'ssm_a_re', 'ssm_a_im', 'ssm_b_re',
                'ssm_b_im', 'ssm_c_re', 'ssm_c_im', 'ssm_d', 'ssm_glu_w', 'ssm_glu_b', 'even_w_out', 'odd_w_in',
                'pool_w', 'pool_scale', 'sgu_norm_g', 'sgu_w', 'sgu_b', 'odd_w_out', 'norm_ffn_g', 'ffn_w_up',
                'ffn_conv_w', 'ffn_conv_b', 'ffn_w_down', 'norm_final_g']
BIG = {'even_w_in': 2, 'ssm_glu_w': 1, 'even_w_out': 1, 'odd_w_in': 2, 'odd_w_out': 1, 'ffn_w_up': 2,
       'ffn_w_down': 1}
SMALL = {'even_conv_w': 2, 'pool_scale': 1, 'sgu_norm_g': 1, 'ffn_conv_w': 2}
BIG_ROWS = 512
SMALL_ROWS = 16


def _pad_to(n, q):
    return -(-n // q) * q


def _pack(arrays, dtype, rows, lead=()):
    flat = [a.reshape(lead + (-1,)).astype(dtype) for a in arrays]
    n = sum(f.shape[-1] for f in flat)
    pad = _pad_to(n, rows * LANE) - n
    if pad:
        flat.append(jnp.zeros(lead + (pad,), dtype))
    return jnp.concatenate(flat, axis=-1).reshape(lead + (rows, -1))


def _unpack(buf, shapes, lead=()):
    flat = buf.reshape(lead + (-1,))
    out, off = [], 0
    for shp in shapes:
        n = math.prod(shp)
        out.append(flat[..., off:off + n].reshape(lead + tuple(shp)))
        off += n
    return out


def _to_dest_major(full, axis):
    shp = full.shape
    split = full.reshape(shp[:axis] + (N_DEV, shp[axis] // N_DEV) + shp[axis + 1:])
    return jnp.moveaxis(split, axis, 0)


def _from_dest_major(blocks, axis):
    moved = jnp.moveaxis(blocks, 0, axis)
    shp = moved.shape
    return moved.reshape(shp[:axis] + (shp[axis] * shp[axis + 1],) + shp[axis + 2:])


def _rows_2d(a):
    return a.reshape(-1, a.shape[-1])


def _rows_2d_lead(a):
    return a.reshape(a.shape[0], -1, a.shape[-1])


def kernel(x, norm_mix_g, even_w_in, even_conv_w, ssm_log_step, ssm_a_re, ssm_a_im, ssm_b_re, ssm_b_im, ssm_c_re, ssm_c_im, ssm_d, ssm_glu_w, ssm_glu_b, even_w_out, odd_w_in, pool_w, pool_scale, sgu_norm_g, sgu_w, sgu_b, odd_w_out, norm_ffn_g, ffn_w_up, ffn_conv_w, ffn_conv_b, ffn_w_down, norm_final_g, loss_target, m_norm_mix_g, m_even_w_in, m_even_conv_w, m_ssm_log_step, m_ssm_a_re, m_ssm_a_im, m_ssm_b_re, m_ssm_b_im, m_ssm_c_re, m_ssm_c_im, m_ssm_d, m_ssm_glu_w, m_ssm_glu_b, m_even_w_out, m_odd_w_in, m_pool_w, m_pool_scale, m_sgu_norm_g, m_sgu_w, m_sgu_b, m_odd_w_out, m_norm_ffn_g, m_ffn_w_up, m_ffn_conv_w, m_ffn_conv_b, m_ffn_w_down, m_norm_final_g, v_norm_mix_g, v_even_w_in, v_even_conv_w, v_ssm_log_step, v_ssm_a_re, v_ssm_a_im, v_ssm_b_re, v_ssm_b_im, v_ssm_c_re, v_ssm_c_im, v_ssm_d, v_ssm_glu_w, v_ssm_glu_b, v_even_w_out, v_odd_w_in, v_pool_w, v_pool_scale, v_sgu_norm_g, v_sgu_w, v_sgu_b, v_odd_w_out, v_norm_ffn_g, v_ffn_w_up, v_ffn_conv_w, v_ffn_conv_b, v_ffn_w_down, v_norm_final_g):
    given = dict(locals())
    wts = {n: given[n] for n in WEIGHT_NAMES}
    mom = {n: given["m_" + n] for n in WEIGHT_NAMES}
    var = {n: given["v_" + n] for n in WEIGHT_NAMES}
    repl = [n for n in WEIGHT_NAMES if n not in BIG and n not in SMALL]

    small_shapes = [wts[n].shape for n in SMALL]
    comm = _ShardedWeights({n: wts[n].astype(BF16) for n in BIG})
    (small_all,) = comm.start([_Gather(_pack([wts[n] for n in SMALL], F32, SMALL_ROWS))])
    full = {n: wts[n] for n in repl}
    for n, blocks in zip(SMALL, _unpack(small_all, small_shapes, lead=(N_DEV,))):
        full[n] = _from_dest_major(blocks, SMALL[n])

    loss, dx, grads = _device_step(x[0], loss_target[0], full, comm)

    repl_shapes = [wts[n].shape for n in repl]
    repl_flat = jnp.concatenate([grads[n].reshape(-1) for n in repl] + [loss.reshape(-1)])
    n_repl = repl_flat.shape[0]
    chunk = _pad_to(-(-n_repl // N_DEV), SMALL_ROWS * LANE)
    repl_flat = jnp.pad(repl_flat, (0, N_DEV * chunk - n_repl))
    small_part = _pack([_to_dest_major(grads[n], SMALL[n]) for n in SMALL], F32, SMALL_ROWS, lead=(N_DEV,))
    small_cols = small_part.shape[2]
    (small_rs,) = comm.finish([_Scatter(
        jnp.concatenate([small_part, repl_flat.reshape(N_DEV, SMALL_ROWS, chunk // SMALL_ROWS)], axis=2))])
    small_sum = _sum_parts(small_rs, name="rs_sum_small")
    (repl_all,) = _exchange_only([_Gather(small_sum[:, small_cols:])], name="ag_repl")
    repl_sum = repl_all.reshape(-1)
    total_loss = repl_sum[n_repl - 1]

    out = {}
    for n in BIG:
        shp = wts[n].shape
        res = _adamw(_rows_2d(wts[n]), _rows_2d(mom[n]), _rows_2d(var[n]), comm.scattered[n], name="adamw_" + n)
        out[n] = [r.reshape(shp) for r in res]

    def small_vec(shard_part, repl_part):
        flat = jnp.concatenate([shard_part.reshape(-1), repl_part])
        return flat.reshape(SMALL_ROWS, -1)

    def small_tree(tree):
        tail = jnp.concatenate([tree[n].reshape(-1) for n in repl])
        tail = jnp.pad(tail, (0, N_DEV * chunk - tail.shape[0]))
        return small_vec(_pack([tree[n] for n in SMALL], F32, SMALL_ROWS), tail)

    res = _adamw(small_tree(wts), small_tree(mom), small_tree(var), small_vec(small_sum[:, :small_cols], repl_sum)[None],
                 name="adamw_small")
    n_small = SMALL_ROWS * small_cols
    for k, r in enumerate(res):
        flat = r.reshape(-1)
        shard = _unpack(flat[:n_small], small_shapes)
        rest = _unpack(flat[n_small:], repl_shapes)
        for n, val in zip(SMALL, shard):
            out.setdefault(n, [None] * 4)[k] = val
        for n, val in zip(repl, rest):
            out.setdefault(n, [None] * 4)[k] = val

    grad_x = dx[None]
    return (total_loss, grad_x, *[out[n][0] for n in WEIGHT_NAMES], *[out[n][1] for n in WEIGHT_NAMES],
            *[out[n][2] for n in WEIGHT_NAMES], *[out[n][3] for n in WEIGHT_NAMES])
```

```python
import functools
import math

import jax
import jax.numpy as jnp
from jax import lax
from jax.experimental import pallas as pl
from jax.experimental.pallas import tpu as pltpu

F32 = jnp.float32
BF16 = jnp.bfloat16

D_MODEL = 1024
DEPTH = 4
D_HALF = D_MODEL // 2
SSM_GROUP = 16
N_SSM_GROUPS = D_HALF // SSM_GROUP
SSM_STATE = 64
POOL_WINDOWS = (2, 4, 8, 16)
SGU_HEADS = 4
CHUNK = 128
D_FF = 2816
CONV_WIDTH = 3
EPS = 1e-6
N_DEV = 8

ADAM_LR = 0.001
ADAM_B1 = 0.9
ADAM_B2 = 0.999
ADAM_EPS = 1e-08
ADAM_WD = 0.01
ADAM_STEP = 10

LANE = 128
SUBLANE = 8
S5_LANE_BLOCKS = D_HALF // LANE
S5_STATE_LANES = (N_SSM_GROUPS // S5_LANE_BLOCKS) * SSM_STATE
S5_TIME_CHUNK = 512
EXCHANGE_BLOCK_ELEMS = 1 << 20

GELU_K = math.sqrt(2.0 / math.pi)
GELU_C = 0.044715

_ARB = pltpu.CompilerParams(dimension_semantics=("arbitrary",))
_ARB2 = pltpu.CompilerParams(dimension_semantics=("arbitrary", "arbitrary"))
_PAR = pltpu.CompilerParams(dimension_semantics=("parallel",))
_PAR2 = pltpu.CompilerParams(dimension_semantics=("parallel", "parallel"))


def _pick_tile(n, cap, mult):
    if n <= cap:
        return n
    best = None
    for t in range(mult, cap + 1, mult):
        if n % t == 0:
            best = t
    assert best is not None, (n, cap, mult)
    return best


_MESH = pl.DeviceIdType.MESH
_ANY = pl.BlockSpec(memory_space=pl.ANY)
SEMS_PER_EXCHANGE = N_DEV - 1


class _Gather:
    into = None

    def __init__(self, src):
        self.src = src
        self.out_shape = jax.ShapeDtypeStruct((N_DEV,) + src.shape, src.dtype)

    def copies(self, x_ref, out_ref, send_sems, recv_sems, local_sem):
        x, y, cc = lax.axis_index("x"), lax.axis_index("y"), lax.axis_index("c")
        me, sibling = (x, y, cc), (x, y, 1 - cc)
        chips = [(1 - x, y), (x, 1 - y), (1 - x, 1 - y)]

        def rows(px, py, pc):
            return out_ref.at[4 * px + 2 * py + pc]

        def copy(k, block, to, src=None):
            return pltpu.make_async_remote_copy(
                src_ref=rows(*block) if src is None else src, dst_ref=rows(*block),
                send_sem=send_sems.at[k], recv_sem=recv_sems.at[k], device_id=to, device_id_type=_MESH)

        return dict(
            mine=pltpu.make_async_copy(x_ref, rows(*me), local_sem),
            first=[copy(0, me, sibling, src=x_ref)] + [copy(1 + k, me, (*chip, cc), src=x_ref)
                                                       for k, chip in enumerate(chips)],
            passed=[copy(4 + k, (*chip, cc), sibling) for k, chip in enumerate(chips)],
            over_ici=[copy(1 + k, (*chip, cc), me) for k, chip in enumerate(chips)],
            from_sibling=[copy(0, sibling, me)] + [copy(4 + k, (*chip, 1 - cc), me) for k, chip in enumerate(chips)])

    def start(self, *refs):
        cps = self.copies(*refs)
        cps["mine"].start()
        for cp in cps["first"]:
            cp.start()

    def finish(self, *refs):
        cps = self.copies(*refs)
        for arrived, onward in zip(cps["over_ici"], cps["passed"]):
            arrived.wait_recv()
            onward.start()
        for arrived in cps["from_sibling"]:
            arrived.wait_recv()
        for cp in cps["first"] + cps["passed"]:
            cp.wait_send()
        cps["mine"].wait()


class _Scatter:
    def __init__(self, src, into=None, row0=0):
        self.src, self.into, self.row0 = src, into, row0
        whole = src if into is None else into
        self.out_shape = jax.ShapeDtypeStruct(whole.shape, whole.dtype)

    def copies(self, p_ref, whole_ref, send_sems, recv_sems, local_sem):
        x, y, cc = lax.axis_index("x"), lax.axis_index("y"), lax.axis_index("c")
        me = 4 * x + 2 * y + cc
        rows = self.src.shape[1]
        out_ref = whole_ref if self.into is None else whole_ref.at[:, pl.ds(self.row0, rows)]
        sends, arrivals = [], []
        for k in range(1, N_DEV):
            px = (1 - x) if k & 4 else x
            py = (1 - y) if k & 2 else y
            pc = (1 - cc) if k & 1 else cc
            peer = 4 * px + 2 * py + pc
            kw = dict(send_sem=send_sems.at[k - 1], recv_sem=recv_sems.at[k - 1], device_id=(px, py, pc),
                      device_id_type=_MESH)
            sends.append(pltpu.make_async_remote_copy(src_ref=p_ref.at[peer], dst_ref=out_ref.at[me], **kw))
            arrivals.append(pltpu.make_async_remote_copy(src_ref=p_ref.at[me], dst_ref=out_ref.at[peer], **kw))
        return dict(mine=pltpu.make_async_copy(p_ref.at[me], out_ref.at[me], local_sem), sends=sends,
                    arrivals=arrivals)

    def start(self, *refs):
        cps = self.copies(*refs)
        cps["mine"].start()
        for cp in cps["sends"]:
            cp.start()

    def finish(self, *refs):
        cps = self.copies(*refs)
        for cp in cps["arrivals"]:
            cp.wait_recv()
        for cp in cps["sends"]:
            cp.wait_send()
        cps["mine"].wait()


class _SemView:
    def __init__(self, ref, lo):
        self.ref, self.lo = ref, lo

    @property
    def at(self):
        return self

    def __getitem__(self, k):
        return self.ref.at[self.lo + k]


def _call(body, *, name, grid, in_specs, out_specs, out_shape, args, scratch_shapes=(), parallel=True, exchanges=()):
    n_axes = len(grid)
    if not exchanges:
        sem = ("parallel" if parallel else "arbitrary",) * n_axes
        return pl.pallas_call(
            body, name=name, grid=grid, in_specs=in_specs, out_specs=out_specs, out_shape=out_shape,
            scratch_shapes=scratch_shapes, compiler_params=pltpu.CompilerParams(dimension_semantics=sem))(*args)
    single = not isinstance(out_shape, (list, tuple))
    out_specs = [out_specs] if single else list(out_specs)
    out_shape = [out_shape] if single else list(out_shape)
    n_in, n_out, n_scr, n_x = len(in_specs), len(out_specs), len(scratch_shapes), len(exchanges)
    landing = [(e, ex.into) for e, ex in enumerate(exchanges) if ex.into is not None]
    aliases = {n_in + n_x + pos: n_out + e for pos, (e, _) in enumerate(landing)}

    def wrapped(*refs):
        ins, refs = refs[:n_in], refs[n_in:]
        x_in, refs = refs[:n_x], refs[n_x + len(landing):]
        outs, refs = refs[:n_out], refs[n_out:]
        x_out, refs = refs[:n_x], refs[n_x:]
        scr, (send_sems, recv_sems, local_sems) = refs[:n_scr], refs[n_scr:]
        ids = [pl.program_id(k) for k in range(n_axes)]
        first = functools.reduce(jnp.logical_and, [i == 0 for i in ids])
        last = functools.reduce(jnp.logical_and, [i == g - 1 for i, g in zip(ids, grid)])

        def sems(e):
            lo = e * SEMS_PER_EXCHANGE
            return _SemView(send_sems, lo), _SemView(recv_sems, lo), local_sems.at[e]

        @pl.when(first)
        def _():
            for e, ex in enumerate(exchanges):
                ex.start(x_in[e], x_out[e], *sems(e))

        body(*ins, *outs, *scr)

        @pl.when(last)
        def _():
            for e, ex in enumerate(exchanges):
                ex.finish(x_in[e], x_out[e], *sems(e))

    res = pl.pallas_call(
        wrapped, name=name, grid=grid, in_specs=list(in_specs) + [_ANY] * (n_x + len(landing)),
        out_specs=out_specs + [_ANY] * n_x, out_shape=out_shape + [ex.out_shape for ex in exchanges],
        input_output_aliases=aliases,
        scratch_shapes=list(scratch_shapes) + [pltpu.SemaphoreType.DMA((n_x * SEMS_PER_EXCHANGE,)),
                                               pltpu.SemaphoreType.DMA((n_x * SEMS_PER_EXCHANGE,)),
                                               pltpu.SemaphoreType.DMA((n_x,))],
        compiler_params=pltpu.CompilerParams(dimension_semantics=("arbitrary",) * n_axes),
    )(*args, *[ex.src for ex in exchanges], *[buf for _, buf in landing])
    outs, x_outs = res[:n_out], res[n_out:]
    return (outs[0] if single else outs), x_outs


def _exchange_only(exchanges, *, name):
    def body():
        pass

    return _call(body, name=name, grid=(1,), in_specs=[], out_specs=[], out_shape=[], args=[],
                 exchanges=exchanges)[1]


def _shift_dn(x, d):
    rolled = pltpu.roll(x, d, 0)
    if x.shape[0] <= SUBLANE or d >= SUBLANE:
        row = lax.broadcasted_iota(jnp.int32, x.shape, 0)
        return jnp.where(row >= d, rolled, 0.0)
    row = lax.broadcasted_iota(jnp.int32, (SUBLANE, x.shape[1]), 0)
    return jnp.concatenate([jnp.where(row >= d, rolled[:SUBLANE], 0.0), rolled[SUBLANE:]], axis=0)


def _shift_up(x, d):
    n = x.shape[0]
    rolled = pltpu.roll(x, n - d, 0)
    if n <= SUBLANE or d >= SUBLANE:
        row = lax.broadcasted_iota(jnp.int32, x.shape, 0)
        return jnp.where(row < n - d, rolled, 0.0)
    row = lax.broadcasted_iota(jnp.int32, (SUBLANE, x.shape[1]), 0)
    return jnp.concatenate([rolled[:n - SUBLANE], jnp.where(row < SUBLANE - d, rolled[n - SUBLANE:], 0.0)], axis=0)


def _gelu(x):
    return 0.5 * x * (1.0 + jnp.tanh(GELU_K * (x + GELU_C * x * x * x)))


def _gelu_grad(x):
    t = jnp.tanh(GELU_K * (x + GELU_C * x * x * x))
    return 0.5 * (1.0 + t) + 0.5 * x * (1.0 - t * t) * (GELU_K * (1.0 + 3.0 * GELU_C * x * x))


def _sigmoid(x):
    return 1.0 / (1.0 + jnp.exp(-x))


def _conv3(x, w_ref):
    return w_ref[0:1, :] * _shift_dn(x, 2) + w_ref[1:2, :] * _shift_dn(x, 1) + w_ref[2:3, :] * x


def _conv3_bwd_x(dy, w_ref):
    return w_ref[2:3, :] * dy + w_ref[1:2, :] * _shift_up(dy, 1) + w_ref[0:1, :] * _shift_up(dy, 2)


def _conv3_bwd_w(dy, x):
    return jnp.concatenate([
        jnp.sum(dy * _shift_dn(x, 2), axis=0, keepdims=True),
        jnp.sum(dy * _shift_dn(x, 1), axis=0, keepdims=True),
        jnp.sum(dy * x, axis=0, keepdims=True)], axis=0)


def _dot(a, b, dims):
    return lax.dot_general(a.astype(BF16), b.astype(BF16), (dims, ((), ())), preferred_element_type=F32)


_NN = ((1,), (0,))
_NT = ((1,), (1,))
_TN = ((0,), (0,))


def _tiles(name, m, n, default):
    tm, tn = TILES.get(name, default)
    return math.gcd(tm, m), math.gcd(tn, n)


def _mm(a, b, mode, *, name, out_dtype=F32, add=None, norm=None, norm_bwd=None, tm_cap=512, tn_cap=1536,
        exchanges=()):
    halves = None
    if mode == "tn":
        r, m = a.shape
        n = b.shape[-1] * (2 if b.ndim == 3 else 1)
        tm, tn = _tiles(name, m, n, (_pick_tile(m, 256, LANE), _pick_tile(n, tn_cap, LANE)))
        if b.ndim == 3:
            per_half = b.shape[-1] // tn
            b_spec = pl.BlockSpec((None, r, tn), lambda i, j: (j // per_half, 0, j % per_half))
        else:
            b_spec = pl.BlockSpec((r, tn), lambda i, j: (0, j))
        in_specs = [pl.BlockSpec((r, tm), lambda i, j: (0, i)), b_spec]
        dims = _TN
    elif mode == "nt" and a.ndim == 3:
        _, m, halves = a.shape
        n = b.shape[0]
        tm, tn = _tiles(name, m, n, (_pick_tile(m, tm_cap, SUBLANE), _pick_tile(n, tn_cap, LANE)))
        in_specs = [pl.BlockSpec((2, tm, halves), lambda i, j: (0, i, 0)),
                    pl.BlockSpec((tn, 2 * halves), lambda i, j: (j, 0))]
        dims = _NT
    elif mode == "nn":
        m, k = a.shape
        n = b.shape[1]
        tm, tn = _tiles(name, m, n, (_pick_tile(m, tm_cap, SUBLANE), _pick_tile(n, tn_cap, LANE)))
        in_specs = [pl.BlockSpec((tm, k), lambda i, j: (i, 0)), pl.BlockSpec((k, tn), lambda i, j: (0, j))]
        dims = _NN
    else:
        m, k = a.shape
        n = b.shape[0]
        tm, tn = _tiles(name, m, n, (_pick_tile(m, tm_cap, SUBLANE), _pick_tile(n, tn_cap, LANE)))
        in_specs = [pl.BlockSpec((tm, k), lambda i, j: (i, 0)), pl.BlockSpec((tn, k), lambda i, j: (j, 0))]
        dims = _NT
    assert m % tm == 0 and n % tn == 0, (name, m, n, tm, tn)
    args = [a, b]
    tile = pl.BlockSpec((tm, tn), lambda i, j: (i, j))
    if add is not None:
        in_specs.append(tile)
        args.append(add)
    out_specs, out_shape = tile, jax.ShapeDtypeStruct((m, n), out_dtype)
    if norm is not None:
        gains, layer = norm
        assert tn == n
        in_specs.append(pl.BlockSpec((None, 1, n), lambda i, j: (layer, 0, 0)))
        args.append(gains.reshape(gains.shape[0], 1, n))
        out_specs, out_shape = [tile, tile], [out_shape, jax.ShapeDtypeStruct((m, n), BF16)]
    if norm_bwd is not None:
        x_in, gains, layer, res = norm_bwd
        assert tn == n and add is None and norm is None
        vec = pl.BlockSpec((None, 1, n), lambda i, j: (layer, 0, 0))
        in_specs += [tile, vec, tile]
        args += [x_in, gains.reshape(gains.shape[0], 1, n), res]
        out_specs = [tile, tile, pl.BlockSpec((1, n), lambda i, j: (0, 0))]
        out_shape = [jax.ShapeDtypeStruct((m, n), F32), jax.ShapeDtypeStruct((m, n), BF16),
                     jax.ShapeDtypeStruct((1, n), F32)]

    def body(*refs):
        if halves is None:
            acc = _dot(refs[0][...], refs[1][...], dims)
        else:
            acc = (_dot(refs[0][0], refs[1][:, :halves], dims) + _dot(refs[0][1], refs[1][:, halves:], dims))
        if add is not None:
            acc = acc + refs[2][...]
        if norm_bwd is not None:
            x_ref, g_ref, res_ref, dx_ref, dxb_ref, dg_ref = refs[2:]

            @pl.when(pl.program_id(0) == 0)
            def _():
                dg_ref[...] = jnp.zeros_like(dg_ref)
            dx, xn = _rms_bwd_rows(acc, x_ref[...], g_ref[...])
            dx = dx + res_ref[...]
            dx_ref[...] = dx
            dxb_ref[...] = dx.astype(BF16)
            dg_ref[...] += jnp.sum(acc * xn, axis=0, keepdims=True)
        elif norm is None:
            refs[-1][...] = acc.astype(out_dtype)
        else:
            refs[-2][...] = acc.astype(out_dtype)
            r = lax.rsqrt(jnp.mean(acc * acc, axis=-1, keepdims=True) + EPS)
            refs[-1][...] = (acc * r * refs[-3][...]).astype(BF16)

    return _call(body, name=name, grid=(m // tm, n // tn), in_specs=in_specs, out_specs=out_specs,
                 out_shape=out_shape, args=args, parallel=norm_bwd is None, exchanges=exchanges)


def _rms_fwd(x, g, *, name):
    l, d = x.shape
    tl = _pick_tile(l, 512, SUBLANE)

    def body(x_ref, g_ref, h_ref):
        xv = x_ref[...]
        r = lax.rsqrt(jnp.mean(xv * xv, axis=-1, keepdims=True) + EPS)
        h_ref[...] = (xv * r * g_ref[...]).astype(BF16)

    return pl.pallas_call(
        body, name=name, grid=(l // tl,),
        in_specs=[pl.BlockSpec((tl, d), lambda i: (i, 0)), pl.BlockSpec((1, d), lambda i: (0, 0))],
        out_specs=pl.BlockSpec((tl, d), lambda i: (i, 0)),
        out_shape=jax.ShapeDtypeStruct((l, d), BF16), compiler_params=_PAR)(x, g)


def _rms_bwd_rows(dh, xv, g):
    r = lax.rsqrt(jnp.mean(xv * xv, axis=-1, keepdims=True) + EPS)
    a = dh * g
    m = jnp.mean(a * xv, axis=-1, keepdims=True)
    return r * a - xv * (r * r * r) * m, xv * r


def _rms_bwd(dh, x, g, res, *, name):
    l, d = x.shape
    tl = _pick_tile(l, 512, SUBLANE)

    def body(dh_ref, x_ref, g_ref, res_ref, dx_ref, dxb_ref, dg_ref):
        @pl.when(pl.program_id(0) == 0)
        def _():
            dg_ref[...] = jnp.zeros_like(dg_ref)
        dhv = dh_ref[...]
        dx, xn = _rms_bwd_rows(dhv, x_ref[...], g_ref[...])
        dx = dx + res_ref[...]
        dx_ref[...] = dx
        dxb_ref[...] = dx.astype(BF16)
        dg_ref[...] += jnp.sum(dhv * xn, axis=0, keepdims=True)

    row = pl.BlockSpec((tl, d), lambda i: (i, 0))
    vec = pl.BlockSpec((1, d), lambda i: (0, 0))
    return pl.pallas_call(
        body, name=name, grid=(l // tl,), in_specs=[row, row, vec, row], out_specs=[row, row, vec],
        out_shape=[jax.ShapeDtypeStruct((l, d), F32), jax.ShapeDtypeStruct((l, d), BF16),
                   jax.ShapeDtypeStruct((1, d), F32)],
        compiler_params=_ARB)(dh, x, g, res)


def _loss_head(x, g, tgt, *, name):
    l, d = x.shape
    tl = _pick_tile(l, 512, SUBLANE)

    def body(x_ref, g_ref, t_ref, loss_ref, dx_ref, dxb_ref, dg_ref):
        @pl.when(pl.program_id(0) == 0)
        def _():
            dg_ref[...] = jnp.zeros_like(dg_ref)
            loss_ref[...] = jnp.zeros_like(loss_ref)
        xv = x_ref[...]
        gv = g_ref[...]
        r = lax.rsqrt(jnp.mean(xv * xv, axis=-1, keepdims=True) + EPS)
        err = xv * r * gv - t_ref[...]
        row_loss = jnp.sum(err * err, axis=-1, keepdims=True) * (0.5 / d)
        loss_ref[...] += jnp.sum(row_loss, axis=0, keepdims=True)
        dy = err * (1.0 / d)
        dx, xn = _rms_bwd_rows(dy, xv, gv)
        dx_ref[...] = dx
        dxb_ref[...] = dx.astype(BF16)
        dg_ref[...] += jnp.sum(dy * xn, axis=0, keepdims=True)

    row = pl.BlockSpec((tl, d), lambda i: (i, 0))
    vec = pl.BlockSpec((1, d), lambda i: (0, 0))
    one = pl.BlockSpec((1, 1), lambda i: (0, 0))
    return pl.pallas_call(
        body, name=name, grid=(l // tl,), in_specs=[row, vec, row], out_specs=[one, row, row, vec],
        out_shape=[jax.ShapeDtypeStruct((1, 1), F32), jax.ShapeDtypeStruct((l, d), F32),
                   jax.ShapeDtypeStruct((l, d), BF16), jax.ShapeDtypeStruct((1, d), F32)],
        compiler_params=_ARB)(x, g, tgt)


def _ffn_act(up, cw, cb, *, name, exchanges=()):
    l = up.shape[0]
    nb = D_FF // LANE

    def body(ug_ref, uv_ref, wg_ref, wv_ref, bg_ref, bv_ref, o_ref):
        gc = _conv3(ug_ref[...], wg_ref) + bg_ref[...]
        vc = _conv3(uv_ref[...], wv_ref) + bv_ref[...]
        o_ref[...] = (gc * _sigmoid(gc) * vc).astype(BF16)

    col = lambda off: pl.BlockSpec((l, LANE), lambda j: (0, j + off))
    w3 = lambda off: pl.BlockSpec((CONV_WIDTH, LANE), lambda j: (0, j + off))
    b1 = lambda off: pl.BlockSpec((1, LANE), lambda j: (0, j + off))
    return _call(body, name=name, grid=(nb,), in_specs=[col(0), col(nb), w3(0), w3(nb), b1(0), b1(nb)],
                 out_specs=col(0), out_shape=jax.ShapeDtypeStruct((l, D_FF), BF16),
                 args=[up, up, cw, cw, cb, cb], exchanges=exchanges)


def _ffn_act_bwd(up, dact, cw, cb, *, name, exchanges=()):
    l = up.shape[0]
    nb = D_FF // LANE

    def half_bwd(k, dc, x, w_ref, dup_ref, dcw_ref, dcb_ref):
        dcb_ref[k] = jnp.sum(dc, axis=0, keepdims=True)
        dcw_ref[k] = _conv3_bwd_w(dc, x)
        dup_ref[k] = _conv3_bwd_x(dc, w_ref).astype(BF16)

    def body(ug_ref, uv_ref, da_ref, wg_ref, wv_ref, bg_ref, bv_ref, dup_ref, dcw_ref, dcb_ref):
        ug, uv, da = ug_ref[...], uv_ref[...], da_ref[...]
        gc = _conv3(ug, wg_ref) + bg_ref[...]
        vc = _conv3(uv, wv_ref) + bv_ref[...]
        sg = _sigmoid(gc)
        half_bwd(0, da * vc * (sg * (1.0 + gc * (1.0 - sg))), ug, wg_ref, dup_ref, dcw_ref, dcb_ref)
        half_bwd(1, da * (gc * sg), uv, wv_ref, dup_ref, dcw_ref, dcb_ref)

    col = lambda off: pl.BlockSpec((l, LANE), lambda j: (0, j + off))
    w3 = lambda off: pl.BlockSpec((CONV_WIDTH, LANE), lambda j: (0, j + off))
    b1 = lambda off: pl.BlockSpec((1, LANE), lambda j: (0, j + off))
    both = lambda rows: pl.BlockSpec((2, rows, LANE), lambda j: (0, 0, j))
    res = _call(
        body, name=name, grid=(nb,),
        in_specs=[col(0), col(nb), col(0), w3(0), w3(nb), b1(0), b1(nb)],
        out_specs=[both(l), both(CONV_WIDTH), both(1)],
        out_shape=[jax.ShapeDtypeStruct((2, l, D_FF), BF16), jax.ShapeDtypeStruct((2, CONV_WIDTH, D_FF), F32),
                   jax.ShapeDtypeStruct((2, 1, D_FF), F32)],
        args=[up, up, dact, cw, cw, cb, cb], exchanges=exchanges)
    (dup, dcw, dcb), moved = res if exchanges else (res, None)
    outs = [dup, jnp.concatenate([dcw[0], dcw[1]], axis=1), jnp.concatenate([dcb[0], dcb[1]], axis=1)]
    return (outs, moved) if exchanges else outs


def _sconv_fwd(proj, cw, *, name):
    l = proj.shape[0]
    nb = D_HALF // LANE

    def body(xa_ref, ba_ref, ca_ref, w_ref, o_ref):
        o_ref[...] = (ba_ref[...] * _conv3(ca_ref[...] * xa_ref[...], w_ref)).astype(BF16)

    col = lambda off: pl.BlockSpec((l, LANE), lambda j: (0, j + off))
    return pl.pallas_call(
        body, name=name, grid=(nb,),
        in_specs=[col(0), col(nb), col(2 * nb), pl.BlockSpec((CONV_WIDTH, LANE), lambda j: (0, j))],
        out_specs=col(0), out_shape=jax.ShapeDtypeStruct((l, D_HALF), BF16),
        compiler_params=_PAR)(proj, proj, proj, cw)


def _sconv_bwd(proj, dcat, cw, *, name):
    l = proj.shape[0]
    nb = D_HALF // LANE

    def body(xa_ref, ba_ref, ca_ref, dy_ref, w_ref, dxa_ref, dba_ref, dca_ref, dw_ref):
        xa, ba, ca, dy = xa_ref[...], ba_ref[...], ca_ref[...], dy_ref[...]
        q = ca * xa
        dba_ref[...] = (dy * _conv3(q, w_ref)).astype(BF16)
        dconv = dy * ba
        dw_ref[...] = _conv3_bwd_w(dconv, q)
        dq = _conv3_bwd_x(dconv, w_ref)
        dxa_ref[...] = (dq * ca).astype(BF16)
        dca_ref[...] = (dq * xa).astype(BF16)

    col = lambda off: pl.BlockSpec((l, LANE), lambda j: (0, j + off))
    w3 = pl.BlockSpec((CONV_WIDTH, LANE), lambda j: (0, j))
    piece = jax.ShapeDtypeStruct((l, D_HALF), BF16)
    return pl.pallas_call(
        body, name=name, grid=(nb,),
        in_specs=[col(0), col(nb), col(2 * nb), col(0), w3],
        out_specs=[col(0), col(0), col(0), w3],
        out_shape=[piece, piece, piece, jax.ShapeDtypeStruct((CONV_WIDTH, D_HALF), F32)],
        compiler_params=_PAR)(proj, proj, proj, dcat, cw)


def _s5_prep(log_step, a_re, a_im, b_re, b_im):
    step = jnp.exp(log_step)[:, None]
    mag = jnp.exp(a_re * step)
    lr = mag * jnp.cos(a_im * step)
    li = mag * jnp.sin(a_im * step)
    nr = lr - 1.0
    den = a_re * a_re + a_im * a_im
    qr = (nr * a_re + li * a_im) / den
    qi = (li * a_re - nr * a_im) / den
    br = qr[..., None] * b_re - qi[..., None] * b_im
    bi = qr[..., None] * b_im + qi[..., None] * b_re
    return lr, li, br, bi


def _block_diag(m):
    nb, ng, r, c = m.shape
    eye = jnp.eye(ng, dtype=m.dtype)
    return jnp.einsum("bgrc,gh->bgrhc", m, eye).reshape(nb, ng * r, ng * c)


def _block_diag_extract(w, r, c):
    nb = w.shape[0]
    ng = w.shape[1] // r
    w5 = w.reshape(nb, ng, r, ng, c)
    return jnp.einsum("bgrhc,gh->bgrc", w5, jnp.eye(ng, dtype=w.dtype))


def _s5_mats(br, bi, c_re, c_im):
    g8 = N_SSM_GROUPS // S5_LANE_BLOCKS
    to_blk = lambda m: m.reshape(S5_LANE_BLOCKS, g8, m.shape[1], m.shape[2])
    wb = jnp.concatenate([_block_diag(to_blk(jnp.swapaxes(br, 1, 2))),
                          _block_diag(to_blk(jnp.swapaxes(bi, 1, 2)))], axis=2)
    wc = jnp.concatenate([_block_diag(to_blk(jnp.swapaxes(c_re, 1, 2))),
                          _block_diag(to_blk(jnp.swapaxes(-c_im, 1, 2)))], axis=1)
    return wb, wc


def _s5_mats_bwd(dwb, dwc):
    g, p, h = N_SSM_GROUPS, SSM_STATE, SSM_GROUP
    half = S5_STATE_LANES
    dbr = jnp.swapaxes(_block_diag_extract(dwb[:, :, :half], h, p).reshape(g, h, p), 1, 2)
    dbi = jnp.swapaxes(_block_diag_extract(dwb[:, :, half:], h, p).reshape(g, h, p), 1, 2)
    dcr = jnp.swapaxes(_block_diag_extract(dwc[:, :half, :], p, h).reshape(g, p, h), 1, 2)
    dci = -jnp.swapaxes(_block_diag_extract(dwc[:, half:, :], p, h).reshape(g, p, h), 1, 2)
    return dbr, dbi, dcr, dci


def _s5_scan_consts(log_step, a_re, a_im, reverse):
    step = jnp.exp(log_step)[:, None]
    xr = (a_re * step).reshape(S5_LANE_BLOCKS, 1, S5_STATE_LANES)
    xi = (a_im * step).reshape(S5_LANE_BLOCKS, 1, S5_STATE_LANES)
    if reverse:
        xi = -xi
    row = jnp.arange(SUBLANE, dtype=F32).reshape(1, SUBLANE, 1)

    def power(n):
        mag = jnp.exp(n * xr)
        return jnp.concatenate([mag * jnp.cos(n * xi), mag * jnp.sin(n * xi)], axis=-1)

    kinds = []
    for d in (1, 2, 4):
        keep = (row <= SUBLANE - 1 - d) if reverse else (row >= d)
        kinds.append(jnp.where(keep, power(jnp.full_like(row, float(d))), 0.0))
    kinds.append(power((SUBLANE - row) if reverse else (row + 1.0)))
    return jnp.stack(kinds, axis=1)


def _scan_rows(s_ref, sc_ref, carry_ref, n_rows, reverse):
    n_grp = n_rows // SUBLANE
    n_col = S5_STATE_LANES // LANE
    half = S5_STATE_LANES

    def step(i, carry):
        grp = (n_grp - 1 - i) if reverse else i
        r0 = pl.multiple_of(grp * SUBLANE, SUBLANE)
        out = []
        for cb in range(n_col):
            lo, hi = cb * LANE, half + cb * LANE
            re = s_ref[pl.ds(r0, SUBLANE), lo:lo + LANE]
            im = s_ref[pl.ds(r0, SUBLANE), hi:hi + LANE]
            for k, d in enumerate((1, 2, 4)):
                sh = (SUBLANE - d) if reverse else d
                rr, ri = pltpu.roll(re, sh, 0), pltpu.roll(im, sh, 0)
                ar, ai = sc_ref[k, :, lo:lo + LANE], sc_ref[k, :, hi:hi + LANE]
                re, im = re + (ar * rr - ai * ri), im + (ar * ri + ai * rr)
            pr, pi = sc_ref[3, :, lo:lo + LANE], sc_ref[3, :, hi:hi + LANE]
            cr, ci = carry[2 * cb], carry[2 * cb + 1]
            re, im = re + (pr * cr - pi * ci), im + (pr * ci + pi * cr)
            s_ref[pl.ds(r0, SUBLANE), lo:lo + LANE] = re
            s_ref[pl.ds(r0, SUBLANE), hi:hi + LANE] = im
            edge = 0 if reverse else SUBLANE - 1
            out.append(jnp.broadcast_to(re[edge:edge + 1, :], (SUBLANE, LANE)))
            out.append(jnp.broadcast_to(im[edge:edge + 1, :], (SUBLANE, LANE)))
        return tuple(out)

    init = []
    for cb in range(n_col):
        init.append(carry_ref[:, cb * LANE:(cb + 1) * LANE])
        init.append(carry_ref[:, half + cb * LANE:half + (cb + 1) * LANE])
    fin = lax.fori_loop(0, n_grp, step, tuple(init))
    for cb in range(n_col):
        carry_ref[:, cb * LANE:(cb + 1) * LANE] = fin[2 * cb]
        carry_ref[:, half + cb * LANE:half + (cb + 1) * LANE] = fin[2 * cb + 1]


def _s5_fwd(proj, wb, wc, d_skip, sc, *, name, exchanges=()):
    l = proj.shape[0]
    tt = _pick_tile(l, S5_TIME_CHUNK, SUBLANE)
    u_off = (proj.shape[1] - D_HALF) // LANE
    w2 = 2 * S5_STATE_LANES

    def body(u_ref, wb_ref, wc_ref, d_ref, sc_ref, s_ref, y_ref, carry_ref):
        @pl.when(pl.program_id(1) == 0)
        def _():
            carry_ref[...] = jnp.zeros_like(carry_ref)
        u = u_ref[...]
        s_ref[...] = _dot(u, wb_ref[0], _NN)
        _scan_rows(s_ref, sc_ref.at[0], carry_ref, tt, False)
        y_ref[...] = _dot(s_ref[...], wc_ref[0], _NN) + d_ref[...] * u

    return _call(
        body, name=name, grid=(S5_LANE_BLOCKS, l // tt),
        in_specs=[pl.BlockSpec((tt, LANE), lambda b, t: (t, b + u_off)),
                  pl.BlockSpec((1, LANE, w2), lambda b, t: (b, 0, 0)),
                  pl.BlockSpec((1, w2, LANE), lambda b, t: (b, 0, 0)),
                  pl.BlockSpec((1, LANE), lambda b, t: (0, b)),
                  pl.BlockSpec((1, 4, SUBLANE, w2), lambda b, t: (b, 0, 0, 0))],
        out_specs=[pl.BlockSpec((tt, w2), lambda b, t: (t, b)), pl.BlockSpec((tt, LANE), lambda b, t: (t, b))],
        out_shape=[jax.ShapeDtypeStruct((l, S5_LANE_BLOCKS * w2), F32), jax.ShapeDtypeStruct((l, D_HALF), F32)],
        scratch_shapes=[pltpu.VMEM((SUBLANE, w2), F32)],
        args=[proj, wb, wc, d_skip, sc], parallel=False, exchanges=exchanges)


def _s5_bwd(proj, dy, states, wb, wc, d_skip, sc_rev, *, name, exchanges=()):
    l = proj.shape[0]
    tt = _pick_tile(l, S5_TIME_CHUNK, SUBLANE)
    nt = l // tt
    u_off = (proj.shape[1] - D_HALF) // LANE
    w2 = 2 * S5_STATE_LANES
    half = S5_STATE_LANES
    grp_per_chunk = tt // SUBLANE

    def body(u_ref, dy_ref, s_ref, halo_ref, wb_ref, wc_ref, d_ref, sc_ref,
             du_ref, dwb_ref, dwc_ref, dlam_ref, dd_ref, g_scr, carry_ref):
        t = pl.program_id(1)

        @pl.when(t == 0)
        def _():
            carry_ref[...] = jnp.zeros_like(carry_ref)
            dwb_ref[...] = jnp.zeros_like(dwb_ref)
            dwc_ref[...] = jnp.zeros_like(dwc_ref)
            dlam_ref[...] = jnp.zeros_like(dlam_ref)
            dd_ref[...] = jnp.zeros_like(dd_ref)

        u = u_ref[...]
        dyv = dy_ref[...]
        g_scr[...] = _dot(dyv, wc_ref[0], _NT)
        _scan_rows(g_scr, sc_ref.at[0], carry_ref, tt, True)
        gv = g_scr[...]
        du_ref[...] = (_dot(gv, wb_ref[0], _NT) + d_ref[...] * dyv).astype(BF16)
        dwb_ref[0] += _dot(u, gv, _TN)
        sv = s_ref[...]
        dwc_ref[0] += _dot(sv, dyv, _TN)
        dd_ref[...] += jnp.sum(dyv * u, axis=0, keepdims=True)
        first_chunk = t == nt - 1
        halo = jnp.where(first_chunk, 0.0, halo_ref[SUBLANE - 1:SUBLANE, :])
        row = lax.broadcasted_iota(jnp.int32, sv.shape, 0)
        sp = jnp.where(row == 0, jnp.broadcast_to(halo, sv.shape), pltpu.roll(sv, 1, 0))
        gr, gi = gv[:, :half], gv[:, half:]
        sr, si = sp[:, :half], sp[:, half:]
        dlr = jnp.sum(gr * sr + gi * si, axis=0, keepdims=True)
        dli = jnp.sum(gi * sr - gr * si, axis=0, keepdims=True)
        dlam_ref[0] += jnp.concatenate([dlr, dli], axis=1)

    rev = lambda t: nt - 1 - t
    return _call(
        body, name=name, grid=(S5_LANE_BLOCKS, nt),
        in_specs=[pl.BlockSpec((tt, LANE), lambda b, t: (rev(t), b + u_off)),
                  pl.BlockSpec((tt, LANE), lambda b, t: (rev(t), b)),
                  pl.BlockSpec((tt, w2), lambda b, t: (rev(t), b)),
                  pl.BlockSpec((SUBLANE, w2), lambda b, t: (jnp.maximum(rev(t) * grp_per_chunk - 1, 0), b)),
                  pl.BlockSpec((1, LANE, w2), lambda b, t: (b, 0, 0)),
                  pl.BlockSpec((1, w2, LANE), lambda b, t: (b, 0, 0)),
                  pl.BlockSpec((1, LANE), lambda b, t: (0, b)),
                  pl.BlockSpec((1, 4, SUBLANE, w2), lambda b, t: (b, 0, 0, 0))],
        out_specs=[pl.BlockSpec((tt, LANE), lambda b, t: (rev(t), b)),
                   pl.BlockSpec((1, LANE, w2), lambda b, t: (b, 0, 0)),
                   pl.BlockSpec((1, w2, LANE), lambda b, t: (b, 0, 0)),
                   pl.BlockSpec((1, 1, w2), lambda b, t: (b, 0, 0)),
                   pl.BlockSpec((1, LANE), lambda b, t: (0, b))],
        out_shape=[jax.ShapeDtypeStruct((l, D_HALF), BF16),
                   jax.ShapeDtypeStruct((S5_LANE_BLOCKS, LANE, w2), F32),
                   jax.ShapeDtypeStruct((S5_LANE_BLOCKS, w2, LANE), F32),
                   jax.ShapeDtypeStruct((S5_LANE_BLOCKS, 1, w2), F32),
                   jax.ShapeDtypeStruct((1, D_HALF), F32)],
        scratch_shapes=[pltpu.VMEM((tt, w2), F32), pltpu.VMEM((SUBLANE, w2), F32)],
        args=[proj, dy, states, states, wb, wc, d_skip, sc_rev], parallel=False, exchanges=exchanges)


def _glu_fwd(ypre, wg, bg, *, name):
    l, d = ypre.shape
    tl = _pick_tile(l, 512, SUBLANE)

    def body(y_ref, w_ref, b_ref, o_ref):
        yg = _gelu(y_ref[...])
        o_ref[...] = (yg * _sigmoid(_dot(yg, w_ref[...], _NN) + b_ref[...])).astype(BF16)

    row = pl.BlockSpec((tl, d), lambda i: (i, 0))
    return pl.pallas_call(
        body, name=name, grid=(l // tl,),
        in_specs=[row, pl.BlockSpec((d, d), lambda i: (0, 0)), pl.BlockSpec((1, d), lambda i: (0, 0))],
        out_specs=row, out_shape=jax.ShapeDtypeStruct((l, d), BF16), compiler_params=_PAR)(ypre, wg, bg)


def _glu_bwd(dcat, ypre, wg, bg, *, name):
    l, d = ypre.shape
    tl = _pick_tile(l, 512, SUBLANE)

    def body(dy_ref, y_ref, w_ref, b_ref, dyp_ref, dw_ref, db_ref):
        @pl.when(pl.program_id(0) == 0)
        def _():
            dw_ref[...] = jnp.zeros_like(dw_ref)
            db_ref[...] = jnp.zeros_like(db_ref)
        yp = y_ref[...]
        dyb = dy_ref[...]
        yg = _gelu(yp)
        sg = _sigmoid(_dot(yg, w_ref[...], _NN) + b_ref[...])
        dz = dyb * yg * sg * (1.0 - sg)
        dyg = dyb * sg + _dot(dz, w_ref[...], _NT)
        dyp_ref[...] = dyg * _gelu_grad(yp)
        dw_ref[...] += _dot(yg, dz, _TN)
        db_ref[...] += jnp.sum(dz, axis=0, keepdims=True)

    row = pl.BlockSpec((tl, d), lambda i: (i, 0))
    mat = pl.BlockSpec((d, d), lambda i: (0, 0))
    vec = pl.BlockSpec((1, d), lambda i: (0, 0))
    return pl.pallas_call(
        body, name=name, grid=(l // tl,),
        in_specs=[pl.BlockSpec((tl, d), lambda i: (i, 1)), row, mat, vec], out_specs=[row, mat, vec],
        out_shape=[jax.ShapeDtypeStruct((l, d), F32), jax.ShapeDtypeStruct((d, d), F32),
                   jax.ShapeDtypeStruct((1, d), F32)],
        compiler_params=_ARB)(dcat, ypre, wg, bg)


def _window_sum(x, w, trailing):
    s, d = x, 1
    while d < w:
        s = s + (_shift_dn(s, d) if trailing else _shift_up(s, d))
        d *= 2
    return s


def _window_count(shape, w):
    row = lax.broadcasted_iota(jnp.int32, shape, 0)
    return jnp.minimum(row + 1, w).astype(F32)


def _pool_fwd(proj, pw, scale, *, name):
    l = proj.shape[0]
    ng = len(POOL_WINDOWS)

    def body(z_ref, w_ref, sc_ref, o_ref):
        z = z_ref[...]
        for k, w in enumerate(POOL_WINDOWS):
            @pl.when(pl.program_id(0) == k)
            def _():
                pooled = _window_sum(z, w, True) / _window_count(z.shape, w) - z
                o_ref[...] = (_dot(pooled, w_ref[0], _NN) * sc_ref[...]).astype(BF16)

    col = pl.BlockSpec((l, LANE), lambda g: (0, g))
    return pl.pallas_call(
        body, name=name, grid=(ng,),
        in_specs=[col, pl.BlockSpec((1, LANE, LANE), lambda g: (g, 0, 0)), pl.BlockSpec((1, LANE), lambda g: (0, g))],
        out_specs=col, out_shape=jax.ShapeDtypeStruct((l, D_HALF), BF16), compiler_params=_PAR)(proj, pw, scale)


def _pool_bwd(proj, dcat, pw, scale, *, name):
    l = proj.shape[0]
    ng = len(POOL_WINDOWS)

    def body(z_ref, dy_ref, w_ref, sc_ref, dz_ref, dw_ref, dsc_ref):
        z = z_ref[...]
        dy = dy_ref[...]
        for k, w in enumerate(POOL_WINDOWS):
            @pl.when(pl.program_id(0) == k)
            def _():
                cnt = _window_count(z.shape, w)
                pooled = _window_sum(z, w, True) / cnt - z
                ypre = _dot(pooled, w_ref[0], _NN)
                dsc_ref[...] = jnp.sum(dy * ypre, axis=0, keepdims=True)
                dyp = dy * sc_ref[...]
                dw_ref[0] = _dot(pooled, dyp, _TN)
                dpool = _dot(dyp, w_ref[0], _NT)
                dz_ref[...] = (_window_sum(dpool / cnt, w, False) - dpool).astype(BF16)

    col = pl.BlockSpec((l, LANE), lambda g: (0, g))
    mat = pl.BlockSpec((1, LANE, LANE), lambda g: (g, 0, 0))
    vec = pl.BlockSpec((1, LANE), lambda g: (0, g))
    return pl.pallas_call(
        body, name=name, grid=(ng,), in_specs=[col, col, mat, vec], out_specs=[col, mat, vec],
        out_shape=[jax.ShapeDtypeStruct((l, D_HALF), BF16), jax.ShapeDtypeStruct((ng, LANE, LANE), F32),
                   jax.ShapeDtypeStruct((1, D_HALF), F32)],
        compiler_params=_PAR)(proj, dcat, pw, scale)


def _tril_mask():
    r = lax.broadcasted_iota(jnp.int32, (CHUNK, CHUNK), 0)
    c = lax.broadcasted_iota(jnp.int32, (CHUNK, CHUNK), 1)
    return r >= c


def _sgu_fwd(proj, ng, sw, sb_t, *, name):
    l = proj.shape[0]
    tl = _pick_tile(l, 512, CHUNK)

    def body(su_ref, sv_ref, g_ref, w_ref, b_ref, o_ref):
        su = _gelu(su_ref[...])
        sv = _gelu(sv_ref[...])
        r = lax.rsqrt(jnp.mean(sv * sv, axis=-1, keepdims=True) + EPS)
        v = sv * r * g_ref[...]
        mask = _tril_mask()
        for h in range(SGU_HEADS):
            wm = jnp.where(mask, w_ref[h], 0.0)
            cs = slice(h * LANE, (h + 1) * LANE)
            for n in range(tl // CHUNK):
                rs = slice(n * CHUNK, (n + 1) * CHUNK)
                mixed = _dot(wm, v[rs, cs], _NN) + b_ref[:, h:h + 1]
                o_ref[rs, cs] = (su[rs, cs] * mixed).astype(BF16)

    blk = lambda c: pl.BlockSpec((tl, D_HALF), lambda i: (i, c))
    return pl.pallas_call(
        body, name=name, grid=(l // tl,),
        in_specs=[blk(1), blk(2), pl.BlockSpec((1, D_HALF), lambda i: (0, 0)),
                  pl.BlockSpec((SGU_HEADS, CHUNK, CHUNK), lambda i: (0, 0, 0)),
                  pl.BlockSpec((CHUNK, SGU_HEADS), lambda i: (0, 0))],
        out_specs=blk(0), out_shape=jax.ShapeDtypeStruct((l, D_HALF), BF16),
        compiler_params=_PAR)(proj, proj, ng, sw, sb_t)


def _sgu_bwd(proj, dcat, ng, sw, sb_t, *, name):
    l = proj.shape[0]
    tl = _pick_tile(l, 512, CHUNK)

    def body(su_ref, sv_ref, dy_ref, g_ref, w_ref, b_ref, dsu_ref, dsv_ref, dw_ref, dbm_ref, dng_ref, dv_scr):
        @pl.when(pl.program_id(0) == 0)
        def _():
            dw_ref[...] = jnp.zeros_like(dw_ref)
            dbm_ref[...] = jnp.zeros_like(dbm_ref)
            dng_ref[...] = jnp.zeros_like(dng_ref)
        su_pre = su_ref[...]
        sv_pre = sv_ref[...]
        su = _gelu(su_pre)
        sv = _gelu(sv_pre)
        gsu = _gelu_grad(su_pre)
        gv = g_ref[...]
        r = lax.rsqrt(jnp.mean(sv * sv, axis=-1, keepdims=True) + EPS)
        v = sv * r * gv
        dy = dy_ref[...]
        mask = _tril_mask()
        for h in range(SGU_HEADS):
            wm = jnp.where(mask, w_ref[h], 0.0)
            cs = slice(h * LANE, (h + 1) * LANE)
            dw_acc = jnp.zeros((CHUNK, CHUNK), F32)
            db_acc = jnp.zeros((CHUNK, LANE), F32)
            for n in range(tl // CHUNK):
                rs = slice(n * CHUNK, (n + 1) * CHUNK)
                vb = v[rs, cs]
                mixed = _dot(wm, vb, _NN) + b_ref[:, h:h + 1]
                dyb = dy[rs, cs]
                dsu_ref[rs, cs] = (dyb * mixed * gsu[rs, cs]).astype(BF16)
                dmix = dyb * su[rs, cs]
                db_acc = db_acc + dmix
                dw_acc = dw_acc + _dot(dmix, vb, _NT)
                dv_scr[rs, cs] = _dot(wm, dmix, _TN)
            dw_ref[h] += jnp.where(mask, dw_acc, 0.0)
            dbm_ref[h] += db_acc
        dv = dv_scr[...]
        a = dv * gv
        m = jnp.mean(a * sv, axis=-1, keepdims=True)
        dsv = r * a - sv * (r * r * r) * m
        dng_ref[...] += jnp.sum(dv * sv * r, axis=0, keepdims=True)
        dsv_ref[...] = (dsv * _gelu_grad(sv_pre)).astype(BF16)

    blk = lambda c: pl.BlockSpec((tl, D_HALF), lambda i: (i, c))
    mats = pl.BlockSpec((SGU_HEADS, CHUNK, CHUNK), lambda i: (0, 0, 0))
    vec = pl.BlockSpec((1, D_HALF), lambda i: (0, 0))
    piece = jax.ShapeDtypeStruct((l, D_HALF), BF16)
    mshape = jax.ShapeDtypeStruct((SGU_HEADS, CHUNK, CHUNK), F32)
    return pl.pallas_call(
        body, name=name, grid=(l // tl,),
        in_specs=[blk(1), blk(2), blk(1), vec, mats, pl.BlockSpec((CHUNK, SGU_HEADS), lambda i: (0, 0))],
        out_specs=[blk(0), blk(0), mats, mats, vec],
        out_shape=[piece, piece, mshape, mshape, jax.ShapeDtypeStruct((1, D_HALF), F32)],
        scratch_shapes=[pltpu.VMEM((tl, D_HALF), F32)],
        compiler_params=_ARB)(proj, proj, dcat, ng, sw, sb_t)


def _s5_params(w):
    prep_args = (w["ssm_log_step"], w["ssm_a_re"], w["ssm_a_im"], w["ssm_b_re"], w["ssm_b_im"])
    (lr, li, br, bi), prep_vjp = jax.vjp(jax.vmap(_s5_prep), *prep_args)
    wb, wc = jax.vmap(_s5_mats)(br, bi, w["ssm_c_re"], w["ssm_c_im"])
    consts = lambda reverse: jax.vmap(functools.partial(_s5_scan_consts, reverse=reverse))(*prep_args[:3])
    return dict(wb=wb.astype(BF16), wc=wc.astype(BF16), d=w["ssm_d"][:, None, :], sc=consts(False),
                sc_rev=consts(True), prep_vjp=prep_vjp)


def _s5_param_grads(s5, dwb, dwc, dlam, dd):
    dbr, dbi, dcr, dci = jax.vmap(_s5_mats_bwd)(dwb, dwc)
    n = dlam.shape[0]
    dlr = dlam[:, :, 0, :S5_STATE_LANES].reshape(n, N_SSM_GROUPS, SSM_STATE)
    dli = dlam[:, :, 0, S5_STATE_LANES:].reshape(n, N_SSM_GROUPS, SSM_STATE)
    dls, dar, dai, db_re, db_im = s5["prep_vjp"]((dlr, dli, dbr, dbi))
    return dict(ssm_log_step=dls, ssm_a_re=dar, ssm_a_im=dai, ssm_b_re=db_re, ssm_b_im=db_im, ssm_c_re=dcr,
                ssm_c_im=dci, ssm_d=dd[:, 0, :])


TILES = {
    "mm_up": (2048, 512), "mm_up_tall": (4096, 512), "mm_up_dw": (512, 1408), "mm_down_dx": (1024, 2816),
    "mm_down": (512, 1024), "mm_down_dw": (256, 1024), "mm_even_in": (2048, 1024), "mm_odd_in": (2048, 768),
    "mm_mix_out": (1024, 1024), "mm_mix_out_dx": (2048, 1024), "mm_mix_out_dw": (1024, 512),
    "mm_even_in_dw": (1024, 512), "mm_odd_in_dw": (1024, 512),
}


def _layer_weights(i):
    j = i // 2
    mixer = [("even_w_in", j), ("even_w_out", j), ("ssm_glu_w", j)] if i % 2 == 0 else [("odd_w_in", j),
                                                                                         ("odd_w_out", j)]
    return dict(w_in=mixer[0], w_out=mixer[1], glu=mixer[2:], up=("ffn_w_up", i), down=("ffn_w_down", i))


class _LocalWeights:
    def __init__(self, w):
        self.w, self.grads = w, {}

    def carried_by(self, stage, i):
        return []

    def delivered(self, stage, i, outs):
        pass

    def weight(self, key):
        return self.w[key[0]][key[1]]

    def grad(self, key, dw):
        self.grads[key] = dw


class _ShardedWeights:
    def __init__(self, shards):
        self.shards = shards
        self.full, self.pending, self.scattered = {}, {}, {}

    def start(self, others):
        keys = [_layer_weights(0)["w_in"]]
        outs = _exchange_only(others + [self._gather(k) for k in keys], name="ag_first")
        self._take(keys, outs[len(others):])
        return outs[:len(others)]

    def _gather(self, key):
        return _Gather(self.shards[key[0]][key[1]])

    def _take(self, keys, outs):
        for key, got in zip(keys, outs):
            if BIG[key[0]] == 2:
                self.full[key] = jnp.swapaxes(got, 0, 1).reshape(got.shape[1], -1)
            else:
                self.full[key] = got.reshape(-1, got.shape[2])

    def _plan(self, stage, i):
        cur = _layer_weights(i)
        nxt = _layer_weights(i + 1) if i + 1 < DEPTH else None
        has_scan = lambda k: k % 2 == 0
        none = ([], [])
        return {
            "mm_in": ([cur["w_out"], *cur["glu"]], []) if i == 0 else none,
            "s5_fwd": ([cur["up"]], []),
            "mm_up": ([cur["down"]] + ([nxt["w_in"], nxt["w_out"], *nxt["glu"]] if nxt else []), []),
            "ffn_act": ([nxt["up"]], []) if nxt and not has_scan(i + 1) else none,
            "ffn_act_bwd": ([], [cur["down"]] + ([nxt["up"]] if nxt and not has_scan(i + 1) else [])),
            "s5_bwd": ([], [cur["up"]]),
            "mm_up_dx": ([], [nxt["w_in"], *nxt["glu"]]) if nxt else none,
            "mm_up_dw": ([], [nxt["w_out"]]) if nxt else none,
            "mm_in_dx": ([], [cur["w_out"]]) if i == 0 else none,
            "mm_in_dw": ([], list(cur["glu"])) if i == 0 else none,
        }[stage]

    def _scatter(self, key):
        name, layer = key
        src = self.pending.pop(key)
        layers, rows, cols = self.shards[name].shape
        if name not in self.scattered:
            self.scattered[name] = lax.empty((N_DEV, layers * rows, cols), src.dtype)
        return _Scatter(src, into=self.scattered[name], row0=layer * rows)

    def carried_by(self, stage, i):
        gather, scatter = self._plan(stage, i)
        return [self._gather(k) for k in gather] + [self._scatter(k) for k in scatter]

    def delivered(self, stage, i, outs):
        gather, scatter = self._plan(stage, i)
        self._take(gather, outs[:len(gather)])
        for (name, _), buf in zip(scatter, outs[len(gather):]):
            self.scattered[name] = buf

    def weight(self, key):
        return self.full[key]

    def grad(self, key, dw):
        self.pending[key] = _to_dest_major(dw, BIG[key[0]] - 1).astype(BF16)

    def finish(self, others):
        keys = list(self.pending)
        outs = _exchange_only(others + [self._scatter(k) for k in keys], name="rs_last")
        for (name, _), buf in zip(keys, outs[len(others):]):
            self.scattered[name] = buf
        return outs[:len(others)]


def _device_step(x, tgt, w, comm):
    saved = []
    s5 = _s5_params(w)
    h = _rms_fwd(x, w["norm_mix_g"][0:1], name="rms_fwd")
    for i in range(DEPTH):
        j = i // 2
        lw = _layer_weights(i)
        if i % 2 == 0:
            proj = _carry(comm, "mm_in", i, _mm, h, comm.weight(lw["w_in"]), "nn", name="mm_even_in")
            ya = _sconv_fwd(proj, w["even_conv_w"][j], name="sconv_fwd")
            states, ypre = _carry(comm, "s5_fwd", i, _s5_fwd, proj, s5["wb"][j], s5["wc"][j], s5["d"][j],
                                  s5["sc"][j], name="s5_fwd")
            yb = _glu_fwd(ypre, comm.weight(lw["glu"][0]), w["ssm_glu_b"][j][None, :], name="glu_fwd")
            cat = jnp.concatenate([ya, yb], axis=1)
            mix = (states, ypre)
        else:
            proj = _carry(comm, "mm_in", i, _mm, h, comm.weight(lw["w_in"]), "nn", name="mm_odd_in")
            yc = _pool_fwd(proj, w["pool_w"][j], w["pool_scale"][j][None, :], name="pool_fwd")
            sb_t = jnp.transpose(w["sgu_b"][j])
            yd = _sgu_fwd(proj, w["sgu_norm_g"][j][None, :], w["sgu_w"][j], sb_t, name="sgu_fwd")
            cat = jnp.concatenate([yc, yd], axis=1)
            mix = (sb_t,)
        x1, h2 = _mm(cat, comm.weight(lw["w_out"]), "nn", add=x, norm=(w["norm_ffn_g"], i), name="mm_mix_out")
        up = _carry(comm, "mm_up", i, _mm, h2, comm.weight(lw["up"]), "nn",
                    name="mm_up" if i + 1 < DEPTH else "mm_up_tall")
        act = _carry(comm, "ffn_act", i, _ffn_act, up, w["ffn_conv_w"][i], w["ffn_conv_b"][i:i + 1], name="ffn_act")
        if i + 1 < DEPTH:
            x2, h_next = _mm(act, comm.weight(lw["down"]), "nn", add=x1, norm=(w["norm_mix_g"], i + 1),
                             name="mm_down")
        else:
            x2, h_next = _mm(act, comm.weight(lw["down"]), "nn", add=x1, name="mm_down_last"), None
        saved.append((x, h, proj, cat, x1, h2, up, act, mix))
        x, h = x2, h_next

    loss, dx, dxb, dgf = _loss_head(x, w["norm_final_g"][None, :], tgt, name="loss_head")
    per_layer = {}
    s5_grads = []

    def put(name, idx, val):
        per_layer.setdefault(name, {})[idx] = val

    for i in reversed(range(DEPTH)):
        j = i // 2
        lw = _layer_weights(i)
        x0, h, proj, cat, x1, h2, up, act, mix = saved[i]
        dact = _mm(dxb, comm.weight(lw["down"]), "nt", name="mm_down_dx")
        comm.grad(lw["down"], _mm(act, dxb, "tn", out_dtype=BF16, name="mm_down_dw"))
        dup, dcw, dcb = _carry(comm, "ffn_act_bwd", i, _ffn_act_bwd, up, dact, w["ffn_conv_w"][i],
                               w["ffn_conv_b"][i:i + 1], name="ffn_act_bwd")
        put("ffn_conv_w", i, dcw)
        put("ffn_conv_b", i, dcb[0])
        dx1, dx1b, dg2 = _carry(comm, "mm_up_dx", i, _mm, dup, comm.weight(lw["up"]), "nt", tm_cap=256, tn_cap=D_MODEL,
                                norm_bwd=(x1, w["norm_ffn_g"], i, dx), name="mm_up_dx")
        comm.grad(lw["up"], _carry(comm, "mm_up_dw", i, _mm, h2, dup, "tn", out_dtype=BF16,
                                   name="mm_up_dw"))
        put("norm_ffn_g", i, dg2[0])
        dcat = _mm(dx1b, comm.weight(lw["w_out"]), "nt", name="mm_mix_out_dx")
        comm.grad(lw["w_out"], _mm(cat, dx1b, "tn", out_dtype=BF16, name="mm_mix_out_dw"))
        if i % 2 == 0:
            states, ypre = mix
            dxa, dba, dca, dcw_a = _sconv_bwd(proj, dcat, w["even_conv_w"][j], name="sconv_bwd")
            put("even_conv_w", j, dcw_a)
            dypre, dwg, dbg = _glu_bwd(dcat, ypre, comm.weight(lw["glu"][0]), w["ssm_glu_b"][j][None, :],
                                       name="glu_bwd")
            comm.grad(lw["glu"][0], dwg)
            put("ssm_glu_b", j, dbg[0])
            du, dwb, dwc, dlam, dd = _carry(comm, "s5_bwd", i, _s5_bwd, proj, dypre, states, s5["wb"][j], s5["wc"][j],
                                            s5["d"][j], s5["sc_rev"][j], name="s5_bwd")
            s5_grads.insert(0, (dwb, dwc, dlam, dd))
            dproj = jnp.concatenate([dxa, dba, dca, du], axis=1)
            in_name = "mm_even_in"
        else:
            (sb_t,) = mix
            dz, dpw, dps = _pool_bwd(proj, dcat, w["pool_w"][j], w["pool_scale"][j][None, :], name="pool_bwd")
            put("pool_w", j, dpw)
            put("pool_scale", j, dps[0])
            dsu, dsv, dsw, dbm, dng = _sgu_bwd(proj, dcat, w["sgu_norm_g"][j][None, :], w["sgu_w"][j], sb_t,
                                               name="sgu_bwd")
            put("sgu_w", j, dsw)
            put("sgu_b", j, jnp.sum(dbm, axis=-1))
            put("sgu_norm_g", j, dng[0])
            dproj = jnp.concatenate([dz, dsu, dsv], axis=1)
            in_name = "mm_odd_in"
        dx, dxb, dg1 = _carry(comm, "mm_in_dx", i, _mm, dproj, comm.weight(lw["w_in"]), "nt", tn_cap=D_MODEL,
                              norm_bwd=(x0, w["norm_mix_g"], i, dx1), name=in_name + "_dx")
        comm.grad(lw["w_in"], _carry(comm, "mm_in_dw", i, _mm, h, dproj, "tn", out_dtype=BF16, name=in_name + "_dw"))
        put("norm_mix_g", i, dg1[0])

    grads = {nm: jnp.stack([vals[k] for k in sorted(vals)]) for nm, vals in per_layer.items()}
    grads["norm_final_g"] = dgf[0]
    grads.update(_s5_param_grads(s5, *[jnp.stack(parts) for parts in zip(*s5_grads)]))
    return loss, dx, grads


def _carry(comm, stage, i, fn, *args, **kwargs):
    exchanges = comm.carried_by(stage, i)
    if not exchanges:
        return fn(*args, **kwargs)
    out, moved = fn(*args, exchanges=exchanges, **kwargs)
    comm.delivered(stage, i, moved)
    return out


def _sum_parts(parts, *, name):
    g, r, c = parts.shape
    tr = _pick_tile(r, max(16, EXCHANGE_BLOCK_ELEMS // c), 16)

    def body(p_ref, o_ref):
        acc = p_ref[0].astype(F32)
        for k in range(1, g):
            acc = acc + p_ref[k].astype(F32)
        o_ref[...] = acc

    return pl.pallas_call(
        body, name=name, grid=(r // tr,), in_specs=[pl.BlockSpec((g, tr, c), lambda i: (0, i, 0))],
        out_specs=pl.BlockSpec((tr, c), lambda i: (i, 0)), out_shape=jax.ShapeDtypeStruct((r, c), F32),
        compiler_params=_PAR)(parts)


def _adamw(w, m, v, g_parts, *, name):
    r, c = w.shape
    g = g_parts.shape[0]
    tc = _pick_tile(c, 8192, LANE)
    tr = _pick_tile(r, max(16, (1 << 18) // tc), 16)
    c1 = 1.0 - ADAM_B1 ** ADAM_STEP
    c2 = 1.0 - ADAM_B2 ** ADAM_STEP

    def body(w_ref, m_ref, v_ref, g_ref, go_ref, d_ref, mo_ref, vo_ref):
        grad = g_ref[0].astype(F32)
        for k in range(1, g):
            grad = grad + g_ref[k].astype(F32)
        m_new = ADAM_B1 * m_ref[...] + (1.0 - ADAM_B1) * grad
        v_new = ADAM_B2 * v_ref[...] + (1.0 - ADAM_B2) * (grad * grad)
        go_ref[...] = grad
        mo_ref[...] = m_new
        vo_ref[...] = v_new
        d_ref[...] = -ADAM_LR * ((m_new / c1) / (jnp.sqrt(v_new / c2) + ADAM_EPS) + ADAM_WD * w_ref[...])

    blk = pl.BlockSpec((tr, tc), lambda i, j: (i, j))
    out = jax.ShapeDtypeStruct((r, c), F32)
    return pl.pallas_call(
        body, name=name, grid=(r // tr, c // tc),
        in_specs=[blk, blk, blk, pl.BlockSpec((g, tr, tc), lambda i, j: (0, i, j))],
        out_specs=[blk, blk, blk, blk], out_shape=[out, out, out, out], compiler_params=_PAR2)(w, m, v, g_parts)


WEIGHT_NAMES = ['norm_mix_g', 'even_w_in', 'even_conv_w', 'ssm_log_step', 'ssm_a_re', 'ssm_a_im', 'ssm_b_re',
                'ssm_b_im', 'ssm_c_re', 'ssm_c_im', 'ssm_d', 'ssm_glu_w', 'ssm_glu_b', 'even_w_out', 'odd_w_in',
                'pool_w', 'pool_scale', 'sgu_norm_g', 'sgu_w', 'sgu_b', 'odd_w_out', 'norm_ffn_g', 'ffn_w_up',
                'ffn_conv_w', 'ffn_conv_b', 'ffn_w_down', 'norm_final_g']
BIG = {'even_w_in': 2, 'ssm_glu_w': 1, 'even_w_out': 1, 'odd_w_in': 2, 'odd_w_out': 1, 'ffn_w_up': 2,
       'ffn_w_down': 1}
SMALL = {'even_conv_w': 2, 'pool_scale': 1, 'sgu_norm_g': 1, 'ffn_conv_w': 2}
BIG_ROWS = 512
SMALL_ROWS = 16


def _pad_to(n, q):
    return -(-n // q) * q


def _pack(arrays, dtype, rows, lead=()):
    flat = [a.reshape(lead + (-1,)).astype(dtype) for a in arrays]
    n = sum(f.shape[-1] for f in flat)
    pad = _pad_to(n, rows * LANE) - n
    if pad:
        flat.append(jnp.zeros(lead + (pad,), dtype))
    return jnp.concatenate(flat, axis=-1).reshape(lead + (rows, -1))


def _unpack(buf, shapes, lead=()):
    flat = buf.reshape(lead + (-1,))
    out, off = [], 0
    for shp in shapes:
        n = math.prod(shp)
        out.append(flat[..., off:off + n].reshape(lead + tuple(shp)))
        off += n
    return out


def _to_dest_major(full, axis):
    shp = full.shape
    split = full.reshape(shp[:axis] + (N_DEV, shp[axis] // N_DEV) + shp[axis + 1:])
    return jnp.moveaxis(split, axis, 0)


def _from_dest_major(blocks, axis):
    moved = jnp.moveaxis(blocks, 0, axis)
    shp = moved.shape
    return moved.reshape(shp[:axis] + (shp[axis] * shp[axis + 1],) + shp[axis + 2:])


def _rows_2d(a):
    return a.reshape(-1, a.shape[-1])


def _rows_2d_lead(a):
    return a.reshape(a.shape[0], -1, a.shape[-1])


def kernel(x, norm_mix_g, even_w_in, even_conv_w, ssm_log_step, ssm_a_re, ssm_a_im, ssm_b_re, ssm_b_im, ssm_c_re, ssm_c_im, ssm_d, ssm_glu_w, ssm_glu_b, even_w_out, odd_w_in, pool_w, pool_scale, sgu_norm_g, sgu_w, sgu_b, odd_w_out, norm_ffn_g, ffn_w_up, ffn_conv_w, ffn_conv_b, ffn_w_down, norm_final_g, loss_target, m_norm_mix_g, m_even_w_in, m_even_conv_w, m_ssm_log_step, m_ssm_a_re, m_ssm_a_im, m_ssm_b_re, m_ssm_b_im, m_ssm_c_re, m_ssm_c_im, m_ssm_d, m_ssm_glu_w, m_ssm_glu_b, m_even_w_out, m_odd_w_in, m_pool_w, m_pool_scale, m_sgu_norm_g, m_sgu_w, m_sgu_b, m_odd_w_out, m_norm_ffn_g, m_ffn_w_up, m_ffn_conv_w, m_ffn_conv_b, m_ffn_w_down, m_norm_final_g, v_norm_mix_g, v_even_w_in, v_even_conv_w, v_ssm_log_step, v_ssm_a_re, v_ssm_a_im, v_ssm_b_re, v_ssm_b_im, v_ssm_c_re, v_ssm_c_im, v_ssm_d, v_ssm_glu_w, v_ssm_glu_b, v_even_w_out, v_odd_w_in, v_pool_w, v_pool_scale, v_sgu_norm_g, v_sgu_w, v_sgu_b, v_odd_w_out, v_norm_ffn_g, v_ffn_w_up, v_ffn_conv_w, v_ffn_conv_b, v_ffn_w_down, v_norm_final_g):
    given = dict(locals())
    wts = {n: given[n] for n in WEIGHT_NAMES}
    mom = {n: given["m_" + n] for n in WEIGHT_NAMES}
    var = {n: given["v_" + n] for n in WEIGHT_NAMES}
    repl = [n for n in WEIGHT_NAMES if n not in BIG and n not in SMALL]

    small_shapes = [wts[n].shape for n in SMALL]
    comm = _ShardedWeights({n: wts[n].astype(BF16) for n in BIG})
    (small_all,) = comm.start([_Gather(_pack([wts[n] for n in SMALL], F32, SMALL_ROWS))])
    full = {n: wts[n] for n in repl}
    for n, blocks in zip(SMALL, _unpack(small_all, small_shapes, lead=(N_DEV,))):
        full[n] = _from_dest_major(blocks, SMALL[n])

    loss, dx, grads = _device_step(x[0], loss_target[0], full, comm)

    repl_shapes = [wts[n].shape for n in repl]
    repl_flat = jnp.concatenate([grads[n].reshape(-1) for n in repl] + [loss.reshape(-1)])
    n_repl = repl_flat.shape[0]
    chunk = _pad_to(-(-n_repl // N_DEV), SMALL_ROWS * LANE)
    repl_flat = jnp.pad(repl_flat, (0, N_DEV * chunk - n_repl))
    small_part = _pack([_to_dest_major(grads[n], SMALL[n]) for n in SMALL], F32, SMALL_ROWS, lead=(N_DEV,))
    small_cols = small_part.shape[2]
    (small_rs,) = comm.finish([_Scatter(
        jnp.concatenate([small_part, repl_flat.reshape(N_DEV, SMALL_ROWS, chunk // SMALL_ROWS)], axis=2))])
    small_sum = _sum_parts(small_rs, name="rs_sum_small")
    (repl_all,) = _exchange_only([_Gather(small_sum[:, small_cols:])], name="ag_repl")
    repl_sum = repl_all.reshape(-1)
    total_loss = repl_sum[n_repl - 1]

    out = {}
    for n in BIG:
        shp = wts[n].shape
        res = _adamw(_rows_2d(wts[n]), _rows_2d(mom[n]), _rows_2d(var[n]), comm.scattered[n], name="adamw_" + n)
        out[n] = [r.reshape(shp) for r in res]

    def small_vec(shard_part, repl_part):
        flat = jnp.concatenate([shard_part.reshape(-1), repl_part])
        return flat.reshape(SMALL_ROWS, -1)

    def small_tree(tree):
        tail = jnp.concatenate([tree[n].reshape(-1) for n in repl])
        tail = jnp.pad(tail, (0, N_DEV * chunk - tail.shape[0]))
        return small_vec(_pack([tree[n] for n in SMALL], F32, SMALL_ROWS), tail)

    res = _adamw(small_tree(wts), small_tree(mom), small_tree(var), small_vec(small_sum[:, :small_cols], repl_sum)[None],
                 name="adamw_small")
    n_small = SMALL_ROWS * small_cols
    for k, r in enumerate(res):
        flat = r.reshape(-1)
        shard = _unpack(flat[:n_small], small_shapes)
        rest = _unpack(flat[n_small:], repl_shapes)
        for n, val in zip(SMALL, shard):
            out.setdefault(n, [None] * 4)[k] = val
        for n, val in zip(repl, rest):
            out.setdefault(n, [None] * 4)[k] = val

    grad_x = dx[None]
    return (total_loss, grad_x, *[out[n][0] for n in WEIGHT_NAMES], *[out[n][1] for n in WEIGHT_NAMES],
            *[out[n][2] for n in WEIGHT_NAMES], *[out[n][3] for n in WEIGHT_NAMES])
```

```python
import functools
import math

import jax
import jax.numpy as jnp
from jax import lax
from jax.experimental import pallas as pl
from jax.experimental.pallas import tpu as pltpu

F32 = jnp.float32
BF16 = jnp.bfloat16

D_MODEL = 1024
DEPTH = 4
D_HALF = D_MODEL // 2
SSM_GROUP = 16
N_SSM_GROUPS = D_HALF // SSM_GROUP
SSM_STATE = 64
POOL_WINDOWS = (2, 4, 8, 16)
SGU_HEADS = 4
CHUNK = 128
D_FF = 2816
CONV_WIDTH = 3
EPS = 1e-6
N_DEV = 8

ADAM_LR = 0.001
ADAM_B1 = 0.9
ADAM_B2 = 0.999
ADAM_EPS = 1e-08
ADAM_WD = 0.01
ADAM_STEP = 10

LANE = 128
SUBLANE = 8
S5_LANE_BLOCKS = D_HALF // LANE
S5_STATE_LANES = (N_SSM_GROUPS // S5_LANE_BLOCKS) * SSM_STATE
S5_TIME_CHUNK = 512
EXCHANGE_BLOCK_ELEMS = 1 << 20

GELU_K = math.sqrt(2.0 / math.pi)
GELU_C = 0.044715

_ARB = pltpu.CompilerParams(dimension_semantics=("arbitrary",))
_ARB2 = pltpu.CompilerParams(dimension_semantics=("arbitrary", "arbitrary"))
_PAR = pltpu.CompilerParams(dimension_semantics=("parallel",))
_PAR2 = pltpu.CompilerParams(dimension_semantics=("parallel", "parallel"))


def _pick_tile(n, cap, mult):
    if n <= cap:
        return n
    best = None
    for t in range(mult, cap + 1, mult):
        if n % t == 0:
            best = t
    assert best is not None, (n, cap, mult)
    return best


_MESH = pl.DeviceIdType.MESH
_ANY = pl.BlockSpec(memory_space=pl.ANY)
SEMS_PER_EXCHANGE = N_DEV - 1


class _Gather:
    into = None

    def __init__(self, src):
        self.src = src
        self.out_shape = jax.ShapeDtypeStruct((N_DEV,) + src.shape, src.dtype)

    def copies(self, x_ref, out_ref, send_sems, recv_sems, local_sem):
        x, y, cc = lax.axis_index("x"), lax.axis_index("y"), lax.axis_index("c")
        me, sibling = (x, y, cc), (x, y, 1 - cc)
        chips = [(1 - x, y), (x, 1 - y), (1 - x, 1 - y)]

        def rows(px, py, pc):
            return out_ref.at[4 * px + 2 * py + pc]

        def copy(k, block, to, src=None):
            return pltpu.make_async_remote_copy(
                src_ref=rows(*block) if src is None else src, dst_ref=rows(*block),
                send_sem=send_sems.at[k], recv_sem=recv_sems.at[k], device_id=to, device_id_type=_MESH)

        return dict(
            mine=pltpu.make_async_copy(x_ref, rows(*me), local_sem),
            first=[copy(0, me, sibling, src=x_ref)] + [copy(1 + k, me, (*chip, cc), src=x_ref)
                                                       for k, chip in enumerate(chips)],
            passed=[copy(4 + k, (*chip, cc), sibling) for k, chip in enumerate(chips)],
            over_ici=[copy(1 + k, (*chip, cc), me) for k, chip in enumerate(chips)],
            from_sibling=[copy(0, sibling, me)] + [copy(4 + k, (*chip, 1 - cc), me) for k, chip in enumerate(chips)])

    def start(self, *refs):
        cps = self.copies(*refs)
        cps["mine"].start()
        for cp in cps["first"]:
            cp.start()

    def finish(self, *refs):
        cps = self.copies(*refs)
        for arrived, onward in zip(cps["over_ici"], cps["passed"]):
            arrived.wait_recv()
            onward.start()
        for arrived in cps["from_sibling"]:
            arrived.wait_recv()
        for cp in cps["first"] + cps["passed"]:
            cp.wait_send()
        cps["mine"].wait()


class _Scatter:
    def __init__(self, src, into=None, row0=0):
        self.src, self.into, self.row0 = src, into, row0
        whole = src if into is None else into
        self.out_shape = jax.ShapeDtypeStruct(whole.shape, whole.dtype)

    def copies(self, p_ref, whole_ref, send_sems, recv_sems, local_sem):
        x, y, cc = lax.axis_index("x"), lax.axis_index("y"), lax.axis_index("c")
        me = 4 * x + 2 * y + cc
        rows = self.src.shape[1]
        out_ref = whole_ref if self.into is None else whole_ref.at[:, pl.ds(self.row0, rows)]
        sends, arrivals = [], []
        for k in range(1, N_DEV):
            px = (1 - x) if k & 4 else x
            py = (1 - y) if k & 2 else y
            pc = (1 - cc) if k & 1 else cc
            peer = 4 * px + 2 * py + pc
            kw = dict(send_sem=send_sems.at[k - 1], recv_sem=recv_sems.at[k - 1], device_id=(px, py, pc),
                      device_id_type=_MESH)
            sends.append(pltpu.make_async_remote_copy(src_ref=p_ref.at[peer], dst_ref=out_ref.at[me], **kw))
            arrivals.append(pltpu.make_async_remote_copy(src_ref=p_ref.at[me], dst_ref=out_ref.at[peer], **kw))
        return dict(mine=pltpu.make_async_copy(p_ref.at[me], out_ref.at[me], local_sem), sends=sends,
                    arrivals=arrivals)

    def start(self, *refs):
        cps = self.copies(*refs)
        cps["mine"].start()
        for cp in cps["sends"]:
            cp.start()

    def finish(self, *refs):
        cps = self.copies(*refs)
        for cp in cps["arrivals"]:
            cp.wait_recv()
        for cp in cps["sends"]:
            cp.wait_send()
        cps["mine"].wait()


class _SemView:
    def __init__(self, ref, lo):
        self.ref, self.lo = ref, lo

    @property
    def at(self):
        return self

    def __getitem__(self, k):
        return self.ref.at[self.lo + k]


def _call(body, *, name, grid, in_specs, out_specs, out_shape, args, scratch_shapes=(), parallel=True, exchanges=()):
    n_axes = len(grid)
    if not exchanges:
        sem = ("parallel" if parallel else "arbitrary",) * n_axes
        return pl.pallas_call(
            body, name=name, grid=grid, in_specs=in_specs, out_specs=out_specs, out_shape=out_shape,
            scratch_shapes=scratch_shapes, compiler_params=pltpu.CompilerParams(dimension_semantics=sem))(*args)
    single = not isinstance(out_shape, (list, tuple))
    out_specs = [out_specs] if single else list(out_specs)
    out_shape = [out_shape] if single else list(out_shape)
    n_in, n_out, n_scr, n_x = len(in_specs), len(out_specs), len(scratch_shapes), len(exchanges)
    landing = [(e, ex.into) for e, ex in enumerate(exchanges) if ex.into is not None]
    aliases = {n_in + n_x + pos: n_out + e for pos, (e, _) in enumerate(landing)}

    def wrapped(*refs):
        ins, refs = refs[:n_in], refs[n_in:]
        x_in, refs = refs[:n_x], refs[n_x + len(landing):]
        outs, refs = refs[:n_out], refs[n_out:]
        x_out, refs = refs[:n_x], refs[n_x:]
        scr, (send_sems, recv_sems, local_sems) = refs[:n_scr], refs[n_scr:]
        ids = [pl.program_id(k) for k in range(n_axes)]
        first = functools.reduce(jnp.logical_and, [i == 0 for i in ids])
        last = functools.reduce(jnp.logical_and, [i == g - 1 for i, g in zip(ids, grid)])

        def sems(e):
            lo = e * SEMS_PER_EXCHANGE
            return _SemView(send_sems, lo), _SemView(recv_sems, lo), local_sems.at[e]

        @pl.when(first)
        def _():
            for e, ex in enumerate(exchanges):
                ex.start(x_in[e], x_out[e], *sems(e))

        body(*ins, *outs, *scr)

        @pl.when(last)
        def _():
            for e, ex in enumerate(exchanges):
                ex.finish(x_in[e], x_out[e], *sems(e))

    res = pl.pallas_call(
        wrapped, name=name, grid=grid, in_specs=list(in_specs) + [_ANY] * (n_x + len(landing)),
        out_specs=out_specs + [_ANY] * n_x, out_shape=out_shape + [ex.out_shape for ex in exchanges],
        input_output_aliases=aliases,
        scratch_shapes=list(scratch_shapes) + [pltpu.SemaphoreType.DMA((n_x * SEMS_PER_EXCHANGE,)),
                                               pltpu.SemaphoreType.DMA((n_x * SEMS_PER_EXCHANGE,)),
                                               pltpu.SemaphoreType.DMA((n_x,))],
        compiler_params=pltpu.CompilerParams(dimension_semantics=("arbitrary",) * n_axes),
    )(*args, *[ex.src for ex in exchanges], *[buf for _, buf in landing])
    outs, x_outs = res[:n_out], res[n_out:]
    return (outs[0] if single else outs), x_outs


def _exchange_only(exchanges, *, name):
    def body():
        pass

    return _call(body, name=name, grid=(1,), in_specs=[], out_specs=[], out_shape=[], args=[],
                 exchanges=exchanges)[1]


def _shift_dn(x, d):
    rolled = pltpu.roll(x, d, 0)
    if x.shape[0] <= SUBLANE or d >= SUBLANE:
        row = lax.broadcasted_iota(jnp.int32, x.shape, 0)
        return jnp.where(row >= d, rolled, 0.0)
    row = lax.broadcasted_iota(jnp.int32, (SUBLANE, x.shape[1]), 0)
    return jnp.concatenate([jnp.where(row >= d, rolled[:SUBLANE], 0.0), rolled[SUBLANE:]], axis=0)


def _shift_up(x, d):
    n = x.shape[0]
    rolled = pltpu.roll(x, n - d, 0)
    if n <= SUBLANE or d >= SUBLANE:
        row = lax.broadcasted_iota(jnp.int32, x.shape, 0)
        return jnp.where(row < n - d, rolled, 0.0)
    row = lax.broadcasted_iota(jnp.int32, (SUBLANE, x.shape[1]), 0)
    return jnp.concatenate([rolled[:n - SUBLANE], jnp.where(row < SUBLANE - d, rolled[n - SUBLANE:], 0.0)], axis=0)


def _gelu(x):
    return 0.5 * x * (1.0 + jnp.tanh(GELU_K * (x + GELU_C * x * x * x)))


def _gelu_grad(x):
    t = jnp.tanh(GELU_K * (x + GELU_C * x * x * x))
    return 0.5 * (1.0 + t) + 0.5 * x * (1.0 - t * t) * (GELU_K * (1.0 + 3.0 * GELU_C * x * x))


def _sigmoid(x):
    return 1.0 / (1.0 + jnp.exp(-x))


def _conv3(x, w_ref):
    return w_ref[0:1, :] * _shift_dn(x, 2) + w_ref[1:2, :] * _shift_dn(x, 1) + w_ref[2:3, :] * x


def _conv3_bwd_x(dy, w_ref):
    return w_ref[2:3, :] * dy + w_ref[1:2, :] * _shift_up(dy, 1) + w_ref[0:1, :] * _shift_up(dy, 2)


def _conv3_bwd_w(dy, x):
    return jnp.concatenate([
        jnp.sum(dy * _shift_dn(x, 2), axis=0, keepdims=True),
        jnp.sum(dy * _shift_dn(x, 1), axis=0, keepdims=True),
        jnp.sum(dy * x, axis=0, keepdims=True)], axis=0)


def _dot(a, b, dims):
    return lax.dot_general(a.astype(BF16), b.astype(BF16), (dims, ((), ())), preferred_element_type=F32)


_NN = ((1,), (0,))
_NT = ((1,), (1,))
_TN = ((0,), (0,))


def _tiles(name, m, n, default):
    tm, tn = TILES.get(name, default)
    return math.gcd(tm, m), math.gcd(tn, n)


def _mm(a, b, mode, *, name, out_dtype=F32, add=None, norm=None, norm_bwd=None, tm_cap=512, tn_cap=1536,
        exchanges=()):
    halves = None
    if mode == "tn":
        r, m = a.shape
        n = b.shape[-1] * (2 if b.ndim == 3 else 1)
        tm, tn = _tiles(name, m, n, (_pick_tile(m, 256, LANE), _pick_tile(n, tn_cap, LANE)))
        if b.ndim == 3:
            per_half = b.shape[-1] // tn
            b_spec = pl.BlockSpec((None, r, tn), lambda i, j: (j // per_half, 0, j % per_half))
        else:
            b_spec = pl.BlockSpec((r, tn), lambda i, j: (0, j))
        in_specs = [pl.BlockSpec((r, tm), lambda i, j: (0, i)), b_spec]
        dims = _TN
    elif mode == "nt" and a.ndim == 3:
        _, m, halves = a.shape
        n = b.shape[0]
        tm, tn = _tiles(name, m, n, (_pick_tile(m, tm_cap, SUBLANE), _pick_tile(n, tn_cap, LANE)))
        in_specs = [pl.BlockSpec((2, tm, halves), lambda i, j: (0, i, 0)),
                    pl.BlockSpec((tn, 2 * halves), lambda i, j: (j, 0))]
        dims = _NT
    elif mode == "nn":
        m, k = a.shape
        n = b.shape[1]
        tm, tn = _tiles(name, m, n, (_pick_tile(m, tm_cap, SUBLANE), _pick_tile(n, tn_cap, LANE)))
        in_specs = [pl.BlockSpec((tm, k), lambda i, j: (i, 0)), pl.BlockSpec((k, tn), lambda i, j: (0, j))]
        dims = _NN
    else:
        m, k = a.shape
        n = b.shape[0]
        tm, tn = _tiles(name, m, n, (_pick_tile(m, tm_cap, SUBLANE), _pick_tile(n, tn_cap, LANE)))
        in_specs = [pl.BlockSpec((tm, k), lambda i, j: (i, 0)), pl.BlockSpec((tn, k), lambda i, j: (j, 0))]
        dims = _NT
    assert m % tm == 0 and n % tn == 0, (name, m, n, tm, tn)
    args = [a, b]
    tile = pl.BlockSpec((tm, tn), lambda i, j: (i, j))
    if add is not None:
        in_specs.append(tile)
        args.append(add)
    out_specs, out_shape = tile, jax.ShapeDtypeStruct((m, n), out_dtype)
    if norm is not None:
        gains, layer = norm
        assert tn == n
        in_specs.append(pl.BlockSpec((None, 1, n), lambda i, j: (layer, 0, 0)))
        args.append(gains.reshape(gains.shape[0], 1, n))
        out_specs, out_shape = [tile, tile], [out_shape, jax.ShapeDtypeStruct((m, n), BF16)]
    if norm_bwd is not None:
        x_in, gains, layer, res = norm_bwd
        assert tn == n and add is None and norm is None
        vec = pl.BlockSpec((None, 1, n), lambda i, j: (layer, 0, 0))
        in_specs += [tile, vec, tile]
        args += [x_in, gains.reshape(gains.shape[0], 1, n), res]
        out_specs = [tile, tile, pl.BlockSpec((1, n), lambda i, j: (0, 0))]
        out_shape = [jax.ShapeDtypeStruct((m, n), F32), jax.ShapeDtypeStruct((m, n), BF16),
                     jax.ShapeDtypeStruct((1, n), F32)]

    def body(*refs):
        if halves is None:
            acc = _dot(refs[0][...], refs[1][...], dims)
        else:
            acc = (_dot(refs[0][0], refs[1][:, :halves], dims) + _dot(refs[0][1], refs[1][:, halves:], dims))
        if add is not None:
            acc = acc + refs[2][...]
        if norm_bwd is not None:
            x_ref, g_ref, res_ref, dx_ref, dxb_ref, dg_ref = refs[2:]

            @pl.when(pl.program_id(0) == 0)
            def _():
                dg_ref[...] = jnp.zeros_like(dg_ref)
            dx, xn = _rms_bwd_rows(acc, x_ref[...], g_ref[...])
            dx = dx + res_ref[...]
            dx_ref[...] = dx
            dxb_ref[...] = dx.astype(BF16)
            dg_ref[...] += jnp.sum(acc * xn, axis=0, keepdims=True)
        elif norm is None:
            refs[-1][...] = acc.astype(out_dtype)
        else:
            refs[-2][...] = acc.astype(out_dtype)
            r = lax.rsqrt(jnp.mean(acc * acc, axis=-1, keepdims=True) + EPS)
            refs[-1][...] = (acc * r * refs[-3][...]).astype(BF16)

    return _call(body, name=name, grid=(m // tm, n // tn), in_specs=in_specs, out_specs=out_specs,
                 out_shape=out_shape, args=args, parallel=norm_bwd is None, exchanges=exchanges)


def _rms_fwd(x, g, *, name):
    l, d = x.shape
    tl = _pick_tile(l, 512, SUBLANE)

    def body(x_ref, g_ref, h_ref):
        xv = x_ref[...]
        r = lax.rsqrt(jnp.mean(xv * xv, axis=-1, keepdims=True) + EPS)
        h_ref[...] = (xv * r * g_ref[...]).astype(BF16)

    return pl.pallas_call(
        body, name=name, grid=(l // tl,),
        in_specs=[pl.BlockSpec((tl, d), lambda i: (i, 0)), pl.BlockSpec((1, d), lambda i: (0, 0))],
        out_specs=pl.BlockSpec((tl, d), lambda i: (i, 0)),
        out_shape=jax.ShapeDtypeStruct((l, d), BF16), compiler_params=_PAR)(x, g)


def _rms_bwd_rows(dh, xv, g):
    r = lax.rsqrt(jnp.mean(xv * xv, axis=-1, keepdims=True) + EPS)
    a = dh * g
    m = jnp.mean(a * xv, axis=-1, keepdims=True)
    return r * a - xv * (r * r * r) * m, xv * r


def _rms_bwd(dh, x, g, res, *, name):
    l, d = x.shape
    tl = _pick_tile(l, 512, SUBLANE)

    def body(dh_ref, x_ref, g_ref, res_ref, dx_ref, dxb_ref, dg_ref):
        @pl.when(pl.program_id(0) == 0)
        def _():
            dg_ref[...] = jnp.zeros_like(dg_ref)
        dhv = dh_ref[...]
        dx, xn = _rms_bwd_rows(dhv, x_ref[...], g_ref[...])
        dx = dx + res_ref[...]
        dx_ref[...] = dx
        dxb_ref[...] = dx.astype(BF16)
        dg_ref[...] += jnp.sum(dhv * xn, axis=0, keepdims=True)

    row = pl.BlockSpec((tl, d), lambda i: (i, 0))
    vec = pl.BlockSpec((1, d), lambda i: (0, 0))
    return pl.pallas_call(
        body, name=name, grid=(l // tl,), in_specs=[row, row, vec, row], out_specs=[row, row, vec],
        out_shape=[jax.ShapeDtypeStruct((l, d), F32), jax.ShapeDtypeStruct((l, d), BF16),
                   jax.ShapeDtypeStruct((1, d), F32)],
        compiler_params=_ARB)(dh, x, g, res)


def _loss_head(x, g, tgt, *, name):
    l, d = x.shape
    tl = _pick_tile(l, 512, SUBLANE)

    def body(x_ref, g_ref, t_ref, loss_ref, dx_ref, dxb_ref, dg_ref):
        @pl.when(pl.program_id(0) == 0)
        def _():
            dg_ref[...] = jnp.zeros_like(dg_ref)
            loss_ref[...] = jnp.zeros_like(loss_ref)
        xv = x_ref[...]
        gv = g_ref[...]
        r = lax.rsqrt(jnp.mean(xv * xv, axis=-1, keepdims=True) + EPS)
        err = xv * r * gv - t_ref[...]
        row_loss = jnp.sum(err * err, axis=-1, keepdims=True) * (0.5 / d)
        loss_ref[...] += jnp.sum(row_loss, axis=0, keepdims=True)
        dy = err * (1.0 / d)
        dx, xn = _rms_bwd_rows(dy, xv, gv)
        dx_ref[...] = dx
        dxb_ref[...] = dx.astype(BF16)
        dg_ref[...] += jnp.sum(dy * xn, axis=0, keepdims=True)

    row = pl.BlockSpec((tl, d), lambda i: (i, 0))
    vec = pl.BlockSpec((1, d), lambda i: (0, 0))
    one = pl.BlockSpec((1, 1), lambda i: (0, 0))
    return pl.pallas_call(
        body, name=name, grid=(l // tl,), in_specs=[row, vec, row], out_specs=[one, row, row, vec],
        out_shape=[jax.ShapeDtypeStruct((1, 1), F32), jax.ShapeDtypeStruct((l, d), F32),
                   jax.ShapeDtypeStruct((l, d), BF16), jax.ShapeDtypeStruct((1, d), F32)],
        compiler_params=_ARB)(x, g, tgt)


def _ffn_act(up, cw, cb, *, name, exchanges=()):
    l = up.shape[0]
    nb = D_FF // LANE

    def body(ug_ref, uv_ref, wg_ref, wv_ref, bg_ref, bv_ref, o_ref):
        gc = _conv3(ug_ref[...], wg_ref) + bg_ref[...]
        vc = _conv3(uv_ref[...], wv_ref) + bv_ref[...]
        o_ref[...] = (gc * _sigmoid(gc) * vc).astype(BF16)

    col = lambda off: pl.BlockSpec((l, LANE), lambda j: (0, j + off))
    w3 = lambda off: pl.BlockSpec((CONV_WIDTH, LANE), lambda j: (0, j + off))
    b1 = lambda off: pl.BlockSpec((1, LANE), lambda j: (0, j + off))
    return _call(body, name=name, grid=(nb,), in_specs=[col(0), col(nb), w3(0), w3(nb), b1(0), b1(nb)],
                 out_specs=col(0), out_shape=jax.ShapeDtypeStruct((l, D_FF), BF16),
                 args=[up, up, cw, cw, cb, cb], exchanges=exchanges)


def _ffn_act_bwd(up, dact, cw, cb, *, name, exchanges=()):
    l = up.shape[0]
    nb = D_FF // LANE

    def conv(x, w_ref, b_ref):
        x1, x2 = _shift_dn(x, 1), _shift_dn(x, 2)
        return w_ref[0:1, :] * x2 + w_ref[1:2, :] * x1 + w_ref[2:3, :] * x + b_ref[...], (x, x1, x2)

    def half_bwd(k, dc, taps, w_ref, dup_ref, dcw_ref, dcb_ref):
        x0, x1, x2 = taps
        dcb_ref[k] = jnp.sum(dc, axis=0, keepdims=True)
        dcw_ref[k] = jnp.concatenate([jnp.sum(dc * x2, axis=0, keepdims=True),
                                      jnp.sum(dc * x1, axis=0, keepdims=True),
                                      jnp.sum(dc * x0, axis=0, keepdims=True)], axis=0)
        dup_ref[k] = _conv3_bwd_x(dc, w_ref).astype(BF16)

    def body(ug_ref, uv_ref, da_ref, wg_ref, wv_ref, bg_ref, bv_ref, dup_ref, dcw_ref, dcb_ref):
        da = da_ref[...]
        gc, g_taps = conv(ug_ref[...], wg_ref, bg_ref)
        vc, v_taps = conv(uv_ref[...], wv_ref, bv_ref)
        sg = _sigmoid(gc)
        half_bwd(0, da * vc * (sg * (1.0 + gc * (1.0 - sg))), g_taps, wg_ref, dup_ref, dcw_ref, dcb_ref)
        half_bwd(1, da * (gc * sg), v_taps, wv_ref, dup_ref, dcw_ref, dcb_ref)

    col = lambda off: pl.BlockSpec((l, LANE), lambda j: (0, j + off))
    w3 = lambda off: pl.BlockSpec((CONV_WIDTH, LANE), lambda j: (0, j + off))
    b1 = lambda off: pl.BlockSpec((1, LANE), lambda j: (0, j + off))
    both = lambda rows: pl.BlockSpec((2, rows, LANE), lambda j: (0, 0, j))
    res = _call(
        body, name=name, grid=(nb,),
        in_specs=[col(0), col(nb), col(0), w3(0), w3(nb), b1(0), b1(nb)],
        out_specs=[both(l), both(CONV_WIDTH), both(1)],
        out_shape=[jax.ShapeDtypeStruct((2, l, D_FF), BF16), jax.ShapeDtypeStruct((2, CONV_WIDTH, D_FF), F32),
                   jax.ShapeDtypeStruct((2, 1, D_FF), F32)],
        args=[up, up, dact, cw, cw, cb, cb], exchanges=exchanges)
    (dup, dcw, dcb), moved = res if exchanges else (res, None)
    outs = [dup, jnp.concatenate([dcw[0], dcw[1]], axis=1), jnp.concatenate([dcb[0], dcb[1]], axis=1)]
    return (outs, moved) if exchanges else outs


def _sconv_fwd(proj, cw, *, name):
    l = proj.shape[0]
    nb = D_HALF // LANE

    def body(xa_ref, ba_ref, ca_ref, w_ref, o_ref):
        o_ref[...] = (ba_ref[...] * _conv3(ca_ref[...] * xa_ref[...], w_ref)).astype(BF16)

    col = lambda off: pl.BlockSpec((l, LANE), lambda j: (0, j + off))
    return pl.pallas_call(
        body, name=name, grid=(nb,),
        in_specs=[col(0), col(nb), col(2 * nb), pl.BlockSpec((CONV_WIDTH, LANE), lambda j: (0, j))],
        out_specs=col(0), out_shape=jax.ShapeDtypeStruct((l, 2 * D_HALF), BF16),
        compiler_params=_PAR)(proj, proj, proj, cw)


def _sconv_bwd(proj, dcat, cw, *, name):
    l = proj.shape[0]
    nb = D_HALF // LANE

    def body(xa_ref, ba_ref, ca_ref, dy_ref, w_ref, dxa_ref, dba_ref, dca_ref, dw_ref):
        xa, ba, ca, dy = xa_ref[...], ba_ref[...], ca_ref[...], dy_ref[...]
        q = ca * xa
        dba_ref[...] = (dy * _conv3(q, w_ref)).astype(BF16)
        dconv = dy * ba
        dw_ref[...] = _conv3_bwd_w(dconv, q)
        dq = _conv3_bwd_x(dconv, w_ref)
        dxa_ref[...] = (dq * ca).astype(BF16)
        dca_ref[...] = (dq * xa).astype(BF16)

    col = lambda off: pl.BlockSpec((l, LANE), lambda j: (0, j + off))
    w3 = pl.BlockSpec((CONV_WIDTH, LANE), lambda j: (0, j))
    piece = jax.ShapeDtypeStruct((l, D_HALF), BF16)
    return pl.pallas_call(
        body, name=name, grid=(nb,),
        in_specs=[col(0), col(nb), col(2 * nb), col(0), w3],
        out_specs=[col(0), col(0), col(0), w3],
        out_shape=[piece, piece, piece, jax.ShapeDtypeStruct((CONV_WIDTH, D_HALF), F32)],
        compiler_params=_PAR)(proj, proj, proj, dcat, cw)


def _s5_prep(log_step, a_re, a_im, b_re, b_im):
    step = jnp.exp(log_step)[:, None]
    mag = jnp.exp(a_re * step)
    lr = mag * jnp.cos(a_im * step)
    li = mag * jnp.sin(a_im * step)
    nr = lr - 1.0
    den = a_re * a_re + a_im * a_im
    qr = (nr * a_re + li * a_im) / den
    qi = (li * a_re - nr * a_im) / den
    br = qr[..., None] * b_re - qi[..., None] * b_im
    bi = qr[..., None] * b_im + qi[..., None] * b_re
    return lr, li, br, bi


def _block_diag(m):
    nb, ng, r, c = m.shape
    eye = jnp.eye(ng, dtype=m.dtype)
    return jnp.einsum("bgrc,gh->bgrhc", m, eye).reshape(nb, ng * r, ng * c)


def _block_diag_extract(w, r, c):
    nb = w.shape[0]
    ng = w.shape[1] // r
    w5 = w.reshape(nb, ng, r, ng, c)
    return jnp.einsum("bgrhc,gh->bgrc", w5, jnp.eye(ng, dtype=w.dtype))


def _s5_mats(br, bi, c_re, c_im):
    g8 = N_SSM_GROUPS // S5_LANE_BLOCKS
    to_blk = lambda m: m.reshape(S5_LANE_BLOCKS, g8, m.shape[1], m.shape[2])
    wb = jnp.concatenate([_block_diag(to_blk(jnp.swapaxes(br, 1, 2))),
                          _block_diag(to_blk(jnp.swapaxes(bi, 1, 2)))], axis=2)
    wc = jnp.concatenate([_block_diag(to_blk(jnp.swapaxes(c_re, 1, 2))),
                          _block_diag(to_blk(jnp.swapaxes(-c_im, 1, 2)))], axis=1)
    return wb, wc


def _s5_mats_bwd(dwb, dwc):
    g, p, h = N_SSM_GROUPS, SSM_STATE, SSM_GROUP
    half = S5_STATE_LANES
    dbr = jnp.swapaxes(_block_diag_extract(dwb[:, :, :half], h, p).reshape(g, h, p), 1, 2)
    dbi = jnp.swapaxes(_block_diag_extract(dwb[:, :, half:], h, p).reshape(g, h, p), 1, 2)
    dcr = jnp.swapaxes(_block_diag_extract(dwc[:, :half, :], p, h).reshape(g, p, h), 1, 2)
    dci = -jnp.swapaxes(_block_diag_extract(dwc[:, half:, :], p, h).reshape(g, p, h), 1, 2)
    return dbr, dbi, dcr, dci


def _s5_scan_consts(log_step, a_re, a_im, reverse):
    step = jnp.exp(log_step)[:, None]
    xr = (a_re * step).reshape(S5_LANE_BLOCKS, 1, S5_STATE_LANES)
    xi = (a_im * step).reshape(S5_LANE_BLOCKS, 1, S5_STATE_LANES)
    if reverse:
        xi = -xi
    row = jnp.arange(SUBLANE, dtype=F32).reshape(1, SUBLANE, 1)

    def power(n):
        mag = jnp.exp(n * xr)
        return jnp.concatenate([mag * jnp.cos(n * xi), mag * jnp.sin(n * xi)], axis=-1)

    kinds = []
    for d in (1, 2, 4):
        keep = (row <= SUBLANE - 1 - d) if reverse else (row >= d)
        kinds.append(jnp.where(keep, power(jnp.full_like(row, float(d))), 0.0))
    kinds.append(power((SUBLANE - row) if reverse else (row + 1.0)))
    return jnp.stack(kinds, axis=1)


def _scan_rows(s_ref, sc_ref, carry_ref, n_rows, reverse):
    n_grp = n_rows // SUBLANE
    n_col = S5_STATE_LANES // LANE
    half = S5_STATE_LANES

    def step(i, carry):
        grp = (n_grp - 1 - i) if reverse else i
        r0 = pl.multiple_of(grp * SUBLANE, SUBLANE)
        out = []
        for cb in range(n_col):
            lo, hi = cb * LANE, half + cb * LANE
            re = s_ref[pl.ds(r0, SUBLANE), lo:lo + LANE]
            im = s_ref[pl.ds(r0, SUBLANE), hi:hi + LANE]
            for k, d in enumerate((1, 2, 4)):
                sh = (SUBLANE - d) if reverse else d
                rr, ri = pltpu.roll(re, sh, 0), pltpu.roll(im, sh, 0)
                ar, ai = sc_ref[k, :, lo:lo + LANE], sc_ref[k, :, hi:hi + LANE]
                re, im = re + (ar * rr - ai * ri), im + (ar * ri + ai * rr)
            pr, pi = sc_ref[3, :, lo:lo + LANE], sc_ref[3, :, hi:hi + LANE]
            cr, ci = carry[2 * cb], carry[2 * cb + 1]
            re, im = re + (pr * cr - pi * ci), im + (pr * ci + pi * cr)
            s_ref[pl.ds(r0, SUBLANE), lo:lo + LANE] = re
            s_ref[pl.ds(r0, SUBLANE), hi:hi + LANE] = im
            edge = 0 if reverse else SUBLANE - 1
            out.append(jnp.broadcast_to(re[edge:edge + 1, :], (SUBLANE, LANE)))
            out.append(jnp.broadcast_to(im[edge:edge + 1, :], (SUBLANE, LANE)))
        return tuple(out)

    init = []
    for cb in range(n_col):
        init.append(carry_ref[:, cb * LANE:(cb + 1) * LANE])
        init.append(carry_ref[:, half + cb * LANE:half + (cb + 1) * LANE])
    fin = lax.fori_loop(0, n_grp, step, tuple(init))
    for cb in range(n_col):
        carry_ref[:, cb * LANE:(cb + 1) * LANE] = fin[2 * cb]
        carry_ref[:, half + cb * LANE:half + (cb + 1) * LANE] = fin[2 * cb + 1]


def _s5_fwd(proj, wb, wc, d_skip, sc, *, name, exchanges=()):
    l = proj.shape[0]
    tt = _pick_tile(l, S5_TIME_CHUNK, SUBLANE)
    u_off = (proj.shape[1] - D_HALF) // LANE
    w2 = 2 * S5_STATE_LANES

    def body(u_ref, wb_ref, wc_ref, d_ref, sc_ref, s_ref, y_ref, carry_ref):
        @pl.when(pl.program_id(1) == 0)
        def _():
            carry_ref[...] = jnp.zeros_like(carry_ref)
        u = u_ref[...]
        s_ref[...] = _dot(u, wb_ref[0], _NN)
        _scan_rows(s_ref, sc_ref.at[0], carry_ref, tt, False)
        y_ref[...] = _dot(s_ref[...], wc_ref[0], _NN) + d_ref[...] * u

    return _call(
        body, name=name, grid=(S5_LANE_BLOCKS, l // tt),
        in_specs=[pl.BlockSpec((tt, LANE), lambda b, t: (t, b + u_off)),
                  pl.BlockSpec((1, LANE, w2), lambda b, t: (b, 0, 0)),
                  pl.BlockSpec((1, w2, LANE), lambda b, t: (b, 0, 0)),
                  pl.BlockSpec((1, LANE), lambda b, t: (0, b)),
                  pl.BlockSpec((1, 4, SUBLANE, w2), lambda b, t: (b, 0, 0, 0))],
        out_specs=[pl.BlockSpec((tt, w2), lambda b, t: (t, b)), pl.BlockSpec((tt, LANE), lambda b, t: (t, b))],
        out_shape=[jax.ShapeDtypeStruct((l, S5_LANE_BLOCKS * w2), F32), jax.ShapeDtypeStruct((l, D_HALF), F32)],
        scratch_shapes=[pltpu.VMEM((SUBLANE, w2), F32)],
        args=[proj, wb, wc, d_skip, sc], parallel=False, exchanges=exchanges)


def _s5_bwd(proj, dy, states, wb, wc, d_skip, sc_rev, *, name, exchanges=()):
    l = proj.shape[0]
    tt = _pick_tile(l, S5_TIME_CHUNK, SUBLANE)
    nt = l // tt
    u_off = (proj.shape[1] - D_HALF) // LANE
    w2 = 2 * S5_STATE_LANES
    half = S5_STATE_LANES
    grp_per_chunk = tt // SUBLANE

    def body(u_ref, dy_ref, s_ref, halo_ref, wb_ref, wc_ref, d_ref, sc_ref,
             du_ref, dwb_ref, dwc_ref, dlam_ref, dd_ref, g_scr, carry_ref):
        t = pl.program_id(1)

        @pl.when(t == 0)
        def _():
            carry_ref[...] = jnp.zeros_like(carry_ref)
            dwb_ref[...] = jnp.zeros_like(dwb_ref)
            dwc_ref[...] = jnp.zeros_like(dwc_ref)
            dlam_ref[...] = jnp.zeros_like(dlam_ref)
            dd_ref[...] = jnp.zeros_like(dd_ref)

        u = u_ref[...]
        dyv = dy_ref[...]
        g_scr[...] = _dot(dyv, wc_ref[0], _NT)
        _scan_rows(g_scr, sc_ref.at[0], carry_ref, tt, True)
        gv = g_scr[...]
        du_ref[...] = (_dot(gv, wb_ref[0], _NT) + d_ref[...] * dyv).astype(BF16)
        dwb_ref[0] += _dot(u, gv, _TN)
        sv = s_ref[...]
        dwc_ref[0] += _dot(sv, dyv, _TN)
        dd_ref[...] += jnp.sum(dyv * u, axis=0, keepdims=True)
        first_chunk = t == nt - 1
        halo = jnp.where(first_chunk, 0.0, halo_ref[SUBLANE - 1:SUBLANE, :])
        row = lax.broadcasted_iota(jnp.int32, sv.shape, 0)
        sp = jnp.where(row == 0, jnp.broadcast_to(halo, sv.shape), pltpu.roll(sv, 1, 0))
        gr, gi = gv[:, :half], gv[:, half:]
        sr, si = sp[:, :half], sp[:, half:]
        dlr = jnp.sum(gr * sr + gi * si, axis=0, keepdims=True)
        dli = jnp.sum(gi * sr - gr * si, axis=0, keepdims=True)
        dlam_ref[0] += jnp.concatenate([dlr, dli], axis=1)

    rev = lambda t: nt - 1 - t
    return _call(
        body, name=name, grid=(S5_LANE_BLOCKS, nt),
        in_specs=[pl.BlockSpec((tt, LANE), lambda b, t: (rev(t), b + u_off)),
                  pl.BlockSpec((tt, LANE), lambda b, t: (rev(t), b)),
                  pl.BlockSpec((tt, w2), lambda b, t: (rev(t), b)),
                  pl.BlockSpec((SUBLANE, w2), lambda b, t: (jnp.maximum(rev(t) * grp_per_chunk - 1, 0), b)),
                  pl.BlockSpec((1, LANE, w2), lambda b, t: (b, 0, 0)),
                  pl.BlockSpec((1, w2, LANE), lambda b, t: (b, 0, 0)),
                  pl.BlockSpec((1, LANE), lambda b, t: (0, b)),
                  pl.BlockSpec((1, 4, SUBLANE, w2), lambda b, t: (b, 0, 0, 0))],
        out_specs=[pl.BlockSpec((tt, LANE), lambda b, t: (rev(t), b)),
                   pl.BlockSpec((1, LANE, w2), lambda b, t: (b, 0, 0)),
                   pl.BlockSpec((1, w2, LANE), lambda b, t: (b, 0, 0)),
                   pl.BlockSpec((1, 1, w2), lambda b, t: (b, 0, 0)),
                   pl.BlockSpec((1, LANE), lambda b, t: (0, b))],
        out_shape=[jax.ShapeDtypeStruct((l, D_HALF), BF16),
                   jax.ShapeDtypeStruct((S5_LANE_BLOCKS, LANE, w2), F32),
                   jax.ShapeDtypeStruct((S5_LANE_BLOCKS, w2, LANE), F32),
                   jax.ShapeDtypeStruct((S5_LANE_BLOCKS, 1, w2), F32),
                   jax.ShapeDtypeStruct((1, D_HALF), F32)],
        scratch_shapes=[pltpu.VMEM((tt, w2), F32), pltpu.VMEM((SUBLANE, w2), F32)],
        args=[proj, dy, states, states, wb, wc, d_skip, sc_rev], parallel=False, exchanges=exchanges)


def _glu_fwd(ypre, wg, bg, cat, *, name):
    l, d = ypre.shape
    tl = _pick_tile(l, 512, SUBLANE)

    def body(y_ref, w_ref, b_ref, cat_ref, o_ref):
        yg = _gelu(y_ref[...])
        o_ref[...] = (yg * _sigmoid(_dot(yg, w_ref[...], _NN) + b_ref[...])).astype(BF16)

    row = pl.BlockSpec((tl, d), lambda i: (i, 0))
    return pl.pallas_call(
        body, name=name, grid=(l // tl,),
        in_specs=[row, pl.BlockSpec((d, d), lambda i: (0, 0)), pl.BlockSpec((1, d), lambda i: (0, 0)), _ANY],
        out_specs=pl.BlockSpec((tl, d), lambda i: (i, 1)), out_shape=jax.ShapeDtypeStruct(cat.shape, cat.dtype),
        input_output_aliases={3: 0}, compiler_params=_PAR)(ypre, wg, bg, cat)


def _glu_bwd(dcat, ypre, wg, bg, *, name):
    l, d = ypre.shape
    tl = _pick_tile(l, 512, SUBLANE)

    def body(dy_ref, y_ref, w_ref, b_ref, dyp_ref, dw_ref, db_ref):
        @pl.when(pl.program_id(0) == 0)
        def _():
            dw_ref[...] = jnp.zeros_like(dw_ref)
            db_ref[...] = jnp.zeros_like(db_ref)
        yp = y_ref[...]
        dyb = dy_ref[...]
        yg = _gelu(yp)
        sg = _sigmoid(_dot(yg, w_ref[...], _NN) + b_ref[...])
        dz = dyb * yg * sg * (1.0 - sg)
        dyg = dyb * sg + _dot(dz, w_ref[...], _NT)
        dyp_ref[...] = dyg * _gelu_grad(yp)
        dw_ref[...] += _dot(yg, dz, _TN)
        db_ref[...] += jnp.sum(dz, axis=0, keepdims=True)

    row = pl.BlockSpec((tl, d), lambda i: (i, 0))
    mat = pl.BlockSpec((d, d), lambda i: (0, 0))
    vec = pl.BlockSpec((1, d), lambda i: (0, 0))
    return pl.pallas_call(
        body, name=name, grid=(l // tl,),
        in_specs=[pl.BlockSpec((tl, d), lambda i: (i, 1)), row, mat, vec], out_specs=[row, mat, vec],
        out_shape=[jax.ShapeDtypeStruct((l, d), F32), jax.ShapeDtypeStruct((d, d), F32),
                   jax.ShapeDtypeStruct((1, d), F32)],
        compiler_params=_ARB)(dcat, ypre, wg, bg)


def _window_sum(x, w, trailing):
    s, d = x, 1
    while d < w:
        s = s + (_shift_dn(s, d) if trailing else _shift_up(s, d))
        d *= 2
    return s


def _window_count(shape, w):
    row = lax.broadcasted_iota(jnp.int32, shape, 0)
    return jnp.minimum(row + 1, w).astype(F32)


def _pool_fwd(proj, pw, scale, *, name):
    l = proj.shape[0]
    ng = len(POOL_WINDOWS)

    def body(z_ref, w_ref, sc_ref, o_ref):
        z = z_ref[...]
        for k, w in enumerate(POOL_WINDOWS):
            @pl.when(pl.program_id(0) == k)
            def _():
                pooled = _window_sum(z, w, True) / _window_count(z.shape, w) - z
                o_ref[...] = (_dot(pooled, w_ref[0], _NN) * sc_ref[...]).astype(BF16)

    col = pl.BlockSpec((l, LANE), lambda g: (0, g))
    return pl.pallas_call(
        body, name=name, grid=(ng,),
        in_specs=[col, pl.BlockSpec((1, LANE, LANE), lambda g: (g, 0, 0)), pl.BlockSpec((1, LANE), lambda g: (0, g))],
        out_specs=col, out_shape=jax.ShapeDtypeStruct((l, 2 * D_HALF), BF16), compiler_params=_PAR)(proj, pw, scale)


def _pool_bwd(proj, dcat, pw, scale, *, name):
    l = proj.shape[0]
    ng = len(POOL_WINDOWS)

    def body(z_ref, dy_ref, w_ref, sc_ref, dz_ref, dw_ref, dsc_ref):
        z = z_ref[...]
        dy = dy_ref[...]
        for k, w in enumerate(POOL_WINDOWS):
            @pl.when(pl.program_id(0) == k)
            def _():
                cnt = _window_count(z.shape, w)
                pooled = _window_sum(z, w, True) / cnt - z
                ypre = _dot(pooled, w_ref[0], _NN)
                dsc_ref[...] = jnp.sum(dy * ypre, axis=0, keepdims=True)
                dyp = dy * sc_ref[...]
                dw_ref[0] = _dot(pooled, dyp, _TN)
                dpool = _dot(dyp, w_ref[0], _NT)
                dz_ref[...] = (_window_sum(dpool / cnt, w, False) - dpool).astype(BF16)

    col = pl.BlockSpec((l, LANE), lambda g: (0, g))
    mat = pl.BlockSpec((1, LANE, LANE), lambda g: (g, 0, 0))
    vec = pl.BlockSpec((1, LANE), lambda g: (0, g))
    return pl.pallas_call(
        body, name=name, grid=(ng,), in_specs=[col, col, mat, vec], out_specs=[col, mat, vec],
        out_shape=[jax.ShapeDtypeStruct((l, D_HALF), BF16), jax.ShapeDtypeStruct((ng, LANE, LANE), F32),
                   jax.ShapeDtypeStruct((1, D_HALF), F32)],
        compiler_params=_PAR)(proj, dcat, pw, scale)


def _tril_mask():
    r = lax.broadcasted_iota(jnp.int32, (CHUNK, CHUNK), 0)
    c = lax.broadcasted_iota(jnp.int32, (CHUNK, CHUNK), 1)
    return r >= c


def _sgu_fwd(proj, ng, sw, sb_t, cat, *, name):
    l = proj.shape[0]
    tl = _pick_tile(l, 512, CHUNK)

    def body(su_ref, sv_ref, g_ref, w_ref, b_ref, cat_ref, o_ref):
        su = _gelu(su_ref[...])
        sv = _gelu(sv_ref[...])
        r = lax.rsqrt(jnp.mean(sv * sv, axis=-1, keepdims=True) + EPS)
        v = sv * r * g_ref[...]
        mask = _tril_mask()
        for h in range(SGU_HEADS):
            wm = jnp.where(mask, w_ref[h], 0.0)
            cs = slice(h * LANE, (h + 1) * LANE)
            for n in range(tl // CHUNK):
                rs = slice(n * CHUNK, (n + 1) * CHUNK)
                mixed = _dot(wm, v[rs, cs], _NN) + b_ref[:, h:h + 1]
                o_ref[rs, cs] = (su[rs, cs] * mixed).astype(BF16)

    blk = lambda c: pl.BlockSpec((tl, D_HALF), lambda i: (i, c))
    return pl.pallas_call(
        body, name=name, grid=(l // tl,),
        in_specs=[blk(1), blk(2), pl.BlockSpec((1, D_HALF), lambda i: (0, 0)),
                  pl.BlockSpec((SGU_HEADS, CHUNK, CHUNK), lambda i: (0, 0, 0)),
                  pl.BlockSpec((CHUNK, SGU_HEADS), lambda i: (0, 0)), _ANY],
        out_specs=blk(1), out_shape=jax.ShapeDtypeStruct(cat.shape, cat.dtype), input_output_aliases={5: 0},
        compiler_params=_PAR)(proj, proj, ng, sw, sb_t, cat)


def _sgu_bwd(proj, dcat, ng, sw, sb_t, *, name):
    l = proj.shape[0]
    tl = _pick_tile(l, 512, CHUNK)

    def body(su_ref, sv_ref, dy_ref, g_ref, w_ref, b_ref, dsu_ref, dsv_ref, dw_ref, dbm_ref, dng_ref, dv_scr):
        @pl.when(pl.program_id(0) == 0)
        def _():
            dw_ref[...] = jnp.zeros_like(dw_ref)
            dbm_ref[...] = jnp.zeros_like(dbm_ref)
            dng_ref[...] = jnp.zeros_like(dng_ref)
        su_pre = su_ref[...]
        sv_pre = sv_ref[...]
        su = _gelu(su_pre)
        sv = _gelu(sv_pre)
        gsu = _gelu_grad(su_pre)
        gv = g_ref[...]
        r = lax.rsqrt(jnp.mean(sv * sv, axis=-1, keepdims=True) + EPS)
        v = sv * r * gv
        dy = dy_ref[...]
        mask = _tril_mask()
        for h in range(SGU_HEADS):
            wm = jnp.where(mask, w_ref[h], 0.0)
            cs = slice(h * LANE, (h + 1) * LANE)
            dw_acc = jnp.zeros((CHUNK, CHUNK), F32)
            db_acc = jnp.zeros((CHUNK, LANE), F32)
            for n in range(tl // CHUNK):
                rs = slice(n * CHUNK, (n + 1) * CHUNK)
                vb = v[rs, cs]
                mixed = _dot(wm, vb, _NN) + b_ref[:, h:h + 1]
                dyb = dy[rs, cs]
                dsu_ref[rs, cs] = (dyb * mixed * gsu[rs, cs]).astype(BF16)
                dmix = dyb * su[rs, cs]
                db_acc = db_acc + dmix
                dw_acc = dw_acc + _dot(dmix, vb, _NT)
                dv_scr[rs, cs] = _dot(wm, dmix, _TN)
            dw_ref[h] += jnp.where(mask, dw_acc, 0.0)
            dbm_ref[h] += db_acc
        dv = dv_scr[...]
        a = dv * gv
        m = jnp.mean(a * sv, axis=-1, keepdims=True)
        dsv = r * a - sv * (r * r * r) * m
        dng_ref[...] += jnp.sum(dv * sv * r, axis=0, keepdims=True)
        dsv_ref[...] = (dsv * _gelu_grad(sv_pre)).astype(BF16)

    blk = lambda c: pl.BlockSpec((tl, D_HALF), lambda i: (i, c))
    mats = pl.BlockSpec((SGU_HEADS, CHUNK, CHUNK), lambda i: (0, 0, 0))
    vec = pl.BlockSpec((1, D_HALF), lambda i: (0, 0))
    piece = jax.ShapeDtypeStruct((l, D_HALF), BF16)
    mshape = jax.ShapeDtypeStruct((SGU_HEADS, CHUNK, CHUNK), F32)
    return pl.pallas_call(
        body, name=name, grid=(l // tl,),
        in_specs=[blk(1), blk(2), blk(1), vec, mats, pl.BlockSpec((CHUNK, SGU_HEADS), lambda i: (0, 0))],
        out_specs=[blk(0), blk(0), mats, mats, vec],
        out_shape=[piece, piece, mshape, mshape, jax.ShapeDtypeStruct((1, D_HALF), F32)],
        scratch_shapes=[pltpu.VMEM((tl, D_HALF), F32)],
        compiler_params=_ARB)(proj, proj, dcat, ng, sw, sb_t)


def _s5_params(w):
    prep_args = (w["ssm_log_step"], w["ssm_a_re"], w["ssm_a_im"], w["ssm_b_re"], w["ssm_b_im"])
    (lr, li, br, bi), prep_vjp = jax.vjp(jax.vmap(_s5_prep), *prep_args)
    wb, wc = jax.vmap(_s5_mats)(br, bi, w["ssm_c_re"], w["ssm_c_im"])
    consts = lambda reverse: jax.vmap(functools.partial(_s5_scan_consts, reverse=reverse))(*prep_args[:3])
    return dict(wb=wb.astype(BF16), wc=wc.astype(BF16), d=w["ssm_d"][:, None, :], sc=consts(False),
                sc_rev=consts(True), prep_vjp=prep_vjp)


def _s5_param_grads(s5, dwb, dwc, dlam, dd):
    dbr, dbi, dcr, dci = jax.vmap(_s5_mats_bwd)(dwb, dwc)
    n = dlam.shape[0]
    dlr = dlam[:, :, 0, :S5_STATE_LANES].reshape(n, N_SSM_GROUPS, SSM_STATE)
    dli = dlam[:, :, 0, S5_STATE_LANES:].reshape(n, N_SSM_GROUPS, SSM_STATE)
    dls, dar, dai, db_re, db_im = s5["prep_vjp"]((dlr, dli, dbr, dbi))
    return dict(ssm_log_step=dls, ssm_a_re=dar, ssm_a_im=dai, ssm_b_re=db_re, ssm_b_im=db_im, ssm_c_re=dcr,
                ssm_c_im=dci, ssm_d=dd[:, 0, :])


TILES = {
    "mm_up": (4096, 512), "mm_up_dw": (512, 1408), "mm_down_dx": (1024, 2816),
    "mm_down": (512, 1024), "mm_down_dw": (256, 1024), "mm_even_in": (2048, 1024), "mm_odd_in": (2048, 768),
    "mm_mix_out": (1024, 1024), "mm_mix_out_dx": (2048, 1024), "mm_mix_out_dw": (1024, 512),
    "mm_even_in_dw": (1024, 512), "mm_odd_in_dw": (1024, 512),
}


def _layer_weights(i):
    j = i // 2
    mixer = [("even_w_in", j), ("even_w_out", j), ("ssm_glu_w", j)] if i % 2 == 0 else [("odd_w_in", j),
                                                                                         ("odd_w_out", j)]
    return dict(w_in=mixer[0], w_out=mixer[1], glu=mixer[2:], up=("ffn_w_up", i), down=("ffn_w_down", i))


class _LocalWeights:
    def __init__(self, w):
        self.w, self.grads = w, {}

    def carried_by(self, stage, i):
        return []

    def delivered(self, stage, i, outs):
        pass

    def weight(self, key):
        return self.w[key[0]][key[1]]

    def grad(self, key, dw):
        self.grads[key] = dw


class _ShardedWeights:
    def __init__(self, shards):
        self.shards = shards
        self.full, self.pending, self.scattered = {}, {}, {}

    def start(self, others):
        keys = [_layer_weights(0)["w_in"]]
        outs = _exchange_only(others + [self._gather(k) for k in keys], name="ag_first")
        self._take(keys, outs[len(others):])
        return outs[:len(others)]

    def _gather(self, key):
        return _Gather(self.shards[key[0]][key[1]])

    def _take(self, keys, outs):
        for key, got in zip(keys, outs):
            if BIG[key[0]] == 2:
                self.full[key] = jnp.swapaxes(got, 0, 1).reshape(got.shape[1], -1)
            else:
                self.full[key] = got.reshape(-1, got.shape[2])

    def _plan(self, stage, i):
        cur = _layer_weights(i)
        nxt = _layer_weights(i + 1) if i + 1 < DEPTH else None
        has_scan = lambda k: k % 2 == 0
        none = ([], [])
        return {
            "mm_in": ([cur["w_out"], *cur["glu"]], []) if i == 0 else none,
            "s5_fwd": ([cur["up"]], []),
            "mm_up": ([cur["down"]] + ([nxt["w_in"], nxt["w_out"], *nxt["glu"]] if nxt else []), []),
            "ffn_act": ([nxt["up"]], []) if nxt and not has_scan(i + 1) else none,
            "ffn_act_bwd": ([], [cur["down"]] + ([nxt["up"]] if nxt and not has_scan(i + 1) else [])),
            "s5_bwd": ([], [cur["up"]]),
            "mm_up_dx": ([], [nxt["w_in"], *nxt["glu"]]) if nxt else none,
            "mm_up_dw": ([], [nxt["w_out"]]) if nxt else none,
            "mm_in_dx": ([], [cur["w_out"]]) if i == 0 else none,
            "mm_in_dw": ([], list(cur["glu"])) if i == 0 else none,
        }[stage]

    def _scatter(self, key):
        name, layer = key
        src = self.pending.pop(key)
        layers, rows, cols = self.shards[name].shape
        if name not in self.scattered:
            self.scattered[name] = lax.empty((N_DEV, layers * rows, cols), src.dtype)
        return _Scatter(src, into=self.scattered[name], row0=layer * rows)

    def carried_by(self, stage, i):
        gather, scatter = self._plan(stage, i)
        return [self._gather(k) for k in gather] + [self._scatter(k) for k in scatter]

    def delivered(self, stage, i, outs):
        gather, scatter = self._plan(stage, i)
        self._take(gather, outs[:len(gather)])
        for (name, _), buf in zip(scatter, outs[len(gather):]):
            self.scattered[name] = buf

    def weight(self, key):
        return self.full[key]

    def grad(self, key, dw):
        self.pending[key] = _to_dest_major(dw, BIG[key[0]] - 1).astype(BF16)

    def finish(self, carrier, others):
        keys = list(self.pending)
        res, outs = carrier(others + [self._scatter(k) for k in keys])
        for (name, _), buf in zip(keys, outs[len(others):]):
            self.scattered[name] = buf
        return res, outs[:len(others)]


def _device_step(x, tgt, w, comm):
    saved = []
    s5 = _s5_params(w)
    h = _rms_fwd(x, w["norm_mix_g"][0:1], name="rms_fwd")
    for i in range(DEPTH):
        j = i // 2
        lw = _layer_weights(i)
        if i % 2 == 0:
            proj = _carry(comm, "mm_in", i, _mm, h, comm.weight(lw["w_in"]), "nn", name="mm_even_in")
            ya = _sconv_fwd(proj, w["even_conv_w"][j], name="sconv_fwd")
            states, ypre = _carry(comm, "s5_fwd", i, _s5_fwd, proj, s5["wb"][j], s5["wc"][j], s5["d"][j],
                                  s5["sc"][j], name="s5_fwd")
            cat = _glu_fwd(ypre, comm.weight(lw["glu"][0]), w["ssm_glu_b"][j][None, :], ya, name="glu_fwd")
            mix = (states, ypre)
        else:
            proj = _carry(comm, "mm_in", i, _mm, h, comm.weight(lw["w_in"]), "nn", name="mm_odd_in")
            yc = _pool_fwd(proj, w["pool_w"][j], w["pool_scale"][j][None, :], name="pool_fwd")
            sb_t = jnp.transpose(w["sgu_b"][j])
            cat = _sgu_fwd(proj, w["sgu_norm_g"][j][None, :], w["sgu_w"][j], sb_t, yc, name="sgu_fwd")
            mix = (sb_t,)
        x1, h2 = _mm(cat, comm.weight(lw["w_out"]), "nn", add=x, norm=(w["norm_ffn_g"], i), name="mm_mix_out")
        up = _carry(comm, "mm_up", i, _mm, h2, comm.weight(lw["up"]), "nn", name="mm_up")
        act = _carry(comm, "ffn_act", i, _ffn_act, up, w["ffn_conv_w"][i], w["ffn_conv_b"][i:i + 1], name="ffn_act")
        if i + 1 < DEPTH:
            x2, h_next = _mm(act, comm.weight(lw["down"]), "nn", add=x1, norm=(w["norm_mix_g"], i + 1),
                             name="mm_down")
        else:
            x2, h_next = _mm(act, comm.weight(lw["down"]), "nn", add=x1, name="mm_down_last"), None
        saved.append((x, h, proj, cat, x1, h2, up, act, mix))
        x, h = x2, h_next

    loss, dx, dxb, dgf = _loss_head(x, w["norm_final_g"][None, :], tgt, name="loss_head")
    per_layer = {}
    s5_grads = []

    def put(name, idx, val):
        per_layer.setdefault(name, {})[idx] = val

    for i in reversed(range(DEPTH)):
        j = i // 2
        lw = _layer_weights(i)
        x0, h, proj, cat, x1, h2, up, act, mix = saved[i]
        dact = _mm(dxb, comm.weight(lw["down"]), "nt", name="mm_down_dx")
        comm.grad(lw["down"], _mm(act, dxb, "tn", out_dtype=BF16, name="mm_down_dw"))
        dup, dcw, dcb = _carry(comm, "ffn_act_bwd", i, _ffn_act_bwd, up, dact, w["ffn_conv_w"][i],
                               w["ffn_conv_b"][i:i + 1], name="ffn_act_bwd")
        put("ffn_conv_w", i, dcw)
        put("ffn_conv_b", i, dcb[0])
        dx1, dx1b, dg2 = _carry(comm, "mm_up_dx", i, _mm, dup, comm.weight(lw["up"]), "nt", tm_cap=256, tn_cap=D_MODEL,
                                norm_bwd=(x1, w["norm_ffn_g"], i, dx), name="mm_up_dx")
        comm.grad(lw["up"], _carry(comm, "mm_up_dw", i, _mm, h2, dup, "tn", out_dtype=BF16,
                                   name="mm_up_dw"))
        put("norm_ffn_g", i, dg2[0])
        dcat = _mm(dx1b, comm.weight(lw["w_out"]), "nt", name="mm_mix_out_dx")
        comm.grad(lw["w_out"], _mm(cat, dx1b, "tn", out_dtype=BF16, name="mm_mix_out_dw"))
        if i % 2 == 0:
            states, ypre = mix
            dxa, dba, dca, dcw_a = _sconv_bwd(proj, dcat, w["even_conv_w"][j], name="sconv_bwd")
            put("even_conv_w", j, dcw_a)
            dypre, dwg, dbg = _glu_bwd(dcat, ypre, comm.weight(lw["glu"][0]), w["ssm_glu_b"][j][None, :],
                                       name="glu_bwd")
            comm.grad(lw["glu"][0], dwg)
            put("ssm_glu_b", j, dbg[0])
            du, dwb, dwc, dlam, dd = _carry(comm, "s5_bwd", i, _s5_bwd, proj, dypre, states, s5["wb"][j], s5["wc"][j],
                                            s5["d"][j], s5["sc_rev"][j], name="s5_bwd")
            s5_grads.insert(0, (dwb, dwc, dlam, dd))
            dproj = jnp.concatenate([dxa, dba, dca, du], axis=1)
            in_name = "mm_even_in"
        else:
            (sb_t,) = mix
            dz, dpw, dps = _pool_bwd(proj, dcat, w["pool_w"][j], w["pool_scale"][j][None, :], name="pool_bwd")
            put("pool_w", j, dpw)
            put("pool_scale", j, dps[0])
            dsu, dsv, dsw, dbm, dng = _sgu_bwd(proj, dcat, w["sgu_norm_g"][j][None, :], w["sgu_w"][j], sb_t,
                                               name="sgu_bwd")
            put("sgu_w", j, dsw)
            put("sgu_b", j, jnp.sum(dbm, axis=-1))
            put("sgu_norm_g", j, dng[0])
            dproj = jnp.concatenate([dz, dsu, dsv], axis=1)
            in_name = "mm_odd_in"
        dx, dxb, dg1 = _carry(comm, "mm_in_dx", i, _mm, dproj, comm.weight(lw["w_in"]), "nt", tn_cap=D_MODEL,
                              norm_bwd=(x0, w["norm_mix_g"], i, dx1), name=in_name + "_dx")
        comm.grad(lw["w_in"], _carry(comm, "mm_in_dw", i, _mm, h, dproj, "tn", out_dtype=BF16, name=in_name + "_dw"))
        put("norm_mix_g", i, dg1[0])

    grads = {nm: jnp.stack([vals[k] for k in sorted(vals)]) for nm, vals in per_layer.items()}
    grads["norm_final_g"] = dgf[0]
    grads.update(_s5_param_grads(s5, *[jnp.stack(parts) for parts in zip(*s5_grads)]))
    return loss, dx, grads


def _carry(comm, stage, i, fn, *args, **kwargs):
    exchanges = comm.carried_by(stage, i)
    if not exchanges:
        return fn(*args, **kwargs)
    out, moved = fn(*args, exchanges=exchanges, **kwargs)
    comm.delivered(stage, i, moved)
    return out


def _sum_parts(parts, *, name):
    g, r, c = parts.shape
    tr = _pick_tile(r, max(16, EXCHANGE_BLOCK_ELEMS // c), 16)

    def body(p_ref, o_ref):
        acc = p_ref[0].astype(F32)
        for k in range(1, g):
            acc = acc + p_ref[k].astype(F32)
        o_ref[...] = acc

    return pl.pallas_call(
        body, name=name, grid=(r // tr,), in_specs=[pl.BlockSpec((g, tr, c), lambda i: (0, i, 0))],
        out_specs=pl.BlockSpec((tr, c), lambda i: (i, 0)), out_shape=jax.ShapeDtypeStruct((r, c), F32),
        compiler_params=_PAR)(parts)


def _adamw(w, m, v, g_parts, *, name, exchanges=()):
    r, c = w.shape
    g = g_parts.shape[0]
    tc = _pick_tile(c, 8192, LANE)
    tr = _pick_tile(r, max(16, (1 << 18) // tc), 16)
    c1 = 1.0 - ADAM_B1 ** ADAM_STEP
    c2 = 1.0 - ADAM_B2 ** ADAM_STEP

    def body(w_ref, m_ref, v_ref, g_ref, go_ref, d_ref, mo_ref, vo_ref):
        grad = g_ref[0].astype(F32)
        for k in range(1, g):
            grad = grad + g_ref[k].astype(F32)
        m_new = ADAM_B1 * m_ref[...] + (1.0 - ADAM_B1) * grad
        v_new = ADAM_B2 * v_ref[...] + (1.0 - ADAM_B2) * (grad * grad)
        go_ref[...] = grad
        mo_ref[...] = m_new
        vo_ref[...] = v_new
        d_ref[...] = -ADAM_LR * ((m_new / c1) / (jnp.sqrt(v_new / c2) + ADAM_EPS) + ADAM_WD * w_ref[...])

    blk = pl.BlockSpec((tr, tc), lambda i, j: (i, j))
    out = jax.ShapeDtypeStruct((r, c), F32)
    return _call(body, name=name, grid=(r // tr, c // tc),
                 in_specs=[blk, blk, blk, pl.BlockSpec((g, tr, tc), lambda i, j: (0, i, j))],
                 out_specs=[blk, blk, blk, blk], out_shape=[out, out, out, out], args=[w, m, v, g_parts],
                 exchanges=exchanges)


WEIGHT_NAMES = ['norm_mix_g', 'even_w_in', 'even_conv_w', 'ssm_log_step', 'ssm_a_re', 'ssm_a_im', 'ssm_b_re',
                'ssm_b_im', 'ssm_c_re', 'ssm_c_im', 'ssm_d', 'ssm_glu_w', 'ssm_glu_b', 'even_w_out', 'odd_w_in',
                'pool_w', 'pool_scale', 'sgu_norm_g', 'sgu_w', 'sgu_b', 'odd_w_out', 'norm_ffn_g', 'ffn_w_up',
                'ffn_conv_w', 'ffn_conv_b', 'ffn_w_down', 'norm_final_g']
BIG = {'even_w_in': 2, 'ssm_glu_w': 1, 'even_w_out': 1, 'odd_w_in': 2, 'odd_w_out': 1, 'ffn_w_up': 2,
       'ffn_w_down': 1}
SMALL = {'even_conv_w': 2, 'pool_scale': 1, 'sgu_norm_g': 1, 'ffn_conv_w': 2}
BIG_ROWS = 512
SMALL_ROWS = 16


def _pad_to(n, q):
    return -(-n // q) * q


def _pack(arrays, dtype, rows, lead=()):
    flat = [a.reshape(lead + (-1,)).astype(dtype) for a in arrays]
    n = sum(f.shape[-1] for f in flat)
    pad = _pad_to(n, rows * LANE) - n
    if pad:
        flat.append(jnp.zeros(lead + (pad,), dtype))
    return jnp.concatenate(flat, axis=-1).reshape(lead + (rows, -1))


def _unpack(buf, shapes, lead=()):
    flat = buf.reshape(lead + (-1,))
    out, off = [], 0
    for shp in shapes:
        n = math.prod(shp)
        out.append(flat[..., off:off + n].reshape(lead + tuple(shp)))
        off += n
    return out


def _to_dest_major(full, axis):
    shp = full.shape
    split = full.reshape(shp[:axis] + (N_DEV, shp[axis] // N_DEV) + shp[axis + 1:])
    return jnp.moveaxis(split, axis, 0)


def _from_dest_major(blocks, axis):
    moved = jnp.moveaxis(blocks, 0, axis)
    shp = moved.shape
    return moved.reshape(shp[:axis] + (shp[axis] * shp[axis + 1],) + shp[axis + 2:])


def _rows_2d(a):
    return a.reshape(-1, a.shape[-1])


def _rows_2d_lead(a):
    return a.reshape(a.shape[0], -1, a.shape[-1])


def kernel(x, norm_mix_g, even_w_in, even_conv_w, ssm_log_step, ssm_a_re, ssm_a_im, ssm_b_re, ssm_b_im, ssm_c_re, ssm_c_im, ssm_d, ssm_glu_w, ssm_glu_b, even_w_out, odd_w_in, pool_w, pool_scale, sgu_norm_g, sgu_w, sgu_b, odd_w_out, norm_ffn_g, ffn_w_up, ffn_conv_w, ffn_conv_b, ffn_w_down, norm_final_g, loss_target, m_norm_mix_g, m_even_w_in, m_even_conv_w, m_ssm_log_step, m_ssm_a_re, m_ssm_a_im, m_ssm_b_re, m_ssm_b_im, m_ssm_c_re, m_ssm_c_im, m_ssm_d, m_ssm_glu_w, m_ssm_glu_b, m_even_w_out, m_odd_w_in, m_pool_w, m_pool_scale, m_sgu_norm_g, m_sgu_w, m_sgu_b, m_odd_w_out, m_norm_ffn_g, m_ffn_w_up, m_ffn_conv_w, m_ffn_conv_b, m_ffn_w_down, m_norm_final_g, v_norm_mix_g, v_even_w_in, v_even_conv_w, v_ssm_log_step, v_ssm_a_re, v_ssm_a_im, v_ssm_b_re, v_ssm_b_im, v_ssm_c_re, v_ssm_c_im, v_ssm_d, v_ssm_glu_w, v_ssm_glu_b, v_even_w_out, v_odd_w_in, v_pool_w, v_pool_scale, v_sgu_norm_g, v_sgu_w, v_sgu_b, v_odd_w_out, v_norm_ffn_g, v_ffn_w_up, v_ffn_conv_w, v_ffn_conv_b, v_ffn_w_down, v_norm_final_g):
    given = dict(locals())
    wts = {n: given[n] for n in WEIGHT_NAMES}
    mom = {n: given["m_" + n] for n in WEIGHT_NAMES}
    var = {n: given["v_" + n] for n in WEIGHT_NAMES}
    repl = [n for n in WEIGHT_NAMES if n not in BIG and n not in SMALL]

    small_shapes = [wts[n].shape for n in SMALL]
    comm = _ShardedWeights({n: wts[n].astype(BF16) for n in BIG})
    (small_all,) = comm.start([_Gather(_pack([wts[n] for n in SMALL], F32, SMALL_ROWS))])
    full = {n: wts[n] for n in repl}
    for n, blocks in zip(SMALL, _unpack(small_all, small_shapes, lead=(N_DEV,))):
        full[n] = _from_dest_major(blocks, SMALL[n])

    loss, dx, grads = _device_step(x[0], loss_target[0], full, comm)

    repl_shapes = [wts[n].shape for n in repl]
    repl_flat = jnp.concatenate([grads[n].reshape(-1) for n in repl] + [loss.reshape(-1)])
    n_repl = repl_flat.shape[0]
    chunk = _pad_to(-(-n_repl // N_DEV), SMALL_ROWS * LANE)
    repl_flat = jnp.pad(repl_flat, (0, N_DEV * chunk - n_repl))
    small_part = _pack([_to_dest_major(grads[n], SMALL[n]) for n in SMALL], F32, SMALL_ROWS, lead=(N_DEV,))
    small_cols = small_part.shape[2]
    small_scatter = _Scatter(
        jnp.concatenate([small_part, repl_flat.reshape(N_DEV, SMALL_ROWS, chunk // SMALL_ROWS)], axis=2))

    out = {}

    def adamw_big(n, exchanges=()):
        return _adamw(_rows_2d(wts[n]), _rows_2d(mom[n]), _rows_2d(var[n]), comm.scattered[n], name="adamw_" + n,
                      exchanges=exchanges)

    order = ["ffn_w_up", "ffn_w_down"] + [n for n in BIG if n not in ("ffn_w_up", "ffn_w_down")]
    for n in order:
        if n == "ffn_w_up":
            res, (small_rs,) = comm.finish(functools.partial(adamw_big, n), [small_scatter])
            small_sum = _sum_parts(small_rs, name="rs_sum_small")
        elif n == "ffn_w_down":
            res, (repl_all,) = adamw_big(n, [_Gather(small_sum[:, small_cols:])])
            repl_sum = repl_all.reshape(-1)
            total_loss = repl_sum[n_repl - 1]
        else:
            res = adamw_big(n)
        out[n] = [r.reshape(wts[n].shape) for r in res]

    def small_vec(shard_part, repl_part):
        flat = jnp.concatenate([shard_part.reshape(-1), repl_part])
        return flat.reshape(SMALL_ROWS, -1)

    def small_tree(tree):
        tail = jnp.concatenate([tree[n].reshape(-1) for n in repl])
        tail = jnp.pad(tail, (0, N_DEV * chunk - tail.shape[0]))
        return small_vec(_pack([tree[n] for n in SMALL], F32, SMALL_ROWS), tail)

    res = _adamw(small_tree(wts), small_tree(mom), small_tree(var), small_vec(small_sum[:, :small_cols], repl_sum)[None],
                 name="adamw_small")
    n_small = SMALL_ROWS * small_cols
    for k, r in enumerate(res):
        flat = r.reshape(-1)
        shard = _unpack(flat[:n_small], small_shapes)
        rest = _unpack(flat[n_small:], repl_shapes)
        for n, val in zip(SMALL, shard):
            out.setdefault(n, [None] * 4)[k] = val
        for n, val in zip(repl, rest):
            out.setdefault(n, [None] * 4)[k] = val

    grad_x = dx[None]
    return (total_loss, grad_x, *[out[n][0] for n in WEIGHT_NAMES], *[out[n][1] for n in WEIGHT_NAMES],
            *[out[n][2] for n in WEIGHT_NAMES], *[out[n][3] for n in WEIGHT_NAMES])
```

```python
import functools
import math

import jax
import jax.numpy as jnp
from jax import lax
from jax.experimental import pallas as pl
from jax.experimental.pallas import tpu as pltpu

F32 = jnp.float32
BF16 = jnp.bfloat16

D_MODEL = 1024
DEPTH = 4
D_HALF = D_MODEL // 2
SSM_GROUP = 16
N_SSM_GROUPS = D_HALF // SSM_GROUP
SSM_STATE = 64
POOL_WINDOWS = (2, 4, 8, 16)
SGU_HEADS = 4
CHUNK = 128
D_FF = 2816
CONV_WIDTH = 3
EPS = 1e-6
N_DEV = 8

ADAM_LR = 0.001
ADAM_B1 = 0.9
ADAM_B2 = 0.999
ADAM_EPS = 1e-08
ADAM_WD = 0.01
ADAM_STEP = 10

LANE = 128
SUBLANE = 8
S5_LANE_BLOCKS = D_HALF // LANE
S5_STATE_LANES = (N_SSM_GROUPS // S5_LANE_BLOCKS) * SSM_STATE
S5_TIME_CHUNK = 512
EXCHANGE_BLOCK_ELEMS = 1 << 20

GELU_K = math.sqrt(2.0 / math.pi)
GELU_C = 0.044715

_ARB = pltpu.CompilerParams(dimension_semantics=("arbitrary",))
_ARB2 = pltpu.CompilerParams(dimension_semantics=("arbitrary", "arbitrary"))
_PAR = pltpu.CompilerParams(dimension_semantics=("parallel",))
_PAR2 = pltpu.CompilerParams(dimension_semantics=("parallel", "parallel"))


def _pick_tile(n, cap, mult):
    if n <= cap:
        return n
    best = None
    for t in range(mult, cap + 1, mult):
        if n % t == 0:
            best = t
    assert best is not None, (n, cap, mult)
    return best


_MESH = pl.DeviceIdType.MESH
_ANY = pl.BlockSpec(memory_space=pl.ANY)
SEMS_PER_EXCHANGE = N_DEV - 1


class _Gather:
    into = None

    def __init__(self, src):
        self.src = src
        self.out_shape = jax.ShapeDtypeStruct((N_DEV,) + src.shape, src.dtype)

    def copies(self, x_ref, out_ref, send_sems, recv_sems, local_sem):
        x, y, cc = lax.axis_index("x"), lax.axis_index("y"), lax.axis_index("c")
        me, sibling = (x, y, cc), (x, y, 1 - cc)
        chips = [(1 - x, y), (x, 1 - y), (1 - x, 1 - y)]

        def rows(px, py, pc):
            return out_ref.at[4 * px + 2 * py + pc]

        def copy(k, block, to, src=None):
            return pltpu.make_async_remote_copy(
                src_ref=rows(*block) if src is None else src, dst_ref=rows(*block),
                send_sem=send_sems.at[k], recv_sem=recv_sems.at[k], device_id=to, device_id_type=_MESH)

        return dict(
            mine=pltpu.make_async_copy(x_ref, rows(*me), local_sem),
            first=[copy(0, me, sibling, src=x_ref)] + [copy(1 + k, me, (*chip, cc), src=x_ref)
                                                       for k, chip in enumerate(chips)],
            passed=[copy(4 + k, (*chip, cc), sibling) for k, chip in enumerate(chips)],
            over_ici=[copy(1 + k, (*chip, cc), me) for k, chip in enumerate(chips)],
            from_sibling=[copy(0, sibling, me)] + [copy(4 + k, (*chip, 1 - cc), me) for k, chip in enumerate(chips)])

    def start(self, *refs):
        cps = self.copies(*refs)
        cps["mine"].start()
        for cp in cps["first"]:
            cp.start()

    def finish(self, *refs):
        cps = self.copies(*refs)
        for arrived, onward in zip(cps["over_ici"], cps["passed"]):
            arrived.wait_recv()
            onward.start()
        for arrived in cps["from_sibling"]:
            arrived.wait_recv()
        for cp in cps["first"] + cps["passed"]:
            cp.wait_send()
        cps["mine"].wait()


class _Scatter:
    def __init__(self, src, into=None, row0=0):
        self.src, self.into, self.row0 = src, into, row0
        whole = src if into is None else into
        self.out_shape = jax.ShapeDtypeStruct(whole.shape, whole.dtype)

    def copies(self, p_ref, whole_ref, send_sems, recv_sems, local_sem):
        x, y, cc = lax.axis_index("x"), lax.axis_index("y"), lax.axis_index("c")
        me = 4 * x + 2 * y + cc
        rows = self.src.shape[1]
        out_ref = whole_ref if self.into is None else whole_ref.at[:, pl.ds(self.row0, rows)]
        sends, arrivals = [], []
        for k in range(1, N_DEV):
            px = (1 - x) if k & 4 else x
            py = (1 - y) if k & 2 else y
            pc = (1 - cc) if k & 1 else cc
            peer = 4 * px + 2 * py + pc
            kw = dict(send_sem=send_sems.at[k - 1], recv_sem=recv_sems.at[k - 1], device_id=(px, py, pc),
                      device_id_type=_MESH)
            sends.append(pltpu.make_async_remote_copy(src_ref=p_ref.at[peer], dst_ref=out_ref.at[me], **kw))
            arrivals.append(pltpu.make_async_remote_copy(src_ref=p_ref.at[me], dst_ref=out_ref.at[peer], **kw))
        return dict(mine=pltpu.make_async_copy(p_ref.at[me], out_ref.at[me], local_sem), sends=sends,
                    arrivals=arrivals)

    def start(self, *refs):
        cps = self.copies(*refs)
        cps["mine"].start()
        for cp in cps["sends"]:
            cp.start()

    def finish(self, *refs):
        cps = self.copies(*refs)
        for cp in cps["arrivals"]:
            cp.wait_recv()
        for cp in cps["sends"]:
            cp.wait_send()
        cps["mine"].wait()


class _SemView:
    def __init__(self, ref, lo):
        self.ref, self.lo = ref, lo

    @property
    def at(self):
        return self

    def __getitem__(self, k):
        return self.ref.at[self.lo + k]


def _call(body, *, name, grid, in_specs, out_specs, out_shape, args, scratch_shapes=(), parallel=True, exchanges=()):
    n_axes = len(grid)
    if not exchanges:
        sem = ("parallel" if parallel else "arbitrary",) * n_axes
        return pl.pallas_call(
            body, name=name, grid=grid, in_specs=in_specs, out_specs=out_specs, out_shape=out_shape,
            scratch_shapes=scratch_shapes, compiler_params=pltpu.CompilerParams(dimension_semantics=sem))(*args)
    single = not isinstance(out_shape, (list, tuple))
    out_specs = [out_specs] if single else list(out_specs)
    out_shape = [out_shape] if single else list(out_shape)
    n_in, n_out, n_scr, n_x = len(in_specs), len(out_specs), len(scratch_shapes), len(exchanges)
    landing = [(e, ex.into) for e, ex in enumerate(exchanges) if ex.into is not None]
    aliases = {n_in + n_x + pos: n_out + e for pos, (e, _) in enumerate(landing)}

    def wrapped(*refs):
        ins, refs = refs[:n_in], refs[n_in:]
        x_in, refs = refs[:n_x], refs[n_x + len(landing):]
        outs, refs = refs[:n_out], refs[n_out:]
        x_out, refs = refs[:n_x], refs[n_x:]
        scr, (send_sems, recv_sems, local_sems) = refs[:n_scr], refs[n_scr:]
        ids = [pl.program_id(k) for k in range(n_axes)]
        first = functools.reduce(jnp.logical_and, [i == 0 for i in ids])
        last = functools.reduce(jnp.logical_and, [i == g - 1 for i, g in zip(ids, grid)])

        def sems(e):
            lo = e * SEMS_PER_EXCHANGE
            return _SemView(send_sems, lo), _SemView(recv_sems, lo), local_sems.at[e]

        @pl.when(first)
        def _():
            for e, ex in enumerate(exchanges):
                ex.start(x_in[e], x_out[e], *sems(e))

        body(*ins, *outs, *scr)

        @pl.when(last)
        def _():
            for e, ex in enumerate(exchanges):
                ex.finish(x_in[e], x_out[e], *sems(e))

    res = pl.pallas_call(
        wrapped, name=name, grid=grid, in_specs=list(in_specs) + [_ANY] * (n_x + len(landing)),
        out_specs=out_specs + [_ANY] * n_x, out_shape=out_shape + [ex.out_shape for ex in exchanges],
        input_output_aliases=aliases,
        scratch_shapes=list(scratch_shapes) + [pltpu.SemaphoreType.DMA((n_x * SEMS_PER_EXCHANGE,)),
                                               pltpu.SemaphoreType.DMA((n_x * SEMS_PER_EXCHANGE,)),
                                               pltpu.SemaphoreType.DMA((n_x,))],
        compiler_params=pltpu.CompilerParams(dimension_semantics=("arbitrary",) * n_axes),
    )(*args, *[ex.src for ex in exchanges], *[buf for _, buf in landing])
    outs, x_outs = res[:n_out], res[n_out:]
    return (outs[0] if single else outs), x_outs


def _exchange_only(exchanges, *, name):
    def body():
        pass

    return _call(body, name=name, grid=(1,), in_specs=[], out_specs=[], out_shape=[], args=[],
                 exchanges=exchanges)[1]


def _shift_dn(x, d):
    rolled = pltpu.roll(x, d, 0)
    if x.shape[0] <= SUBLANE or d >= SUBLANE:
        row = lax.broadcasted_iota(jnp.int32, x.shape, 0)
        return jnp.where(row >= d, rolled, 0.0)
    row = lax.broadcasted_iota(jnp.int32, (SUBLANE, x.shape[1]), 0)
    return jnp.concatenate([jnp.where(row >= d, rolled[:SUBLANE], 0.0), rolled[SUBLANE:]], axis=0)


def _shift_up(x, d):
    n = x.shape[0]
    rolled = pltpu.roll(x, n - d, 0)
    if n <= SUBLANE or d >= SUBLANE:
        row = lax.broadcasted_iota(jnp.int32, x.shape, 0)
        return jnp.where(row < n - d, rolled, 0.0)
    row = lax.broadcasted_iota(jnp.int32, (SUBLANE, x.shape[1]), 0)
    return jnp.concatenate([rolled[:n - SUBLANE], jnp.where(row < SUBLANE - d, rolled[n - SUBLANE:], 0.0)], axis=0)


def _gelu(x):
    return 0.5 * x * (1.0 + jnp.tanh(GELU_K * (x + GELU_C * x * x * x)))


def _gelu_grad(x):
    t = jnp.tanh(GELU_K * (x + GELU_C * x * x * x))
    return 0.5 * (1.0 + t) + 0.5 * x * (1.0 - t * t) * (GELU_K * (1.0 + 3.0 * GELU_C * x * x))


def _sigmoid(x):
    return 1.0 / (1.0 + jnp.exp(-x))


def _conv3(x, w_ref):
    return w_ref[0:1, :] * _shift_dn(x, 2) + w_ref[1:2, :] * _shift_dn(x, 1) + w_ref[2:3, :] * x


def _conv3_bwd_x(dy, w_ref):
    return w_ref[2:3, :] * dy + w_ref[1:2, :] * _shift_up(dy, 1) + w_ref[0:1, :] * _shift_up(dy, 2)


def _conv3_bwd_w(dy, x):
    return jnp.concatenate([
        jnp.sum(dy * _shift_dn(x, 2), axis=0, keepdims=True),
        jnp.sum(dy * _shift_dn(x, 1), axis=0, keepdims=True),
        jnp.sum(dy * x, axis=0, keepdims=True)], axis=0)


def _dot(a, b, dims):
    return lax.dot_general(a.astype(BF16), b.astype(BF16), (dims, ((), ())), preferred_element_type=F32)


_NN = ((1,), (0,))
_NT = ((1,), (1,))
_TN = ((0,), (0,))


def _tiles(name, m, n, default):
    tm, tn = TILES.get(name, default)
    return math.gcd(tm, m), math.gcd(tn, n)


def _mm(a, b, mode, *, name, out_dtype=F32, add=None, norm=None, norm_bwd=None, tm_cap=512, tn_cap=1536,
        exchanges=()):
    halves = None
    if mode == "tn":
        r, m = a.shape
        n = b.shape[-1] * (2 if b.ndim == 3 else 1)
        tm, tn = _tiles(name, m, n, (_pick_tile(m, 256, LANE), _pick_tile(n, tn_cap, LANE)))
        if b.ndim == 3:
            per_half = b.shape[-1] // tn
            b_spec = pl.BlockSpec((None, r, tn), lambda i, j: (j // per_half, 0, j % per_half))
        else:
            b_spec = pl.BlockSpec((r, tn), lambda i, j: (0, j))
        in_specs = [pl.BlockSpec((r, tm), lambda i, j: (0, i)), b_spec]
        dims = _TN
    elif mode == "nt" and a.ndim == 3:
        _, m, halves = a.shape
        n = b.shape[0]
        tm, tn = _tiles(name, m, n, (_pick_tile(m, tm_cap, SUBLANE), _pick_tile(n, tn_cap, LANE)))
        in_specs = [pl.BlockSpec((2, tm, halves), lambda i, j: (0, i, 0)),
                    pl.BlockSpec((tn, 2 * halves), lambda i, j: (j, 0))]
        dims = _NT
    elif mode == "nn":
        m, k = a.shape
        n = b.shape[1]
        tm, tn = _tiles(name, m, n, (_pick_tile(m, tm_cap, SUBLANE), _pick_tile(n, tn_cap, LANE)))
        in_specs = [pl.BlockSpec((tm, k), lambda i, j: (i, 0)), pl.BlockSpec((k, tn), lambda i, j: (0, j))]
        dims = _NN
    else:
        m, k = a.shape
        n = b.shape[0]
        tm, tn = _tiles(name, m, n, (_pick_tile(m, tm_cap, SUBLANE), _pick_tile(n, tn_cap, LANE)))
        in_specs = [pl.BlockSpec((tm, k), lambda i, j: (i, 0)), pl.BlockSpec((tn, k), lambda i, j: (j, 0))]
        dims = _NT
    assert m % tm == 0 and n % tn == 0, (name, m, n, tm, tn)
    args = [a, b]
    tile = pl.BlockSpec((tm, tn), lambda i, j: (i, j))
    if add is not None:
        in_specs.append(tile)
        args.append(add)
    out_specs, out_shape = tile, jax.ShapeDtypeStruct((m, n), out_dtype)
    if norm is not None:
        gains, layer = norm
        assert tn == n
        in_specs.append(pl.BlockSpec((None, 1, n), lambda i, j: (layer, 0, 0)))
        args.append(gains.reshape(gains.shape[0], 1, n))
        out_specs, out_shape = [tile, tile], [out_shape, jax.ShapeDtypeStruct((m, n), BF16)]
    if norm_bwd is not None:
        x_in, gains, layer, res = norm_bwd
        assert tn == n and add is None and norm is None
        vec = pl.BlockSpec((None, 1, n), lambda i, j: (layer, 0, 0))
        in_specs += [tile, vec, tile]
        args += [x_in, gains.reshape(gains.shape[0], 1, n), res]
        out_specs = [tile, tile, pl.BlockSpec((1, n), lambda i, j: (0, 0))]
        out_shape = [jax.ShapeDtypeStruct((m, n), F32), jax.ShapeDtypeStruct((m, n), BF16),
                     jax.ShapeDtypeStruct((1, n), F32)]

    def body(*refs):
        if halves is None:
            acc = _dot(refs[0][...], refs[1][...], dims)
        else:
            acc = (_dot(refs[0][0], refs[1][:, :halves], dims) + _dot(refs[0][1], refs[1][:, halves:], dims))
        if add is not None:
            acc = acc + refs[2][...]
        if norm_bwd is not None:
            x_ref, g_ref, res_ref, dx_ref, dxb_ref, dg_ref = refs[2:]

            @pl.when(pl.program_id(0) == 0)
            def _():
                dg_ref[...] = jnp.zeros_like(dg_ref)
            dx, xn = _rms_bwd_rows(acc, x_ref[...], g_ref[...])
            dx = dx + res_ref[...]
            dx_ref[...] = dx
            dxb_ref[...] = dx.astype(BF16)
            dg_ref[...] += jnp.sum(acc * xn, axis=0, keepdims=True)
        elif norm is None:
            refs[-1][...] = acc.astype(out_dtype)
        else:
            refs[-2][...] = acc.astype(out_dtype)
            r = lax.rsqrt(jnp.mean(acc * acc, axis=-1, keepdims=True) + EPS)
            refs[-1][...] = (acc * r * refs[-3][...]).astype(BF16)

    return _call(body, name=name, grid=(m // tm, n // tn), in_specs=in_specs, out_specs=out_specs,
                 out_shape=out_shape, args=args, parallel=norm_bwd is None, exchanges=exchanges)


def _rms_fwd(x, g, *, name):
    l, d = x.shape
    tl = _pick_tile(l, 512, SUBLANE)

    def body(x_ref, g_ref, h_ref):
        xv = x_ref[...]
        r = lax.rsqrt(jnp.mean(xv * xv, axis=-1, keepdims=True) + EPS)
        h_ref[...] = (xv * r * g_ref[...]).astype(BF16)

    return pl.pallas_call(
        body, name=name, grid=(l // tl,),
        in_specs=[pl.BlockSpec((tl, d), lambda i: (i, 0)), pl.BlockSpec((1, d), lambda i: (0, 0))],
        out_specs=pl.BlockSpec((tl, d), lambda i: (i, 0)),
        out_shape=jax.ShapeDtypeStruct((l, d), BF16), compiler_params=_PAR)(x, g)


def _rms_bwd_rows(dh, xv, g):
    r = lax.rsqrt(jnp.mean(xv * xv, axis=-1, keepdims=True) + EPS)
    a = dh * g
    m = jnp.mean(a * xv, axis=-1, keepdims=True)
    return r * a - xv * (r * r * r) * m, xv * r


def _rms_bwd(dh, x, g, res, *, name):
    l, d = x.shape
    tl = _pick_tile(l, 512, SUBLANE)

    def body(dh_ref, x_ref, g_ref, res_ref, dx_ref, dxb_ref, dg_ref):
        @pl.when(pl.program_id(0) == 0)
        def _():
            dg_ref[...] = jnp.zeros_like(dg_ref)
        dhv = dh_ref[...]
        dx, xn = _rms_bwd_rows(dhv, x_ref[...], g_ref[...])
        dx = dx + res_ref[...]
        dx_ref[...] = dx
        dxb_ref[...] = dx.astype(BF16)
        dg_ref[...] += jnp.sum(dhv * xn, axis=0, keepdims=True)

    row = pl.BlockSpec((tl, d), lambda i: (i, 0))
    vec = pl.BlockSpec((1, d), lambda i: (0, 0))
    return pl.pallas_call(
        body, name=name, grid=(l // tl,), in_specs=[row, row, vec, row], out_specs=[row, row, vec],
        out_shape=[jax.ShapeDtypeStruct((l, d), F32), jax.ShapeDtypeStruct((l, d), BF16),
                   jax.ShapeDtypeStruct((1, d), F32)],
        compiler_params=_ARB)(dh, x, g, res)


def _loss_head(x, g, tgt, *, name):
    l, d = x.shape
    tl = _pick_tile(l, 512, SUBLANE)

    def body(x_ref, g_ref, t_ref, loss_ref, dx_ref, dxb_ref, dg_ref):
        @pl.when(pl.program_id(0) == 0)
        def _():
            dg_ref[...] = jnp.zeros_like(dg_ref)
            loss_ref[...] = jnp.zeros_like(loss_ref)
        xv = x_ref[...]
        gv = g_ref[...]
        r = lax.rsqrt(jnp.mean(xv * xv, axis=-1, keepdims=True) + EPS)
        err = xv * r * gv - t_ref[...]
        row_loss = jnp.sum(err * err, axis=-1, keepdims=True) * (0.5 / d)
        loss_ref[...] += jnp.sum(row_loss, axis=0, keepdims=True)
        dy = err * (1.0 / d)
        dx, xn = _rms_bwd_rows(dy, xv, gv)
        dx_ref[...] = dx
        dxb_ref[...] = dx.astype(BF16)
        dg_ref[...] += jnp.sum(dy * xn, axis=0, keepdims=True)

    row = pl.BlockSpec((tl, d), lambda i: (i, 0))
    vec = pl.BlockSpec((1, d), lambda i: (0, 0))
    one = pl.BlockSpec((1, 1), lambda i: (0, 0))
    return pl.pallas_call(
        body, name=name, grid=(l // tl,), in_specs=[row, vec, row], out_specs=[one, row, row, vec],
        out_shape=[jax.ShapeDtypeStruct((1, 1), F32), jax.ShapeDtypeStruct((l, d), F32),
                   jax.ShapeDtypeStruct((l, d), BF16), jax.ShapeDtypeStruct((1, d), F32)],
        compiler_params=_ARB)(x, g, tgt)


def _ffn_act(up, cw, cb, *, name, exchanges=()):
    l = up.shape[0]
    nb = D_FF // LANE

    def body(ug_ref, uv_ref, wg_ref, wv_ref, bg_ref, bv_ref, o_ref):
        gc = _conv3(ug_ref[...], wg_ref) + bg_ref[...]
        vc = _conv3(uv_ref[...], wv_ref) + bv_ref[...]
        o_ref[...] = (gc * _sigmoid(gc) * vc).astype(BF16)

    col = lambda off: pl.BlockSpec((l, LANE), lambda j: (0, j + off))
    w3 = lambda off: pl.BlockSpec((CONV_WIDTH, LANE), lambda j: (0, j + off))
    b1 = lambda off: pl.BlockSpec((1, LANE), lambda j: (0, j + off))
    return _call(body, name=name, grid=(nb,), in_specs=[col(0), col(nb), w3(0), w3(nb), b1(0), b1(nb)],
                 out_specs=col(0), out_shape=jax.ShapeDtypeStruct((l, D_FF), BF16),
                 args=[up, up, cw, cw, cb, cb], exchanges=exchanges)


def _ffn_act_bwd(up, dact, cw, cb, *, name, exchanges=()):
    l = up.shape[0]
    nb = D_FF // LANE

    def conv(x, w_ref, b_ref):
        x1, x2 = _shift_dn(x, 1), _shift_dn(x, 2)
        return w_ref[0:1, :] * x2 + w_ref[1:2, :] * x1 + w_ref[2:3, :] * x + b_ref[...], (x, x1, x2)

    def half_bwd(k, dc, taps, w_ref, dup_ref, dcw_ref, dcb_ref):
        x0, x1, x2 = taps
        dcb_ref[k] = jnp.sum(dc, axis=0, keepdims=True)
        dcw_ref[k] = jnp.concatenate([jnp.sum(dc * x2, axis=0, keepdims=True),
                                      jnp.sum(dc * x1, axis=0, keepdims=True),
                                      jnp.sum(dc * x0, axis=0, keepdims=True)], axis=0)
        dup_ref[k] = _conv3_bwd_x(dc, w_ref).astype(BF16)

    def body(ug_ref, uv_ref, da_ref, wg_ref, wv_ref, bg_ref, bv_ref, dup_ref, dcw_ref, dcb_ref):
        da = da_ref[...]
        gc, g_taps = conv(ug_ref[...], wg_ref, bg_ref)
        vc, v_taps = conv(uv_ref[...], wv_ref, bv_ref)
        sg = _sigmoid(gc)
        half_bwd(0, da * vc * (sg * (1.0 + gc * (1.0 - sg))), g_taps, wg_ref, dup_ref, dcw_ref, dcb_ref)
        half_bwd(1, da * (gc * sg), v_taps, wv_ref, dup_ref, dcw_ref, dcb_ref)

    col = lambda off: pl.BlockSpec((l, LANE), lambda j: (0, j + off))
    w3 = lambda off: pl.BlockSpec((CONV_WIDTH, LANE), lambda j: (0, j + off))
    b1 = lambda off: pl.BlockSpec((1, LANE), lambda j: (0, j + off))
    both = lambda rows: pl.BlockSpec((2, rows, LANE), lambda j: (0, 0, j))
    res = _call(
        body, name=name, grid=(nb,),
        in_specs=[col(0), col(nb), col(0), w3(0), w3(nb), b1(0), b1(nb)],
        out_specs=[both(l), both(CONV_WIDTH), both(1)],
        out_shape=[jax.ShapeDtypeStruct((2, l, D_FF), BF16), jax.ShapeDtypeStruct((2, CONV_WIDTH, D_FF), F32),
                   jax.ShapeDtypeStruct((2, 1, D_FF), F32)],
        args=[up, up, dact, cw, cw, cb, cb], exchanges=exchanges)
    (dup, dcw, dcb), moved = res if exchanges else (res, None)
    outs = [dup, jnp.concatenate([dcw[0], dcw[1]], axis=1), jnp.concatenate([dcb[0], dcb[1]], axis=1)]
    return (outs, moved) if exchanges else outs


def _sconv_fwd(proj, cw, *, name):
    l = proj.shape[0]
    nb = D_HALF // LANE

    def body(xa_ref, ba_ref, ca_ref, w_ref, o_ref):
        o_ref[...] = (ba_ref[...] * _conv3(ca_ref[...] * xa_ref[...], w_ref)).astype(BF16)

    col = lambda off: pl.BlockSpec((l, LANE), lambda j: (0, j + off))
    return pl.pallas_call(
        body, name=name, grid=(nb,),
        in_specs=[col(0), col(nb), col(2 * nb), pl.BlockSpec((CONV_WIDTH, LANE), lambda j: (0, j))],
        out_specs=col(0), out_shape=jax.ShapeDtypeStruct((l, 2 * D_HALF), BF16),
        compiler_params=_PAR)(proj, proj, proj, cw)


def _sconv_bwd(proj, dcat, cw, *, name):
    l = proj.shape[0]
    nb = D_HALF // LANE

    def body(xa_ref, ba_ref, ca_ref, dy_ref, w_ref, dxa_ref, dba_ref, dca_ref, dw_ref):
        xa, ba, ca, dy = xa_ref[...], ba_ref[...], ca_ref[...], dy_ref[...]
        q = ca * xa
        dba_ref[...] = (dy * _conv3(q, w_ref)).astype(BF16)
        dconv = dy * ba
        dw_ref[...] = _conv3_bwd_w(dconv, q)
        dq = _conv3_bwd_x(dconv, w_ref)
        dxa_ref[...] = (dq * ca).astype(BF16)
        dca_ref[...] = (dq * xa).astype(BF16)

    col = lambda off: pl.BlockSpec((l, LANE), lambda j: (0, j + off))
    w3 = pl.BlockSpec((CONV_WIDTH, LANE), lambda j: (0, j))
    piece = jax.ShapeDtypeStruct((l, D_HALF), BF16)
    return pl.pallas_call(
        body, name=name, grid=(nb,),
        in_specs=[col(0), col(nb), col(2 * nb), col(0), w3],
        out_specs=[col(0), col(0), col(0), w3],
        out_shape=[piece, piece, piece, jax.ShapeDtypeStruct((CONV_WIDTH, D_HALF), F32)],
        compiler_params=_PAR)(proj, proj, proj, dcat, cw)


def _s5_prep(log_step, a_re, a_im, b_re, b_im):
    step = jnp.exp(log_step)[:, None]
    mag = jnp.exp(a_re * step)
    lr = mag * jnp.cos(a_im * step)
    li = mag * jnp.sin(a_im * step)
    nr = lr - 1.0
    den = a_re * a_re + a_im * a_im
    qr = (nr * a_re + li * a_im) / den
    qi = (li * a_re - nr * a_im) / den
    br = qr[..., None] * b_re - qi[..., None] * b_im
    bi = qr[..., None] * b_im + qi[..., None] * b_re
    return lr, li, br, bi


def _block_diag(m):
    nb, ng, r, c = m.shape
    eye = jnp.eye(ng, dtype=m.dtype)
    return jnp.einsum("bgrc,gh->bgrhc", m, eye).reshape(nb, ng * r, ng * c)


def _block_diag_extract(w, r, c):
    nb = w.shape[0]
    ng = w.shape[1] // r
    w5 = w.reshape(nb, ng, r, ng, c)
    return jnp.einsum("bgrhc,gh->bgrc", w5, jnp.eye(ng, dtype=w.dtype))


def _s5_mats(br, bi, c_re, c_im):
    g8 = N_SSM_GROUPS // S5_LANE_BLOCKS
    to_blk = lambda m: m.reshape(S5_LANE_BLOCKS, g8, m.shape[1], m.shape[2])
    wb = jnp.concatenate([_block_diag(to_blk(jnp.swapaxes(br, 1, 2))),
                          _block_diag(to_blk(jnp.swapaxes(bi, 1, 2)))], axis=2)
    wc = jnp.concatenate([_block_diag(to_blk(jnp.swapaxes(c_re, 1, 2))),
                          _block_diag(to_blk(jnp.swapaxes(-c_im, 1, 2)))], axis=1)
    return wb, wc


def _s5_mats_bwd(dwb, dwc):
    g, p, h = N_SSM_GROUPS, SSM_STATE, SSM_GROUP
    half = S5_STATE_LANES
    dbr = jnp.swapaxes(_block_diag_extract(dwb[:, :, :half], h, p).reshape(g, h, p), 1, 2)
    dbi = jnp.swapaxes(_block_diag_extract(dwb[:, :, half:], h, p).reshape(g, h, p), 1, 2)
    dcr = jnp.swapaxes(_block_diag_extract(dwc[:, :half, :], p, h).reshape(g, p, h), 1, 2)
    dci = -jnp.swapaxes(_block_diag_extract(dwc[:, half:, :], p, h).reshape(g, p, h), 1, 2)
    return dbr, dbi, dcr, dci


def _s5_scan_consts(log_step, a_re, a_im, reverse):
    step = jnp.exp(log_step)[:, None]
    xr = (a_re * step).reshape(S5_LANE_BLOCKS, 1, S5_STATE_LANES)
    xi = (a_im * step).reshape(S5_LANE_BLOCKS, 1, S5_STATE_LANES)
    if reverse:
        xi = -xi
    row = jnp.arange(SUBLANE, dtype=F32).reshape(1, SUBLANE, 1)

    def power(n):
        mag = jnp.exp(n * xr)
        return jnp.concatenate([mag * jnp.cos(n * xi), mag * jnp.sin(n * xi)], axis=-1)

    kinds = []
    for d in (1, 2, 4):
        keep = (row <= SUBLANE - 1 - d) if reverse else (row >= d)
        kinds.append(jnp.where(keep, power(jnp.full_like(row, float(d))), 0.0))
    kinds.append(power((SUBLANE - row) if reverse else (row + 1.0)))
    return jnp.stack(kinds, axis=1)


def _scan_rows(s_ref, sc_ref, carry_ref, n_rows, reverse):
    n_grp = n_rows // SUBLANE
    n_col = S5_STATE_LANES // LANE
    half = S5_STATE_LANES

    def step(i, carry):
        grp = (n_grp - 1 - i) if reverse else i
        r0 = pl.multiple_of(grp * SUBLANE, SUBLANE)
        out = []
        for cb in range(n_col):
            lo, hi = cb * LANE, half + cb * LANE
            re = s_ref[pl.ds(r0, SUBLANE), lo:lo + LANE]
            im = s_ref[pl.ds(r0, SUBLANE), hi:hi + LANE]
            for k, d in enumerate((1, 2, 4)):
                sh = (SUBLANE - d) if reverse else d
                rr, ri = pltpu.roll(re, sh, 0), pltpu.roll(im, sh, 0)
                ar, ai = sc_ref[k, :, lo:lo + LANE], sc_ref[k, :, hi:hi + LANE]
                re, im = re + (ar * rr - ai * ri), im + (ar * ri + ai * rr)
            pr, pi = sc_ref[3, :, lo:lo + LANE], sc_ref[3, :, hi:hi + LANE]
            cr, ci = carry[2 * cb], carry[2 * cb + 1]
            re, im = re + (pr * cr - pi * ci), im + (pr * ci + pi * cr)
            s_ref[pl.ds(r0, SUBLANE), lo:lo + LANE] = re
            s_ref[pl.ds(r0, SUBLANE), hi:hi + LANE] = im
            edge = 0 if reverse else SUBLANE - 1
            out.append(jnp.broadcast_to(re[edge:edge + 1, :], (SUBLANE, LANE)))
            out.append(jnp.broadcast_to(im[edge:edge + 1, :], (SUBLANE, LANE)))
        return tuple(out)

    init = []
    for cb in range(n_col):
        init.append(carry_ref[:, cb * LANE:(cb + 1) * LANE])
        init.append(carry_ref[:, half + cb * LANE:half + (cb + 1) * LANE])
    fin = lax.fori_loop(0, n_grp, step, tuple(init))
    for cb in range(n_col):
        carry_ref[:, cb * LANE:(cb + 1) * LANE] = fin[2 * cb]
        carry_ref[:, half + cb * LANE:half + (cb + 1) * LANE] = fin[2 * cb + 1]


def _s5_fwd(proj, wb, wc, d_skip, sc, *, name, exchanges=()):
    l = proj.shape[0]
    tt = _pick_tile(l, S5_TIME_CHUNK, SUBLANE)
    u_off = (proj.shape[1] - D_HALF) // LANE
    w2 = 2 * S5_STATE_LANES

    def body(u_ref, wb_ref, wc_ref, d_ref, sc_ref, s_ref, y_ref, carry_ref):
        @pl.when(pl.program_id(1) == 0)
        def _():
            carry_ref[...] = jnp.zeros_like(carry_ref)
        u = u_ref[...]
        s_ref[...] = _dot(u, wb_ref[0], _NN)
        _scan_rows(s_ref, sc_ref.at[0], carry_ref, tt, False)
        y_ref[...] = _dot(s_ref[...], wc_ref[0], _NN) + d_ref[...] * u

    return _call(
        body, name=name, grid=(S5_LANE_BLOCKS, l // tt),
        in_specs=[pl.BlockSpec((tt, LANE), lambda b, t: (t, b + u_off)),
                  pl.BlockSpec((1, LANE, w2), lambda b, t: (b, 0, 0)),
                  pl.BlockSpec((1, w2, LANE), lambda b, t: (b, 0, 0)),
                  pl.BlockSpec((1, LANE), lambda b, t: (0, b)),
                  pl.BlockSpec((1, 4, SUBLANE, w2), lambda b, t: (b, 0, 0, 0))],
        out_specs=[pl.BlockSpec((tt, w2), lambda b, t: (t, b)), pl.BlockSpec((tt, LANE), lambda b, t: (t, b))],
        out_shape=[jax.ShapeDtypeStruct((l, S5_LANE_BLOCKS * w2), F32), jax.ShapeDtypeStruct((l, D_HALF), F32)],
        scratch_shapes=[pltpu.VMEM((SUBLANE, w2), F32)],
        args=[proj, wb, wc, d_skip, sc], parallel=False, exchanges=exchanges)


def _s5_bwd(proj, dy, states, wb, wc, d_skip, sc_rev, *, name, exchanges=()):
    l = proj.shape[0]
    tt = _pick_tile(l, S5_TIME_CHUNK, SUBLANE)
    nt = l // tt
    u_off = (proj.shape[1] - D_HALF) // LANE
    w2 = 2 * S5_STATE_LANES
    half = S5_STATE_LANES
    grp_per_chunk = tt // SUBLANE

    def body(u_ref, dy_ref, s_ref, halo_ref, wb_ref, wc_ref, d_ref, sc_ref,
             du_ref, dwb_ref, dwc_ref, dlam_ref, dd_ref, g_scr, carry_ref):
        t = pl.program_id(1)

        @pl.when(t == 0)
        def _():
            carry_ref[...] = jnp.zeros_like(carry_ref)
            dwb_ref[...] = jnp.zeros_like(dwb_ref)
            dwc_ref[...] = jnp.zeros_like(dwc_ref)
            dlam_ref[...] = jnp.zeros_like(dlam_ref)
            dd_ref[...] = jnp.zeros_like(dd_ref)

        u = u_ref[...]
        dyv = dy_ref[...]
        g_scr[...] = _dot(dyv, wc_ref[0], _NT)
        _scan_rows(g_scr, sc_ref.at[0], carry_ref, tt, True)
        gv = g_scr[...]
        du_ref[...] = (_dot(gv, wb_ref[0], _NT) + d_ref[...] * dyv).astype(BF16)
        dwb_ref[0] += _dot(u, gv, _TN)
        sv = s_ref[...]
        dwc_ref[0] += _dot(sv, dyv, _TN)
        dd_ref[...] += jnp.sum(dyv * u, axis=0, keepdims=True)
        first_chunk = t == nt - 1
        halo = jnp.where(first_chunk, 0.0, halo_ref[SUBLANE - 1:SUBLANE, :])
        row = lax.broadcasted_iota(jnp.int32, sv.shape, 0)
        sp = jnp.where(row == 0, jnp.broadcast_to(halo, sv.shape), pltpu.roll(sv, 1, 0))
        gr, gi = gv[:, :half], gv[:, half:]
        sr, si = sp[:, :half], sp[:, half:]
        dlr = jnp.sum(gr * sr + gi * si, axis=0, keepdims=True)
        dli = jnp.sum(gi * sr - gr * si, axis=0, keepdims=True)
        dlam_ref[0] += jnp.concatenate([dlr, dli], axis=1)

    rev = lambda t: nt - 1 - t
    return _call(
        body, name=name, grid=(S5_LANE_BLOCKS, nt),
        in_specs=[pl.BlockSpec((tt, LANE), lambda b, t: (rev(t), b + u_off)),
                  pl.BlockSpec((tt, LANE), lambda b, t: (rev(t), b)),
                  pl.BlockSpec((tt, w2), lambda b, t: (rev(t), b)),
                  pl.BlockSpec((SUBLANE, w2), lambda b, t: (jnp.maximum(rev(t) * grp_per_chunk - 1, 0), b)),
                  pl.BlockSpec((1, LANE, w2), lambda b, t: (b, 0, 0)),
                  pl.BlockSpec((1, w2, LANE), lambda b, t: (b, 0, 0)),
                  pl.BlockSpec((1, LANE), lambda b, t: (0, b)),
                  pl.BlockSpec((1, 4, SUBLANE, w2), lambda b, t: (b, 0, 0, 0))],
        out_specs=[pl.BlockSpec((tt, LANE), lambda b, t: (rev(t), b)),
                   pl.BlockSpec((1, LANE, w2), lambda b, t: (b, 0, 0)),
                   pl.BlockSpec((1, w2, LANE), lambda b, t: (b, 0, 0)),
                   pl.BlockSpec((1, 1, w2), lambda b, t: (b, 0, 0)),
                   pl.BlockSpec((1, LANE), lambda b, t: (0, b))],
        out_shape=[jax.ShapeDtypeStruct((l, D_HALF), BF16),
                   jax.ShapeDtypeStruct((S5_LANE_BLOCKS, LANE, w2), F32),
                   jax.ShapeDtypeStruct((S5_LANE_BLOCKS, w2, LANE), F32),
                   jax.ShapeDtypeStruct((S5_LANE_BLOCKS, 1, w2), F32),
                   jax.ShapeDtypeStruct((1, D_HALF), F32)],
        scratch_shapes=[pltpu.VMEM((tt, w2), F32), pltpu.VMEM((SUBLANE, w2), F32)],
        args=[proj, dy, states, states, wb, wc, d_skip, sc_rev], parallel=False, exchanges=exchanges)


def _glu_fwd(ypre, wg, bg, cat, *, name):
    l, d = ypre.shape
    tl = _pick_tile(l, 512, SUBLANE)

    def body(y_ref, w_ref, b_ref, cat_ref, o_ref):
        yg = _gelu(y_ref[...])
        o_ref[...] = (yg * _sigmoid(_dot(yg, w_ref[...], _NN) + b_ref[...])).astype(BF16)

    row = pl.BlockSpec((tl, d), lambda i: (i, 0))
    return pl.pallas_call(
        body, name=name, grid=(l // tl,),
        in_specs=[row, pl.BlockSpec((d, d), lambda i: (0, 0)), pl.BlockSpec((1, d), lambda i: (0, 0)), _ANY],
        out_specs=pl.BlockSpec((tl, d), lambda i: (i, 1)), out_shape=jax.ShapeDtypeStruct(cat.shape, cat.dtype),
        input_output_aliases={3: 0}, compiler_params=_PAR)(ypre, wg, bg, cat)


def _glu_bwd(dcat, ypre, wg, bg, *, name):
    l, d = ypre.shape
    tl = _pick_tile(l, 512, SUBLANE)

    def body(dy_ref, y_ref, w_ref, b_ref, dyp_ref, dw_ref, db_ref):
        @pl.when(pl.program_id(0) == 0)
        def _():
            dw_ref[...] = jnp.zeros_like(dw_ref)
            db_ref[...] = jnp.zeros_like(db_ref)
        yp = y_ref[...]
        dyb = dy_ref[...]
        yg = _gelu(yp)
        sg = _sigmoid(_dot(yg, w_ref[...], _NN) + b_ref[...])
        dz = dyb * yg * sg * (1.0 - sg)
        dyg = dyb * sg + _dot(dz, w_ref[...], _NT)
        dyp_ref[...] = dyg * _gelu_grad(yp)
        dw_ref[...] += _dot(yg, dz, _TN)
        db_ref[...] += jnp.sum(dz, axis=0, keepdims=True)

    row = pl.BlockSpec((tl, d), lambda i: (i, 0))
    mat = pl.BlockSpec((d, d), lambda i: (0, 0))
    vec = pl.BlockSpec((1, d), lambda i: (0, 0))
    return pl.pallas_call(
        body, name=name, grid=(l // tl,),
        in_specs=[pl.BlockSpec((tl, d), lambda i: (i, 1)), row, mat, vec], out_specs=[row, mat, vec],
        out_shape=[jax.ShapeDtypeStruct((l, d), F32), jax.ShapeDtypeStruct((d, d), F32),
                   jax.ShapeDtypeStruct((1, d), F32)],
        compiler_params=_ARB)(dcat, ypre, wg, bg)


def _window_sum(x, w, trailing):
    s, d = x, 1
    while d < w:
        s = s + (_shift_dn(s, d) if trailing else _shift_up(s, d))
        d *= 2
    return s


def _window_count(shape, w):
    row = lax.broadcasted_iota(jnp.int32, shape, 0)
    return jnp.minimum(row + 1, w).astype(F32)


def _pool_fwd(proj, pw, scale, *, name):
    l = proj.shape[0]
    ng = len(POOL_WINDOWS)

    def body(z_ref, w_ref, sc_ref, o_ref):
        z = z_ref[...]
        for k, w in enumerate(POOL_WINDOWS):
            @pl.when(pl.program_id(0) == k)
            def _():
                pooled = _window_sum(z, w, True) / _window_count(z.shape, w) - z
                o_ref[...] = (_dot(pooled, w_ref[0], _NN) * sc_ref[...]).astype(BF16)

    col = pl.BlockSpec((l, LANE), lambda g: (0, g))
    return pl.pallas_call(
        body, name=name, grid=(ng,),
        in_specs=[col, pl.BlockSpec((1, LANE, LANE), lambda g: (g, 0, 0)), pl.BlockSpec((1, LANE), lambda g: (0, g))],
        out_specs=col, out_shape=jax.ShapeDtypeStruct((l, 2 * D_HALF), BF16), compiler_params=_PAR)(proj, pw, scale)


def _pool_bwd(proj, dcat, pw, scale, *, name):
    l = proj.shape[0]
    ng = len(POOL_WINDOWS)

    def body(z_ref, dy_ref, w_ref, sc_ref, dz_ref, dw_ref, dsc_ref):
        z = z_ref[...]
        dy = dy_ref[...]
        for k, w in enumerate(POOL_WINDOWS):
            @pl.when(pl.program_id(0) == k)
            def _():
                cnt = _window_count(z.shape, w)
                pooled = _window_sum(z, w, True) / cnt - z
                ypre = _dot(pooled, w_ref[0], _NN)
                dsc_ref[...] = jnp.sum(dy * ypre, axis=0, keepdims=True)
                dyp = dy * sc_ref[...]
                dw_ref[0] = _dot(pooled, dyp, _TN)
                dpool = _dot(dyp, w_ref[0], _NT)
                dz_ref[...] = (_window_sum(dpool / cnt, w, False) - dpool).astype(BF16)

    col = pl.BlockSpec((l, LANE), lambda g: (0, g))
    mat = pl.BlockSpec((1, LANE, LANE), lambda g: (g, 0, 0))
    vec = pl.BlockSpec((1, LANE), lambda g: (0, g))
    return pl.pallas_call(
        body, name=name, grid=(ng,), in_specs=[col, col, mat, vec], out_specs=[col, mat, vec],
        out_shape=[jax.ShapeDtypeStruct((l, D_HALF), BF16), jax.ShapeDtypeStruct((ng, LANE, LANE), F32),
                   jax.ShapeDtypeStruct((1, D_HALF), F32)],
        compiler_params=_PAR)(proj, dcat, pw, scale)


def _tril_mask():
    r = lax.broadcasted_iota(jnp.int32, (CHUNK, CHUNK), 0)
    c = lax.broadcasted_iota(jnp.int32, (CHUNK, CHUNK), 1)
    return r >= c


def _sgu_fwd(proj, ng, sw, sb_t, cat, *, name):
    l = proj.shape[0]
    tl = _pick_tile(l, 512, CHUNK)

    def body(su_ref, sv_ref, g_ref, w_ref, b_ref, cat_ref, o_ref):
        su = _gelu(su_ref[...])
        sv = _gelu(sv_ref[...])
        r = lax.rsqrt(jnp.mean(sv * sv, axis=-1, keepdims=True) + EPS)
        v = sv * r * g_ref[...]
        mask = _tril_mask()
        for h in range(SGU_HEADS):
            wm = jnp.where(mask, w_ref[h], 0.0)
            cs = slice(h * LANE, (h + 1) * LANE)
            for n in range(tl // CHUNK):
                rs = slice(n * CHUNK, (n + 1) * CHUNK)
                mixed = _dot(wm, v[rs, cs], _NN) + b_ref[:, h:h + 1]
                o_ref[rs, cs] = (su[rs, cs] * mixed).astype(BF16)

    blk = lambda c: pl.BlockSpec((tl, D_HALF), lambda i: (i, c))
    return pl.pallas_call(
        body, name=name, grid=(l // tl,),
        in_specs=[blk(1), blk(2), pl.BlockSpec((1, D_HALF), lambda i: (0, 0)),
                  pl.BlockSpec((SGU_HEADS, CHUNK, CHUNK), lambda i: (0, 0, 0)),
                  pl.BlockSpec((CHUNK, SGU_HEADS), lambda i: (0, 0)), _ANY],
        out_specs=blk(1), out_shape=jax.ShapeDtypeStruct(cat.shape, cat.dtype), input_output_aliases={5: 0},
        compiler_params=_PAR)(proj, proj, ng, sw, sb_t, cat)


def _sgu_bwd(proj, dcat, ng, sw, sb_t, *, name):
    l = proj.shape[0]
    tl = _pick_tile(l, 512, CHUNK)

    def body(su_ref, sv_ref, dy_ref, g_ref, w_ref, b_ref, dsu_ref, dsv_ref, dw_ref, dbm_ref, dng_ref, dv_scr):
        @pl.when(pl.program_id(0) == 0)
        def _():
            dw_ref[...] = jnp.zeros_like(dw_ref)
            dbm_ref[...] = jnp.zeros_like(dbm_ref)
            dng_ref[...] = jnp.zeros_like(dng_ref)
        su_pre = su_ref[...]
        sv_pre = sv_ref[...]
        su = _gelu(su_pre)
        sv = _gelu(sv_pre)
        gsu = _gelu_grad(su_pre)
        gv = g_ref[...]
        r = lax.rsqrt(jnp.mean(sv * sv, axis=-1, keepdims=True) + EPS)
        v = sv * r * gv
        dy = dy_ref[...]
        mask = _tril_mask()
        for h in range(SGU_HEADS):
            wm = jnp.where(mask, w_ref[h], 0.0)
            cs = slice(h * LANE, (h + 1) * LANE)
            dw_acc = jnp.zeros((CHUNK, CHUNK), F32)
            db_acc = jnp.zeros((CHUNK, LANE), F32)
            for n in range(tl // CHUNK):
                rs = slice(n * CHUNK, (n + 1) * CHUNK)
                vb = v[rs, cs]
                mixed = _dot(wm, vb, _NN) + b_ref[:, h:h + 1]
                dyb = dy[rs, cs]
                dsu_ref[rs, cs] = (dyb * mixed * gsu[rs, cs]).astype(BF16)
                dmix = dyb * su[rs, cs]
                db_acc = db_acc + dmix
                dw_acc = dw_acc + _dot(dmix, vb, _NT)
                dv_scr[rs, cs] = _dot(wm, dmix, _TN)
            dw_ref[h] += jnp.where(mask, dw_acc, 0.0)
            dbm_ref[h] += db_acc
        dv = dv_scr[...]
        a = dv * gv
        m = jnp.mean(a * sv, axis=-1, keepdims=True)
        dsv = r * a - sv * (r * r * r) * m
        dng_ref[...] += jnp.sum(dv * sv * r, axis=0, keepdims=True)
        dsv_ref[...] = (dsv * _gelu_grad(sv_pre)).astype(BF16)

    blk = lambda c: pl.BlockSpec((tl, D_HALF), lambda i: (i, c))
    mats = pl.BlockSpec((SGU_HEADS, CHUNK, CHUNK), lambda i: (0, 0, 0))
    vec = pl.BlockSpec((1, D_HALF), lambda i: (0, 0))
    piece = jax.ShapeDtypeStruct((l, D_HALF), BF16)
    mshape = jax.ShapeDtypeStruct((SGU_HEADS, CHUNK, CHUNK), F32)
    return pl.pallas_call(
        body, name=name, grid=(l // tl,),
        in_specs=[blk(1), blk(2), blk(1), vec, mats, pl.BlockSpec((CHUNK, SGU_HEADS), lambda i: (0, 0))],
        out_specs=[blk(0), blk(0), mats, mats, vec],
        out_shape=[piece, piece, mshape, mshape, jax.ShapeDtypeStruct((1, D_HALF), F32)],
        scratch_shapes=[pltpu.VMEM((tl, D_HALF), F32)],
        compiler_params=_ARB)(proj, proj, dcat, ng, sw, sb_t)


def _s5_params(w):
    prep_args = (w["ssm_log_step"], w["ssm_a_re"], w["ssm_a_im"], w["ssm_b_re"], w["ssm_b_im"])
    (lr, li, br, bi), prep_vjp = jax.vjp(jax.vmap(_s5_prep), *prep_args)
    wb, wc = jax.vmap(_s5_mats)(br, bi, w["ssm_c_re"], w["ssm_c_im"])
    consts = lambda reverse: jax.vmap(functools.partial(_s5_scan_consts, reverse=reverse))(*prep_args[:3])
    return dict(wb=wb.astype(BF16), wc=wc.astype(BF16), d=w["ssm_d"][:, None, :], sc=consts(False),
                sc_rev=consts(True), prep_vjp=prep_vjp)


def _s5_param_grads(s5, dwb, dwc, dlam, dd):
    dbr, dbi, dcr, dci = jax.vmap(_s5_mats_bwd)(dwb, dwc)
    n = dlam.shape[0]
    dlr = dlam[:, :, 0, :S5_STATE_LANES].reshape(n, N_SSM_GROUPS, SSM_STATE)
    dli = dlam[:, :, 0, S5_STATE_LANES:].reshape(n, N_SSM_GROUPS, SSM_STATE)
    dls, dar, dai, db_re, db_im = s5["prep_vjp"]((dlr, dli, dbr, dbi))
    return dict(ssm_log_step=dls, ssm_a_re=dar, ssm_a_im=dai, ssm_b_re=db_re, ssm_b_im=db_im, ssm_c_re=dcr,
                ssm_c_im=dci, ssm_d=dd[:, 0, :])


TILES = {
    "mm_up": (4096, 512), "mm_up_dw": (512, 1408), "mm_down_dx": (1024, 2816),
    "mm_down": (512, 1024), "mm_down_dw": (256, 1024), "mm_even_in": (2048, 1024), "mm_odd_in": (2048, 768),
    "mm_mix_out": (1024, 1024), "mm_mix_out_dx": (2048, 1024), "mm_mix_out_dw": (1024, 512),
    "mm_even_in_dw": (1024, 512), "mm_odd_in_dw": (1024, 512),
}


def _layer_weights(i):
    j = i // 2
    mixer = [("even_w_in", j), ("even_w_out", j), ("ssm_glu_w", j)] if i % 2 == 0 else [("odd_w_in", j),
                                                                                         ("odd_w_out", j)]
    return dict(w_in=mixer[0], w_out=mixer[1], glu=mixer[2:], up=("ffn_w_up", i), down=("ffn_w_down", i))


class _LocalWeights:
    def __init__(self, w):
        self.w, self.grads = w, {}

    def carried_by(self, stage, i):
        return []

    def delivered(self, stage, i, outs):
        pass

    def weight(self, key):
        return self.w[key[0]][key[1]]

    def grad(self, key, dw):
        self.grads[key] = dw


class _ShardedWeights:
    def __init__(self, shards):
        self.shards = shards
        self.full, self.halves, self.pending, self.scattered = {}, {}, {}, {}

    def start(self, others):
        keys = [_layer_weights(0)["w_in"]]
        outs = _exchange_only(others + [self._gather(k) for k in keys], name="ag_first")
        self._take(keys, outs[len(others):])
        return outs[:len(others)]

    def _gather(self, key):
        shard = self.shards[key[0]][key[1]]
        if len(key) == 3:
            rows = shard.shape[0] // 2
            shard = shard[key[2] * rows:(key[2] + 1) * rows]
        return _Gather(shard)

    def _take(self, keys, outs):
        for key, got in zip(keys, outs):
            if BIG[key[0]] == 2:
                whole = jnp.swapaxes(got, 0, 1).reshape(got.shape[1], -1)
            else:
                whole = got.reshape(-1, got.shape[2])
            if len(key) == 3:
                self.halves[key] = whole
                if all((*key[:2], h) in self.halves for h in (0, 1)):
                    whole = jnp.concatenate([self.halves.pop((*key[:2], h)) for h in (0, 1)], axis=0)
                    self.full[key[:2]] = whole
            else:
                self.full[key] = whole

    def _plan(self, stage, i):
        cur = _layer_weights(i)
        nxt = _layer_weights(i + 1) if i + 1 < DEPTH else None
        has_scan = lambda k: k % 2 == 0
        none = ([], [])
        up_half = lambda h: ([(*nxt["up"], h)], []) if nxt and not has_scan(i + 1) else none
        return {
            "mm_in": ([cur["w_out"], *cur["glu"]], []) if i == 0 else none,
            "s5_fwd": ([cur["up"]], []),
            "mm_up": ([cur["down"]] + ([nxt["w_in"], nxt["w_out"], *nxt["glu"]] if nxt else []), []),
            "ffn_act": up_half(0),
            "mm_down": up_half(1),
            "ffn_act_bwd": ([], [cur["down"]] + ([nxt["up"]] if nxt and not has_scan(i + 1) else [])),
            "s5_bwd": ([], [cur["up"]]),
            "mm_up_dx": ([], [nxt["w_in"], *nxt["glu"]]) if nxt else none,
            "mm_up_dw": ([], [nxt["w_out"]]) if nxt else none,
            "mm_in_dw": ([], [cur["w_out"], *cur["glu"]]) if i == 0 else none,
            "mm_in_dx": ([], [cur["w_in"]]) if i == 0 else none,
        }[stage]

    def _scatter(self, key):
        name, layer = key
        src = self.pending.pop(key)
        layers, rows, cols = self.shards[name].shape
        if name not in self.scattered:
            self.scattered[name] = lax.empty((N_DEV, layers * rows, cols), src.dtype)
        return _Scatter(src, into=self.scattered[name], row0=layer * rows)

    def carried_by(self, stage, i):
        gather, scatter = self._plan(stage, i)
        return [self._gather(k) for k in gather] + [self._scatter(k) for k in scatter]

    def delivered(self, stage, i, outs):
        gather, scatter = self._plan(stage, i)
        self._take(gather, outs[:len(gather)])
        for (name, _), buf in zip(scatter, outs[len(gather):]):
            self.scattered[name] = buf

    def weight(self, key):
        return self.full[key]

    def grad(self, key, dw):
        self.pending[key] = _to_dest_major(dw, BIG[key[0]] - 1).astype(BF16)

    def finish(self, carrier, others):
        keys = list(self.pending)
        res, outs = carrier(others + [self._scatter(k) for k in keys])
        for (name, _), buf in zip(keys, outs[len(others):]):
            self.scattered[name] = buf
        return res, outs[:len(others)]


def _device_step(x, tgt, w, comm):
    saved = []
    s5 = _s5_params(w)
    h = _rms_fwd(x, w["norm_mix_g"][0:1], name="rms_fwd")
    for i in range(DEPTH):
        j = i // 2
        lw = _layer_weights(i)
        if i % 2 == 0:
            proj = _carry(comm, "mm_in", i, _mm, h, comm.weight(lw["w_in"]), "nn", name="mm_even_in")
            ya = _sconv_fwd(proj, w["even_conv_w"][j], name="sconv_fwd")
            states, ypre = _carry(comm, "s5_fwd", i, _s5_fwd, proj, s5["wb"][j], s5["wc"][j], s5["d"][j],
                                  s5["sc"][j], name="s5_fwd")
            cat = _glu_fwd(ypre, comm.weight(lw["glu"][0]), w["ssm_glu_b"][j][None, :], ya, name="glu_fwd")
            mix = (states, ypre)
        else:
            proj = _carry(comm, "mm_in", i, _mm, h, comm.weight(lw["w_in"]), "nn", name="mm_odd_in")
            yc = _pool_fwd(proj, w["pool_w"][j], w["pool_scale"][j][None, :], name="pool_fwd")
            sb_t = jnp.transpose(w["sgu_b"][j])
            cat = _sgu_fwd(proj, w["sgu_norm_g"][j][None, :], w["sgu_w"][j], sb_t, yc, name="sgu_fwd")
            mix = (sb_t,)
        x1, h2 = _mm(cat, comm.weight(lw["w_out"]), "nn", add=x, norm=(w["norm_ffn_g"], i), name="mm_mix_out")
        up = _carry(comm, "mm_up", i, _mm, h2, comm.weight(lw["up"]), "nn", name="mm_up")
        act = _carry(comm, "ffn_act", i, _ffn_act, up, w["ffn_conv_w"][i], w["ffn_conv_b"][i:i + 1], name="ffn_act")
        if i + 1 < DEPTH:
            x2, h_next = _carry(comm, "mm_down", i, _mm, act, comm.weight(lw["down"]), "nn", add=x1,
                                norm=(w["norm_mix_g"], i + 1), name="mm_down")
        else:
            x2, h_next = _mm(act, comm.weight(lw["down"]), "nn", add=x1, name="mm_down_last"), None
        saved.append((x, h, proj, cat, x1, h2, up, act, mix))
        x, h = x2, h_next

    loss, dx, dxb, dgf = _loss_head(x, w["norm_final_g"][None, :], tgt, name="loss_head")
    per_layer = {}
    s5_grads = []

    def put(name, idx, val):
        per_layer.setdefault(name, {})[idx] = val

    for i in reversed(range(DEPTH)):
        j = i // 2
        lw = _layer_weights(i)
        x0, h, proj, cat, x1, h2, up, act, mix = saved[i]
        dact = _mm(dxb, comm.weight(lw["down"]), "nt", name="mm_down_dx")
        comm.grad(lw["down"], _mm(act, dxb, "tn", out_dtype=BF16, name="mm_down_dw"))
        dup, dcw, dcb = _carry(comm, "ffn_act_bwd", i, _ffn_act_bwd, up, dact, w["ffn_conv_w"][i],
                               w["ffn_conv_b"][i:i + 1], name="ffn_act_bwd")
        put("ffn_conv_w", i, dcw)
        put("ffn_conv_b", i, dcb[0])
        dx1, dx1b, dg2 = _carry(comm, "mm_up_dx", i, _mm, dup, comm.weight(lw["up"]), "nt", tm_cap=256, tn_cap=D_MODEL,
                                norm_bwd=(x1, w["norm_ffn_g"], i, dx), name="mm_up_dx")
        comm.grad(lw["up"], _carry(comm, "mm_up_dw", i, _mm, h2, dup, "tn", out_dtype=BF16,
                                   name="mm_up_dw"))
        put("norm_ffn_g", i, dg2[0])
        dcat = _mm(dx1b, comm.weight(lw["w_out"]), "nt", name="mm_mix_out_dx")
        comm.grad(lw["w_out"], _mm(cat, dx1b, "tn", out_dtype=BF16, name="mm_mix_out_dw"))
        if i % 2 == 0:
            states, ypre = mix
            dxa, dba, dca, dcw_a = _sconv_bwd(proj, dcat, w["even_conv_w"][j], name="sconv_bwd")
            put("even_conv_w", j, dcw_a)
            dypre, dwg, dbg = _glu_bwd(dcat, ypre, comm.weight(lw["glu"][0]), w["ssm_glu_b"][j][None, :],
                                       name="glu_bwd")
            comm.grad(lw["glu"][0], dwg)
            put("ssm_glu_b", j, dbg[0])
            du, dwb, dwc, dlam, dd = _carry(comm, "s5_bwd", i, _s5_bwd, proj, dypre, states, s5["wb"][j], s5["wc"][j],
                                            s5["d"][j], s5["sc_rev"][j], name="s5_bwd")
            s5_grads.insert(0, (dwb, dwc, dlam, dd))
            dproj = jnp.concatenate([dxa, dba, dca, du], axis=1)
            in_name = "mm_even_in"
        else:
            (sb_t,) = mix
            dz, dpw, dps = _pool_bwd(proj, dcat, w["pool_w"][j], w["pool_scale"][j][None, :], name="pool_bwd")
            put("pool_w", j, dpw)
            put("pool_scale", j, dps[0])
            dsu, dsv, dsw, dbm, dng = _sgu_bwd(proj, dcat, w["sgu_norm_g"][j][None, :], w["sgu_w"][j], sb_t,
                                               name="sgu_bwd")
            put("sgu_w", j, dsw)
            put("sgu_b", j, jnp.sum(dbm, axis=-1))
            put("sgu_norm_g", j, dng[0])
            dproj = jnp.concatenate([dz, dsu, dsv], axis=1)
            in_name = "mm_odd_in"
        comm.grad(lw["w_in"], _carry(comm, "mm_in_dw", i, _mm, h, dproj, "tn", out_dtype=BF16, name=in_name + "_dw"))
        dx, dxb, dg1 = _carry(comm, "mm_in_dx", i, _mm, dproj, comm.weight(lw["w_in"]), "nt", tn_cap=D_MODEL,
                              norm_bwd=(x0, w["norm_mix_g"], i, dx1), name=in_name + "_dx")
        put("norm_mix_g", i, dg1[0])

    grads = {nm: [vals[k] for k in sorted(vals)] for nm, vals in per_layer.items()}
    grads.update({nm: jnp.stack(grads[nm]) for nm in SMALL})
    grads["norm_final_g"] = dgf[0]
    grads.update(_s5_param_grads(s5, *[jnp.stack(parts) for parts in zip(*s5_grads)]))
    return loss, dx, grads


def _carry(comm, stage, i, fn, *args, **kwargs):
    exchanges = comm.carried_by(stage, i)
    if not exchanges:
        return fn(*args, **kwargs)
    out, moved = fn(*args, exchanges=exchanges, **kwargs)
    comm.delivered(stage, i, moved)
    return out


def _sum_parts(parts, *, name):
    g, r, c = parts.shape
    tr = _pick_tile(r, max(16, EXCHANGE_BLOCK_ELEMS // c), 16)

    def body(p_ref, o_ref):
        acc = p_ref[0].astype(F32)
        for k in range(1, g):
            acc = acc + p_ref[k].astype(F32)
        o_ref[...] = acc

    return pl.pallas_call(
        body, name=name, grid=(r // tr,), in_specs=[pl.BlockSpec((g, tr, c), lambda i: (0, i, 0))],
        out_specs=pl.BlockSpec((tr, c), lambda i: (i, 0)), out_shape=jax.ShapeDtypeStruct((r, c), F32),
        compiler_params=_PAR)(parts)


def _adamw(w, m, v, g_parts, *, name, exchanges=()):
    r, c = w.shape
    g = g_parts.shape[0]
    tc = _pick_tile(c, 8192, LANE)
    tr = _pick_tile(r, max(16, (1 << 18) // tc), 16)
    c1 = 1.0 - ADAM_B1 ** ADAM_STEP
    c2 = 1.0 - ADAM_B2 ** ADAM_STEP

    def body(w_ref, m_ref, v_ref, g_ref, go_ref, d_ref, mo_ref, vo_ref):
        grad = g_ref[0].astype(F32)
        for k in range(1, g):
            grad = grad + g_ref[k].astype(F32)
        m_new = ADAM_B1 * m_ref[...] + (1.0 - ADAM_B1) * grad
        v_new = ADAM_B2 * v_ref[...] + (1.0 - ADAM_B2) * (grad * grad)
        go_ref[...] = grad
        mo_ref[...] = m_new
        vo_ref[...] = v_new
        d_ref[...] = -ADAM_LR * ((m_new / c1) / (jnp.sqrt(v_new / c2) + ADAM_EPS) + ADAM_WD * w_ref[...])

    blk = pl.BlockSpec((tr, tc), lambda i, j: (i, j))
    out = jax.ShapeDtypeStruct((r, c), F32)
    return _call(body, name=name, grid=(r // tr, c // tc),
                 in_specs=[blk, blk, blk, pl.BlockSpec((g, tr, tc), lambda i, j: (0, i, j))],
                 out_specs=[blk, blk, blk, blk], out_shape=[out, out, out, out], args=[w, m, v, g_parts],
                 exchanges=exchanges)


WEIGHT_NAMES = ['norm_mix_g', 'even_w_in', 'even_conv_w', 'ssm_log_step', 'ssm_a_re', 'ssm_a_im', 'ssm_b_re',
                'ssm_b_im', 'ssm_c_re', 'ssm_c_im', 'ssm_d', 'ssm_glu_w', 'ssm_glu_b', 'even_w_out', 'odd_w_in',
                'pool_w', 'pool_scale', 'sgu_norm_g', 'sgu_w', 'sgu_b', 'odd_w_out', 'norm_ffn_g', 'ffn_w_up',
                'ffn_conv_w', 'ffn_conv_b', 'ffn_w_down', 'norm_final_g']
BIG = {'even_w_in': 2, 'ssm_glu_w': 1, 'even_w_out': 1, 'odd_w_in': 2, 'odd_w_out': 1, 'ffn_w_up': 2,
       'ffn_w_down': 1}
SMALL = {'even_conv_w': 2, 'pool_scale': 1, 'sgu_norm_g': 1, 'ffn_conv_w': 2}
BIG_ROWS = 512
SMALL_ROWS = 16


def _pad_to(n, q):
    return -(-n // q) * q


def _pack(arrays, dtype, rows, lead=()):
    flat = [a.reshape(lead + (-1,)).astype(dtype) for a in arrays]
    n = sum(f.shape[-1] for f in flat)
    pad = _pad_to(n, rows * LANE) - n
    if pad:
        flat.append(jnp.zeros(lead + (pad,), dtype))
    return jnp.concatenate(flat, axis=-1).reshape(lead + (rows, -1))


def _unpack(buf, shapes, lead=()):
    flat = buf.reshape(lead + (-1,))
    out, off = [], 0
    for shp in shapes:
        n = math.prod(shp)
        out.append(flat[..., off:off + n].reshape(lead + tuple(shp)))
        off += n
    return out


def _to_dest_major(full, axis):
    shp = full.shape
    split = full.reshape(shp[:axis] + (N_DEV, shp[axis] // N_DEV) + shp[axis + 1:])
    return jnp.moveaxis(split, axis, 0)


def _from_dest_major(blocks, axis):
    moved = jnp.moveaxis(blocks, 0, axis)
    shp = moved.shape
    return moved.reshape(shp[:axis] + (shp[axis] * shp[axis + 1],) + shp[axis + 2:])


def _rows_2d(a):
    return a.reshape(-1, a.shape[-1])


def _rows_2d_lead(a):
    return a.reshape(a.shape[0], -1, a.shape[-1])


def kernel(x, norm_mix_g, even_w_in, even_conv_w, ssm_log_step, ssm_a_re, ssm_a_im, ssm_b_re, ssm_b_im, ssm_c_re, ssm_c_im, ssm_d, ssm_glu_w, ssm_glu_b, even_w_out, odd_w_in, pool_w, pool_scale, sgu_norm_g, sgu_w, sgu_b, odd_w_out, norm_ffn_g, ffn_w_up, ffn_conv_w, ffn_conv_b, ffn_w_down, norm_final_g, loss_target, m_norm_mix_g, m_even_w_in, m_even_conv_w, m_ssm_log_step, m_ssm_a_re, m_ssm_a_im, m_ssm_b_re, m_ssm_b_im, m_ssm_c_re, m_ssm_c_im, m_ssm_d, m_ssm_glu_w, m_ssm_glu_b, m_even_w_out, m_odd_w_in, m_pool_w, m_pool_scale, m_sgu_norm_g, m_sgu_w, m_sgu_b, m_odd_w_out, m_norm_ffn_g, m_ffn_w_up, m_ffn_conv_w, m_ffn_conv_b, m_ffn_w_down, m_norm_final_g, v_norm_mix_g, v_even_w_in, v_even_conv_w, v_ssm_log_step, v_ssm_a_re, v_ssm_a_im, v_ssm_b_re, v_ssm_b_im, v_ssm_c_re, v_ssm_c_im, v_ssm_d, v_ssm_glu_w, v_ssm_glu_b, v_even_w_out, v_odd_w_in, v_pool_w, v_pool_scale, v_sgu_norm_g, v_sgu_w, v_sgu_b, v_odd_w_out, v_norm_ffn_g, v_ffn_w_up, v_ffn_conv_w, v_ffn_conv_b, v_ffn_w_down, v_norm_final_g):
    given = dict(locals())
    wts = {n: given[n] for n in WEIGHT_NAMES}
    mom = {n: given["m_" + n] for n in WEIGHT_NAMES}
    var = {n: given["v_" + n] for n in WEIGHT_NAMES}
    repl = [n for n in WEIGHT_NAMES if n not in BIG and n not in SMALL]

    small_shapes = [wts[n].shape for n in SMALL]
    comm = _ShardedWeights({n: wts[n].astype(BF16) for n in BIG})
    (small_all,) = comm.start([_Gather(_pack([wts[n] for n in SMALL], F32, SMALL_ROWS))])
    full = {n: wts[n] for n in repl}
    for n, blocks in zip(SMALL, _unpack(small_all, small_shapes, lead=(N_DEV,))):
        full[n] = _from_dest_major(blocks, SMALL[n])

    loss, dx, grads = _device_step(x[0], loss_target[0], full, comm)

    repl_shapes = [wts[n].shape for n in repl]
    pieces = [p.reshape(-1) for n in repl for p in (grads[n] if isinstance(grads[n], list) else [grads[n]])]
    repl_flat = jnp.concatenate(pieces + [loss.reshape(-1)])
    n_repl = repl_flat.shape[0]
    chunk = _pad_to(-(-n_repl // N_DEV), SMALL_ROWS * LANE)
    repl_flat = jnp.pad(repl_flat, (0, N_DEV * chunk - n_repl))
    small_part = _pack([_to_dest_major(grads[n], SMALL[n]) for n in SMALL], F32, SMALL_ROWS, lead=(N_DEV,))
    small_cols = small_part.shape[2]
    small_scatter = _Scatter(
        jnp.concatenate([small_part, repl_flat.reshape(N_DEV, SMALL_ROWS, chunk // SMALL_ROWS)], axis=2))

    _, (small_rs,) = comm.finish(lambda exchanges: (None, _exchange_only(exchanges, name="rs_last")), [small_scatter])
    small_sum = _sum_parts(small_rs, name="rs_sum_small")
    (repl_all,) = _exchange_only([_Gather(small_sum[:, small_cols:])], name="ag_repl")
    repl_sum = repl_all.reshape(-1)
    total_loss = repl_sum[n_repl - 1]

    out = {}
    for n in BIG:
        res = _adamw(_rows_2d(wts[n]), _rows_2d(mom[n]), _rows_2d(var[n]), comm.scattered[n], name="adamw_" + n)
        out[n] = [r.reshape(wts[n].shape) for r in res]

    def small_vec(shard_part, repl_part):
        flat = jnp.concatenate([shard_part.reshape(-1), repl_part])
        return flat.reshape(SMALL_ROWS, -1)

    def small_tree(tree):
        tail = jnp.concatenate([tree[n].reshape(-1) for n in repl])
        tail = jnp.pad(tail, (0, N_DEV * chunk - tail.shape[0]))
        return small_vec(_pack([tree[n] for n in SMALL], F32, SMALL_ROWS), tail)

    res = _adamw(small_tree(wts), small_tree(mom), small_tree(var), small_vec(small_sum[:, :small_cols], repl_sum)[None],
                 name="adamw_small")
    n_small = SMALL_ROWS * small_cols
    for k, r in enumerate(res):
        flat = r.reshape(-1)
        shard = _unpack(flat[:n_small], small_shapes)
        rest = _unpack(flat[n_small:], repl_shapes)
        for n, val in zip(SMALL, shard):
            out.setdefault(n, [None] * 4)[k] = val
        for n, val in zip(repl, rest):
            out.setdefault(n, [None] * 4)[k] = val

    grad_x = dx[None]
    return (total_loss, grad_x, *[out[n][0] for n in WEIGHT_NAMES], *[out[n][1] for n in WEIGHT_NAMES],
            *[out[n][2] for n in WEIGHT_NAMES], *[out[n][3] for n in WEIGHT_NAMES])
```

```python
import functools
import math

import jax
import jax.numpy as jnp
from jax import lax
from jax.experimental import pallas as pl
from jax.experimental.pallas import tpu as pltpu

F32 = jnp.float32
BF16 = jnp.bfloat16

D_MODEL = 1024
DEPTH = 4
D_HALF = D_MODEL // 2
SSM_GROUP = 16
N_SSM_GROUPS = D_HALF // SSM_GROUP
SSM_STATE = 64
POOL_WINDOWS = (2, 4, 8, 16)
SGU_HEADS = 4
CHUNK = 128
D_FF = 2816
CONV_WIDTH = 3
EPS = 1e-6
N_DEV = 8

ADAM_LR = 0.001
ADAM_B1 = 0.9
ADAM_B2 = 0.999
ADAM_EPS = 1e-08
ADAM_WD = 0.01
ADAM_STEP = 10

LANE = 128
SUBLANE = 8
S5_LANE_BLOCKS = D_HALF // LANE
S5_STATE_LANES = (N_SSM_GROUPS // S5_LANE_BLOCKS) * SSM_STATE
S5_TIME_CHUNK = 512
EXCHANGE_BLOCK_ELEMS = 1 << 20

GELU_K = math.sqrt(2.0 / math.pi)
GELU_C = 0.044715

_ARB = pltpu.CompilerParams(dimension_semantics=("arbitrary",))
_ARB2 = pltpu.CompilerParams(dimension_semantics=("arbitrary", "arbitrary"))
_PAR = pltpu.CompilerParams(dimension_semantics=("parallel",))
_PAR2 = pltpu.CompilerParams(dimension_semantics=("parallel", "parallel"))


def _pick_tile(n, cap, mult):
    if n <= cap:
        return n
    best = None
    for t in range(mult, cap + 1, mult):
        if n % t == 0:
            best = t
    assert best is not None, (n, cap, mult)
    return best


_MESH = pl.DeviceIdType.MESH
_ANY = pl.BlockSpec(memory_space=pl.ANY)
SEMS_PER_EXCHANGE = N_DEV - 1


class _Gather:
    def __init__(self, src, into=None, row0=0):
        self.src, self.into, self.row0 = src, into, row0
        self.out_shape = jax.ShapeDtypeStruct((N_DEV,) + src.shape if into is None else into.shape, src.dtype)

    def copies(self, x_ref, whole_ref, send_sems, recv_sems, local_sem):
        out_ref = whole_ref if self.into is None else whole_ref.at[:, pl.ds(self.row0, self.src.shape[0])]
        x, y, cc = lax.axis_index("x"), lax.axis_index("y"), lax.axis_index("c")
        me, sibling = (x, y, cc), (x, y, 1 - cc)
        chips = [(1 - x, y), (x, 1 - y), (1 - x, 1 - y)]

        def rows(px, py, pc):
            return out_ref.at[4 * px + 2 * py + pc]

        def copy(k, block, to, src=None):
            return pltpu.make_async_remote_copy(
                src_ref=rows(*block) if src is None else src, dst_ref=rows(*block),
                send_sem=send_sems.at[k], recv_sem=recv_sems.at[k], device_id=to, device_id_type=_MESH)

        return dict(
            mine=pltpu.make_async_copy(x_ref, rows(*me), local_sem),
            first=[copy(0, me, sibling, src=x_ref)] + [copy(1 + k, me, (*chip, cc), src=x_ref)
                                                       for k, chip in enumerate(chips)],
            passed=[copy(4 + k, (*chip, cc), sibling) for k, chip in enumerate(chips)],
            over_ici=[copy(1 + k, (*chip, cc), me) for k, chip in enumerate(chips)],
            from_sibling=[copy(0, sibling, me)] + [copy(4 + k, (*chip, 1 - cc), me) for k, chip in enumerate(chips)])

    def start(self, *refs):
        cps = self.copies(*refs)
        cps["mine"].start()
        for cp in cps["first"]:
            cp.start()

    def finish(self, *refs):
        cps = self.copies(*refs)
        for arrived, onward in zip(cps["over_ici"], cps["passed"]):
            arrived.wait_recv()
            onward.start()
        for arrived in cps["from_sibling"]:
            arrived.wait_recv()
        for cp in cps["first"] + cps["passed"]:
            cp.wait_send()
        cps["mine"].wait()


class _Scatter:
    def __init__(self, src, into=None, row0=0):
        self.src, self.into, self.row0 = src, into, row0
        whole = src if into is None else into
        self.out_shape = jax.ShapeDtypeStruct(whole.shape, whole.dtype)

    def copies(self, p_ref, whole_ref, send_sems, recv_sems, local_sem):
        x, y, cc = lax.axis_index("x"), lax.axis_index("y"), lax.axis_index("c")
        me = 4 * x + 2 * y + cc
        rows = self.src.shape[1]
        out_ref = whole_ref if self.into is None else whole_ref.at[:, pl.ds(self.row0, rows)]
        sends, arrivals = [], []
        for k in range(1, N_DEV):
            px = (1 - x) if k & 4 else x
            py = (1 - y) if k & 2 else y
            pc = (1 - cc) if k & 1 else cc
            peer = 4 * px + 2 * py + pc
            kw = dict(send_sem=send_sems.at[k - 1], recv_sem=recv_sems.at[k - 1], device_id=(px, py, pc),
                      device_id_type=_MESH)
            sends.append(pltpu.make_async_remote_copy(src_ref=p_ref.at[peer], dst_ref=out_ref.at[me], **kw))
            arrivals.append(pltpu.make_async_remote_copy(src_ref=p_ref.at[me], dst_ref=out_ref.at[peer], **kw))
        return dict(mine=pltpu.make_async_copy(p_ref.at[me], out_ref.at[me], local_sem), sends=sends,
                    arrivals=arrivals)

    def start(self, *refs):
        cps = self.copies(*refs)
        cps["mine"].start()
        for cp in cps["sends"]:
            cp.start()

    def finish(self, *refs):
        cps = self.copies(*refs)
        for cp in cps["arrivals"]:
            cp.wait_recv()
        for cp in cps["sends"]:
            cp.wait_send()
        cps["mine"].wait()


class _SemView:
    def __init__(self, ref, lo):
        self.ref, self.lo = ref, lo

    @property
    def at(self):
        return self

    def __getitem__(self, k):
        return self.ref.at[self.lo + k]


def _call(body, *, name, grid, in_specs, out_specs, out_shape, args, scratch_shapes=(), parallel=True, exchanges=()):
    n_axes = len(grid)
    if not exchanges:
        sem = ("parallel" if parallel else "arbitrary",) * n_axes
        return pl.pallas_call(
            body, name=name, grid=grid, in_specs=in_specs, out_specs=out_specs, out_shape=out_shape,
            scratch_shapes=scratch_shapes, compiler_params=pltpu.CompilerParams(dimension_semantics=sem))(*args)
    single = not isinstance(out_shape, (list, tuple))
    out_specs = [out_specs] if single else list(out_specs)
    out_shape = [out_shape] if single else list(out_shape)
    n_in, n_out, n_scr, n_x = len(in_specs), len(out_specs), len(scratch_shapes), len(exchanges)
    landing = [(e, ex.into) for e, ex in enumerate(exchanges) if ex.into is not None]
    aliases = {n_in + n_x + pos: n_out + e for pos, (e, _) in enumerate(landing)}

    def wrapped(*refs):
        ins, refs = refs[:n_in], refs[n_in:]
        x_in, refs = refs[:n_x], refs[n_x + len(landing):]
        outs, refs = refs[:n_out], refs[n_out:]
        x_out, refs = refs[:n_x], refs[n_x:]
        scr, (send_sems, recv_sems, local_sems) = refs[:n_scr], refs[n_scr:]
        ids = [pl.program_id(k) for k in range(n_axes)]
        first = functools.reduce(jnp.logical_and, [i == 0 for i in ids])
        last = functools.reduce(jnp.logical_and, [i == g - 1 for i, g in zip(ids, grid)])

        def sems(e):
            lo = e * SEMS_PER_EXCHANGE
            return _SemView(send_sems, lo), _SemView(recv_sems, lo), local_sems.at[e]

        @pl.when(first)
        def _():
            for e, ex in enumerate(exchanges):
                ex.start(x_in[e], x_out[e], *sems(e))

        body(*ins, *outs, *scr)

        @pl.when(last)
        def _():
            for e, ex in enumerate(exchanges):
                ex.finish(x_in[e], x_out[e], *sems(e))

    res = pl.pallas_call(
        wrapped, name=name, grid=grid, in_specs=list(in_specs) + [_ANY] * (n_x + len(landing)),
        out_specs=out_specs + [_ANY] * n_x, out_shape=out_shape + [ex.out_shape for ex in exchanges],
        input_output_aliases=aliases,
        scratch_shapes=list(scratch_shapes) + [pltpu.SemaphoreType.DMA((n_x * SEMS_PER_EXCHANGE,)),
                                               pltpu.SemaphoreType.DMA((n_x * SEMS_PER_EXCHANGE,)),
                                               pltpu.SemaphoreType.DMA((n_x,))],
        compiler_params=pltpu.CompilerParams(dimension_semantics=("arbitrary",) * n_axes),
    )(*args, *[ex.src for ex in exchanges], *[buf for _, buf in landing])
    outs, x_outs = res[:n_out], res[n_out:]
    return (outs[0] if single else outs), x_outs


def _exchange_only(exchanges, *, name):
    def body():
        pass

    return _call(body, name=name, grid=(1,), in_specs=[], out_specs=[], out_shape=[], args=[],
                 exchanges=exchanges)[1]


def _shift_dn(x, d):
    rolled = pltpu.roll(x, d, 0)
    if x.shape[0] <= SUBLANE or d >= SUBLANE:
        row = lax.broadcasted_iota(jnp.int32, x.shape, 0)
        return jnp.where(row >= d, rolled, 0.0)
    row = lax.broadcasted_iota(jnp.int32, (SUBLANE, x.shape[1]), 0)
    return jnp.concatenate([jnp.where(row >= d, rolled[:SUBLANE], 0.0), rolled[SUBLANE:]], axis=0)


def _shift_up(x, d):
    n = x.shape[0]
    rolled = pltpu.roll(x, n - d, 0)
    if n <= SUBLANE or d >= SUBLANE:
        row = lax.broadcasted_iota(jnp.int32, x.shape, 0)
        return jnp.where(row < n - d, rolled, 0.0)
    row = lax.broadcasted_iota(jnp.int32, (SUBLANE, x.shape[1]), 0)
    return jnp.concatenate([rolled[:n - SUBLANE], jnp.where(row < SUBLANE - d, rolled[n - SUBLANE:], 0.0)], axis=0)


def _gelu(x):
    return 0.5 * x * (1.0 + jnp.tanh(GELU_K * (x + GELU_C * x * x * x)))


def _gelu_grad(x):
    t = jnp.tanh(GELU_K * (x + GELU_C * x * x * x))
    return 0.5 * (1.0 + t) + 0.5 * x * (1.0 - t * t) * (GELU_K * (1.0 + 3.0 * GELU_C * x * x))


def _sigmoid(x):
    return 0.5 + 0.5 * jnp.tanh(0.5 * x)


def _conv3(x, w_ref):
    return w_ref[0:1, :] * _shift_dn(x, 2) + w_ref[1:2, :] * _shift_dn(x, 1) + w_ref[2:3, :] * x


def _conv3_bwd_x(dy, w_ref):
    return w_ref[2:3, :] * dy + w_ref[1:2, :] * _shift_up(dy, 1) + w_ref[0:1, :] * _shift_up(dy, 2)


def _conv3_bwd_w(dy, x):
    return jnp.concatenate([
        jnp.sum(dy * _shift_dn(x, 2), axis=0, keepdims=True),
        jnp.sum(dy * _shift_dn(x, 1), axis=0, keepdims=True),
        jnp.sum(dy * x, axis=0, keepdims=True)], axis=0)


def _dot(a, b, dims):
    return lax.dot_general(a.astype(BF16), b.astype(BF16), (dims, ((), ())), preferred_element_type=F32)


_NN = ((1,), (0,))
_NT = ((1,), (1,))
_TN = ((0,), (0,))


def _tiles(name, m, n, default):
    tm, tn = TILES.get(name, default)
    return math.gcd(tm, m), math.gcd(tn, n)


def _mm(a, b, mode, *, name, out_dtype=F32, add=None, norm=None, norm_bwd=None, tm_cap=512, tn_cap=1536,
        exchanges=()):
    halves = None
    if mode == "tn":
        r, m = a.shape
        n = b.shape[-1] * (2 if b.ndim == 3 else 1)
        tm, tn = _tiles(name, m, n, (_pick_tile(m, 256, LANE), _pick_tile(n, tn_cap, LANE)))
        if b.ndim == 3:
            per_half = b.shape[-1] // tn
            b_spec = pl.BlockSpec((None, r, tn), lambda i, j: (j // per_half, 0, j % per_half))
        else:
            b_spec = pl.BlockSpec((r, tn), lambda i, j: (0, j))
        in_specs = [pl.BlockSpec((r, tm), lambda i, j: (0, i)), b_spec]
        dims = _TN
    elif mode == "nt" and a.ndim == 3:
        _, m, halves = a.shape
        n = b.shape[0]
        tm, tn = _tiles(name, m, n, (_pick_tile(m, tm_cap, SUBLANE), _pick_tile(n, tn_cap, LANE)))
        in_specs = [pl.BlockSpec((2, tm, halves), lambda i, j: (0, i, 0)),
                    pl.BlockSpec((tn, 2 * halves), lambda i, j: (j, 0))]
        dims = _NT
    elif mode == "nn":
        m, k = a.shape
        n = b.shape[1]
        tm, tn = _tiles(name, m, n, (_pick_tile(m, tm_cap, SUBLANE), _pick_tile(n, tn_cap, LANE)))
        in_specs = [pl.BlockSpec((tm, k), lambda i, j: (i, 0)), pl.BlockSpec((k, tn), lambda i, j: (0, j))]
        dims = _NN
    else:
        m, k = a.shape
        n = b.shape[0]
        tm, tn = _tiles(name, m, n, (_pick_tile(m, tm_cap, SUBLANE), _pick_tile(n, tn_cap, LANE)))
        in_specs = [pl.BlockSpec((tm, k), lambda i, j: (i, 0)), pl.BlockSpec((tn, k), lambda i, j: (j, 0))]
        dims = _NT
    assert m % tm == 0 and n % tn == 0, (name, m, n, tm, tn)
    args = [a, b]
    tile = pl.BlockSpec((tm, tn), lambda i, j: (i, j))
    if add is not None:
        in_specs.append(tile)
        args.append(add)
    out_specs, out_shape = tile, jax.ShapeDtypeStruct((m, n), out_dtype)
    if norm is not None:
        gains, layer = norm
        assert tn == n
        in_specs.append(pl.BlockSpec((None, 1, n), lambda i, j: (layer, 0, 0)))
        args.append(gains.reshape(gains.shape[0], 1, n))
        out_specs, out_shape = [tile, tile], [out_shape, jax.ShapeDtypeStruct((m, n), BF16)]
    if norm_bwd is not None:
        x_in, gains, layer, res = norm_bwd
        assert tn == n and add is None and norm is None
        vec = pl.BlockSpec((None, 1, n), lambda i, j: (layer, 0, 0))
        in_specs += [tile, vec, tile]
        args += [x_in, gains.reshape(gains.shape[0], 1, n), res]
        out_specs = [tile, tile, pl.BlockSpec((1, n), lambda i, j: (0, 0))]
        out_shape = [jax.ShapeDtypeStruct((m, n), F32), jax.ShapeDtypeStruct((m, n), BF16),
                     jax.ShapeDtypeStruct((1, n), F32)]

    def body(*refs):
        if halves is None:
            acc = _dot(refs[0][...], refs[1][...], dims)
        else:
            acc = (_dot(refs[0][0], refs[1][:, :halves], dims) + _dot(refs[0][1], refs[1][:, halves:], dims))
        if add is not None:
            acc = acc + refs[2][...]
        if norm_bwd is not None:
            x_ref, g_ref, res_ref, dx_ref, dxb_ref, dg_ref = refs[2:]

            @pl.when(pl.program_id(0) == 0)
            def _():
                dg_ref[...] = jnp.zeros_like(dg_ref)
            dx, xn = _rms_bwd_rows(acc, x_ref[...], g_ref[...])
            dx = dx + res_ref[...]
            dx_ref[...] = dx
            dxb_ref[...] = dx.astype(BF16)
            dg_ref[...] += jnp.sum(acc * xn, axis=0, keepdims=True)
        elif norm is None:
            refs[-1][...] = acc.astype(out_dtype)
        else:
            refs[-2][...] = acc.astype(out_dtype)
            r = lax.rsqrt(jnp.mean(acc * acc, axis=-1, keepdims=True) + EPS)
            refs[-1][...] = (acc * r * refs[-3][...]).astype(BF16)

    return _call(body, name=name, grid=(m // tm, n // tn), in_specs=in_specs, out_specs=out_specs,
                 out_shape=out_shape, args=args, parallel=norm_bwd is None, exchanges=exchanges)


def _rms_fwd(x, g, *, name):
    l, d = x.shape
    tl = _pick_tile(l, 512, SUBLANE)

    def body(x_ref, g_ref, h_ref):
        xv = x_ref[...]
        r = lax.rsqrt(jnp.mean(xv * xv, axis=-1, keepdims=True) + EPS)
        h_ref[...] = (xv * r * g_ref[...]).astype(BF16)

    return pl.pallas_call(
        body, name=name, grid=(l // tl,),
        in_specs=[pl.BlockSpec((tl, d), lambda i: (i, 0)), pl.BlockSpec((1, d), lambda i: (0, 0))],
        out_specs=pl.BlockSpec((tl, d), lambda i: (i, 0)),
        out_shape=jax.ShapeDtypeStruct((l, d), BF16), compiler_params=_PAR)(x, g)


def _rms_bwd_rows(dh, xv, g):
    r = lax.rsqrt(jnp.mean(xv * xv, axis=-1, keepdims=True) + EPS)
    a = dh * g
    m = jnp.mean(a * xv, axis=-1, keepdims=True)
    return r * a - xv * (r * r * r) * m, xv * r


def _rms_bwd(dh, x, g, res, *, name):
    l, d = x.shape
    tl = _pick_tile(l, 512, SUBLANE)

    def body(dh_ref, x_ref, g_ref, res_ref, dx_ref, dxb_ref, dg_ref):
        @pl.when(pl.program_id(0) == 0)
        def _():
            dg_ref[...] = jnp.zeros_like(dg_ref)
        dhv = dh_ref[...]
        dx, xn = _rms_bwd_rows(dhv, x_ref[...], g_ref[...])
        dx = dx + res_ref[...]
        dx_ref[...] = dx
        dxb_ref[...] = dx.astype(BF16)
        dg_ref[...] += jnp.sum(dhv * xn, axis=0, keepdims=True)

    row = pl.BlockSpec((tl, d), lambda i: (i, 0))
    vec = pl.BlockSpec((1, d), lambda i: (0, 0))
    return pl.pallas_call(
        body, name=name, grid=(l // tl,), in_specs=[row, row, vec, row], out_specs=[row, row, vec],
        out_shape=[jax.ShapeDtypeStruct((l, d), F32), jax.ShapeDtypeStruct((l, d), BF16),
                   jax.ShapeDtypeStruct((1, d), F32)],
        compiler_params=_ARB)(dh, x, g, res)


def _loss_head(x, g, tgt, *, name):
    l, d = x.shape
    tl = _pick_tile(l, 512, SUBLANE)

    def body(x_ref, g_ref, t_ref, loss_ref, dx_ref, dxb_ref, dg_ref):
        @pl.when(pl.program_id(0) == 0)
        def _():
            dg_ref[...] = jnp.zeros_like(dg_ref)
            loss_ref[...] = jnp.zeros_like(loss_ref)
        xv = x_ref[...]
        gv = g_ref[...]
        r = lax.rsqrt(jnp.mean(xv * xv, axis=-1, keepdims=True) + EPS)
        err = xv * r * gv - t_ref[...]
        row_loss = jnp.sum(err * err, axis=-1, keepdims=True) * (0.5 / d)
        loss_ref[...] += jnp.sum(row_loss, axis=0, keepdims=True)
        dy = err * (1.0 / d)
        dx, xn = _rms_bwd_rows(dy, xv, gv)
        dx_ref[...] = dx
        dxb_ref[...] = dx.astype(BF16)
        dg_ref[...] += jnp.sum(dy * xn, axis=0, keepdims=True)

    row = pl.BlockSpec((tl, d), lambda i: (i, 0))
    vec = pl.BlockSpec((1, d), lambda i: (0, 0))
    one = pl.BlockSpec((1, 1), lambda i: (0, 0))
    return pl.pallas_call(
        body, name=name, grid=(l // tl,), in_specs=[row, vec, row], out_specs=[one, row, row, vec],
        out_shape=[jax.ShapeDtypeStruct((1, 1), F32), jax.ShapeDtypeStruct((l, d), F32),
                   jax.ShapeDtypeStruct((l, d), BF16), jax.ShapeDtypeStruct((1, d), F32)],
        compiler_params=_ARB)(x, g, tgt)


def _ffn_act(up, cw, cb, *, name, exchanges=()):
    l = up.shape[0]
    nb = D_FF // LANE

    def body(ug_ref, uv_ref, wg_ref, wv_ref, bg_ref, bv_ref, o_ref):
        gc = _conv3(ug_ref[...], wg_ref) + bg_ref[...]
        vc = _conv3(uv_ref[...], wv_ref) + bv_ref[...]
        o_ref[...] = (gc * _sigmoid(gc) * vc).astype(BF16)

    col = lambda off: pl.BlockSpec((l, LANE), lambda j: (0, j + off))
    w3 = lambda off: pl.BlockSpec((CONV_WIDTH, LANE), lambda j: (0, j + off))
    b1 = lambda off: pl.BlockSpec((1, LANE), lambda j: (0, j + off))
    return _call(body, name=name, grid=(nb,), in_specs=[col(0), col(nb), w3(0), w3(nb), b1(0), b1(nb)],
                 out_specs=col(0), out_shape=jax.ShapeDtypeStruct((l, D_FF), BF16),
                 args=[up, up, cw, cw, cb, cb], exchanges=exchanges)


def _ffn_act_bwd(up, dact, cw, cb, *, name, exchanges=()):
    l = up.shape[0]
    nb = D_FF // LANE

    def conv(x, w_ref, b_ref):
        x1, x2 = _shift_dn(x, 1), _shift_dn(x, 2)
        return w_ref[0:1, :] * x2 + w_ref[1:2, :] * x1 + w_ref[2:3, :] * x + b_ref[...], (x, x1, x2)

    def half_bwd(k, dc, taps, w_ref, dup_ref, dcw_ref, dcb_ref):
        x0, x1, x2 = taps
        dcb_ref[k] = jnp.sum(dc, axis=0, keepdims=True)
        dcw_ref[k] = jnp.concatenate([jnp.sum(dc * x2, axis=0, keepdims=True),
                                      jnp.sum(dc * x1, axis=0, keepdims=True),
                                      jnp.sum(dc * x0, axis=0, keepdims=True)], axis=0)
        dup_ref[k] = _conv3_bwd_x(dc, w_ref).astype(BF16)

    def body(ug_ref, uv_ref, da_ref, wg_ref, wv_ref, bg_ref, bv_ref, dup_ref, dcw_ref, dcb_ref):
        da = da_ref[...]
        gc, g_taps = conv(ug_ref[...], wg_ref, bg_ref)
        vc, v_taps = conv(uv_ref[...], wv_ref, bv_ref)
        sg = _sigmoid(gc)
        half_bwd(0, da * vc * (sg * (1.0 + gc * (1.0 - sg))), g_taps, wg_ref, dup_ref, dcw_ref, dcb_ref)
        half_bwd(1, da * (gc * sg), v_taps, wv_ref, dup_ref, dcw_ref, dcb_ref)

    col = lambda off: pl.BlockSpec((l, LANE), lambda j: (0, j + off))
    w3 = lambda off: pl.BlockSpec((CONV_WIDTH, LANE), lambda j: (0, j + off))
    b1 = lambda off: pl.BlockSpec((1, LANE), lambda j: (0, j + off))
    both = lambda rows: pl.BlockSpec((2, rows, LANE), lambda j: (0, 0, j))
    res = _call(
        body, name=name, grid=(nb,),
        in_specs=[col(0), col(nb), col(0), w3(0), w3(nb), b1(0), b1(nb)],
        out_specs=[both(l), both(CONV_WIDTH), both(1)],
        out_shape=[jax.ShapeDtypeStruct((2, l, D_FF), BF16), jax.ShapeDtypeStruct((2, CONV_WIDTH, D_FF), F32),
                   jax.ShapeDtypeStruct((2, 1, D_FF), F32)],
        args=[up, up, dact, cw, cw, cb, cb], exchanges=exchanges)
    (dup, dcw, dcb), moved = res if exchanges else (res, None)
    outs = [dup, jnp.concatenate([dcw[0], dcw[1]], axis=1), jnp.concatenate([dcb[0], dcb[1]], axis=1)]
    return (outs, moved) if exchanges else outs


def _sconv_fwd(proj, cw, *, name):
    l = proj.shape[0]
    nb = D_HALF // LANE

    def body(xa_ref, ba_ref, ca_ref, w_ref, o_ref):
        o_ref[...] = (ba_ref[...] * _conv3(ca_ref[...] * xa_ref[...], w_ref)).astype(BF16)

    col = lambda off: pl.BlockSpec((l, LANE), lambda j: (0, j + off))
    return pl.pallas_call(
        body, name=name, grid=(nb,),
        in_specs=[col(0), col(nb), col(2 * nb), pl.BlockSpec((CONV_WIDTH, LANE), lambda j: (0, j))],
        out_specs=col(0), out_shape=jax.ShapeDtypeStruct((l, 2 * D_HALF), BF16),
        compiler_params=_PAR)(proj, proj, proj, cw)


def _sconv_bwd(proj, dcat, cw, *, name):
    l = proj.shape[0]
    nb = D_HALF // LANE

    def body(xa_ref, ba_ref, ca_ref, dy_ref, w_ref, dxa_ref, dba_ref, dca_ref, dw_ref):
        xa, ba, ca, dy = xa_ref[...], ba_ref[...], ca_ref[...], dy_ref[...]
        q = ca * xa
        dba_ref[...] = (dy * _conv3(q, w_ref)).astype(BF16)
        dconv = dy * ba
        dw_ref[...] = _conv3_bwd_w(dconv, q)
        dq = _conv3_bwd_x(dconv, w_ref)
        dxa_ref[...] = (dq * ca).astype(BF16)
        dca_ref[...] = (dq * xa).astype(BF16)

    col = lambda off: pl.BlockSpec((l, LANE), lambda j: (0, j + off))
    w3 = pl.BlockSpec((CONV_WIDTH, LANE), lambda j: (0, j))
    piece = jax.ShapeDtypeStruct((l, D_HALF), BF16)
    return pl.pallas_call(
        body, name=name, grid=(nb,),
        in_specs=[col(0), col(nb), col(2 * nb), col(0), w3],
        out_specs=[col(0), col(0), col(0), w3],
        out_shape=[piece, piece, piece, jax.ShapeDtypeStruct((CONV_WIDTH, D_HALF), F32)],
        compiler_params=_PAR)(proj, proj, proj, dcat, cw)


def _s5_prep(log_step, a_re, a_im, b_re, b_im):
    step = jnp.exp(log_step)[:, None]
    mag = jnp.exp(a_re * step)
    lr = mag * jnp.cos(a_im * step)
    li = mag * jnp.sin(a_im * step)
    nr = lr - 1.0
    den = a_re * a_re + a_im * a_im
    qr = (nr * a_re + li * a_im) / den
    qi = (li * a_re - nr * a_im) / den
    br = qr[..., None] * b_re - qi[..., None] * b_im
    bi = qr[..., None] * b_im + qi[..., None] * b_re
    return lr, li, br, bi


def _block_diag(m):
    nb, ng, r, c = m.shape
    eye = jnp.eye(ng, dtype=m.dtype)
    return jnp.einsum("bgrc,gh->bgrhc", m, eye).reshape(nb, ng * r, ng * c)


def _block_diag_extract(w, r, c):
    nb = w.shape[0]
    ng = w.shape[1] // r
    w5 = w.reshape(nb, ng, r, ng, c)
    return jnp.einsum("bgrhc,gh->bgrc", w5, jnp.eye(ng, dtype=w.dtype))


def _s5_mats(br, bi, c_re, c_im):
    g8 = N_SSM_GROUPS // S5_LANE_BLOCKS
    to_blk = lambda m: m.reshape(S5_LANE_BLOCKS, g8, m.shape[1], m.shape[2])
    wb = jnp.concatenate([_block_diag(to_blk(jnp.swapaxes(br, 1, 2))),
                          _block_diag(to_blk(jnp.swapaxes(bi, 1, 2)))], axis=2)
    wc = jnp.concatenate([_block_diag(to_blk(jnp.swapaxes(c_re, 1, 2))),
                          _block_diag(to_blk(jnp.swapaxes(-c_im, 1, 2)))], axis=1)
    return wb, wc


def _s5_mats_bwd(dwb, dwc):
    g, p, h = N_SSM_GROUPS, SSM_STATE, SSM_GROUP
    half = S5_STATE_LANES
    dbr = jnp.swapaxes(_block_diag_extract(dwb[:, :, :half], h, p).reshape(g, h, p), 1, 2)
    dbi = jnp.swapaxes(_block_diag_extract(dwb[:, :, half:], h, p).reshape(g, h, p), 1, 2)
    dcr = jnp.swapaxes(_block_diag_extract(dwc[:, :half, :], p, h).reshape(g, p, h), 1, 2)
    dci = -jnp.swapaxes(_block_diag_extract(dwc[:, half:, :], p, h).reshape(g, p, h), 1, 2)
    return dbr, dbi, dcr, dci


def _s5_scan_consts(log_step, a_re, a_im, reverse):
    step = jnp.exp(log_step)[:, None]
    xr = (a_re * step).reshape(S5_LANE_BLOCKS, 1, S5_STATE_LANES)
    xi = (a_im * step).reshape(S5_LANE_BLOCKS, 1, S5_STATE_LANES)
    if reverse:
        xi = -xi
    row = jnp.arange(SUBLANE, dtype=F32).reshape(1, SUBLANE, 1)

    def power(n):
        mag = jnp.exp(n * xr)
        return jnp.concatenate([mag * jnp.cos(n * xi), mag * jnp.sin(n * xi)], axis=-1)

    kinds = []
    for d in (1, 2, 4):
        keep = (row <= SUBLANE - 1 - d) if reverse else (row >= d)
        kinds.append(jnp.where(keep, power(jnp.full_like(row, float(d))), 0.0))
    kinds.append(power((SUBLANE - row) if reverse else (row + 1.0)))
    return jnp.stack(kinds, axis=1)


def _scan_rows(s_ref, sc_ref, carry_ref, n_rows, reverse):
    n_grp = n_rows // SUBLANE
    n_col = S5_STATE_LANES // LANE
    half = S5_STATE_LANES

    def step(i, carry):
        grp = (n_grp - 1 - i) if reverse else i
        r0 = pl.multiple_of(grp * SUBLANE, SUBLANE)
        out = []
        for cb in range(n_col):
            lo, hi = cb * LANE, half + cb * LANE
            re = s_ref[pl.ds(r0, SUBLANE), lo:lo + LANE]
            im = s_ref[pl.ds(r0, SUBLANE), hi:hi + LANE]
            for k, d in enumerate((1, 2, 4)):
                sh = (SUBLANE - d) if reverse else d
                rr, ri = pltpu.roll(re, sh, 0), pltpu.roll(im, sh, 0)
                ar, ai = sc_ref[k, :, lo:lo + LANE], sc_ref[k, :, hi:hi + LANE]
                re, im = re + (ar * rr - ai * ri), im + (ar * ri + ai * rr)
            pr, pi = sc_ref[3, :, lo:lo + LANE], sc_ref[3, :, hi:hi + LANE]
            cr, ci = carry[2 * cb], carry[2 * cb + 1]
            re, im = re + (pr * cr - pi * ci), im + (pr * ci + pi * cr)
            s_ref[pl.ds(r0, SUBLANE), lo:lo + LANE] = re
            s_ref[pl.ds(r0, SUBLANE), hi:hi + LANE] = im
            edge = 0 if reverse else SUBLANE - 1
            out.append(jnp.broadcast_to(re[edge:edge + 1, :], (SUBLANE, LANE)))
            out.append(jnp.broadcast_to(im[edge:edge + 1, :], (SUBLANE, LANE)))
        return tuple(out)

    init = []
    for cb in range(n_col):
        init.append(carry_ref[:, cb * LANE:(cb + 1) * LANE])
        init.append(carry_ref[:, half + cb * LANE:half + (cb + 1) * LANE])
    fin = lax.fori_loop(0, n_grp, step, tuple(init))
    for cb in range(n_col):
        carry_ref[:, cb * LANE:(cb + 1) * LANE] = fin[2 * cb]
        carry_ref[:, half + cb * LANE:half + (cb + 1) * LANE] = fin[2 * cb + 1]


def _s5_fwd(proj, wb, wc, d_skip, sc, *, name, exchanges=()):
    l = proj.shape[0]
    tt = _pick_tile(l, S5_TIME_CHUNK, SUBLANE)
    u_off = (proj.shape[1] - D_HALF) // LANE
    w2 = 2 * S5_STATE_LANES

    def body(u_ref, wb_ref, wc_ref, d_ref, sc_ref, s_ref, y_ref, carry_ref):
        @pl.when(pl.program_id(1) == 0)
        def _():
            carry_ref[...] = jnp.zeros_like(carry_ref)
        u = u_ref[...]
        s_ref[...] = _dot(u, wb_ref[0], _NN)
        _scan_rows(s_ref, sc_ref.at[0], carry_ref, tt, False)
        y_ref[...] = _dot(s_ref[...], wc_ref[0], _NN) + d_ref[...] * u

    return _call(
        body, name=name, grid=(S5_LANE_BLOCKS, l // tt),
        in_specs=[pl.BlockSpec((tt, LANE), lambda b, t: (t, b + u_off)),
                  pl.BlockSpec((1, LANE, w2), lambda b, t: (b, 0, 0)),
                  pl.BlockSpec((1, w2, LANE), lambda b, t: (b, 0, 0)),
                  pl.BlockSpec((1, LANE), lambda b, t: (0, b)),
                  pl.BlockSpec((1, 4, SUBLANE, w2), lambda b, t: (b, 0, 0, 0))],
        out_specs=[pl.BlockSpec((tt, w2), lambda b, t: (t, b)), pl.BlockSpec((tt, LANE), lambda b, t: (t, b))],
        out_shape=[jax.ShapeDtypeStruct((l, S5_LANE_BLOCKS * w2), F32), jax.ShapeDtypeStruct((l, D_HALF), F32)],
        scratch_shapes=[pltpu.VMEM((SUBLANE, w2), F32)],
        args=[proj, wb, wc, d_skip, sc], parallel=False, exchanges=exchanges)


def _s5_bwd(proj, dy, states, wb, wc, d_skip, sc_rev, *, name, exchanges=()):
    l = proj.shape[0]
    tt = _pick_tile(l, S5_TIME_CHUNK, SUBLANE)
    nt = l // tt
    u_off = (proj.shape[1] - D_HALF) // LANE
    w2 = 2 * S5_STATE_LANES
    half = S5_STATE_LANES
    grp_per_chunk = tt // SUBLANE

    def body(u_ref, dy_ref, s_ref, halo_ref, wb_ref, wc_ref, d_ref, sc_ref,
             du_ref, dwb_ref, dwc_ref, dlam_ref, dd_ref, g_scr, carry_ref):
        t = pl.program_id(1)

        @pl.when(t == 0)
        def _():
            carry_ref[...] = jnp.zeros_like(carry_ref)
            dwb_ref[...] = jnp.zeros_like(dwb_ref)
            dwc_ref[...] = jnp.zeros_like(dwc_ref)
            dlam_ref[...] = jnp.zeros_like(dlam_ref)
            dd_ref[...] = jnp.zeros_like(dd_ref)

        u = u_ref[...]
        dyv = dy_ref[...]
        g_scr[...] = _dot(dyv, wc_ref[0], _NT)
        _scan_rows(g_scr, sc_ref.at[0], carry_ref, tt, True)
        gv = g_scr[...]
        du_ref[...] = (_dot(gv, wb_ref[0], _NT) + d_ref[...] * dyv).astype(BF16)
        dwb_ref[0] += _dot(u, gv, _TN)
        sv = s_ref[...]
        dwc_ref[0] += _dot(sv, dyv, _TN)
        dd_ref[...] += jnp.sum(dyv * u, axis=0, keepdims=True)
        first_chunk = t == nt - 1
        halo = jnp.where(first_chunk, 0.0, halo_ref[SUBLANE - 1:SUBLANE, :])
        row = lax.broadcasted_iota(jnp.int32, sv.shape, 0)
        sp = jnp.where(row == 0, jnp.broadcast_to(halo, sv.shape), pltpu.roll(sv, 1, 0))
        gr, gi = gv[:, :half], gv[:, half:]
        sr, si = sp[:, :half], sp[:, half:]
        dlr = jnp.sum(gr * sr + gi * si, axis=0, keepdims=True)
        dli = jnp.sum(gi * sr - gr * si, axis=0, keepdims=True)
        dlam_ref[0] += jnp.concatenate([dlr, dli], axis=1)

    rev = lambda t: nt - 1 - t
    return _call(
        body, name=name, grid=(S5_LANE_BLOCKS, nt),
        in_specs=[pl.BlockSpec((tt, LANE), lambda b, t: (rev(t), b + u_off)),
                  pl.BlockSpec((tt, LANE), lambda b, t: (rev(t), b)),
                  pl.BlockSpec((tt, w2), lambda b, t: (rev(t), b)),
                  pl.BlockSpec((SUBLANE, w2), lambda b, t: (jnp.maximum(rev(t) * grp_per_chunk - 1, 0), b)),
                  pl.BlockSpec((1, LANE, w2), lambda b, t: (b, 0, 0)),
                  pl.BlockSpec((1, w2, LANE), lambda b, t: (b, 0, 0)),
                  pl.BlockSpec((1, LANE), lambda b, t: (0, b)),
                  pl.BlockSpec((1, 4, SUBLANE, w2), lambda b, t: (b, 0, 0, 0))],
        out_specs=[pl.BlockSpec((tt, LANE), lambda b, t: (rev(t), b)),
                   pl.BlockSpec((1, LANE, w2), lambda b, t: (b, 0, 0)),
                   pl.BlockSpec((1, w2, LANE), lambda b, t: (b, 0, 0)),
                   pl.BlockSpec((1, 1, w2), lambda b, t: (b, 0, 0)),
                   pl.BlockSpec((1, LANE), lambda b, t: (0, b))],
        out_shape=[jax.ShapeDtypeStruct((l, D_HALF), BF16),
                   jax.ShapeDtypeStruct((S5_LANE_BLOCKS, LANE, w2), F32),
                   jax.ShapeDtypeStruct((S5_LANE_BLOCKS, w2, LANE), F32),
                   jax.ShapeDtypeStruct((S5_LANE_BLOCKS, 1, w2), F32),
                   jax.ShapeDtypeStruct((1, D_HALF), F32)],
        scratch_shapes=[pltpu.VMEM((tt, w2), F32), pltpu.VMEM((SUBLANE, w2), F32)],
        args=[proj, dy, states, states, wb, wc, d_skip, sc_rev], parallel=False, exchanges=exchanges)


def _glu_fwd(ypre, wg, bg, cat, *, name):
    l, d = ypre.shape
    tl = _pick_tile(l, 512, SUBLANE)

    def body(y_ref, w_ref, b_ref, cat_ref, o_ref):
        yg = _gelu(y_ref[...])
        o_ref[...] = (yg * _sigmoid(_dot(yg, w_ref[...], _NN) + b_ref[...])).astype(BF16)

    row = pl.BlockSpec((tl, d), lambda i: (i, 0))
    return pl.pallas_call(
        body, name=name, grid=(l // tl,),
        in_specs=[row, pl.BlockSpec((d, d), lambda i: (0, 0)), pl.BlockSpec((1, d), lambda i: (0, 0)), _ANY],
        out_specs=pl.BlockSpec((tl, d), lambda i: (i, 1)), out_shape=jax.ShapeDtypeStruct(cat.shape, cat.dtype),
        input_output_aliases={3: 0}, compiler_params=_PAR)(ypre, wg, bg, cat)


def _glu_bwd(dcat, ypre, wg, bg, *, name):
    l, d = ypre.shape
    tl = _pick_tile(l, 512, SUBLANE)

    def body(dy_ref, y_ref, w_ref, b_ref, dyp_ref, dw_ref, db_ref):
        @pl.when(pl.program_id(0) == 0)
        def _():
            dw_ref[...] = jnp.zeros_like(dw_ref)
            db_ref[...] = jnp.zeros_like(db_ref)
        yp = y_ref[...]
        dyb = dy_ref[...]
        yg = _gelu(yp)
        sg = _sigmoid(_dot(yg, w_ref[...], _NN) + b_ref[...])
        dz = dyb * yg * sg * (1.0 - sg)
        dyg = dyb * sg + _dot(dz, w_ref[...], _NT)
        dyp_ref[...] = dyg * _gelu_grad(yp)
        dw_ref[...] += _dot(yg, dz, _TN)
        db_ref[...] += jnp.sum(dz, axis=0, keepdims=True)

    row = pl.BlockSpec((tl, d), lambda i: (i, 0))
    mat = pl.BlockSpec((d, d), lambda i: (0, 0))
    vec = pl.BlockSpec((1, d), lambda i: (0, 0))
    return pl.pallas_call(
        body, name=name, grid=(l // tl,),
        in_specs=[pl.BlockSpec((tl, d), lambda i: (i, 1)), row, mat, vec], out_specs=[row, mat, vec],
        out_shape=[jax.ShapeDtypeStruct((l, d), F32), jax.ShapeDtypeStruct((d, d), F32),
                   jax.ShapeDtypeStruct((1, d), F32)],
        compiler_params=_ARB)(dcat, ypre, wg, bg)


def _window_sum(x, w, trailing):
    s, d = x, 1
    while d < w:
        s = s + (_shift_dn(s, d) if trailing else _shift_up(s, d))
        d *= 2
    return s


def _window_count(shape, w):
    row = lax.broadcasted_iota(jnp.int32, shape, 0)
    return jnp.minimum(row + 1, w).astype(F32)


def _pool_fwd(proj, pw, scale, *, name):
    l = proj.shape[0]
    ng = len(POOL_WINDOWS)

    def body(z_ref, w_ref, sc_ref, o_ref):
        z = z_ref[...]
        for k, w in enumerate(POOL_WINDOWS):
            @pl.when(pl.program_id(0) == k)
            def _():
                pooled = _window_sum(z, w, True) / _window_count(z.shape, w) - z
                o_ref[...] = (_dot(pooled, w_ref[0], _NN) * sc_ref[...]).astype(BF16)

    col = pl.BlockSpec((l, LANE), lambda g: (0, g))
    return pl.pallas_call(
        body, name=name, grid=(ng,),
        in_specs=[col, pl.BlockSpec((1, LANE, LANE), lambda g: (g, 0, 0)), pl.BlockSpec((1, LANE), lambda g: (0, g))],
        out_specs=col, out_shape=jax.ShapeDtypeStruct((l, 2 * D_HALF), BF16), compiler_params=_PAR)(proj, pw, scale)


def _pool_bwd(proj, dcat, pw, scale, *, name):
    l = proj.shape[0]
    ng = len(POOL_WINDOWS)

    def body(z_ref, dy_ref, w_ref, sc_ref, dz_ref, dw_ref, dsc_ref):
        z = z_ref[...]
        dy = dy_ref[...]
        for k, w in enumerate(POOL_WINDOWS):
            @pl.when(pl.program_id(0) == k)
            def _():
                cnt = _window_count(z.shape, w)
                pooled = _window_sum(z, w, True) / cnt - z
                ypre = _dot(pooled, w_ref[0], _NN)
                dsc_ref[...] = jnp.sum(dy * ypre, axis=0, keepdims=True)
                dyp = dy * sc_ref[...]
                dw_ref[0] = _dot(pooled, dyp, _TN)
                dpool = _dot(dyp, w_ref[0], _NT)
                dz_ref[...] = (_window_sum(dpool / cnt, w, False) - dpool).astype(BF16)

    col = pl.BlockSpec((l, LANE), lambda g: (0, g))
    mat = pl.BlockSpec((1, LANE, LANE), lambda g: (g, 0, 0))
    vec = pl.BlockSpec((1, LANE), lambda g: (0, g))
    return pl.pallas_call(
        body, name=name, grid=(ng,), in_specs=[col, col, mat, vec], out_specs=[col, mat, vec],
        out_shape=[jax.ShapeDtypeStruct((l, D_HALF), BF16), jax.ShapeDtypeStruct((ng, LANE, LANE), F32),
                   jax.ShapeDtypeStruct((1, D_HALF), F32)],
        compiler_params=_PAR)(proj, dcat, pw, scale)


def _tril_mask():
    r = lax.broadcasted_iota(jnp.int32, (CHUNK, CHUNK), 0)
    c = lax.broadcasted_iota(jnp.int32, (CHUNK, CHUNK), 1)
    return r >= c


def _sgu_fwd(proj, ng, sw, sb_t, cat, *, name):
    l = proj.shape[0]
    tl = _pick_tile(l, 512, CHUNK)

    def body(su_ref, sv_ref, g_ref, w_ref, b_ref, cat_ref, o_ref):
        su = _gelu(su_ref[...])
        sv = _gelu(sv_ref[...])
        r = lax.rsqrt(jnp.mean(sv * sv, axis=-1, keepdims=True) + EPS)
        v = sv * r * g_ref[...]
        mask = _tril_mask()
        for h in range(SGU_HEADS):
            wm = jnp.where(mask, w_ref[h], 0.0)
            cs = slice(h * LANE, (h + 1) * LANE)
            for n in range(tl // CHUNK):
                rs = slice(n * CHUNK, (n + 1) * CHUNK)
                mixed = _dot(wm, v[rs, cs], _NN) + b_ref[:, h:h + 1]
                o_ref[rs, cs] = (su[rs, cs] * mixed).astype(BF16)

    blk = lambda c: pl.BlockSpec((tl, D_HALF), lambda i: (i, c))
    return pl.pallas_call(
        body, name=name, grid=(l // tl,),
        in_specs=[blk(1), blk(2), pl.BlockSpec((1, D_HALF), lambda i: (0, 0)),
                  pl.BlockSpec((SGU_HEADS, CHUNK, CHUNK), lambda i: (0, 0, 0)),
                  pl.BlockSpec((CHUNK, SGU_HEADS), lambda i: (0, 0)), _ANY],
        out_specs=blk(1), out_shape=jax.ShapeDtypeStruct(cat.shape, cat.dtype), input_output_aliases={5: 0},
        compiler_params=_PAR)(proj, proj, ng, sw, sb_t, cat)


def _sgu_bwd(proj, dcat, ng, sw, sb_t, *, name):
    l = proj.shape[0]
    tl = _pick_tile(l, 512, CHUNK)

    def body(su_ref, sv_ref, dy_ref, g_ref, w_ref, b_ref, dsu_ref, dsv_ref, dw_ref, dbm_ref, dng_ref, dv_scr):
        @pl.when(pl.program_id(0) == 0)
        def _():
            dw_ref[...] = jnp.zeros_like(dw_ref)
            dbm_ref[...] = jnp.zeros_like(dbm_ref)
            dng_ref[...] = jnp.zeros_like(dng_ref)
        su_pre = su_ref[...]
        sv_pre = sv_ref[...]
        su = _gelu(su_pre)
        sv = _gelu(sv_pre)
        gsu = _gelu_grad(su_pre)
        gv = g_ref[...]
        r = lax.rsqrt(jnp.mean(sv * sv, axis=-1, keepdims=True) + EPS)
        v = sv * r * gv
        dy = dy_ref[...]
        mask = _tril_mask()
        for h in range(SGU_HEADS):
            wm = jnp.where(mask, w_ref[h], 0.0)
            cs = slice(h * LANE, (h + 1) * LANE)
            dw_acc = jnp.zeros((CHUNK, CHUNK), F32)
            db_acc = jnp.zeros((CHUNK, LANE), F32)
            for n in range(tl // CHUNK):
                rs = slice(n * CHUNK, (n + 1) * CHUNK)
                vb = v[rs, cs]
                mixed = _dot(wm, vb, _NN) + b_ref[:, h:h + 1]
                dyb = dy[rs, cs]
                dsu_ref[rs, cs] = (dyb * mixed * gsu[rs, cs]).astype(BF16)
                dmix = dyb * su[rs, cs]
                db_acc = db_acc + dmix
                dw_acc = dw_acc + _dot(dmix, vb, _NT)
                dv_scr[rs, cs] = _dot(wm, dmix, _TN)
            dw_ref[h] += jnp.where(mask, dw_acc, 0.0)
            dbm_ref[h] += db_acc
        dv = dv_scr[...]
        a = dv * gv
        m = jnp.mean(a * sv, axis=-1, keepdims=True)
        dsv = r * a - sv * (r * r * r) * m
        dng_ref[...] += jnp.sum(dv * sv * r, axis=0, keepdims=True)
        dsv_ref[...] = (dsv * _gelu_grad(sv_pre)).astype(BF16)

    blk = lambda c: pl.BlockSpec((tl, D_HALF), lambda i: (i, c))
    mats = pl.BlockSpec((SGU_HEADS, CHUNK, CHUNK), lambda i: (0, 0, 0))
    vec = pl.BlockSpec((1, D_HALF), lambda i: (0, 0))
    piece = jax.ShapeDtypeStruct((l, D_HALF), BF16)
    mshape = jax.ShapeDtypeStruct((SGU_HEADS, CHUNK, CHUNK), F32)
    return pl.pallas_call(
        body, name=name, grid=(l // tl,),
        in_specs=[blk(1), blk(2), blk(1), vec, mats, pl.BlockSpec((CHUNK, SGU_HEADS), lambda i: (0, 0))],
        out_specs=[blk(0), blk(0), mats, mats, vec],
        out_shape=[piece, piece, mshape, mshape, jax.ShapeDtypeStruct((1, D_HALF), F32)],
        scratch_shapes=[pltpu.VMEM((tl, D_HALF), F32)],
        compiler_params=_ARB)(proj, proj, dcat, ng, sw, sb_t)


def _s5_params(w):
    prep_args = (w["ssm_log_step"], w["ssm_a_re"], w["ssm_a_im"], w["ssm_b_re"], w["ssm_b_im"])
    (lr, li, br, bi), prep_vjp = jax.vjp(jax.vmap(_s5_prep), *prep_args)
    wb, wc = jax.vmap(_s5_mats)(br, bi, w["ssm_c_re"], w["ssm_c_im"])
    consts = lambda reverse: jax.vmap(functools.partial(_s5_scan_consts, reverse=reverse))(*prep_args[:3])
    return dict(wb=wb.astype(BF16), wc=wc.astype(BF16), d=w["ssm_d"][:, None, :], sc=consts(False),
                sc_rev=consts(True), prep_vjp=prep_vjp)


def _s5_param_grads(s5, dwb, dwc, dlam, dd):
    dbr, dbi, dcr, dci = jax.vmap(_s5_mats_bwd)(dwb, dwc)
    n = dlam.shape[0]
    dlr = dlam[:, :, 0, :S5_STATE_LANES].reshape(n, N_SSM_GROUPS, SSM_STATE)
    dli = dlam[:, :, 0, S5_STATE_LANES:].reshape(n, N_SSM_GROUPS, SSM_STATE)
    dls, dar, dai, db_re, db_im = s5["prep_vjp"]((dlr, dli, dbr, dbi))
    return dict(ssm_log_step=dls, ssm_a_re=dar, ssm_a_im=dai, ssm_b_re=db_re, ssm_b_im=db_im, ssm_c_re=dcr,
                ssm_c_im=dci, ssm_d=dd[:, 0, :])


TILES = {
    "mm_up": (4096, 512), "mm_up_dw": (1024, 1408), "mm_down_dx": (1024, 2816),
    "mm_down": (512, 1024), "mm_down_dw": (256, 1024), "mm_even_in": (2048, 1024), "mm_odd_in": (2048, 768),
    "mm_mix_out": (1024, 1024), "mm_mix_out_dx": (2048, 1024), "mm_mix_out_dw": (1024, 512),
    "mm_even_in_dw": (1024, 512), "mm_odd_in_dw": (1024, 512),
}


def _layer_weights(i):
    j = i // 2
    mixer = [("even_w_in", j), ("even_w_out", j), ("ssm_glu_w", j)] if i % 2 == 0 else [("odd_w_in", j),
                                                                                         ("odd_w_out", j)]
    return dict(w_in=mixer[0], w_out=mixer[1], glu=mixer[2:], up=("ffn_w_up", i), down=("ffn_w_down", i))


class _LocalWeights:
    def __init__(self, w):
        self.w, self.grads = w, {}

    def carried_by(self, stage, i):
        return []

    def delivered(self, stage, i, outs):
        pass

    def weight(self, key):
        return self.w[key[0]][key[1]]

    def grad(self, key, dw):
        self.grads[key] = dw


class _ShardedWeights:
    def __init__(self, shards):
        self.shards = shards
        self.full, self.halves, self.pending, self.scattered = {}, {}, {}, {}

    def start(self, others):
        keys = [_layer_weights(0)["w_in"]]
        outs = _exchange_only(others + [self._gather(k) for k in keys], name="ag_first")
        self._take(keys, outs[len(others):])
        return outs[:len(others)]

    def _gather(self, key):
        shard = self.shards[key[0]][key[1]]
        if len(key) == 2:
            return _Gather(shard)
        rows = shard.shape[0] // 2
        buf = self.halves.get(key[:2])
        if buf is None:
            buf = lax.empty((N_DEV,) + shard.shape, shard.dtype)
        return _Gather(shard[key[2] * rows:(key[2] + 1) * rows], into=buf, row0=key[2] * rows)

    def _take(self, keys, outs):
        for key, got in zip(keys, outs):
            if len(key) == 3 and key[:2] not in self.halves:
                self.halves[key[:2]] = got
                continue
            if BIG[key[0]] == 2:
                self.full[key[:2]] = jnp.swapaxes(got, 0, 1).reshape(got.shape[1], -1)
            else:
                self.full[key[:2]] = got.reshape(-1, got.shape[2])

    def _plan(self, stage, i):
        cur = _layer_weights(i)
        nxt = _layer_weights(i + 1) if i + 1 < DEPTH else None
        has_scan = lambda k: k % 2 == 0
        none = ([], [])
        up_half = lambda h: ([(*nxt["up"], h)], []) if nxt and not has_scan(i + 1) else none
        return {
            "mm_in": ([cur["w_out"], *cur["glu"]], []) if i == 0 else none,
            "s5_fwd": ([cur["up"]], []),
            "mm_up": ([cur["down"]] + ([nxt["w_in"], nxt["w_out"], *nxt["glu"]] if nxt else []), []),
            "ffn_act": up_half(0),
            "mm_down": up_half(1),
            "ffn_act_bwd": ([], [cur["down"]] + ([nxt["up"]] if nxt and not has_scan(i + 1) else [])),
            "s5_bwd": ([], [cur["up"]]),
            "mm_up_dx": ([], [nxt["w_in"], *nxt["glu"]]) if nxt else none,
            "mm_up_dw": ([], [nxt["w_out"]]) if nxt else none,
            "mm_in_dw": ([], [cur["w_out"], *cur["glu"]]) if i == 0 else none,
            "mm_in_dx": ([], [cur["w_in"]]) if i == 0 else none,
        }[stage]

    def _scatter(self, key):
        name, layer = key
        src = self.pending.pop(key)
        layers, rows, cols = self.shards[name].shape
        if name not in self.scattered:
            self.scattered[name] = lax.empty((N_DEV, layers * rows, cols), src.dtype)
        return _Scatter(src, into=self.scattered[name], row0=layer * rows)

    def carried_by(self, stage, i):
        gather, scatter = self._plan(stage, i)
        return [self._gather(k) for k in gather] + [self._scatter(k) for k in scatter]

    def delivered(self, stage, i, outs):
        gather, scatter = self._plan(stage, i)
        self._take(gather, outs[:len(gather)])
        for (name, _), buf in zip(scatter, outs[len(gather):]):
            self.scattered[name] = buf

    def weight(self, key):
        return self.full[key]

    def grad(self, key, dw):
        self.pending[key] = _to_dest_major(dw, BIG[key[0]] - 1).astype(BF16)

    def finish(self, carrier, others):
        keys = list(self.pending)
        res, outs = carrier(others + [self._scatter(k) for k in keys])
        for (name, _), buf in zip(keys, outs[len(others):]):
            self.scattered[name] = buf
        return res, outs[:len(others)]


def _device_step(x, tgt, w, comm):
    saved = []
    s5 = _s5_params(w)
    h = _rms_fwd(x, w["norm_mix_g"][0:1], name="rms_fwd")
    for i in range(DEPTH):
        j = i // 2
        lw = _layer_weights(i)
        if i % 2 == 0:
            proj = _carry(comm, "mm_in", i, _mm, h, comm.weight(lw["w_in"]), "nn", name="mm_even_in")
            ya = _sconv_fwd(proj, w["even_conv_w"][j], name="sconv_fwd")
            states, ypre = _carry(comm, "s5_fwd", i, _s5_fwd, proj, s5["wb"][j], s5["wc"][j], s5["d"][j],
                                  s5["sc"][j], name="s5_fwd")
            cat = _glu_fwd(ypre, comm.weight(lw["glu"][0]), w["ssm_glu_b"][j][None, :], ya, name="glu_fwd")
            mix = (states, ypre)
        else:
            proj = _carry(comm, "mm_in", i, _mm, h, comm.weight(lw["w_in"]), "nn", name="mm_odd_in")
            yc = _pool_fwd(proj, w["pool_w"][j], w["pool_scale"][j][None, :], name="pool_fwd")
            sb_t = jnp.transpose(w["sgu_b"][j])
            cat = _sgu_fwd(proj, w["sgu_norm_g"][j][None, :], w["sgu_w"][j], sb_t, yc, name="sgu_fwd")
            mix = (sb_t,)
        x1, h2 = _mm(cat, comm.weight(lw["w_out"]), "nn", add=x, norm=(w["norm_ffn_g"], i), name="mm_mix_out")
        up = _carry(comm, "mm_up", i, _mm, h2, comm.weight(lw["up"]), "nn", name="mm_up")
        act = _carry(comm, "ffn_act", i, _ffn_act, up, w["ffn_conv_w"][i], w["ffn_conv_b"][i:i + 1], name="ffn_act")
        if i + 1 < DEPTH:
            x2, h_next = _carry(comm, "mm_down", i, _mm, act, comm.weight(lw["down"]), "nn", add=x1,
                                norm=(w["norm_mix_g"], i + 1), name="mm_down")
        else:
            x2, h_next = _mm(act, comm.weight(lw["down"]), "nn", add=x1, name="mm_down_last"), None
        saved.append((x, h, proj, cat, x1, h2, up, act, mix))
        x, h = x2, h_next

    loss, dx, dxb, dgf = _loss_head(x, w["norm_final_g"][None, :], tgt, name="loss_head")
    per_layer = {}
    s5_grads = []

    def put(name, idx, val):
        per_layer.setdefault(name, {})[idx] = val

    for i in reversed(range(DEPTH)):
        j = i // 2
        lw = _layer_weights(i)
        x0, h, proj, cat, x1, h2, up, act, mix = saved[i]
        dact = _mm(dxb, comm.weight(lw["down"]), "nt", name="mm_down_dx")
        comm.grad(lw["down"], _mm(act, dxb, "tn", out_dtype=BF16, name="mm_down_dw"))
        dup, dcw, dcb = _carry(comm, "ffn_act_bwd", i, _ffn_act_bwd, up, dact, w["ffn_conv_w"][i],
                               w["ffn_conv_b"][i:i + 1], name="ffn_act_bwd")
        put("ffn_conv_w", i, dcw)
        put("ffn_conv_b", i, dcb[0])
        dx1, dx1b, dg2 = _carry(comm, "mm_up_dx", i, _mm, dup, comm.weight(lw["up"]), "nt", tm_cap=512, tn_cap=D_MODEL,
                                norm_bwd=(x1, w["norm_ffn_g"], i, dx), name="mm_up_dx")
        comm.grad(lw["up"], _carry(comm, "mm_up_dw", i, _mm, h2, dup, "tn", out_dtype=BF16,
                                   name="mm_up_dw"))
        put("norm_ffn_g", i, dg2[0])
        dcat = _mm(dx1b, comm.weight(lw["w_out"]), "nt", name="mm_mix_out_dx")
        comm.grad(lw["w_out"], _mm(cat, dx1b, "tn", out_dtype=BF16, name="mm_mix_out_dw"))
        if i % 2 == 0:
            states, ypre = mix
            dxa, dba, dca, dcw_a = _sconv_bwd(proj, dcat, w["even_conv_w"][j], name="sconv_bwd")
            put("even_conv_w", j, dcw_a)
            dypre, dwg, dbg = _glu_bwd(dcat, ypre, comm.weight(lw["glu"][0]), w["ssm_glu_b"][j][None, :],
                                       name="glu_bwd")
            comm.grad(lw["glu"][0], dwg)
            put("ssm_glu_b", j, dbg[0])
            du, dwb, dwc, dlam, dd = _carry(comm, "s5_bwd", i, _s5_bwd, proj, dypre, states, s5["wb"][j], s5["wc"][j],
                                            s5["d"][j], s5["sc_rev"][j], name="s5_bwd")
            s5_grads.insert(0, (dwb, dwc, dlam, dd))
            dproj = jnp.concatenate([dxa, dba, dca, du], axis=1)
            in_name = "mm_even_in"
        else:
            (sb_t,) = mix
            dz, dpw, dps = _pool_bwd(proj, dcat, w["pool_w"][j], w["pool_scale"][j][None, :], name="pool_bwd")
            put("pool_w", j, dpw)
            put("pool_scale", j, dps[0])
            dsu, dsv, dsw, dbm, dng = _sgu_bwd(proj, dcat, w["sgu_norm_g"][j][None, :], w["sgu_w"][j], sb_t,
                                               name="sgu_bwd")
            put("sgu_w", j, dsw)
            put("sgu_b", j, jnp.sum(dbm, axis=-1))
            put("sgu_norm_g", j, dng[0])
            dproj = jnp.concatenate([dz, dsu, dsv], axis=1)
            in_name = "mm_odd_in"
        comm.grad(lw["w_in"], _carry(comm, "mm_in_dw", i, _mm, h, dproj, "tn", out_dtype=BF16, name=in_name + "_dw"))
        dx, dxb, dg1 = _carry(comm, "mm_in_dx", i, _mm, dproj, comm.weight(lw["w_in"]), "nt", tn_cap=D_MODEL,
                              norm_bwd=(x0, w["norm_mix_g"], i, dx1), name=in_name + "_dx")
        put("norm_mix_g", i, dg1[0])

    grads = {nm: [vals[k] for k in sorted(vals)] for nm, vals in per_layer.items()}
    grads.update({nm: jnp.stack(grads[nm]) for nm in SMALL})
    grads["norm_final_g"] = dgf[0]
    grads.update(_s5_param_grads(s5, *[jnp.stack(parts) for parts in zip(*s5_grads)]))
    return loss, dx, grads


def _carry(comm, stage, i, fn, *args, **kwargs):
    exchanges = comm.carried_by(stage, i)
    if not exchanges:
        return fn(*args, **kwargs)
    out, moved = fn(*args, exchanges=exchanges, **kwargs)
    comm.delivered(stage, i, moved)
    return out


def _sum_parts(parts, *, name):
    g, r, c = parts.shape
    tr = _pick_tile(r, max(16, EXCHANGE_BLOCK_ELEMS // c), 16)

    def body(p_ref, o_ref):
        acc = p_ref[0].astype(F32)
        for k in range(1, g):
            acc = acc + p_ref[k].astype(F32)
        o_ref[...] = acc

    return pl.pallas_call(
        body, name=name, grid=(r // tr,), in_specs=[pl.BlockSpec((g, tr, c), lambda i: (0, i, 0))],
        out_specs=pl.BlockSpec((tr, c), lambda i: (i, 0)), out_shape=jax.ShapeDtypeStruct((r, c), F32),
        compiler_params=_PAR)(parts)


def _adamw(w, m, v, g_parts, *, name, exchanges=()):
    r, c = w.shape
    g = g_parts.shape[0]
    tc = _pick_tile(c, 8192, LANE)
    tr = _pick_tile(r, max(16, (1 << 18) // tc), 16)
    c1 = 1.0 - ADAM_B1 ** ADAM_STEP
    c2 = 1.0 - ADAM_B2 ** ADAM_STEP

    def body(w_ref, m_ref, v_ref, g_ref, go_ref, d_ref, mo_ref, vo_ref):
        grad = g_ref[0].astype(F32)
        for k in range(1, g):
            grad = grad + g_ref[k].astype(F32)
        m_new = ADAM_B1 * m_ref[...] + (1.0 - ADAM_B1) * grad
        v_new = ADAM_B2 * v_ref[...] + (1.0 - ADAM_B2) * (grad * grad)
        go_ref[...] = grad
        mo_ref[...] = m_new
        vo_ref[...] = v_new
        d_ref[...] = -ADAM_LR * ((m_new / c1) / (jnp.sqrt(v_new / c2) + ADAM_EPS) + ADAM_WD * w_ref[...])

    blk = pl.BlockSpec((tr, tc), lambda i, j: (i, j))
    out = jax.ShapeDtypeStruct((r, c), F32)
    return _call(body, name=name, grid=(r // tr, c // tc),
                 in_specs=[blk, blk, blk, pl.BlockSpec((g, tr, tc), lambda i, j: (0, i, j))],
                 out_specs=[blk, blk, blk, blk], out_shape=[out, out, out, out], args=[w, m, v, g_parts],
                 exchanges=exchanges)


WEIGHT_NAMES = ['norm_mix_g', 'even_w_in', 'even_conv_w', 'ssm_log_step', 'ssm_a_re', 'ssm_a_im', 'ssm_b_re',
                'ssm_b_im', 'ssm_c_re', 'ssm_c_im', 'ssm_d', 'ssm_glu_w', 'ssm_glu_b', 'even_w_out', 'odd_w_in',
                'pool_w', 'pool_scale', 'sgu_norm_g', 'sgu_w', 'sgu_b', 'odd_w_out', 'norm_ffn_g', 'ffn_w_up',
                'ffn_conv_w', 'ffn_conv_b', 'ffn_w_down', 'norm_final_g']
BIG = {'even_w_in': 2, 'ssm_glu_w': 1, 'even_w_out': 1, 'odd_w_in': 2, 'odd_w_out': 1, 'ffn_w_up': 2,
       'ffn_w_down': 1}
SMALL = {'even_conv_w': 2, 'pool_scale': 1, 'sgu_norm_g': 1, 'ffn_conv_w': 2}
BIG_ROWS = 512
SMALL_ROWS = 16


def _pad_to(n, q):
    return -(-n // q) * q


def _pack(arrays, dtype, rows, lead=()):
    flat = [a.reshape(lead + (-1,)).astype(dtype) for a in arrays]
    n = sum(f.shape[-1] for f in flat)
    pad = _pad_to(n, rows * LANE) - n
    if pad:
        flat.append(jnp.zeros(lead + (pad,), dtype))
    return jnp.concatenate(flat, axis=-1).reshape(lead + (rows, -1))


def _unpack(buf, shapes, lead=()):
    flat = buf.reshape(lead + (-1,))
    out, off = [], 0
    for shp in shapes:
        n = math.prod(shp)
        out.append(flat[..., off:off + n].reshape(lead + tuple(shp)))
        off += n
    return out


def _to_dest_major(full, axis):
    shp = full.shape
    split = full.reshape(shp[:axis] + (N_DEV, shp[axis] // N_DEV) + shp[axis + 1:])
    return jnp.moveaxis(split, axis, 0)


def _from_dest_major(blocks, axis):
    moved = jnp.moveaxis(blocks, 0, axis)
    shp = moved.shape
    return moved.reshape(shp[:axis] + (shp[axis] * shp[axis + 1],) + shp[axis + 2:])


def _rows_2d(a):
    return a.reshape(-1, a.shape[-1])


def _rows_2d_lead(a):
    return a.reshape(a.shape[0], -1, a.shape[-1])


def kernel(x, norm_mix_g, even_w_in, even_conv_w, ssm_log_step, ssm_a_re, ssm_a_im, ssm_b_re, ssm_b_im, ssm_c_re, ssm_c_im, ssm_d, ssm_glu_w, ssm_glu_b, even_w_out, odd_w_in, pool_w, pool_scale, sgu_norm_g, sgu_w, sgu_b, odd_w_out, norm_ffn_g, ffn_w_up, ffn_conv_w, ffn_conv_b, ffn_w_down, norm_final_g, loss_target, m_norm_mix_g, m_even_w_in, m_even_conv_w, m_ssm_log_step, m_ssm_a_re, m_ssm_a_im, m_ssm_b_re, m_ssm_b_im, m_ssm_c_re, m_ssm_c_im, m_ssm_d, m_ssm_glu_w, m_ssm_glu_b, m_even_w_out, m_odd_w_in, m_pool_w, m_pool_scale, m_sgu_norm_g, m_sgu_w, m_sgu_b, m_odd_w_out, m_norm_ffn_g, m_ffn_w_up, m_ffn_conv_w, m_ffn_conv_b, m_ffn_w_down, m_norm_final_g, v_norm_mix_g, v_even_w_in, v_even_conv_w, v_ssm_log_step, v_ssm_a_re, v_ssm_a_im, v_ssm_b_re, v_ssm_b_im, v_ssm_c_re, v_ssm_c_im, v_ssm_d, v_ssm_glu_w, v_ssm_glu_b, v_even_w_out, v_odd_w_in, v_pool_w, v_pool_scale, v_sgu_norm_g, v_sgu_w, v_sgu_b, v_odd_w_out, v_norm_ffn_g, v_ffn_w_up, v_ffn_conv_w, v_ffn_conv_b, v_ffn_w_down, v_norm_final_g):
    given = dict(locals())
    wts = {n: given[n] for n in WEIGHT_NAMES}
    mom = {n: given["m_" + n] for n in WEIGHT_NAMES}
    var = {n: given["v_" + n] for n in WEIGHT_NAMES}
    repl = [n for n in WEIGHT_NAMES if n not in BIG and n not in SMALL]

    small_shapes = [wts[n].shape for n in SMALL]
    comm = _ShardedWeights({n: wts[n].astype(BF16) for n in BIG})
    (small_all,) = comm.start([_Gather(_pack([wts[n] for n in SMALL], F32, SMALL_ROWS))])
    full = {n: wts[n] for n in repl}
    for n, blocks in zip(SMALL, _unpack(small_all, small_shapes, lead=(N_DEV,))):
        full[n] = _from_dest_major(blocks, SMALL[n])

    loss, dx, grads = _device_step(x[0], loss_target[0], full, comm)

    repl_shapes = [wts[n].shape for n in repl]
    pieces = [p.reshape(-1) for n in repl for p in (grads[n] if isinstance(grads[n], list) else [grads[n]])]
    repl_flat = jnp.concatenate(pieces + [loss.reshape(-1)])
    n_repl = repl_flat.shape[0]
    chunk = _pad_to(-(-n_repl // N_DEV), SMALL_ROWS * LANE)
    repl_flat = jnp.pad(repl_flat, (0, N_DEV * chunk - n_repl))
    small_part = _pack([_to_dest_major(grads[n], SMALL[n]) for n in SMALL], F32, SMALL_ROWS, lead=(N_DEV,))
    small_cols = small_part.shape[2]
    small_scatter = _Scatter(
        jnp.concatenate([small_part, repl_flat.reshape(N_DEV, SMALL_ROWS, chunk // SMALL_ROWS)], axis=2))

    _, (small_rs,) = comm.finish(lambda exchanges: (None, _exchange_only(exchanges, name="rs_last")), [small_scatter])
    small_sum = _sum_parts(small_rs, name="rs_sum_small")
    (repl_all,) = _exchange_only([_Gather(small_sum[:, small_cols:])], name="ag_repl")
    repl_sum = repl_all.reshape(-1)
    total_loss = repl_sum[n_repl - 1]

    out = {}
    for n in BIG:
        res = _adamw(_rows_2d(wts[n]), _rows_2d(mom[n]), _rows_2d(var[n]), comm.scattered[n], name="adamw_" + n)
        out[n] = [r.reshape(wts[n].shape) for r in res]

    def small_vec(shard_part, repl_part):
        flat = jnp.concatenate([shard_part.reshape(-1), repl_part])
        return flat.reshape(SMALL_ROWS, -1)

    def small_tree(tree):
        tail = jnp.concatenate([tree[n].reshape(-1) for n in repl])
        tail = jnp.pad(tail, (0, N_DEV * chunk - tail.shape[0]))
        return small_vec(_pack([tree[n] for n in SMALL], F32, SMALL_ROWS), tail)

    res = _adamw(small_tree(wts), small_tree(mom), small_tree(var), small_vec(small_sum[:, :small_cols], repl_sum)[None],
                 name="adamw_small")
    n_small = SMALL_ROWS * small_cols
    for k, r in enumerate(res):
        flat = r.reshape(-1)
        shard = _unpack(flat[:n_small], small_shapes)
        rest = _unpack(flat[n_small:], repl_shapes)
        for n, val in zip(SMALL, shard):
            out.setdefault(n, [None] * 4)[k] = val
        for n, val in zip(repl, rest):
            out.setdefault(n, [None] * 4)[k] = val

    grad_x = dx[None]
    return (total_loss, grad_x, *[out[n][0] for n in WEIGHT_NAMES], *[out[n][1] for n in WEIGHT_NAMES],
            *[out[n][2] for n in WEIGHT_NAMES], *[out[n][3] for n in WEIGHT_NAMES])
```

```python
import functools
import math

import jax
import jax.numpy as jnp
from jax import lax
from jax.experimental import pallas as pl
from jax.experimental.pallas import tpu as pltpu

F32 = jnp.float32
BF16 = jnp.bfloat16

D_MODEL = 1024
DEPTH = 4
D_HALF = D_MODEL // 2
SSM_GROUP = 16
N_SSM_GROUPS = D_HALF // SSM_GROUP
SSM_STATE = 64
POOL_WINDOWS = (2, 4, 8, 16)
SGU_HEADS = 4
CHUNK = 128
D_FF = 2816
CONV_WIDTH = 3
EPS = 1e-6
N_DEV = 8

ADAM_LR = 0.001
ADAM_B1 = 0.9
ADAM_B2 = 0.999
ADAM_EPS = 1e-08
ADAM_WD = 0.01
ADAM_STEP = 10

LANE = 128
SUBLANE = 8
S5_LANE_BLOCKS = D_HALF // LANE
S5_STATE_LANES = (N_SSM_GROUPS // S5_LANE_BLOCKS) * SSM_STATE
S5_TIME_CHUNK = 512
EXCHANGE_BLOCK_ELEMS = 1 << 20

GELU_K = math.sqrt(2.0 / math.pi)
GELU_C = 0.044715

_ARB = pltpu.CompilerParams(dimension_semantics=("arbitrary",))
_ARB2 = pltpu.CompilerParams(dimension_semantics=("arbitrary", "arbitrary"))
_PAR = pltpu.CompilerParams(dimension_semantics=("parallel",))
_PAR2 = pltpu.CompilerParams(dimension_semantics=("parallel", "parallel"))


def _pick_tile(n, cap, mult):
    if n <= cap:
        return n
    best = None
    for t in range(mult, cap + 1, mult):
        if n % t == 0:
            best = t
    assert best is not None, (n, cap, mult)
    return best


_MESH = pl.DeviceIdType.MESH
_ANY = pl.BlockSpec(memory_space=pl.ANY)
SEMS_PER_EXCHANGE = N_DEV - 1


class _Gather:
    def __init__(self, src, into=None, row0=0):
        self.src, self.into, self.row0 = src, into, row0
        self.out_shape = jax.ShapeDtypeStruct((N_DEV,) + src.shape if into is None else into.shape, src.dtype)

    def copies(self, x_ref, whole_ref, send_sems, recv_sems, local_sem):
        out_ref = whole_ref if self.into is None else whole_ref.at[:, pl.ds(self.row0, self.src.shape[0])]
        x, y, cc = lax.axis_index("x"), lax.axis_index("y"), lax.axis_index("c")
        me, sibling = (x, y, cc), (x, y, 1 - cc)
        chips = [(1 - x, y), (x, 1 - y), (1 - x, 1 - y)]

        def rows(px, py, pc):
            return out_ref.at[4 * px + 2 * py + pc]

        def copy(k, block, to, src=None):
            return pltpu.make_async_remote_copy(
                src_ref=rows(*block) if src is None else src, dst_ref=rows(*block),
                send_sem=send_sems.at[k], recv_sem=recv_sems.at[k], device_id=to, device_id_type=_MESH)

        return dict(
            mine=pltpu.make_async_copy(x_ref, rows(*me), local_sem),
            first=[copy(0, me, sibling, src=x_ref)] + [copy(1 + k, me, (*chip, cc), src=x_ref)
                                                       for k, chip in enumerate(chips)],
            passed=[copy(4 + k, (*chip, cc), sibling) for k, chip in enumerate(chips)],
            over_ici=[copy(1 + k, (*chip, cc), me) for k, chip in enumerate(chips)],
            from_sibling=[copy(0, sibling, me)] + [copy(4 + k, (*chip, 1 - cc), me) for k, chip in enumerate(chips)])

    def start(self, *refs):
        cps = self.copies(*refs)
        cps["mine"].start()
        for cp in cps["first"]:
            cp.start()

    def finish(self, *refs):
        cps = self.copies(*refs)
        for arrived, onward in zip(cps["over_ici"], cps["passed"]):
            arrived.wait_recv()
            onward.start()
        for arrived in cps["from_sibling"]:
            arrived.wait_recv()
        for cp in cps["first"] + cps["passed"]:
            cp.wait_send()
        cps["mine"].wait()


class _Scatter:
    def __init__(self, src, into=None, row0=0):
        self.src, self.into, self.row0 = src, into, row0
        whole = src if into is None else into
        self.out_shape = jax.ShapeDtypeStruct(whole.shape, whole.dtype)

    def copies(self, p_ref, whole_ref, send_sems, recv_sems, local_sem):
        x, y, cc = lax.axis_index("x"), lax.axis_index("y"), lax.axis_index("c")
        me = 4 * x + 2 * y + cc
        rows = self.src.shape[1]
        out_ref = whole_ref if self.into is None else whole_ref.at[:, pl.ds(self.row0, rows)]
        sends, arrivals = [], []
        for k in range(1, N_DEV):
            px = (1 - x) if k & 4 else x
            py = (1 - y) if k & 2 else y
            pc = (1 - cc) if k & 1 else cc
            peer = 4 * px + 2 * py + pc
            kw = dict(send_sem=send_sems.at[k - 1], recv_sem=recv_sems.at[k - 1], device_id=(px, py, pc),
                      device_id_type=_MESH)
            sends.append(pltpu.make_async_remote_copy(src_ref=p_ref.at[peer], dst_ref=out_ref.at[me], **kw))
            arrivals.append(pltpu.make_async_remote_copy(src_ref=p_ref.at[me], dst_ref=out_ref.at[peer], **kw))
        return dict(mine=pltpu.make_async_copy(p_ref.at[me], out_ref.at[me], local_sem), sends=sends,
                    arrivals=arrivals)

    def start(self, *refs):
        cps = self.copies(*refs)
        cps["mine"].start()
        for cp in cps["sends"]:
            cp.start()

    def finish(self, *refs):
        cps = self.copies(*refs)
        for cp in cps["arrivals"]:
            cp.wait_recv()
        for cp in cps["sends"]:
            cp.wait_send()
        cps["mine"].wait()


class _SemView:
    def __init__(self, ref, lo):
        self.ref, self.lo = ref, lo

    @property
    def at(self):
        return self

    def __getitem__(self, k):
        return self.ref.at[self.lo + k]


def _call(body, *, name, grid, in_specs, out_specs, out_shape, args, scratch_shapes=(), parallel=True, exchanges=()):
    n_axes = len(grid)
    if not exchanges:
        sem = ("parallel" if parallel else "arbitrary",) * n_axes
        return pl.pallas_call(
            body, name=name, grid=grid, in_specs=in_specs, out_specs=out_specs, out_shape=out_shape,
            scratch_shapes=scratch_shapes, compiler_params=pltpu.CompilerParams(dimension_semantics=sem))(*args)
    single = not isinstance(out_shape, (list, tuple))
    out_specs = [out_specs] if single else list(out_specs)
    out_shape = [out_shape] if single else list(out_shape)
    n_in, n_out, n_scr, n_x = len(in_specs), len(out_specs), len(scratch_shapes), len(exchanges)
    landing = [(e, ex.into) for e, ex in enumerate(exchanges) if ex.into is not None]
    aliases = {n_in + n_x + pos: n_out + e for pos, (e, _) in enumerate(landing)}

    def wrapped(*refs):
        ins, refs = refs[:n_in], refs[n_in:]
        x_in, refs = refs[:n_x], refs[n_x + len(landing):]
        outs, refs = refs[:n_out], refs[n_out:]
        x_out, refs = refs[:n_x], refs[n_x:]
        scr, (send_sems, recv_sems, local_sems) = refs[:n_scr], refs[n_scr:]
        ids = [pl.program_id(k) for k in range(n_axes)]
        first = functools.reduce(jnp.logical_and, [i == 0 for i in ids])
        last = functools.reduce(jnp.logical_and, [i == g - 1 for i, g in zip(ids, grid)])

        def sems(e):
            lo = e * SEMS_PER_EXCHANGE
            return _SemView(send_sems, lo), _SemView(recv_sems, lo), local_sems.at[e]

        @pl.when(first)
        def _():
            for e, ex in enumerate(exchanges):
                ex.start(x_in[e], x_out[e], *sems(e))

        body(*ins, *outs, *scr)

        @pl.when(last)
        def _():
            for e, ex in enumerate(exchanges):
                ex.finish(x_in[e], x_out[e], *sems(e))

    res = pl.pallas_call(
        wrapped, name=name, grid=grid, in_specs=list(in_specs) + [_ANY] * (n_x + len(landing)),
        out_specs=out_specs + [_ANY] * n_x, out_shape=out_shape + [ex.out_shape for ex in exchanges],
        input_output_aliases=aliases,
        scratch_shapes=list(scratch_shapes) + [pltpu.SemaphoreType.DMA((n_x * SEMS_PER_EXCHANGE,)),
                                               pltpu.SemaphoreType.DMA((n_x * SEMS_PER_EXCHANGE,)),
                                               pltpu.SemaphoreType.DMA((n_x,))],
        compiler_params=pltpu.CompilerParams(dimension_semantics=("arbitrary",) * n_axes),
    )(*args, *[ex.src for ex in exchanges], *[buf for _, buf in landing])
    outs, x_outs = res[:n_out], res[n_out:]
    return (outs[0] if single else outs), x_outs


def _exchange_only(exchanges, *, name):
    def body():
        pass

    return _call(body, name=name, grid=(1,), in_specs=[], out_specs=[], out_shape=[], args=[],
                 exchanges=exchanges)[1]


def _shift_dn(x, d):
    rolled = pltpu.roll(x, d, 0)
    if x.shape[0] <= SUBLANE or d >= SUBLANE:
        row = lax.broadcasted_iota(jnp.int32, x.shape, 0)
        return jnp.where(row >= d, rolled, 0.0)
    row = lax.broadcasted_iota(jnp.int32, (SUBLANE, x.shape[1]), 0)
    return jnp.concatenate([jnp.where(row >= d, rolled[:SUBLANE], 0.0), rolled[SUBLANE:]], axis=0)


def _shift_up(x, d):
    n = x.shape[0]
    rolled = pltpu.roll(x, n - d, 0)
    if n <= SUBLANE or d >= SUBLANE:
        row = lax.broadcasted_iota(jnp.int32, x.shape, 0)
        return jnp.where(row < n - d, rolled, 0.0)
    row = lax.broadcasted_iota(jnp.int32, (SUBLANE, x.shape[1]), 0)
    return jnp.concatenate([rolled[:n - SUBLANE], jnp.where(row < SUBLANE - d, rolled[n - SUBLANE:], 0.0)], axis=0)


def _gelu(x):
    return 0.5 * x * (1.0 + jnp.tanh(GELU_K * (x + GELU_C * x * x * x)))


def _gelu_grad(x):
    t = jnp.tanh(GELU_K * (x + GELU_C * x * x * x))
    return 0.5 * (1.0 + t) + 0.5 * x * (1.0 - t * t) * (GELU_K * (1.0 + 3.0 * GELU_C * x * x))


def _sigmoid(x):
    return 0.5 + 0.5 * jnp.tanh(0.5 * x)


def _conv3(x, w_ref):
    return w_ref[0:1, :] * _shift_dn(x, 2) + w_ref[1:2, :] * _shift_dn(x, 1) + w_ref[2:3, :] * x


def _conv3_bwd_x(dy, w_ref):
    return w_ref[2:3, :] * dy + w_ref[1:2, :] * _shift_up(dy, 1) + w_ref[0:1, :] * _shift_up(dy, 2)


def _conv3_bwd_w(dy, x):
    return jnp.concatenate([
        jnp.sum(dy * _shift_dn(x, 2), axis=0, keepdims=True),
        jnp.sum(dy * _shift_dn(x, 1), axis=0, keepdims=True),
        jnp.sum(dy * x, axis=0, keepdims=True)], axis=0)


def _dot(a, b, dims):
    return lax.dot_general(a.astype(BF16), b.astype(BF16), (dims, ((), ())), preferred_element_type=F32)


_NN = ((1,), (0,))
_NT = ((1,), (1,))
_TN = ((0,), (0,))


def _tiles(name, m, n, default):
    tm, tn = TILES.get(name, default)
    return math.gcd(tm, m), math.gcd(tn, n)


def _mm(a, b, mode, *, name, out_dtype=F32, add=None, norm=None, norm_bwd=None, tm_cap=512, tn_cap=1536,
        exchanges=()):
    halves = None
    if mode == "tn":
        r, m = a.shape
        n = b.shape[-1] * (2 if b.ndim == 3 else 1)
        tm, tn = _tiles(name, m, n, (_pick_tile(m, 256, LANE), _pick_tile(n, tn_cap, LANE)))
        if b.ndim == 3:
            per_half = b.shape[-1] // tn
            b_spec = pl.BlockSpec((None, r, tn), lambda i, j: (j // per_half, 0, j % per_half))
        else:
            b_spec = pl.BlockSpec((r, tn), lambda i, j: (0, j))
        in_specs = [pl.BlockSpec((r, tm), lambda i, j: (0, i)), b_spec]
        dims = _TN
    elif mode == "nt" and a.ndim == 3:
        _, m, halves = a.shape
        n = b.shape[0]
        tm, tn = _tiles(name, m, n, (_pick_tile(m, tm_cap, SUBLANE), _pick_tile(n, tn_cap, LANE)))
        in_specs = [pl.BlockSpec((2, tm, halves), lambda i, j: (0, i, 0)),
                    pl.BlockSpec((tn, 2 * halves), lambda i, j: (j, 0))]
        dims = _NT
    elif mode == "nn":
        m, k = a.shape
        n = b.shape[1]
        tm, tn = _tiles(name, m, n, (_pick_tile(m, tm_cap, SUBLANE), _pick_tile(n, tn_cap, LANE)))
        in_specs = [pl.BlockSpec((tm, k), lambda i, j: (i, 0)), pl.BlockSpec((k, tn), lambda i, j: (0, j))]
        dims = _NN
    else:
        m, k = a.shape
        n = b.shape[0]
        tm, tn = _tiles(name, m, n, (_pick_tile(m, tm_cap, SUBLANE), _pick_tile(n, tn_cap, LANE)))
        in_specs = [pl.BlockSpec((tm, k), lambda i, j: (i, 0)), pl.BlockSpec((tn, k), lambda i, j: (j, 0))]
        dims = _NT
    assert m % tm == 0 and n % tn == 0, (name, m, n, tm, tn)
    args = [a, b]
    tile = pl.BlockSpec((tm, tn), lambda i, j: (i, j))
    if add is not None:
        in_specs.append(tile)
        args.append(add)
    out_specs, out_shape = tile, jax.ShapeDtypeStruct((m, n), out_dtype)
    if norm is not None:
        gains, layer = norm
        assert tn == n
        in_specs.append(pl.BlockSpec((None, 1, n), lambda i, j: (layer, 0, 0)))
        args.append(gains.reshape(gains.shape[0], 1, n))
        out_specs, out_shape = [tile, tile], [out_shape, jax.ShapeDtypeStruct((m, n), BF16)]
    if norm_bwd is not None:
        x_in, gains, layer, res = norm_bwd
        assert tn == n and add is None and norm is None
        vec = pl.BlockSpec((None, 1, n), lambda i, j: (layer, 0, 0))
        in_specs += [tile, vec, tile]
        args += [x_in, gains.reshape(gains.shape[0], 1, n), res]
        out_specs = [tile, tile, pl.BlockSpec((1, n), lambda i, j: (0, 0))]
        out_shape = [jax.ShapeDtypeStruct((m, n), F32), jax.ShapeDtypeStruct((m, n), BF16),
                     jax.ShapeDtypeStruct((1, n), F32)]

    def body(*refs):
        if halves is None:
            acc = _dot(refs[0][...], refs[1][...], dims)
        else:
            acc = (_dot(refs[0][0], refs[1][:, :halves], dims) + _dot(refs[0][1], refs[1][:, halves:], dims))
        if add is not None:
            acc = acc + refs[2][...]
        if norm_bwd is not None:
            x_ref, g_ref, res_ref, dx_ref, dxb_ref, dg_ref = refs[2:]

            @pl.when(pl.program_id(0) == 0)
            def _():
                dg_ref[...] = jnp.zeros_like(dg_ref)
            dx, xn = _rms_bwd_rows(acc, x_ref[...], g_ref[...])
            dx = dx + res_ref[...]
            dx_ref[...] = dx
            dxb_ref[...] = dx.astype(BF16)
            dg_ref[...] += jnp.sum(acc * xn, axis=0, keepdims=True)
        elif norm is None:
            refs[-1][...] = acc.astype(out_dtype)
        else:
            refs[-2][...] = acc.astype(out_dtype)
            r = lax.rsqrt(jnp.mean(acc * acc, axis=-1, keepdims=True) + EPS)
            refs[-1][...] = (acc * r * refs[-3][...]).astype(BF16)

    return _call(body, name=name, grid=(m // tm, n // tn), in_specs=in_specs, out_specs=out_specs,
                 out_shape=out_shape, args=args, parallel=norm_bwd is None, exchanges=exchanges)


def _rms_fwd(x, g, *, name):
    l, d = x.shape
    tl = _pick_tile(l, 512, SUBLANE)

    def body(x_ref, g_ref, h_ref):
        xv = x_ref[...]
        r = lax.rsqrt(jnp.mean(xv * xv, axis=-1, keepdims=True) + EPS)
        h_ref[...] = (xv * r * g_ref[...]).astype(BF16)

    return pl.pallas_call(
        body, name=name, grid=(l // tl,),
        in_specs=[pl.BlockSpec((tl, d), lambda i: (i, 0)), pl.BlockSpec((1, d), lambda i: (0, 0))],
        out_specs=pl.BlockSpec((tl, d), lambda i: (i, 0)),
        out_shape=jax.ShapeDtypeStruct((l, d), BF16), compiler_params=_PAR)(x, g)


def _rms_bwd_rows(dh, xv, g):
    r = lax.rsqrt(jnp.mean(xv * xv, axis=-1, keepdims=True) + EPS)
    a = dh * g
    m = jnp.mean(a * xv, axis=-1, keepdims=True)
    return r * a - xv * (r * r * r) * m, xv * r


def _rms_bwd(dh, x, g, res, *, name):
    l, d = x.shape
    tl = _pick_tile(l, 512, SUBLANE)

    def body(dh_ref, x_ref, g_ref, res_ref, dx_ref, dxb_ref, dg_ref):
        @pl.when(pl.program_id(0) == 0)
        def _():
            dg_ref[...] = jnp.zeros_like(dg_ref)
        dhv = dh_ref[...]
        dx, xn = _rms_bwd_rows(dhv, x_ref[...], g_ref[...])
        dx = dx + res_ref[...]
        dx_ref[...] = dx
        dxb_ref[...] = dx.astype(BF16)
        dg_ref[...] += jnp.sum(dhv * xn, axis=0, keepdims=True)

    row = pl.BlockSpec((tl, d), lambda i: (i, 0))
    vec = pl.BlockSpec((1, d), lambda i: (0, 0))
    return pl.pallas_call(
        body, name=name, grid=(l // tl,), in_specs=[row, row, vec, row], out_specs=[row, row, vec],
        out_shape=[jax.ShapeDtypeStruct((l, d), F32), jax.ShapeDtypeStruct((l, d), BF16),
                   jax.ShapeDtypeStruct((1, d), F32)],
        compiler_params=_ARB)(dh, x, g, res)


def _loss_head(x, g, tgt, *, name):
    l, d = x.shape
    tl = _pick_tile(l, 512, SUBLANE)

    def body(x_ref, g_ref, t_ref, loss_ref, dx_ref, dxb_ref, dg_ref):
        @pl.when(pl.program_id(0) == 0)
        def _():
            dg_ref[...] = jnp.zeros_like(dg_ref)
            loss_ref[...] = jnp.zeros_like(loss_ref)
        xv = x_ref[...]
        gv = g_ref[...]
        r = lax.rsqrt(jnp.mean(xv * xv, axis=-1, keepdims=True) + EPS)
        err = xv * r * gv - t_ref[...]
        row_loss = jnp.sum(err * err, axis=-1, keepdims=True) * (0.5 / d)
        loss_ref[...] += jnp.sum(row_loss, axis=0, keepdims=True)
        dy = err * (1.0 / d)
        dx, xn = _rms_bwd_rows(dy, xv, gv)
        dx_ref[...] = dx
        dxb_ref[...] = dx.astype(BF16)
        dg_ref[...] += jnp.sum(dy * xn, axis=0, keepdims=True)

    row = pl.BlockSpec((tl, d), lambda i: (i, 0))
    vec = pl.BlockSpec((1, d), lambda i: (0, 0))
    one = pl.BlockSpec((1, 1), lambda i: (0, 0))
    return pl.pallas_call(
        body, name=name, grid=(l // tl,), in_specs=[row, vec, row], out_specs=[one, row, row, vec],
        out_shape=[jax.ShapeDtypeStruct((1, 1), F32), jax.ShapeDtypeStruct((l, d), F32),
                   jax.ShapeDtypeStruct((l, d), BF16), jax.ShapeDtypeStruct((1, d), F32)],
        compiler_params=_ARB)(x, g, tgt)


def _ffn_act(up, cw, cb, *, name, exchanges=()):
    l = up.shape[0]
    nb = D_FF // LANE

    def body(ug_ref, uv_ref, wg_ref, wv_ref, bg_ref, bv_ref, o_ref, gv_ref):
        gc = _conv3(ug_ref[...], wg_ref) + bg_ref[...]
        vc = _conv3(uv_ref[...], wv_ref) + bv_ref[...]
        gv_ref[0] = gc
        gv_ref[1] = vc
        o_ref[...] = (gc * _sigmoid(gc) * vc).astype(BF16)

    col = lambda off: pl.BlockSpec((l, LANE), lambda j: (0, j + off))
    w3 = lambda off: pl.BlockSpec((CONV_WIDTH, LANE), lambda j: (0, j + off))
    b1 = lambda off: pl.BlockSpec((1, LANE), lambda j: (0, j + off))
    return _call(body, name=name, grid=(nb,), in_specs=[col(0), col(nb), w3(0), w3(nb), b1(0), b1(nb)],
                 out_specs=[col(0), pl.BlockSpec((2, l, LANE), lambda j: (0, 0, j))],
                 out_shape=[jax.ShapeDtypeStruct((l, D_FF), BF16), jax.ShapeDtypeStruct((2, l, D_FF), F32)],
                 args=[up, up, cw, cw, cb, cb], exchanges=exchanges)


def _ffn_act_bwd(up, gv, dact, cw, *, name, exchanges=()):
    l = up.shape[0]
    nb = D_FF // LANE

    def half_bwd(k, dc, x, w_ref, dup_ref, dcw_ref, dcb_ref):
        d1, d2 = _shift_up(dc, 1), _shift_up(dc, 2)
        dcb_ref[k] = jnp.sum(dc, axis=0, keepdims=True)
        dcw_ref[k] = jnp.concatenate([jnp.sum(d2 * x, axis=0, keepdims=True),
                                      jnp.sum(d1 * x, axis=0, keepdims=True),
                                      jnp.sum(dc * x, axis=0, keepdims=True)], axis=0)
        dup_ref[k] = (w_ref[2:3, :] * dc + w_ref[1:2, :] * d1 + w_ref[0:1, :] * d2).astype(BF16)

    def body(ug_ref, uv_ref, gv_ref, da_ref, wg_ref, wv_ref, dup_ref, dcw_ref, dcb_ref):
        gc, vc, da = gv_ref[0], gv_ref[1], da_ref[...]
        sg = _sigmoid(gc)
        half_bwd(0, da * vc * (sg * (1.0 + gc * (1.0 - sg))), ug_ref[...], wg_ref, dup_ref, dcw_ref, dcb_ref)
        half_bwd(1, da * (gc * sg), uv_ref[...], wv_ref, dup_ref, dcw_ref, dcb_ref)

    col = lambda off: pl.BlockSpec((l, LANE), lambda j: (0, j + off))
    w3 = lambda off: pl.BlockSpec((CONV_WIDTH, LANE), lambda j: (0, j + off))
    both = lambda rows: pl.BlockSpec((2, rows, LANE), lambda j: (0, 0, j))
    res = _call(
        body, name=name, grid=(nb,),
        in_specs=[col(0), col(nb), both(l), col(0), w3(0), w3(nb)],
        out_specs=[both(l), both(CONV_WIDTH), both(1)],
        out_shape=[jax.ShapeDtypeStruct((2, l, D_FF), BF16), jax.ShapeDtypeStruct((2, CONV_WIDTH, D_FF), F32),
                   jax.ShapeDtypeStruct((2, 1, D_FF), F32)],
        args=[up, up, gv, dact, cw, cw], exchanges=exchanges)
    (dup, dcw, dcb), moved = res if exchanges else (res, None)
    outs = [dup, jnp.concatenate([dcw[0], dcw[1]], axis=1), jnp.concatenate([dcb[0], dcb[1]], axis=1)]
    return (outs, moved) if exchanges else outs


def _sconv_fwd(proj, cw, *, name):
    l = proj.shape[0]
    nb = D_HALF // LANE

    def body(xa_ref, ba_ref, ca_ref, w_ref, o_ref):
        o_ref[...] = (ba_ref[...] * _conv3(ca_ref[...] * xa_ref[...], w_ref)).astype(BF16)

    col = lambda off: pl.BlockSpec((l, LANE), lambda j: (0, j + off))
    return pl.pallas_call(
        body, name=name, grid=(nb,),
        in_specs=[col(0), col(nb), col(2 * nb), pl.BlockSpec((CONV_WIDTH, LANE), lambda j: (0, j))],
        out_specs=col(0), out_shape=jax.ShapeDtypeStruct((l, 2 * D_HALF), BF16),
        compiler_params=_PAR)(proj, proj, proj, cw)


def _sconv_bwd(proj, dcat, cw, *, name):
    l = proj.shape[0]
    nb = D_HALF // LANE

    def body(xa_ref, ba_ref, ca_ref, dy_ref, w_ref, dxa_ref, dba_ref, dca_ref, dw_ref):
        xa, ba, ca, dy = xa_ref[...], ba_ref[...], ca_ref[...], dy_ref[...]
        q = ca * xa
        dba_ref[...] = (dy * _conv3(q, w_ref)).astype(BF16)
        dconv = dy * ba
        d1, d2 = _shift_up(dconv, 1), _shift_up(dconv, 2)
        dw_ref[...] = jnp.concatenate([jnp.sum(d2 * q, axis=0, keepdims=True), jnp.sum(d1 * q, axis=0, keepdims=True),
                                       jnp.sum(dconv * q, axis=0, keepdims=True)], axis=0)
        dq = w_ref[2:3, :] * dconv + w_ref[1:2, :] * d1 + w_ref[0:1, :] * d2
        dxa_ref[...] = (dq * ca).astype(BF16)
        dca_ref[...] = (dq * xa).astype(BF16)

    col = lambda off: pl.BlockSpec((l, LANE), lambda j: (0, j + off))
    w3 = pl.BlockSpec((CONV_WIDTH, LANE), lambda j: (0, j))
    piece = jax.ShapeDtypeStruct((l, D_HALF), BF16)
    return pl.pallas_call(
        body, name=name, grid=(nb,),
        in_specs=[col(0), col(nb), col(2 * nb), col(0), w3],
        out_specs=[col(0), col(0), col(0), w3],
        out_shape=[piece, piece, piece, jax.ShapeDtypeStruct((CONV_WIDTH, D_HALF), F32)],
        compiler_params=_PAR)(proj, proj, proj, dcat, cw)


def _s5_prep(log_step, a_re, a_im, b_re, b_im):
    step = jnp.exp(log_step)[:, None]
    mag = jnp.exp(a_re * step)
    lr = mag * jnp.cos(a_im * step)
    li = mag * jnp.sin(a_im * step)
    nr = lr - 1.0
    den = a_re * a_re + a_im * a_im
    qr = (nr * a_re + li * a_im) / den
    qi = (li * a_re - nr * a_im) / den
    br = qr[..., None] * b_re - qi[..., None] * b_im
    bi = qr[..., None] * b_im + qi[..., None] * b_re
    return lr, li, br, bi


def _block_diag(m):
    nb, ng, r, c = m.shape
    eye = jnp.eye(ng, dtype=m.dtype)
    return jnp.einsum("bgrc,gh->bgrhc", m, eye).reshape(nb, ng * r, ng * c)


def _block_diag_extract(w, r, c):
    nb = w.shape[0]
    ng = w.shape[1] // r
    w5 = w.reshape(nb, ng, r, ng, c)
    return jnp.einsum("bgrhc,gh->bgrc", w5, jnp.eye(ng, dtype=w.dtype))


def _s5_mats(br, bi, c_re, c_im):
    g8 = N_SSM_GROUPS // S5_LANE_BLOCKS
    to_blk = lambda m: m.reshape(S5_LANE_BLOCKS, g8, m.shape[1], m.shape[2])
    wb = jnp.concatenate([_block_diag(to_blk(jnp.swapaxes(br, 1, 2))),
                          _block_diag(to_blk(jnp.swapaxes(bi, 1, 2)))], axis=2)
    wc = jnp.concatenate([_block_diag(to_blk(jnp.swapaxes(c_re, 1, 2))),
                          _block_diag(to_blk(jnp.swapaxes(-c_im, 1, 2)))], axis=1)
    return wb, wc


def _s5_mats_bwd(dwb, dwc):
    g, p, h = N_SSM_GROUPS, SSM_STATE, SSM_GROUP
    half = S5_STATE_LANES
    dbr = jnp.swapaxes(_block_diag_extract(dwb[:, :, :half], h, p).reshape(g, h, p), 1, 2)
    dbi = jnp.swapaxes(_block_diag_extract(dwb[:, :, half:], h, p).reshape(g, h, p), 1, 2)
    dcr = jnp.swapaxes(_block_diag_extract(dwc[:, :half, :], p, h).reshape(g, p, h), 1, 2)
    dci = -jnp.swapaxes(_block_diag_extract(dwc[:, half:, :], p, h).reshape(g, p, h), 1, 2)
    return dbr, dbi, dcr, dci


def _s5_scan_consts(log_step, a_re, a_im, reverse):
    step = jnp.exp(log_step)[:, None]
    xr = (a_re * step).reshape(S5_LANE_BLOCKS, 1, S5_STATE_LANES)
    xi = (a_im * step).reshape(S5_LANE_BLOCKS, 1, S5_STATE_LANES)
    if reverse:
        xi = -xi
    row = jnp.arange(SUBLANE, dtype=F32).reshape(1, SUBLANE, 1)

    def power(n):
        mag = jnp.exp(n * xr)
        return jnp.concatenate([mag * jnp.cos(n * xi), mag * jnp.sin(n * xi)], axis=-1)

    kinds = []
    for d in (1, 2, 4):
        keep = (row <= SUBLANE - 1 - d) if reverse else (row >= d)
        kinds.append(jnp.where(keep, power(jnp.full_like(row, float(d))), 0.0))
    kinds.append(power((SUBLANE - row) if reverse else (row + 1.0)))
    return jnp.stack(kinds, axis=1)


def _scan_rows(s_ref, sc_ref, carry_ref, n_rows, reverse):
    n_grp = n_rows // SUBLANE
    n_col = S5_STATE_LANES // LANE
    half = S5_STATE_LANES

    def step(i, carry):
        grp = (n_grp - 1 - i) if reverse else i
        r0 = pl.multiple_of(grp * SUBLANE, SUBLANE)
        out = []
        for cb in range(n_col):
            lo, hi = cb * LANE, half + cb * LANE
            re = s_ref[pl.ds(r0, SUBLANE), lo:lo + LANE]
            im = s_ref[pl.ds(r0, SUBLANE), hi:hi + LANE]
            for k, d in enumerate((1, 2, 4)):
                sh = (SUBLANE - d) if reverse else d
                rr, ri = pltpu.roll(re, sh, 0), pltpu.roll(im, sh, 0)
                ar, ai = sc_ref[k, :, lo:lo + LANE], sc_ref[k, :, hi:hi + LANE]
                re, im = re + (ar * rr - ai * ri), im + (ar * ri + ai * rr)
            pr, pi = sc_ref[3, :, lo:lo + LANE], sc_ref[3, :, hi:hi + LANE]
            cr, ci = carry[2 * cb], carry[2 * cb + 1]
            re, im = re + (pr * cr - pi * ci), im + (pr * ci + pi * cr)
            s_ref[pl.ds(r0, SUBLANE), lo:lo + LANE] = re
            s_ref[pl.ds(r0, SUBLANE), hi:hi + LANE] = im
            edge = 0 if reverse else SUBLANE - 1
            out.append(jnp.broadcast_to(re[edge:edge + 1, :], (SUBLANE, LANE)))
            out.append(jnp.broadcast_to(im[edge:edge + 1, :], (SUBLANE, LANE)))
        return tuple(out)

    init = []
    for cb in range(n_col):
        init.append(carry_ref[:, cb * LANE:(cb + 1) * LANE])
        init.append(carry_ref[:, half + cb * LANE:half + (cb + 1) * LANE])
    fin = lax.fori_loop(0, n_grp, step, tuple(init))
    for cb in range(n_col):
        carry_ref[:, cb * LANE:(cb + 1) * LANE] = fin[2 * cb]
        carry_ref[:, half + cb * LANE:half + (cb + 1) * LANE] = fin[2 * cb + 1]


def _s5_fwd(proj, wb, wc, d_skip, sc, *, name, exchanges=()):
    l = proj.shape[0]
    tt = _pick_tile(l, S5_TIME_CHUNK, SUBLANE)
    u_off = (proj.shape[1] - D_HALF) // LANE
    w2 = 2 * S5_STATE_LANES

    def body(u_ref, wb_ref, wc_ref, d_ref, sc_ref, s_ref, y_ref, carry_ref):
        @pl.when(pl.program_id(1) == 0)
        def _():
            carry_ref[...] = jnp.zeros_like(carry_ref)
        u = u_ref[...]
        s_ref[...] = _dot(u, wb_ref[0], _NN)
        _scan_rows(s_ref, sc_ref.at[0], carry_ref, tt, False)
        y_ref[...] = _dot(s_ref[...], wc_ref[0], _NN) + d_ref[...] * u

    return _call(
        body, name=name, grid=(S5_LANE_BLOCKS, l // tt),
        in_specs=[pl.BlockSpec((tt, LANE), lambda b, t: (t, b + u_off)),
                  pl.BlockSpec((1, LANE, w2), lambda b, t: (b, 0, 0)),
                  pl.BlockSpec((1, w2, LANE), lambda b, t: (b, 0, 0)),
                  pl.BlockSpec((1, LANE), lambda b, t: (0, b)),
                  pl.BlockSpec((1, 4, SUBLANE, w2), lambda b, t: (b, 0, 0, 0))],
        out_specs=[pl.BlockSpec((tt, w2), lambda b, t: (t, b)), pl.BlockSpec((tt, LANE), lambda b, t: (t, b))],
        out_shape=[jax.ShapeDtypeStruct((l, S5_LANE_BLOCKS * w2), F32), jax.ShapeDtypeStruct((l, D_HALF), F32)],
        scratch_shapes=[pltpu.VMEM((SUBLANE, w2), F32)],
        args=[proj, wb, wc, d_skip, sc], parallel=False, exchanges=exchanges)


def _s5_bwd(proj, dy, states, wb, wc, d_skip, sc_rev, *, name, exchanges=()):
    l = proj.shape[0]
    tt = _pick_tile(l, S5_TIME_CHUNK, SUBLANE)
    nt = l // tt
    u_off = (proj.shape[1] - D_HALF) // LANE
    w2 = 2 * S5_STATE_LANES
    half = S5_STATE_LANES
    grp_per_chunk = tt // SUBLANE

    def body(u_ref, dy_ref, s_ref, halo_ref, wb_ref, wc_ref, d_ref, sc_ref,
             du_ref, dwb_ref, dwc_ref, dlam_ref, dd_ref, g_scr, carry_ref):
        t = pl.program_id(1)

        @pl.when(t == 0)
        def _():
            carry_ref[...] = jnp.zeros_like(carry_ref)
            dwb_ref[...] = jnp.zeros_like(dwb_ref)
            dwc_ref[...] = jnp.zeros_like(dwc_ref)
            dlam_ref[...] = jnp.zeros_like(dlam_ref)
            dd_ref[...] = jnp.zeros_like(dd_ref)

        u = u_ref[...]
        dyv = dy_ref[...]
        g_scr[...] = _dot(dyv, wc_ref[0], _NT)
        _scan_rows(g_scr, sc_ref.at[0], carry_ref, tt, True)
        gv = g_scr[...]
        du_ref[...] = (_dot(gv, wb_ref[0], _NT) + d_ref[...] * dyv).astype(BF16)
        dwb_ref[0] += _dot(u, gv, _TN)
        sv = s_ref[...]
        dwc_ref[0] += _dot(sv, dyv, _TN)
        dd_ref[...] += jnp.sum(dyv * u, axis=0, keepdims=True)
        first_chunk = t == nt - 1
        halo = jnp.where(first_chunk, 0.0, halo_ref[SUBLANE - 1:SUBLANE, :])
        row = lax.broadcasted_iota(jnp.int32, sv.shape, 0)
        sp = jnp.where(row == 0, jnp.broadcast_to(halo, sv.shape), pltpu.roll(sv, 1, 0))
        gr, gi = gv[:, :half], gv[:, half:]
        sr, si = sp[:, :half], sp[:, half:]
        dlr = jnp.sum(gr * sr + gi * si, axis=0, keepdims=True)
        dli = jnp.sum(gi * sr - gr * si, axis=0, keepdims=True)
        dlam_ref[0] += jnp.concatenate([dlr, dli], axis=1)

    rev = lambda t: nt - 1 - t
    return _call(
        body, name=name, grid=(S5_LANE_BLOCKS, nt),
        in_specs=[pl.BlockSpec((tt, LANE), lambda b, t: (rev(t), b + u_off)),
                  pl.BlockSpec((tt, LANE), lambda b, t: (rev(t), b)),
                  pl.BlockSpec((tt, w2), lambda b, t: (rev(t), b)),
                  pl.BlockSpec((SUBLANE, w2), lambda b, t: (jnp.maximum(rev(t) * grp_per_chunk - 1, 0), b)),
                  pl.BlockSpec((1, LANE, w2), lambda b, t: (b, 0, 0)),
                  pl.BlockSpec((1, w2, LANE), lambda b, t: (b, 0, 0)),
                  pl.BlockSpec((1, LANE), lambda b, t: (0, b)),
                  pl.BlockSpec((1, 4, SUBLANE, w2), lambda b, t: (b, 0, 0, 0))],
        out_specs=[pl.BlockSpec((tt, LANE), lambda b, t: (rev(t), b)),
                   pl.BlockSpec((1, LANE, w2), lambda b, t: (b, 0, 0)),
                   pl.BlockSpec((1, w2, LANE), lambda b, t: (b, 0, 0)),
                   pl.BlockSpec((1, 1, w2), lambda b, t: (b, 0, 0)),
                   pl.BlockSpec((1, LANE), lambda b, t: (0, b))],
        out_shape=[jax.ShapeDtypeStruct((l, D_HALF), BF16),
                   jax.ShapeDtypeStruct((S5_LANE_BLOCKS, LANE, w2), F32),
                   jax.ShapeDtypeStruct((S5_LANE_BLOCKS, w2, LANE), F32),
                   jax.ShapeDtypeStruct((S5_LANE_BLOCKS, 1, w2), F32),
                   jax.ShapeDtypeStruct((1, D_HALF), F32)],
        scratch_shapes=[pltpu.VMEM((tt, w2), F32), pltpu.VMEM((SUBLANE, w2), F32)],
        args=[proj, dy, states, states, wb, wc, d_skip, sc_rev], parallel=False, exchanges=exchanges)


def _glu_fwd(ypre, wg, bg, cat, *, name):
    l, d = ypre.shape
    tl = _pick_tile(l, 512, SUBLANE)

    def body(y_ref, w_ref, b_ref, cat_ref, o_ref):
        yg = _gelu(y_ref[...])
        o_ref[...] = (yg * _sigmoid(_dot(yg, w_ref[...], _NN) + b_ref[...])).astype(BF16)

    row = pl.BlockSpec((tl, d), lambda i: (i, 0))
    return pl.pallas_call(
        body, name=name, grid=(l // tl,),
        in_specs=[row, pl.BlockSpec((d, d), lambda i: (0, 0)), pl.BlockSpec((1, d), lambda i: (0, 0)), _ANY],
        out_specs=pl.BlockSpec((tl, d), lambda i: (i, 1)), out_shape=jax.ShapeDtypeStruct(cat.shape, cat.dtype),
        input_output_aliases={3: 0}, compiler_params=_PAR)(ypre, wg, bg, cat)


def _glu_bwd(dcat, ypre, wg, bg, *, name):
    l, d = ypre.shape
    tl = _pick_tile(l, 512, SUBLANE)

    def body(dy_ref, y_ref, w_ref, b_ref, dyp_ref, dw_ref, db_ref):
        @pl.when(pl.program_id(0) == 0)
        def _():
            dw_ref[...] = jnp.zeros_like(dw_ref)
            db_ref[...] = jnp.zeros_like(db_ref)
        yp = y_ref[...]
        dyb = dy_ref[...]
        yg = _gelu(yp)
        sg = _sigmoid(_dot(yg, w_ref[...], _NN) + b_ref[...])
        dz = dyb * yg * sg * (1.0 - sg)
        dyg = dyb * sg + _dot(dz, w_ref[...], _NT)
        dyp_ref[...] = dyg * _gelu_grad(yp)
        dw_ref[...] += _dot(yg, dz, _TN)
        db_ref[...] += jnp.sum(dz, axis=0, keepdims=True)

    row = pl.BlockSpec((tl, d), lambda i: (i, 0))
    mat = pl.BlockSpec((d, d), lambda i: (0, 0))
    vec = pl.BlockSpec((1, d), lambda i: (0, 0))
    return pl.pallas_call(
        body, name=name, grid=(l // tl,),
        in_specs=[pl.BlockSpec((tl, d), lambda i: (i, 1)), row, mat, vec], out_specs=[row, mat, vec],
        out_shape=[jax.ShapeDtypeStruct((l, d), F32), jax.ShapeDtypeStruct((d, d), F32),
                   jax.ShapeDtypeStruct((1, d), F32)],
        compiler_params=_ARB)(dcat, ypre, wg, bg)


def _window_sum(x, w, trailing):
    s, d = x, 1
    while d < w:
        s = s + (_shift_dn(s, d) if trailing else _shift_up(s, d))
        d *= 2
    return s


def _window_count(shape, w):
    row = lax.broadcasted_iota(jnp.int32, shape, 0)
    return jnp.minimum(row + 1, w).astype(F32)


def _pool_fwd(proj, pw, scale, *, name):
    l = proj.shape[0]
    ng = len(POOL_WINDOWS)

    def body(z_ref, w_ref, sc_ref, o_ref):
        z = z_ref[...]
        for k, w in enumerate(POOL_WINDOWS):
            @pl.when(pl.program_id(0) == k)
            def _():
                pooled = _window_sum(z, w, True) / _window_count(z.shape, w) - z
                o_ref[...] = (_dot(pooled, w_ref[0], _NN) * sc_ref[...]).astype(BF16)

    col = pl.BlockSpec((l, LANE), lambda g: (0, g))
    return pl.pallas_call(
        body, name=name, grid=(ng,),
        in_specs=[col, pl.BlockSpec((1, LANE, LANE), lambda g: (g, 0, 0)), pl.BlockSpec((1, LANE), lambda g: (0, g))],
        out_specs=col, out_shape=jax.ShapeDtypeStruct((l, 2 * D_HALF), BF16), compiler_params=_PAR)(proj, pw, scale)


def _pool_bwd(proj, dcat, pw, scale, *, name):
    l = proj.shape[0]
    ng = len(POOL_WINDOWS)

    def body(z_ref, dy_ref, w_ref, sc_ref, dz_ref, dw_ref, dsc_ref):
        z = z_ref[...]
        dy = dy_ref[...]
        for k, w in enumerate(POOL_WINDOWS):
            @pl.when(pl.program_id(0) == k)
            def _():
                cnt = _window_count(z.shape, w)
                pooled = _window_sum(z, w, True) / cnt - z
                ypre = _dot(pooled, w_ref[0], _NN)
                dsc_ref[...] = jnp.sum(dy * ypre, axis=0, keepdims=True)
                dyp = dy * sc_ref[...]
                dw_ref[0] = _dot(pooled, dyp, _TN)
                dpool = _dot(dyp, w_ref[0], _NT)
                dz_ref[...] = (_window_sum(dpool / cnt, w, False) - dpool).astype(BF16)

    col = pl.BlockSpec((l, LANE), lambda g: (0, g))
    mat = pl.BlockSpec((1, LANE, LANE), lambda g: (g, 0, 0))
    vec = pl.BlockSpec((1, LANE), lambda g: (0, g))
    return pl.pallas_call(
        body, name=name, grid=(ng,), in_specs=[col, col, mat, vec], out_specs=[col, mat, vec],
        out_shape=[jax.ShapeDtypeStruct((l, D_HALF), BF16), jax.ShapeDtypeStruct((ng, LANE, LANE), F32),
                   jax.ShapeDtypeStruct((1, D_HALF), F32)],
        compiler_params=_PAR)(proj, dcat, pw, scale)


def _tril_mask():
    r = lax.broadcasted_iota(jnp.int32, (CHUNK, CHUNK), 0)
    c = lax.broadcasted_iota(jnp.int32, (CHUNK, CHUNK), 1)
    return r >= c


def _sgu_fwd(proj, ng, sw, sb_t, cat, *, name):
    l = proj.shape[0]
    tl = _pick_tile(l, 512, CHUNK)

    def body(su_ref, sv_ref, g_ref, w_ref, b_ref, cat_ref, o_ref):
        su = _gelu(su_ref[...])
        sv = _gelu(sv_ref[...])
        r = lax.rsqrt(jnp.mean(sv * sv, axis=-1, keepdims=True) + EPS)
        v = sv * r * g_ref[...]
        mask = _tril_mask()
        for h in range(SGU_HEADS):
            wm = jnp.where(mask, w_ref[h], 0.0)
            cs = slice(h * LANE, (h + 1) * LANE)
            for n in range(tl // CHUNK):
                rs = slice(n * CHUNK, (n + 1) * CHUNK)
                mixed = _dot(wm, v[rs, cs], _NN) + b_ref[:, h:h + 1]
                o_ref[rs, cs] = (su[rs, cs] * mixed).astype(BF16)

    blk = lambda c: pl.BlockSpec((tl, D_HALF), lambda i: (i, c))
    return pl.pallas_call(
        body, name=name, grid=(l // tl,),
        in_specs=[blk(1), blk(2), pl.BlockSpec((1, D_HALF), lambda i: (0, 0)),
                  pl.BlockSpec((SGU_HEADS, CHUNK, CHUNK), lambda i: (0, 0, 0)),
                  pl.BlockSpec((CHUNK, SGU_HEADS), lambda i: (0, 0)), _ANY],
        out_specs=blk(1), out_shape=jax.ShapeDtypeStruct(cat.shape, cat.dtype), input_output_aliases={5: 0},
        compiler_params=_PAR)(proj, proj, ng, sw, sb_t, cat)


def _sgu_bwd(proj, dcat, ng, sw, sb_t, *, name):
    l = proj.shape[0]
    tl = _pick_tile(l, 512, CHUNK)

    def body(su_ref, sv_ref, dy_ref, g_ref, w_ref, b_ref, dsu_ref, dsv_ref, dw_ref, dbm_ref, dng_ref, dv_scr):
        @pl.when(pl.program_id(0) == 0)
        def _():
            dw_ref[...] = jnp.zeros_like(dw_ref)
            dbm_ref[...] = jnp.zeros_like(dbm_ref)
            dng_ref[...] = jnp.zeros_like(dng_ref)
        su_pre = su_ref[...]
        sv_pre = sv_ref[...]
        su = _gelu(su_pre)
        sv = _gelu(sv_pre)
        gsu = _gelu_grad(su_pre)
        gv = g_ref[...]
        r = lax.rsqrt(jnp.mean(sv * sv, axis=-1, keepdims=True) + EPS)
        v = sv * r * gv
        dy = dy_ref[...]
        mask = _tril_mask()
        for h in range(SGU_HEADS):
            wm = jnp.where(mask, w_ref[h], 0.0)
            cs = slice(h * LANE, (h + 1) * LANE)
            dw_acc = jnp.zeros((CHUNK, CHUNK), F32)
            db_acc = jnp.zeros((CHUNK, LANE), F32)
            for n in range(tl // CHUNK):
                rs = slice(n * CHUNK, (n + 1) * CHUNK)
                vb = v[rs, cs]
                mixed = _dot(wm, vb, _NN) + b_ref[:, h:h + 1]
                dyb = dy[rs, cs]
                dsu_ref[rs, cs] = (dyb * mixed * gsu[rs, cs]).astype(BF16)
                dmix = dyb * su[rs, cs]
                db_acc = db_acc + dmix
                dw_acc = dw_acc + _dot(dmix, vb, _NT)
                dv_scr[rs, cs] = _dot(wm, dmix, _TN)
            dw_ref[h] += jnp.where(mask, dw_acc, 0.0)
            dbm_ref[h] += db_acc
        dv = dv_scr[...]
        a = dv * gv
        m = jnp.mean(a * sv, axis=-1, keepdims=True)
        dsv = r * a - sv * (r * r * r) * m
        dng_ref[...] += jnp.sum(dv * sv * r, axis=0, keepdims=True)
        dsv_ref[...] = (dsv * _gelu_grad(sv_pre)).astype(BF16)

    blk = lambda c: pl.BlockSpec((tl, D_HALF), lambda i: (i, c))
    mats = pl.BlockSpec((SGU_HEADS, CHUNK, CHUNK), lambda i: (0, 0, 0))
    vec = pl.BlockSpec((1, D_HALF), lambda i: (0, 0))
    piece = jax.ShapeDtypeStruct((l, D_HALF), BF16)
    mshape = jax.ShapeDtypeStruct((SGU_HEADS, CHUNK, CHUNK), F32)
    return pl.pallas_call(
        body, name=name, grid=(l // tl,),
        in_specs=[blk(1), blk(2), blk(1), vec, mats, pl.BlockSpec((CHUNK, SGU_HEADS), lambda i: (0, 0))],
        out_specs=[blk(0), blk(0), mats, mats, vec],
        out_shape=[piece, piece, mshape, mshape, jax.ShapeDtypeStruct((1, D_HALF), F32)],
        scratch_shapes=[pltpu.VMEM((tl, D_HALF), F32)],
        compiler_params=_ARB)(proj, proj, dcat, ng, sw, sb_t)


def _s5_params(w):
    prep_args = (w["ssm_log_step"], w["ssm_a_re"], w["ssm_a_im"], w["ssm_b_re"], w["ssm_b_im"])
    (lr, li, br, bi), prep_vjp = jax.vjp(jax.vmap(_s5_prep), *prep_args)
    wb, wc = jax.vmap(_s5_mats)(br, bi, w["ssm_c_re"], w["ssm_c_im"])
    consts = lambda reverse: jax.vmap(functools.partial(_s5_scan_consts, reverse=reverse))(*prep_args[:3])
    return dict(wb=wb.astype(BF16), wc=wc.astype(BF16), d=w["ssm_d"][:, None, :], sc=consts(False),
                sc_rev=consts(True), prep_vjp=prep_vjp)


def _s5_param_grads(s5, dwb, dwc, dlam, dd):
    dbr, dbi, dcr, dci = jax.vmap(_s5_mats_bwd)(dwb, dwc)
    n = dlam.shape[0]
    dlr = dlam[:, :, 0, :S5_STATE_LANES].reshape(n, N_SSM_GROUPS, SSM_STATE)
    dli = dlam[:, :, 0, S5_STATE_LANES:].reshape(n, N_SSM_GROUPS, SSM_STATE)
    dls, dar, dai, db_re, db_im = s5["prep_vjp"]((dlr, dli, dbr, dbi))
    return dict(ssm_log_step=dls, ssm_a_re=dar, ssm_a_im=dai, ssm_b_re=db_re, ssm_b_im=db_im, ssm_c_re=dcr,
                ssm_c_im=dci, ssm_d=dd[:, 0, :])


TILES = {
    "mm_up": (4096, 512), "mm_up_dw": (1024, 1408), "mm_down_dx": (1024, 2816),
    "mm_down": (512, 1024), "mm_down_dw": (256, 1024), "mm_even_in": (2048, 1024), "mm_odd_in": (2048, 768),
    "mm_mix_out": (1024, 1024), "mm_mix_out_dx": (2048, 1024), "mm_mix_out_dw": (1024, 512),
    "mm_even_in_dw": (1024, 512), "mm_odd_in_dw": (1024, 512),
}


def _layer_weights(i):
    j = i // 2
    mixer = [("even_w_in", j), ("even_w_out", j), ("ssm_glu_w", j)] if i % 2 == 0 else [("odd_w_in", j),
                                                                                         ("odd_w_out", j)]
    return dict(w_in=mixer[0], w_out=mixer[1], glu=mixer[2:], up=("ffn_w_up", i), down=("ffn_w_down", i))


class _LocalWeights:
    def __init__(self, w):
        self.w, self.grads = w, {}

    def carried_by(self, stage, i):
        return []

    def delivered(self, stage, i, outs):
        pass

    def weight(self, key):
        return self.w[key[0]][key[1]]

    def grad(self, key, dw):
        self.grads[key] = dw


class _ShardedWeights:
    def __init__(self, shards):
        self.shards = shards
        self.full, self.halves, self.pending, self.scattered = {}, {}, {}, {}

    def start(self, others):
        keys = [_layer_weights(0)["w_in"]]
        outs = _exchange_only(others + [self._gather(k) for k in keys], name="ag_first")
        self._take(keys, outs[len(others):])
        return outs[:len(others)]

    def _gather(self, key):
        shard = self.shards[key[0]][key[1]]
        if len(key) == 2:
            return _Gather(shard)
        rows = shard.shape[0] // 2
        buf = self.halves.get(key[:2])
        if buf is None:
            buf = lax.empty((N_DEV,) + shard.shape, shard.dtype)
        return _Gather(shard[key[2] * rows:(key[2] + 1) * rows], into=buf, row0=key[2] * rows)

    def _take(self, keys, outs):
        for key, got in zip(keys, outs):
            if len(key) == 3 and key[:2] not in self.halves:
                self.halves[key[:2]] = got
                continue
            if BIG[key[0]] == 2:
                self.full[key[:2]] = jnp.swapaxes(got, 0, 1).reshape(got.shape[1], -1)
            else:
                self.full[key[:2]] = got.reshape(-1, got.shape[2])

    def _plan(self, stage, i):
        cur = _layer_weights(i)
        nxt = _layer_weights(i + 1) if i + 1 < DEPTH else None
        has_scan = lambda k: k % 2 == 0
        none = ([], [])
        up_half = lambda h: ([(*nxt["up"], h)], []) if nxt and not has_scan(i + 1) else none
        return {
            "mm_in": ([cur["w_out"], *cur["glu"]], []) if i == 0 else none,
            "s5_fwd": ([cur["up"]], []),
            "mm_up": ([cur["down"]] + ([nxt["w_in"], nxt["w_out"], *nxt["glu"]] if nxt else []), []),
            "ffn_act": up_half(0),
            "mm_down": up_half(1),
            "ffn_act_bwd": ([], [cur["down"]] + ([nxt["up"]] if nxt and not has_scan(i + 1) else [])),
            "s5_bwd": ([], [cur["up"]]),
            "mm_up_dx": ([], [nxt["w_in"], *nxt["glu"]]) if nxt else none,
            "mm_up_dw": ([], [nxt["w_out"]]) if nxt else none,
            "mm_in_dw": ([], [cur["w_out"], *cur["glu"]]) if i == 0 else none,
            "mm_in_dx": ([], [cur["w_in"]]) if i == 0 else none,
        }[stage]

    def _scatter(self, key):
        name, layer = key
        src = self.pending.pop(key)
        layers, rows, cols = self.shards[name].shape
        if name not in self.scattered:
            self.scattered[name] = lax.empty((N_DEV, layers * rows, cols), src.dtype)
        return _Scatter(src, into=self.scattered[name], row0=layer * rows)

    def carried_by(self, stage, i):
        gather, scatter = self._plan(stage, i)
        return [self._gather(k) for k in gather] + [self._scatter(k) for k in scatter]

    def delivered(self, stage, i, outs):
        gather, scatter = self._plan(stage, i)
        self._take(gather, outs[:len(gather)])
        for (name, _), buf in zip(scatter, outs[len(gather):]):
            self.scattered[name] = buf

    def weight(self, key):
        return self.full[key]

    def grad(self, key, dw):
        self.pending[key] = _to_dest_major(dw, BIG[key[0]] - 1).astype(BF16)

    def finish(self, carrier, others):
        keys = list(self.pending)
        res, outs = carrier(others + [self._scatter(k) for k in keys])
        for (name, _), buf in zip(keys, outs[len(others):]):
            self.scattered[name] = buf
        return res, outs[:len(others)]


def _device_step(x, tgt, w, comm):
    saved = []
    s5 = _s5_params(w)
    h = _rms_fwd(x, w["norm_mix_g"][0:1], name="rms_fwd")
    for i in range(DEPTH):
        j = i // 2
        lw = _layer_weights(i)
        if i % 2 == 0:
            proj = _carry(comm, "mm_in", i, _mm, h, comm.weight(lw["w_in"]), "nn", name="mm_even_in")
            ya = _sconv_fwd(proj, w["even_conv_w"][j], name="sconv_fwd")
            states, ypre = _carry(comm, "s5_fwd", i, _s5_fwd, proj, s5["wb"][j], s5["wc"][j], s5["d"][j],
                                  s5["sc"][j], name="s5_fwd")
            cat = _glu_fwd(ypre, comm.weight(lw["glu"][0]), w["ssm_glu_b"][j][None, :], ya, name="glu_fwd")
            mix = (states, ypre)
        else:
            proj = _carry(comm, "mm_in", i, _mm, h, comm.weight(lw["w_in"]), "nn", name="mm_odd_in")
            yc = _pool_fwd(proj, w["pool_w"][j], w["pool_scale"][j][None, :], name="pool_fwd")
            sb_t = jnp.transpose(w["sgu_b"][j])
            cat = _sgu_fwd(proj, w["sgu_norm_g"][j][None, :], w["sgu_w"][j], sb_t, yc, name="sgu_fwd")
            mix = (sb_t,)
        x1, h2 = _mm(cat, comm.weight(lw["w_out"]), "nn", add=x, norm=(w["norm_ffn_g"], i), name="mm_mix_out")
        up = _carry(comm, "mm_up", i, _mm, h2, comm.weight(lw["up"]), "nn", name="mm_up")
        act, gv = _carry(comm, "ffn_act", i, _ffn_act, up, w["ffn_conv_w"][i], w["ffn_conv_b"][i:i + 1],
                         name="ffn_act")
        if i + 1 < DEPTH:
            x2, h_next = _carry(comm, "mm_down", i, _mm, act, comm.weight(lw["down"]), "nn", add=x1,
                                norm=(w["norm_mix_g"], i + 1), name="mm_down")
        else:
            x2, h_next = _mm(act, comm.weight(lw["down"]), "nn", add=x1, name="mm_down_last"), None
        saved.append((x, h, proj, cat, x1, h2, up, gv, act, mix))
        x, h = x2, h_next

    loss, dx, dxb, dgf = _loss_head(x, w["norm_final_g"][None, :], tgt, name="loss_head")
    per_layer = {}
    s5_grads = []

    def put(name, idx, val):
        per_layer.setdefault(name, {})[idx] = val

    for i in reversed(range(DEPTH)):
        j = i // 2
        lw = _layer_weights(i)
        x0, h, proj, cat, x1, h2, up, gv, act, mix = saved[i]
        dact = _mm(dxb, comm.weight(lw["down"]), "nt", name="mm_down_dx")
        comm.grad(lw["down"], _mm(act, dxb, "tn", out_dtype=BF16, name="mm_down_dw"))
        dup, dcw, dcb = _carry(comm, "ffn_act_bwd", i, _ffn_act_bwd, up, gv, dact, w["ffn_conv_w"][i],
                               name="ffn_act_bwd")
        put("ffn_conv_w", i, dcw)
        put("ffn_conv_b", i, dcb[0])
        dx1, dx1b, dg2 = _carry(comm, "mm_up_dx", i, _mm, dup, comm.weight(lw["up"]), "nt", tm_cap=512, tn_cap=D_MODEL,
                                norm_bwd=(x1, w["norm_ffn_g"], i, dx), name="mm_up_dx")
        comm.grad(lw["up"], _carry(comm, "mm_up_dw", i, _mm, h2, dup, "tn", out_dtype=BF16,
                                   name="mm_up_dw"))
        put("norm_ffn_g", i, dg2[0])
        dcat = _mm(dx1b, comm.weight(lw["w_out"]), "nt", name="mm_mix_out_dx")
        comm.grad(lw["w_out"], _mm(cat, dx1b, "tn", out_dtype=BF16, name="mm_mix_out_dw"))
        if i % 2 == 0:
            states, ypre = mix
            dxa, dba, dca, dcw_a = _sconv_bwd(proj, dcat, w["even_conv_w"][j], name="sconv_bwd")
            put("even_conv_w", j, dcw_a)
            dypre, dwg, dbg = _glu_bwd(dcat, ypre, comm.weight(lw["glu"][0]), w["ssm_glu_b"][j][None, :],
                                       name="glu_bwd")
            comm.grad(lw["glu"][0], dwg)
            put("ssm_glu_b", j, dbg[0])
            du, dwb, dwc, dlam, dd = _carry(comm, "s5_bwd", i, _s5_bwd, proj, dypre, states, s5["wb"][j], s5["wc"][j],
                                            s5["d"][j], s5["sc_rev"][j], name="s5_bwd")
            s5_grads.insert(0, (dwb, dwc, dlam, dd))
            dproj = jnp.concatenate([dxa, dba, dca, du], axis=1)
            in_name = "mm_even_in"
        else:
            (sb_t,) = mix
            dz, dpw, dps = _pool_bwd(proj, dcat, w["pool_w"][j], w["pool_scale"][j][None, :], name="pool_bwd")
            put("pool_w", j, dpw)
            put("pool_scale", j, dps[0])
            dsu, dsv, dsw, dbm, dng = _sgu_bwd(proj, dcat, w["sgu_norm_g"][j][None, :], w["sgu_w"][j], sb_t,
                                               name="sgu_bwd")
            put("sgu_w", j, dsw)
            put("sgu_b", j, jnp.sum(dbm, axis=-1))
            put("sgu_norm_g", j, dng[0])
            dproj = jnp.concatenate([dz, dsu, dsv], axis=1)
            in_name = "mm_odd_in"
        comm.grad(lw["w_in"], _carry(comm, "mm_in_dw", i, _mm, h, dproj, "tn", out_dtype=BF16, name=in_name + "_dw"))
        dx, dxb, dg1 = _carry(comm, "mm_in_dx", i, _mm, dproj, comm.weight(lw["w_in"]), "nt", tn_cap=D_MODEL,
                              norm_bwd=(x0, w["norm_mix_g"], i, dx1), name=in_name + "_dx")
        put("norm_mix_g", i, dg1[0])

    grads = {nm: [vals[k] for k in sorted(vals)] for nm, vals in per_layer.items()}
    grads.update({nm: jnp.stack(grads[nm]) for nm in SMALL})
    grads["norm_final_g"] = dgf[0]
    grads.update(_s5_param_grads(s5, *[jnp.stack(parts) for parts in zip(*s5_grads)]))
    return loss, dx, grads


def _carry(comm, stage, i, fn, *args, **kwargs):
    exchanges = comm.carried_by(stage, i)
    if not exchanges:
        return fn(*args, **kwargs)
    out, moved = fn(*args, exchanges=exchanges, **kwargs)
    comm.delivered(stage, i, moved)
    return out


def _sum_parts(parts, *, name):
    g, r, c = parts.shape
    tr = _pick_tile(r, max(16, EXCHANGE_BLOCK_ELEMS // c), 16)

    def body(p_ref, o_ref):
        acc = p_ref[0].astype(F32)
        for k in range(1, g):
            acc = acc + p_ref[k].astype(F32)
        o_ref[...] = acc

    return pl.pallas_call(
        body, name=name, grid=(r // tr,), in_specs=[pl.BlockSpec((g, tr, c), lambda i: (0, i, 0))],
        out_specs=pl.BlockSpec((tr, c), lambda i: (i, 0)), out_shape=jax.ShapeDtypeStruct((r, c), F32),
        compiler_params=_PAR)(parts)


def _adamw(w, m, v, g_parts, *, name, exchanges=()):
    r, c = w.shape
    g = g_parts.shape[0]
    tc = _pick_tile(c, 8192, LANE)
    tr = _pick_tile(r, max(16, (1 << 18) // tc), 16)
    c1 = 1.0 - ADAM_B1 ** ADAM_STEP
    c2 = 1.0 - ADAM_B2 ** ADAM_STEP

    def body(w_ref, m_ref, v_ref, g_ref, go_ref, d_ref, mo_ref, vo_ref):
        grad = g_ref[0].astype(F32)
        for k in range(1, g):
            grad = grad + g_ref[k].astype(F32)
        m_new = ADAM_B1 * m_ref[...] + (1.0 - ADAM_B1) * grad
        v_new = ADAM_B2 * v_ref[...] + (1.0 - ADAM_B2) * (grad * grad)
        go_ref[...] = grad
        mo_ref[...] = m_new
        vo_ref[...] = v_new
        d_ref[...] = -ADAM_LR * ((m_new / c1) / (jnp.sqrt(v_new / c2) + ADAM_EPS) + ADAM_WD * w_ref[...])

    blk = pl.BlockSpec((tr, tc), lambda i, j: (i, j))
    out = jax.ShapeDtypeStruct((r, c), F32)
    return _call(body, name=name, grid=(r // tr, c // tc),
                 in_specs=[blk, blk, blk, pl.BlockSpec((g, tr, tc), lambda i, j: (0, i, j))],
                 out_specs=[blk, blk, blk, blk], out_shape=[out, out, out, out], args=[w, m, v, g_parts],
                 exchanges=exchanges)


WEIGHT_NAMES = ['norm_mix_g', 'even_w_in', 'even_conv_w', 'ssm_log_step', 'ssm_a_re', 'ssm_a_im', 'ssm_b_re',
                'ssm_b_im', 'ssm_c_re', 'ssm_c_im', 'ssm_d', 'ssm_glu_w', 'ssm_glu_b', 'even_w_out', 'odd_w_in',
                'pool_w', 'pool_scale', 'sgu_norm_g', 'sgu_w', 'sgu_b', 'odd_w_out', 'norm_ffn_g', 'ffn_w_up',
                'ffn_conv_w', 'ffn_conv_b', 'ffn_w_down', 'norm_final_g']
BIG = {'even_w_in': 2, 'ssm_glu_w': 1, 'even_w_out': 1, 'odd_w_in': 2, 'odd_w_out': 1, 'ffn_w_up': 2,
       'ffn_w_down': 1}
SMALL = {'even_conv_w': 2, 'pool_scale': 1, 'sgu_norm_g': 1, 'ffn_conv_w': 2}
BIG_ROWS = 512
SMALL_ROWS = 16


def _pad_to(n, q):
    return -(-n // q) * q


def _pack(arrays, dtype, rows, lead=()):
    flat = [a.reshape(lead + (-1,)).astype(dtype) for a in arrays]
    n = sum(f.shape[-1] for f in flat)
    pad = _pad_to(n, rows * LANE) - n
    if pad:
        flat.append(jnp.zeros(lead + (pad,), dtype))
    return jnp.concatenate(flat, axis=-1).reshape(lead + (rows, -1))


def _unpack(buf, shapes, lead=()):
    flat = buf.reshape(lead + (-1,))
    out, off = [], 0
    for shp in shapes:
        n = math.prod(shp)
        out.append(flat[..., off:off + n].reshape(lead + tuple(shp)))
        off += n
    return out


def _to_dest_major(full, axis):
    shp = full.shape
    split = full.reshape(shp[:axis] + (N_DEV, shp[axis] // N_DEV) + shp[axis + 1:])
    return jnp.moveaxis(split, axis, 0)


def _from_dest_major(blocks, axis):
    moved = jnp.moveaxis(blocks, 0, axis)
    shp = moved.shape
    return moved.reshape(shp[:axis] + (shp[axis] * shp[axis + 1],) + shp[axis + 2:])


def _rows_2d(a):
    return a.reshape(-1, a.shape[-1])


def _rows_2d_lead(a):
    return a.reshape(a.shape[0], -1, a.shape[-1])


def kernel(x, norm_mix_g, even_w_in, even_conv_w, ssm_log_step, ssm_a_re, ssm_a_im, ssm_b_re, ssm_b_im, ssm_c_re, ssm_c_im, ssm_d, ssm_glu_w, ssm_glu_b, even_w_out, odd_w_in, pool_w, pool_scale, sgu_norm_g, sgu_w, sgu_b, odd_w_out, norm_ffn_g, ffn_w_up, ffn_conv_w, ffn_conv_b, ffn_w_down, norm_final_g, loss_target, m_norm_mix_g, m_even_w_in, m_even_conv_w, m_ssm_log_step, m_ssm_a_re, m_ssm_a_im, m_ssm_b_re, m_ssm_b_im, m_ssm_c_re, m_ssm_c_im, m_ssm_d, m_ssm_glu_w, m_ssm_glu_b, m_even_w_out, m_odd_w_in, m_pool_w, m_pool_scale, m_sgu_norm_g, m_sgu_w, m_sgu_b, m_odd_w_out, m_norm_ffn_g, m_ffn_w_up, m_ffn_conv_w, m_ffn_conv_b, m_ffn_w_down, m_norm_final_g, v_norm_mix_g, v_even_w_in, v_even_conv_w, v_ssm_log_step, v_ssm_a_re, v_ssm_a_im, v_ssm_b_re, v_ssm_b_im, v_ssm_c_re, v_ssm_c_im, v_ssm_d, v_ssm_glu_w, v_ssm_glu_b, v_even_w_out, v_odd_w_in, v_pool_w, v_pool_scale, v_sgu_norm_g, v_sgu_w, v_sgu_b, v_odd_w_out, v_norm_ffn_g, v_ffn_w_up, v_ffn_conv_w, v_ffn_conv_b, v_ffn_w_down, v_norm_final_g):
    given = dict(locals())
    wts = {n: given[n] for n in WEIGHT_NAMES}
    mom = {n: given["m_" + n] for n in WEIGHT_NAMES}
    var = {n: given["v_" + n] for n in WEIGHT_NAMES}
    repl = [n for n in WEIGHT_NAMES if n not in BIG and n not in SMALL]

    small_shapes = [wts[n].shape for n in SMALL]
    comm = _ShardedWeights({n: wts[n].astype(BF16) for n in BIG})
    (small_all,) = comm.start([_Gather(_pack([wts[n] for n in SMALL], F32, SMALL_ROWS))])
    full = {n: wts[n] for n in repl}
    for n, blocks in zip(SMALL, _unpack(small_all, small_shapes, lead=(N_DEV,))):
        full[n] = _from_dest_major(blocks, SMALL[n])

    loss, dx, grads = _device_step(x[0], loss_target[0], full, comm)

    repl_shapes = [wts[n].shape for n in repl]
    pieces = [p.reshape(-1) for n in repl for p in (grads[n] if isinstance(grads[n], list) else [grads[n]])]
    repl_flat = jnp.concatenate(pieces + [loss.reshape(-1)])
    n_repl = repl_flat.shape[0]
    chunk = _pad_to(-(-n_repl // N_DEV), SMALL_ROWS * LANE)
    repl_flat = jnp.pad(repl_flat, (0, N_DEV * chunk - n_repl))
    small_part = _pack([_to_dest_major(grads[n], SMALL[n]) for n in SMALL], F32, SMALL_ROWS, lead=(N_DEV,))
    small_cols = small_part.shape[2]
    small_scatter = _Scatter(
        jnp.concatenate([small_part, repl_flat.reshape(N_DEV, SMALL_ROWS, chunk // SMALL_ROWS)], axis=2))

    _, (small_rs,) = comm.finish(lambda exchanges: (None, _exchange_only(exchanges, name="rs_last")), [small_scatter])
    small_sum = _sum_parts(small_rs, name="rs_sum_small")
    (repl_all,) = _exchange_only([_Gather(small_sum[:, small_cols:])], name="ag_repl")
    repl_sum = repl_all.reshape(-1)
    total_loss = repl_sum[n_repl - 1]

    out = {}
    for n in BIG:
        res = _adamw(_rows_2d(wts[n]), _rows_2d(mom[n]), _rows_2d(var[n]), comm.scattered[n], name="adamw_" + n)
        out[n] = [r.reshape(wts[n].shape) for r in res]

    def small_vec(shard_part, repl_part):
        flat = jnp.concatenate([shard_part.reshape(-1), repl_part])
        return flat.reshape(SMALL_ROWS, -1)

    def small_tree(tree):
        tail = jnp.concatenate([tree[n].reshape(-1) for n in repl])
        tail = jnp.pad(tail, (0, N_DEV * chunk - tail.shape[0]))
        return small_vec(_pack([tree[n] for n in SMALL], F32, SMALL_ROWS), tail)

    res = _adamw(small_tree(wts), small_tree(mom), small_tree(var), small_vec(small_sum[:, :small_cols], repl_sum)[None],
                 name="adamw_small")
    n_small = SMALL_ROWS * small_cols
    for k, r in enumerate(res):
        flat = r.reshape(-1)
        shard = _unpack(flat[:n_small], small_shapes)
        rest = _unpack(flat[n_small:], repl_shapes)
        for n, val in zip(SMALL, shard):
            out.setdefault(n, [None] * 4)[k] = val
        for n, val in zip(repl, rest):
            out.setdefault(n, [None] * 4)[k] = val

    grad_x = dx[None]
    return (total_loss, grad_x, *[out[n][0] for n in WEIGHT_NAMES], *[out[n][1] for n in WEIGHT_NAMES],
            *[out[n][2] for n in WEIGHT_NAMES], *[out[n][3] for n in WEIGHT_NAMES])
```

```python
import functools
import math

import jax
import jax.numpy as jnp
from jax import lax
from jax.experimental import pallas as pl
from jax.experimental.pallas import tpu as pltpu

F32 = jnp.float32
BF16 = jnp.bfloat16

D_MODEL = 1024
DEPTH = 4
D_HALF = D_MODEL // 2
SSM_GROUP = 16
N_SSM_GROUPS = D_HALF // SSM_GROUP
SSM_STATE = 64
POOL_WINDOWS = (2, 4, 8, 16)
SGU_HEADS = 4
CHUNK = 128
D_FF = 2816
CONV_WIDTH = 3
EPS = 1e-6
N_DEV = 8

ADAM_LR = 0.001
ADAM_B1 = 0.9
ADAM_B2 = 0.999
ADAM_EPS = 1e-08
ADAM_WD = 0.01
ADAM_STEP = 10

LANE = 128
SUBLANE = 8
S5_LANE_BLOCKS = D_HALF // LANE
S5_STATE_LANES = (N_SSM_GROUPS // S5_LANE_BLOCKS) * SSM_STATE
S5_TIME_CHUNK = 512
EXCHANGE_BLOCK_ELEMS = 1 << 20

GELU_K = math.sqrt(2.0 / math.pi)
GELU_C = 0.044715

_ARB = pltpu.CompilerParams(dimension_semantics=("arbitrary",))
_ARB2 = pltpu.CompilerParams(dimension_semantics=("arbitrary", "arbitrary"))
_PAR = pltpu.CompilerParams(dimension_semantics=("parallel",))
_PAR2 = pltpu.CompilerParams(dimension_semantics=("parallel", "parallel"))


def _pick_tile(n, cap, mult):
    if n <= cap:
        return n
    best = None
    for t in range(mult, cap + 1, mult):
        if n % t == 0:
            best = t
    assert best is not None, (n, cap, mult)
    return best


_MESH = pl.DeviceIdType.MESH
_ANY = pl.BlockSpec(memory_space=pl.ANY)
SEMS_PER_EXCHANGE = N_DEV - 1


class _Gather:
    def __init__(self, src, into=None, row0=0):
        self.src, self.into, self.row0 = src, into, row0
        self.out_shape = jax.ShapeDtypeStruct((N_DEV,) + src.shape if into is None else into.shape, src.dtype)

    def copies(self, x_ref, whole_ref, send_sems, recv_sems, local_sem):
        out_ref = whole_ref if self.into is None else whole_ref.at[:, pl.ds(self.row0, self.src.shape[0])]
        x, y, cc = lax.axis_index("x"), lax.axis_index("y"), lax.axis_index("c")
        me, sibling = (x, y, cc), (x, y, 1 - cc)
        chips = [(1 - x, y), (x, 1 - y), (1 - x, 1 - y)]

        def rows(px, py, pc):
            return out_ref.at[4 * px + 2 * py + pc]

        def copy(k, block, to, src=None):
            return pltpu.make_async_remote_copy(
                src_ref=rows(*block) if src is None else src, dst_ref=rows(*block),
                send_sem=send_sems.at[k], recv_sem=recv_sems.at[k], device_id=to, device_id_type=_MESH)

        return dict(
            mine=pltpu.make_async_copy(x_ref, rows(*me), local_sem),
            first=[copy(0, me, sibling, src=x_ref)] + [copy(1 + k, me, (*chip, cc), src=x_ref)
                                                       for k, chip in enumerate(chips)],
            passed=[copy(4 + k, (*chip, cc), sibling) for k, chip in enumerate(chips)],
            over_ici=[copy(1 + k, (*chip, cc), me) for k, chip in enumerate(chips)],
            from_sibling=[copy(0, sibling, me)] + [copy(4 + k, (*chip, 1 - cc), me) for k, chip in enumerate(chips)])

    def start(self, *refs):
        cps = self.copies(*refs)
        cps["mine"].start()
        for cp in cps["first"]:
            cp.start()

    def finish(self, *refs):
        cps = self.copies(*refs)
        for arrived, onward in zip(cps["over_ici"], cps["passed"]):
            arrived.wait_recv()
            onward.start()
        for arrived in cps["from_sibling"]:
            arrived.wait_recv()
        for cp in cps["first"] + cps["passed"]:
            cp.wait_send()
        cps["mine"].wait()


class _Scatter:
    def __init__(self, src, into=None, row0=0):
        self.src, self.into, self.row0 = src, into, row0
        whole = src if into is None else into
        self.out_shape = jax.ShapeDtypeStruct(whole.shape, whole.dtype)

    def copies(self, p_ref, whole_ref, send_sems, recv_sems, local_sem):
        x, y, cc = lax.axis_index("x"), lax.axis_index("y"), lax.axis_index("c")
        me = 4 * x + 2 * y + cc
        rows = self.src.shape[1]
        out_ref = whole_ref if self.into is None else whole_ref.at[:, pl.ds(self.row0, rows)]
        sends, arrivals = [], []
        for k in range(1, N_DEV):
            px = (1 - x) if k & 4 else x
            py = (1 - y) if k & 2 else y
            pc = (1 - cc) if k & 1 else cc
            peer = 4 * px + 2 * py + pc
            kw = dict(send_sem=send_sems.at[k - 1], recv_sem=recv_sems.at[k - 1], device_id=(px, py, pc),
                      device_id_type=_MESH)
            sends.append(pltpu.make_async_remote_copy(src_ref=p_ref.at[peer], dst_ref=out_ref.at[me], **kw))
            arrivals.append(pltpu.make_async_remote_copy(src_ref=p_ref.at[me], dst_ref=out_ref.at[peer], **kw))
        return dict(mine=pltpu.make_async_copy(p_ref.at[me], out_ref.at[me], local_sem), sends=sends,
                    arrivals=arrivals)

    def start(self, *refs):
        cps = self.copies(*refs)
        cps["mine"].start()
        for cp in cps["sends"]:
            cp.start()

    def finish(self, *refs):
        cps = self.copies(*refs)
        for cp in cps["arrivals"]:
            cp.wait_recv()
        for cp in cps["sends"]:
            cp.wait_send()
        cps["mine"].wait()


class _SemView:
    def __init__(self, ref, lo):
        self.ref, self.lo = ref, lo

    @property
    def at(self):
        return self

    def __getitem__(self, k):
        return self.ref.at[self.lo + k]


def _call(body, *, name, grid, in_specs, out_specs, out_shape, args, scratch_shapes=(), parallel=True, exchanges=()):
    n_axes = len(grid)
    if not exchanges:
        sem = ("parallel" if parallel else "arbitrary",) * n_axes
        return pl.pallas_call(
            body, name=name, grid=grid, in_specs=in_specs, out_specs=out_specs, out_shape=out_shape,
            scratch_shapes=scratch_shapes, compiler_params=pltpu.CompilerParams(dimension_semantics=sem))(*args)
    single = not isinstance(out_shape, (list, tuple))
    out_specs = [out_specs] if single else list(out_specs)
    out_shape = [out_shape] if single else list(out_shape)
    n_in, n_out, n_scr, n_x = len(in_specs), len(out_specs), len(scratch_shapes), len(exchanges)
    landing = [(e, ex.into) for e, ex in enumerate(exchanges) if ex.into is not None]
    aliases = {n_in + n_x + pos: n_out + e for pos, (e, _) in enumerate(landing)}

    def wrapped(*refs):
        ins, refs = refs[:n_in], refs[n_in:]
        x_in, refs = refs[:n_x], refs[n_x + len(landing):]
        outs, refs = refs[:n_out], refs[n_out:]
        x_out, refs = refs[:n_x], refs[n_x:]
        scr, (send_sems, recv_sems, local_sems) = refs[:n_scr], refs[n_scr:]
        ids = [pl.program_id(k) for k in range(n_axes)]
        first = functools.reduce(jnp.logical_and, [i == 0 for i in ids])
        last = functools.reduce(jnp.logical_and, [i == g - 1 for i, g in zip(ids, grid)])

        def sems(e):
            lo = e * SEMS_PER_EXCHANGE
            return _SemView(send_sems, lo), _SemView(recv_sems, lo), local_sems.at[e]

        @pl.when(first)
        def _():
            for e, ex in enumerate(exchanges):
                ex.start(x_in[e], x_out[e], *sems(e))

        body(*ins, *outs, *scr)

        @pl.when(last)
        def _():
            for e, ex in enumerate(exchanges):
                ex.finish(x_in[e], x_out[e], *sems(e))

    res = pl.pallas_call(
        wrapped, name=name, grid=grid, in_specs=list(in_specs) + [_ANY] * (n_x + len(landing)),
        out_specs=out_specs + [_ANY] * n_x, out_shape=out_shape + [ex.out_shape for ex in exchanges],
        input_output_aliases=aliases,
        scratch_shapes=list(scratch_shapes) + [pltpu.SemaphoreType.DMA((n_x * SEMS_PER_EXCHANGE,)),
                                               pltpu.SemaphoreType.DMA((n_x * SEMS_PER_EXCHANGE,)),
                                               pltpu.SemaphoreType.DMA((n_x,))],
        compiler_params=pltpu.CompilerParams(dimension_semantics=("arbitrary",) * n_axes),
    )(*args, *[ex.src for ex in exchanges], *[buf for _, buf in landing])
    outs, x_outs = res[:n_out], res[n_out:]
    return (outs[0] if single else outs), x_outs


def _exchange_only(exchanges, *, name):
    def body():
        pass

    return _call(body, name=name, grid=(1,), in_specs=[], out_specs=[], out_shape=[], args=[],
                 exchanges=exchanges)[1]


def _shift_dn(x, d):
    rolled = pltpu.roll(x, d, 0)
    if x.shape[0] <= SUBLANE or d >= SUBLANE:
        row = lax.broadcasted_iota(jnp.int32, x.shape, 0)
        return jnp.where(row >= d, rolled, 0.0)
    row = lax.broadcasted_iota(jnp.int32, (SUBLANE, x.shape[1]), 0)
    return jnp.concatenate([jnp.where(row >= d, rolled[:SUBLANE], 0.0), rolled[SUBLANE:]], axis=0)


def _shift_up(x, d):
    n = x.shape[0]
    rolled = pltpu.roll(x, n - d, 0)
    if n <= SUBLANE or d >= SUBLANE:
        row = lax.broadcasted_iota(jnp.int32, x.shape, 0)
        return jnp.where(row < n - d, rolled, 0.0)
    row = lax.broadcasted_iota(jnp.int32, (SUBLANE, x.shape[1]), 0)
    return jnp.concatenate([rolled[:n - SUBLANE], jnp.where(row < SUBLANE - d, rolled[n - SUBLANE:], 0.0)], axis=0)


def _gelu(x):
    return 0.5 * x * (1.0 + jnp.tanh(GELU_K * (x + GELU_C * x * x * x)))


def _gelu_grad(x):
    t = jnp.tanh(GELU_K * (x + GELU_C * x * x * x))
    return 0.5 * (1.0 + t) + 0.5 * x * (1.0 - t * t) * (GELU_K * (1.0 + 3.0 * GELU_C * x * x))


def _sigmoid(x):
    return 0.5 + 0.5 * jnp.tanh(0.5 * x)


def _conv3(x, w_ref):
    return w_ref[0:1, :] * _shift_dn(x, 2) + w_ref[1:2, :] * _shift_dn(x, 1) + w_ref[2:3, :] * x


def _conv3_bwd_x(dy, w_ref):
    return w_ref[2:3, :] * dy + w_ref[1:2, :] * _shift_up(dy, 1) + w_ref[0:1, :] * _shift_up(dy, 2)


def _conv3_bwd_w(dy, x):
    return jnp.concatenate([
        jnp.sum(dy * _shift_dn(x, 2), axis=0, keepdims=True),
        jnp.sum(dy * _shift_dn(x, 1), axis=0, keepdims=True),
        jnp.sum(dy * x, axis=0, keepdims=True)], axis=0)


def _dot(a, b, dims):
    return lax.dot_general(a.astype(BF16), b.astype(BF16), (dims, ((), ())), preferred_element_type=F32)


_NN = ((1,), (0,))
_NT = ((1,), (1,))
_TN = ((0,), (0,))


def _tiles(name, m, n, default):
    tm, tn = TILES.get(name, default)
    return math.gcd(tm, m), math.gcd(tn, n)


def _mm(a, b, mode, *, name, out_dtype=F32, add=None, norm=None, norm_bwd=None, tm_cap=512, tn_cap=1536,
        exchanges=()):
    halves = None
    if mode == "tn":
        r, m = a.shape
        n = b.shape[-1] * (2 if b.ndim == 3 else 1)
        tm, tn = _tiles(name, m, n, (_pick_tile(m, 256, LANE), _pick_tile(n, tn_cap, LANE)))
        if b.ndim == 3:
            per_half = b.shape[-1] // tn
            b_spec = pl.BlockSpec((None, r, tn), lambda i, j: (j // per_half, 0, j % per_half))
        else:
            b_spec = pl.BlockSpec((r, tn), lambda i, j: (0, j))
        in_specs = [pl.BlockSpec((r, tm), lambda i, j: (0, i)), b_spec]
        dims = _TN
    elif mode == "nt" and a.ndim == 3:
        _, m, halves = a.shape
        n = b.shape[0]
        tm, tn = _tiles(name, m, n, (_pick_tile(m, tm_cap, SUBLANE), _pick_tile(n, tn_cap, LANE)))
        in_specs = [pl.BlockSpec((2, tm, halves), lambda i, j: (0, i, 0)),
                    pl.BlockSpec((tn, 2 * halves), lambda i, j: (j, 0))]
        dims = _NT
    elif mode == "nn":
        m, k = a.shape
        n = b.shape[1]
        tm, tn = _tiles(name, m, n, (_pick_tile(m, tm_cap, SUBLANE), _pick_tile(n, tn_cap, LANE)))
        in_specs = [pl.BlockSpec((tm, k), lambda i, j: (i, 0)), pl.BlockSpec((k, tn), lambda i, j: (0, j))]
        dims = _NN
    else:
        m, k = a.shape
        n = b.shape[0]
        tm, tn = _tiles(name, m, n, (_pick_tile(m, tm_cap, SUBLANE), _pick_tile(n, tn_cap, LANE)))
        in_specs = [pl.BlockSpec((tm, k), lambda i, j: (i, 0)), pl.BlockSpec((tn, k), lambda i, j: (j, 0))]
        dims = _NT
    assert m % tm == 0 and n % tn == 0, (name, m, n, tm, tn)
    args = [a, b]
    tile = pl.BlockSpec((tm, tn), lambda i, j: (i, j))
    if add is not None:
        in_specs.append(tile)
        args.append(add)
    out_specs, out_shape = tile, jax.ShapeDtypeStruct((m, n), out_dtype)
    if norm is not None:
        gains, layer = norm
        assert tn == n
        in_specs.append(pl.BlockSpec((None, 1, n), lambda i, j: (layer, 0, 0)))
        args.append(gains.reshape(gains.shape[0], 1, n))
        out_specs, out_shape = [tile, tile], [out_shape, jax.ShapeDtypeStruct((m, n), BF16)]
    if norm_bwd is not None:
        x_in, gains, layer, res = norm_bwd
        assert tn == n and add is None and norm is None
        vec = pl.BlockSpec((None, 1, n), lambda i, j: (layer, 0, 0))
        in_specs += [tile, vec, tile]
        args += [x_in, gains.reshape(gains.shape[0], 1, n), res]
        out_specs = [tile, tile, pl.BlockSpec((1, n), lambda i, j: (0, 0))]
        out_shape = [jax.ShapeDtypeStruct((m, n), F32), jax.ShapeDtypeStruct((m, n), BF16),
                     jax.ShapeDtypeStruct((1, n), F32)]

    def body(*refs):
        if halves is None:
            acc = _dot(refs[0][...], refs[1][...], dims)
        else:
            acc = (_dot(refs[0][0], refs[1][:, :halves], dims) + _dot(refs[0][1], refs[1][:, halves:], dims))
        if add is not None:
            acc = acc + refs[2][...]
        if norm_bwd is not None:
            x_ref, g_ref, res_ref, dx_ref, dxb_ref, dg_ref = refs[2:]

            @pl.when(pl.program_id(0) == 0)
            def _():
                dg_ref[...] = jnp.zeros_like(dg_ref)
            dx, xn = _rms_bwd_rows(acc, x_ref[...], g_ref[...])
            dx = dx + res_ref[...]
            dx_ref[...] = dx
            dxb_ref[...] = dx.astype(BF16)
            dg_ref[...] += jnp.sum(acc * xn, axis=0, keepdims=True)
        elif norm is None:
            refs[-1][...] = acc.astype(out_dtype)
        else:
            refs[-2][...] = acc.astype(out_dtype)
            r = lax.rsqrt(jnp.mean(acc * acc, axis=-1, keepdims=True) + EPS)
            refs[-1][...] = (acc * r * refs[-3][...]).astype(BF16)

    return _call(body, name=name, grid=(m // tm, n // tn), in_specs=in_specs, out_specs=out_specs,
                 out_shape=out_shape, args=args, parallel=norm_bwd is None, exchanges=exchanges)


def _rms_fwd(x, g, *, name):
    l, d = x.shape
    tl = _pick_tile(l, 512, SUBLANE)

    def body(x_ref, g_ref, h_ref):
        xv = x_ref[...]
        r = lax.rsqrt(jnp.mean(xv * xv, axis=-1, keepdims=True) + EPS)
        h_ref[...] = (xv * r * g_ref[...]).astype(BF16)

    return pl.pallas_call(
        body, name=name, grid=(l // tl,),
        in_specs=[pl.BlockSpec((tl, d), lambda i: (i, 0)), pl.BlockSpec((1, d), lambda i: (0, 0))],
        out_specs=pl.BlockSpec((tl, d), lambda i: (i, 0)),
        out_shape=jax.ShapeDtypeStruct((l, d), BF16), compiler_params=_PAR)(x, g)


def _rms_bwd_rows(dh, xv, g):
    r = lax.rsqrt(jnp.mean(xv * xv, axis=-1, keepdims=True) + EPS)
    a = dh * g
    m = jnp.mean(a * xv, axis=-1, keepdims=True)
    return r * a - xv * (r * r * r) * m, xv * r


def _rms_bwd(dh, x, g, res, *, name):
    l, d = x.shape
    tl = _pick_tile(l, 512, SUBLANE)

    def body(dh_ref, x_ref, g_ref, res_ref, dx_ref, dxb_ref, dg_ref):
        @pl.when(pl.program_id(0) == 0)
        def _():
            dg_ref[...] = jnp.zeros_like(dg_ref)
        dhv = dh_ref[...]
        dx, xn = _rms_bwd_rows(dhv, x_ref[...], g_ref[...])
        dx = dx + res_ref[...]
        dx_ref[...] = dx
        dxb_ref[...] = dx.astype(BF16)
        dg_ref[...] += jnp.sum(dhv * xn, axis=0, keepdims=True)

    row = pl.BlockSpec((tl, d), lambda i: (i, 0))
    vec = pl.BlockSpec((1, d), lambda i: (0, 0))
    return pl.pallas_call(
        body, name=name, grid=(l // tl,), in_specs=[row, row, vec, row], out_specs=[row, row, vec],
        out_shape=[jax.ShapeDtypeStruct((l, d), F32), jax.ShapeDtypeStruct((l, d), BF16),
                   jax.ShapeDtypeStruct((1, d), F32)],
        compiler_params=_ARB)(dh, x, g, res)


def _loss_head(x, g, tgt, *, name):
    l, d = x.shape
    tl = _pick_tile(l, 512, SUBLANE)

    def body(x_ref, g_ref, t_ref, loss_ref, dx_ref, dxb_ref, dg_ref):
        @pl.when(pl.program_id(0) == 0)
        def _():
            dg_ref[...] = jnp.zeros_like(dg_ref)
            loss_ref[...] = jnp.zeros_like(loss_ref)
        xv = x_ref[...]
        gv = g_ref[...]
        r = lax.rsqrt(jnp.mean(xv * xv, axis=-1, keepdims=True) + EPS)
        err = xv * r * gv - t_ref[...]
        row_loss = jnp.sum(err * err, axis=-1, keepdims=True) * (0.5 / d)
        loss_ref[...] += jnp.sum(row_loss, axis=0, keepdims=True)
        dy = err * (1.0 / d)
        dx, xn = _rms_bwd_rows(dy, xv, gv)
        dx_ref[...] = dx
        dxb_ref[...] = dx.astype(BF16)
        dg_ref[...] += jnp.sum(dy * xn, axis=0, keepdims=True)

    row = pl.BlockSpec((tl, d), lambda i: (i, 0))
    vec = pl.BlockSpec((1, d), lambda i: (0, 0))
    one = pl.BlockSpec((1, 1), lambda i: (0, 0))
    return pl.pallas_call(
        body, name=name, grid=(l // tl,), in_specs=[row, vec, row], out_specs=[one, row, row, vec],
        out_shape=[jax.ShapeDtypeStruct((1, 1), F32), jax.ShapeDtypeStruct((l, d), F32),
                   jax.ShapeDtypeStruct((l, d), BF16), jax.ShapeDtypeStruct((1, d), F32)],
        compiler_params=_ARB)(x, g, tgt)


def _ffn_act(up, cw, cb, *, name, exchanges=()):
    l = up.shape[0]
    nb = D_FF // LANE

    def body(ug_ref, uv_ref, wg_ref, wv_ref, bg_ref, bv_ref, o_ref, gv_ref):
        gc = _conv3(ug_ref[...], wg_ref) + bg_ref[...]
        vc = _conv3(uv_ref[...], wv_ref) + bv_ref[...]
        gv_ref[0] = gc
        gv_ref[1] = vc
        o_ref[...] = (gc * _sigmoid(gc) * vc).astype(BF16)

    col = lambda off: pl.BlockSpec((l, LANE), lambda j: (0, j + off))
    w3 = lambda off: pl.BlockSpec((CONV_WIDTH, LANE), lambda j: (0, j + off))
    b1 = lambda off: pl.BlockSpec((1, LANE), lambda j: (0, j + off))
    return _call(body, name=name, grid=(nb,), in_specs=[col(0), col(nb), w3(0), w3(nb), b1(0), b1(nb)],
                 out_specs=[col(0), pl.BlockSpec((2, l, LANE), lambda j: (0, 0, j))],
                 out_shape=[jax.ShapeDtypeStruct((l, D_FF), BF16), jax.ShapeDtypeStruct((2, l, D_FF), F32)],
                 args=[up, up, cw, cw, cb, cb], exchanges=exchanges)


def _ffn_act_bwd(up, gv, dact, cw, *, name, exchanges=()):
    l = up.shape[0]
    nb = D_FF // LANE

    def half_bwd(k, dc, x, w_ref, dup_ref, dcw_ref, dcb_ref):
        d1, d2 = _shift_up(dc, 1), _shift_up(dc, 2)
        dcb_ref[k] = jnp.sum(dc, axis=0, keepdims=True)
        dcw_ref[k] = jnp.concatenate([jnp.sum(d2 * x, axis=0, keepdims=True),
                                      jnp.sum(d1 * x, axis=0, keepdims=True),
                                      jnp.sum(dc * x, axis=0, keepdims=True)], axis=0)
        dup_ref[k] = (w_ref[2:3, :] * dc + w_ref[1:2, :] * d1 + w_ref[0:1, :] * d2).astype(BF16)

    def body(ug_ref, uv_ref, gv_ref, da_ref, wg_ref, wv_ref, dup_ref, dcw_ref, dcb_ref):
        gc, vc, da = gv_ref[0], gv_ref[1], da_ref[...]
        sg = _sigmoid(gc)
        half_bwd(0, da * vc * (sg * (1.0 + gc * (1.0 - sg))), ug_ref[...], wg_ref, dup_ref, dcw_ref, dcb_ref)
        half_bwd(1, da * (gc * sg), uv_ref[...], wv_ref, dup_ref, dcw_ref, dcb_ref)

    col = lambda off: pl.BlockSpec((l, LANE), lambda j: (0, j + off))
    w3 = lambda off: pl.BlockSpec((CONV_WIDTH, LANE), lambda j: (0, j + off))
    both = lambda rows: pl.BlockSpec((2, rows, LANE), lambda j: (0, 0, j))
    res = _call(
        body, name=name, grid=(nb,),
        in_specs=[col(0), col(nb), both(l), col(0), w3(0), w3(nb)],
        out_specs=[both(l), both(CONV_WIDTH), both(1)],
        out_shape=[jax.ShapeDtypeStruct((2, l, D_FF), BF16), jax.ShapeDtypeStruct((2, CONV_WIDTH, D_FF), F32),
                   jax.ShapeDtypeStruct((2, 1, D_FF), F32)],
        args=[up, up, gv, dact, cw, cw], exchanges=exchanges)
    (dup, dcw, dcb), moved = res if exchanges else (res, None)
    outs = [dup, jnp.concatenate([dcw[0], dcw[1]], axis=1), jnp.concatenate([dcb[0], dcb[1]], axis=1)]
    return (outs, moved) if exchanges else outs


def _sconv_fwd(proj, cw, *, name):
    l = proj.shape[0]
    nb = D_HALF // LANE

    def body(xa_ref, ba_ref, ca_ref, w_ref, o_ref):
        o_ref[...] = (ba_ref[...] * _conv3(ca_ref[...] * xa_ref[...], w_ref)).astype(BF16)

    col = lambda off: pl.BlockSpec((l, LANE), lambda j: (0, j + off))
    return pl.pallas_call(
        body, name=name, grid=(nb,),
        in_specs=[col(0), col(nb), col(2 * nb), pl.BlockSpec((CONV_WIDTH, LANE), lambda j: (0, j))],
        out_specs=col(0), out_shape=jax.ShapeDtypeStruct((l, 2 * D_HALF), BF16),
        compiler_params=_PAR)(proj, proj, proj, cw)


def _sconv_bwd(proj, dcat, cw, *, name):
    l = proj.shape[0]
    nb = D_HALF // LANE

    def body(xa_ref, ba_ref, ca_ref, dy_ref, w_ref, dxa_ref, dba_ref, dca_ref, dw_ref):
        xa, ba, ca, dy = xa_ref[...], ba_ref[...], ca_ref[...], dy_ref[...]
        q = ca * xa
        dba_ref[...] = (dy * _conv3(q, w_ref)).astype(BF16)
        dconv = dy * ba
        d1, d2 = _shift_up(dconv, 1), _shift_up(dconv, 2)
        dw_ref[...] = jnp.concatenate([jnp.sum(d2 * q, axis=0, keepdims=True), jnp.sum(d1 * q, axis=0, keepdims=True),
                                       jnp.sum(dconv * q, axis=0, keepdims=True)], axis=0)
        dq = w_ref[2:3, :] * dconv + w_ref[1:2, :] * d1 + w_ref[0:1, :] * d2
        dxa_ref[...] = (dq * ca).astype(BF16)
        dca_ref[...] = (dq * xa).astype(BF16)

    col = lambda off: pl.BlockSpec((l, LANE), lambda j: (0, j + off))
    w3 = pl.BlockSpec((CONV_WIDTH, LANE), lambda j: (0, j))
    piece = jax.ShapeDtypeStruct((l, D_HALF), BF16)
    return pl.pallas_call(
        body, name=name, grid=(nb,),
        in_specs=[col(0), col(nb), col(2 * nb), col(0), w3],
        out_specs=[col(0), col(0), col(0), w3],
        out_shape=[piece, piece, piece, jax.ShapeDtypeStruct((CONV_WIDTH, D_HALF), F32)],
        compiler_params=_PAR)(proj, proj, proj, dcat, cw)


def _s5_prep(log_step, a_re, a_im, b_re, b_im):
    step = jnp.exp(log_step)[:, None]
    mag = jnp.exp(a_re * step)
    lr = mag * jnp.cos(a_im * step)
    li = mag * jnp.sin(a_im * step)
    nr = lr - 1.0
    den = a_re * a_re + a_im * a_im
    qr = (nr * a_re + li * a_im) / den
    qi = (li * a_re - nr * a_im) / den
    br = qr[..., None] * b_re - qi[..., None] * b_im
    bi = qr[..., None] * b_im + qi[..., None] * b_re
    return lr, li, br, bi


def _block_diag(m):
    nb, ng, r, c = m.shape
    eye = jnp.eye(ng, dtype=m.dtype)
    return jnp.einsum("bgrc,gh->bgrhc", m, eye).reshape(nb, ng * r, ng * c)


def _block_diag_extract(w, r, c):
    nb = w.shape[0]
    ng = w.shape[1] // r
    w5 = w.reshape(nb, ng, r, ng, c)
    return jnp.einsum("bgrhc,gh->bgrc", w5, jnp.eye(ng, dtype=w.dtype))


def _s5_mats(br, bi, c_re, c_im):
    g8 = N_SSM_GROUPS // S5_LANE_BLOCKS
    to_blk = lambda m: m.reshape(S5_LANE_BLOCKS, g8, m.shape[1], m.shape[2])
    wb = jnp.concatenate([_block_diag(to_blk(jnp.swapaxes(br, 1, 2))),
                          _block_diag(to_blk(jnp.swapaxes(bi, 1, 2)))], axis=2)
    wc = jnp.concatenate([_block_diag(to_blk(jnp.swapaxes(c_re, 1, 2))),
                          _block_diag(to_blk(jnp.swapaxes(-c_im, 1, 2)))], axis=1)
    return wb, wc


def _s5_mats_bwd(dwb, dwc):
    g, p, h = N_SSM_GROUPS, SSM_STATE, SSM_GROUP
    half = S5_STATE_LANES
    dbr = jnp.swapaxes(_block_diag_extract(dwb[:, :, :half], h, p).reshape(g, h, p), 1, 2)
    dbi = jnp.swapaxes(_block_diag_extract(dwb[:, :, half:], h, p).reshape(g, h, p), 1, 2)
    dcr = jnp.swapaxes(_block_diag_extract(dwc[:, :half, :], p, h).reshape(g, p, h), 1, 2)
    dci = -jnp.swapaxes(_block_diag_extract(dwc[:, half:, :], p, h).reshape(g, p, h), 1, 2)
    return dbr, dbi, dcr, dci


def _s5_scan_consts(log_step, a_re, a_im, reverse):
    step = jnp.exp(log_step)[:, None]
    xr = (a_re * step).reshape(S5_LANE_BLOCKS, 1, S5_STATE_LANES)
    xi = (a_im * step).reshape(S5_LANE_BLOCKS, 1, S5_STATE_LANES)
    if reverse:
        xi = -xi
    row = jnp.arange(SUBLANE, dtype=F32).reshape(1, SUBLANE, 1)

    def power(n):
        mag = jnp.exp(n * xr)
        return jnp.concatenate([mag * jnp.cos(n * xi), mag * jnp.sin(n * xi)], axis=-1)

    kinds = []
    for d in (1, 2, 4):
        keep = (row <= SUBLANE - 1 - d) if reverse else (row >= d)
        kinds.append(jnp.where(keep, power(jnp.full_like(row, float(d))), 0.0))
    kinds.append(power((SUBLANE - row) if reverse else (row + 1.0)))
    return jnp.stack(kinds, axis=1)


def _scan_rows(s_ref, sc_ref, carry_ref, n_rows, reverse):
    n_grp = n_rows // SUBLANE
    n_col = S5_STATE_LANES // LANE
    half = S5_STATE_LANES

    def step(i, carry):
        grp = (n_grp - 1 - i) if reverse else i
        r0 = pl.multiple_of(grp * SUBLANE, SUBLANE)
        out = []
        for cb in range(n_col):
            lo, hi = cb * LANE, half + cb * LANE
            re = s_ref[pl.ds(r0, SUBLANE), lo:lo + LANE]
            im = s_ref[pl.ds(r0, SUBLANE), hi:hi + LANE]
            for k, d in enumerate((1, 2, 4)):
                sh = (SUBLANE - d) if reverse else d
                rr, ri = pltpu.roll(re, sh, 0), pltpu.roll(im, sh, 0)
                ar, ai = sc_ref[k, :, lo:lo + LANE], sc_ref[k, :, hi:hi + LANE]
                re, im = re + (ar * rr - ai * ri), im + (ar * ri + ai * rr)
            pr, pi = sc_ref[3, :, lo:lo + LANE], sc_ref[3, :, hi:hi + LANE]
            cr, ci = carry[2 * cb], carry[2 * cb + 1]
            re, im = re + (pr * cr - pi * ci), im + (pr * ci + pi * cr)
            s_ref[pl.ds(r0, SUBLANE), lo:lo + LANE] = re
            s_ref[pl.ds(r0, SUBLANE), hi:hi + LANE] = im
            edge = 0 if reverse else SUBLANE - 1
            out.append(jnp.broadcast_to(re[edge:edge + 1, :], (SUBLANE, LANE)))
            out.append(jnp.broadcast_to(im[edge:edge + 1, :], (SUBLANE, LANE)))
        return tuple(out)

    init = []
    for cb in range(n_col):
        init.append(carry_ref[:, cb * LANE:(cb + 1) * LANE])
        init.append(carry_ref[:, half + cb * LANE:half + (cb + 1) * LANE])
    fin = lax.fori_loop(0, n_grp, step, tuple(init))
    for cb in range(n_col):
        carry_ref[:, cb * LANE:(cb + 1) * LANE] = fin[2 * cb]
        carry_ref[:, half + cb * LANE:half + (cb + 1) * LANE] = fin[2 * cb + 1]


def _s5_fwd(proj, wb, wc, d_skip, sc, *, name, exchanges=()):
    l = proj.shape[0]
    tt = _pick_tile(l, S5_TIME_CHUNK, SUBLANE)
    u_off = (proj.shape[1] - D_HALF) // LANE
    w2 = 2 * S5_STATE_LANES

    def body(u_ref, wb_ref, wc_ref, d_ref, sc_ref, s_ref, y_ref, carry_ref):
        @pl.when(pl.program_id(1) == 0)
        def _():
            carry_ref[...] = jnp.zeros_like(carry_ref)
        u = u_ref[...]
        s_ref[...] = _dot(u, wb_ref[0], _NN)
        _scan_rows(s_ref, sc_ref.at[0], carry_ref, tt, False)
        y_ref[...] = _dot(s_ref[...], wc_ref[0], _NN) + d_ref[...] * u

    return _call(
        body, name=name, grid=(S5_LANE_BLOCKS, l // tt),
        in_specs=[pl.BlockSpec((tt, LANE), lambda b, t: (t, b + u_off)),
                  pl.BlockSpec((1, LANE, w2), lambda b, t: (b, 0, 0)),
                  pl.BlockSpec((1, w2, LANE), lambda b, t: (b, 0, 0)),
                  pl.BlockSpec((1, LANE), lambda b, t: (0, b)),
                  pl.BlockSpec((1, 4, SUBLANE, w2), lambda b, t: (b, 0, 0, 0))],
        out_specs=[pl.BlockSpec((tt, w2), lambda b, t: (t, b)), pl.BlockSpec((tt, LANE), lambda b, t: (t, b))],
        out_shape=[jax.ShapeDtypeStruct((l, S5_LANE_BLOCKS * w2), F32), jax.ShapeDtypeStruct((l, D_HALF), F32)],
        scratch_shapes=[pltpu.VMEM((SUBLANE, w2), F32)],
        args=[proj, wb, wc, d_skip, sc], parallel=False, exchanges=exchanges)


def _s5_bwd(proj, dy, states, wb, wc, d_skip, sc_rev, *, name, exchanges=()):
    l = proj.shape[0]
    tt = _pick_tile(l, S5_TIME_CHUNK, SUBLANE)
    nt = l // tt
    u_off = (proj.shape[1] - D_HALF) // LANE
    w2 = 2 * S5_STATE_LANES
    half = S5_STATE_LANES
    grp_per_chunk = tt // SUBLANE

    def body(u_ref, dy_ref, s_ref, halo_ref, wb_ref, wc_ref, d_ref, sc_ref,
             du_ref, dwb_ref, dwc_ref, dlam_ref, dd_ref, g_scr, carry_ref):
        t = pl.program_id(1)

        @pl.when(t == 0)
        def _():
            carry_ref[...] = jnp.zeros_like(carry_ref)
            dwb_ref[...] = jnp.zeros_like(dwb_ref)
            dwc_ref[...] = jnp.zeros_like(dwc_ref)
            dlam_ref[...] = jnp.zeros_like(dlam_ref)
            dd_ref[...] = jnp.zeros_like(dd_ref)

        u = u_ref[...]
        dyv = dy_ref[...]
        g_scr[...] = _dot(dyv, wc_ref[0], _NT)
        _scan_rows(g_scr, sc_ref.at[0], carry_ref, tt, True)
        gv = g_scr[...]
        du_ref[...] = (_dot(gv, wb_ref[0], _NT) + d_ref[...] * dyv).astype(BF16)
        dwb_ref[0] += _dot(u, gv, _TN)
        sv = s_ref[...]
        dwc_ref[0] += _dot(sv, dyv, _TN)
        dd_ref[...] += jnp.sum(dyv * u, axis=0, keepdims=True)
        first_chunk = t == nt - 1
        halo = jnp.where(first_chunk, 0.0, halo_ref[SUBLANE - 1:SUBLANE, :])
        row = lax.broadcasted_iota(jnp.int32, sv.shape, 0)
        sp = jnp.where(row == 0, jnp.broadcast_to(halo, sv.shape), pltpu.roll(sv, 1, 0))
        gr, gi = gv[:, :half], gv[:, half:]
        sr, si = sp[:, :half], sp[:, half:]
        dlr = jnp.sum(gr * sr + gi * si, axis=0, keepdims=True)
        dli = jnp.sum(gi * sr - gr * si, axis=0, keepdims=True)
        dlam_ref[0] += jnp.concatenate([dlr, dli], axis=1)

    rev = lambda t: nt - 1 - t
    return _call(
        body, name=name, grid=(S5_LANE_BLOCKS, nt),
        in_specs=[pl.BlockSpec((tt, LANE), lambda b, t: (rev(t), b + u_off)),
                  pl.BlockSpec((tt, LANE), lambda b, t: (rev(t), b)),
                  pl.BlockSpec((tt, w2), lambda b, t: (rev(t), b)),
                  pl.BlockSpec((SUBLANE, w2), lambda b, t: (jnp.maximum(rev(t) * grp_per_chunk - 1, 0), b)),
                  pl.BlockSpec((1, LANE, w2), lambda b, t: (b, 0, 0)),
                  pl.BlockSpec((1, w2, LANE), lambda b, t: (b, 0, 0)),
                  pl.BlockSpec((1, LANE), lambda b, t: (0, b)),
                  pl.BlockSpec((1, 4, SUBLANE, w2), lambda b, t: (b, 0, 0, 0))],
        out_specs=[pl.BlockSpec((tt, LANE), lambda b, t: (rev(t), b)),
                   pl.BlockSpec((1, LANE, w2), lambda b, t: (b, 0, 0)),
                   pl.BlockSpec((1, w2, LANE), lambda b, t: (b, 0, 0)),
                   pl.BlockSpec((1, 1, w2), lambda b, t: (b, 0, 0)),
                   pl.BlockSpec((1, LANE), lambda b, t: (0, b))],
        out_shape=[jax.ShapeDtypeStruct((l, D_HALF), BF16),
                   jax.ShapeDtypeStruct((S5_LANE_BLOCKS, LANE, w2), F32),
                   jax.ShapeDtypeStruct((S5_LANE_BLOCKS, w2, LANE), F32),
                   jax.ShapeDtypeStruct((S5_LANE_BLOCKS, 1, w2), F32),
                   jax.ShapeDtypeStruct((1, D_HALF), F32)],
        scratch_shapes=[pltpu.VMEM((tt, w2), F32), pltpu.VMEM((SUBLANE, w2), F32)],
        args=[proj, dy, states, states, wb, wc, d_skip, sc_rev], parallel=False, exchanges=exchanges)


def _glu_fwd(ypre, wg, bg, cat, *, name):
    l, d = ypre.shape
    tl = _pick_tile(l, 512, SUBLANE)

    def body(y_ref, w_ref, b_ref, cat_ref, o_ref):
        yg = _gelu(y_ref[...])
        o_ref[...] = (yg * _sigmoid(_dot(yg, w_ref[...], _NN) + b_ref[...])).astype(BF16)

    row = pl.BlockSpec((tl, d), lambda i: (i, 0))
    return pl.pallas_call(
        body, name=name, grid=(l // tl,),
        in_specs=[row, pl.BlockSpec((d, d), lambda i: (0, 0)), pl.BlockSpec((1, d), lambda i: (0, 0)), _ANY],
        out_specs=pl.BlockSpec((tl, d), lambda i: (i, 1)), out_shape=jax.ShapeDtypeStruct(cat.shape, cat.dtype),
        input_output_aliases={3: 0}, compiler_params=_PAR)(ypre, wg, bg, cat)


def _glu_bwd(dcat, ypre, wg, bg, *, name):
    l, d = ypre.shape
    tl = _pick_tile(l, 512, SUBLANE)

    def body(dy_ref, y_ref, w_ref, b_ref, dyp_ref, dw_ref, db_ref):
        @pl.when(pl.program_id(0) == 0)
        def _():
            dw_ref[...] = jnp.zeros_like(dw_ref)
            db_ref[...] = jnp.zeros_like(db_ref)
        yp = y_ref[...]
        dyb = dy_ref[...]
        yg = _gelu(yp)
        sg = _sigmoid(_dot(yg, w_ref[...], _NN) + b_ref[...])
        dz = dyb * yg * sg * (1.0 - sg)
        dyg = dyb * sg + _dot(dz, w_ref[...], _NT)
        dyp_ref[...] = dyg * _gelu_grad(yp)
        dw_ref[...] += _dot(yg, dz, _TN)
        db_ref[...] += jnp.sum(dz, axis=0, keepdims=True)

    row = pl.BlockSpec((tl, d), lambda i: (i, 0))
    mat = pl.BlockSpec((d, d), lambda i: (0, 0))
    vec = pl.BlockSpec((1, d), lambda i: (0, 0))
    return pl.pallas_call(
        body, name=name, grid=(l // tl,),
        in_specs=[pl.BlockSpec((tl, d), lambda i: (i, 1)), row, mat, vec], out_specs=[row, mat, vec],
        out_shape=[jax.ShapeDtypeStruct((l, d), F32), jax.ShapeDtypeStruct((d, d), F32),
                   jax.ShapeDtypeStruct((1, d), F32)],
        compiler_params=_ARB)(dcat, ypre, wg, bg)


def _window_sum(x, w, trailing):
    s, d = x, 1
    while d < w:
        s = s + (_shift_dn(s, d) if trailing else _shift_up(s, d))
        d *= 2
    return s


def _window_count(shape, w):
    row = lax.broadcasted_iota(jnp.int32, shape, 0)
    return jnp.minimum(row + 1, w).astype(F32)


def _pool_fwd(proj, pw, scale, *, name):
    l = proj.shape[0]
    ng = len(POOL_WINDOWS)

    def body(z_ref, w_ref, sc_ref, o_ref):
        z = z_ref[...]
        for k, w in enumerate(POOL_WINDOWS):
            @pl.when(pl.program_id(0) == k)
            def _():
                pooled = _window_sum(z, w, True) / _window_count(z.shape, w) - z
                o_ref[...] = (_dot(pooled, w_ref[0], _NN) * sc_ref[...]).astype(BF16)

    col = pl.BlockSpec((l, LANE), lambda g: (0, g))
    return pl.pallas_call(
        body, name=name, grid=(ng,),
        in_specs=[col, pl.BlockSpec((1, LANE, LANE), lambda g: (g, 0, 0)), pl.BlockSpec((1, LANE), lambda g: (0, g))],
        out_specs=col, out_shape=jax.ShapeDtypeStruct((l, 2 * D_HALF), BF16), compiler_params=_PAR)(proj, pw, scale)


def _pool_bwd(proj, dcat, pw, scale, *, name):
    l = proj.shape[0]
    ng = len(POOL_WINDOWS)

    def body(z_ref, dy_ref, w_ref, sc_ref, dz_ref, dw_ref, dsc_ref):
        z = z_ref[...]
        dy = dy_ref[...]
        for k, w in enumerate(POOL_WINDOWS):
            @pl.when(pl.program_id(0) == k)
            def _():
                cnt = _window_count(z.shape, w)
                pooled = _window_sum(z, w, True) / cnt - z
                ypre = _dot(pooled, w_ref[0], _NN)
                dsc_ref[...] = jnp.sum(dy * ypre, axis=0, keepdims=True)
                dyp = dy * sc_ref[...]
                dw_ref[0] = _dot(pooled, dyp, _TN)
                dpool = _dot(dyp, w_ref[0], _NT)
                dz_ref[...] = (_window_sum(dpool / cnt, w, False) - dpool).astype(BF16)

    col = pl.BlockSpec((l, LANE), lambda g: (0, g))
    mat = pl.BlockSpec((1, LANE, LANE), lambda g: (g, 0, 0))
    vec = pl.BlockSpec((1, LANE), lambda g: (0, g))
    return pl.pallas_call(
        body, name=name, grid=(ng,), in_specs=[col, col, mat, vec], out_specs=[col, mat, vec],
        out_shape=[jax.ShapeDtypeStruct((l, D_HALF), BF16), jax.ShapeDtypeStruct((ng, LANE, LANE), F32),
                   jax.ShapeDtypeStruct((1, D_HALF), F32)],
        compiler_params=_PAR)(proj, dcat, pw, scale)


def _tril_mask():
    r = lax.broadcasted_iota(jnp.int32, (CHUNK, CHUNK), 0)
    c = lax.broadcasted_iota(jnp.int32, (CHUNK, CHUNK), 1)
    return r >= c


def _sgu_fwd(proj, ng, sw, sb_t, cat, *, name):
    l = proj.shape[0]
    tl = _pick_tile(l, 512, CHUNK)

    def body(su_ref, sv_ref, g_ref, w_ref, b_ref, cat_ref, o_ref):
        su = _gelu(su_ref[...])
        sv = _gelu(sv_ref[...])
        r = lax.rsqrt(jnp.mean(sv * sv, axis=-1, keepdims=True) + EPS)
        v = sv * r * g_ref[...]
        mask = _tril_mask()
        for h in range(SGU_HEADS):
            wm = jnp.where(mask, w_ref[h], 0.0)
            cs = slice(h * LANE, (h + 1) * LANE)
            for n in range(tl // CHUNK):
                rs = slice(n * CHUNK, (n + 1) * CHUNK)
                mixed = _dot(wm, v[rs, cs], _NN) + b_ref[:, h:h + 1]
                o_ref[rs, cs] = (su[rs, cs] * mixed).astype(BF16)

    blk = lambda c: pl.BlockSpec((tl, D_HALF), lambda i: (i, c))
    return pl.pallas_call(
        body, name=name, grid=(l // tl,),
        in_specs=[blk(1), blk(2), pl.BlockSpec((1, D_HALF), lambda i: (0, 0)),
                  pl.BlockSpec((SGU_HEADS, CHUNK, CHUNK), lambda i: (0, 0, 0)),
                  pl.BlockSpec((CHUNK, SGU_HEADS), lambda i: (0, 0)), _ANY],
        out_specs=blk(1), out_shape=jax.ShapeDtypeStruct(cat.shape, cat.dtype), input_output_aliases={5: 0},
        compiler_params=_PAR)(proj, proj, ng, sw, sb_t, cat)


def _sgu_bwd(proj, dcat, ng, sw, sb_t, *, name):
    l = proj.shape[0]
    tl = _pick_tile(l, 512, CHUNK)

    def body(su_ref, sv_ref, dy_ref, g_ref, w_ref, b_ref, dsu_ref, dsv_ref, dw_ref, dbm_ref, dng_ref, dv_scr):
        @pl.when(pl.program_id(0) == 0)
        def _():
            dw_ref[...] = jnp.zeros_like(dw_ref)
            dbm_ref[...] = jnp.zeros_like(dbm_ref)
            dng_ref[...] = jnp.zeros_like(dng_ref)
        su_pre = su_ref[...]
        sv_pre = sv_ref[...]
        su = _gelu(su_pre)
        sv = _gelu(sv_pre)
        gsu = _gelu_grad(su_pre)
        gv = g_ref[...]
        r = lax.rsqrt(jnp.mean(sv * sv, axis=-1, keepdims=True) + EPS)
        v = sv * r * gv
        dy = dy_ref[...]
        mask = _tril_mask()
        for h in range(SGU_HEADS):
            wm = jnp.where(mask, w_ref[h], 0.0)
            cs = slice(h * LANE, (h + 1) * LANE)
            dw_acc = jnp.zeros((CHUNK, CHUNK), F32)
            db_acc = jnp.zeros((CHUNK, LANE), F32)
            for n in range(tl // CHUNK):
                rs = slice(n * CHUNK, (n + 1) * CHUNK)
                vb = v[rs, cs]
                mixed = _dot(wm, vb, _NN) + b_ref[:, h:h + 1]
                dyb = dy[rs, cs]
                dsu_ref[rs, cs] = (dyb * mixed * gsu[rs, cs]).astype(BF16)
                dmix = dyb * su[rs, cs]
                db_acc = db_acc + dmix
                dw_acc = dw_acc + _dot(dmix, vb, _NT)
                dv_scr[rs, cs] = _dot(wm, dmix, _TN)
            dw_ref[h] += jnp.where(mask, dw_acc, 0.0)
            dbm_ref[h] += db_acc
        dv = dv_scr[...]
        a = dv * gv
        m = jnp.mean(a * sv, axis=-1, keepdims=True)
        dsv = r * a - sv * (r * r * r) * m
        dng_ref[...] += jnp.sum(dv * sv * r, axis=0, keepdims=True)
        dsv_ref[...] = (dsv * _gelu_grad(sv_pre)).astype(BF16)

    blk = lambda c: pl.BlockSpec((tl, D_HALF), lambda i: (i, c))
    mats = pl.BlockSpec((SGU_HEADS, CHUNK, CHUNK), lambda i: (0, 0, 0))
    vec = pl.BlockSpec((1, D_HALF), lambda i: (0, 0))
    piece = jax.ShapeDtypeStruct((l, D_HALF), BF16)
    mshape = jax.ShapeDtypeStruct((SGU_HEADS, CHUNK, CHUNK), F32)
    return pl.pallas_call(
        body, name=name, grid=(l // tl,),
        in_specs=[blk(1), blk(2), blk(1), vec, mats, pl.BlockSpec((CHUNK, SGU_HEADS), lambda i: (0, 0))],
        out_specs=[blk(0), blk(0), mats, mats, vec],
        out_shape=[piece, piece, mshape, mshape, jax.ShapeDtypeStruct((1, D_HALF), F32)],
        scratch_shapes=[pltpu.VMEM((tl, D_HALF), F32)],
        compiler_params=_ARB)(proj, proj, dcat, ng, sw, sb_t)


def _s5_params(w):
    prep_args = (w["ssm_log_step"], w["ssm_a_re"], w["ssm_a_im"], w["ssm_b_re"], w["ssm_b_im"])
    (lr, li, br, bi), prep_vjp = jax.vjp(jax.vmap(_s5_prep), *prep_args)
    wb, wc = jax.vmap(_s5_mats)(br, bi, w["ssm_c_re"], w["ssm_c_im"])
    consts = lambda reverse: jax.vmap(functools.partial(_s5_scan_consts, reverse=reverse))(*prep_args[:3])
    return dict(wb=wb.astype(BF16), wc=wc.astype(BF16), d=w["ssm_d"][:, None, :], sc=consts(False),
                sc_rev=consts(True), prep_vjp=prep_vjp)


def _s5_param_grads(s5, dwb, dwc, dlam, dd):
    dbr, dbi, dcr, dci = jax.vmap(_s5_mats_bwd)(dwb, dwc)
    n = dlam.shape[0]
    dlr = dlam[:, :, 0, :S5_STATE_LANES].reshape(n, N_SSM_GROUPS, SSM_STATE)
    dli = dlam[:, :, 0, S5_STATE_LANES:].reshape(n, N_SSM_GROUPS, SSM_STATE)
    dls, dar, dai, db_re, db_im = s5["prep_vjp"]((dlr, dli, dbr, dbi))
    return dict(ssm_log_step=dls, ssm_a_re=dar, ssm_a_im=dai, ssm_b_re=db_re, ssm_b_im=db_im, ssm_c_re=dcr,
                ssm_c_im=dci, ssm_d=dd[:, 0, :])


TILES = {
    "mm_up": (4096, 512), "mm_up_dw": (1024, 1408), "mm_down_dx": (1024, 2816),
    "mm_down": (512, 1024), "mm_down_dw": (256, 1024), "mm_even_in": (2048, 1024), "mm_odd_in": (2048, 768),
    "mm_mix_out": (1024, 1024), "mm_mix_out_dx": (2048, 1024), "mm_mix_out_dw": (1024, 512),
    "mm_even_in_dw": (1024, 512), "mm_odd_in_dw": (1024, 512),
}


def _layer_weights(i):
    j = i // 2
    mixer = [("even_w_in", j), ("even_w_out", j), ("ssm_glu_w", j)] if i % 2 == 0 else [("odd_w_in", j),
                                                                                         ("odd_w_out", j)]
    return dict(w_in=mixer[0], w_out=mixer[1], glu=mixer[2:], up=("ffn_w_up", i), down=("ffn_w_down", i))


class _LocalWeights:
    def __init__(self, w):
        self.w, self.grads = w, {}

    def carried_by(self, stage, i):
        return []

    def delivered(self, stage, i, outs):
        pass

    def weight(self, key):
        return self.w[key[0]][key[1]]

    def grad(self, key, dw):
        self.grads[key] = dw


class _ShardedWeights:
    def __init__(self, shards):
        self.shards = shards
        self.full, self.halves, self.pending, self.scattered = {}, {}, {}, {}

    def start(self, others):
        keys = [_layer_weights(0)["w_in"]]
        outs = _exchange_only(others + [self._gather(k) for k in keys], name="ag_first")
        self._take(keys, outs[len(others):])
        return outs[:len(others)]

    def _gather(self, key):
        shard = self.shards[key[0]][key[1]]
        if len(key) == 2:
            return _Gather(shard)
        rows = shard.shape[0] // 2
        buf = self.halves.get(key[:2])
        if buf is None:
            buf = lax.empty((N_DEV,) + shard.shape, shard.dtype)
        return _Gather(shard[key[2] * rows:(key[2] + 1) * rows], into=buf, row0=key[2] * rows)

    def _take(self, keys, outs):
        for key, got in zip(keys, outs):
            if len(key) == 3 and key[:2] not in self.halves:
                self.halves[key[:2]] = got
                continue
            if BIG[key[0]] == 2:
                self.full[key[:2]] = jnp.swapaxes(got, 0, 1).reshape(got.shape[1], -1)
            else:
                self.full[key[:2]] = got.reshape(-1, got.shape[2])

    def _plan(self, stage, i):
        cur = _layer_weights(i)
        nxt = _layer_weights(i + 1) if i + 1 < DEPTH else None
        has_scan = lambda k: k % 2 == 0
        none = ([], [])
        up_half = lambda h: ([(*nxt["up"], h)], []) if nxt and not has_scan(i + 1) else none
        return {
            "mm_in": ([cur["w_out"], *cur["glu"]], []) if i == 0 else none,
            "s5_fwd": ([cur["up"]], []),
            "mm_up": ([cur["down"]] + ([nxt["w_in"], nxt["w_out"], *nxt["glu"]] if nxt else []), []),
            "ffn_act": up_half(0),
            "mm_down": up_half(1),
            "ffn_act_bwd": ([], [cur["down"]] + ([] if not nxt else [nxt["w_in"]] if not has_scan(i + 1)
                                                 else [nxt["w_out"], *nxt["glu"]])),
            "mm_up_dx": none if not nxt else ([], [(*nxt["up"], 0)]) if not has_scan(i + 1) else ([], [nxt["w_in"]]),
            "mm_up_dw": ([], [(*nxt["up"], 1)]) if nxt and not has_scan(i + 1) else none,
            "s5_bwd": ([], [cur["up"]] + ([nxt["w_out"]] if nxt else [])),
            "mm_in_dw": ([], [cur["w_out"], *cur["glu"]]) if i == 0 else none,
            "mm_in_dx": ([], [cur["w_in"]]) if i == 0 else none,
        }[stage]

    def _scatter(self, key):
        name, layer = key[:2]
        layers, rows, cols = self.shards[name].shape
        if len(key) == 3:
            src = self.pending[key[:2]] if key[2] == 0 else self.pending.pop(key[:2])
            src = src[:, key[2] * (rows // 2):(key[2] + 1) * (rows // 2)]
            row0 = layer * rows + key[2] * (rows // 2)
        else:
            src, row0 = self.pending.pop(key), layer * rows
        if name not in self.scattered:
            self.scattered[name] = lax.empty((N_DEV, layers * rows, cols), src.dtype)
        return _Scatter(src, into=self.scattered[name], row0=row0)

    def carried_by(self, stage, i):
        gather, scatter = self._plan(stage, i)
        return [self._gather(k) for k in gather] + [self._scatter(k) for k in scatter]

    def delivered(self, stage, i, outs):
        gather, scatter = self._plan(stage, i)
        self._take(gather, outs[:len(gather)])
        for key, buf in zip(scatter, outs[len(gather):]):
            self.scattered[key[0]] = buf

    def weight(self, key):
        return self.full[key]

    def grad(self, key, dw):
        self.pending[key] = _to_dest_major(dw, BIG[key[0]] - 1).astype(BF16)

    def finish(self, carrier, others):
        keys = list(self.pending)
        res, outs = carrier(others + [self._scatter(k) for k in keys])
        for (name, _), buf in zip(keys, outs[len(others):]):
            self.scattered[name] = buf
        return res, outs[:len(others)]


def _device_step(x, tgt, w, comm):
    saved = []
    s5 = _s5_params(w)
    h = _rms_fwd(x, w["norm_mix_g"][0:1], name="rms_fwd")
    for i in range(DEPTH):
        j = i // 2
        lw = _layer_weights(i)
        if i % 2 == 0:
            proj = _carry(comm, "mm_in", i, _mm, h, comm.weight(lw["w_in"]), "nn", name="mm_even_in")
            ya = _sconv_fwd(proj, w["even_conv_w"][j], name="sconv_fwd")
            states, ypre = _carry(comm, "s5_fwd", i, _s5_fwd, proj, s5["wb"][j], s5["wc"][j], s5["d"][j],
                                  s5["sc"][j], name="s5_fwd")
            cat = _glu_fwd(ypre, comm.weight(lw["glu"][0]), w["ssm_glu_b"][j][None, :], ya, name="glu_fwd")
            mix = (states, ypre)
        else:
            proj = _carry(comm, "mm_in", i, _mm, h, comm.weight(lw["w_in"]), "nn", name="mm_odd_in")
            yc = _pool_fwd(proj, w["pool_w"][j], w["pool_scale"][j][None, :], name="pool_fwd")
            sb_t = jnp.transpose(w["sgu_b"][j])
            cat = _sgu_fwd(proj, w["sgu_norm_g"][j][None, :], w["sgu_w"][j], sb_t, yc, name="sgu_fwd")
            mix = (sb_t,)
        x1, h2 = _mm(cat, comm.weight(lw["w_out"]), "nn", add=x, norm=(w["norm_ffn_g"], i), name="mm_mix_out")
        up = _carry(comm, "mm_up", i, _mm, h2, comm.weight(lw["up"]), "nn", name="mm_up")
        act, gv = _carry(comm, "ffn_act", i, _ffn_act, up, w["ffn_conv_w"][i], w["ffn_conv_b"][i:i + 1],
                         name="ffn_act")
        if i + 1 < DEPTH:
            x2, h_next = _carry(comm, "mm_down", i, _mm, act, comm.weight(lw["down"]), "nn", add=x1,
                                norm=(w["norm_mix_g"], i + 1), name="mm_down")
        else:
            x2, h_next = _mm(act, comm.weight(lw["down"]), "nn", add=x1, name="mm_down_last"), None
        saved.append((x, h, proj, cat, x1, h2, up, gv, act, mix))
        x, h = x2, h_next

    loss, dx, dxb, dgf = _loss_head(x, w["norm_final_g"][None, :], tgt, name="loss_head")
    per_layer = {}
    s5_grads = []

    def put(name, idx, val):
        per_layer.setdefault(name, {})[idx] = val

    for i in reversed(range(DEPTH)):
        j = i // 2
        lw = _layer_weights(i)
        x0, h, proj, cat, x1, h2, up, gv, act, mix = saved[i]
        dact = _mm(dxb, comm.weight(lw["down"]), "nt", name="mm_down_dx")
        comm.grad(lw["down"], _mm(act, dxb, "tn", out_dtype=BF16, name="mm_down_dw"))
        dup, dcw, dcb = _carry(comm, "ffn_act_bwd", i, _ffn_act_bwd, up, gv, dact, w["ffn_conv_w"][i],
                               name="ffn_act_bwd")
        put("ffn_conv_w", i, dcw)
        put("ffn_conv_b", i, dcb[0])
        dx1, dx1b, dg2 = _carry(comm, "mm_up_dx", i, _mm, dup, comm.weight(lw["up"]), "nt", tm_cap=512, tn_cap=D_MODEL,
                                norm_bwd=(x1, w["norm_ffn_g"], i, dx), name="mm_up_dx")
        comm.grad(lw["up"], _carry(comm, "mm_up_dw", i, _mm, h2, dup, "tn", out_dtype=BF16,
                                   name="mm_up_dw"))
        put("norm_ffn_g", i, dg2[0])
        dcat = _mm(dx1b, comm.weight(lw["w_out"]), "nt", name="mm_mix_out_dx")
        comm.grad(lw["w_out"], _mm(cat, dx1b, "tn", out_dtype=BF16, name="mm_mix_out_dw"))
        if i % 2 == 0:
            states, ypre = mix
            dxa, dba, dca, dcw_a = _sconv_bwd(proj, dcat, w["even_conv_w"][j], name="sconv_bwd")
            put("even_conv_w", j, dcw_a)
            dypre, dwg, dbg = _glu_bwd(dcat, ypre, comm.weight(lw["glu"][0]), w["ssm_glu_b"][j][None, :],
                                       name="glu_bwd")
            comm.grad(lw["glu"][0], dwg)
            put("ssm_glu_b", j, dbg[0])
            du, dwb, dwc, dlam, dd = _carry(comm, "s5_bwd", i, _s5_bwd, proj, dypre, states, s5["wb"][j], s5["wc"][j],
                                            s5["d"][j], s5["sc_rev"][j], name="s5_bwd")
            s5_grads.insert(0, (dwb, dwc, dlam, dd))
            dproj = jnp.concatenate([dxa, dba, dca, du], axis=1)
            in_name = "mm_even_in"
        else:
            (sb_t,) = mix
            dz, dpw, dps = _pool_bwd(proj, dcat, w["pool_w"][j], w["pool_scale"][j][None, :], name="pool_bwd")
            put("pool_w", j, dpw)
            put("pool_scale", j, dps[0])
            dsu, dsv, dsw, dbm, dng = _sgu_bwd(proj, dcat, w["sgu_norm_g"][j][None, :], w["sgu_w"][j], sb_t,
                                               name="sgu_bwd")
            put("sgu_w", j, dsw)
            put("sgu_b", j, jnp.sum(dbm, axis=-1))
            put("sgu_norm_g", j, dng[0])
            dproj = jnp.concatenate([dz, dsu, dsv], axis=1)
            in_name = "mm_odd_in"
        comm.grad(lw["w_in"], _carry(comm, "mm_in_dw", i, _mm, h, dproj, "tn", out_dtype=BF16, name=in_name + "_dw"))
        dx, dxb, dg1 = _carry(comm, "mm_in_dx", i, _mm, dproj, comm.weight(lw["w_in"]), "nt", tn_cap=D_MODEL,
                              norm_bwd=(x0, w["norm_mix_g"], i, dx1), name=in_name + "_dx")
        put("norm_mix_g", i, dg1[0])

    grads = {nm: [vals[k] for k in sorted(vals)] for nm, vals in per_layer.items()}
    grads.update({nm: jnp.stack(grads[nm]) for nm in SMALL})
    grads["norm_final_g"] = dgf[0]
    grads.update(_s5_param_grads(s5, *[jnp.stack(parts) for parts in zip(*s5_grads)]))
    return loss, dx, grads


def _carry(comm, stage, i, fn, *args, **kwargs):
    exchanges = comm.carried_by(stage, i)
    if not exchanges:
        return fn(*args, **kwargs)
    out, moved = fn(*args, exchanges=exchanges, **kwargs)
    comm.delivered(stage, i, moved)
    return out


def _sum_parts(parts, *, name):
    g, r, c = parts.shape
    tr = _pick_tile(r, max(16, EXCHANGE_BLOCK_ELEMS // c), 16)

    def body(p_ref, o_ref):
        acc = p_ref[0].astype(F32)
        for k in range(1, g):
            acc = acc + p_ref[k].astype(F32)
        o_ref[...] = acc

    return pl.pallas_call(
        body, name=name, grid=(r // tr,), in_specs=[pl.BlockSpec((g, tr, c), lambda i: (0, i, 0))],
        out_specs=pl.BlockSpec((tr, c), lambda i: (i, 0)), out_shape=jax.ShapeDtypeStruct((r, c), F32),
        compiler_params=_PAR)(parts)


def _adamw(w, m, v, g_parts, *, name, exchanges=()):
    r, c = w.shape
    g = g_parts.shape[0]
    tc = _pick_tile(c, 8192, LANE)
    tr = _pick_tile(r, max(16, (1 << 18) // tc), 16)
    c1 = 1.0 - ADAM_B1 ** ADAM_STEP
    c2 = 1.0 - ADAM_B2 ** ADAM_STEP

    def body(w_ref, m_ref, v_ref, g_ref, go_ref, d_ref, mo_ref, vo_ref):
        grad = g_ref[0].astype(F32)
        for k in range(1, g):
            grad = grad + g_ref[k].astype(F32)
        m_new = ADAM_B1 * m_ref[...] + (1.0 - ADAM_B1) * grad
        v_new = ADAM_B2 * v_ref[...] + (1.0 - ADAM_B2) * (grad * grad)
        go_ref[...] = grad
        mo_ref[...] = m_new
        vo_ref[...] = v_new
        d_ref[...] = -ADAM_LR * ((m_new / c1) / (jnp.sqrt(v_new / c2) + ADAM_EPS) + ADAM_WD * w_ref[...])

    blk = pl.BlockSpec((tr, tc), lambda i, j: (i, j))
    out = jax.ShapeDtypeStruct((r, c), F32)
    return _call(body, name=name, grid=(r // tr, c // tc),
                 in_specs=[blk, blk, blk, pl.BlockSpec((g, tr, tc), lambda i, j: (0, i, j))],
                 out_specs=[blk, blk, blk, blk], out_shape=[out, out, out, out], args=[w, m, v, g_parts],
                 exchanges=exchanges)


WEIGHT_NAMES = ['norm_mix_g', 'even_w_in', 'even_conv_w', 'ssm_log_step', 'ssm_a_re', 'ssm_a_im', 'ssm_b_re',
                'ssm_b_im', 'ssm_c_re', 'ssm_c_im', 'ssm_d', 'ssm_glu_w', 'ssm_glu_b', 'even_w_out', 'odd_w_in',
                'pool_w', 'pool_scale', 'sgu_norm_g', 'sgu_w', 'sgu_b', 'odd_w_out', 'norm_ffn_g', 'ffn_w_up',
                'ffn_conv_w', 'ffn_conv_b', 'ffn_w_down', 'norm_final_g']
BIG = {'even_w_in': 2, 'ssm_glu_w': 1, 'even_w_out': 1, 'odd_w_in': 2, 'odd_w_out': 1, 'ffn_w_up': 2,
       'ffn_w_down': 1}
SMALL = {'even_conv_w': 2, 'pool_scale': 1, 'sgu_norm_g': 1, 'ffn_conv_w': 2}
BIG_ROWS = 512
SMALL_ROWS = 16


def _pad_to(n, q):
    return -(-n // q) * q


def _pack(arrays, dtype, rows, lead=()):
    flat = [a.reshape(lead + (-1,)).astype(dtype) for a in arrays]
    n = sum(f.shape[-1] for f in flat)
    pad = _pad_to(n, rows * LANE) - n
    if pad:
        flat.append(jnp.zeros(lead + (pad,), dtype))
    return jnp.concatenate(flat, axis=-1).reshape(lead + (rows, -1))


def _unpack(buf, shapes, lead=()):
    flat = buf.reshape(lead + (-1,))
    out, off = [], 0
    for shp in shapes:
        n = math.prod(shp)
        out.append(flat[..., off:off + n].reshape(lead + tuple(shp)))
        off += n
    return out


def _to_dest_major(full, axis):
    shp = full.shape
    split = full.reshape(shp[:axis] + (N_DEV, shp[axis] // N_DEV) + shp[axis + 1:])
    return jnp.moveaxis(split, axis, 0)


def _from_dest_major(blocks, axis):
    moved = jnp.moveaxis(blocks, 0, axis)
    shp = moved.shape
    return moved.reshape(shp[:axis] + (shp[axis] * shp[axis + 1],) + shp[axis + 2:])


def _rows_2d(a):
    return a.reshape(-1, a.shape[-1])


def _rows_2d_lead(a):
    return a.reshape(a.shape[0], -1, a.shape[-1])


def kernel(x, norm_mix_g, even_w_in, even_conv_w, ssm_log_step, ssm_a_re, ssm_a_im, ssm_b_re, ssm_b_im, ssm_c_re, ssm_c_im, ssm_d, ssm_glu_w, ssm_glu_b, even_w_out, odd_w_in, pool_w, pool_scale, sgu_norm_g, sgu_w, sgu_b, odd_w_out, norm_ffn_g, ffn_w_up, ffn_conv_w, ffn_conv_b, ffn_w_down, norm_final_g, loss_target, m_norm_mix_g, m_even_w_in, m_even_conv_w, m_ssm_log_step, m_ssm_a_re, m_ssm_a_im, m_ssm_b_re, m_ssm_b_im, m_ssm_c_re, m_ssm_c_im, m_ssm_d, m_ssm_glu_w, m_ssm_glu_b, m_even_w_out, m_odd_w_in, m_pool_w, m_pool_scale, m_sgu_norm_g, m_sgu_w, m_sgu_b, m_odd_w_out, m_norm_ffn_g, m_ffn_w_up, m_ffn_conv_w, m_ffn_conv_b, m_ffn_w_down, m_norm_final_g, v_norm_mix_g, v_even_w_in, v_even_conv_w, v_ssm_log_step, v_ssm_a_re, v_ssm_a_im, v_ssm_b_re, v_ssm_b_im, v_ssm_c_re, v_ssm_c_im, v_ssm_d, v_ssm_glu_w, v_ssm_glu_b, v_even_w_out, v_odd_w_in, v_pool_w, v_pool_scale, v_sgu_norm_g, v_sgu_w, v_sgu_b, v_odd_w_out, v_norm_ffn_g, v_ffn_w_up, v_ffn_conv_w, v_ffn_conv_b, v_ffn_w_down, v_norm_final_g):
    given = dict(locals())
    wts = {n: given[n] for n in WEIGHT_NAMES}
    mom = {n: given["m_" + n] for n in WEIGHT_NAMES}
    var = {n: given["v_" + n] for n in WEIGHT_NAMES}
    repl = [n for n in WEIGHT_NAMES if n not in BIG and n not in SMALL]

    small_shapes = [wts[n].shape for n in SMALL]
    comm = _ShardedWeights({n: wts[n].astype(BF16) for n in BIG})
    (small_all,) = comm.start([_Gather(_pack([wts[n] for n in SMALL], F32, SMALL_ROWS))])
    full = {n: wts[n] for n in repl}
    for n, blocks in zip(SMALL, _unpack(small_all, small_shapes, lead=(N_DEV,))):
        full[n] = _from_dest_major(blocks, SMALL[n])

    loss, dx, grads = _device_step(x[0], loss_target[0], full, comm)

    repl_shapes = [wts[n].shape for n in repl]
    pieces = [p.reshape(-1) for n in repl for p in (grads[n] if isinstance(grads[n], list) else [grads[n]])]
    repl_flat = jnp.concatenate(pieces + [loss.reshape(-1)])
    n_repl = repl_flat.shape[0]
    chunk = _pad_to(-(-n_repl // N_DEV), SMALL_ROWS * LANE)
    repl_flat = jnp.pad(repl_flat, (0, N_DEV * chunk - n_repl))
    small_part = _pack([_to_dest_major(grads[n], SMALL[n]) for n in SMALL], F32, SMALL_ROWS, lead=(N_DEV,))
    small_cols = small_part.shape[2]
    small_scatter = _Scatter(
        jnp.concatenate([small_part, repl_flat.reshape(N_DEV, SMALL_ROWS, chunk // SMALL_ROWS)], axis=2))

    _, (small_rs,) = comm.finish(lambda exchanges: (None, _exchange_only(exchanges, name="rs_last")), [small_scatter])
    small_sum = _sum_parts(small_rs, name="rs_sum_small")
    (repl_all,) = _exchange_only([_Gather(small_sum[:, small_cols:])], name="ag_repl")
    repl_sum = repl_all.reshape(-1)
    total_loss = repl_sum[n_repl - 1]

    out = {}
    for n in BIG:
        res = _adamw(_rows_2d(wts[n]), _rows_2d(mom[n]), _rows_2d(var[n]), comm.scattered[n], name="adamw_" + n)
        out[n] = [r.reshape(wts[n].shape) for r in res]

    def small_vec(shard_part, repl_part):
        flat = jnp.concatenate([shard_part.reshape(-1), repl_part])
        return flat.reshape(SMALL_ROWS, -1)

    def small_tree(tree):
        tail = jnp.concatenate([tree[n].reshape(-1) for n in repl])
        tail = jnp.pad(tail, (0, N_DEV * chunk - tail.shape[0]))
        return small_vec(_pack([tree[n] for n in SMALL], F32, SMALL_ROWS), tail)

    res = _adamw(small_tree(wts), small_tree(mom), small_tree(var), small_vec(small_sum[:, :small_cols], repl_sum)[None],
                 name="adamw_small")
    n_small = SMALL_ROWS * small_cols
    for k, r in enumerate(res):
        flat = r.reshape(-1)
        shard = _unpack(flat[:n_small], small_shapes)
        rest = _unpack(flat[n_small:], repl_shapes)
        for n, val in zip(SMALL, shard):
            out.setdefault(n, [None] * 4)[k] = val
        for n, val in zip(repl, rest):
            out.setdefault(n, [None] * 4)[k] = val

    grad_x = dx[None]
    return (total_loss, grad_x, *[out[n][0] for n in WEIGHT_NAMES], *[out[n][1] for n in WEIGHT_NAMES],
            *[out[n][2] for n in WEIGHT_NAMES], *[out[n][3] for n in WEIGHT_NAMES])
```

```python
import functools
import math

import jax
import jax.numpy as jnp
from jax import lax
from jax.experimental import pallas as pl
from jax.experimental.pallas import tpu as pltpu

F32 = jnp.float32
BF16 = jnp.bfloat16

D_MODEL = 1024
DEPTH = 4
D_HALF = D_MODEL // 2
SSM_GROUP = 16
N_SSM_GROUPS = D_HALF // SSM_GROUP
SSM_STATE = 64
POOL_WINDOWS = (2, 4, 8, 16)
SGU_HEADS = 4
CHUNK = 128
D_FF = 2816
CONV_WIDTH = 3
EPS = 1e-6
N_DEV = 8

ADAM_LR = 0.001
ADAM_B1 = 0.9
ADAM_B2 = 0.999
ADAM_EPS = 1e-08
ADAM_WD = 0.01
ADAM_STEP = 10

LANE = 128
SUBLANE = 8
S5_LANE_BLOCKS = D_HALF // LANE
S5_STATE_LANES = (N_SSM_GROUPS // S5_LANE_BLOCKS) * SSM_STATE
S5_TIME_CHUNK = 512
EXCHANGE_BLOCK_ELEMS = 1 << 20

GELU_K = math.sqrt(2.0 / math.pi)
GELU_C = 0.044715

_ARB = pltpu.CompilerParams(dimension_semantics=("arbitrary",))
_PAR = pltpu.CompilerParams(dimension_semantics=("parallel",))


def _pick_tile(n, cap, mult):
    if n <= cap:
        return n
    best = None
    for t in range(mult, cap + 1, mult):
        if n % t == 0:
            best = t
    assert best is not None, (n, cap, mult)
    return best


_MESH = pl.DeviceIdType.MESH
_ANY = pl.BlockSpec(memory_space=pl.ANY)
SEMS_PER_EXCHANGE = N_DEV - 1


class _Gather:
    into = None

    def __init__(self, src):
        self.src = src
        self.out_shape = jax.ShapeDtypeStruct((N_DEV,) + src.shape, src.dtype)

    def copies(self, x_ref, out_ref, send_sems, recv_sems, local_sem):
        x, y, cc = lax.axis_index("x"), lax.axis_index("y"), lax.axis_index("c")
        me, sibling = (x, y, cc), (x, y, 1 - cc)
        chips = [(1 - x, y), (x, 1 - y), (1 - x, 1 - y)]

        def rows(px, py, pc):
            return out_ref.at[4 * px + 2 * py + pc]

        def copy(k, block, to, src=None):
            return pltpu.make_async_remote_copy(
                src_ref=rows(*block) if src is None else src, dst_ref=rows(*block),
                send_sem=send_sems.at[k], recv_sem=recv_sems.at[k], device_id=to, device_id_type=_MESH)

        return dict(
            mine=pltpu.make_async_copy(x_ref, rows(*me), local_sem),
            first=[copy(0, me, sibling, src=x_ref)] + [copy(1 + k, me, (*chip, cc), src=x_ref)
                                                       for k, chip in enumerate(chips)],
            passed=[copy(4 + k, (*chip, cc), sibling) for k, chip in enumerate(chips)],
            over_ici=[copy(1 + k, (*chip, cc), me) for k, chip in enumerate(chips)],
            from_sibling=[copy(0, sibling, me)] + [copy(4 + k, (*chip, 1 - cc), me) for k, chip in enumerate(chips)])

    def start(self, *refs):
        cps = self.copies(*refs)
        cps["mine"].start()
        for cp in cps["first"]:
            cp.start()

    def finish(self, *refs):
        cps = self.copies(*refs)
        for arrived, onward in zip(cps["over_ici"], cps["passed"]):
            arrived.wait_recv()
            onward.start()
        for arrived in cps["from_sibling"]:
            arrived.wait_recv()
        for cp in cps["first"] + cps["passed"]:
            cp.wait_send()
        cps["mine"].wait()


class _Scatter:
    def __init__(self, src, into=None, row0=0):
        self.src, self.into, self.row0 = src, into, row0
        whole = src if into is None else into
        self.out_shape = jax.ShapeDtypeStruct(whole.shape, whole.dtype)

    def copies(self, p_ref, whole_ref, send_sems, recv_sems, local_sem):
        x, y, cc = lax.axis_index("x"), lax.axis_index("y"), lax.axis_index("c")
        me = 4 * x + 2 * y + cc
        rows = self.src.shape[1]
        out_ref = whole_ref if self.into is None else whole_ref.at[:, pl.ds(self.row0, rows)]
        sends, arrivals = [], []
        for k in range(1, N_DEV):
            px = (1 - x) if k & 4 else x
            py = (1 - y) if k & 2 else y
            pc = (1 - cc) if k & 1 else cc
            peer = 4 * px + 2 * py + pc
            kw = dict(send_sem=send_sems.at[k - 1], recv_sem=recv_sems.at[k - 1], device_id=(px, py, pc),
                      device_id_type=_MESH)
            sends.append(pltpu.make_async_remote_copy(src_ref=p_ref.at[peer], dst_ref=out_ref.at[me], **kw))
            arrivals.append(pltpu.make_async_remote_copy(src_ref=p_ref.at[me], dst_ref=out_ref.at[peer], **kw))
        return dict(mine=pltpu.make_async_copy(p_ref.at[me], out_ref.at[me], local_sem), sends=sends,
                    arrivals=arrivals)

    def start(self, *refs):
        cps = self.copies(*refs)
        cps["mine"].start()
        for cp in cps["sends"]:
            cp.start()

    def finish(self, *refs):
        cps = self.copies(*refs)
        for cp in cps["arrivals"]:
            cp.wait_recv()
        for cp in cps["sends"]:
            cp.wait_send()
        cps["mine"].wait()


class _SemView:
    def __init__(self, ref, lo):
        self.ref, self.lo = ref, lo

    @property
    def at(self):
        return self

    def __getitem__(self, k):
        return self.ref.at[self.lo + k]


def _call(body, *, name, grid, in_specs, out_specs, out_shape, args, scratch_shapes=(), parallel=True, exchanges=()):
    n_axes = len(grid)
    if not exchanges:
        sem = ("parallel" if parallel else "arbitrary",) * n_axes
        return pl.pallas_call(
            body, name=name, grid=grid, in_specs=in_specs, out_specs=out_specs, out_shape=out_shape,
            scratch_shapes=scratch_shapes, compiler_params=pltpu.CompilerParams(dimension_semantics=sem))(*args)
    single = not isinstance(out_shape, (list, tuple))
    out_specs = [out_specs] if single else list(out_specs)
    out_shape = [out_shape] if single else list(out_shape)
    n_in, n_out, n_scr, n_x = len(in_specs), len(out_specs), len(scratch_shapes), len(exchanges)
    landing = [(e, ex.into) for e, ex in enumerate(exchanges) if ex.into is not None]
    aliases = {n_in + n_x + pos: n_out + e for pos, (e, _) in enumerate(landing)}

    def wrapped(*refs):
        ins, refs = refs[:n_in], refs[n_in:]
        x_in, refs = refs[:n_x], refs[n_x + len(landing):]
        outs, refs = refs[:n_out], refs[n_out:]
        x_out, refs = refs[:n_x], refs[n_x:]
        scr, (send_sems, recv_sems, local_sems) = refs[:n_scr], refs[n_scr:]
        ids = [pl.program_id(k) for k in range(n_axes)]
        first = functools.reduce(jnp.logical_and, [i == 0 for i in ids])
        last = functools.reduce(jnp.logical_and, [i == g - 1 for i, g in zip(ids, grid)])

        def sems(e):
            lo = e * SEMS_PER_EXCHANGE
            return _SemView(send_sems, lo), _SemView(recv_sems, lo), local_sems.at[e]

        @pl.when(first)
        def _():
            for e, ex in enumerate(exchanges):
                ex.start(x_in[e], x_out[e], *sems(e))

        body(*ins, *outs, *scr)

        @pl.when(last)
        def _():
            for e, ex in enumerate(exchanges):
                ex.finish(x_in[e], x_out[e], *sems(e))

    res = pl.pallas_call(
        wrapped, name=name, grid=grid, in_specs=list(in_specs) + [_ANY] * (n_x + len(landing)),
        out_specs=out_specs + [_ANY] * n_x, out_shape=out_shape + [ex.out_shape for ex in exchanges],
        input_output_aliases=aliases,
        scratch_shapes=list(scratch_shapes) + [pltpu.SemaphoreType.DMA((n_x * SEMS_PER_EXCHANGE,)),
                                               pltpu.SemaphoreType.DMA((n_x * SEMS_PER_EXCHANGE,)),
                                               pltpu.SemaphoreType.DMA((n_x,))],
        compiler_params=pltpu.CompilerParams(dimension_semantics=("arbitrary",) * n_axes),
    )(*args, *[ex.src for ex in exchanges], *[buf for _, buf in landing])
    outs, x_outs = res[:n_out], res[n_out:]
    return (outs[0] if single else outs), x_outs


def _exchange_only(exchanges, *, name):
    def body():
        pass

    return _call(body, name=name, grid=(1,), in_specs=[], out_specs=[], out_shape=[], args=[],
                 exchanges=exchanges)[1]


def _shift_dn(x, d):
    rolled = pltpu.roll(x, d, 0)
    if x.shape[0] <= SUBLANE or d >= SUBLANE:
        row = lax.broadcasted_iota(jnp.int32, x.shape, 0)
        return jnp.where(row >= d, rolled, 0.0)
    row = lax.broadcasted_iota(jnp.int32, (SUBLANE, x.shape[1]), 0)
    return jnp.concatenate([jnp.where(row >= d, rolled[:SUBLANE], 0.0), rolled[SUBLANE:]], axis=0)


def _shift_up(x, d):
    n = x.shape[0]
    rolled = pltpu.roll(x, n - d, 0)
    if n <= SUBLANE or d >= SUBLANE:
        row = lax.broadcasted_iota(jnp.int32, x.shape, 0)
        return jnp.where(row < n - d, rolled, 0.0)
    row = lax.broadcasted_iota(jnp.int32, (SUBLANE, x.shape[1]), 0)
    return jnp.concatenate([rolled[:n - SUBLANE], jnp.where(row < SUBLANE - d, rolled[n - SUBLANE:], 0.0)], axis=0)


def _gelu(x):
    return 0.5 * x * (1.0 + jnp.tanh(GELU_K * (x + GELU_C * x * x * x)))


def _gelu_grad(x):
    t = jnp.tanh(GELU_K * (x + GELU_C * x * x * x))
    return 0.5 * (1.0 + t) + 0.5 * x * (1.0 - t * t) * (GELU_K * (1.0 + 3.0 * GELU_C * x * x))


def _sigmoid(x):
    return 0.5 + 0.5 * jnp.tanh(0.5 * x)


def _conv3(x, w_ref):
    return w_ref[0:1, :] * _shift_dn(x, 2) + w_ref[1:2, :] * _shift_dn(x, 1) + w_ref[2:3, :] * x


def _dot(a, b, dims):
    return lax.dot_general(a.astype(BF16), b.astype(BF16), (dims, ((), ())), preferred_element_type=F32)


_NN = ((1,), (0,))
_NT = ((1,), (1,))
_TN = ((0,), (0,))


def _tiles(name, m, n, default):
    tm, tn = TILES.get(name, default)
    return math.gcd(tm, m), math.gcd(tn, n)


def _mm(a, b, mode, *, name, out_dtype=F32, add=None, norm=None, norm_bwd=None, tm_cap=512, tn_cap=1536,
        exchanges=()):
    halves = None
    if mode == "tn":
        r, m = a.shape
        n = b.shape[-1] * (2 if b.ndim == 3 else 1)
        tm, tn = _tiles(name, m, n, (_pick_tile(m, 256, LANE), _pick_tile(n, tn_cap, LANE)))
        if b.ndim == 3:
            per_half = b.shape[-1] // tn
            b_spec = pl.BlockSpec((None, r, tn), lambda i, j: (j // per_half, 0, j % per_half))
        else:
            b_spec = pl.BlockSpec((r, tn), lambda i, j: (0, j))
        in_specs = [pl.BlockSpec((r, tm), lambda i, j: (0, i)), b_spec]
        dims = _TN
    elif mode == "nt" and a.ndim == 3:
        _, m, halves = a.shape
        n = b.shape[0]
        tm, tn = _tiles(name, m, n, (_pick_tile(m, tm_cap, SUBLANE), _pick_tile(n, tn_cap, LANE)))
        in_specs = [pl.BlockSpec((2, tm, halves), lambda i, j: (0, i, 0)),
                    pl.BlockSpec((tn, 2 * halves), lambda i, j: (j, 0))]
        dims = _NT
    elif mode == "nn":
        m, k = a.shape
        n = b.shape[1]
        tm, tn = _tiles(name, m, n, (_pick_tile(m, tm_cap, SUBLANE), _pick_tile(n, tn_cap, LANE)))
        in_specs = [pl.BlockSpec((tm, k), lambda i, j: (i, 0)), pl.BlockSpec((k, tn), lambda i, j: (0, j))]
        dims = _NN
    else:
        m, k = a.shape
        n = b.shape[0]
        tm, tn = _tiles(name, m, n, (_pick_tile(m, tm_cap, SUBLANE), _pick_tile(n, tn_cap, LANE)))
        in_specs = [pl.BlockSpec((tm, k), lambda i, j: (i, 0)), pl.BlockSpec((tn, k), lambda i, j: (j, 0))]
        dims = _NT
    assert m % tm == 0 and n % tn == 0, (name, m, n, tm, tn)
    args = [a, b]
    tile = pl.BlockSpec((tm, tn), lambda i, j: (i, j))
    if add is not None:
        in_specs.append(tile)
        args.append(add)
    out_specs, out_shape = tile, jax.ShapeDtypeStruct((m, n), out_dtype)
    if norm is not None:
        gains, layer = norm
        assert tn == n
        in_specs.append(pl.BlockSpec((None, 1, n), lambda i, j: (layer, 0, 0)))
        args.append(gains.reshape(gains.shape[0], 1, n))
        out_specs, out_shape = [tile, tile], [out_shape, jax.ShapeDtypeStruct((m, n), BF16)]
    if norm_bwd is not None:
        x_in, gains, layer, res = norm_bwd
        assert tn == n and add is None and norm is None
        vec = pl.BlockSpec((None, 1, n), lambda i, j: (layer, 0, 0))
        in_specs += [tile, vec, tile]
        args += [x_in, gains.reshape(gains.shape[0], 1, n), res]
        out_specs = [tile, tile, pl.BlockSpec((1, n), lambda i, j: (0, 0))]
        out_shape = [jax.ShapeDtypeStruct((m, n), F32), jax.ShapeDtypeStruct((m, n), BF16),
                     jax.ShapeDtypeStruct((1, n), F32)]

    def body(*refs):
        if halves is None:
            acc = _dot(refs[0][...], refs[1][...], dims)
        else:
            acc = (_dot(refs[0][0], refs[1][:, :halves], dims) + _dot(refs[0][1], refs[1][:, halves:], dims))
        if add is not None:
            acc = acc + refs[2][...]
        if norm_bwd is not None:
            x_ref, g_ref, res_ref, dx_ref, dxb_ref, dg_ref = refs[2:]

            @pl.when(pl.program_id(0) == 0)
            def _():
                dg_ref[...] = jnp.zeros_like(dg_ref)
            dx, xn = _rms_bwd_rows(acc, x_ref[...], g_ref[...])
            dx = dx + res_ref[...]
            dx_ref[...] = dx
            dxb_ref[...] = dx.astype(BF16)
            dg_ref[...] += jnp.sum(acc * xn, axis=0, keepdims=True)
        elif norm is None:
            refs[-1][...] = acc.astype(out_dtype)
        else:
            refs[-2][...] = acc.astype(out_dtype)
            r = lax.rsqrt(jnp.mean(acc * acc, axis=-1, keepdims=True) + EPS)
            refs[-1][...] = (acc * r * refs[-3][...]).astype(BF16)

    return _call(body, name=name, grid=(m // tm, n // tn), in_specs=in_specs, out_specs=out_specs,
                 out_shape=out_shape, args=args, parallel=norm_bwd is None, exchanges=exchanges)


def _rms_fwd(x, g, *, name):
    l, d = x.shape
    tl = _pick_tile(l, 512, SUBLANE)

    def body(x_ref, g_ref, h_ref):
        xv = x_ref[...]
        r = lax.rsqrt(jnp.mean(xv * xv, axis=-1, keepdims=True) + EPS)
        h_ref[...] = (xv * r * g_ref[...]).astype(BF16)

    return pl.pallas_call(
        body, name=name, grid=(l // tl,),
        in_specs=[pl.BlockSpec((tl, d), lambda i: (i, 0)), pl.BlockSpec((1, d), lambda i: (0, 0))],
        out_specs=pl.BlockSpec((tl, d), lambda i: (i, 0)),
        out_shape=jax.ShapeDtypeStruct((l, d), BF16), compiler_params=_PAR)(x, g)


def _rms_bwd_rows(dh, xv, g):
    r = lax.rsqrt(jnp.mean(xv * xv, axis=-1, keepdims=True) + EPS)
    a = dh * g
    m = jnp.mean(a * xv, axis=-1, keepdims=True)
    return r * a - xv * (r * r * r) * m, xv * r


def _loss_head(x, g, tgt, *, name):
    l, d = x.shape
    tl = _pick_tile(l, 512, SUBLANE)

    def body(x_ref, g_ref, t_ref, loss_ref, dx_ref, dxb_ref, dg_ref):
        @pl.when(pl.program_id(0) == 0)
        def _():
            dg_ref[...] = jnp.zeros_like(dg_ref)
            loss_ref[...] = jnp.zeros_like(loss_ref)
        xv = x_ref[...]
        gv = g_ref[...]
        r = lax.rsqrt(jnp.mean(xv * xv, axis=-1, keepdims=True) + EPS)
        err = xv * r * gv - t_ref[...]
        row_loss = jnp.sum(err * err, axis=-1, keepdims=True) * (0.5 / d)
        loss_ref[...] += jnp.sum(row_loss, axis=0, keepdims=True)
        dy = err * (1.0 / d)
        dx, xn = _rms_bwd_rows(dy, xv, gv)
        dx_ref[...] = dx
        dxb_ref[...] = dx.astype(BF16)
        dg_ref[...] += jnp.sum(dy * xn, axis=0, keepdims=True)

    row = pl.BlockSpec((tl, d), lambda i: (i, 0))
    vec = pl.BlockSpec((1, d), lambda i: (0, 0))
    one = pl.BlockSpec((1, 1), lambda i: (0, 0))
    return pl.pallas_call(
        body, name=name, grid=(l // tl,), in_specs=[row, vec, row], out_specs=[one, row, row, vec],
        out_shape=[jax.ShapeDtypeStruct((1, 1), F32), jax.ShapeDtypeStruct((l, d), F32),
                   jax.ShapeDtypeStruct((l, d), BF16), jax.ShapeDtypeStruct((1, d), F32)],
        compiler_params=_ARB)(x, g, tgt)


def _ffn_act(up, cw, cb, *, name, exchanges=()):
    l = up.shape[0]
    nb = D_FF // LANE

    def body(ug_ref, uv_ref, wg_ref, wv_ref, bg_ref, bv_ref, o_ref, gv_ref):
        gc = _conv3(ug_ref[...], wg_ref) + bg_ref[...]
        vc = _conv3(uv_ref[...], wv_ref) + bv_ref[...]
        gv_ref[0] = gc
        gv_ref[1] = vc
        o_ref[...] = (gc * _sigmoid(gc) * vc).astype(BF16)

    col = lambda off: pl.BlockSpec((l, LANE), lambda j: (0, j + off))
    w3 = lambda off: pl.BlockSpec((CONV_WIDTH, LANE), lambda j: (0, j + off))
    b1 = lambda off: pl.BlockSpec((1, LANE), lambda j: (0, j + off))
    return _call(body, name=name, grid=(nb,), in_specs=[col(0), col(nb), w3(0), w3(nb), b1(0), b1(nb)],
                 out_specs=[col(0), pl.BlockSpec((2, l, LANE), lambda j: (0, 0, j))],
                 out_shape=[jax.ShapeDtypeStruct((l, D_FF), BF16), jax.ShapeDtypeStruct((2, l, D_FF), F32)],
                 args=[up, up, cw, cw, cb, cb], exchanges=exchanges)


def _ffn_act_bwd(up, gv, dact, cw, *, name, exchanges=()):
    l = up.shape[0]
    nb = D_FF // LANE

    def half_bwd(k, dc, x, w_ref, dup_ref, dcw_ref, dcb_ref):
        d1, d2 = _shift_up(dc, 1), _shift_up(dc, 2)
        dcb_ref[k] = jnp.sum(dc, axis=0, keepdims=True)
        dcw_ref[k] = jnp.concatenate([jnp.sum(d2 * x, axis=0, keepdims=True),
                                      jnp.sum(d1 * x, axis=0, keepdims=True),
                                      jnp.sum(dc * x, axis=0, keepdims=True)], axis=0)
        dup_ref[k] = (w_ref[2:3, :] * dc + w_ref[1:2, :] * d1 + w_ref[0:1, :] * d2).astype(BF16)

    def body(ug_ref, uv_ref, gv_ref, da_ref, wg_ref, wv_ref, dup_ref, dcw_ref, dcb_ref):
        gc, vc, da = gv_ref[0], gv_ref[1], da_ref[...]
        sg = _sigmoid(gc)
        half_bwd(0, da * vc * (sg * (1.0 + gc * (1.0 - sg))), ug_ref[...], wg_ref, dup_ref, dcw_ref, dcb_ref)
        half_bwd(1, da * (gc * sg), uv_ref[...], wv_ref, dup_ref, dcw_ref, dcb_ref)

    col = lambda off: pl.BlockSpec((l, LANE), lambda j: (0, j + off))
    w3 = lambda off: pl.BlockSpec((CONV_WIDTH, LANE), lambda j: (0, j + off))
    both = lambda rows: pl.BlockSpec((2, rows, LANE), lambda j: (0, 0, j))
    res = _call(
        body, name=name, grid=(nb,),
        in_specs=[col(0), col(nb), both(l), col(0), w3(0), w3(nb)],
        out_specs=[both(l), both(CONV_WIDTH), both(1)],
        out_shape=[jax.ShapeDtypeStruct((2, l, D_FF), BF16), jax.ShapeDtypeStruct((2, CONV_WIDTH, D_FF), F32),
                   jax.ShapeDtypeStruct((2, 1, D_FF), F32)],
        args=[up, up, gv, dact, cw, cw], exchanges=exchanges)
    (dup, dcw, dcb), moved = res if exchanges else (res, None)
    outs = [dup, jnp.concatenate([dcw[0], dcw[1]], axis=1), jnp.concatenate([dcb[0], dcb[1]], axis=1)]
    return (outs, moved) if exchanges else outs


def _sconv_fwd(proj, cw, *, name):
    l = proj.shape[0]
    nb = D_HALF // LANE

    def body(xa_ref, ba_ref, ca_ref, w_ref, o_ref):
        o_ref[...] = (ba_ref[...] * _conv3(ca_ref[...] * xa_ref[...], w_ref)).astype(BF16)

    col = lambda off: pl.BlockSpec((l, LANE), lambda j: (0, j + off))
    return pl.pallas_call(
        body, name=name, grid=(nb,),
        in_specs=[col(0), col(nb), col(2 * nb), pl.BlockSpec((CONV_WIDTH, LANE), lambda j: (0, j))],
        out_specs=col(0), out_shape=jax.ShapeDtypeStruct((l, 2 * D_HALF), BF16),
        compiler_params=_PAR)(proj, proj, proj, cw)


def _sconv_bwd(proj, dcat, cw, *, name):
    l = proj.shape[0]
    nb = D_HALF // LANE

    def body(xa_ref, ba_ref, ca_ref, dy_ref, w_ref, dxa_ref, dba_ref, dca_ref, dw_ref):
        xa, ba, ca, dy = xa_ref[...], ba_ref[...], ca_ref[...], dy_ref[...]
        q = ca * xa
        dba_ref[...] = (dy * _conv3(q, w_ref)).astype(BF16)
        dconv = dy * ba
        d1, d2 = _shift_up(dconv, 1), _shift_up(dconv, 2)
        dw_ref[...] = jnp.concatenate([jnp.sum(d2 * q, axis=0, keepdims=True), jnp.sum(d1 * q, axis=0, keepdims=True),
                                       jnp.sum(dconv * q, axis=0, keepdims=True)], axis=0)
        dq = w_ref[2:3, :] * dconv + w_ref[1:2, :] * d1 + w_ref[0:1, :] * d2
        dxa_ref[...] = (dq * ca).astype(BF16)
        dca_ref[...] = (dq * xa).astype(BF16)

    col = lambda off: pl.BlockSpec((l, LANE), lambda j: (0, j + off))
    w3 = pl.BlockSpec((CONV_WIDTH, LANE), lambda j: (0, j))
    piece = jax.ShapeDtypeStruct((l, D_HALF), BF16)
    return pl.pallas_call(
        body, name=name, grid=(nb,),
        in_specs=[col(0), col(nb), col(2 * nb), col(0), w3],
        out_specs=[col(0), col(0), col(0), w3],
        out_shape=[piece, piece, piece, jax.ShapeDtypeStruct((CONV_WIDTH, D_HALF), F32)],
        compiler_params=_PAR)(proj, proj, proj, dcat, cw)


def _s5_prep(log_step, a_re, a_im, b_re, b_im):
    step = jnp.exp(log_step)[:, None]
    mag = jnp.exp(a_re * step)
    lr = mag * jnp.cos(a_im * step)
    li = mag * jnp.sin(a_im * step)
    nr = lr - 1.0
    den = a_re * a_re + a_im * a_im
    qr = (nr * a_re + li * a_im) / den
    qi = (li * a_re - nr * a_im) / den
    br = qr[..., None] * b_re - qi[..., None] * b_im
    bi = qr[..., None] * b_im + qi[..., None] * b_re
    return lr, li, br, bi


def _block_diag(m):
    nb, ng, r, c = m.shape
    eye = jnp.eye(ng, dtype=m.dtype)
    return jnp.einsum("bgrc,gh->bgrhc", m, eye).reshape(nb, ng * r, ng * c)


def _block_diag_extract(w, r, c):
    nb = w.shape[0]
    ng = w.shape[1] // r
    w5 = w.reshape(nb, ng, r, ng, c)
    return jnp.einsum("bgrhc,gh->bgrc", w5, jnp.eye(ng, dtype=w.dtype))


def _s5_mats(br, bi, c_re, c_im):
    g8 = N_SSM_GROUPS // S5_LANE_BLOCKS
    to_blk = lambda m: m.reshape(S5_LANE_BLOCKS, g8, m.shape[1], m.shape[2])
    wb = jnp.concatenate([_block_diag(to_blk(jnp.swapaxes(br, 1, 2))),
                          _block_diag(to_blk(jnp.swapaxes(bi, 1, 2)))], axis=2)
    wc = jnp.concatenate([_block_diag(to_blk(jnp.swapaxes(c_re, 1, 2))),
                          _block_diag(to_blk(jnp.swapaxes(-c_im, 1, 2)))], axis=1)
    return wb, wc


def _s5_mats_bwd(dwb, dwc):
    g, p, h = N_SSM_GROUPS, SSM_STATE, SSM_GROUP
    half = S5_STATE_LANES
    dbr = jnp.swapaxes(_block_diag_extract(dwb[:, :, :half], h, p).reshape(g, h, p), 1, 2)
    dbi = jnp.swapaxes(_block_diag_extract(dwb[:, :, half:], h, p).reshape(g, h, p), 1, 2)
    dcr = jnp.swapaxes(_block_diag_extract(dwc[:, :half, :], p, h).reshape(g, p, h), 1, 2)
    dci = -jnp.swapaxes(_block_diag_extract(dwc[:, half:, :], p, h).reshape(g, p, h), 1, 2)
    return dbr, dbi, dcr, dci


def _s5_scan_consts(log_step, a_re, a_im, reverse):
    step = jnp.exp(log_step)[:, None]
    xr = (a_re * step).reshape(S5_LANE_BLOCKS, 1, S5_STATE_LANES)
    xi = (a_im * step).reshape(S5_LANE_BLOCKS, 1, S5_STATE_LANES)
    if reverse:
        xi = -xi
    row = jnp.arange(SUBLANE, dtype=F32).reshape(1, SUBLANE, 1)

    def power(n):
        mag = jnp.exp(n * xr)
        return jnp.concatenate([mag * jnp.cos(n * xi), mag * jnp.sin(n * xi)], axis=-1)

    kinds = []
    for d in (1, 2, 4):
        keep = (row <= SUBLANE - 1 - d) if reverse else (row >= d)
        kinds.append(jnp.where(keep, power(jnp.full_like(row, float(d))), 0.0))
    kinds.append(power((SUBLANE - row) if reverse else (row + 1.0)))
    return jnp.stack(kinds, axis=1)


def _scan_rows(s_ref, sc_ref, carry_ref, n_rows, reverse):
    n_grp = n_rows // SUBLANE
    n_col = S5_STATE_LANES // LANE
    half = S5_STATE_LANES

    def step(i, carry):
        grp = (n_grp - 1 - i) if reverse else i
        r0 = pl.multiple_of(grp * SUBLANE, SUBLANE)
        out = []
        for cb in range(n_col):
            lo, hi = cb * LANE, half + cb * LANE
            re = s_ref[pl.ds(r0, SUBLANE), lo:lo + LANE]
            im = s_ref[pl.ds(r0, SUBLANE), hi:hi + LANE]
            for k, d in enumerate((1, 2, 4)):
                sh = (SUBLANE - d) if reverse else d
                rr, ri = pltpu.roll(re, sh, 0), pltpu.roll(im, sh, 0)
                ar, ai = sc_ref[k, :, lo:lo + LANE], sc_ref[k, :, hi:hi + LANE]
                re, im = re + (ar * rr - ai * ri), im + (ar * ri + ai * rr)
            pr, pi = sc_ref[3, :, lo:lo + LANE], sc_ref[3, :, hi:hi + LANE]
            cr, ci = carry[2 * cb], carry[2 * cb + 1]
            re, im = re + (pr * cr - pi * ci), im + (pr * ci + pi * cr)
            s_ref[pl.ds(r0, SUBLANE), lo:lo + LANE] = re
            s_ref[pl.ds(r0, SUBLANE), hi:hi + LANE] = im
            edge = 0 if reverse else SUBLANE - 1
            out.append(jnp.broadcast_to(re[edge:edge + 1, :], (SUBLANE, LANE)))
            out.append(jnp.broadcast_to(im[edge:edge + 1, :], (SUBLANE, LANE)))
        return tuple(out)

    init = []
    for cb in range(n_col):
        init.append(carry_ref[:, cb * LANE:(cb + 1) * LANE])
        init.append(carry_ref[:, half + cb * LANE:half + (cb + 1) * LANE])
    fin = lax.fori_loop(0, n_grp, step, tuple(init))
    for cb in range(n_col):
        carry_ref[:, cb * LANE:(cb + 1) * LANE] = fin[2 * cb]
        carry_ref[:, half + cb * LANE:half + (cb + 1) * LANE] = fin[2 * cb + 1]


def _s5_fwd(proj, wb, wc, d_skip, sc, *, name, exchanges=()):
    l = proj.shape[0]
    tt = _pick_tile(l, S5_TIME_CHUNK, SUBLANE)
    u_off = (proj.shape[1] - D_HALF) // LANE
    w2 = 2 * S5_STATE_LANES

    def body(u_ref, wb_ref, wc_ref, d_ref, sc_ref, s_ref, y_ref, carry_ref):
        @pl.when(pl.program_id(1) == 0)
        def _():
            carry_ref[...] = jnp.zeros_like(carry_ref)
        u = u_ref[...]
        s_ref[...] = _dot(u, wb_ref[0], _NN)
        _scan_rows(s_ref, sc_ref.at[0], carry_ref, tt, False)
        y_ref[...] = _dot(s_ref[...], wc_ref[0], _NN) + d_ref[...] * u

    return _call(
        body, name=name, grid=(S5_LANE_BLOCKS, l // tt),
        in_specs=[pl.BlockSpec((tt, LANE), lambda b, t: (t, b + u_off)),
                  pl.BlockSpec((1, LANE, w2), lambda b, t: (b, 0, 0)),
                  pl.BlockSpec((1, w2, LANE), lambda b, t: (b, 0, 0)),
                  pl.BlockSpec((1, LANE), lambda b, t: (0, b)),
                  pl.BlockSpec((1, 4, SUBLANE, w2), lambda b, t: (b, 0, 0, 0))],
        out_specs=[pl.BlockSpec((tt, w2), lambda b, t: (t, b)), pl.BlockSpec((tt, LANE), lambda b, t: (t, b))],
        out_shape=[jax.ShapeDtypeStruct((l, S5_LANE_BLOCKS * w2), F32), jax.ShapeDtypeStruct((l, D_HALF), F32)],
        scratch_shapes=[pltpu.VMEM((SUBLANE, w2), F32)],
        args=[proj, wb, wc, d_skip, sc], parallel=False, exchanges=exchanges)


def _s5_bwd(proj, dy, states, wb, wc, d_skip, sc_rev, *, name, exchanges=()):
    l = proj.shape[0]
    tt = _pick_tile(l, S5_TIME_CHUNK, SUBLANE)
    nt = l // tt
    u_off = (proj.shape[1] - D_HALF) // LANE
    w2 = 2 * S5_STATE_LANES
    half = S5_STATE_LANES
    grp_per_chunk = tt // SUBLANE

    def body(u_ref, dy_ref, s_ref, halo_ref, wb_ref, wc_ref, d_ref, sc_ref,
             du_ref, dwb_ref, dwc_ref, dlam_ref, dd_ref, g_scr, carry_ref):
        t = pl.program_id(1)

        @pl.when(t == 0)
        def _():
            carry_ref[...] = jnp.zeros_like(carry_ref)
            dwb_ref[...] = jnp.zeros_like(dwb_ref)
            dwc_ref[...] = jnp.zeros_like(dwc_ref)
            dlam_ref[...] = jnp.zeros_like(dlam_ref)
            dd_ref[...] = jnp.zeros_like(dd_ref)

        u = u_ref[...]
        dyv = dy_ref[...]
        g_scr[...] = _dot(dyv, wc_ref[0], _NT)
        _scan_rows(g_scr, sc_ref.at[0], carry_ref, tt, True)
        gv = g_scr[...]
        du_ref[...] = (_dot(gv, wb_ref[0], _NT) + d_ref[...] * dyv).astype(BF16)
        dwb_ref[0] += _dot(u, gv, _TN)
        sv = s_ref[...]
        dwc_ref[0] += _dot(sv, dyv, _TN)
        dd_ref[...] += jnp.sum(dyv * u, axis=0, keepdims=True)
        first_chunk = t == nt - 1
        halo = jnp.where(first_chunk, 0.0, halo_ref[SUBLANE - 1:SUBLANE, :])
        row = lax.broadcasted_iota(jnp.int32, sv.shape, 0)
        sp = jnp.where(row == 0, jnp.broadcast_to(halo, sv.shape), pltpu.roll(sv, 1, 0))
        gr, gi = gv[:, :half], gv[:, half:]
        sr, si = sp[:, :half], sp[:, half:]
        dlr = jnp.sum(gr * sr + gi * si, axis=0, keepdims=True)
        dli = jnp.sum(gi * sr - gr * si, axis=0, keepdims=True)
        dlam_ref[0] += jnp.concatenate([dlr, dli], axis=1)

    rev = lambda t: nt - 1 - t
    return _call(
        body, name=name, grid=(S5_LANE_BLOCKS, nt),
        in_specs=[pl.BlockSpec((tt, LANE), lambda b, t: (rev(t), b + u_off)),
                  pl.BlockSpec((tt, LANE), lambda b, t: (rev(t), b)),
                  pl.BlockSpec((tt, w2), lambda b, t: (rev(t), b)),
                  pl.BlockSpec((SUBLANE, w2), lambda b, t: (jnp.maximum(rev(t) * grp_per_chunk - 1, 0), b)),
                  pl.BlockSpec((1, LANE, w2), lambda b, t: (b, 0, 0)),
                  pl.BlockSpec((1, w2, LANE), lambda b, t: (b, 0, 0)),
                  pl.BlockSpec((1, LANE), lambda b, t: (0, b)),
                  pl.BlockSpec((1, 4, SUBLANE, w2), lambda b, t: (b, 0, 0, 0))],
        out_specs=[pl.BlockSpec((tt, LANE), lambda b, t: (rev(t), b)),
                   pl.BlockSpec((1, LANE, w2), lambda b, t: (b, 0, 0)),
                   pl.BlockSpec((1, w2, LANE), lambda b, t: (b, 0, 0)),
                   pl.BlockSpec((1, 1, w2), lambda b, t: (b, 0, 0)),
                   pl.BlockSpec((1, LANE), lambda b, t: (0, b))],
        out_shape=[jax.ShapeDtypeStruct((l, D_HALF), BF16),
                   jax.ShapeDtypeStruct((S5_LANE_BLOCKS, LANE, w2), F32),
                   jax.ShapeDtypeStruct((S5_LANE_BLOCKS, w2, LANE), F32),
                   jax.ShapeDtypeStruct((S5_LANE_BLOCKS, 1, w2), F32),
                   jax.ShapeDtypeStruct((1, D_HALF), F32)],
        scratch_shapes=[pltpu.VMEM((tt, w2), F32), pltpu.VMEM((SUBLANE, w2), F32)],
        args=[proj, dy, states, states, wb, wc, d_skip, sc_rev], parallel=False, exchanges=exchanges)


def _glu_fwd(ypre, wg, bg, cat, *, name):
    l, d = ypre.shape
    tl = _pick_tile(l, 512, SUBLANE)

    def body(y_ref, w_ref, b_ref, cat_ref, o_ref):
        yg = _gelu(y_ref[...])
        o_ref[...] = (yg * _sigmoid(_dot(yg, w_ref[...], _NN) + b_ref[...])).astype(BF16)

    row = pl.BlockSpec((tl, d), lambda i: (i, 0))
    return pl.pallas_call(
        body, name=name, grid=(l // tl,),
        in_specs=[row, pl.BlockSpec((d, d), lambda i: (0, 0)), pl.BlockSpec((1, d), lambda i: (0, 0)), _ANY],
        out_specs=pl.BlockSpec((tl, d), lambda i: (i, 1)), out_shape=jax.ShapeDtypeStruct(cat.shape, cat.dtype),
        input_output_aliases={3: 0}, compiler_params=_PAR)(ypre, wg, bg, cat)


def _glu_bwd(dcat, ypre, wg, bg, *, name):
    l, d = ypre.shape
    tl = _pick_tile(l, 512, SUBLANE)

    def body(dy_ref, y_ref, w_ref, b_ref, dyp_ref, dw_ref, db_ref):
        @pl.when(pl.program_id(0) == 0)
        def _():
            dw_ref[...] = jnp.zeros_like(dw_ref)
            db_ref[...] = jnp.zeros_like(db_ref)
        yp = y_ref[...]
        dyb = dy_ref[...]
        yg = _gelu(yp)
        sg = _sigmoid(_dot(yg, w_ref[...], _NN) + b_ref[...])
        dz = dyb * yg * sg * (1.0 - sg)
        dyg = dyb * sg + _dot(dz, w_ref[...], _NT)
        dyp_ref[...] = dyg * _gelu_grad(yp)
        dw_ref[...] += _dot(yg, dz, _TN)
        db_ref[...] += jnp.sum(dz, axis=0, keepdims=True)

    row = pl.BlockSpec((tl, d), lambda i: (i, 0))
    mat = pl.BlockSpec((d, d), lambda i: (0, 0))
    vec = pl.BlockSpec((1, d), lambda i: (0, 0))
    return pl.pallas_call(
        body, name=name, grid=(l // tl,),
        in_specs=[pl.BlockSpec((tl, d), lambda i: (i, 1)), row, mat, vec], out_specs=[row, mat, vec],
        out_shape=[jax.ShapeDtypeStruct((l, d), F32), jax.ShapeDtypeStruct((d, d), F32),
                   jax.ShapeDtypeStruct((1, d), F32)],
        compiler_params=_ARB)(dcat, ypre, wg, bg)


def _window_sum(x, w, trailing):
    s, d = x, 1
    while d < w:
        s = s + (_shift_dn(s, d) if trailing else _shift_up(s, d))
        d *= 2
    return s


def _window_count(shape, w):
    row = lax.broadcasted_iota(jnp.int32, shape, 0)
    return jnp.minimum(row + 1, w).astype(F32)


def _pool_fwd(proj, pw, scale, *, name):
    l = proj.shape[0]
    ng = len(POOL_WINDOWS)

    def body(z_ref, w_ref, sc_ref, o_ref):
        z = z_ref[...]
        for k, w in enumerate(POOL_WINDOWS):
            @pl.when(pl.program_id(0) == k)
            def _():
                pooled = _window_sum(z, w, True) / _window_count(z.shape, w) - z
                o_ref[...] = (_dot(pooled, w_ref[0], _NN) * sc_ref[...]).astype(BF16)

    col = pl.BlockSpec((l, LANE), lambda g: (0, g))
    return pl.pallas_call(
        body, name=name, grid=(ng,),
        in_specs=[col, pl.BlockSpec((1, LANE, LANE), lambda g: (g, 0, 0)), pl.BlockSpec((1, LANE), lambda g: (0, g))],
        out_specs=col, out_shape=jax.ShapeDtypeStruct((l, 2 * D_HALF), BF16), compiler_params=_PAR)(proj, pw, scale)


def _pool_bwd(proj, dcat, pw, scale, *, name):
    l = proj.shape[0]
    ng = len(POOL_WINDOWS)

    def body(z_ref, dy_ref, w_ref, sc_ref, dz_ref, dw_ref, dsc_ref):
        z = z_ref[...]
        dy = dy_ref[...]
        for k, w in enumerate(POOL_WINDOWS):
            @pl.when(pl.program_id(0) == k)
            def _():
                cnt = _window_count(z.shape, w)
                pooled = _window_sum(z, w, True) / cnt - z
                ypre = _dot(pooled, w_ref[0], _NN)
                dsc_ref[...] = jnp.sum(dy * ypre, axis=0, keepdims=True)
                dyp = dy * sc_ref[...]
                dw_ref[0] = _dot(pooled, dyp, _TN)
                dpool = _dot(dyp, w_ref[0], _NT)
                dz_ref[...] = (_window_sum(dpool / cnt, w, False) - dpool).astype(BF16)

    col = pl.BlockSpec((l, LANE), lambda g: (0, g))
    mat = pl.BlockSpec((1, LANE, LANE), lambda g: (g, 0, 0))
    vec = pl.BlockSpec((1, LANE), lambda g: (0, g))
    return pl.pallas_call(
        body, name=name, grid=(ng,), in_specs=[col, col, mat, vec], out_specs=[col, mat, vec],
        out_shape=[jax.ShapeDtypeStruct((l, D_HALF), BF16), jax.ShapeDtypeStruct((ng, LANE, LANE), F32),
                   jax.ShapeDtypeStruct((1, D_HALF), F32)],
        compiler_params=_PAR)(proj, dcat, pw, scale)


def _tril_mask():
    r = lax.broadcasted_iota(jnp.int32, (CHUNK, CHUNK), 0)
    c = lax.broadcasted_iota(jnp.int32, (CHUNK, CHUNK), 1)
    return r >= c


def _sgu_fwd(proj, ng, sw, sb_t, cat, *, name):
    l = proj.shape[0]
    tl = _pick_tile(l, 512, CHUNK)

    def body(su_ref, sv_ref, g_ref, w_ref, b_ref, cat_ref, o_ref):
        su = _gelu(su_ref[...])
        sv = _gelu(sv_ref[...])
        r = lax.rsqrt(jnp.mean(sv * sv, axis=-1, keepdims=True) + EPS)
        v = sv * r * g_ref[...]
        mask = _tril_mask()
        for h in range(SGU_HEADS):
            wm = jnp.where(mask, w_ref[h], 0.0)
            cs = slice(h * LANE, (h + 1) * LANE)
            for n in range(tl // CHUNK):
                rs = slice(n * CHUNK, (n + 1) * CHUNK)
                mixed = _dot(wm, v[rs, cs], _NN) + b_ref[:, h:h + 1]
                o_ref[rs, cs] = (su[rs, cs] * mixed).astype(BF16)

    blk = lambda c: pl.BlockSpec((tl, D_HALF), lambda i: (i, c))
    return pl.pallas_call(
        body, name=name, grid=(l // tl,),
        in_specs=[blk(1), blk(2), pl.BlockSpec((1, D_HALF), lambda i: (0, 0)),
                  pl.BlockSpec((SGU_HEADS, CHUNK, CHUNK), lambda i: (0, 0, 0)),
                  pl.BlockSpec((CHUNK, SGU_HEADS), lambda i: (0, 0)), _ANY],
        out_specs=blk(1), out_shape=jax.ShapeDtypeStruct(cat.shape, cat.dtype), input_output_aliases={5: 0},
        compiler_params=_PAR)(proj, proj, ng, sw, sb_t, cat)


def _sgu_bwd(proj, dcat, ng, sw, sb_t, *, name):
    l = proj.shape[0]
    tl = _pick_tile(l, 512, CHUNK)

    def body(su_ref, sv_ref, dy_ref, g_ref, w_ref, b_ref, dsu_ref, dsv_ref, dw_ref, dbm_ref, dng_ref, dv_scr):
        @pl.when(pl.program_id(0) == 0)
        def _():
            dw_ref[...] = jnp.zeros_like(dw_ref)
            dbm_ref[...] = jnp.zeros_like(dbm_ref)
            dng_ref[...] = jnp.zeros_like(dng_ref)
        su_pre = su_ref[...]
        sv_pre = sv_ref[...]
        su = _gelu(su_pre)
        sv = _gelu(sv_pre)
        gsu = _gelu_grad(su_pre)
        gv = g_ref[...]
        r = lax.rsqrt(jnp.mean(sv * sv, axis=-1, keepdims=True) + EPS)
        v = sv * r * gv
        dy = dy_ref[...]
        mask = _tril_mask()
        for h in range(SGU_HEADS):
            wm = jnp.where(mask, w_ref[h], 0.0)
            cs = slice(h * LANE, (h + 1) * LANE)
            dw_acc = jnp.zeros((CHUNK, CHUNK), F32)
            db_acc = jnp.zeros((CHUNK, LANE), F32)
            for n in range(tl // CHUNK):
                rs = slice(n * CHUNK, (n + 1) * CHUNK)
                vb = v[rs, cs]
                mixed = _dot(wm, vb, _NN) + b_ref[:, h:h + 1]
                dyb = dy[rs, cs]
                dsu_ref[rs, cs] = (dyb * mixed * gsu[rs, cs]).astype(BF16)
                dmix = dyb * su[rs, cs]
                db_acc = db_acc + dmix
                dw_acc = dw_acc + _dot(dmix, vb, _NT)
                dv_scr[rs, cs] = _dot(wm, dmix, _TN)
            dw_ref[h] += jnp.where(mask, dw_acc, 0.0)
            dbm_ref[h] += db_acc
        dv = dv_scr[...]
        a = dv * gv
        m = jnp.mean(a * sv, axis=-1, keepdims=True)
        dsv = r * a - sv * (r * r * r) * m
        dng_ref[...] += jnp.sum(dv * sv * r, axis=0, keepdims=True)
        dsv_ref[...] = (dsv * _gelu_grad(sv_pre)).astype(BF16)

    blk = lambda c: pl.BlockSpec((tl, D_HALF), lambda i: (i, c))
    mats = pl.BlockSpec((SGU_HEADS, CHUNK, CHUNK), lambda i: (0, 0, 0))
    vec = pl.BlockSpec((1, D_HALF), lambda i: (0, 0))
    piece = jax.ShapeDtypeStruct((l, D_HALF), BF16)
    mshape = jax.ShapeDtypeStruct((SGU_HEADS, CHUNK, CHUNK), F32)
    return pl.pallas_call(
        body, name=name, grid=(l // tl,),
        in_specs=[blk(1), blk(2), blk(1), vec, mats, pl.BlockSpec((CHUNK, SGU_HEADS), lambda i: (0, 0))],
        out_specs=[blk(0), blk(0), mats, mats, vec],
        out_shape=[piece, piece, mshape, mshape, jax.ShapeDtypeStruct((1, D_HALF), F32)],
        scratch_shapes=[pltpu.VMEM((tl, D_HALF), F32)],
        compiler_params=_ARB)(proj, proj, dcat, ng, sw, sb_t)


def _s5_params(w):
    prep_args = (w["ssm_log_step"], w["ssm_a_re"], w["ssm_a_im"], w["ssm_b_re"], w["ssm_b_im"])
    (lr, li, br, bi), prep_vjp = jax.vjp(jax.vmap(_s5_prep), *prep_args)
    wb, wc = jax.vmap(_s5_mats)(br, bi, w["ssm_c_re"], w["ssm_c_im"])
    consts = lambda reverse: jax.vmap(functools.partial(_s5_scan_consts, reverse=reverse))(*prep_args[:3])
    return dict(wb=wb.astype(BF16), wc=wc.astype(BF16), d=w["ssm_d"][:, None, :], sc=consts(False),
                sc_rev=consts(True), prep_vjp=prep_vjp)


def _s5_param_grads(s5, dwb, dwc, dlam, dd):
    dbr, dbi, dcr, dci = jax.vmap(_s5_mats_bwd)(dwb, dwc)
    n = dlam.shape[0]
    dlr = dlam[:, :, 0, :S5_STATE_LANES].reshape(n, N_SSM_GROUPS, SSM_STATE)
    dli = dlam[:, :, 0, S5_STATE_LANES:].reshape(n, N_SSM_GROUPS, SSM_STATE)
    dls, dar, dai, db_re, db_im = s5["prep_vjp"]((dlr, dli, dbr, dbi))
    return dict(ssm_log_step=dls, ssm_a_re=dar, ssm_a_im=dai, ssm_b_re=db_re, ssm_b_im=db_im, ssm_c_re=dcr,
                ssm_c_im=dci, ssm_d=dd[:, 0, :])


TILES = {
    "mm_up": (4096, 512), "mm_up_dw": (1024, 1408), "mm_down_dx": (1024, 2816),
    "mm_down": (512, 1024), "mm_down_dw": (256, 1024), "mm_even_in": (2048, 1024), "mm_odd_in": (2048, 768),
    "mm_mix_out": (1024, 1024), "mm_mix_out_dx": (2048, 1024), "mm_mix_out_dw": (1024, 512),
    "mm_even_in_dw": (1024, 512), "mm_odd_in_dw": (1024, 512),
}


def _layer_weights(i):
    j = i // 2
    mixer = [("even_w_in", j), ("even_w_out", j), ("ssm_glu_w", j)] if i % 2 == 0 else [("odd_w_in", j),
                                                                                         ("odd_w_out", j)]
    return dict(w_in=mixer[0], w_out=mixer[1], glu=mixer[2:], up=("ffn_w_up", i), down=("ffn_w_down", i))


class _LocalWeights:
    def __init__(self, w):
        self.w, self.grads = w, {}

    def carried_by(self, stage, i):
        return []

    def delivered(self, stage, i, outs):
        pass

    def weight(self, key):
        return self.w[key[0]][key[1]]

    def grad(self, key, dw):
        self.grads[key] = dw


class _ShardedWeights:
    def __init__(self, shards):
        self.shards = shards
        self.full, self.pending, self.scattered = {}, {}, {}

    def start(self, others):
        keys = [_layer_weights(0)["w_in"]]
        outs = _exchange_only(others + [self._gather(k) for k in keys], name="ag_first")
        self._take(keys, outs[len(others):])
        return outs[:len(others)]

    def _gather(self, key):
        return _Gather(self.shards[key[0]][key[1]])

    def _take(self, keys, outs):
        for key, got in zip(keys, outs):
            if BIG[key[0]] == 2:
                self.full[key] = jnp.swapaxes(got, 0, 1).reshape(got.shape[1], -1)
            else:
                self.full[key] = got.reshape(-1, got.shape[2])

    def _plan(self, stage, i):
        cur = _layer_weights(i)
        nxt = _layer_weights(i + 1) if i + 1 < DEPTH else None
        has_scan = lambda k: k % 2 == 0
        none = ([], [])
        return {
            "mm_in": ([cur["w_out"], *cur["glu"]], []) if i == 0 else none,
            "s5_fwd": ([cur["up"]], []),
            "mm_up": ([cur["down"]] + ([nxt["w_in"], nxt["w_out"], *nxt["glu"]] if nxt else []), []),
            "ffn_act": ([nxt["up"]], []) if nxt and not has_scan(i + 1) else none,
            "ffn_act_bwd": ([], [cur["down"]] + ([] if not nxt else [nxt["w_in"]] if not has_scan(i + 1)
                                                 else [nxt["w_out"], *nxt["glu"]])),
            "mm_up_dx": none if not nxt else ([], [(*nxt["up"], 0)]) if not has_scan(i + 1) else ([], [nxt["w_in"]]),
            "mm_up_dw": ([], [(*nxt["up"], 1)]) if nxt and not has_scan(i + 1) else none,
            "s5_bwd": ([], [cur["up"]] + ([nxt["w_out"]] if nxt else [])),
            "mm_in_dw": ([], [cur["w_out"], *cur["glu"]]) if i == 0 else none,
            "mm_in_dx": ([], [cur["w_in"]]) if i == 0 else none,
        }[stage]

    def _scatter(self, key):
        name, layer = key[:2]
        layers, rows, cols = self.shards[name].shape
        if len(key) == 3:
            src = self.pending[key[:2]] if key[2] == 0 else self.pending.pop(key[:2])
            src = src[:, key[2] * (rows // 2):(key[2] + 1) * (rows // 2)]
            row0 = layer * rows + key[2] * (rows // 2)
        else:
            src, row0 = self.pending.pop(key), layer * rows
        if name not in self.scattered:
            self.scattered[name] = lax.empty((N_DEV, layers * rows, cols), src.dtype)
        return _Scatter(src, into=self.scattered[name], row0=row0)

    def carried_by(self, stage, i):
        gather, scatter = self._plan(stage, i)
        return [self._gather(k) for k in gather] + [self._scatter(k) for k in scatter]

    def delivered(self, stage, i, outs):
        gather, scatter = self._plan(stage, i)
        self._take(gather, outs[:len(gather)])
        for key, buf in zip(scatter, outs[len(gather):]):
            self.scattered[key[0]] = buf

    def weight(self, key):
        return self.full[key]

    def grad(self, key, dw):
        self.pending[key] = _to_dest_major(dw, BIG[key[0]] - 1).astype(BF16)

    def finish(self, carrier, others):
        keys = list(self.pending)
        res, outs = carrier(others + [self._scatter(k) for k in keys])
        for (name, _), buf in zip(keys, outs[len(others):]):
            self.scattered[name] = buf
        return res, outs[:len(others)]


def _device_step(x, tgt, w, comm):
    saved = []
    s5 = _s5_params(w)
    h = _rms_fwd(x, w["norm_mix_g"][0:1], name="rms_fwd")
    for i in range(DEPTH):
        j = i // 2
        lw = _layer_weights(i)
        if i % 2 == 0:
            proj = _carry(comm, "mm_in", i, _mm, h, comm.weight(lw["w_in"]), "nn", name="mm_even_in")
            ya = _sconv_fwd(proj, w["even_conv_w"][j], name="sconv_fwd")
            states, ypre = _carry(comm, "s5_fwd", i, _s5_fwd, proj, s5["wb"][j], s5["wc"][j], s5["d"][j],
                                  s5["sc"][j], name="s5_fwd")
            cat = _glu_fwd(ypre, comm.weight(lw["glu"][0]), w["ssm_glu_b"][j][None, :], ya, name="glu_fwd")
            mix = (states, ypre)
        else:
            proj = _carry(comm, "mm_in", i, _mm, h, comm.weight(lw["w_in"]), "nn", name="mm_odd_in")
            yc = _pool_fwd(proj, w["pool_w"][j], w["pool_scale"][j][None, :], name="pool_fwd")
            sb_t = jnp.transpose(w["sgu_b"][j])
            cat = _sgu_fwd(proj, w["sgu_norm_g"][j][None, :], w["sgu_w"][j], sb_t, yc, name="sgu_fwd")
            mix = (sb_t,)
        x1, h2 = _mm(cat, comm.weight(lw["w_out"]), "nn", add=x, norm=(w["norm_ffn_g"], i), name="mm_mix_out")
        up = _carry(comm, "mm_up", i, _mm, h2, comm.weight(lw["up"]), "nn", name="mm_up")
        act, gv = _carry(comm, "ffn_act", i, _ffn_act, up, w["ffn_conv_w"][i], w["ffn_conv_b"][i:i + 1],
                         name="ffn_act")
        if i + 1 < DEPTH:
            x2, h_next = _mm(act, comm.weight(lw["down"]), "nn", add=x1, norm=(w["norm_mix_g"], i + 1),
                             name="mm_down")
        else:
            x2, h_next = _mm(act, comm.weight(lw["down"]), "nn", add=x1, name="mm_down_last"), None
        saved.append((x, h, proj, cat, x1, h2, up, gv, act, mix))
        x, h = x2, h_next

    loss, dx, dxb, dgf = _loss_head(x, w["norm_final_g"][None, :], tgt, name="loss_head")
    per_layer = {}
    s5_grads = []

    def put(name, idx, val):
        per_layer.setdefault(name, {})[idx] = val

    for i in reversed(range(DEPTH)):
        j = i // 2
        lw = _layer_weights(i)
        x0, h, proj, cat, x1, h2, up, gv, act, mix = saved[i]
        dact = _mm(dxb, comm.weight(lw["down"]), "nt", name="mm_down_dx")
        comm.grad(lw["down"], _mm(act, dxb, "tn", out_dtype=BF16, name="mm_down_dw"))
        dup, dcw, dcb = _carry(comm, "ffn_act_bwd", i, _ffn_act_bwd, up, gv, dact, w["ffn_conv_w"][i],
                               name="ffn_act_bwd")
        put("ffn_conv_w", i, dcw)
        put("ffn_conv_b", i, dcb[0])
        dx1, dx1b, dg2 = _carry(comm, "mm_up_dx", i, _mm, dup, comm.weight(lw["up"]), "nt", tm_cap=512, tn_cap=D_MODEL,
                                norm_bwd=(x1, w["norm_ffn_g"], i, dx), name="mm_up_dx")
        comm.grad(lw["up"], _carry(comm, "mm_up_dw", i, _mm, h2, dup, "tn", out_dtype=BF16,
                                   name="mm_up_dw"))
        put("norm_ffn_g", i, dg2[0])
        dcat = _mm(dx1b, comm.weight(lw["w_out"]), "nt", name="mm_mix_out_dx")
        comm.grad(lw["w_out"], _mm(cat, dx1b, "tn", out_dtype=BF16, name="mm_mix_out_dw"))
        if i % 2 == 0:
            states, ypre = mix
            dxa, dba, dca, dcw_a = _sconv_bwd(proj, dcat, w["even_conv_w"][j], name="sconv_bwd")
            put("even_conv_w", j, dcw_a)
            dypre, dwg, dbg = _glu_bwd(dcat, ypre, comm.weight(lw["glu"][0]), w["ssm_glu_b"][j][None, :],
                                       name="glu_bwd")
            comm.grad(lw["glu"][0], dwg)
            put("ssm_glu_b", j, dbg[0])
            du, dwb, dwc, dlam, dd = _carry(comm, "s5_bwd", i, _s5_bwd, proj, dypre, states, s5["wb"][j], s5["wc"][j],
                                            s5["d"][j], s5["sc_rev"][j], name="s5_bwd")
            s5_grads.insert(0, (dwb, dwc, dlam, dd))
            dproj = jnp.concatenate([dxa, dba, dca, du], axis=1)
            in_name = "mm_even_in"
        else:
            (sb_t,) = mix
            dz, dpw, dps = _pool_bwd(proj, dcat, w["pool_w"][j], w["pool_scale"][j][None, :], name="pool_bwd")
            put("pool_w", j, dpw)
            put("pool_scale", j, dps[0])
            dsu, dsv, dsw, dbm, dng = _sgu_bwd(proj, dcat, w["sgu_norm_g"][j][None, :], w["sgu_w"][j], sb_t,
                                               name="sgu_bwd")
            put("sgu_w", j, dsw)
            put("sgu_b", j, jnp.sum(dbm, axis=-1))
            put("sgu_norm_g", j, dng[0])
            dproj = jnp.concatenate([dz, dsu, dsv], axis=1)
            in_name = "mm_odd_in"
        comm.grad(lw["w_in"], _carry(comm, "mm_in_dw", i, _mm, h, dproj, "tn", out_dtype=BF16, name=in_name + "_dw"))
        dx, dxb, dg1 = _carry(comm, "mm_in_dx", i, _mm, dproj, comm.weight(lw["w_in"]), "nt", tn_cap=D_MODEL,
                              norm_bwd=(x0, w["norm_mix_g"], i, dx1), name=in_name + "_dx")
        put("norm_mix_g", i, dg1[0])

    grads = {nm: [vals[k] for k in sorted(vals)] for nm, vals in per_layer.items()}
    grads.update({nm: jnp.stack(grads[nm]) for nm in SMALL})
    grads["norm_final_g"] = dgf[0]
    grads.update(_s5_param_grads(s5, *[jnp.stack(parts) for parts in zip(*s5_grads)]))
    return loss, dx, grads


def _carry(comm, stage, i, fn, *args, **kwargs):
    exchanges = comm.carried_by(stage, i)
    if not exchanges:
        return fn(*args, **kwargs)
    out, moved = fn(*args, exchanges=exchanges, **kwargs)
    comm.delivered(stage, i, moved)
    return out


def _sum_parts(parts, *, name):
    g, r, c = parts.shape
    tr = _pick_tile(r, max(16, EXCHANGE_BLOCK_ELEMS // c), 16)

    def body(p_ref, o_ref):
        acc = p_ref[0].astype(F32)
        for k in range(1, g):
            acc = acc + p_ref[k].astype(F32)
        o_ref[...] = acc

    return pl.pallas_call(
        body, name=name, grid=(r // tr,), in_specs=[pl.BlockSpec((g, tr, c), lambda i: (0, i, 0))],
        out_specs=pl.BlockSpec((tr, c), lambda i: (i, 0)), out_shape=jax.ShapeDtypeStruct((r, c), F32),
        compiler_params=_PAR)(parts)


def _adamw(w, m, v, g_parts, *, name, exchanges=()):
    r, c = w.shape
    g = g_parts.shape[0]
    tc = _pick_tile(c, 8192, LANE)
    tr = _pick_tile(r, max(16, (1 << 18) // tc), 16)
    c1 = 1.0 - ADAM_B1 ** ADAM_STEP
    c2 = 1.0 - ADAM_B2 ** ADAM_STEP

    def body(w_ref, m_ref, v_ref, g_ref, go_ref, d_ref, mo_ref, vo_ref):
        grad = g_ref[0].astype(F32)
        for k in range(1, g):
            grad = grad + g_ref[k].astype(F32)
        m_new = ADAM_B1 * m_ref[...] + (1.0 - ADAM_B1) * grad
        v_new = ADAM_B2 * v_ref[...] + (1.0 - ADAM_B2) * (grad * grad)
        go_ref[...] = grad
        mo_ref[...] = m_new
        vo_ref[...] = v_new
        d_ref[...] = -ADAM_LR * ((m_new / c1) / (jnp.sqrt(v_new / c2) + ADAM_EPS) + ADAM_WD * w_ref[...])

    blk = pl.BlockSpec((tr, tc), lambda i, j: (i, j))
    out = jax.ShapeDtypeStruct((r, c), F32)
    return _call(body, name=name, grid=(r // tr, c // tc),
                 in_specs=[blk, blk, blk, pl.BlockSpec((g, tr, tc), lambda i, j: (0, i, j))],
                 out_specs=[blk, blk, blk, blk], out_shape=[out, out, out, out], args=[w, m, v, g_parts],
                 exchanges=exchanges)


WEIGHT_NAMES = ['norm_mix_g', 'even_w_in', 'even_conv_w', 'ssm_log_step', 'ssm_a_re', 'ssm_a_im', 'ssm_b_re',
                'ssm_b_im', 'ssm_c_re', 'ssm_c_im', 'ssm_d', 'ssm_glu_w', 'ssm_glu_b', 'even_w_out', 'odd_w_in',
                'pool_w', 'pool_scale', 'sgu_norm_g', 'sgu_w', 'sgu_b', 'odd_w_out', 'norm_ffn_g', 'ffn_w_up',
                'ffn_conv_w', 'ffn_conv_b', 'ffn_w_down', 'norm_final_g']
BIG = {'even_w_in': 2, 'ssm_glu_w': 1, 'even_w_out': 1, 'odd_w_in': 2, 'odd_w_out': 1, 'ffn_w_up': 2,
       'ffn_w_down': 1}
SMALL = {'even_conv_w': 2, 'pool_scale': 1, 'sgu_norm_g': 1, 'ffn_conv_w': 2}
SMALL_ROWS = 16


def _pad_to(n, q):
    return -(-n // q) * q


def _pack(arrays, dtype, rows, lead=()):
    flat = [a.reshape(lead + (-1,)).astype(dtype) for a in arrays]
    n = sum(f.shape[-1] for f in flat)
    pad = _pad_to(n, rows * LANE) - n
    if pad:
        flat.append(jnp.zeros(lead + (pad,), dtype))
    return jnp.concatenate(flat, axis=-1).reshape(lead + (rows, -1))


def _unpack(buf, shapes, lead=()):
    flat = buf.reshape(lead + (-1,))
    out, off = [], 0
    for shp in shapes:
        n = math.prod(shp)
        out.append(flat[..., off:off + n].reshape(lead + tuple(shp)))
        off += n
    return out


def _to_dest_major(full, axis):
    shp = full.shape
    split = full.reshape(shp[:axis] + (N_DEV, shp[axis] // N_DEV) + shp[axis + 1:])
    return jnp.moveaxis(split, axis, 0)


def _from_dest_major(blocks, axis):
    moved = jnp.moveaxis(blocks, 0, axis)
    shp = moved.shape
    return moved.reshape(shp[:axis] + (shp[axis] * shp[axis + 1],) + shp[axis + 2:])


def _rows_2d(a):
    return a.reshape(-1, a.shape[-1])


def kernel(x, norm_mix_g, even_w_in, even_conv_w, ssm_log_step, ssm_a_re, ssm_a_im, ssm_b_re, ssm_b_im, ssm_c_re, ssm_c_im, ssm_d, ssm_glu_w, ssm_glu_b, even_w_out, odd_w_in, pool_w, pool_scale, sgu_norm_g, sgu_w, sgu_b, odd_w_out, norm_ffn_g, ffn_w_up, ffn_conv_w, ffn_conv_b, ffn_w_down, norm_final_g, loss_target, m_norm_mix_g, m_even_w_in, m_even_conv_w, m_ssm_log_step, m_ssm_a_re, m_ssm_a_im, m_ssm_b_re, m_ssm_b_im, m_ssm_c_re, m_ssm_c_im, m_ssm_d, m_ssm_glu_w, m_ssm_glu_b, m_even_w_out, m_odd_w_in, m_pool_w, m_pool_scale, m_sgu_norm_g, m_sgu_w, m_sgu_b, m_odd_w_out, m_norm_ffn_g, m_ffn_w_up, m_ffn_conv_w, m_ffn_conv_b, m_ffn_w_down, m_norm_final_g, v_norm_mix_g, v_even_w_in, v_even_conv_w, v_ssm_log_step, v_ssm_a_re, v_ssm_a_im, v_ssm_b_re, v_ssm_b_im, v_ssm_c_re, v_ssm_c_im, v_ssm_d, v_ssm_glu_w, v_ssm_glu_b, v_even_w_out, v_odd_w_in, v_pool_w, v_pool_scale, v_sgu_norm_g, v_sgu_w, v_sgu_b, v_odd_w_out, v_norm_ffn_g, v_ffn_w_up, v_ffn_conv_w, v_ffn_conv_b, v_ffn_w_down, v_norm_final_g):
    given = dict(locals())
    wts = {n: given[n] for n in WEIGHT_NAMES}
    mom = {n: given["m_" + n] for n in WEIGHT_NAMES}
    var = {n: given["v_" + n] for n in WEIGHT_NAMES}
    repl = [n for n in WEIGHT_NAMES if n not in BIG and n not in SMALL]

    small_shapes = [wts[n].shape for n in SMALL]
    comm = _ShardedWeights({n: wts[n].astype(BF16) for n in BIG})
    (small_all,) = comm.start([_Gather(_pack([wts[n] for n in SMALL], F32, SMALL_ROWS))])
    full = {n: wts[n] for n in repl}
    for n, blocks in zip(SMALL, _unpack(small_all, small_shapes, lead=(N_DEV,))):
        full[n] = _from_dest_major(blocks, SMALL[n])

    loss, dx, grads = _device_step(x[0], loss_target[0], full, comm)

    repl_shapes = [wts[n].shape for n in repl]
    pieces = [p.reshape(-1) for n in repl for p in (grads[n] if isinstance(grads[n], list) else [grads[n]])]
    repl_flat = jnp.concatenate(pieces + [loss.reshape(-1)])
    n_repl = repl_flat.shape[0]
    chunk = _pad_to(-(-n_repl // N_DEV), SMALL_ROWS * LANE)
    repl_flat = jnp.pad(repl_flat, (0, N_DEV * chunk - n_repl))
    small_part = _pack([_to_dest_major(grads[n], SMALL[n]) for n in SMALL], F32, SMALL_ROWS, lead=(N_DEV,))
    small_cols = small_part.shape[2]
    small_scatter = _Scatter(
        jnp.concatenate([small_part, repl_flat.reshape(N_DEV, SMALL_ROWS, chunk // SMALL_ROWS)], axis=2))

    _, (small_rs,) = comm.finish(lambda exchanges: (None, _exchange_only(exchanges, name="rs_last")), [small_scatter])
    small_sum = _sum_parts(small_rs, name="rs_sum_small")
    (repl_all,) = _exchange_only([_Gather(small_sum[:, small_cols:])], name="ag_repl")
    repl_sum = repl_all.reshape(-1)
    total_loss = repl_sum[n_repl - 1]

    out = {}
    for n in BIG:
        res = _adamw(_rows_2d(wts[n]), _rows_2d(mom[n]), _rows_2d(var[n]), comm.scattered[n], name="adamw_" + n)
        out[n] = [r.reshape(wts[n].shape) for r in res]

    def small_vec(shard_part, repl_part):
        flat = jnp.concatenate([shard_part.reshape(-1), repl_part])
        return flat.reshape(SMALL_ROWS, -1)

    def small_tree(tree):
        tail = jnp.concatenate([tree[n].reshape(-1) for n in repl])
        tail = jnp.pad(tail, (0, N_DEV * chunk - tail.shape[0]))
        return small_vec(_pack([tree[n] for n in SMALL], F32, SMALL_ROWS), tail)

    res = _adamw(small_tree(wts), small_tree(mom), small_tree(var), small_vec(small_sum[:, :small_cols], repl_sum)[None],
                 name="adamw_small")
    n_small = SMALL_ROWS * small_cols
    for k, r in enumerate(res):
        flat = r.reshape(-1)
        shard = _unpack(flat[:n_small], small_shapes)
        rest = _unpack(flat[n_small:], repl_shapes)
        for n, val in zip(SMALL, shard):
            out.setdefault(n, [None] * 4)[k] = val
        for n, val in zip(repl, rest):
            out.setdefault(n, [None] * 4)[k] = val

    grad_x = dx[None]
    return (total_loss, grad_x, *[out[n][0] for n in WEIGHT_NAMES], *[out[n][1] for n in WEIGHT_NAMES],
            *[out[n][2] for n in WEIGHT_NAMES], *[out[n][3] for n in WEIGHT_NAMES])
```

```python
import functools
import math

import jax
import jax.numpy as jnp
from jax import lax
from jax.experimental import pallas as pl
from jax.experimental.pallas import tpu as pltpu

F32 = jnp.float32
BF16 = jnp.bfloat16

D_MODEL = 1024
DEPTH = 4
D_HALF = D_MODEL // 2
SSM_GROUP = 16
N_SSM_GROUPS = D_HALF // SSM_GROUP
SSM_STATE = 64
POOL_WINDOWS = (2, 4, 8, 16)
SGU_HEADS = 4
CHUNK = 128
D_FF = 2816
CONV_WIDTH = 3
EPS = 1e-6
N_DEV = 8

ADAM_LR = 0.001
ADAM_B1 = 0.9
ADAM_B2 = 0.999
ADAM_EPS = 1e-08
ADAM_WD = 0.01
ADAM_STEP = 10

LANE = 128
SUBLANE = 8
S5_LANE_BLOCKS = D_HALF // LANE
S5_STATE_LANES = (N_SSM_GROUPS // S5_LANE_BLOCKS) * SSM_STATE
S5_TIME_CHUNK = 512
EXCHANGE_BLOCK_ELEMS = 1 << 20

GELU_K = math.sqrt(2.0 / math.pi)
GELU_C = 0.044715

_ARB = pltpu.CompilerParams(dimension_semantics=("arbitrary",))
_PAR = pltpu.CompilerParams(dimension_semantics=("parallel",))


def _pick_tile(n, cap, mult):
    if n <= cap:
        return n
    best = None
    for t in range(mult, cap + 1, mult):
        if n % t == 0:
            best = t
    assert best is not None, (n, cap, mult)
    return best


_MESH = pl.DeviceIdType.MESH
_ANY = pl.BlockSpec(memory_space=pl.ANY)
SEMS_PER_EXCHANGE = N_DEV - 1


class _Gather:
    def __init__(self, src, into=None, row0=0):
        self.src, self.into, self.row0 = src, into, row0
        self.out_shape = jax.ShapeDtypeStruct((N_DEV,) + src.shape if into is None else into.shape, src.dtype)

    def copies(self, x_ref, whole_ref, send_sems, recv_sems, local_sem):
        out_ref = whole_ref if self.into is None else whole_ref.at[:, pl.ds(self.row0, self.src.shape[0])]
        x, y, cc = lax.axis_index("x"), lax.axis_index("y"), lax.axis_index("c")
        me, sibling = (x, y, cc), (x, y, 1 - cc)
        chips = [(1 - x, y), (x, 1 - y), (1 - x, 1 - y)]

        def rows(px, py, pc):
            return out_ref.at[4 * px + 2 * py + pc]

        def copy(k, block, to, src=None):
            return pltpu.make_async_remote_copy(
                src_ref=rows(*block) if src is None else src, dst_ref=rows(*block),
                send_sem=send_sems.at[k], recv_sem=recv_sems.at[k], device_id=to, device_id_type=_MESH)

        return dict(
            mine=pltpu.make_async_copy(x_ref, rows(*me), local_sem),
            first=[copy(0, me, sibling, src=x_ref)] + [copy(1 + k, me, (*chip, cc), src=x_ref)
                                                       for k, chip in enumerate(chips)],
            passed=[copy(4 + k, (*chip, cc), sibling) for k, chip in enumerate(chips)],
            over_ici=[copy(1 + k, (*chip, cc), me) for k, chip in enumerate(chips)],
            from_sibling=[copy(0, sibling, me)] + [copy(4 + k, (*chip, 1 - cc), me) for k, chip in enumerate(chips)])

    def start(self, *refs):
        cps = self.copies(*refs)
        cps["mine"].start()
        for cp in cps["first"]:
            cp.start()

    def finish(self, *refs):
        cps = self.copies(*refs)
        for arrived, onward in zip(cps["over_ici"], cps["passed"]):
            arrived.wait_recv()
            onward.start()
        for arrived in cps["from_sibling"]:
            arrived.wait_recv()
        for cp in cps["first"] + cps["passed"]:
            cp.wait_send()
        cps["mine"].wait()


class _Scatter:
    def __init__(self, src, into=None, row0=0):
        self.src, self.into, self.row0 = src, into, row0
        whole = src if into is None else into
        self.out_shape = jax.ShapeDtypeStruct(whole.shape, whole.dtype)

    def copies(self, p_ref, whole_ref, send_sems, recv_sems, local_sem):
        x, y, cc = lax.axis_index("x"), lax.axis_index("y"), lax.axis_index("c")
        me = 4 * x + 2 * y + cc
        rows = self.src.shape[1]
        out_ref = whole_ref if self.into is None else whole_ref.at[:, pl.ds(self.row0, rows)]
        sends, arrivals = [], []
        for k in range(1, N_DEV):
            px = (1 - x) if k & 4 else x
            py = (1 - y) if k & 2 else y
            pc = (1 - cc) if k & 1 else cc
            peer = 4 * px + 2 * py + pc
            kw = dict(send_sem=send_sems.at[k - 1], recv_sem=recv_sems.at[k - 1], device_id=(px, py, pc),
                      device_id_type=_MESH)
            sends.append(pltpu.make_async_remote_copy(src_ref=p_ref.at[peer], dst_ref=out_ref.at[me], **kw))
            arrivals.append(pltpu.make_async_remote_copy(src_ref=p_ref.at[me], dst_ref=out_ref.at[peer], **kw))
        return dict(mine=pltpu.make_async_copy(p_ref.at[me], out_ref.at[me], local_sem), sends=sends,
                    arrivals=arrivals)

    def start(self, *refs):
        cps = self.copies(*refs)
        cps["mine"].start()
        for cp in cps["sends"]:
            cp.start()

    def finish(self, *refs):
        cps = self.copies(*refs)
        for cp in cps["arrivals"]:
            cp.wait_recv()
        for cp in cps["sends"]:
            cp.wait_send()
        cps["mine"].wait()


class _SemView:
    def __init__(self, ref, lo):
        self.ref, self.lo = ref, lo

    @property
    def at(self):
        return self

    def __getitem__(self, k):
        return self.ref.at[self.lo + k]


def _call(body, *, name, grid, in_specs, out_specs, out_shape, args, scratch_shapes=(), parallel=True, exchanges=()):
    n_axes = len(grid)
    if not exchanges:
        sem = ("parallel" if parallel else "arbitrary",) * n_axes
        return pl.pallas_call(
            body, name=name, grid=grid, in_specs=in_specs, out_specs=out_specs, out_shape=out_shape,
            scratch_shapes=scratch_shapes, compiler_params=pltpu.CompilerParams(dimension_semantics=sem))(*args)
    single = not isinstance(out_shape, (list, tuple))
    out_specs = [out_specs] if single else list(out_specs)
    out_shape = [out_shape] if single else list(out_shape)
    n_in, n_out, n_scr, n_x = len(in_specs), len(out_specs), len(scratch_shapes), len(exchanges)
    landing = [(e, ex.into) for e, ex in enumerate(exchanges) if ex.into is not None]
    aliases = {n_in + n_x + pos: n_out + e for pos, (e, _) in enumerate(landing)}

    def wrapped(*refs):
        ins, refs = refs[:n_in], refs[n_in:]
        x_in, refs = refs[:n_x], refs[n_x + len(landing):]
        outs, refs = refs[:n_out], refs[n_out:]
        x_out, refs = refs[:n_x], refs[n_x:]
        scr, (send_sems, recv_sems, local_sems) = refs[:n_scr], refs[n_scr:]
        ids = [pl.program_id(k) for k in range(n_axes)]
        first = functools.reduce(jnp.logical_and, [i == 0 for i in ids])
        last = functools.reduce(jnp.logical_and, [i == g - 1 for i, g in zip(ids, grid)])

        def sems(e):
            lo = e * SEMS_PER_EXCHANGE
            return _SemView(send_sems, lo), _SemView(recv_sems, lo), local_sems.at[e]

        @pl.when(first)
        def _():
            for e, ex in enumerate(exchanges):
                ex.start(x_in[e], x_out[e], *sems(e))

        body(*ins, *outs, *scr)

        @pl.when(last)
        def _():
            for e, ex in enumerate(exchanges):
                ex.finish(x_in[e], x_out[e], *sems(e))

    res = pl.pallas_call(
        wrapped, name=name, grid=grid, in_specs=list(in_specs) + [_ANY] * (n_x + len(landing)),
        out_specs=out_specs + [_ANY] * n_x, out_shape=out_shape + [ex.out_shape for ex in exchanges],
        input_output_aliases=aliases,
        scratch_shapes=list(scratch_shapes) + [pltpu.SemaphoreType.DMA((n_x * SEMS_PER_EXCHANGE,)),
                                               pltpu.SemaphoreType.DMA((n_x * SEMS_PER_EXCHANGE,)),
                                               pltpu.SemaphoreType.DMA((n_x,))],
        compiler_params=pltpu.CompilerParams(dimension_semantics=("arbitrary",) * n_axes),
    )(*args, *[ex.src for ex in exchanges], *[buf for _, buf in landing])
    outs, x_outs = res[:n_out], res[n_out:]
    return (outs[0] if single else outs), x_outs


def _exchange_only(exchanges, *, name):
    def body():
        pass

    return _call(body, name=name, grid=(1,), in_specs=[], out_specs=[], out_shape=[], args=[],
                 exchanges=exchanges)[1]


def _shift_dn(x, d):
    rolled = pltpu.roll(x, d, 0)
    if x.shape[0] <= SUBLANE or d >= SUBLANE:
        row = lax.broadcasted_iota(jnp.int32, x.shape, 0)
        return jnp.where(row >= d, rolled, 0.0)
    row = lax.broadcasted_iota(jnp.int32, (SUBLANE, x.shape[1]), 0)
    return jnp.concatenate([jnp.where(row >= d, rolled[:SUBLANE], 0.0), rolled[SUBLANE:]], axis=0)


def _shift_up(x, d):
    n = x.shape[0]
    rolled = pltpu.roll(x, n - d, 0)
    if n <= SUBLANE or d >= SUBLANE:
        row = lax.broadcasted_iota(jnp.int32, x.shape, 0)
        return jnp.where(row < n - d, rolled, 0.0)
    row = lax.broadcasted_iota(jnp.int32, (SUBLANE, x.shape[1]), 0)
    return jnp.concatenate([rolled[:n - SUBLANE], jnp.where(row < SUBLANE - d, rolled[n - SUBLANE:], 0.0)], axis=0)


def _gelu(x):
    return 0.5 * x * (1.0 + jnp.tanh(GELU_K * (x + GELU_C * x * x * x)))


def _gelu_grad(x):
    t = jnp.tanh(GELU_K * (x + GELU_C * x * x * x))
    return 0.5 * (1.0 + t) + 0.5 * x * (1.0 - t * t) * (GELU_K * (1.0 + 3.0 * GELU_C * x * x))


def _sigmoid(x):
    return 0.5 + 0.5 * jnp.tanh(0.5 * x)


def _conv3(x, w_ref):
    return w_ref[0:1, :] * _shift_dn(x, 2) + w_ref[1:2, :] * _shift_dn(x, 1) + w_ref[2:3, :] * x


def _dot(a, b, dims):
    return lax.dot_general(a.astype(BF16), b.astype(BF16), (dims, ((), ())), preferred_element_type=F32)


_NN = ((1,), (0,))
_NT = ((1,), (1,))
_TN = ((0,), (0,))


def _tiles(name, m, n, default):
    tm, tn = TILES.get(name, default)
    return math.gcd(tm, m), math.gcd(tn, n)


def _mm(a, b, mode, *, name, out_dtype=F32, add=None, norm=None, norm_bwd=None, tm_cap=512, tn_cap=1536,
        exchanges=()):
    halves = None
    if mode == "tn":
        r, m = a.shape
        n = b.shape[-1] * (2 if b.ndim == 3 else 1)
        tm, tn = _tiles(name, m, n, (_pick_tile(m, 256, LANE), _pick_tile(n, tn_cap, LANE)))
        if b.ndim == 3:
            per_half = b.shape[-1] // tn
            b_spec = pl.BlockSpec((None, r, tn), lambda i, j: (j // per_half, 0, j % per_half))
        else:
            b_spec = pl.BlockSpec((r, tn), lambda i, j: (0, j))
        in_specs = [pl.BlockSpec((r, tm), lambda i, j: (0, i)), b_spec]
        dims = _TN
    elif mode == "nt" and a.ndim == 3:
        _, m, halves = a.shape
        n = b.shape[0]
        tm, tn = _tiles(name, m, n, (_pick_tile(m, tm_cap, SUBLANE), _pick_tile(n, tn_cap, LANE)))
        in_specs = [pl.BlockSpec((2, tm, halves), lambda i, j: (0, i, 0)),
                    pl.BlockSpec((tn, 2 * halves), lambda i, j: (j, 0))]
        dims = _NT
    elif mode == "nn":
        m, k = a.shape
        n = b.shape[1]
        tm, tn = _tiles(name, m, n, (_pick_tile(m, tm_cap, SUBLANE), _pick_tile(n, tn_cap, LANE)))
        in_specs = [pl.BlockSpec((tm, k), lambda i, j: (i, 0)), pl.BlockSpec((k, tn), lambda i, j: (0, j))]
        dims = _NN
    else:
        m, k = a.shape
        n = b.shape[0]
        tm, tn = _tiles(name, m, n, (_pick_tile(m, tm_cap, SUBLANE), _pick_tile(n, tn_cap, LANE)))
        in_specs = [pl.BlockSpec((tm, k), lambda i, j: (i, 0)), pl.BlockSpec((tn, k), lambda i, j: (j, 0))]
        dims = _NT
    assert m % tm == 0 and n % tn == 0, (name, m, n, tm, tn)
    args = [a, b]
    tile = pl.BlockSpec((tm, tn), lambda i, j: (i, j))
    if add is not None:
        in_specs.append(tile)
        args.append(add)
    out_specs, out_shape = tile, jax.ShapeDtypeStruct((m, n), out_dtype)
    if norm is not None:
        gains, layer = norm
        assert tn == n
        in_specs.append(pl.BlockSpec((None, 1, n), lambda i, j: (layer, 0, 0)))
        args.append(gains.reshape(gains.shape[0], 1, n))
        out_specs, out_shape = [tile, tile], [out_shape, jax.ShapeDtypeStruct((m, n), BF16)]
    if norm_bwd is not None:
        x_in, gains, layer, res = norm_bwd
        assert tn == n and add is None and norm is None
        vec = pl.BlockSpec((None, 1, n), lambda i, j: (layer, 0, 0))
        in_specs += [tile, vec, tile]
        args += [x_in, gains.reshape(gains.shape[0], 1, n), res]
        out_specs = [tile, tile, pl.BlockSpec((1, n), lambda i, j: (0, 0))]
        out_shape = [jax.ShapeDtypeStruct((m, n), F32), jax.ShapeDtypeStruct((m, n), BF16),
                     jax.ShapeDtypeStruct((1, n), F32)]

    def body(*refs):
        if halves is None:
            acc = _dot(refs[0][...], refs[1][...], dims)
        else:
            acc = (_dot(refs[0][0], refs[1][:, :halves], dims) + _dot(refs[0][1], refs[1][:, halves:], dims))
        if add is not None:
            acc = acc + refs[2][...]
        if norm_bwd is not None:
            x_ref, g_ref, res_ref, dx_ref, dxb_ref, dg_ref = refs[2:]

            @pl.when(pl.program_id(0) == 0)
            def _():
                dg_ref[...] = jnp.zeros_like(dg_ref)
            dx, xn = _rms_bwd_rows(acc, x_ref[...], g_ref[...])
            dx = dx + res_ref[...]
            dx_ref[...] = dx
            dxb_ref[...] = dx.astype(BF16)
            dg_ref[...] += jnp.sum(acc * xn, axis=0, keepdims=True)
        elif norm is None:
            refs[-1][...] = acc.astype(out_dtype)
        else:
            refs[-2][...] = acc.astype(out_dtype)
            r = lax.rsqrt(jnp.mean(acc * acc, axis=-1, keepdims=True) + EPS)
            refs[-1][...] = (acc * r * refs[-3][...]).astype(BF16)

    return _call(body, name=name, grid=(m // tm, n // tn), in_specs=in_specs, out_specs=out_specs,
                 out_shape=out_shape, args=args, parallel=norm_bwd is None, exchanges=exchanges)


def _rms_fwd(x, g, *, name):
    l, d = x.shape
    tl = _pick_tile(l, 512, SUBLANE)

    def body(x_ref, g_ref, h_ref):
        xv = x_ref[...]
        r = lax.rsqrt(jnp.mean(xv * xv, axis=-1, keepdims=True) + EPS)
        h_ref[...] = (xv * r * g_ref[...]).astype(BF16)

    return pl.pallas_call(
        body, name=name, grid=(l // tl,),
        in_specs=[pl.BlockSpec((tl, d), lambda i: (i, 0)), pl.BlockSpec((1, d), lambda i: (0, 0))],
        out_specs=pl.BlockSpec((tl, d), lambda i: (i, 0)),
        out_shape=jax.ShapeDtypeStruct((l, d), BF16), compiler_params=_PAR)(x, g)


def _rms_bwd_rows(dh, xv, g):
    r = lax.rsqrt(jnp.mean(xv * xv, axis=-1, keepdims=True) + EPS)
    a = dh * g
    m = jnp.mean(a * xv, axis=-1, keepdims=True)
    return r * a - xv * (r * r * r) * m, xv * r


def _loss_head(x, g, tgt, *, name):
    l, d = x.shape
    tl = _pick_tile(l, 512, SUBLANE)

    def body(x_ref, g_ref, t_ref, loss_ref, dx_ref, dxb_ref, dg_ref):
        @pl.when(pl.program_id(0) == 0)
        def _():
            dg_ref[...] = jnp.zeros_like(dg_ref)
            loss_ref[...] = jnp.zeros_like(loss_ref)
        xv = x_ref[...]
        gv = g_ref[...]
        r = lax.rsqrt(jnp.mean(xv * xv, axis=-1, keepdims=True) + EPS)
        err = xv * r * gv - t_ref[...]
        row_loss = jnp.sum(err * err, axis=-1, keepdims=True) * (0.5 / d)
        loss_ref[...] += jnp.sum(row_loss, axis=0, keepdims=True)
        dy = err * (1.0 / d)
        dx, xn = _rms_bwd_rows(dy, xv, gv)
        dx_ref[...] = dx
        dxb_ref[...] = dx.astype(BF16)
        dg_ref[...] += jnp.sum(dy * xn, axis=0, keepdims=True)

    row = pl.BlockSpec((tl, d), lambda i: (i, 0))
    vec = pl.BlockSpec((1, d), lambda i: (0, 0))
    one = pl.BlockSpec((1, 1), lambda i: (0, 0))
    return pl.pallas_call(
        body, name=name, grid=(l // tl,), in_specs=[row, vec, row], out_specs=[one, row, row, vec],
        out_shape=[jax.ShapeDtypeStruct((1, 1), F32), jax.ShapeDtypeStruct((l, d), F32),
                   jax.ShapeDtypeStruct((l, d), BF16), jax.ShapeDtypeStruct((1, d), F32)],
        compiler_params=_ARB)(x, g, tgt)


def _ffn_act(up, cw, cb, *, name, exchanges=()):
    l = up.shape[0]
    nb = D_FF // LANE

    def body(ug_ref, uv_ref, wg_ref, wv_ref, bg_ref, bv_ref, o_ref, gv_ref):
        gc = _conv3(ug_ref[...], wg_ref) + bg_ref[...]
        vc = _conv3(uv_ref[...], wv_ref) + bv_ref[...]
        gv_ref[0] = gc
        gv_ref[1] = vc
        o_ref[...] = (gc * _sigmoid(gc) * vc).astype(BF16)

    col = lambda off: pl.BlockSpec((l, LANE), lambda j: (0, j + off))
    w3 = lambda off: pl.BlockSpec((CONV_WIDTH, LANE), lambda j: (0, j + off))
    b1 = lambda off: pl.BlockSpec((1, LANE), lambda j: (0, j + off))
    return _call(body, name=name, grid=(nb,), in_specs=[col(0), col(nb), w3(0), w3(nb), b1(0), b1(nb)],
                 out_specs=[col(0), pl.BlockSpec((2, l, LANE), lambda j: (0, 0, j))],
                 out_shape=[jax.ShapeDtypeStruct((l, D_FF), BF16), jax.ShapeDtypeStruct((2, l, D_FF), F32)],
                 args=[up, up, cw, cw, cb, cb], exchanges=exchanges)


def _ffn_act_bwd(up, gv, dact, cw, *, name, exchanges=()):
    l = up.shape[0]
    nb = D_FF // LANE

    def half_bwd(k, dc, x, w_ref, dup_ref, dcw_ref, dcb_ref):
        d1, d2 = _shift_up(dc, 1), _shift_up(dc, 2)
        dcb_ref[k] = jnp.sum(dc, axis=0, keepdims=True)
        dcw_ref[k] = jnp.concatenate([jnp.sum(d2 * x, axis=0, keepdims=True),
                                      jnp.sum(d1 * x, axis=0, keepdims=True),
                                      jnp.sum(dc * x, axis=0, keepdims=True)], axis=0)
        dup_ref[k] = (w_ref[2:3, :] * dc + w_ref[1:2, :] * d1 + w_ref[0:1, :] * d2).astype(BF16)

    def body(ug_ref, uv_ref, gv_ref, da_ref, wg_ref, wv_ref, dup_ref, dcw_ref, dcb_ref):
        gc, vc, da = gv_ref[0], gv_ref[1], da_ref[...]
        sg = _sigmoid(gc)
        half_bwd(0, da * vc * (sg * (1.0 + gc * (1.0 - sg))), ug_ref[...], wg_ref, dup_ref, dcw_ref, dcb_ref)
        half_bwd(1, da * (gc * sg), uv_ref[...], wv_ref, dup_ref, dcw_ref, dcb_ref)

    col = lambda off: pl.BlockSpec((l, LANE), lambda j: (0, j + off))
    w3 = lambda off: pl.BlockSpec((CONV_WIDTH, LANE), lambda j: (0, j + off))
    both = lambda rows: pl.BlockSpec((2, rows, LANE), lambda j: (0, 0, j))
    res = _call(
        body, name=name, grid=(nb,),
        in_specs=[col(0), col(nb), both(l), col(0), w3(0), w3(nb)],
        out_specs=[both(l), both(CONV_WIDTH), both(1)],
        out_shape=[jax.ShapeDtypeStruct((2, l, D_FF), BF16), jax.ShapeDtypeStruct((2, CONV_WIDTH, D_FF), F32),
                   jax.ShapeDtypeStruct((2, 1, D_FF), F32)],
        args=[up, up, gv, dact, cw, cw], exchanges=exchanges)
    (dup, dcw, dcb), moved = res if exchanges else (res, None)
    outs = [dup, jnp.concatenate([dcw[0], dcw[1]], axis=1), jnp.concatenate([dcb[0], dcb[1]], axis=1)]
    return (outs, moved) if exchanges else outs


def _sconv_fwd(proj, cw, *, name):
    l = proj.shape[0]
    nb = D_HALF // LANE

    def body(xa_ref, ba_ref, ca_ref, w_ref, o_ref):
        o_ref[...] = (ba_ref[...] * _conv3(ca_ref[...] * xa_ref[...], w_ref)).astype(BF16)

    col = lambda off: pl.BlockSpec((l, LANE), lambda j: (0, j + off))
    return pl.pallas_call(
        body, name=name, grid=(nb,),
        in_specs=[col(0), col(nb), col(2 * nb), pl.BlockSpec((CONV_WIDTH, LANE), lambda j: (0, j))],
        out_specs=col(0), out_shape=jax.ShapeDtypeStruct((l, 2 * D_HALF), BF16),
        compiler_params=_PAR)(proj, proj, proj, cw)


def _sconv_bwd(proj, dcat, cw, *, name):
    l = proj.shape[0]
    nb = D_HALF // LANE

    def body(xa_ref, ba_ref, ca_ref, dy_ref, w_ref, dxa_ref, dba_ref, dca_ref, dw_ref):
        xa, ba, ca, dy = xa_ref[...], ba_ref[...], ca_ref[...], dy_ref[...]
        q = ca * xa
        dba_ref[...] = (dy * _conv3(q, w_ref)).astype(BF16)
        dconv = dy * ba
        d1, d2 = _shift_up(dconv, 1), _shift_up(dconv, 2)
        dw_ref[...] = jnp.concatenate([jnp.sum(d2 * q, axis=0, keepdims=True), jnp.sum(d1 * q, axis=0, keepdims=True),
                                       jnp.sum(dconv * q, axis=0, keepdims=True)], axis=0)
        dq = w_ref[2:3, :] * dconv + w_ref[1:2, :] * d1 + w_ref[0:1, :] * d2
        dxa_ref[...] = (dq * ca).astype(BF16)
        dca_ref[...] = (dq * xa).astype(BF16)

    col = lambda off: pl.BlockSpec((l, LANE), lambda j: (0, j + off))
    w3 = pl.BlockSpec((CONV_WIDTH, LANE), lambda j: (0, j))
    piece = jax.ShapeDtypeStruct((l, D_HALF), BF16)
    return pl.pallas_call(
        body, name=name, grid=(nb,),
        in_specs=[col(0), col(nb), col(2 * nb), col(0), w3],
        out_specs=[col(0), col(0), col(0), w3],
        out_shape=[piece, piece, piece, jax.ShapeDtypeStruct((CONV_WIDTH, D_HALF), F32)],
        compiler_params=_PAR)(proj, proj, proj, dcat, cw)


def _s5_prep(log_step, a_re, a_im, b_re, b_im):
    step = jnp.exp(log_step)[:, None]
    mag = jnp.exp(a_re * step)
    lr = mag * jnp.cos(a_im * step)
    li = mag * jnp.sin(a_im * step)
    nr = lr - 1.0
    den = a_re * a_re + a_im * a_im
    qr = (nr * a_re + li * a_im) / den
    qi = (li * a_re - nr * a_im) / den
    br = qr[..., None] * b_re - qi[..., None] * b_im
    bi = qr[..., None] * b_im + qi[..., None] * b_re
    return lr, li, br, bi


def _block_diag(m):
    nb, ng, r, c = m.shape
    eye = jnp.eye(ng, dtype=m.dtype)
    return jnp.einsum("bgrc,gh->bgrhc", m, eye).reshape(nb, ng * r, ng * c)


def _block_diag_extract(w, r, c):
    nb = w.shape[0]
    ng = w.shape[1] // r
    w5 = w.reshape(nb, ng, r, ng, c)
    return jnp.einsum("bgrhc,gh->bgrc", w5, jnp.eye(ng, dtype=w.dtype))


def _s5_mats(br, bi, c_re, c_im):
    g8 = N_SSM_GROUPS // S5_LANE_BLOCKS
    to_blk = lambda m: m.reshape(S5_LANE_BLOCKS, g8, m.shape[1], m.shape[2])
    wb = jnp.concatenate([_block_diag(to_blk(jnp.swapaxes(br, 1, 2))),
                          _block_diag(to_blk(jnp.swapaxes(bi, 1, 2)))], axis=2)
    wc = jnp.concatenate([_block_diag(to_blk(jnp.swapaxes(c_re, 1, 2))),
                          _block_diag(to_blk(jnp.swapaxes(-c_im, 1, 2)))], axis=1)
    return wb, wc


def _s5_mats_bwd(dwb, dwc):
    g, p, h = N_SSM_GROUPS, SSM_STATE, SSM_GROUP
    half = S5_STATE_LANES
    dbr = jnp.swapaxes(_block_diag_extract(dwb[:, :, :half], h, p).reshape(g, h, p), 1, 2)
    dbi = jnp.swapaxes(_block_diag_extract(dwb[:, :, half:], h, p).reshape(g, h, p), 1, 2)
    dcr = jnp.swapaxes(_block_diag_extract(dwc[:, :half, :], p, h).reshape(g, p, h), 1, 2)
    dci = -jnp.swapaxes(_block_diag_extract(dwc[:, half:, :], p, h).reshape(g, p, h), 1, 2)
    return dbr, dbi, dcr, dci


def _s5_scan_consts(log_step, a_re, a_im, reverse):
    step = jnp.exp(log_step)[:, None]
    xr = (a_re * step).reshape(S5_LANE_BLOCKS, 1, S5_STATE_LANES)
    xi = (a_im * step).reshape(S5_LANE_BLOCKS, 1, S5_STATE_LANES)
    if reverse:
        xi = -xi
    row = jnp.arange(SUBLANE, dtype=F32).reshape(1, SUBLANE, 1)

    def power(n):
        mag = jnp.exp(n * xr)
        return jnp.concatenate([mag * jnp.cos(n * xi), mag * jnp.sin(n * xi)], axis=-1)

    kinds = []
    for d in (1, 2, 4):
        keep = (row <= SUBLANE - 1 - d) if reverse else (row >= d)
        kinds.append(jnp.where(keep, power(jnp.full_like(row, float(d))), 0.0))
    kinds.append(power((SUBLANE - row) if reverse else (row + 1.0)))
    return jnp.stack(kinds, axis=1)


def _scan_rows(s_ref, sc_ref, carry_ref, n_rows, reverse):
    n_grp = n_rows // SUBLANE
    n_col = S5_STATE_LANES // LANE
    half = S5_STATE_LANES

    def step(i, carry):
        grp = (n_grp - 1 - i) if reverse else i
        r0 = pl.multiple_of(grp * SUBLANE, SUBLANE)
        out = []
        for cb in range(n_col):
            lo, hi = cb * LANE, half + cb * LANE
            re = s_ref[pl.ds(r0, SUBLANE), lo:lo + LANE]
            im = s_ref[pl.ds(r0, SUBLANE), hi:hi + LANE]
            for k, d in enumerate((1, 2, 4)):
                sh = (SUBLANE - d) if reverse else d
                rr, ri = pltpu.roll(re, sh, 0), pltpu.roll(im, sh, 0)
                ar, ai = sc_ref[k, :, lo:lo + LANE], sc_ref[k, :, hi:hi + LANE]
                re, im = re + (ar * rr - ai * ri), im + (ar * ri + ai * rr)
            pr, pi = sc_ref[3, :, lo:lo + LANE], sc_ref[3, :, hi:hi + LANE]
            cr, ci = carry[2 * cb], carry[2 * cb + 1]
            re, im = re + (pr * cr - pi * ci), im + (pr * ci + pi * cr)
            s_ref[pl.ds(r0, SUBLANE), lo:lo + LANE] = re
            s_ref[pl.ds(r0, SUBLANE), hi:hi + LANE] = im
            edge = 0 if reverse else SUBLANE - 1
            out.append(jnp.broadcast_to(re[edge:edge + 1, :], (SUBLANE, LANE)))
            out.append(jnp.broadcast_to(im[edge:edge + 1, :], (SUBLANE, LANE)))
        return tuple(out)

    init = []
    for cb in range(n_col):
        init.append(carry_ref[:, cb * LANE:(cb + 1) * LANE])
        init.append(carry_ref[:, half + cb * LANE:half + (cb + 1) * LANE])
    fin = lax.fori_loop(0, n_grp, step, tuple(init), unroll=2)
    for cb in range(n_col):
        carry_ref[:, cb * LANE:(cb + 1) * LANE] = fin[2 * cb]
        carry_ref[:, half + cb * LANE:half + (cb + 1) * LANE] = fin[2 * cb + 1]


def _s5_fwd(proj, wb, wc, d_skip, sc, *, name, exchanges=()):
    l = proj.shape[0]
    tt = _pick_tile(l, S5_TIME_CHUNK, SUBLANE)
    u_off = (proj.shape[1] - D_HALF) // LANE
    w2 = 2 * S5_STATE_LANES

    def body(u_ref, wb_ref, wc_ref, d_ref, sc_ref, s_ref, y_ref, carry_ref):
        @pl.when(pl.program_id(1) == 0)
        def _():
            carry_ref[...] = jnp.zeros_like(carry_ref)
        u = u_ref[...]
        s_ref[...] = _dot(u, wb_ref[0], _NN)
        _scan_rows(s_ref, sc_ref.at[0], carry_ref, tt, False)
        y_ref[...] = _dot(s_ref[...], wc_ref[0], _NN) + d_ref[...] * u

    return _call(
        body, name=name, grid=(S5_LANE_BLOCKS, l // tt),
        in_specs=[pl.BlockSpec((tt, LANE), lambda b, t: (t, b + u_off)),
                  pl.BlockSpec((1, LANE, w2), lambda b, t: (b, 0, 0)),
                  pl.BlockSpec((1, w2, LANE), lambda b, t: (b, 0, 0)),
                  pl.BlockSpec((1, LANE), lambda b, t: (0, b)),
                  pl.BlockSpec((1, 4, SUBLANE, w2), lambda b, t: (b, 0, 0, 0))],
        out_specs=[pl.BlockSpec((tt, w2), lambda b, t: (t, b)), pl.BlockSpec((tt, LANE), lambda b, t: (t, b))],
        out_shape=[jax.ShapeDtypeStruct((l, S5_LANE_BLOCKS * w2), F32), jax.ShapeDtypeStruct((l, D_HALF), F32)],
        scratch_shapes=[pltpu.VMEM((SUBLANE, w2), F32)],
        args=[proj, wb, wc, d_skip, sc], parallel=False, exchanges=exchanges)


def _s5_bwd(proj, dy, states, wb, wc, d_skip, sc_rev, *, name, exchanges=()):
    l = proj.shape[0]
    tt = _pick_tile(l, S5_TIME_CHUNK, SUBLANE)
    nt = l // tt
    u_off = (proj.shape[1] - D_HALF) // LANE
    w2 = 2 * S5_STATE_LANES
    half = S5_STATE_LANES
    grp_per_chunk = tt // SUBLANE

    def body(u_ref, dy_ref, s_ref, halo_ref, wb_ref, wc_ref, d_ref, sc_ref,
             du_ref, dwb_ref, dwc_ref, dlam_ref, dd_ref, g_scr, carry_ref):
        t = pl.program_id(1)

        @pl.when(t == 0)
        def _():
            carry_ref[...] = jnp.zeros_like(carry_ref)
            dwb_ref[...] = jnp.zeros_like(dwb_ref)
            dwc_ref[...] = jnp.zeros_like(dwc_ref)
            dlam_ref[...] = jnp.zeros_like(dlam_ref)
            dd_ref[...] = jnp.zeros_like(dd_ref)

        u = u_ref[...]
        dyv = dy_ref[...]
        g_scr[...] = _dot(dyv, wc_ref[0], _NT)
        _scan_rows(g_scr, sc_ref.at[0], carry_ref, tt, True)
        gv = g_scr[...]
        du_ref[...] = (_dot(gv, wb_ref[0], _NT) + d_ref[...] * dyv).astype(BF16)
        dwb_ref[0] += _dot(u, gv, _TN)
        sv = s_ref[...]
        dwc_ref[0] += _dot(sv, dyv, _TN)
        dd_ref[...] += jnp.sum(dyv * u, axis=0, keepdims=True)
        first_chunk = t == nt - 1
        halo = jnp.where(first_chunk, 0.0, halo_ref[SUBLANE - 1:SUBLANE, :])
        row = lax.broadcasted_iota(jnp.int32, sv.shape, 0)
        sp = jnp.where(row == 0, jnp.broadcast_to(halo, sv.shape), pltpu.roll(sv, 1, 0))
        gr, gi = gv[:, :half], gv[:, half:]
        sr, si = sp[:, :half], sp[:, half:]
        dlr = jnp.sum(gr * sr + gi * si, axis=0, keepdims=True)
        dli = jnp.sum(gi * sr - gr * si, axis=0, keepdims=True)
        dlam_ref[0] += jnp.concatenate([dlr, dli], axis=1)

    rev = lambda t: nt - 1 - t
    return _call(
        body, name=name, grid=(S5_LANE_BLOCKS, nt),
        in_specs=[pl.BlockSpec((tt, LANE), lambda b, t: (rev(t), b + u_off)),
                  pl.BlockSpec((tt, LANE), lambda b, t: (rev(t), b)),
                  pl.BlockSpec((tt, w2), lambda b, t: (rev(t), b)),
                  pl.BlockSpec((SUBLANE, w2), lambda b, t: (jnp.maximum(rev(t) * grp_per_chunk - 1, 0), b)),
                  pl.BlockSpec((1, LANE, w2), lambda b, t: (b, 0, 0)),
                  pl.BlockSpec((1, w2, LANE), lambda b, t: (b, 0, 0)),
                  pl.BlockSpec((1, LANE), lambda b, t: (0, b)),
                  pl.BlockSpec((1, 4, SUBLANE, w2), lambda b, t: (b, 0, 0, 0))],
        out_specs=[pl.BlockSpec((tt, LANE), lambda b, t: (rev(t), b)),
                   pl.BlockSpec((1, LANE, w2), lambda b, t: (b, 0, 0)),
                   pl.BlockSpec((1, w2, LANE), lambda b, t: (b, 0, 0)),
                   pl.BlockSpec((1, 1, w2), lambda b, t: (b, 0, 0)),
                   pl.BlockSpec((1, LANE), lambda b, t: (0, b))],
        out_shape=[jax.ShapeDtypeStruct((l, D_HALF), BF16),
                   jax.ShapeDtypeStruct((S5_LANE_BLOCKS, LANE, w2), F32),
                   jax.ShapeDtypeStruct((S5_LANE_BLOCKS, w2, LANE), F32),
                   jax.ShapeDtypeStruct((S5_LANE_BLOCKS, 1, w2), F32),
                   jax.ShapeDtypeStruct((1, D_HALF), F32)],
        scratch_shapes=[pltpu.VMEM((tt, w2), F32), pltpu.VMEM((SUBLANE, w2), F32)],
        args=[proj, dy, states, states, wb, wc, d_skip, sc_rev], parallel=False, exchanges=exchanges)


def _glu_fwd(ypre, wg, bg, cat, *, name):
    l, d = ypre.shape
    tl = _pick_tile(l, 512, SUBLANE)

    def body(y_ref, w_ref, b_ref, cat_ref, o_ref):
        yg = _gelu(y_ref[...])
        o_ref[...] = (yg * _sigmoid(_dot(yg, w_ref[...], _NN) + b_ref[...])).astype(BF16)

    row = pl.BlockSpec((tl, d), lambda i: (i, 0))
    return pl.pallas_call(
        body, name=name, grid=(l // tl,),
        in_specs=[row, pl.BlockSpec((d, d), lambda i: (0, 0)), pl.BlockSpec((1, d), lambda i: (0, 0)), _ANY],
        out_specs=pl.BlockSpec((tl, d), lambda i: (i, 1)), out_shape=jax.ShapeDtypeStruct(cat.shape, cat.dtype),
        input_output_aliases={3: 0}, compiler_params=_PAR)(ypre, wg, bg, cat)


def _glu_bwd(dcat, ypre, wg, bg, *, name):
    l, d = ypre.shape
    tl = _pick_tile(l, 512, SUBLANE)

    def body(dy_ref, y_ref, w_ref, b_ref, dyp_ref, dw_ref, db_ref):
        @pl.when(pl.program_id(0) == 0)
        def _():
            dw_ref[...] = jnp.zeros_like(dw_ref)
            db_ref[...] = jnp.zeros_like(db_ref)
        yp = y_ref[...]
        dyb = dy_ref[...]
        yg = _gelu(yp)
        sg = _sigmoid(_dot(yg, w_ref[...], _NN) + b_ref[...])
        dz = dyb * yg * sg * (1.0 - sg)
        dyg = dyb * sg + _dot(dz, w_ref[...], _NT)
        dyp_ref[...] = dyg * _gelu_grad(yp)
        dw_ref[...] += _dot(yg, dz, _TN)
        db_ref[...] += jnp.sum(dz, axis=0, keepdims=True)

    row = pl.BlockSpec((tl, d), lambda i: (i, 0))
    mat = pl.BlockSpec((d, d), lambda i: (0, 0))
    vec = pl.BlockSpec((1, d), lambda i: (0, 0))
    return pl.pallas_call(
        body, name=name, grid=(l // tl,),
        in_specs=[pl.BlockSpec((tl, d), lambda i: (i, 1)), row, mat, vec], out_specs=[row, mat, vec],
        out_shape=[jax.ShapeDtypeStruct((l, d), F32), jax.ShapeDtypeStruct((d, d), F32),
                   jax.ShapeDtypeStruct((1, d), F32)],
        compiler_params=_ARB)(dcat, ypre, wg, bg)


def _window_sum(x, w, trailing):
    s, d = x, 1
    while d < w:
        s = s + (_shift_dn(s, d) if trailing else _shift_up(s, d))
        d *= 2
    return s


def _window_count(shape, w):
    row = lax.broadcasted_iota(jnp.int32, shape, 0)
    return jnp.minimum(row + 1, w).astype(F32)


def _pool_fwd(proj, pw, scale, *, name):
    l = proj.shape[0]
    ng = len(POOL_WINDOWS)

    def body(z_ref, w_ref, sc_ref, o_ref):
        z = z_ref[...]
        for k, w in enumerate(POOL_WINDOWS):
            @pl.when(pl.program_id(0) == k)
            def _():
                pooled = _window_sum(z, w, True) / _window_count(z.shape, w) - z
                o_ref[...] = (_dot(pooled, w_ref[0], _NN) * sc_ref[...]).astype(BF16)

    col = pl.BlockSpec((l, LANE), lambda g: (0, g))
    return pl.pallas_call(
        body, name=name, grid=(ng,),
        in_specs=[col, pl.BlockSpec((1, LANE, LANE), lambda g: (g, 0, 0)), pl.BlockSpec((1, LANE), lambda g: (0, g))],
        out_specs=col, out_shape=jax.ShapeDtypeStruct((l, 2 * D_HALF), BF16), compiler_params=_PAR)(proj, pw, scale)


def _pool_bwd(proj, dcat, pw, scale, *, name):
    l = proj.shape[0]
    ng = len(POOL_WINDOWS)

    def body(z_ref, dy_ref, w_ref, sc_ref, dz_ref, dw_ref, dsc_ref):
        z = z_ref[...]
        dy = dy_ref[...]
        for k, w in enumerate(POOL_WINDOWS):
            @pl.when(pl.program_id(0) == k)
            def _():
                cnt = _window_count(z.shape, w)
                pooled = _window_sum(z, w, True) / cnt - z
                ypre = _dot(pooled, w_ref[0], _NN)
                dsc_ref[...] = jnp.sum(dy * ypre, axis=0, keepdims=True)
                dyp = dy * sc_ref[...]
                dw_ref[0] = _dot(pooled, dyp, _TN)
                dpool = _dot(dyp, w_ref[0], _NT)
                dz_ref[...] = (_window_sum(dpool / cnt, w, False) - dpool).astype(BF16)

    col = pl.BlockSpec((l, LANE), lambda g: (0, g))
    mat = pl.BlockSpec((1, LANE, LANE), lambda g: (g, 0, 0))
    vec = pl.BlockSpec((1, LANE), lambda g: (0, g))
    return pl.pallas_call(
        body, name=name, grid=(ng,), in_specs=[col, col, mat, vec], out_specs=[col, mat, vec],
        out_shape=[jax.ShapeDtypeStruct((l, D_HALF), BF16), jax.ShapeDtypeStruct((ng, LANE, LANE), F32),
                   jax.ShapeDtypeStruct((1, D_HALF), F32)],
        compiler_params=_PAR)(proj, dcat, pw, scale)


def _tril_mask():
    r = lax.broadcasted_iota(jnp.int32, (CHUNK, CHUNK), 0)
    c = lax.broadcasted_iota(jnp.int32, (CHUNK, CHUNK), 1)
    return r >= c


def _sgu_fwd(proj, ng, sw, sb_t, cat, *, name):
    l = proj.shape[0]
    tl = _pick_tile(l, 512, CHUNK)

    def body(su_ref, sv_ref, g_ref, w_ref, b_ref, cat_ref, o_ref):
        su = _gelu(su_ref[...])
        sv = _gelu(sv_ref[...])
        r = lax.rsqrt(jnp.mean(sv * sv, axis=-1, keepdims=True) + EPS)
        v = sv * r * g_ref[...]
        mask = _tril_mask()
        for h in range(SGU_HEADS):
            wm = jnp.where(mask, w_ref[h], 0.0)
            cs = slice(h * LANE, (h + 1) * LANE)
            for n in range(tl // CHUNK):
                rs = slice(n * CHUNK, (n + 1) * CHUNK)
                mixed = _dot(wm, v[rs, cs], _NN) + b_ref[:, h:h + 1]
                o_ref[rs, cs] = (su[rs, cs] * mixed).astype(BF16)

    blk = lambda c: pl.BlockSpec((tl, D_HALF), lambda i: (i, c))
    return pl.pallas_call(
        body, name=name, grid=(l // tl,),
        in_specs=[blk(1), blk(2), pl.BlockSpec((1, D_HALF), lambda i: (0, 0)),
                  pl.BlockSpec((SGU_HEADS, CHUNK, CHUNK), lambda i: (0, 0, 0)),
                  pl.BlockSpec((CHUNK, SGU_HEADS), lambda i: (0, 0)), _ANY],
        out_specs=blk(1), out_shape=jax.ShapeDtypeStruct(cat.shape, cat.dtype), input_output_aliases={5: 0},
        compiler_params=_PAR)(proj, proj, ng, sw, sb_t, cat)


def _sgu_bwd(proj, dcat, ng, sw, sb_t, *, name):
    l = proj.shape[0]
    tl = _pick_tile(l, 512, CHUNK)

    def body(su_ref, sv_ref, dy_ref, g_ref, w_ref, b_ref, dsu_ref, dsv_ref, dw_ref, dbm_ref, dng_ref, dv_scr):
        @pl.when(pl.program_id(0) == 0)
        def _():
            dw_ref[...] = jnp.zeros_like(dw_ref)
            dbm_ref[...] = jnp.zeros_like(dbm_ref)
            dng_ref[...] = jnp.zeros_like(dng_ref)
        su_pre = su_ref[...]
        sv_pre = sv_ref[...]
        su = _gelu(su_pre)
        sv = _gelu(sv_pre)
        gsu = _gelu_grad(su_pre)
        gv = g_ref[...]
        r = lax.rsqrt(jnp.mean(sv * sv, axis=-1, keepdims=True) + EPS)
        v = sv * r * gv
        dy = dy_ref[...]
        mask = _tril_mask()
        for h in range(SGU_HEADS):
            wm = jnp.where(mask, w_ref[h], 0.0)
            cs = slice(h * LANE, (h + 1) * LANE)
            dw_acc = jnp.zeros((CHUNK, CHUNK), F32)
            db_acc = jnp.zeros((CHUNK, LANE), F32)
            for n in range(tl // CHUNK):
                rs = slice(n * CHUNK, (n + 1) * CHUNK)
                vb = v[rs, cs]
                mixed = _dot(wm, vb, _NN) + b_ref[:, h:h + 1]
                dyb = dy[rs, cs]
                dsu_ref[rs, cs] = (dyb * mixed * gsu[rs, cs]).astype(BF16)
                dmix = dyb * su[rs, cs]
                db_acc = db_acc + dmix
                dw_acc = dw_acc + _dot(dmix, vb, _NT)
                dv_scr[rs, cs] = _dot(wm, dmix, _TN)
            dw_ref[h] += jnp.where(mask, dw_acc, 0.0)
            dbm_ref[h] += db_acc
        dv = dv_scr[...]
        a = dv * gv
        m = jnp.mean(a * sv, axis=-1, keepdims=True)
        dsv = r * a - sv * (r * r * r) * m
        dng_ref[...] += jnp.sum(dv * sv * r, axis=0, keepdims=True)
        dsv_ref[...] = (dsv * _gelu_grad(sv_pre)).astype(BF16)

    blk = lambda c: pl.BlockSpec((tl, D_HALF), lambda i: (i, c))
    mats = pl.BlockSpec((SGU_HEADS, CHUNK, CHUNK), lambda i: (0, 0, 0))
    vec = pl.BlockSpec((1, D_HALF), lambda i: (0, 0))
    piece = jax.ShapeDtypeStruct((l, D_HALF), BF16)
    mshape = jax.ShapeDtypeStruct((SGU_HEADS, CHUNK, CHUNK), F32)
    return pl.pallas_call(
        body, name=name, grid=(l // tl,),
        in_specs=[blk(1), blk(2), blk(1), vec, mats, pl.BlockSpec((CHUNK, SGU_HEADS), lambda i: (0, 0))],
        out_specs=[blk(0), blk(0), mats, mats, vec],
        out_shape=[piece, piece, mshape, mshape, jax.ShapeDtypeStruct((1, D_HALF), F32)],
        scratch_shapes=[pltpu.VMEM((tl, D_HALF), F32)],
        compiler_params=_ARB)(proj, proj, dcat, ng, sw, sb_t)


def _s5_params(w):
    prep_args = (w["ssm_log_step"], w["ssm_a_re"], w["ssm_a_im"], w["ssm_b_re"], w["ssm_b_im"])
    (lr, li, br, bi), prep_vjp = jax.vjp(jax.vmap(_s5_prep), *prep_args)
    wb, wc = jax.vmap(_s5_mats)(br, bi, w["ssm_c_re"], w["ssm_c_im"])
    consts = lambda reverse: jax.vmap(functools.partial(_s5_scan_consts, reverse=reverse))(*prep_args[:3])
    return dict(wb=wb.astype(BF16), wc=wc.astype(BF16), d=w["ssm_d"][:, None, :], sc=consts(False),
                sc_rev=consts(True), prep_vjp=prep_vjp)


def _s5_param_grads(s5, dwb, dwc, dlam, dd):
    dbr, dbi, dcr, dci = jax.vmap(_s5_mats_bwd)(dwb, dwc)
    n = dlam.shape[0]
    dlr = dlam[:, :, 0, :S5_STATE_LANES].reshape(n, N_SSM_GROUPS, SSM_STATE)
    dli = dlam[:, :, 0, S5_STATE_LANES:].reshape(n, N_SSM_GROUPS, SSM_STATE)
    dls, dar, dai, db_re, db_im = s5["prep_vjp"]((dlr, dli, dbr, dbi))
    return dict(ssm_log_step=dls, ssm_a_re=dar, ssm_a_im=dai, ssm_b_re=db_re, ssm_b_im=db_im, ssm_c_re=dcr,
                ssm_c_im=dci, ssm_d=dd[:, 0, :])


TILES = {
    "mm_up": (4096, 512), "mm_up_dw": (1024, 1408), "mm_down_dx": (1024, 2816),
    "mm_down": (512, 1024), "mm_down_dw": (256, 1024), "mm_even_in": (2048, 1024), "mm_odd_in": (2048, 768),
    "mm_mix_out": (1024, 1024), "mm_mix_out_dx": (2048, 1024), "mm_mix_out_dw": (1024, 512),
    "mm_even_in_dw": (1024, 512), "mm_odd_in_dw": (1024, 512),
}


def _layer_weights(i):
    j = i // 2
    mixer = [("even_w_in", j), ("even_w_out", j), ("ssm_glu_w", j)] if i % 2 == 0 else [("odd_w_in", j),
                                                                                         ("odd_w_out", j)]
    return dict(w_in=mixer[0], w_out=mixer[1], glu=mixer[2:], up=("ffn_w_up", i), down=("ffn_w_down", i))


class _LocalWeights:
    def __init__(self, w):
        self.w, self.grads = w, {}

    def carried_by(self, stage, i):
        return []

    def delivered(self, stage, i, outs):
        pass

    def weight(self, key):
        return self.w[key[0]][key[1]]

    def grad(self, key, dw):
        self.grads[key] = dw


class _ShardedWeights:
    def __init__(self, shards):
        self.shards = shards
        self.full, self.halves, self.pending, self.scattered = {}, {}, {}, {}

    def start(self, others):
        keys = [_layer_weights(0)["w_in"]]
        outs = _exchange_only(others + [self._gather(k) for k in keys], name="ag_first")
        self._take(keys, outs[len(others):])
        return outs[:len(others)]

    def _gather(self, key):
        shard = self.shards[key[0]][key[1]]
        if len(key) == 2:
            return _Gather(shard)
        rows = shard.shape[0] // 2
        buf = self.halves.get(key[:2])
        if buf is None:
            buf = lax.empty((N_DEV,) + shard.shape, shard.dtype)
        return _Gather(shard[key[2] * rows:(key[2] + 1) * rows], into=buf, row0=key[2] * rows)

    def _take(self, keys, outs):
        for key, got in zip(keys, outs):
            if len(key) == 3 and key[:2] not in self.halves:
                self.halves[key[:2]] = got
                continue
            if BIG[key[0]] == 2:
                self.full[key[:2]] = jnp.swapaxes(got, 0, 1).reshape(got.shape[1], -1)
            else:
                self.full[key[:2]] = got.reshape(-1, got.shape[2])

    def _plan(self, stage, i):
        cur = _layer_weights(i)
        nxt = _layer_weights(i + 1) if i + 1 < DEPTH else None
        has_scan = lambda k: k % 2 == 0
        none = ([], [])
        up_half = lambda h: ([(*nxt["up"], h)], []) if nxt and not has_scan(i + 1) else none
        return {
            "mm_in": ([cur["w_out"], *cur["glu"]], []) if i == 0 else none,
            "s5_fwd": ([cur["up"]], []),
            "mm_up": ([cur["down"]] + ([nxt["w_in"], nxt["w_out"], *nxt["glu"]] if nxt else []), []),
            "ffn_act": up_half(0),
            "mm_down": up_half(1),
            "ffn_act_bwd": ([], [cur["down"]] + ([] if not nxt else [nxt["w_in"]] if not has_scan(i + 1)
                                                 else [nxt["w_out"], *nxt["glu"]])),
            "mm_up_dx": none if not nxt else ([], [(*nxt["up"], 0)]) if not has_scan(i + 1) else ([], [nxt["w_in"]]),
            "mm_up_dw": ([], [(*nxt["up"], 1)]) if nxt and not has_scan(i + 1) else none,
            "s5_bwd": ([], [cur["up"]] + ([nxt["w_out"]] if nxt else [])),
            "mm_in_dw": ([], [cur["w_out"], *cur["glu"]]) if i == 0 else none,
            "mm_in_dx": ([], [cur["w_in"]]) if i == 0 else none,
        }[stage]

    def _scatter(self, key):
        name, layer = key[:2]
        layers, rows, cols = self.shards[name].shape
        if len(key) == 3:
            src = self.pending[key[:2]] if key[2] == 0 else self.pending.pop(key[:2])
            src = src[:, key[2] * (rows // 2):(key[2] + 1) * (rows // 2)]
            row0 = layer * rows + key[2] * (rows // 2)
        else:
            src, row0 = self.pending.pop(key), layer * rows
        if name not in self.scattered:
            self.scattered[name] = lax.empty((N_DEV, layers * rows, cols), src.dtype)
        return _Scatter(src, into=self.scattered[name], row0=row0)

    def carried_by(self, stage, i):
        gather, scatter = self._plan(stage, i)
        return [self._gather(k) for k in gather] + [self._scatter(k) for k in scatter]

    def delivered(self, stage, i, outs):
        gather, scatter = self._plan(stage, i)
        self._take(gather, outs[:len(gather)])
        for key, buf in zip(scatter, outs[len(gather):]):
            self.scattered[key[0]] = buf

    def weight(self, key):
        return self.full[key]

    def grad(self, key, dw):
        self.pending[key] = _to_dest_major(dw, BIG[key[0]] - 1).astype(BF16)

    def finish(self, carrier, others):
        keys = list(self.pending)
        res, outs = carrier(others + [self._scatter(k) for k in keys])
        for (name, _), buf in zip(keys, outs[len(others):]):
            self.scattered[name] = buf
        return res, outs[:len(others)]


def _device_step(x, tgt, w, comm):
    saved = []
    s5 = _s5_params(w)
    h = _rms_fwd(x, w["norm_mix_g"][0:1], name="rms_fwd")
    for i in range(DEPTH):
        j = i // 2
        lw = _layer_weights(i)
        if i % 2 == 0:
            proj = _carry(comm, "mm_in", i, _mm, h, comm.weight(lw["w_in"]), "nn", name="mm_even_in")
            ya = _sconv_fwd(proj, w["even_conv_w"][j], name="sconv_fwd")
            states, ypre = _carry(comm, "s5_fwd", i, _s5_fwd, proj, s5["wb"][j], s5["wc"][j], s5["d"][j],
                                  s5["sc"][j], name="s5_fwd")
            cat = _glu_fwd(ypre, comm.weight(lw["glu"][0]), w["ssm_glu_b"][j][None, :], ya, name="glu_fwd")
            mix = (states, ypre)
        else:
            proj = _carry(comm, "mm_in", i, _mm, h, comm.weight(lw["w_in"]), "nn", name="mm_odd_in")
            yc = _pool_fwd(proj, w["pool_w"][j], w["pool_scale"][j][None, :], name="pool_fwd")
            sb_t = jnp.transpose(w["sgu_b"][j])
            cat = _sgu_fwd(proj, w["sgu_norm_g"][j][None, :], w["sgu_w"][j], sb_t, yc, name="sgu_fwd")
            mix = (sb_t,)
        x1, h2 = _mm(cat, comm.weight(lw["w_out"]), "nn", add=x, norm=(w["norm_ffn_g"], i), name="mm_mix_out")
        up = _carry(comm, "mm_up", i, _mm, h2, comm.weight(lw["up"]), "nn", name="mm_up")
        act, gv = _carry(comm, "ffn_act", i, _ffn_act, up, w["ffn_conv_w"][i], w["ffn_conv_b"][i:i + 1],
                         name="ffn_act")
        if i + 1 < DEPTH:
            x2, h_next = _carry(comm, "mm_down", i, _mm, act, comm.weight(lw["down"]), "nn", add=x1,
                                norm=(w["norm_mix_g"], i + 1), name="mm_down")
        else:
            x2, h_next = _mm(act, comm.weight(lw["down"]), "nn", add=x1, name="mm_down_last"), None
        saved.append((x, h, proj, cat, x1, h2, up, gv, act, mix))
        x, h = x2, h_next

    loss, dx, dxb, dgf = _loss_head(x, w["norm_final_g"][None, :], tgt, name="loss_head")
    per_layer = {}
    s5_grads = []

    def put(name, idx, val):
        per_layer.setdefault(name, {})[idx] = val

    for i in reversed(range(DEPTH)):
        j = i // 2
        lw = _layer_weights(i)
        x0, h, proj, cat, x1, h2, up, gv, act, mix = saved[i]
        dact = _mm(dxb, comm.weight(lw["down"]), "nt", name="mm_down_dx")
        comm.grad(lw["down"], _mm(act, dxb, "tn", out_dtype=BF16, name="mm_down_dw"))
        dup, dcw, dcb = _carry(comm, "ffn_act_bwd", i, _ffn_act_bwd, up, gv, dact, w["ffn_conv_w"][i],
                               name="ffn_act_bwd")
        put("ffn_conv_w", i, dcw)
        put("ffn_conv_b", i, dcb[0])
        dx1, dx1b, dg2 = _carry(comm, "mm_up_dx", i, _mm, dup, comm.weight(lw["up"]), "nt", tm_cap=512, tn_cap=D_MODEL,
                                norm_bwd=(x1, w["norm_ffn_g"], i, dx), name="mm_up_dx")
        comm.grad(lw["up"], _carry(comm, "mm_up_dw", i, _mm, h2, dup, "tn", out_dtype=BF16,
                                   name="mm_up_dw"))
        put("norm_ffn_g", i, dg2[0])
        dcat = _mm(dx1b, comm.weight(lw["w_out"]), "nt", name="mm_mix_out_dx")
        comm.grad(lw["w_out"], _mm(cat, dx1b, "tn", out_dtype=BF16, name="mm_mix_out_dw"))
        if i % 2 == 0:
            states, ypre = mix
            dxa, dba, dca, dcw_a = _sconv_bwd(proj, dcat, w["even_conv_w"][j], name="sconv_bwd")
            put("even_conv_w", j, dcw_a)
            dypre, dwg, dbg = _glu_bwd(dcat, ypre, comm.weight(lw["glu"][0]), w["ssm_glu_b"][j][None, :],
                                       name="glu_bwd")
            comm.grad(lw["glu"][0], dwg)
            put("ssm_glu_b", j, dbg[0])
            du, dwb, dwc, dlam, dd = _carry(comm, "s5_bwd", i, _s5_bwd, proj, dypre, states, s5["wb"][j], s5["wc"][j],
                                            s5["d"][j], s5["sc_rev"][j], name="s5_bwd")
            s5_grads.insert(0, (dwb, dwc, dlam, dd))
            dproj = jnp.concatenate([dxa, dba, dca, du], axis=1)
            in_name = "mm_even_in"
        else:
            (sb_t,) = mix
            dz, dpw, dps = _pool_bwd(proj, dcat, w["pool_w"][j], w["pool_scale"][j][None, :], name="pool_bwd")
            put("pool_w", j, dpw)
            put("pool_scale", j, dps[0])
            dsu, dsv, dsw, dbm, dng = _sgu_bwd(proj, dcat, w["sgu_norm_g"][j][None, :], w["sgu_w"][j], sb_t,
                                               name="sgu_bwd")
            put("sgu_w", j, dsw)
            put("sgu_b", j, jnp.sum(dbm, axis=-1))
            put("sgu_norm_g", j, dng[0])
            dproj = jnp.concatenate([dz, dsu, dsv], axis=1)
            in_name = "mm_odd_in"
        comm.grad(lw["w_in"], _carry(comm, "mm_in_dw", i, _mm, h, dproj, "tn", out_dtype=BF16, name=in_name + "_dw"))
        dx, dxb, dg1 = _carry(comm, "mm_in_dx", i, _mm, dproj, comm.weight(lw["w_in"]), "nt", tn_cap=D_MODEL,
                              norm_bwd=(x0, w["norm_mix_g"], i, dx1), name=in_name + "_dx")
        put("norm_mix_g", i, dg1[0])

    grads = {nm: [vals[k] for k in sorted(vals)] for nm, vals in per_layer.items()}
    grads.update({nm: jnp.stack(grads[nm]) for nm in SMALL})
    grads["norm_final_g"] = dgf[0]
    grads.update(_s5_param_grads(s5, *[jnp.stack(parts) for parts in zip(*s5_grads)]))
    return loss, dx, grads


def _carry(comm, stage, i, fn, *args, **kwargs):
    exchanges = comm.carried_by(stage, i)
    if not exchanges:
        return fn(*args, **kwargs)
    out, moved = fn(*args, exchanges=exchanges, **kwargs)
    comm.delivered(stage, i, moved)
    return out


def _sum_parts(parts, *, name):
    g, r, c = parts.shape
    tr = _pick_tile(r, max(16, EXCHANGE_BLOCK_ELEMS // c), 16)

    def body(p_ref, o_ref):
        acc = p_ref[0].astype(F32)
        for k in range(1, g):
            acc = acc + p_ref[k].astype(F32)
        o_ref[...] = acc

    return pl.pallas_call(
        body, name=name, grid=(r // tr,), in_specs=[pl.BlockSpec((g, tr, c), lambda i: (0, i, 0))],
        out_specs=pl.BlockSpec((tr, c), lambda i: (i, 0)), out_shape=jax.ShapeDtypeStruct((r, c), F32),
        compiler_params=_PAR)(parts)


def _adamw(w, m, v, g_parts, *, name, exchanges=()):
    r, c = w.shape
    g = g_parts.shape[0]
    tc = _pick_tile(c, 8192, LANE)
    tr = _pick_tile(r, max(16, (1 << 18) // tc), 16)
    c1 = 1.0 - ADAM_B1 ** ADAM_STEP
    c2 = 1.0 - ADAM_B2 ** ADAM_STEP

    def body(w_ref, m_ref, v_ref, g_ref, go_ref, d_ref, mo_ref, vo_ref):
        grad = g_ref[0].astype(F32)
        for k in range(1, g):
            grad = grad + g_ref[k].astype(F32)
        m_new = ADAM_B1 * m_ref[...] + (1.0 - ADAM_B1) * grad
        v_new = ADAM_B2 * v_ref[...] + (1.0 - ADAM_B2) * (grad * grad)
        go_ref[...] = grad
        mo_ref[...] = m_new
        vo_ref[...] = v_new
        d_ref[...] = -ADAM_LR * ((m_new / c1) / (jnp.sqrt(v_new / c2) + ADAM_EPS) + ADAM_WD * w_ref[...])

    blk = pl.BlockSpec((tr, tc), lambda i, j: (i, j))
    out = jax.ShapeDtypeStruct((r, c), F32)
    return _call(body, name=name, grid=(r // tr, c // tc),
                 in_specs=[blk, blk, blk, pl.BlockSpec((g, tr, tc), lambda i, j: (0, i, j))],
                 out_specs=[blk, blk, blk, blk], out_shape=[out, out, out, out], args=[w, m, v, g_parts],
                 exchanges=exchanges)


WEIGHT_NAMES = ['norm_mix_g', 'even_w_in', 'even_conv_w', 'ssm_log_step', 'ssm_a_re', 'ssm_a_im', 'ssm_b_re',
                'ssm_b_im', 'ssm_c_re', 'ssm_c_im', 'ssm_d', 'ssm_glu_w', 'ssm_glu_b', 'even_w_out', 'odd_w_in',
                'pool_w', 'pool_scale', 'sgu_norm_g', 'sgu_w', 'sgu_b', 'odd_w_out', 'norm_ffn_g', 'ffn_w_up',
                'ffn_conv_w', 'ffn_conv_b', 'ffn_w_down', 'norm_final_g']
BIG = {'even_w_in': 2, 'ssm_glu_w': 1, 'even_w_out': 1, 'odd_w_in': 2, 'odd_w_out': 1, 'ffn_w_up': 2,
       'ffn_w_down': 1}
SMALL = {'even_conv_w': 2, 'pool_scale': 1, 'sgu_norm_g': 1, 'ffn_conv_w': 2}
SMALL_ROWS = 16


def _pad_to(n, q):
    return -(-n // q) * q


def _pack(arrays, dtype, rows, lead=()):
    flat = [a.reshape(lead + (-1,)).astype(dtype) for a in arrays]
    n = sum(f.shape[-1] for f in flat)
    pad = _pad_to(n, rows * LANE) - n
    if pad:
        flat.append(jnp.zeros(lead + (pad,), dtype))
    return jnp.concatenate(flat, axis=-1).reshape(lead + (rows, -1))


def _unpack(buf, shapes, lead=()):
    flat = buf.reshape(lead + (-1,))
    out, off = [], 0
    for shp in shapes:
        n = math.prod(shp)
        out.append(flat[..., off:off + n].reshape(lead + tuple(shp)))
        off += n
    return out


def _to_dest_major(full, axis):
    shp = full.shape
    split = full.reshape(shp[:axis] + (N_DEV, shp[axis] // N_DEV) + shp[axis + 1:])
    return jnp.moveaxis(split, axis, 0)


def _from_dest_major(blocks, axis):
    moved = jnp.moveaxis(blocks, 0, axis)
    shp = moved.shape
    return moved.reshape(shp[:axis] + (shp[axis] * shp[axis + 1],) + shp[axis + 2:])


def _rows_2d(a):
    return a.reshape(-1, a.shape[-1])


def kernel(x, norm_mix_g, even_w_in, even_conv_w, ssm_log_step, ssm_a_re, ssm_a_im, ssm_b_re, ssm_b_im, ssm_c_re, ssm_c_im, ssm_d, ssm_glu_w, ssm_glu_b, even_w_out, odd_w_in, pool_w, pool_scale, sgu_norm_g, sgu_w, sgu_b, odd_w_out, norm_ffn_g, ffn_w_up, ffn_conv_w, ffn_conv_b, ffn_w_down, norm_final_g, loss_target, m_norm_mix_g, m_even_w_in, m_even_conv_w, m_ssm_log_step, m_ssm_a_re, m_ssm_a_im, m_ssm_b_re, m_ssm_b_im, m_ssm_c_re, m_ssm_c_im, m_ssm_d, m_ssm_glu_w, m_ssm_glu_b, m_even_w_out, m_odd_w_in, m_pool_w, m_pool_scale, m_sgu_norm_g, m_sgu_w, m_sgu_b, m_odd_w_out, m_norm_ffn_g, m_ffn_w_up, m_ffn_conv_w, m_ffn_conv_b, m_ffn_w_down, m_norm_final_g, v_norm_mix_g, v_even_w_in, v_even_conv_w, v_ssm_log_step, v_ssm_a_re, v_ssm_a_im, v_ssm_b_re, v_ssm_b_im, v_ssm_c_re, v_ssm_c_im, v_ssm_d, v_ssm_glu_w, v_ssm_glu_b, v_even_w_out, v_odd_w_in, v_pool_w, v_pool_scale, v_sgu_norm_g, v_sgu_w, v_sgu_b, v_odd_w_out, v_norm_ffn_g, v_ffn_w_up, v_ffn_conv_w, v_ffn_conv_b, v_ffn_w_down, v_norm_final_g):
    given = dict(locals())
    wts = {n: given[n] for n in WEIGHT_NAMES}
    mom = {n: given["m_" + n] for n in WEIGHT_NAMES}
    var = {n: given["v_" + n] for n in WEIGHT_NAMES}
    repl = [n for n in WEIGHT_NAMES if n not in BIG and n not in SMALL]

    small_shapes = [wts[n].shape for n in SMALL]
    comm = _ShardedWeights({n: wts[n].astype(BF16) for n in BIG})
    (small_all,) = comm.start([_Gather(_pack([wts[n] for n in SMALL], F32, SMALL_ROWS))])
    full = {n: wts[n] for n in repl}
    for n, blocks in zip(SMALL, _unpack(small_all, small_shapes, lead=(N_DEV,))):
        full[n] = _from_dest_major(blocks, SMALL[n])

    loss, dx, grads = _device_step(x[0], loss_target[0], full, comm)

    repl_shapes = [wts[n].shape for n in repl]
    pieces = [p.reshape(-1) for n in repl for p in (grads[n] if isinstance(grads[n], list) else [grads[n]])]
    repl_flat = jnp.concatenate(pieces + [loss.reshape(-1)])
    n_repl = repl_flat.shape[0]
    chunk = _pad_to(-(-n_repl // N_DEV), SMALL_ROWS * LANE)
    repl_flat = jnp.pad(repl_flat, (0, N_DEV * chunk - n_repl))
    small_part = _pack([_to_dest_major(grads[n], SMALL[n]) for n in SMALL], F32, SMALL_ROWS, lead=(N_DEV,))
    small_cols = small_part.shape[2]
    small_scatter = _Scatter(
        jnp.concatenate([small_part, repl_flat.reshape(N_DEV, SMALL_ROWS, chunk // SMALL_ROWS)], axis=2))

    _, (small_rs,) = comm.finish(lambda exchanges: (None, _exchange_only(exchanges, name="rs_last")), [small_scatter])
    small_sum = _sum_parts(small_rs, name="rs_sum_small")
    (repl_all,) = _exchange_only([_Gather(small_sum[:, small_cols:])], name="ag_repl")
    repl_sum = repl_all.reshape(-1)
    total_loss = repl_sum[n_repl - 1]

    out = {}
    for n in BIG:
        res = _adamw(_rows_2d(wts[n]), _rows_2d(mom[n]), _rows_2d(var[n]), comm.scattered[n], name="adamw_" + n)
        out[n] = [r.reshape(wts[n].shape) for r in res]

    def small_vec(shard_part, repl_part):
        flat = jnp.concatenate([shard_part.reshape(-1), repl_part])
        return flat.reshape(SMALL_ROWS, -1)

    def small_tree(tree):
        tail = jnp.concatenate([tree[n].reshape(-1) for n in repl])
        tail = jnp.pad(tail, (0, N_DEV * chunk - tail.shape[0]))
        return small_vec(_pack([tree[n] for n in SMALL], F32, SMALL_ROWS), tail)

    res = _adamw(small_tree(wts), small_tree(mom), small_tree(var), small_vec(small_sum[:, :small_cols], repl_sum)[None],
                 name="adamw_small")
    n_small = SMALL_ROWS * small_cols
    for k, r in enumerate(res):
        flat = r.reshape(-1)
        shard = _unpack(flat[:n_small], small_shapes)
        rest = _unpack(flat[n_small:], repl_shapes)
        for n, val in zip(SMALL, shard):
            out.setdefault(n, [None] * 4)[k] = val
        for n, val in zip(repl, rest):
            out.setdefault(n, [None] * 4)[k] = val

    grad_x = dx[None]
    return (total_loss, grad_x, *[out[n][0] for n in WEIGHT_NAMES], *[out[n][1] for n in WEIGHT_NAMES],
            *[out[n][2] for n in WEIGHT_NAMES], *[out[n][3] for n in WEIGHT_NAMES])
```

```python
import functools
import math

import jax
import jax.numpy as jnp
from jax import lax
from jax.experimental import pallas as pl
from jax.experimental.pallas import tpu as pltpu

F32 = jnp.float32
BF16 = jnp.bfloat16

D_MODEL = 1024
DEPTH = 4
D_HALF = D_MODEL // 2
SSM_GROUP = 16
N_SSM_GROUPS = D_HALF // SSM_GROUP
SSM_STATE = 64
POOL_WINDOWS = (2, 4, 8, 16)
SGU_HEADS = 4
CHUNK = 128
D_FF = 2816
CONV_WIDTH = 3
EPS = 1e-6
N_DEV = 8

ADAM_LR = 0.001
ADAM_B1 = 0.9
ADAM_B2 = 0.999
ADAM_EPS = 1e-08
ADAM_WD = 0.01
ADAM_STEP = 10

LANE = 128
SUBLANE = 8
S5_LANE_BLOCKS = D_HALF // LANE
S5_STATE_LANES = (N_SSM_GROUPS // S5_LANE_BLOCKS) * SSM_STATE
S5_TIME_CHUNK = 1024
EXCHANGE_BLOCK_ELEMS = 1 << 20

GELU_K = math.sqrt(2.0 / math.pi)
GELU_C = 0.044715

_ARB = pltpu.CompilerParams(dimension_semantics=("arbitrary",))
_PAR = pltpu.CompilerParams(dimension_semantics=("parallel",))


def _pick_tile(n, cap, mult):
    if n <= cap:
        return n
    best = None
    for t in range(mult, cap + 1, mult):
        if n % t == 0:
            best = t
    assert best is not None, (n, cap, mult)
    return best


_MESH = pl.DeviceIdType.MESH
_ANY = pl.BlockSpec(memory_space=pl.ANY)
SEMS_PER_EXCHANGE = N_DEV - 1


class _Gather:
    def __init__(self, src, into=None, row0=0):
        self.src, self.into, self.row0 = src, into, row0
        self.out_shape = jax.ShapeDtypeStruct((N_DEV,) + src.shape if into is None else into.shape, src.dtype)

    def copies(self, x_ref, whole_ref, send_sems, recv_sems, local_sem):
        out_ref = whole_ref if self.into is None else whole_ref.at[:, pl.ds(self.row0, self.src.shape[0])]
        x, y, cc = lax.axis_index("x"), lax.axis_index("y"), lax.axis_index("c")
        me, sibling = (x, y, cc), (x, y, 1 - cc)
        chips = [(1 - x, y), (x, 1 - y), (1 - x, 1 - y)]

        def rows(px, py, pc):
            return out_ref.at[4 * px + 2 * py + pc]

        def copy(k, block, to, src=None):
            return pltpu.make_async_remote_copy(
                src_ref=rows(*block) if src is None else src, dst_ref=rows(*block),
                send_sem=send_sems.at[k], recv_sem=recv_sems.at[k], device_id=to, device_id_type=_MESH)

        return dict(
            mine=pltpu.make_async_copy(x_ref, rows(*me), local_sem),
            first=[copy(0, me, sibling, src=x_ref)] + [copy(1 + k, me, (*chip, cc), src=x_ref)
                                                       for k, chip in enumerate(chips)],
            passed=[copy(4 + k, (*chip, cc), sibling) for k, chip in enumerate(chips)],
            over_ici=[copy(1 + k, (*chip, cc), me) for k, chip in enumerate(chips)],
            from_sibling=[copy(0, sibling, me)] + [copy(4 + k, (*chip, 1 - cc), me) for k, chip in enumerate(chips)])

    def start(self, *refs):
        cps = self.copies(*refs)
        cps["mine"].start()
        for cp in cps["first"]:
            cp.start()

    def finish(self, *refs):
        cps = self.copies(*refs)
        for arrived, onward in zip(cps["over_ici"], cps["passed"]):
            arrived.wait_recv()
            onward.start()
        for arrived in cps["from_sibling"]:
            arrived.wait_recv()
        for cp in cps["first"] + cps["passed"]:
            cp.wait_send()
        cps["mine"].wait()


class _Scatter:
    def __init__(self, src, into=None, row0=0):
        self.src, self.into, self.row0 = src, into, row0
        whole = src if into is None else into
        self.out_shape = jax.ShapeDtypeStruct(whole.shape, whole.dtype)

    def copies(self, p_ref, whole_ref, send_sems, recv_sems, local_sem):
        x, y, cc = lax.axis_index("x"), lax.axis_index("y"), lax.axis_index("c")
        me = 4 * x + 2 * y + cc
        rows = self.src.shape[1]
        out_ref = whole_ref if self.into is None else whole_ref.at[:, pl.ds(self.row0, rows)]
        sends, arrivals = [], []
        for k in range(1, N_DEV):
            px = (1 - x) if k & 4 else x
            py = (1 - y) if k & 2 else y
            pc = (1 - cc) if k & 1 else cc
            peer = 4 * px + 2 * py + pc
            kw = dict(send_sem=send_sems.at[k - 1], recv_sem=recv_sems.at[k - 1], device_id=(px, py, pc),
                      device_id_type=_MESH)
            sends.append(pltpu.make_async_remote_copy(src_ref=p_ref.at[peer], dst_ref=out_ref.at[me], **kw))
            arrivals.append(pltpu.make_async_remote_copy(src_ref=p_ref.at[me], dst_ref=out_ref.at[peer], **kw))
        return dict(mine=pltpu.make_async_copy(p_ref.at[me], out_ref.at[me], local_sem), sends=sends,
                    arrivals=arrivals)

    def start(self, *refs):
        cps = self.copies(*refs)
        cps["mine"].start()
        for cp in cps["sends"]:
            cp.start()

    def finish(self, *refs):
        cps = self.copies(*refs)
        for cp in cps["arrivals"]:
            cp.wait_recv()
        for cp in cps["sends"]:
            cp.wait_send()
        cps["mine"].wait()


class _SemView:
    def __init__(self, ref, lo):
        self.ref, self.lo = ref, lo

    @property
    def at(self):
        return self

    def __getitem__(self, k):
        return self.ref.at[self.lo + k]


def _call(body, *, name, grid, in_specs, out_specs, out_shape, args, scratch_shapes=(), parallel=True, exchanges=()):
    n_axes = len(grid)
    if not exchanges:
        sem = ("parallel" if parallel else "arbitrary",) * n_axes
        return pl.pallas_call(
            body, name=name, grid=grid, in_specs=in_specs, out_specs=out_specs, out_shape=out_shape,
            scratch_shapes=scratch_shapes, compiler_params=pltpu.CompilerParams(dimension_semantics=sem))(*args)
    single = not isinstance(out_shape, (list, tuple))
    out_specs = [out_specs] if single else list(out_specs)
    out_shape = [out_shape] if single else list(out_shape)
    n_in, n_out, n_scr, n_x = len(in_specs), len(out_specs), len(scratch_shapes), len(exchanges)
    landing = [(e, ex.into) for e, ex in enumerate(exchanges) if ex.into is not None]
    aliases = {n_in + n_x + pos: n_out + e for pos, (e, _) in enumerate(landing)}

    def wrapped(*refs):
        ins, refs = refs[:n_in], refs[n_in:]
        x_in, refs = refs[:n_x], refs[n_x + len(landing):]
        outs, refs = refs[:n_out], refs[n_out:]
        x_out, refs = refs[:n_x], refs[n_x:]
        scr, (send_sems, recv_sems, local_sems) = refs[:n_scr], refs[n_scr:]
        ids = [pl.program_id(k) for k in range(n_axes)]
        first = functools.reduce(jnp.logical_and, [i == 0 for i in ids])
        last = functools.reduce(jnp.logical_and, [i == g - 1 for i, g in zip(ids, grid)])

        def sems(e):
            lo = e * SEMS_PER_EXCHANGE
            return _SemView(send_sems, lo), _SemView(recv_sems, lo), local_sems.at[e]

        @pl.when(first)
        def _():
            for e, ex in enumerate(exchanges):
                ex.start(x_in[e], x_out[e], *sems(e))

        body(*ins, *outs, *scr)

        @pl.when(last)
        def _():
            for e, ex in enumerate(exchanges):
                ex.finish(x_in[e], x_out[e], *sems(e))

    res = pl.pallas_call(
        wrapped, name=name, grid=grid, in_specs=list(in_specs) + [_ANY] * (n_x + len(landing)),
        out_specs=out_specs + [_ANY] * n_x, out_shape=out_shape + [ex.out_shape for ex in exchanges],
        input_output_aliases=aliases,
        scratch_shapes=list(scratch_shapes) + [pltpu.SemaphoreType.DMA((n_x * SEMS_PER_EXCHANGE,)),
                                               pltpu.SemaphoreType.DMA((n_x * SEMS_PER_EXCHANGE,)),
                                               pltpu.SemaphoreType.DMA((n_x,))],
        compiler_params=pltpu.CompilerParams(dimension_semantics=("arbitrary",) * n_axes),
    )(*args, *[ex.src for ex in exchanges], *[buf for _, buf in landing])
    outs, x_outs = res[:n_out], res[n_out:]
    return (outs[0] if single else outs), x_outs


def _exchange_only(exchanges, *, name):
    def body():
        pass

    return _call(body, name=name, grid=(1,), in_specs=[], out_specs=[], out_shape=[], args=[],
                 exchanges=exchanges)[1]


def _shift_dn(x, d):
    rolled = pltpu.roll(x, d, 0)
    if x.shape[0] <= SUBLANE or d >= SUBLANE:
        row = lax.broadcasted_iota(jnp.int32, x.shape, 0)
        return jnp.where(row >= d, rolled, 0.0)
    row = lax.broadcasted_iota(jnp.int32, (SUBLANE, x.shape[1]), 0)
    return jnp.concatenate([jnp.where(row >= d, rolled[:SUBLANE], 0.0), rolled[SUBLANE:]], axis=0)


def _shift_up(x, d):
    n = x.shape[0]
    rolled = pltpu.roll(x, n - d, 0)
    if n <= SUBLANE or d >= SUBLANE:
        row = lax.broadcasted_iota(jnp.int32, x.shape, 0)
        return jnp.where(row < n - d, rolled, 0.0)
    row = lax.broadcasted_iota(jnp.int32, (SUBLANE, x.shape[1]), 0)
    return jnp.concatenate([rolled[:n - SUBLANE], jnp.where(row < SUBLANE - d, rolled[n - SUBLANE:], 0.0)], axis=0)


def _gelu(x):
    return 0.5 * x * (1.0 + jnp.tanh(GELU_K * (x + GELU_C * x * x * x)))


def _gelu_grad(x):
    t = jnp.tanh(GELU_K * (x + GELU_C * x * x * x))
    return 0.5 * (1.0 + t) + 0.5 * x * (1.0 - t * t) * (GELU_K * (1.0 + 3.0 * GELU_C * x * x))


def _sigmoid(x):
    return 0.5 + 0.5 * jnp.tanh(0.5 * x)


def _conv3(x, w_ref):
    return w_ref[0:1, :] * _shift_dn(x, 2) + w_ref[1:2, :] * _shift_dn(x, 1) + w_ref[2:3, :] * x


def _dot(a, b, dims):
    return lax.dot_general(a.astype(BF16), b.astype(BF16), (dims, ((), ())), preferred_element_type=F32)


_NN = ((1,), (0,))
_NT = ((1,), (1,))
_TN = ((0,), (0,))


def _tiles(name, m, n, default):
    tm, tn = TILES.get(name, default)
    return math.gcd(tm, m), math.gcd(tn, n)


def _mm(a, b, mode, *, name, out_dtype=F32, add=None, norm=None, norm_bwd=None, tm_cap=512, tn_cap=1536,
        exchanges=()):
    halves = None
    if mode == "tn":
        r, m = a.shape
        n = b.shape[-1] * (2 if b.ndim == 3 else 1)
        tm, tn = _tiles(name, m, n, (_pick_tile(m, 256, LANE), _pick_tile(n, tn_cap, LANE)))
        if b.ndim == 3:
            per_half = b.shape[-1] // tn
            b_spec = pl.BlockSpec((None, r, tn), lambda i, j: (j // per_half, 0, j % per_half))
        else:
            b_spec = pl.BlockSpec((r, tn), lambda i, j: (0, j))
        in_specs = [pl.BlockSpec((r, tm), lambda i, j: (0, i)), b_spec]
        dims = _TN
    elif mode == "nt" and a.ndim == 3:
        _, m, halves = a.shape
        n = b.shape[0]
        tm, tn = _tiles(name, m, n, (_pick_tile(m, tm_cap, SUBLANE), _pick_tile(n, tn_cap, LANE)))
        in_specs = [pl.BlockSpec((2, tm, halves), lambda i, j: (0, i, 0)),
                    pl.BlockSpec((tn, 2 * halves), lambda i, j: (j, 0))]
        dims = _NT
    elif mode == "nn":
        m, k = a.shape
        n = b.shape[1]
        tm, tn = _tiles(name, m, n, (_pick_tile(m, tm_cap, SUBLANE), _pick_tile(n, tn_cap, LANE)))
        in_specs = [pl.BlockSpec((tm, k), lambda i, j: (i, 0)), pl.BlockSpec((k, tn), lambda i, j: (0, j))]
        dims = _NN
    else:
        m, k = a.shape
        n = b.shape[0]
        tm, tn = _tiles(name, m, n, (_pick_tile(m, tm_cap, SUBLANE), _pick_tile(n, tn_cap, LANE)))
        in_specs = [pl.BlockSpec((tm, k), lambda i, j: (i, 0)), pl.BlockSpec((tn, k), lambda i, j: (j, 0))]
        dims = _NT
    assert m % tm == 0 and n % tn == 0, (name, m, n, tm, tn)
    args = [a, b]
    tile = pl.BlockSpec((tm, tn), lambda i, j: (i, j))
    if add is not None:
        in_specs.append(tile)
        args.append(add)
    out_specs, out_shape = tile, jax.ShapeDtypeStruct((m, n), out_dtype)
    if norm is not None:
        gains, layer = norm
        assert tn == n
        in_specs.append(pl.BlockSpec((None, 1, n), lambda i, j: (layer, 0, 0)))
        args.append(gains.reshape(gains.shape[0], 1, n))
        out_specs, out_shape = [tile, tile], [out_shape, jax.ShapeDtypeStruct((m, n), BF16)]
    if norm_bwd is not None:
        x_in, gains, layer, res = norm_bwd
        assert tn == n and add is None and norm is None
        vec = pl.BlockSpec((None, 1, n), lambda i, j: (layer, 0, 0))
        in_specs += [tile, vec, tile]
        args += [x_in, gains.reshape(gains.shape[0], 1, n), res]
        out_specs = [tile, tile, pl.BlockSpec((1, n), lambda i, j: (0, 0))]
        out_shape = [jax.ShapeDtypeStruct((m, n), F32), jax.ShapeDtypeStruct((m, n), BF16),
                     jax.ShapeDtypeStruct((1, n), F32)]

    def body(*refs):
        if halves is None:
            acc = _dot(refs[0][...], refs[1][...], dims)
        else:
            acc = (_dot(refs[0][0], refs[1][:, :halves], dims) + _dot(refs[0][1], refs[1][:, halves:], dims))
        if add is not None:
            acc = acc + refs[2][...]
        if norm_bwd is not None:
            x_ref, g_ref, res_ref, dx_ref, dxb_ref, dg_ref = refs[2:]

            @pl.when(pl.program_id(0) == 0)
            def _():
                dg_ref[...] = jnp.zeros_like(dg_ref)
            dx, xn = _rms_bwd_rows(acc, x_ref[...], g_ref[...])
            dx = dx + res_ref[...]
            dx_ref[...] = dx
            dxb_ref[...] = dx.astype(BF16)
            dg_ref[...] += jnp.sum(acc * xn, axis=0, keepdims=True)
        elif norm is None:
            refs[-1][...] = acc.astype(out_dtype)
        else:
            refs[-2][...] = acc.astype(out_dtype)
            r = lax.rsqrt(jnp.mean(acc * acc, axis=-1, keepdims=True) + EPS)
            refs[-1][...] = (acc * r * refs[-3][...]).astype(BF16)

    return _call(body, name=name, grid=(m // tm, n // tn), in_specs=in_specs, out_specs=out_specs,
                 out_shape=out_shape, args=args, parallel=norm_bwd is None, exchanges=exchanges)


def _rms_fwd(x, g, *, name):
    l, d = x.shape
    tl = _pick_tile(l, 512, SUBLANE)

    def body(x_ref, g_ref, h_ref):
        xv = x_ref[...]
        r = lax.rsqrt(jnp.mean(xv * xv, axis=-1, keepdims=True) + EPS)
        h_ref[...] = (xv * r * g_ref[...]).astype(BF16)

    return pl.pallas_call(
        body, name=name, grid=(l // tl,),
        in_specs=[pl.BlockSpec((tl, d), lambda i: (i, 0)), pl.BlockSpec((1, d), lambda i: (0, 0))],
        out_specs=pl.BlockSpec((tl, d), lambda i: (i, 0)),
        out_shape=jax.ShapeDtypeStruct((l, d), BF16), compiler_params=_PAR)(x, g)


def _rms_bwd_rows(dh, xv, g):
    r = lax.rsqrt(jnp.mean(xv * xv, axis=-1, keepdims=True) + EPS)
    a = dh * g
    m = jnp.mean(a * xv, axis=-1, keepdims=True)
    return r * a - xv * (r * r * r) * m, xv * r


def _loss_head(x, g, tgt, *, name):
    l, d = x.shape
    tl = _pick_tile(l, 512, SUBLANE)

    def body(x_ref, g_ref, t_ref, loss_ref, dx_ref, dxb_ref, dg_ref):
        @pl.when(pl.program_id(0) == 0)
        def _():
            dg_ref[...] = jnp.zeros_like(dg_ref)
            loss_ref[...] = jnp.zeros_like(loss_ref)
        xv = x_ref[...]
        gv = g_ref[...]
        r = lax.rsqrt(jnp.mean(xv * xv, axis=-1, keepdims=True) + EPS)
        err = xv * r * gv - t_ref[...]
        row_loss = jnp.sum(err * err, axis=-1, keepdims=True) * (0.5 / d)
        loss_ref[...] += jnp.sum(row_loss, axis=0, keepdims=True)
        dy = err * (1.0 / d)
        dx, xn = _rms_bwd_rows(dy, xv, gv)
        dx_ref[...] = dx
        dxb_ref[...] = dx.astype(BF16)
        dg_ref[...] += jnp.sum(dy * xn, axis=0, keepdims=True)

    row = pl.BlockSpec((tl, d), lambda i: (i, 0))
    vec = pl.BlockSpec((1, d), lambda i: (0, 0))
    one = pl.BlockSpec((1, 1), lambda i: (0, 0))
    return pl.pallas_call(
        body, name=name, grid=(l // tl,), in_specs=[row, vec, row], out_specs=[one, row, row, vec],
        out_shape=[jax.ShapeDtypeStruct((1, 1), F32), jax.ShapeDtypeStruct((l, d), F32),
                   jax.ShapeDtypeStruct((l, d), BF16), jax.ShapeDtypeStruct((1, d), F32)],
        compiler_params=_ARB)(x, g, tgt)


def _ffn_act(up, cw, cb, *, name, exchanges=()):
    l = up.shape[0]
    nb = D_FF // LANE

    def body(ug_ref, uv_ref, wg_ref, wv_ref, bg_ref, bv_ref, o_ref, gv_ref):
        gc = _conv3(ug_ref[...], wg_ref) + bg_ref[...]
        vc = _conv3(uv_ref[...], wv_ref) + bv_ref[...]
        gv_ref[0] = gc
        gv_ref[1] = vc
        o_ref[...] = (gc * _sigmoid(gc) * vc).astype(BF16)

    col = lambda off: pl.BlockSpec((l, LANE), lambda j: (0, j + off))
    w3 = lambda off: pl.BlockSpec((CONV_WIDTH, LANE), lambda j: (0, j + off))
    b1 = lambda off: pl.BlockSpec((1, LANE), lambda j: (0, j + off))
    return _call(body, name=name, grid=(nb,), in_specs=[col(0), col(nb), w3(0), w3(nb), b1(0), b1(nb)],
                 out_specs=[col(0), pl.BlockSpec((2, l, LANE), lambda j: (0, 0, j))],
                 out_shape=[jax.ShapeDtypeStruct((l, D_FF), BF16), jax.ShapeDtypeStruct((2, l, D_FF), F32)],
                 args=[up, up, cw, cw, cb, cb], exchanges=exchanges)


def _ffn_act_bwd(up, gv, dact, cw, *, name, exchanges=()):
    l = up.shape[0]
    nb = D_FF // LANE

    def half_bwd(k, dc, x, w_ref, dup_ref, dcw_ref, dcb_ref):
        d1, d2 = _shift_up(dc, 1), _shift_up(dc, 2)
        dcb_ref[k] = jnp.sum(dc, axis=0, keepdims=True)
        dcw_ref[k] = jnp.concatenate([jnp.sum(d2 * x, axis=0, keepdims=True),
                                      jnp.sum(d1 * x, axis=0, keepdims=True),
                                      jnp.sum(dc * x, axis=0, keepdims=True)], axis=0)
        dup_ref[k] = (w_ref[2:3, :] * dc + w_ref[1:2, :] * d1 + w_ref[0:1, :] * d2).astype(BF16)

    def body(ug_ref, uv_ref, gv_ref, da_ref, wg_ref, wv_ref, dup_ref, dcw_ref, dcb_ref):
        gc, vc, da = gv_ref[0], gv_ref[1], da_ref[...]
        sg = _sigmoid(gc)
        half_bwd(0, da * vc * (sg * (1.0 + gc * (1.0 - sg))), ug_ref[...], wg_ref, dup_ref, dcw_ref, dcb_ref)
        half_bwd(1, da * (gc * sg), uv_ref[...], wv_ref, dup_ref, dcw_ref, dcb_ref)

    col = lambda off: pl.BlockSpec((l, LANE), lambda j: (0, j + off))
    w3 = lambda off: pl.BlockSpec((CONV_WIDTH, LANE), lambda j: (0, j + off))
    both = lambda rows: pl.BlockSpec((2, rows, LANE), lambda j: (0, 0, j))
    res = _call(
        body, name=name, grid=(nb,),
        in_specs=[col(0), col(nb), both(l), col(0), w3(0), w3(nb)],
        out_specs=[both(l), both(CONV_WIDTH), both(1)],
        out_shape=[jax.ShapeDtypeStruct((2, l, D_FF), BF16), jax.ShapeDtypeStruct((2, CONV_WIDTH, D_FF), F32),
                   jax.ShapeDtypeStruct((2, 1, D_FF), F32)],
        args=[up, up, gv, dact, cw, cw], exchanges=exchanges)
    (dup, dcw, dcb), moved = res if exchanges else (res, None)
    outs = [dup, jnp.concatenate([dcw[0], dcw[1]], axis=1), jnp.concatenate([dcb[0], dcb[1]], axis=1)]
    return (outs, moved) if exchanges else outs


def _sconv_fwd(proj, cw, *, name):
    l = proj.shape[0]
    nb = D_HALF // LANE

    def body(xa_ref, ba_ref, ca_ref, w_ref, o_ref):
        o_ref[...] = (ba_ref[...] * _conv3(ca_ref[...] * xa_ref[...], w_ref)).astype(BF16)

    col = lambda off: pl.BlockSpec((l, LANE), lambda j: (0, j + off))
    return pl.pallas_call(
        body, name=name, grid=(nb,),
        in_specs=[col(0), col(nb), col(2 * nb), pl.BlockSpec((CONV_WIDTH, LANE), lambda j: (0, j))],
        out_specs=col(0), out_shape=jax.ShapeDtypeStruct((l, 2 * D_HALF), BF16),
        compiler_params=_PAR)(proj, proj, proj, cw)


def _sconv_bwd(proj, dcat, cw, *, name):
    l = proj.shape[0]
    nb = D_HALF // LANE

    def body(xa_ref, ba_ref, ca_ref, dy_ref, w_ref, dxa_ref, dba_ref, dca_ref, dw_ref):
        xa, ba, ca, dy = xa_ref[...], ba_ref[...], ca_ref[...], dy_ref[...]
        q = ca * xa
        dba_ref[...] = (dy * _conv3(q, w_ref)).astype(BF16)
        dconv = dy * ba
        d1, d2 = _shift_up(dconv, 1), _shift_up(dconv, 2)
        dw_ref[...] = jnp.concatenate([jnp.sum(d2 * q, axis=0, keepdims=True), jnp.sum(d1 * q, axis=0, keepdims=True),
                                       jnp.sum(dconv * q, axis=0, keepdims=True)], axis=0)
        dq = w_ref[2:3, :] * dconv + w_ref[1:2, :] * d1 + w_ref[0:1, :] * d2
        dxa_ref[...] = (dq * ca).astype(BF16)
        dca_ref[...] = (dq * xa).astype(BF16)

    col = lambda off: pl.BlockSpec((l, LANE), lambda j: (0, j + off))
    w3 = pl.BlockSpec((CONV_WIDTH, LANE), lambda j: (0, j))
    piece = jax.ShapeDtypeStruct((l, D_HALF), BF16)
    return pl.pallas_call(
        body, name=name, grid=(nb,),
        in_specs=[col(0), col(nb), col(2 * nb), col(0), w3],
        out_specs=[col(0), col(0), col(0), w3],
        out_shape=[piece, piece, piece, jax.ShapeDtypeStruct((CONV_WIDTH, D_HALF), F32)],
        compiler_params=_PAR)(proj, proj, proj, dcat, cw)


def _s5_prep(log_step, a_re, a_im, b_re, b_im):
    step = jnp.exp(log_step)[:, None]
    mag = jnp.exp(a_re * step)
    lr = mag * jnp.cos(a_im * step)
    li = mag * jnp.sin(a_im * step)
    nr = lr - 1.0
    den = a_re * a_re + a_im * a_im
    qr = (nr * a_re + li * a_im) / den
    qi = (li * a_re - nr * a_im) / den
    br = qr[..., None] * b_re - qi[..., None] * b_im
    bi = qr[..., None] * b_im + qi[..., None] * b_re
    return lr, li, br, bi


def _block_diag(m):
    nb, ng, r, c = m.shape
    eye = jnp.eye(ng, dtype=m.dtype)
    return jnp.einsum("bgrc,gh->bgrhc", m, eye).reshape(nb, ng * r, ng * c)


def _block_diag_extract(w, r, c):
    nb = w.shape[0]
    ng = w.shape[1] // r
    w5 = w.reshape(nb, ng, r, ng, c)
    return jnp.einsum("bgrhc,gh->bgrc", w5, jnp.eye(ng, dtype=w.dtype))


def _s5_mats(br, bi, c_re, c_im):
    g8 = N_SSM_GROUPS // S5_LANE_BLOCKS
    to_blk = lambda m: m.reshape(S5_LANE_BLOCKS, g8, m.shape[1], m.shape[2])
    wb = jnp.concatenate([_block_diag(to_blk(jnp.swapaxes(br, 1, 2))),
                          _block_diag(to_blk(jnp.swapaxes(bi, 1, 2)))], axis=2)
    wc = jnp.concatenate([_block_diag(to_blk(jnp.swapaxes(c_re, 1, 2))),
                          _block_diag(to_blk(jnp.swapaxes(-c_im, 1, 2)))], axis=1)
    return wb, wc


def _s5_mats_bwd(dwb, dwc):
    g, p, h = N_SSM_GROUPS, SSM_STATE, SSM_GROUP
    half = S5_STATE_LANES
    dbr = jnp.swapaxes(_block_diag_extract(dwb[:, :, :half], h, p).reshape(g, h, p), 1, 2)
    dbi = jnp.swapaxes(_block_diag_extract(dwb[:, :, half:], h, p).reshape(g, h, p), 1, 2)
    dcr = jnp.swapaxes(_block_diag_extract(dwc[:, :half, :], p, h).reshape(g, p, h), 1, 2)
    dci = -jnp.swapaxes(_block_diag_extract(dwc[:, half:, :], p, h).reshape(g, p, h), 1, 2)
    return dbr, dbi, dcr, dci


def _s5_scan_consts(log_step, a_re, a_im, reverse):
    step = jnp.exp(log_step)[:, None]
    xr = (a_re * step).reshape(S5_LANE_BLOCKS, 1, S5_STATE_LANES)
    xi = (a_im * step).reshape(S5_LANE_BLOCKS, 1, S5_STATE_LANES)
    if reverse:
        xi = -xi
    row = jnp.arange(SUBLANE, dtype=F32).reshape(1, SUBLANE, 1)

    def power(n):
        mag = jnp.exp(n * xr)
        return jnp.concatenate([mag * jnp.cos(n * xi), mag * jnp.sin(n * xi)], axis=-1)

    kinds = []
    for d in (1, 2, 4):
        keep = (row <= SUBLANE - 1 - d) if reverse else (row >= d)
        kinds.append(jnp.where(keep, power(jnp.full_like(row, float(d))), 0.0))
    kinds.append(power((SUBLANE - row) if reverse else (row + 1.0)))
    return jnp.stack(kinds, axis=1)


def _scan_rows(s_ref, sc_ref, carry_ref, n_rows, reverse):
    n_grp = n_rows // SUBLANE
    n_col = S5_STATE_LANES // LANE
    half = S5_STATE_LANES

    def step(i, carry):
        grp = (n_grp - 1 - i) if reverse else i
        r0 = pl.multiple_of(grp * SUBLANE, SUBLANE)
        out = []
        for cb in range(n_col):
            lo, hi = cb * LANE, half + cb * LANE
            re = s_ref[pl.ds(r0, SUBLANE), lo:lo + LANE]
            im = s_ref[pl.ds(r0, SUBLANE), hi:hi + LANE]
            for k, d in enumerate((1, 2, 4)):
                sh = (SUBLANE - d) if reverse else d
                rr, ri = pltpu.roll(re, sh, 0), pltpu.roll(im, sh, 0)
                ar, ai = sc_ref[k, :, lo:lo + LANE], sc_ref[k, :, hi:hi + LANE]
                re, im = re + (ar * rr - ai * ri), im + (ar * ri + ai * rr)
            pr, pi = sc_ref[3, :, lo:lo + LANE], sc_ref[3, :, hi:hi + LANE]
            cr, ci = carry[2 * cb], carry[2 * cb + 1]
            re, im = re + (pr * cr - pi * ci), im + (pr * ci + pi * cr)
            s_ref[pl.ds(r0, SUBLANE), lo:lo + LANE] = re
            s_ref[pl.ds(r0, SUBLANE), hi:hi + LANE] = im
            edge = 0 if reverse else SUBLANE - 1
            out.append(jnp.broadcast_to(re[edge:edge + 1, :], (SUBLANE, LANE)))
            out.append(jnp.broadcast_to(im[edge:edge + 1, :], (SUBLANE, LANE)))
        return tuple(out)

    init = []
    for cb in range(n_col):
        init.append(carry_ref[:, cb * LANE:(cb + 1) * LANE])
        init.append(carry_ref[:, half + cb * LANE:half + (cb + 1) * LANE])
    fin = lax.fori_loop(0, n_grp, step, tuple(init), unroll=2)
    for cb in range(n_col):
        carry_ref[:, cb * LANE:(cb + 1) * LANE] = fin[2 * cb]
        carry_ref[:, half + cb * LANE:half + (cb + 1) * LANE] = fin[2 * cb + 1]


def _s5_fwd(proj, wb, wc, d_skip, sc, *, name, exchanges=()):
    l = proj.shape[0]
    tt = _pick_tile(l, S5_TIME_CHUNK, SUBLANE)
    u_off = (proj.shape[1] - D_HALF) // LANE
    w2 = 2 * S5_STATE_LANES

    def body(u_ref, wb_ref, wc_ref, d_ref, sc_ref, s_ref, y_ref, carry_ref):
        @pl.when(pl.program_id(1) == 0)
        def _():
            carry_ref[...] = jnp.zeros_like(carry_ref)
        u = u_ref[...]
        s_ref[...] = _dot(u, wb_ref[0], _NN)
        _scan_rows(s_ref, sc_ref.at[0], carry_ref, tt, False)
        y_ref[...] = _dot(s_ref[...], wc_ref[0], _NN) + d_ref[...] * u

    return _call(
        body, name=name, grid=(S5_LANE_BLOCKS, l // tt),
        in_specs=[pl.BlockSpec((tt, LANE), lambda b, t: (t, b + u_off)),
                  pl.BlockSpec((1, LANE, w2), lambda b, t: (b, 0, 0)),
                  pl.BlockSpec((1, w2, LANE), lambda b, t: (b, 0, 0)),
                  pl.BlockSpec((1, LANE), lambda b, t: (0, b)),
                  pl.BlockSpec((1, 4, SUBLANE, w2), lambda b, t: (b, 0, 0, 0))],
        out_specs=[pl.BlockSpec((tt, w2), lambda b, t: (t, b)), pl.BlockSpec((tt, LANE), lambda b, t: (t, b))],
        out_shape=[jax.ShapeDtypeStruct((l, S5_LANE_BLOCKS * w2), F32), jax.ShapeDtypeStruct((l, D_HALF), F32)],
        scratch_shapes=[pltpu.VMEM((SUBLANE, w2), F32)],
        args=[proj, wb, wc, d_skip, sc], parallel=False, exchanges=exchanges)


def _s5_bwd(proj, dy, states, wb, wc, d_skip, sc_rev, *, name, exchanges=()):
    l = proj.shape[0]
    tt = _pick_tile(l, S5_TIME_CHUNK, SUBLANE)
    nt = l // tt
    u_off = (proj.shape[1] - D_HALF) // LANE
    w2 = 2 * S5_STATE_LANES
    half = S5_STATE_LANES
    grp_per_chunk = tt // SUBLANE

    def body(u_ref, dy_ref, s_ref, halo_ref, wb_ref, wc_ref, d_ref, sc_ref,
             du_ref, dwb_ref, dwc_ref, dlam_ref, dd_ref, g_scr, carry_ref):
        t = pl.program_id(1)

        @pl.when(t == 0)
        def _():
            carry_ref[...] = jnp.zeros_like(carry_ref)
            dwb_ref[...] = jnp.zeros_like(dwb_ref)
            dwc_ref[...] = jnp.zeros_like(dwc_ref)
            dlam_ref[...] = jnp.zeros_like(dlam_ref)
            dd_ref[...] = jnp.zeros_like(dd_ref)

        u = u_ref[...]
        dyv = dy_ref[...]
        g_scr[...] = _dot(dyv, wc_ref[0], _NT)
        _scan_rows(g_scr, sc_ref.at[0], carry_ref, tt, True)
        gv = g_scr[...]
        du_ref[...] = (_dot(gv, wb_ref[0], _NT) + d_ref[...] * dyv).astype(BF16)
        dwb_ref[0] += _dot(u, gv, _TN)
        sv = s_ref[...]
        dwc_ref[0] += _dot(sv, dyv, _TN)
        dd_ref[...] += jnp.sum(dyv * u, axis=0, keepdims=True)
        first_chunk = t == nt - 1
        halo = jnp.where(first_chunk, 0.0, halo_ref[SUBLANE - 1:SUBLANE, :])
        row = lax.broadcasted_iota(jnp.int32, sv.shape, 0)
        sp = jnp.where(row == 0, jnp.broadcast_to(halo, sv.shape), pltpu.roll(sv, 1, 0))
        gr, gi = gv[:, :half], gv[:, half:]
        sr, si = sp[:, :half], sp[:, half:]
        dlr = jnp.sum(gr * sr + gi * si, axis=0, keepdims=True)
        dli = jnp.sum(gi * sr - gr * si, axis=0, keepdims=True)
        dlam_ref[0] += jnp.concatenate([dlr, dli], axis=1)

    rev = lambda t: nt - 1 - t
    return _call(
        body, name=name, grid=(S5_LANE_BLOCKS, nt),
        in_specs=[pl.BlockSpec((tt, LANE), lambda b, t: (rev(t), b + u_off)),
                  pl.BlockSpec((tt, LANE), lambda b, t: (rev(t), b)),
                  pl.BlockSpec((tt, w2), lambda b, t: (rev(t), b)),
                  pl.BlockSpec((SUBLANE, w2), lambda b, t: (jnp.maximum(rev(t) * grp_per_chunk - 1, 0), b)),
                  pl.BlockSpec((1, LANE, w2), lambda b, t: (b, 0, 0)),
                  pl.BlockSpec((1, w2, LANE), lambda b, t: (b, 0, 0)),
                  pl.BlockSpec((1, LANE), lambda b, t: (0, b)),
                  pl.BlockSpec((1, 4, SUBLANE, w2), lambda b, t: (b, 0, 0, 0))],
        out_specs=[pl.BlockSpec((tt, LANE), lambda b, t: (rev(t), b)),
                   pl.BlockSpec((1, LANE, w2), lambda b, t: (b, 0, 0)),
                   pl.BlockSpec((1, w2, LANE), lambda b, t: (b, 0, 0)),
                   pl.BlockSpec((1, 1, w2), lambda b, t: (b, 0, 0)),
                   pl.BlockSpec((1, LANE), lambda b, t: (0, b))],
        out_shape=[jax.ShapeDtypeStruct((l, D_HALF), BF16),
                   jax.ShapeDtypeStruct((S5_LANE_BLOCKS, LANE, w2), F32),
                   jax.ShapeDtypeStruct((S5_LANE_BLOCKS, w2, LANE), F32),
                   jax.ShapeDtypeStruct((S5_LANE_BLOCKS, 1, w2), F32),
                   jax.ShapeDtypeStruct((1, D_HALF), F32)],
        scratch_shapes=[pltpu.VMEM((tt, w2), F32), pltpu.VMEM((SUBLANE, w2), F32)],
        args=[proj, dy, states, states, wb, wc, d_skip, sc_rev], parallel=False, exchanges=exchanges)


def _glu_fwd(ypre, wg, bg, cat, *, name):
    l, d = ypre.shape
    tl = _pick_tile(l, 512, SUBLANE)

    def body(y_ref, w_ref, b_ref, cat_ref, o_ref):
        yg = _gelu(y_ref[...])
        o_ref[...] = (yg * _sigmoid(_dot(yg, w_ref[...], _NN) + b_ref[...])).astype(BF16)

    row = pl.BlockSpec((tl, d), lambda i: (i, 0))
    return pl.pallas_call(
        body, name=name, grid=(l // tl,),
        in_specs=[row, pl.BlockSpec((d, d), lambda i: (0, 0)), pl.BlockSpec((1, d), lambda i: (0, 0)), _ANY],
        out_specs=pl.BlockSpec((tl, d), lambda i: (i, 1)), out_shape=jax.ShapeDtypeStruct(cat.shape, cat.dtype),
        input_output_aliases={3: 0}, compiler_params=_PAR)(ypre, wg, bg, cat)


def _glu_bwd(dcat, ypre, wg, bg, *, name):
    l, d = ypre.shape
    tl = _pick_tile(l, 512, SUBLANE)

    def body(dy_ref, y_ref, w_ref, b_ref, dyp_ref, dw_ref, db_ref):
        @pl.when(pl.program_id(0) == 0)
        def _():
            dw_ref[...] = jnp.zeros_like(dw_ref)
            db_ref[...] = jnp.zeros_like(db_ref)
        yp = y_ref[...]
        dyb = dy_ref[...]
        yg = _gelu(yp)
        sg = _sigmoid(_dot(yg, w_ref[...], _NN) + b_ref[...])
        dz = dyb * yg * sg * (1.0 - sg)
        dyg = dyb * sg + _dot(dz, w_ref[...], _NT)
        dyp_ref[...] = dyg * _gelu_grad(yp)
        dw_ref[...] += _dot(yg, dz, _TN)
        db_ref[...] += jnp.sum(dz, axis=0, keepdims=True)

    row = pl.BlockSpec((tl, d), lambda i: (i, 0))
    mat = pl.BlockSpec((d, d), lambda i: (0, 0))
    vec = pl.BlockSpec((1, d), lambda i: (0, 0))
    return pl.pallas_call(
        body, name=name, grid=(l // tl,),
        in_specs=[pl.BlockSpec((tl, d), lambda i: (i, 1)), row, mat, vec], out_specs=[row, mat, vec],
        out_shape=[jax.ShapeDtypeStruct((l, d), F32), jax.ShapeDtypeStruct((d, d), F32),
                   jax.ShapeDtypeStruct((1, d), F32)],
        compiler_params=_ARB)(dcat, ypre, wg, bg)


def _window_sum(x, w, trailing):
    s, d = x, 1
    while d < w:
        s = s + (_shift_dn(s, d) if trailing else _shift_up(s, d))
        d *= 2
    return s


def _window_count(shape, w):
    row = lax.broadcasted_iota(jnp.int32, shape, 0)
    return jnp.minimum(row + 1, w).astype(F32)


def _pool_fwd(proj, pw, scale, *, name):
    l = proj.shape[0]
    ng = len(POOL_WINDOWS)

    def body(z_ref, w_ref, sc_ref, o_ref):
        z = z_ref[...]
        for k, w in enumerate(POOL_WINDOWS):
            @pl.when(pl.program_id(0) == k)
            def _():
                pooled = _window_sum(z, w, True) / _window_count(z.shape, w) - z
                o_ref[...] = (_dot(pooled, w_ref[0], _NN) * sc_ref[...]).astype(BF16)

    col = pl.BlockSpec((l, LANE), lambda g: (0, g))
    return pl.pallas_call(
        body, name=name, grid=(ng,),
        in_specs=[col, pl.BlockSpec((1, LANE, LANE), lambda g: (g, 0, 0)), pl.BlockSpec((1, LANE), lambda g: (0, g))],
        out_specs=col, out_shape=jax.ShapeDtypeStruct((l, 2 * D_HALF), BF16), compiler_params=_PAR)(proj, pw, scale)


def _pool_bwd(proj, dcat, pw, scale, *, name):
    l = proj.shape[0]
    ng = len(POOL_WINDOWS)

    def body(z_ref, dy_ref, w_ref, sc_ref, dz_ref, dw_ref, dsc_ref):
        z = z_ref[...]
        dy = dy_ref[...]
        for k, w in enumerate(POOL_WINDOWS):
            @pl.when(pl.program_id(0) == k)
            def _():
                cnt = _window_count(z.shape, w)
                pooled = _window_sum(z, w, True) / cnt - z
                ypre = _dot(pooled, w_ref[0], _NN)
                dsc_ref[...] = jnp.sum(dy * ypre, axis=0, keepdims=True)
                dyp = dy * sc_ref[...]
                dw_ref[0] = _dot(pooled, dyp, _TN)
                dpool = _dot(dyp, w_ref[0], _NT)
                dz_ref[...] = (_window_sum(dpool / cnt, w, False) - dpool).astype(BF16)

    col = pl.BlockSpec((l, LANE), lambda g: (0, g))
    mat = pl.BlockSpec((1, LANE, LANE), lambda g: (g, 0, 0))
    vec = pl.BlockSpec((1, LANE), lambda g: (0, g))
    return pl.pallas_call(
        body, name=name, grid=(ng,), in_specs=[col, col, mat, vec], out_specs=[col, mat, vec],
        out_shape=[jax.ShapeDtypeStruct((l, D_HALF), BF16), jax.ShapeDtypeStruct((ng, LANE, LANE), F32),
                   jax.ShapeDtypeStruct((1, D_HALF), F32)],
        compiler_params=_PAR)(proj, dcat, pw, scale)


def _tril_mask():
    r = lax.broadcasted_iota(jnp.int32, (CHUNK, CHUNK), 0)
    c = lax.broadcasted_iota(jnp.int32, (CHUNK, CHUNK), 1)
    return r >= c


def _sgu_fwd(proj, ng, sw, sb_t, cat, *, name):
    l = proj.shape[0]
    tl = _pick_tile(l, 512, CHUNK)

    def body(su_ref, sv_ref, g_ref, w_ref, b_ref, cat_ref, o_ref):
        su = _gelu(su_ref[...])
        sv = _gelu(sv_ref[...])
        r = lax.rsqrt(jnp.mean(sv * sv, axis=-1, keepdims=True) + EPS)
        v = sv * r * g_ref[...]
        mask = _tril_mask()
        for h in range(SGU_HEADS):
            wm = jnp.where(mask, w_ref[h], 0.0)
            cs = slice(h * LANE, (h + 1) * LANE)
            for n in range(tl // CHUNK):
                rs = slice(n * CHUNK, (n + 1) * CHUNK)
                mixed = _dot(wm, v[rs, cs], _NN) + b_ref[:, h:h + 1]
                o_ref[rs, cs] = (su[rs, cs] * mixed).astype(BF16)

    blk = lambda c: pl.BlockSpec((tl, D_HALF), lambda i: (i, c))
    return pl.pallas_call(
        body, name=name, grid=(l // tl,),
        in_specs=[blk(1), blk(2), pl.BlockSpec((1, D_HALF), lambda i: (0, 0)),
                  pl.BlockSpec((SGU_HEADS, CHUNK, CHUNK), lambda i: (0, 0, 0)),
                  pl.BlockSpec((CHUNK, SGU_HEADS), lambda i: (0, 0)), _ANY],
        out_specs=blk(1), out_shape=jax.ShapeDtypeStruct(cat.shape, cat.dtype), input_output_aliases={5: 0},
        compiler_params=_PAR)(proj, proj, ng, sw, sb_t, cat)


def _sgu_bwd(proj, dcat, ng, sw, sb_t, *, name):
    l = proj.shape[0]
    tl = _pick_tile(l, 512, CHUNK)

    def body(su_ref, sv_ref, dy_ref, g_ref, w_ref, b_ref, dsu_ref, dsv_ref, dw_ref, dbm_ref, dng_ref, dv_scr):
        @pl.when(pl.program_id(0) == 0)
        def _():
            dw_ref[...] = jnp.zeros_like(dw_ref)
            dbm_ref[...] = jnp.zeros_like(dbm_ref)
            dng_ref[...] = jnp.zeros_like(dng_ref)
        su_pre = su_ref[...]
        sv_pre = sv_ref[...]
        su = _gelu(su_pre)
        sv = _gelu(sv_pre)
        gsu = _gelu_grad(su_pre)
        gv = g_ref[...]
        r = lax.rsqrt(jnp.mean(sv * sv, axis=-1, keepdims=True) + EPS)
        v = sv * r * gv
        dy = dy_ref[...]
        mask = _tril_mask()
        for h in range(SGU_HEADS):
            wm = jnp.where(mask, w_ref[h], 0.0)
            cs = slice(h * LANE, (h + 1) * LANE)
            dw_acc = jnp.zeros((CHUNK, CHUNK), F32)
            db_acc = jnp.zeros((CHUNK, LANE), F32)
            for n in range(tl // CHUNK):
                rs = slice(n * CHUNK, (n + 1) * CHUNK)
                vb = v[rs, cs]
                mixed = _dot(wm, vb, _NN) + b_ref[:, h:h + 1]
                dyb = dy[rs, cs]
                dsu_ref[rs, cs] = (dyb * mixed * gsu[rs, cs]).astype(BF16)
                dmix = dyb * su[rs, cs]
                db_acc = db_acc + dmix
                dw_acc = dw_acc + _dot(dmix, vb, _NT)
                dv_scr[rs, cs] = _dot(wm, dmix, _TN)
            dw_ref[h] += jnp.where(mask, dw_acc, 0.0)
            dbm_ref[h] += db_acc
        dv = dv_scr[...]
        a = dv * gv
        m = jnp.mean(a * sv, axis=-1, keepdims=True)
        dsv = r * a - sv * (r * r * r) * m
        dng_ref[...] += jnp.sum(dv * sv * r, axis=0, keepdims=True)
        dsv_ref[...] = (dsv * _gelu_grad(sv_pre)).astype(BF16)

    blk = lambda c: pl.BlockSpec((tl, D_HALF), lambda i: (i, c))
    mats = pl.BlockSpec((SGU_HEADS, CHUNK, CHUNK), lambda i: (0, 0, 0))
    vec = pl.BlockSpec((1, D_HALF), lambda i: (0, 0))
    piece = jax.ShapeDtypeStruct((l, D_HALF), BF16)
    mshape = jax.ShapeDtypeStruct((SGU_HEADS, CHUNK, CHUNK), F32)
    return pl.pallas_call(
        body, name=name, grid=(l // tl,),
        in_specs=[blk(1), blk(2), blk(1), vec, mats, pl.BlockSpec((CHUNK, SGU_HEADS), lambda i: (0, 0))],
        out_specs=[blk(0), blk(0), mats, mats, vec],
        out_shape=[piece, piece, mshape, mshape, jax.ShapeDtypeStruct((1, D_HALF), F32)],
        scratch_shapes=[pltpu.VMEM((tl, D_HALF), F32)],
        compiler_params=_ARB)(proj, proj, dcat, ng, sw, sb_t)


def _s5_params(w):
    prep_args = (w["ssm_log_step"], w["ssm_a_re"], w["ssm_a_im"], w["ssm_b_re"], w["ssm_b_im"])
    (lr, li, br, bi), prep_vjp = jax.vjp(jax.vmap(_s5_prep), *prep_args)
    wb, wc = jax.vmap(_s5_mats)(br, bi, w["ssm_c_re"], w["ssm_c_im"])
    consts = lambda reverse: jax.vmap(functools.partial(_s5_scan_consts, reverse=reverse))(*prep_args[:3])
    return dict(wb=wb.astype(BF16), wc=wc.astype(BF16), d=w["ssm_d"][:, None, :], sc=consts(False),
                sc_rev=consts(True), prep_vjp=prep_vjp)


def _s5_param_grads(s5, dwb, dwc, dlam, dd):
    dbr, dbi, dcr, dci = jax.vmap(_s5_mats_bwd)(dwb, dwc)
    n = dlam.shape[0]
    dlr = dlam[:, :, 0, :S5_STATE_LANES].reshape(n, N_SSM_GROUPS, SSM_STATE)
    dli = dlam[:, :, 0, S5_STATE_LANES:].reshape(n, N_SSM_GROUPS, SSM_STATE)
    dls, dar, dai, db_re, db_im = s5["prep_vjp"]((dlr, dli, dbr, dbi))
    return dict(ssm_log_step=dls, ssm_a_re=dar, ssm_a_im=dai, ssm_b_re=db_re, ssm_b_im=db_im, ssm_c_re=dcr,
                ssm_c_im=dci, ssm_d=dd[:, 0, :])


TILES = {
    "mm_up": (4096, 512), "mm_up_dw": (1024, 1408), "mm_down_dx": (1024, 2816),
    "mm_down": (512, 1024), "mm_down_dw": (256, 1024), "mm_even_in": (2048, 1024), "mm_odd_in": (2048, 768),
    "mm_mix_out": (1024, 1024), "mm_mix_out_dx": (2048, 1024), "mm_mix_out_dw": (1024, 512),
    "mm_even_in_dw": (1024, 512), "mm_odd_in_dw": (1024, 512),
}


def _layer_weights(i):
    j = i // 2
    mixer = [("even_w_in", j), ("even_w_out", j), ("ssm_glu_w", j)] if i % 2 == 0 else [("odd_w_in", j),
                                                                                         ("odd_w_out", j)]
    return dict(w_in=mixer[0], w_out=mixer[1], glu=mixer[2:], up=("ffn_w_up", i), down=("ffn_w_down", i))


class _LocalWeights:
    def __init__(self, w):
        self.w, self.grads = w, {}

    def carried_by(self, stage, i):
        return []

    def delivered(self, stage, i, outs):
        pass

    def weight(self, key):
        return self.w[key[0]][key[1]]

    def grad(self, key, dw):
        self.grads[key] = dw


class _ShardedWeights:
    def __init__(self, shards):
        self.shards = shards
        self.full, self.halves, self.pending, self.scattered = {}, {}, {}, {}

    def start(self, others):
        keys = [_layer_weights(0)["w_in"]]
        outs = _exchange_only(others + [self._gather(k) for k in keys], name="ag_first")
        self._take(keys, outs[len(others):])
        return outs[:len(others)]

    def _gather(self, key):
        shard = self.shards[key[0]][key[1]]
        if len(key) == 2:
            return _Gather(shard)
        rows = shard.shape[0] // 2
        buf = self.halves.get(key[:2])
        if buf is None:
            buf = lax.empty((N_DEV,) + shard.shape, shard.dtype)
        return _Gather(shard[key[2] * rows:(key[2] + 1) * rows], into=buf, row0=key[2] * rows)

    def _take(self, keys, outs):
        for key, got in zip(keys, outs):
            if len(key) == 3 and key[:2] not in self.halves:
                self.halves[key[:2]] = got
                continue
            if BIG[key[0]] == 2:
                self.full[key[:2]] = jnp.swapaxes(got, 0, 1).reshape(got.shape[1], -1)
            else:
                self.full[key[:2]] = got.reshape(-1, got.shape[2])

    def _plan(self, stage, i):
        cur = _layer_weights(i)
        nxt = _layer_weights(i + 1) if i + 1 < DEPTH else None
        has_scan = lambda k: k % 2 == 0
        none = ([], [])
        up_half = lambda h: ([(*nxt["up"], h)], []) if nxt and not has_scan(i + 1) else none
        return {
            "mm_in": ([cur["w_out"], *cur["glu"]], []) if i == 0 else none,
            "s5_fwd": ([cur["up"]], []),
            "mm_up": ([cur["down"]] + ([nxt["w_in"], nxt["w_out"], *nxt["glu"]] if nxt else []), []),
            "ffn_act": up_half(0),
            "mm_down": up_half(1),
            "ffn_act_bwd": ([], [cur["down"]] + ([] if not nxt else [nxt["w_in"]] if not has_scan(i + 1)
                                                 else [nxt["w_out"], *nxt["glu"]])),
            "mm_up_dx": none if not nxt else ([], [(*nxt["up"], 0)]) if not has_scan(i + 1) else ([], [nxt["w_in"]]),
            "mm_up_dw": ([], [(*nxt["up"], 1)]) if nxt and not has_scan(i + 1) else none,
            "s5_bwd": ([], [cur["up"]] + ([nxt["w_out"]] if nxt else [])),
            "mm_in_dw": ([], [cur["w_out"], *cur["glu"]]) if i == 0 else none,
            "mm_in_dx": ([], [cur["w_in"]]) if i == 0 else none,
        }[stage]

    def _scatter(self, key):
        name, layer = key[:2]
        layers, rows, cols = self.shards[name].shape
        if len(key) == 3:
            src = self.pending[key[:2]] if key[2] == 0 else self.pending.pop(key[:2])
            src = src[:, key[2] * (rows // 2):(key[2] + 1) * (rows // 2)]
            row0 = layer * rows + key[2] * (rows // 2)
        else:
            src, row0 = self.pending.pop(key), layer * rows
        if name not in self.scattered:
            self.scattered[name] = lax.empty((N_DEV, layers * rows, cols), src.dtype)
        return _Scatter(src, into=self.scattered[name], row0=row0)

    def carried_by(self, stage, i):
        gather, scatter = self._plan(stage, i)
        return [self._gather(k) for k in gather] + [self._scatter(k) for k in scatter]

    def delivered(self, stage, i, outs):
        gather, scatter = self._plan(stage, i)
        self._take(gather, outs[:len(gather)])
        for key, buf in zip(scatter, outs[len(gather):]):
            self.scattered[key[0]] = buf

    def weight(self, key):
        return self.full[key]

    def grad(self, key, dw):
        self.pending[key] = _to_dest_major(dw, BIG[key[0]] - 1).astype(BF16)

    def finish(self, carrier, others):
        keys = list(self.pending)
        res, outs = carrier(others + [self._scatter(k) for k in keys])
        for (name, _), buf in zip(keys, outs[len(others):]):
            self.scattered[name] = buf
        return res, outs[:len(others)]


def _device_step(x, tgt, w, comm):
    saved = []
    s5 = _s5_params(w)
    h = _rms_fwd(x, w["norm_mix_g"][0:1], name="rms_fwd")
    for i in range(DEPTH):
        j = i // 2
        lw = _layer_weights(i)
        if i % 2 == 0:
            proj = _carry(comm, "mm_in", i, _mm, h, comm.weight(lw["w_in"]), "nn", name="mm_even_in")
            ya = _sconv_fwd(proj, w["even_conv_w"][j], name="sconv_fwd")
            states, ypre = _carry(comm, "s5_fwd", i, _s5_fwd, proj, s5["wb"][j], s5["wc"][j], s5["d"][j],
                                  s5["sc"][j], name="s5_fwd")
            cat = _glu_fwd(ypre, comm.weight(lw["glu"][0]), w["ssm_glu_b"][j][None, :], ya, name="glu_fwd")
            mix = (states, ypre)
        else:
            proj = _carry(comm, "mm_in", i, _mm, h, comm.weight(lw["w_in"]), "nn", name="mm_odd_in")
            yc = _pool_fwd(proj, w["pool_w"][j], w["pool_scale"][j][None, :], name="pool_fwd")
            sb_t = jnp.transpose(w["sgu_b"][j])
            cat = _sgu_fwd(proj, w["sgu_norm_g"][j][None, :], w["sgu_w"][j], sb_t, yc, name="sgu_fwd")
            mix = (sb_t,)
        x1, h2 = _mm(cat, comm.weight(lw["w_out"]), "nn", add=x, norm=(w["norm_ffn_g"], i), name="mm_mix_out")
        up = _carry(comm, "mm_up", i, _mm, h2, comm.weight(lw["up"]), "nn", name="mm_up")
        act, gv = _carry(comm, "ffn_act", i, _ffn_act, up, w["ffn_conv_w"][i], w["ffn_conv_b"][i:i + 1],
                         name="ffn_act")
        if i + 1 < DEPTH:
            x2, h_next = _carry(comm, "mm_down", i, _mm, act, comm.weight(lw["down"]), "nn", add=x1,
                                norm=(w["norm_mix_g"], i + 1), name="mm_down")
        else:
            x2, h_next = _mm(act, comm.weight(lw["down"]), "nn", add=x1, name="mm_down_last"), None
        saved.append((x, h, proj, cat, x1, h2, up, gv, act, mix))
        x, h = x2, h_next

    loss, dx, dxb, dgf = _loss_head(x, w["norm_final_g"][None, :], tgt, name="loss_head")
    per_layer = {}
    s5_grads = []

    def put(name, idx, val):
        per_layer.setdefault(name, {})[idx] = val

    for i in reversed(range(DEPTH)):
        j = i // 2
        lw = _layer_weights(i)
        x0, h, proj, cat, x1, h2, up, gv, act, mix = saved[i]
        dact = _mm(dxb, comm.weight(lw["down"]), "nt", name="mm_down_dx")
        comm.grad(lw["down"], _mm(act, dxb, "tn", out_dtype=BF16, name="mm_down_dw"))
        dup, dcw, dcb = _carry(comm, "ffn_act_bwd", i, _ffn_act_bwd, up, gv, dact, w["ffn_conv_w"][i],
                               name="ffn_act_bwd")
        put("ffn_conv_w", i, dcw)
        put("ffn_conv_b", i, dcb[0])
        dx1, dx1b, dg2 = _carry(comm, "mm_up_dx", i, _mm, dup, comm.weight(lw["up"]), "nt", tm_cap=512, tn_cap=D_MODEL,
                                norm_bwd=(x1, w["norm_ffn_g"], i, dx), name="mm_up_dx")
        comm.grad(lw["up"], _carry(comm, "mm_up_dw", i, _mm, h2, dup, "tn", out_dtype=BF16,
                                   name="mm_up_dw"))
        put("norm_ffn_g", i, dg2[0])
        dcat = _mm(dx1b, comm.weight(lw["w_out"]), "nt", name="mm_mix_out_dx")
        comm.grad(lw["w_out"], _mm(cat, dx1b, "tn", out_dtype=BF16, name="mm_mix_out_dw"))
        if i % 2 == 0:
            states, ypre = mix
            dxa, dba, dca, dcw_a = _sconv_bwd(proj, dcat, w["even_conv_w"][j], name="sconv_bwd")
            put("even_conv_w", j, dcw_a)
            dypre, dwg, dbg = _glu_bwd(dcat, ypre, comm.weight(lw["glu"][0]), w["ssm_glu_b"][j][None, :],
                                       name="glu_bwd")
            comm.grad(lw["glu"][0], dwg)
            put("ssm_glu_b", j, dbg[0])
            du, dwb, dwc, dlam, dd = _carry(comm, "s5_bwd", i, _s5_bwd, proj, dypre, states, s5["wb"][j], s5["wc"][j],
                                            s5["d"][j], s5["sc_rev"][j], name="s5_bwd")
            s5_grads.insert(0, (dwb, dwc, dlam, dd))
            dproj = jnp.concatenate([dxa, dba, dca, du], axis=1)
            in_name = "mm_even_in"
        else:
            (sb_t,) = mix
            dz, dpw, dps = _pool_bwd(proj, dcat, w["pool_w"][j], w["pool_scale"][j][None, :], name="pool_bwd")
            put("pool_w", j, dpw)
            put("pool_scale", j, dps[0])
            dsu, dsv, dsw, dbm, dng = _sgu_bwd(proj, dcat, w["sgu_norm_g"][j][None, :], w["sgu_w"][j], sb_t,
                                               name="sgu_bwd")
            put("sgu_w", j, dsw)
            put("sgu_b", j, jnp.sum(dbm, axis=-1))
            put("sgu_norm_g", j, dng[0])
            dproj = jnp.concatenate([dz, dsu, dsv], axis=1)
            in_name = "mm_odd_in"
        comm.grad(lw["w_in"], _carry(comm, "mm_in_dw", i, _mm, h, dproj, "tn", out_dtype=BF16, name=in_name + "_dw"))
        dx, dxb, dg1 = _carry(comm, "mm_in_dx", i, _mm, dproj, comm.weight(lw["w_in"]), "nt", tn_cap=D_MODEL,
                              norm_bwd=(x0, w["norm_mix_g"], i, dx1), name=in_name + "_dx")
        put("norm_mix_g", i, dg1[0])

    grads = {nm: [vals[k] for k in sorted(vals)] for nm, vals in per_layer.items()}
    grads.update({nm: jnp.stack(grads[nm]) for nm in SMALL})
    grads["norm_final_g"] = dgf[0]
    grads.update(_s5_param_grads(s5, *[jnp.stack(parts) for parts in zip(*s5_grads)]))
    return loss, dx, grads


def _carry(comm, stage, i, fn, *args, **kwargs):
    exchanges = comm.carried_by(stage, i)
    if not exchanges:
        return fn(*args, **kwargs)
    out, moved = fn(*args, exchanges=exchanges, **kwargs)
    comm.delivered(stage, i, moved)
    return out


def _sum_parts(parts, *, name):
    g, r, c = parts.shape
    tr = _pick_tile(r, max(16, EXCHANGE_BLOCK_ELEMS // c), 16)

    def body(p_ref, o_ref):
        acc = p_ref[0].astype(F32)
        for k in range(1, g):
            acc = acc + p_ref[k].astype(F32)
        o_ref[...] = acc

    return pl.pallas_call(
        body, name=name, grid=(r // tr,), in_specs=[pl.BlockSpec((g, tr, c), lambda i: (0, i, 0))],
        out_specs=pl.BlockSpec((tr, c), lambda i: (i, 0)), out_shape=jax.ShapeDtypeStruct((r, c), F32),
        compiler_params=_PAR)(parts)


def _adamw(w, m, v, g_parts, *, name, exchanges=()):
    r, c = w.shape
    g = g_parts.shape[0]
    tc = _pick_tile(c, 8192, LANE)
    tr = _pick_tile(r, max(16, (1 << 18) // tc), 16)
    c1 = 1.0 - ADAM_B1 ** ADAM_STEP
    c2 = 1.0 - ADAM_B2 ** ADAM_STEP

    def body(w_ref, m_ref, v_ref, g_ref, go_ref, d_ref, mo_ref, vo_ref):
        grad = g_ref[0].astype(F32)
        for k in range(1, g):
            grad = grad + g_ref[k].astype(F32)
        m_new = ADAM_B1 * m_ref[...] + (1.0 - ADAM_B1) * grad
        v_new = ADAM_B2 * v_ref[...] + (1.0 - ADAM_B2) * (grad * grad)
        go_ref[...] = grad
        mo_ref[...] = m_new
        vo_ref[...] = v_new
        d_ref[...] = -ADAM_LR * ((m_new / c1) / (jnp.sqrt(v_new / c2) + ADAM_EPS) + ADAM_WD * w_ref[...])

    blk = pl.BlockSpec((tr, tc), lambda i, j: (i, j))
    out = jax.ShapeDtypeStruct((r, c), F32)
    return _call(body, name=name, grid=(r // tr, c // tc),
                 in_specs=[blk, blk, blk, pl.BlockSpec((g, tr, tc), lambda i, j: (0, i, j))],
                 out_specs=[blk, blk, blk, blk], out_shape=[out, out, out, out], args=[w, m, v, g_parts],
                 exchanges=exchanges)


WEIGHT_NAMES = ['norm_mix_g', 'even_w_in', 'even_conv_w', 'ssm_log_step', 'ssm_a_re', 'ssm_a_im', 'ssm_b_re',
                'ssm_b_im', 'ssm_c_re', 'ssm_c_im', 'ssm_d', 'ssm_glu_w', 'ssm_glu_b', 'even_w_out', 'odd_w_in',
                'pool_w', 'pool_scale', 'sgu_norm_g', 'sgu_w', 'sgu_b', 'odd_w_out', 'norm_ffn_g', 'ffn_w_up',
                'ffn_conv_w', 'ffn_conv_b', 'ffn_w_down', 'norm_final_g']
BIG = {'even_w_in': 2, 'ssm_glu_w': 1, 'even_w_out': 1, 'odd_w_in': 2, 'odd_w_out': 1, 'ffn_w_up': 2,
       'ffn_w_down': 1}
SMALL = {'even_conv_w': 2, 'pool_scale': 1, 'sgu_norm_g': 1, 'ffn_conv_w': 2}
SMALL_ROWS = 16


def _pad_to(n, q):
    return -(-n // q) * q


def _pack(arrays, dtype, rows, lead=()):
    flat = [a.reshape(lead + (-1,)).astype(dtype) for a in arrays]
    n = sum(f.shape[-1] for f in flat)
    pad = _pad_to(n, rows * LANE) - n
    if pad:
        flat.append(jnp.zeros(lead + (pad,), dtype))
    return jnp.concatenate(flat, axis=-1).reshape(lead + (rows, -1))


def _unpack(buf, shapes, lead=()):
    flat = buf.reshape(lead + (-1,))
    out, off = [], 0
    for shp in shapes:
        n = math.prod(shp)
        out.append(flat[..., off:off + n].reshape(lead + tuple(shp)))
        off += n
    return out


def _to_dest_major(full, axis):
    shp = full.shape
    split = full.reshape(shp[:axis] + (N_DEV, shp[axis] // N_DEV) + shp[axis + 1:])
    return jnp.moveaxis(split, axis, 0)


def _from_dest_major(blocks, axis):
    moved = jnp.moveaxis(blocks, 0, axis)
    shp = moved.shape
    return moved.reshape(shp[:axis] + (shp[axis] * shp[axis + 1],) + shp[axis + 2:])


def _rows_2d(a):
    return a.reshape(-1, a.shape[-1])


def kernel(x, norm_mix_g, even_w_in, even_conv_w, ssm_log_step, ssm_a_re, ssm_a_im, ssm_b_re, ssm_b_im, ssm_c_re, ssm_c_im, ssm_d, ssm_glu_w, ssm_glu_b, even_w_out, odd_w_in, pool_w, pool_scale, sgu_norm_g, sgu_w, sgu_b, odd_w_out, norm_ffn_g, ffn_w_up, ffn_conv_w, ffn_conv_b, ffn_w_down, norm_final_g, loss_target, m_norm_mix_g, m_even_w_in, m_even_conv_w, m_ssm_log_step, m_ssm_a_re, m_ssm_a_im, m_ssm_b_re, m_ssm_b_im, m_ssm_c_re, m_ssm_c_im, m_ssm_d, m_ssm_glu_w, m_ssm_glu_b, m_even_w_out, m_odd_w_in, m_pool_w, m_pool_scale, m_sgu_norm_g, m_sgu_w, m_sgu_b, m_odd_w_out, m_norm_ffn_g, m_ffn_w_up, m_ffn_conv_w, m_ffn_conv_b, m_ffn_w_down, m_norm_final_g, v_norm_mix_g, v_even_w_in, v_even_conv_w, v_ssm_log_step, v_ssm_a_re, v_ssm_a_im, v_ssm_b_re, v_ssm_b_im, v_ssm_c_re, v_ssm_c_im, v_ssm_d, v_ssm_glu_w, v_ssm_glu_b, v_even_w_out, v_odd_w_in, v_pool_w, v_pool_scale, v_sgu_norm_g, v_sgu_w, v_sgu_b, v_odd_w_out, v_norm_ffn_g, v_ffn_w_up, v_ffn_conv_w, v_ffn_conv_b, v_ffn_w_down, v_norm_final_g):
    given = dict(locals())
    wts = {n: given[n] for n in WEIGHT_NAMES}
    mom = {n: given["m_" + n] for n in WEIGHT_NAMES}
    var = {n: given["v_" + n] for n in WEIGHT_NAMES}
    repl = [n for n in WEIGHT_NAMES if n not in BIG and n not in SMALL]

    small_shapes = [wts[n].shape for n in SMALL]
    comm = _ShardedWeights({n: wts[n].astype(BF16) for n in BIG})
    (small_all,) = comm.start([_Gather(_pack([wts[n] for n in SMALL], F32, SMALL_ROWS))])
    full = {n: wts[n] for n in repl}
    for n, blocks in zip(SMALL, _unpack(small_all, small_shapes, lead=(N_DEV,))):
        full[n] = _from_dest_major(blocks, SMALL[n])

    loss, dx, grads = _device_step(x[0], loss_target[0], full, comm)

    repl_shapes = [wts[n].shape for n in repl]
    pieces = [p.reshape(-1) for n in repl for p in (grads[n] if isinstance(grads[n], list) else [grads[n]])]
    repl_flat = jnp.concatenate(pieces + [loss.reshape(-1)])
    n_repl = repl_flat.shape[0]
    chunk = _pad_to(-(-n_repl // N_DEV), SMALL_ROWS * LANE)
    repl_flat = jnp.pad(repl_flat, (0, N_DEV * chunk - n_repl))
    small_part = _pack([_to_dest_major(grads[n], SMALL[n]) for n in SMALL], F32, SMALL_ROWS, lead=(N_DEV,))
    small_cols = small_part.shape[2]
    small_scatter = _Scatter(
        jnp.concatenate([small_part, repl_flat.reshape(N_DEV, SMALL_ROWS, chunk // SMALL_ROWS)], axis=2))

    _, (small_rs,) = comm.finish(lambda exchanges: (None, _exchange_only(exchanges, name="rs_last")), [small_scatter])
    small_sum = _sum_parts(small_rs, name="rs_sum_small")
    (repl_all,) = _exchange_only([_Gather(small_sum[:, small_cols:])], name="ag_repl")
    repl_sum = repl_all.reshape(-1)
    total_loss = repl_sum[n_repl - 1]

    out = {}
    for n in BIG:
        res = _adamw(_rows_2d(wts[n]), _rows_2d(mom[n]), _rows_2d(var[n]), comm.scattered[n], name="adamw_" + n)
        out[n] = [r.reshape(wts[n].shape) for r in res]

    def small_vec(shard_part, repl_part):
        flat = jnp.concatenate([shard_part.reshape(-1), repl_part])
        return flat.reshape(SMALL_ROWS, -1)

    def small_tree(tree):
        tail = jnp.concatenate([tree[n].reshape(-1) for n in repl])
        tail = jnp.pad(tail, (0, N_DEV * chunk - tail.shape[0]))
        return small_vec(_pack([tree[n] for n in SMALL], F32, SMALL_ROWS), tail)

    res = _adamw(small_tree(wts), small_tree(mom), small_tree(var), small_vec(small_sum[:, :small_cols], repl_sum)[None],
                 name="adamw_small")
    n_small = SMALL_ROWS * small_cols
    for k, r in enumerate(res):
        flat = r.reshape(-1)
        shard = _unpack(flat[:n_small], small_shapes)
        rest = _unpack(flat[n_small:], repl_shapes)
        for n, val in zip(SMALL, shard):
            out.setdefault(n, [None] * 4)[k] = val
        for n, val in zip(repl, rest):
            out.setdefault(n, [None] * 4)[k] = val

    grad_x = dx[None]
    return (total_loss, grad_x, *[out[n][0] for n in WEIGHT_NAMES], *[out[n][1] for n in WEIGHT_NAMES],
            *[out[n][2] for n in WEIGHT_NAMES], *[out[n][3] for n in WEIGHT_NAMES])
```

```python
import functools
import math

import jax
import jax.numpy as jnp
from jax import lax
from jax.experimental import pallas as pl
from jax.experimental.pallas import tpu as pltpu

F32 = jnp.float32
BF16 = jnp.bfloat16

D_MODEL = 1024
DEPTH = 4
D_HALF = D_MODEL // 2
SSM_GROUP = 16
N_SSM_GROUPS = D_HALF // SSM_GROUP
SSM_STATE = 64
POOL_WINDOWS = (2, 4, 8, 16)
SGU_HEADS = 4
CHUNK = 128
D_FF = 2816
CONV_WIDTH = 3
EPS = 1e-6
N_DEV = 8

ADAM_LR = 0.001
ADAM_B1 = 0.9
ADAM_B2 = 0.999
ADAM_EPS = 1e-08
ADAM_WD = 0.01
ADAM_STEP = 10

LANE = 128
SUBLANE = 8
S5_LANE_BLOCKS = D_HALF // LANE
S5_STATE_LANES = (N_SSM_GROUPS // S5_LANE_BLOCKS) * SSM_STATE
S5_TIME_CHUNK = 512
EXCHANGE_BLOCK_ELEMS = 1 << 20

GELU_K = math.sqrt(2.0 / math.pi)
GELU_C = 0.044715

_ARB = pltpu.CompilerParams(dimension_semantics=("arbitrary",))
_PAR = pltpu.CompilerParams(dimension_semantics=("parallel",))


def _pick_tile(n, cap, mult):
    if n <= cap:
        return n
    best = None
    for t in range(mult, cap + 1, mult):
        if n % t == 0:
            best = t
    assert best is not None, (n, cap, mult)
    return best


_MESH = pl.DeviceIdType.MESH
_ANY = pl.BlockSpec(memory_space=pl.ANY)
SEMS_PER_EXCHANGE = N_DEV - 1


class _Gather:
    def __init__(self, src, into=None, row0=0):
        self.src, self.into, self.row0 = src, into, row0
        self.out_shape = jax.ShapeDtypeStruct((N_DEV,) + src.shape if into is None else into.shape, src.dtype)

    def copies(self, x_ref, whole_ref, send_sems, recv_sems, local_sem):
        out_ref = whole_ref if self.into is None else whole_ref.at[:, pl.ds(self.row0, self.src.shape[0])]
        x, y, cc = lax.axis_index("x"), lax.axis_index("y"), lax.axis_index("c")
        me, sibling = (x, y, cc), (x, y, 1 - cc)
        chips = [(1 - x, y), (x, 1 - y), (1 - x, 1 - y)]

        def rows(px, py, pc):
            return out_ref.at[4 * px + 2 * py + pc]

        def copy(k, block, to, src=None):
            return pltpu.make_async_remote_copy(
                src_ref=rows(*block) if src is None else src, dst_ref=rows(*block),
                send_sem=send_sems.at[k], recv_sem=recv_sems.at[k], device_id=to, device_id_type=_MESH)

        return dict(
            mine=pltpu.make_async_copy(x_ref, rows(*me), local_sem),
            first=[copy(0, me, sibling, src=x_ref)] + [copy(1 + k, me, (*chip, cc), src=x_ref)
                                                       for k, chip in enumerate(chips)],
            passed=[copy(4 + k, (*chip, cc), sibling) for k, chip in enumerate(chips)],
            over_ici=[copy(1 + k, (*chip, cc), me) for k, chip in enumerate(chips)],
            from_sibling=[copy(0, sibling, me)] + [copy(4 + k, (*chip, 1 - cc), me) for k, chip in enumerate(chips)])

    def start(self, *refs):
        cps = self.copies(*refs)
        cps["mine"].start()
        for cp in cps["first"]:
            cp.start()

    def finish(self, *refs):
        cps = self.copies(*refs)
        for arrived, onward in zip(cps["over_ici"], cps["passed"]):
            arrived.wait_recv()
            onward.start()
        for arrived in cps["from_sibling"]:
            arrived.wait_recv()
        for cp in cps["first"] + cps["passed"]:
            cp.wait_send()
        cps["mine"].wait()


class _Scatter:
    def __init__(self, src, into=None, row0=0):
        self.src, self.into, self.row0 = src, into, row0
        whole = src if into is None else into
        self.out_shape = jax.ShapeDtypeStruct(whole.shape, whole.dtype)

    def copies(self, p_ref, whole_ref, send_sems, recv_sems, local_sem):
        x, y, cc = lax.axis_index("x"), lax.axis_index("y"), lax.axis_index("c")
        me = 4 * x + 2 * y + cc
        rows = self.src.shape[1]
        out_ref = whole_ref if self.into is None else whole_ref.at[:, pl.ds(self.row0, rows)]
        sends, arrivals = [], []
        for k in range(1, N_DEV):
            px = (1 - x) if k & 4 else x
            py = (1 - y) if k & 2 else y
            pc = (1 - cc) if k & 1 else cc
            peer = 4 * px + 2 * py + pc
            kw = dict(send_sem=send_sems.at[k - 1], recv_sem=recv_sems.at[k - 1], device_id=(px, py, pc),
                      device_id_type=_MESH)
            sends.append(pltpu.make_async_remote_copy(src_ref=p_ref.at[peer], dst_ref=out_ref.at[me], **kw))
            arrivals.append(pltpu.make_async_remote_copy(src_ref=p_ref.at[me], dst_ref=out_ref.at[peer], **kw))
        return dict(mine=pltpu.make_async_copy(p_ref.at[me], out_ref.at[me], local_sem), sends=sends,
                    arrivals=arrivals)

    def start(self, *refs):
        cps = self.copies(*refs)
        cps["mine"].start()
        for cp in cps["sends"]:
            cp.start()

    def finish(self, *refs):
        cps = self.copies(*refs)
        for cp in cps["arrivals"]:
            cp.wait_recv()
        for cp in cps["sends"]:
            cp.wait_send()
        cps["mine"].wait()


class _SemView:
    def __init__(self, ref, lo):
        self.ref, self.lo = ref, lo

    @property
    def at(self):
        return self

    def __getitem__(self, k):
        return self.ref.at[self.lo + k]


def _call(body, *, name, grid, in_specs, out_specs, out_shape, args, scratch_shapes=(), parallel=True, exchanges=()):
    n_axes = len(grid)
    if not exchanges:
        sem = ("parallel" if parallel else "arbitrary",) * n_axes
        return pl.pallas_call(
            body, name=name, grid=grid, in_specs=in_specs, out_specs=out_specs, out_shape=out_shape,
            scratch_shapes=scratch_shapes, compiler_params=pltpu.CompilerParams(dimension_semantics=sem))(*args)
    single = not isinstance(out_shape, (list, tuple))
    out_specs = [out_specs] if single else list(out_specs)
    out_shape = [out_shape] if single else list(out_shape)
    n_in, n_out, n_scr, n_x = len(in_specs), len(out_specs), len(scratch_shapes), len(exchanges)
    landing = [(e, ex.into) for e, ex in enumerate(exchanges) if ex.into is not None]
    aliases = {n_in + n_x + pos: n_out + e for pos, (e, _) in enumerate(landing)}

    def wrapped(*refs):
        ins, refs = refs[:n_in], refs[n_in:]
        x_in, refs = refs[:n_x], refs[n_x + len(landing):]
        outs, refs = refs[:n_out], refs[n_out:]
        x_out, refs = refs[:n_x], refs[n_x:]
        scr, (send_sems, recv_sems, local_sems) = refs[:n_scr], refs[n_scr:]
        ids = [pl.program_id(k) for k in range(n_axes)]
        first = functools.reduce(jnp.logical_and, [i == 0 for i in ids])
        last = functools.reduce(jnp.logical_and, [i == g - 1 for i, g in zip(ids, grid)])

        def sems(e):
            lo = e * SEMS_PER_EXCHANGE
            return _SemView(send_sems, lo), _SemView(recv_sems, lo), local_sems.at[e]

        @pl.when(first)
        def _():
            for e, ex in enumerate(exchanges):
                ex.start(x_in[e], x_out[e], *sems(e))

        body(*ins, *outs, *scr)

        @pl.when(last)
        def _():
            for e, ex in enumerate(exchanges):
                ex.finish(x_in[e], x_out[e], *sems(e))

    res = pl.pallas_call(
        wrapped, name=name, grid=grid, in_specs=list(in_specs) + [_ANY] * (n_x + len(landing)),
        out_specs=out_specs + [_ANY] * n_x, out_shape=out_shape + [ex.out_shape for ex in exchanges],
        input_output_aliases=aliases,
        scratch_shapes=list(scratch_shapes) + [pltpu.SemaphoreType.DMA((n_x * SEMS_PER_EXCHANGE,)),
                                               pltpu.SemaphoreType.DMA((n_x * SEMS_PER_EXCHANGE,)),
                                               pltpu.SemaphoreType.DMA((n_x,))],
        compiler_params=pltpu.CompilerParams(dimension_semantics=("arbitrary",) * n_axes),
    )(*args, *[ex.src for ex in exchanges], *[buf for _, buf in landing])
    outs, x_outs = res[:n_out], res[n_out:]
    return (outs[0] if single else outs), x_outs


def _exchange_only(exchanges, *, name):
    def body():
        pass

    return _call(body, name=name, grid=(1,), in_specs=[], out_specs=[], out_shape=[], args=[],
                 exchanges=exchanges)[1]


def _shift_dn(x, d):
    rolled = pltpu.roll(x, d, 0)
    if x.shape[0] <= SUBLANE or d >= SUBLANE:
        row = lax.broadcasted_iota(jnp.int32, x.shape, 0)
        return jnp.where(row >= d, rolled, 0.0)
    row = lax.broadcasted_iota(jnp.int32, (SUBLANE, x.shape[1]), 0)
    return jnp.concatenate([jnp.where(row >= d, rolled[:SUBLANE], 0.0), rolled[SUBLANE:]], axis=0)


def _shift_up(x, d):
    n = x.shape[0]
    rolled = pltpu.roll(x, n - d, 0)
    if n <= SUBLANE or d >= SUBLANE:
        row = lax.broadcasted_iota(jnp.int32, x.shape, 0)
        return jnp.where(row < n - d, rolled, 0.0)
    row = lax.broadcasted_iota(jnp.int32, (SUBLANE, x.shape[1]), 0)
    return jnp.concatenate([rolled[:n - SUBLANE], jnp.where(row < SUBLANE - d, rolled[n - SUBLANE:], 0.0)], axis=0)


def _gelu(x):
    return 0.5 * x * (1.0 + jnp.tanh(GELU_K * (x + GELU_C * x * x * x)))


def _gelu_grad(x):
    t = jnp.tanh(GELU_K * (x + GELU_C * x * x * x))
    return 0.5 * (1.0 + t) + 0.5 * x * (1.0 - t * t) * (GELU_K * (1.0 + 3.0 * GELU_C * x * x))


def _sigmoid(x):
    return 0.5 + 0.5 * jnp.tanh(0.5 * x)


def _conv3(x, w_ref):
    return w_ref[0:1, :] * _shift_dn(x, 2) + w_ref[1:2, :] * _shift_dn(x, 1) + w_ref[2:3, :] * x


def _dot(a, b, dims):
    return lax.dot_general(a.astype(BF16), b.astype(BF16), (dims, ((), ())), preferred_element_type=F32)


_NN = ((1,), (0,))
_NT = ((1,), (1,))
_TN = ((0,), (0,))


def _tiles(name, m, n, default):
    tm, tn = TILES.get(name, default)
    return math.gcd(tm, m), math.gcd(tn, n)


def _mm(a, b, mode, *, name, out_dtype=F32, add=None, norm=None, norm_bwd=None, tm_cap=512, tn_cap=1536,
        exchanges=()):
    halves = None
    if mode == "tn":
        r, m = a.shape
        n = b.shape[-1] * (2 if b.ndim == 3 else 1)
        tm, tn = _tiles(name, m, n, (_pick_tile(m, 256, LANE), _pick_tile(n, tn_cap, LANE)))
        if b.ndim == 3:
            per_half = b.shape[-1] // tn
            b_spec = pl.BlockSpec((None, r, tn), lambda i, j: (j // per_half, 0, j % per_half))
        else:
            b_spec = pl.BlockSpec((r, tn), lambda i, j: (0, j))
        in_specs = [pl.BlockSpec((r, tm), lambda i, j: (0, i)), b_spec]
        dims = _TN
    elif mode == "nt" and a.ndim == 3:
        _, m, halves = a.shape
        n = b.shape[0]
        tm, tn = _tiles(name, m, n, (_pick_tile(m, tm_cap, SUBLANE), _pick_tile(n, tn_cap, LANE)))
        in_specs = [pl.BlockSpec((2, tm, halves), lambda i, j: (0, i, 0)),
                    pl.BlockSpec((tn, 2 * halves), lambda i, j: (j, 0))]
        dims = _NT
    elif mode == "nn":
        m, k = a.shape
        n = b.shape[1]
        tm, tn = _tiles(name, m, n, (_pick_tile(m, tm_cap, SUBLANE), _pick_tile(n, tn_cap, LANE)))
        in_specs = [pl.BlockSpec((tm, k), lambda i, j: (i, 0)), pl.BlockSpec((k, tn), lambda i, j: (0, j))]
        dims = _NN
    else:
        m, k = a.shape
        n = b.shape[0]
        tm, tn = _tiles(name, m, n, (_pick_tile(m, tm_cap, SUBLANE), _pick_tile(n, tn_cap, LANE)))
        in_specs = [pl.BlockSpec((tm, k), lambda i, j: (i, 0)), pl.BlockSpec((tn, k), lambda i, j: (j, 0))]
        dims = _NT
    assert m % tm == 0 and n % tn == 0, (name, m, n, tm, tn)
    args = [a, b]
    tile = pl.BlockSpec((tm, tn), lambda i, j: (i, j))
    if add is not None:
        in_specs.append(tile)
        args.append(add)
    out_specs, out_shape = tile, jax.ShapeDtypeStruct((m, n), out_dtype)
    if norm is not None:
        gains, layer = norm
        assert tn == n
        in_specs.append(pl.BlockSpec((None, 1, n), lambda i, j: (layer, 0, 0)))
        args.append(gains.reshape(gains.shape[0], 1, n))
        out_specs, out_shape = [tile, tile], [out_shape, jax.ShapeDtypeStruct((m, n), BF16)]
    if norm_bwd is not None:
        x_in, gains, layer, res = norm_bwd
        assert tn == n and add is None and norm is None
        vec = pl.BlockSpec((None, 1, n), lambda i, j: (layer, 0, 0))
        in_specs += [tile, vec, tile]
        args += [x_in, gains.reshape(gains.shape[0], 1, n), res]
        out_specs = [tile, tile, pl.BlockSpec((1, n), lambda i, j: (0, 0))]
        out_shape = [jax.ShapeDtypeStruct((m, n), F32), jax.ShapeDtypeStruct((m, n), BF16),
                     jax.ShapeDtypeStruct((1, n), F32)]

    def body(*refs):
        if halves is None:
            acc = _dot(refs[0][...], refs[1][...], dims)
        else:
            acc = (_dot(refs[0][0], refs[1][:, :halves], dims) + _dot(refs[0][1], refs[1][:, halves:], dims))
        if add is not None:
            acc = acc + refs[2][...]
        if norm_bwd is not None:
            x_ref, g_ref, res_ref, dx_ref, dxb_ref, dg_ref = refs[2:]

            @pl.when(pl.program_id(0) == 0)
            def _():
                dg_ref[...] = jnp.zeros_like(dg_ref)
            dx, xn = _rms_bwd_rows(acc, x_ref[...], g_ref[...])
            dx = dx + res_ref[...]
            dx_ref[...] = dx
            dxb_ref[...] = dx.astype(BF16)
            dg_ref[...] += jnp.sum(acc * xn, axis=0, keepdims=True)
        elif norm is None:
            refs[-1][...] = acc.astype(out_dtype)
        else:
            refs[-2][...] = acc.astype(out_dtype)
            r = lax.rsqrt(jnp.mean(acc * acc, axis=-1, keepdims=True) + EPS)
            refs[-1][...] = (acc * r * refs[-3][...]).astype(BF16)

    return _call(body, name=name, grid=(m // tm, n // tn), in_specs=in_specs, out_specs=out_specs,
                 out_shape=out_shape, args=args, parallel=norm_bwd is None, exchanges=exchanges)


def _rms_fwd(x, g, *, name):
    l, d = x.shape
    tl = _pick_tile(l, 512, SUBLANE)

    def body(x_ref, g_ref, h_ref):
        xv = x_ref[...]
        r = lax.rsqrt(jnp.mean(xv * xv, axis=-1, keepdims=True) + EPS)
        h_ref[...] = (xv * r * g_ref[...]).astype(BF16)

    return pl.pallas_call(
        body, name=name, grid=(l // tl,),
        in_specs=[pl.BlockSpec((tl, d), lambda i: (i, 0)), pl.BlockSpec((1, d), lambda i: (0, 0))],
        out_specs=pl.BlockSpec((tl, d), lambda i: (i, 0)),
        out_shape=jax.ShapeDtypeStruct((l, d), BF16), compiler_params=_PAR)(x, g)


def _rms_bwd_rows(dh, xv, g):
    r = lax.rsqrt(jnp.mean(xv * xv, axis=-1, keepdims=True) + EPS)
    a = dh * g
    m = jnp.mean(a * xv, axis=-1, keepdims=True)
    return r * a - xv * (r * r * r) * m, xv * r


def _loss_head(x, g, tgt, *, name):
    l, d = x.shape
    tl = _pick_tile(l, 512, SUBLANE)

    def body(x_ref, g_ref, t_ref, loss_ref, dx_ref, dxb_ref, dg_ref):
        @pl.when(pl.program_id(0) == 0)
        def _():
            dg_ref[...] = jnp.zeros_like(dg_ref)
            loss_ref[...] = jnp.zeros_like(loss_ref)
        xv = x_ref[...]
        gv = g_ref[...]
        r = lax.rsqrt(jnp.mean(xv * xv, axis=-1, keepdims=True) + EPS)
        err = xv * r * gv - t_ref[...]
        row_loss = jnp.sum(err * err, axis=-1, keepdims=True) * (0.5 / d)
        loss_ref[...] += jnp.sum(row_loss, axis=0, keepdims=True)
        dy = err * (1.0 / d)
        dx, xn = _rms_bwd_rows(dy, xv, gv)
        dx_ref[...] = dx
        dxb_ref[...] = dx.astype(BF16)
        dg_ref[...] += jnp.sum(dy * xn, axis=0, keepdims=True)

    row = pl.BlockSpec((tl, d), lambda i: (i, 0))
    vec = pl.BlockSpec((1, d), lambda i: (0, 0))
    one = pl.BlockSpec((1, 1), lambda i: (0, 0))
    return pl.pallas_call(
        body, name=name, grid=(l // tl,), in_specs=[row, vec, row], out_specs=[one, row, row, vec],
        out_shape=[jax.ShapeDtypeStruct((1, 1), F32), jax.ShapeDtypeStruct((l, d), F32),
                   jax.ShapeDtypeStruct((l, d), BF16), jax.ShapeDtypeStruct((1, d), F32)],
        compiler_params=_ARB)(x, g, tgt)


def _ffn_act(up, cw, cb, *, name, exchanges=()):
    l = up.shape[0]
    nb = D_FF // LANE

    def body(ug_ref, uv_ref, wg_ref, wv_ref, bg_ref, bv_ref, o_ref, gv_ref):
        gc = _conv3(ug_ref[...], wg_ref) + bg_ref[...]
        vc = _conv3(uv_ref[...], wv_ref) + bv_ref[...]
        sg = _sigmoid(gc)
        silu = gc * sg
        gv_ref[0] = vc * (sg * (1.0 + gc * (1.0 - sg)))
        gv_ref[1] = silu
        o_ref[...] = (silu * vc).astype(BF16)

    col = lambda off: pl.BlockSpec((l, LANE), lambda j: (0, j + off))
    w3 = lambda off: pl.BlockSpec((CONV_WIDTH, LANE), lambda j: (0, j + off))
    b1 = lambda off: pl.BlockSpec((1, LANE), lambda j: (0, j + off))
    return _call(body, name=name, grid=(nb,), in_specs=[col(0), col(nb), w3(0), w3(nb), b1(0), b1(nb)],
                 out_specs=[col(0), pl.BlockSpec((2, l, LANE), lambda j: (0, 0, j))],
                 out_shape=[jax.ShapeDtypeStruct((l, D_FF), BF16), jax.ShapeDtypeStruct((2, l, D_FF), F32)],
                 args=[up, up, cw, cw, cb, cb], exchanges=exchanges)


def _ffn_act_bwd(up, gv, dact, cw, *, name, exchanges=()):
    l = up.shape[0]
    nb = D_FF // LANE

    def half_bwd(k, dc, x, w_ref, dup_ref, dcw_ref, dcb_ref):
        d1, d2 = _shift_up(dc, 1), _shift_up(dc, 2)
        dcb_ref[k] = jnp.sum(dc, axis=0, keepdims=True)
        dcw_ref[k] = jnp.concatenate([jnp.sum(d2 * x, axis=0, keepdims=True),
                                      jnp.sum(d1 * x, axis=0, keepdims=True),
                                      jnp.sum(dc * x, axis=0, keepdims=True)], axis=0)
        dup_ref[k] = (w_ref[2:3, :] * dc + w_ref[1:2, :] * d1 + w_ref[0:1, :] * d2).astype(BF16)

    def body(ug_ref, uv_ref, gv_ref, da_ref, wg_ref, wv_ref, dup_ref, dcw_ref, dcb_ref):
        da = da_ref[...]
        half_bwd(0, da * gv_ref[0], ug_ref[...], wg_ref, dup_ref, dcw_ref, dcb_ref)
        half_bwd(1, da * gv_ref[1], uv_ref[...], wv_ref, dup_ref, dcw_ref, dcb_ref)

    col = lambda off: pl.BlockSpec((l, LANE), lambda j: (0, j + off))
    w3 = lambda off: pl.BlockSpec((CONV_WIDTH, LANE), lambda j: (0, j + off))
    both = lambda rows: pl.BlockSpec((2, rows, LANE), lambda j: (0, 0, j))
    res = _call(
        body, name=name, grid=(nb,),
        in_specs=[col(0), col(nb), both(l), col(0), w3(0), w3(nb)],
        out_specs=[both(l), both(CONV_WIDTH), both(1)],
        out_shape=[jax.ShapeDtypeStruct((2, l, D_FF), BF16), jax.ShapeDtypeStruct((2, CONV_WIDTH, D_FF), F32),
                   jax.ShapeDtypeStruct((2, 1, D_FF), F32)],
        args=[up, up, gv, dact, cw, cw], exchanges=exchanges)
    (dup, dcw, dcb), moved = res if exchanges else (res, None)
    outs = [dup, jnp.concatenate([dcw[0], dcw[1]], axis=1), jnp.concatenate([dcb[0], dcb[1]], axis=1)]
    return (outs, moved) if exchanges else outs


def _sconv_fwd(proj, cw, *, name):
    l = proj.shape[0]
    nb = D_HALF // LANE

    def body(xa_ref, ba_ref, ca_ref, w_ref, o_ref):
        o_ref[...] = (ba_ref[...] * _conv3(ca_ref[...] * xa_ref[...], w_ref)).astype(BF16)

    col = lambda off: pl.BlockSpec((l, LANE), lambda j: (0, j + off))
    return pl.pallas_call(
        body, name=name, grid=(nb,),
        in_specs=[col(0), col(nb), col(2 * nb), pl.BlockSpec((CONV_WIDTH, LANE), lambda j: (0, j))],
        out_specs=col(0), out_shape=jax.ShapeDtypeStruct((l, 2 * D_HALF), BF16),
        compiler_params=_PAR)(proj, proj, proj, cw)


def _sconv_bwd(proj, dcat, cw, *, name):
    l = proj.shape[0]
    nb = D_HALF // LANE

    def body(xa_ref, ba_ref, ca_ref, dy_ref, w_ref, dxa_ref, dba_ref, dca_ref, dw_ref):
        xa, ba, ca, dy = xa_ref[...], ba_ref[...], ca_ref[...], dy_ref[...]
        q = ca * xa
        dba_ref[...] = (dy * _conv3(q, w_ref)).astype(BF16)
        dconv = dy * ba
        d1, d2 = _shift_up(dconv, 1), _shift_up(dconv, 2)
        dw_ref[...] = jnp.concatenate([jnp.sum(d2 * q, axis=0, keepdims=True), jnp.sum(d1 * q, axis=0, keepdims=True),
                                       jnp.sum(dconv * q, axis=0, keepdims=True)], axis=0)
        dq = w_ref[2:3, :] * dconv + w_ref[1:2, :] * d1 + w_ref[0:1, :] * d2
        dxa_ref[...] = (dq * ca).astype(BF16)
        dca_ref[...] = (dq * xa).astype(BF16)

    col = lambda off: pl.BlockSpec((l, LANE), lambda j: (0, j + off))
    w3 = pl.BlockSpec((CONV_WIDTH, LANE), lambda j: (0, j))
    piece = jax.ShapeDtypeStruct((l, D_HALF), BF16)
    return pl.pallas_call(
        body, name=name, grid=(nb,),
        in_specs=[col(0), col(nb), col(2 * nb), col(0), w3],
        out_specs=[col(0), col(0), col(0), w3],
        out_shape=[piece, piece, piece, jax.ShapeDtypeStruct((CONV_WIDTH, D_HALF), F32)],
        compiler_params=_PAR)(proj, proj, proj, dcat, cw)


def _s5_prep(log_step, a_re, a_im, b_re, b_im):
    step = jnp.exp(log_step)[:, None]
    mag = jnp.exp(a_re * step)
    lr = mag * jnp.cos(a_im * step)
    li = mag * jnp.sin(a_im * step)
    nr = lr - 1.0
    den = a_re * a_re + a_im * a_im
    qr = (nr * a_re + li * a_im) / den
    qi = (li * a_re - nr * a_im) / den
    br = qr[..., None] * b_re - qi[..., None] * b_im
    bi = qr[..., None] * b_im + qi[..., None] * b_re
    return lr, li, br, bi


def _block_diag(m):
    nb, ng, r, c = m.shape
    eye = jnp.eye(ng, dtype=m.dtype)
    return jnp.einsum("bgrc,gh->bgrhc", m, eye).reshape(nb, ng * r, ng * c)


def _block_diag_extract(w, r, c):
    nb = w.shape[0]
    ng = w.shape[1] // r
    w5 = w.reshape(nb, ng, r, ng, c)
    return jnp.einsum("bgrhc,gh->bgrc", w5, jnp.eye(ng, dtype=w.dtype))


def _s5_mats(br, bi, c_re, c_im):
    g8 = N_SSM_GROUPS // S5_LANE_BLOCKS
    to_blk = lambda m: m.reshape(S5_LANE_BLOCKS, g8, m.shape[1], m.shape[2])
    wb = jnp.concatenate([_block_diag(to_blk(jnp.swapaxes(br, 1, 2))),
                          _block_diag(to_blk(jnp.swapaxes(bi, 1, 2)))], axis=2)
    wc = jnp.concatenate([_block_diag(to_blk(jnp.swapaxes(c_re, 1, 2))),
                          _block_diag(to_blk(jnp.swapaxes(-c_im, 1, 2)))], axis=1)
    return wb, wc


def _s5_mats_bwd(dwb, dwc):
    g, p, h = N_SSM_GROUPS, SSM_STATE, SSM_GROUP
    half = S5_STATE_LANES
    dbr = jnp.swapaxes(_block_diag_extract(dwb[:, :, :half], h, p).reshape(g, h, p), 1, 2)
    dbi = jnp.swapaxes(_block_diag_extract(dwb[:, :, half:], h, p).reshape(g, h, p), 1, 2)
    dcr = jnp.swapaxes(_block_diag_extract(dwc[:, :half, :], p, h).reshape(g, p, h), 1, 2)
    dci = -jnp.swapaxes(_block_diag_extract(dwc[:, half:, :], p, h).reshape(g, p, h), 1, 2)
    return dbr, dbi, dcr, dci


def _s5_scan_consts(log_step, a_re, a_im, reverse):
    step = jnp.exp(log_step)[:, None]
    xr = (a_re * step).reshape(S5_LANE_BLOCKS, 1, S5_STATE_LANES)
    xi = (a_im * step).reshape(S5_LANE_BLOCKS, 1, S5_STATE_LANES)
    if reverse:
        xi = -xi
    row = jnp.arange(SUBLANE, dtype=F32).reshape(1, SUBLANE, 1)

    def power(n):
        mag = jnp.exp(n * xr)
        return jnp.concatenate([mag * jnp.cos(n * xi), mag * jnp.sin(n * xi)], axis=-1)

    kinds = []
    for d in (1, 2, 4):
        keep = (row <= SUBLANE - 1 - d) if reverse else (row >= d)
        kinds.append(jnp.where(keep, power(jnp.full_like(row, float(d))), 0.0))
    kinds.append(power((SUBLANE - row) if reverse else (row + 1.0)))
    return jnp.stack(kinds, axis=1)


def _scan_rows(s_ref, sc_ref, carry_ref, n_rows, reverse):
    n_grp = n_rows // SUBLANE
    n_col = S5_STATE_LANES // LANE
    half = S5_STATE_LANES

    def step(i, carry):
        grp = (n_grp - 1 - i) if reverse else i
        r0 = pl.multiple_of(grp * SUBLANE, SUBLANE)
        out = []
        for cb in range(n_col):
            lo, hi = cb * LANE, half + cb * LANE
            re = s_ref[pl.ds(r0, SUBLANE), lo:lo + LANE]
            im = s_ref[pl.ds(r0, SUBLANE), hi:hi + LANE]
            for k, d in enumerate((1, 2, 4)):
                sh = (SUBLANE - d) if reverse else d
                rr, ri = pltpu.roll(re, sh, 0), pltpu.roll(im, sh, 0)
                ar, ai = sc_ref[k, :, lo:lo + LANE], sc_ref[k, :, hi:hi + LANE]
                re, im = re + (ar * rr - ai * ri), im + (ar * ri + ai * rr)
            pr, pi = sc_ref[3, :, lo:lo + LANE], sc_ref[3, :, hi:hi + LANE]
            cr, ci = carry[2 * cb], carry[2 * cb + 1]
            re, im = re + (pr * cr - pi * ci), im + (pr * ci + pi * cr)
            s_ref[pl.ds(r0, SUBLANE), lo:lo + LANE] = re
            s_ref[pl.ds(r0, SUBLANE), hi:hi + LANE] = im
            edge = 0 if reverse else SUBLANE - 1
            out.append(jnp.broadcast_to(re[edge:edge + 1, :], (SUBLANE, LANE)))
            out.append(jnp.broadcast_to(im[edge:edge + 1, :], (SUBLANE, LANE)))
        return tuple(out)

    init = []
    for cb in range(n_col):
        init.append(carry_ref[:, cb * LANE:(cb + 1) * LANE])
        init.append(carry_ref[:, half + cb * LANE:half + (cb + 1) * LANE])
    fin = lax.fori_loop(0, n_grp, step, tuple(init), unroll=2)
    for cb in range(n_col):
        carry_ref[:, cb * LANE:(cb + 1) * LANE] = fin[2 * cb]
        carry_ref[:, half + cb * LANE:half + (cb + 1) * LANE] = fin[2 * cb + 1]


def _s5_fwd(proj, wb, wc, d_skip, sc, *, name, exchanges=()):
    l = proj.shape[0]
    tt = _pick_tile(l, S5_TIME_CHUNK, SUBLANE)
    u_off = (proj.shape[1] - D_HALF) // LANE
    w2 = 2 * S5_STATE_LANES

    def body(u_ref, wb_ref, wc_ref, d_ref, sc_ref, s_ref, y_ref, carry_ref):
        @pl.when(pl.program_id(1) == 0)
        def _():
            carry_ref[...] = jnp.zeros_like(carry_ref)
        u = u_ref[...]
        s_ref[...] = _dot(u, wb_ref[0], _NN)
        _scan_rows(s_ref, sc_ref.at[0], carry_ref, tt, False)
        y_ref[...] = _dot(s_ref[...], wc_ref[0], _NN) + d_ref[...] * u

    return _call(
        body, name=name, grid=(S5_LANE_BLOCKS, l // tt),
        in_specs=[pl.BlockSpec((tt, LANE), lambda b, t: (t, b + u_off)),
                  pl.BlockSpec((1, LANE, w2), lambda b, t: (b, 0, 0)),
                  pl.BlockSpec((1, w2, LANE), lambda b, t: (b, 0, 0)),
                  pl.BlockSpec((1, LANE), lambda b, t: (0, b)),
                  pl.BlockSpec((1, 4, SUBLANE, w2), lambda b, t: (b, 0, 0, 0))],
        out_specs=[pl.BlockSpec((tt, w2), lambda b, t: (t, b)), pl.BlockSpec((tt, LANE), lambda b, t: (t, b))],
        out_shape=[jax.ShapeDtypeStruct((l, S5_LANE_BLOCKS * w2), F32), jax.ShapeDtypeStruct((l, D_HALF), F32)],
        scratch_shapes=[pltpu.VMEM((SUBLANE, w2), F32)],
        args=[proj, wb, wc, d_skip, sc], parallel=False, exchanges=exchanges)


def _s5_bwd(proj, dy, states, wb, wc, d_skip, sc_rev, *, name, exchanges=()):
    l = proj.shape[0]
    tt = _pick_tile(l, S5_TIME_CHUNK, SUBLANE)
    nt = l // tt
    u_off = (proj.shape[1] - D_HALF) // LANE
    w2 = 2 * S5_STATE_LANES
    half = S5_STATE_LANES
    grp_per_chunk = tt // SUBLANE

    def body(u_ref, dy_ref, s_ref, halo_ref, wb_ref, wc_ref, d_ref, sc_ref,
             du_ref, dwb_ref, dwc_ref, dlam_ref, dd_ref, g_scr, carry_ref):
        t = pl.program_id(1)

        @pl.when(t == 0)
        def _():
            carry_ref[...] = jnp.zeros_like(carry_ref)
            dwb_ref[...] = jnp.zeros_like(dwb_ref)
            dwc_ref[...] = jnp.zeros_like(dwc_ref)
            dlam_ref[...] = jnp.zeros_like(dlam_ref)
            dd_ref[...] = jnp.zeros_like(dd_ref)

        u = u_ref[...]
        dyv = dy_ref[...]
        g_scr[...] = _dot(dyv, wc_ref[0], _NT)
        _scan_rows(g_scr, sc_ref.at[0], carry_ref, tt, True)
        gv = g_scr[...]
        du_ref[...] = (_dot(gv, wb_ref[0], _NT) + d_ref[...] * dyv).astype(BF16)
        dwb_ref[0] += _dot(u, gv, _TN)
        sv = s_ref[...]
        dwc_ref[0] += _dot(sv, dyv, _TN)
        dd_ref[...] += jnp.sum(dyv * u, axis=0, keepdims=True)
        first_chunk = t == nt - 1
        halo = jnp.where(first_chunk, 0.0, halo_ref[SUBLANE - 1:SUBLANE, :])
        row = lax.broadcasted_iota(jnp.int32, sv.shape, 0)
        sp = jnp.where(row == 0, jnp.broadcast_to(halo, sv.shape), pltpu.roll(sv, 1, 0))
        gr, gi = gv[:, :half], gv[:, half:]
        sr, si = sp[:, :half], sp[:, half:]
        dlr = jnp.sum(gr * sr + gi * si, axis=0, keepdims=True)
        dli = jnp.sum(gi * sr - gr * si, axis=0, keepdims=True)
        dlam_ref[0] += jnp.concatenate([dlr, dli], axis=1)

    rev = lambda t: nt - 1 - t
    return _call(
        body, name=name, grid=(S5_LANE_BLOCKS, nt),
        in_specs=[pl.BlockSpec((tt, LANE), lambda b, t: (rev(t), b + u_off)),
                  pl.BlockSpec((tt, LANE), lambda b, t: (rev(t), b)),
                  pl.BlockSpec((tt, w2), lambda b, t: (rev(t), b)),
                  pl.BlockSpec((SUBLANE, w2), lambda b, t: (jnp.maximum(rev(t) * grp_per_chunk - 1, 0), b)),
                  pl.BlockSpec((1, LANE, w2), lambda b, t: (b, 0, 0)),
                  pl.BlockSpec((1, w2, LANE), lambda b, t: (b, 0, 0)),
                  pl.BlockSpec((1, LANE), lambda b, t: (0, b)),
                  pl.BlockSpec((1, 4, SUBLANE, w2), lambda b, t: (b, 0, 0, 0))],
        out_specs=[pl.BlockSpec((tt, LANE), lambda b, t: (rev(t), b)),
                   pl.BlockSpec((1, LANE, w2), lambda b, t: (b, 0, 0)),
                   pl.BlockSpec((1, w2, LANE), lambda b, t: (b, 0, 0)),
                   pl.BlockSpec((1, 1, w2), lambda b, t: (b, 0, 0)),
                   pl.BlockSpec((1, LANE), lambda b, t: (0, b))],
        out_shape=[jax.ShapeDtypeStruct((l, D_HALF), BF16),
                   jax.ShapeDtypeStruct((S5_LANE_BLOCKS, LANE, w2), F32),
                   jax.ShapeDtypeStruct((S5_LANE_BLOCKS, w2, LANE), F32),
                   jax.ShapeDtypeStruct((S5_LANE_BLOCKS, 1, w2), F32),
                   jax.ShapeDtypeStruct((1, D_HALF), F32)],
        scratch_shapes=[pltpu.VMEM((tt, w2), F32), pltpu.VMEM((SUBLANE, w2), F32)],
        args=[proj, dy, states, states, wb, wc, d_skip, sc_rev], parallel=False, exchanges=exchanges)


def _glu_fwd(ypre, wg, bg, cat, *, name):
    l, d = ypre.shape
    tl = _pick_tile(l, 512, SUBLANE)

    def body(y_ref, w_ref, b_ref, cat_ref, o_ref):
        yg = _gelu(y_ref[...])
        o_ref[...] = (yg * _sigmoid(_dot(yg, w_ref[...], _NN) + b_ref[...])).astype(BF16)

    row = pl.BlockSpec((tl, d), lambda i: (i, 0))
    return pl.pallas_call(
        body, name=name, grid=(l // tl,),
        in_specs=[row, pl.BlockSpec((d, d), lambda i: (0, 0)), pl.BlockSpec((1, d), lambda i: (0, 0)), _ANY],
        out_specs=pl.BlockSpec((tl, d), lambda i: (i, 1)), out_shape=jax.ShapeDtypeStruct(cat.shape, cat.dtype),
        input_output_aliases={3: 0}, compiler_params=_PAR)(ypre, wg, bg, cat)


def _glu_bwd(dcat, ypre, wg, bg, *, name):
    l, d = ypre.shape
    tl = _pick_tile(l, 512, SUBLANE)

    def body(dy_ref, y_ref, w_ref, b_ref, dyp_ref, dw_ref, db_ref):
        @pl.when(pl.program_id(0) == 0)
        def _():
            dw_ref[...] = jnp.zeros_like(dw_ref)
            db_ref[...] = jnp.zeros_like(db_ref)
        yp = y_ref[...]
        dyb = dy_ref[...]
        yg = _gelu(yp)
        sg = _sigmoid(_dot(yg, w_ref[...], _NN) + b_ref[...])
        dz = dyb * yg * sg * (1.0 - sg)
        dyg = dyb * sg + _dot(dz, w_ref[...], _NT)
        dyp_ref[...] = dyg * _gelu_grad(yp)
        dw_ref[...] += _dot(yg, dz, _TN)
        db_ref[...] += jnp.sum(dz, axis=0, keepdims=True)

    row = pl.BlockSpec((tl, d), lambda i: (i, 0))
    mat = pl.BlockSpec((d, d), lambda i: (0, 0))
    vec = pl.BlockSpec((1, d), lambda i: (0, 0))
    return pl.pallas_call(
        body, name=name, grid=(l // tl,),
        in_specs=[pl.BlockSpec((tl, d), lambda i: (i, 1)), row, mat, vec], out_specs=[row, mat, vec],
        out_shape=[jax.ShapeDtypeStruct((l, d), F32), jax.ShapeDtypeStruct((d, d), F32),
                   jax.ShapeDtypeStruct((1, d), F32)],
        compiler_params=_ARB)(dcat, ypre, wg, bg)


def _window_sum(x, w, trailing):
    s, d = x, 1
    while d < w:
        s = s + (_shift_dn(s, d) if trailing else _shift_up(s, d))
        d *= 2
    return s


def _window_count(shape, w):
    row = lax.broadcasted_iota(jnp.int32, shape, 0)
    return jnp.minimum(row + 1, w).astype(F32)


def _pool_fwd(proj, pw, scale, *, name):
    l = proj.shape[0]
    ng = len(POOL_WINDOWS)

    def body(z_ref, w_ref, sc_ref, o_ref):
        z = z_ref[...]
        for k, w in enumerate(POOL_WINDOWS):
            @pl.when(pl.program_id(0) == k)
            def _():
                pooled = _window_sum(z, w, True) / _window_count(z.shape, w) - z
                o_ref[...] = (_dot(pooled, w_ref[0], _NN) * sc_ref[...]).astype(BF16)

    col = pl.BlockSpec((l, LANE), lambda g: (0, g))
    return pl.pallas_call(
        body, name=name, grid=(ng,),
        in_specs=[col, pl.BlockSpec((1, LANE, LANE), lambda g: (g, 0, 0)), pl.BlockSpec((1, LANE), lambda g: (0, g))],
        out_specs=col, out_shape=jax.ShapeDtypeStruct((l, 2 * D_HALF), BF16), compiler_params=_PAR)(proj, pw, scale)


def _pool_bwd(proj, dcat, pw, scale, *, name):
    l = proj.shape[0]
    ng = len(POOL_WINDOWS)

    def body(z_ref, dy_ref, w_ref, sc_ref, dz_ref, dw_ref, dsc_ref):
        z = z_ref[...]
        dy = dy_ref[...]
        for k, w in enumerate(POOL_WINDOWS):
            @pl.when(pl.program_id(0) == k)
            def _():
                cnt = _window_count(z.shape, w)
                pooled = _window_sum(z, w, True) / cnt - z
                ypre = _dot(pooled, w_ref[0], _NN)
                dsc_ref[...] = jnp.sum(dy * ypre, axis=0, keepdims=True)
                dyp = dy * sc_ref[...]
                dw_ref[0] = _dot(pooled, dyp, _TN)
                dpool = _dot(dyp, w_ref[0], _NT)
                dz_ref[...] = (_window_sum(dpool / cnt, w, False) - dpool).astype(BF16)

    col = pl.BlockSpec((l, LANE), lambda g: (0, g))
    mat = pl.BlockSpec((1, LANE, LANE), lambda g: (g, 0, 0))
    vec = pl.BlockSpec((1, LANE), lambda g: (0, g))
    return pl.pallas_call(
        body, name=name, grid=(ng,), in_specs=[col, col, mat, vec], out_specs=[col, mat, vec],
        out_shape=[jax.ShapeDtypeStruct((l, D_HALF), BF16), jax.ShapeDtypeStruct((ng, LANE, LANE), F32),
                   jax.ShapeDtypeStruct((1, D_HALF), F32)],
        compiler_params=_PAR)(proj, dcat, pw, scale)


def _tril_mask():
    r = lax.broadcasted_iota(jnp.int32, (CHUNK, CHUNK), 0)
    c = lax.broadcasted_iota(jnp.int32, (CHUNK, CHUNK), 1)
    return r >= c


def _sgu_fwd(proj, ng, sw, sb_t, cat, *, name):
    l = proj.shape[0]
    tl = _pick_tile(l, 512, CHUNK)

    def body(su_ref, sv_ref, g_ref, w_ref, b_ref, cat_ref, o_ref):
        su = _gelu(su_ref[...])
        sv = _gelu(sv_ref[...])
        r = lax.rsqrt(jnp.mean(sv * sv, axis=-1, keepdims=True) + EPS)
        v = sv * r * g_ref[...]
        mask = _tril_mask()
        for h in range(SGU_HEADS):
            wm = jnp.where(mask, w_ref[h], 0.0)
            cs = slice(h * LANE, (h + 1) * LANE)
            for n in range(tl // CHUNK):
                rs = slice(n * CHUNK, (n + 1) * CHUNK)
                mixed = _dot(wm, v[rs, cs], _NN) + b_ref[:, h:h + 1]
                o_ref[rs, cs] = (su[rs, cs] * mixed).astype(BF16)

    blk = lambda c: pl.BlockSpec((tl, D_HALF), lambda i: (i, c))
    return pl.pallas_call(
        body, name=name, grid=(l // tl,),
        in_specs=[blk(1), blk(2), pl.BlockSpec((1, D_HALF), lambda i: (0, 0)),
                  pl.BlockSpec((SGU_HEADS, CHUNK, CHUNK), lambda i: (0, 0, 0)),
                  pl.BlockSpec((CHUNK, SGU_HEADS), lambda i: (0, 0)), _ANY],
        out_specs=blk(1), out_shape=jax.ShapeDtypeStruct(cat.shape, cat.dtype), input_output_aliases={5: 0},
        compiler_params=_PAR)(proj, proj, ng, sw, sb_t, cat)


def _sgu_bwd(proj, dcat, ng, sw, sb_t, *, name):
    l = proj.shape[0]
    tl = _pick_tile(l, 512, CHUNK)

    def body(su_ref, sv_ref, dy_ref, g_ref, w_ref, b_ref, dsu_ref, dsv_ref, dw_ref, dbm_ref, dng_ref, dv_scr):
        @pl.when(pl.program_id(0) == 0)
        def _():
            dw_ref[...] = jnp.zeros_like(dw_ref)
            dbm_ref[...] = jnp.zeros_like(dbm_ref)
            dng_ref[...] = jnp.zeros_like(dng_ref)
        su_pre = su_ref[...]
        sv_pre = sv_ref[...]
        su = _gelu(su_pre)
        sv = _gelu(sv_pre)
        gsu = _gelu_grad(su_pre)
        gv = g_ref[...]
        r = lax.rsqrt(jnp.mean(sv * sv, axis=-1, keepdims=True) + EPS)
        v = sv * r * gv
        dy = dy_ref[...]
        mask = _tril_mask()
        for h in range(SGU_HEADS):
            wm = jnp.where(mask, w_ref[h], 0.0)
            cs = slice(h * LANE, (h + 1) * LANE)
            dw_acc = jnp.zeros((CHUNK, CHUNK), F32)
            db_acc = jnp.zeros((CHUNK, LANE), F32)
            for n in range(tl // CHUNK):
                rs = slice(n * CHUNK, (n + 1) * CHUNK)
                vb = v[rs, cs]
                mixed = _dot(wm, vb, _NN) + b_ref[:, h:h + 1]
                dyb = dy[rs, cs]
                dsu_ref[rs, cs] = (dyb * mixed * gsu[rs, cs]).astype(BF16)
                dmix = dyb * su[rs, cs]
                db_acc = db_acc + dmix
                dw_acc = dw_acc + _dot(dmix, vb, _NT)
                dv_scr[rs, cs] = _dot(wm, dmix, _TN)
            dw_ref[h] += jnp.where(mask, dw_acc, 0.0)
            dbm_ref[h] += db_acc
        dv = dv_scr[...]
        a = dv * gv
        m = jnp.mean(a * sv, axis=-1, keepdims=True)
        dsv = r * a - sv * (r * r * r) * m
        dng_ref[...] += jnp.sum(dv * sv * r, axis=0, keepdims=True)
        dsv_ref[...] = (dsv * _gelu_grad(sv_pre)).astype(BF16)

    blk = lambda c: pl.BlockSpec((tl, D_HALF), lambda i: (i, c))
    mats = pl.BlockSpec((SGU_HEADS, CHUNK, CHUNK), lambda i: (0, 0, 0))
    vec = pl.BlockSpec((1, D_HALF), lambda i: (0, 0))
    piece = jax.ShapeDtypeStruct((l, D_HALF), BF16)
    mshape = jax.ShapeDtypeStruct((SGU_HEADS, CHUNK, CHUNK), F32)
    return pl.pallas_call(
        body, name=name, grid=(l // tl,),
        in_specs=[blk(1), blk(2), blk(1), vec, mats, pl.BlockSpec((CHUNK, SGU_HEADS), lambda i: (0, 0))],
        out_specs=[blk(0), blk(0), mats, mats, vec],
        out_shape=[piece, piece, mshape, mshape, jax.ShapeDtypeStruct((1, D_HALF), F32)],
        scratch_shapes=[pltpu.VMEM((tl, D_HALF), F32)],
        compiler_params=_ARB)(proj, proj, dcat, ng, sw, sb_t)


def _s5_params(w):
    prep_args = (w["ssm_log_step"], w["ssm_a_re"], w["ssm_a_im"], w["ssm_b_re"], w["ssm_b_im"])
    (lr, li, br, bi), prep_vjp = jax.vjp(jax.vmap(_s5_prep), *prep_args)
    wb, wc = jax.vmap(_s5_mats)(br, bi, w["ssm_c_re"], w["ssm_c_im"])
    consts = lambda reverse: jax.vmap(functools.partial(_s5_scan_consts, reverse=reverse))(*prep_args[:3])
    return dict(wb=wb.astype(BF16), wc=wc.astype(BF16), d=w["ssm_d"][:, None, :], sc=consts(False),
                sc_rev=consts(True), prep_vjp=prep_vjp)


def _s5_param_grads(s5, dwb, dwc, dlam, dd):
    dbr, dbi, dcr, dci = jax.vmap(_s5_mats_bwd)(dwb, dwc)
    n = dlam.shape[0]
    dlr = dlam[:, :, 0, :S5_STATE_LANES].reshape(n, N_SSM_GROUPS, SSM_STATE)
    dli = dlam[:, :, 0, S5_STATE_LANES:].reshape(n, N_SSM_GROUPS, SSM_STATE)
    dls, dar, dai, db_re, db_im = s5["prep_vjp"]((dlr, dli, dbr, dbi))
    return dict(ssm_log_step=dls, ssm_a_re=dar, ssm_a_im=dai, ssm_b_re=db_re, ssm_b_im=db_im, ssm_c_re=dcr,
                ssm_c_im=dci, ssm_d=dd[:, 0, :])


TILES = {
    "mm_up": (4096, 512), "mm_up_dw": (1024, 1408), "mm_down_dx": (1024, 2816),
    "mm_down": (512, 1024), "mm_down_dw": (256, 1024), "mm_even_in": (2048, 1024), "mm_odd_in": (2048, 768),
    "mm_mix_out": (1024, 1024), "mm_mix_out_dx": (2048, 1024), "mm_mix_out_dw": (1024, 512),
    "mm_even_in_dw": (1024, 512), "mm_odd_in_dw": (1024, 512),
}


def _layer_weights(i):
    j = i // 2
    mixer = [("even_w_in", j), ("even_w_out", j), ("ssm_glu_w", j)] if i % 2 == 0 else [("odd_w_in", j),
                                                                                         ("odd_w_out", j)]
    return dict(w_in=mixer[0], w_out=mixer[1], glu=mixer[2:], up=("ffn_w_up", i), down=("ffn_w_down", i))


class _LocalWeights:
    def __init__(self, w):
        self.w, self.grads = w, {}

    def carried_by(self, stage, i):
        return []

    def delivered(self, stage, i, outs):
        pass

    def weight(self, key):
        return self.w[key[0]][key[1]]

    def grad(self, key, dw):
        self.grads[key] = dw


class _ShardedWeights:
    def __init__(self, shards):
        self.shards = shards
        self.full, self.halves, self.pending, self.scattered = {}, {}, {}, {}

    def start(self, others):
        keys = [_layer_weights(0)["w_in"]]
        outs = _exchange_only(others + [self._gather(k) for k in keys], name="ag_first")
        self._take(keys, outs[len(others):])
        return outs[:len(others)]

    def _gather(self, key):
        shard = self.shards[key[0]][key[1]]
        if len(key) == 2:
            return _Gather(shard)
        rows = shard.shape[0] // 2
        buf = self.halves.get(key[:2])
        if buf is None:
            buf = lax.empty((N_DEV,) + shard.shape, shard.dtype)
        return _Gather(shard[key[2] * rows:(key[2] + 1) * rows], into=buf, row0=key[2] * rows)

    def _take(self, keys, outs):
        for key, got in zip(keys, outs):
            if len(key) == 3 and key[:2] not in self.halves:
                self.halves[key[:2]] = got
                continue
            if BIG[key[0]] == 2:
                self.full[key[:2]] = jnp.swapaxes(got, 0, 1).reshape(got.shape[1], -1)
            else:
                self.full[key[:2]] = got.reshape(-1, got.shape[2])

    def _plan(self, stage, i):
        cur = _layer_weights(i)
        nxt = _layer_weights(i + 1) if i + 1 < DEPTH else None
        has_scan = lambda k: k % 2 == 0
        none = ([], [])
        up_half = lambda h: ([(*nxt["up"], h)], []) if nxt and not has_scan(i + 1) else none
        return {
            "mm_in": ([cur["w_out"], *cur["glu"]], []) if i == 0 else none,
            "s5_fwd": ([cur["up"]], []),
            "mm_up": ([cur["down"]] + ([nxt["w_in"], nxt["w_out"], *nxt["glu"]] if nxt else []), []),
            "ffn_act": up_half(0),
            "mm_down": up_half(1),
            "ffn_act_bwd": ([], [cur["down"]] + ([] if not nxt else [nxt["w_in"]] if not has_scan(i + 1)
                                                 else [nxt["w_out"], *nxt["glu"]])),
            "mm_up_dx": none if not nxt else ([], [(*nxt["up"], 0)]) if not has_scan(i + 1) else ([], [nxt["w_in"]]),
            "mm_up_dw": ([], [(*nxt["up"], 1)]) if nxt and not has_scan(i + 1) else none,
            "s5_bwd": ([], [cur["up"]] + ([nxt["w_out"]] if nxt else [])),
            "mm_in_dw": ([], [cur["w_out"], *cur["glu"]]) if i == 0 else none,
            "mm_in_dx": ([], [cur["w_in"]]) if i == 0 else none,
        }[stage]

    def _scatter(self, key):
        name, layer = key[:2]
        layers, rows, cols = self.shards[name].shape
        if len(key) == 3:
            src = self.pending[key[:2]] if key[2] == 0 else self.pending.pop(key[:2])
            src = src[:, key[2] * (rows // 2):(key[2] + 1) * (rows // 2)]
            row0 = layer * rows + key[2] * (rows // 2)
        else:
            src, row0 = self.pending.pop(key), layer * rows
        if name not in self.scattered:
            self.scattered[name] = lax.empty((N_DEV, layers * rows, cols), src.dtype)
        return _Scatter(src, into=self.scattered[name], row0=row0)

    def carried_by(self, stage, i):
        gather, scatter = self._plan(stage, i)
        return [self._gather(k) for k in gather] + [self._scatter(k) for k in scatter]

    def delivered(self, stage, i, outs):
        gather, scatter = self._plan(stage, i)
        self._take(gather, outs[:len(gather)])
        for key, buf in zip(scatter, outs[len(gather):]):
            self.scattered[key[0]] = buf

    def weight(self, key):
        return self.full[key]

    def grad(self, key, dw):
        self.pending[key] = _to_dest_major(dw, BIG[key[0]] - 1).astype(BF16)

    def finish(self, carrier, others):
        keys = list(self.pending)
        res, outs = carrier(others + [self._scatter(k) for k in keys])
        for (name, _), buf in zip(keys, outs[len(others):]):
            self.scattered[name] = buf
        return res, outs[:len(others)]


def _device_step(x, tgt, w, comm):
    saved = []
    s5 = _s5_params(w)
    h = _rms_fwd(x, w["norm_mix_g"][0:1], name="rms_fwd")
    for i in range(DEPTH):
        j = i // 2
        lw = _layer_weights(i)
        if i % 2 == 0:
            proj = _carry(comm, "mm_in", i, _mm, h, comm.weight(lw["w_in"]), "nn", name="mm_even_in")
            ya = _sconv_fwd(proj, w["even_conv_w"][j], name="sconv_fwd")
            states, ypre = _carry(comm, "s5_fwd", i, _s5_fwd, proj, s5["wb"][j], s5["wc"][j], s5["d"][j],
                                  s5["sc"][j], name="s5_fwd")
            cat = _glu_fwd(ypre, comm.weight(lw["glu"][0]), w["ssm_glu_b"][j][None, :], ya, name="glu_fwd")
            mix = (states, ypre)
        else:
            proj = _carry(comm, "mm_in", i, _mm, h, comm.weight(lw["w_in"]), "nn", name="mm_odd_in")
            yc = _pool_fwd(proj, w["pool_w"][j], w["pool_scale"][j][None, :], name="pool_fwd")
            sb_t = jnp.transpose(w["sgu_b"][j])
            cat = _sgu_fwd(proj, w["sgu_norm_g"][j][None, :], w["sgu_w"][j], sb_t, yc, name="sgu_fwd")
            mix = (sb_t,)
        x1, h2 = _mm(cat, comm.weight(lw["w_out"]), "nn", add=x, norm=(w["norm_ffn_g"], i), name="mm_mix_out")
        up = _carry(comm, "mm_up", i, _mm, h2, comm.weight(lw["up"]), "nn", name="mm_up")
        act, gv = _carry(comm, "ffn_act", i, _ffn_act, up, w["ffn_conv_w"][i], w["ffn_conv_b"][i:i + 1],
                         name="ffn_act")
        if i + 1 < DEPTH:
            x2, h_next = _carry(comm, "mm_down", i, _mm, act, comm.weight(lw["down"]), "nn", add=x1,
                                norm=(w["norm_mix_g"], i + 1), name="mm_down")
        else:
            x2, h_next = _mm(act, comm.weight(lw["down"]), "nn", add=x1, name="mm_down_last"), None
        saved.append((x, h, proj, cat, x1, h2, up, gv, act, mix))
        x, h = x2, h_next

    loss, dx, dxb, dgf = _loss_head(x, w["norm_final_g"][None, :], tgt, name="loss_head")
    per_layer = {}
    s5_grads = []

    def put(name, idx, val):
        per_layer.setdefault(name, {})[idx] = val

    for i in reversed(range(DEPTH)):
        j = i // 2
        lw = _layer_weights(i)
        x0, h, proj, cat, x1, h2, up, gv, act, mix = saved[i]
        dact = _mm(dxb, comm.weight(lw["down"]), "nt", name="mm_down_dx")
        comm.grad(lw["down"], _mm(act, dxb, "tn", out_dtype=BF16, name="mm_down_dw"))
        dup, dcw, dcb = _carry(comm, "ffn_act_bwd", i, _ffn_act_bwd, up, gv, dact, w["ffn_conv_w"][i],
                               name="ffn_act_bwd")
        put("ffn_conv_w", i, dcw)
        put("ffn_conv_b", i, dcb[0])
        dx1, dx1b, dg2 = _carry(comm, "mm_up_dx", i, _mm, dup, comm.weight(lw["up"]), "nt", tm_cap=512, tn_cap=D_MODEL,
                                norm_bwd=(x1, w["norm_ffn_g"], i, dx), name="mm_up_dx")
        comm.grad(lw["up"], _carry(comm, "mm_up_dw", i, _mm, h2, dup, "tn", out_dtype=BF16,
                                   name="mm_up_dw"))
        put("norm_ffn_g", i, dg2[0])
        dcat = _mm(dx1b, comm.weight(lw["w_out"]), "nt", name="mm_mix_out_dx")
        comm.grad(lw["w_out"], _mm(cat, dx1b, "tn", out_dtype=BF16, name="mm_mix_out_dw"))
        if i % 2 == 0:
            states, ypre = mix
            dxa, dba, dca, dcw_a = _sconv_bwd(proj, dcat, w["even_conv_w"][j], name="sconv_bwd")
            put("even_conv_w", j, dcw_a)
            dypre, dwg, dbg = _glu_bwd(dcat, ypre, comm.weight(lw["glu"][0]), w["ssm_glu_b"][j][None, :],
                                       name="glu_bwd")
            comm.grad(lw["glu"][0], dwg)
            put("ssm_glu_b", j, dbg[0])
            du, dwb, dwc, dlam, dd = _carry(comm, "s5_bwd", i, _s5_bwd, proj, dypre, states, s5["wb"][j], s5["wc"][j],
                                            s5["d"][j], s5["sc_rev"][j], name="s5_bwd")
            s5_grads.insert(0, (dwb, dwc, dlam, dd))
            dproj = jnp.concatenate([dxa, dba, dca, du], axis=1)
            in_name = "mm_even_in"
        else:
            (sb_t,) = mix
            dz, dpw, dps = _pool_bwd(proj, dcat, w["pool_w"][j], w["pool_scale"][j][None, :], name="pool_bwd")
            put("pool_w", j, dpw)
            put("pool_scale", j, dps[0])
            dsu, dsv, dsw, dbm, dng = _sgu_bwd(proj, dcat, w["sgu_norm_g"][j][None, :], w["sgu_w"][j], sb_t,
                                               name="sgu_bwd")
            put("sgu_w", j, dsw)
            put("sgu_b", j, jnp.sum(dbm, axis=-1))
            put("sgu_norm_g", j, dng[0])
            dproj = jnp.concatenate([dz, dsu, dsv], axis=1)
            in_name = "mm_odd_in"
        comm.grad(lw["w_in"], _carry(comm, "mm_in_dw", i, _mm, h, dproj, "tn", out_dtype=BF16, name=in_name + "_dw"))
        dx, dxb, dg1 = _carry(comm, "mm_in_dx", i, _mm, dproj, comm.weight(lw["w_in"]), "nt", tn_cap=D_MODEL,
                              norm_bwd=(x0, w["norm_mix_g"], i, dx1), name=in_name + "_dx")
        put("norm_mix_g", i, dg1[0])

    grads = {nm: [vals[k] for k in sorted(vals)] for nm, vals in per_layer.items()}
    grads.update({nm: jnp.stack(grads[nm]) for nm in SMALL})
    grads["norm_final_g"] = dgf[0]
    grads.update(_s5_param_grads(s5, *[jnp.stack(parts) for parts in zip(*s5_grads)]))
    return loss, dx, grads


def _carry(comm, stage, i, fn, *args, **kwargs):
    exchanges = comm.carried_by(stage, i)
    if not exchanges:
        return fn(*args, **kwargs)
    out, moved = fn(*args, exchanges=exchanges, **kwargs)
    comm.delivered(stage, i, moved)
    return out


def _sum_parts(parts, *, name):
    g, r, c = parts.shape
    tr = _pick_tile(r, max(16, EXCHANGE_BLOCK_ELEMS // c), 16)

    def body(p_ref, o_ref):
        acc = p_ref[0].astype(F32)
        for k in range(1, g):
            acc = acc + p_ref[k].astype(F32)
        o_ref[...] = acc

    return pl.pallas_call(
        body, name=name, grid=(r // tr,), in_specs=[pl.BlockSpec((g, tr, c), lambda i: (0, i, 0))],
        out_specs=pl.BlockSpec((tr, c), lambda i: (i, 0)), out_shape=jax.ShapeDtypeStruct((r, c), F32),
        compiler_params=_PAR)(parts)


def _adamw(w, m, v, g_parts, *, name, exchanges=()):
    r, c = w.shape
    g = g_parts.shape[0]
    tc = _pick_tile(c, 8192, LANE)
    tr = _pick_tile(r, max(16, (1 << 18) // tc), 16)
    c1 = 1.0 - ADAM_B1 ** ADAM_STEP
    c2 = 1.0 - ADAM_B2 ** ADAM_STEP

    def body(w_ref, m_ref, v_ref, g_ref, go_ref, d_ref, mo_ref, vo_ref):
        grad = g_ref[0].astype(F32)
        for k in range(1, g):
            grad = grad + g_ref[k].astype(F32)
        m_new = ADAM_B1 * m_ref[...] + (1.0 - ADAM_B1) * grad
        v_new = ADAM_B2 * v_ref[...] + (1.0 - ADAM_B2) * (grad * grad)
        go_ref[...] = grad
        mo_ref[...] = m_new
        vo_ref[...] = v_new
        d_ref[...] = -ADAM_LR * ((m_new / c1) / (jnp.sqrt(v_new / c2) + ADAM_EPS) + ADAM_WD * w_ref[...])

    blk = pl.BlockSpec((tr, tc), lambda i, j: (i, j))
    out = jax.ShapeDtypeStruct((r, c), F32)
    return _call(body, name=name, grid=(r // tr, c // tc),
                 in_specs=[blk, blk, blk, pl.BlockSpec((g, tr, tc), lambda i, j: (0, i, j))],
                 out_specs=[blk, blk, blk, blk], out_shape=[out, out, out, out], args=[w, m, v, g_parts],
                 exchanges=exchanges)


WEIGHT_NAMES = ['norm_mix_g', 'even_w_in', 'even_conv_w', 'ssm_log_step', 'ssm_a_re', 'ssm_a_im', 'ssm_b_re',
                'ssm_b_im', 'ssm_c_re', 'ssm_c_im', 'ssm_d', 'ssm_glu_w', 'ssm_glu_b', 'even_w_out', 'odd_w_in',
                'pool_w', 'pool_scale', 'sgu_norm_g', 'sgu_w', 'sgu_b', 'odd_w_out', 'norm_ffn_g', 'ffn_w_up',
                'ffn_conv_w', 'ffn_conv_b', 'ffn_w_down', 'norm_final_g']
BIG = {'even_w_in': 2, 'ssm_glu_w': 1, 'even_w_out': 1, 'odd_w_in': 2, 'odd_w_out': 1, 'ffn_w_up': 2,
       'ffn_w_down': 1}
SMALL = {'even_conv_w': 2, 'pool_scale': 1, 'sgu_norm_g': 1, 'ffn_conv_w': 2}
SMALL_ROWS = 16


def _pad_to(n, q):
    return -(-n // q) * q


def _pack(arrays, dtype, rows, lead=()):
    flat = [a.reshape(lead + (-1,)).astype(dtype) for a in arrays]
    n = sum(f.shape[-1] for f in flat)
    pad = _pad_to(n, rows * LANE) - n
    if pad:
        flat.append(jnp.zeros(lead + (pad,), dtype))
    return jnp.concatenate(flat, axis=-1).reshape(lead + (rows, -1))


def _unpack(buf, shapes, lead=()):
    flat = buf.reshape(lead + (-1,))
    out, off = [], 0
    for shp in shapes:
        n = math.prod(shp)
        out.append(flat[..., off:off + n].reshape(lead + tuple(shp)))
        off += n
    return out


def _to_dest_major(full, axis):
    shp = full.shape
    split = full.reshape(shp[:axis] + (N_DEV, shp[axis] // N_DEV) + shp[axis + 1:])
    return jnp.moveaxis(split, axis, 0)


def _from_dest_major(blocks, axis):
    moved = jnp.moveaxis(blocks, 0, axis)
    shp = moved.shape
    return moved.reshape(shp[:axis] + (shp[axis] * shp[axis + 1],) + shp[axis + 2:])


def _rows_2d(a):
    return a.reshape(-1, a.shape[-1])


def kernel(x, norm_mix_g, even_w_in, even_conv_w, ssm_log_step, ssm_a_re, ssm_a_im, ssm_b_re, ssm_b_im, ssm_c_re, ssm_c_im, ssm_d, ssm_glu_w, ssm_glu_b, even_w_out, odd_w_in, pool_w, pool_scale, sgu_norm_g, sgu_w, sgu_b, odd_w_out, norm_ffn_g, ffn_w_up, ffn_conv_w, ffn_conv_b, ffn_w_down, norm_final_g, loss_target, m_norm_mix_g, m_even_w_in, m_even_conv_w, m_ssm_log_step, m_ssm_a_re, m_ssm_a_im, m_ssm_b_re, m_ssm_b_im, m_ssm_c_re, m_ssm_c_im, m_ssm_d, m_ssm_glu_w, m_ssm_glu_b, m_even_w_out, m_odd_w_in, m_pool_w, m_pool_scale, m_sgu_norm_g, m_sgu_w, m_sgu_b, m_odd_w_out, m_norm_ffn_g, m_ffn_w_up, m_ffn_conv_w, m_ffn_conv_b, m_ffn_w_down, m_norm_final_g, v_norm_mix_g, v_even_w_in, v_even_conv_w, v_ssm_log_step, v_ssm_a_re, v_ssm_a_im, v_ssm_b_re, v_ssm_b_im, v_ssm_c_re, v_ssm_c_im, v_ssm_d, v_ssm_glu_w, v_ssm_glu_b, v_even_w_out, v_odd_w_in, v_pool_w, v_pool_scale, v_sgu_norm_g, v_sgu_w, v_sgu_b, v_odd_w_out, v_norm_ffn_g, v_ffn_w_up, v_ffn_conv_w, v_ffn_conv_b, v_ffn_w_down, v_norm_final_g):
    given = dict(locals())
    wts = {n: given[n] for n in WEIGHT_NAMES}
    mom = {n: given["m_" + n] for n in WEIGHT_NAMES}
    var = {n: given["v_" + n] for n in WEIGHT_NAMES}
    repl = [n for n in WEIGHT_NAMES if n not in BIG and n not in SMALL]

    small_shapes = [wts[n].shape for n in SMALL]
    comm = _ShardedWeights({n: wts[n].astype(BF16) for n in BIG})
    (small_all,) = comm.start([_Gather(_pack([wts[n] for n in SMALL], F32, SMALL_ROWS))])
    full = {n: wts[n] for n in repl}
    for n, blocks in zip(SMALL, _unpack(small_all, small_shapes, lead=(N_DEV,))):
        full[n] = _from_dest_major(blocks, SMALL[n])

    loss, dx, grads = _device_step(x[0], loss_target[0], full, comm)

    repl_shapes = [wts[n].shape for n in repl]
    pieces = [p.reshape(-1) for n in repl for p in (grads[n] if isinstance(grads[n], list) else [grads[n]])]
    repl_flat = jnp.concatenate(pieces + [loss.reshape(-1)])
    n_repl = repl_flat.shape[0]
    chunk = _pad_to(-(-n_repl // N_DEV), SMALL_ROWS * LANE)
    repl_flat = jnp.pad(repl_flat, (0, N_DEV * chunk - n_repl))
    small_part = _pack([_to_dest_major(grads[n], SMALL[n]) for n in SMALL], F32, SMALL_ROWS, lead=(N_DEV,))
    small_cols = small_part.shape[2]
    small_scatter = _Scatter(
        jnp.concatenate([small_part, repl_flat.reshape(N_DEV, SMALL_ROWS, chunk // SMALL_ROWS)], axis=2))

    _, (small_rs,) = comm.finish(lambda exchanges: (None, _exchange_only(exchanges, name="rs_last")), [small_scatter])
    small_sum = _sum_parts(small_rs, name="rs_sum_small")
    (repl_all,) = _exchange_only([_Gather(small_sum[:, small_cols:])], name="ag_repl")
    repl_sum = repl_all.reshape(-1)
    total_loss = repl_sum[n_repl - 1]

    out = {}
    for n in BIG:
        res = _adamw(_rows_2d(wts[n]), _rows_2d(mom[n]), _rows_2d(var[n]), comm.scattered[n], name="adamw_" + n)
        out[n] = [r.reshape(wts[n].shape) for r in res]

    def small_vec(shard_part, repl_part):
        flat = jnp.concatenate([shard_part.reshape(-1), repl_part])
        return flat.reshape(SMALL_ROWS, -1)

    def small_tree(tree):
        tail = jnp.concatenate([tree[n].reshape(-1) for n in repl])
        tail = jnp.pad(tail, (0, N_DEV * chunk - tail.shape[0]))
        return small_vec(_pack([tree[n] for n in SMALL], F32, SMALL_ROWS), tail)

    res = _adamw(small_tree(wts), small_tree(mom), small_tree(var), small_vec(small_sum[:, :small_cols], repl_sum)[None],
                 name="adamw_small")
    n_small = SMALL_ROWS * small_cols
    for k, r in enumerate(res):
        flat = r.reshape(-1)
        shard = _unpack(flat[:n_small], small_shapes)
        rest = _unpack(flat[n_small:], repl_shapes)
        for n, val in zip(SMALL, shard):
            out.setdefault(n, [None] * 4)[k] = val
        for n, val in zip(repl, rest):
            out.setdefault(n, [None] * 4)[k] = val

    grad_x = dx[None]
    return (total_loss, grad_x, *[out[n][0] for n in WEIGHT_NAMES], *[out[n][1] for n in WEIGHT_NAMES],
            *[out[n][2] for n in WEIGHT_NAMES], *[out[n][3] for n in WEIGHT_NAMES])
```

```python
import functools
import math

import jax
import jax.numpy as jnp
from jax import lax
from jax.experimental import pallas as pl
from jax.experimental.pallas import tpu as pltpu

F32 = jnp.float32
BF16 = jnp.bfloat16

D_MODEL = 1024
DEPTH = 4
D_HALF = D_MODEL // 2
SSM_GROUP = 16
N_SSM_GROUPS = D_HALF // SSM_GROUP
SSM_STATE = 64
POOL_WINDOWS = (2, 4, 8, 16)
SGU_HEADS = 4
CHUNK = 128
D_FF = 2816
CONV_WIDTH = 3
EPS = 1e-6
N_DEV = 8

ADAM_LR = 0.001
ADAM_B1 = 0.9
ADAM_B2 = 0.999
ADAM_EPS = 1e-08
ADAM_WD = 0.01
ADAM_STEP = 10

LANE = 128
SUBLANE = 8
S5_LANE_BLOCKS = D_HALF // LANE
S5_STATE_LANES = (N_SSM_GROUPS // S5_LANE_BLOCKS) * SSM_STATE
S5_TIME_CHUNK = 512
EXCHANGE_BLOCK_ELEMS = 1 << 20

GELU_K = math.sqrt(2.0 / math.pi)
GELU_C = 0.044715

_ARB = pltpu.CompilerParams(dimension_semantics=("arbitrary",))
_PAR = pltpu.CompilerParams(dimension_semantics=("parallel",))


def _pick_tile(n, cap, mult):
    if n <= cap:
        return n
    best = None
    for t in range(mult, cap + 1, mult):
        if n % t == 0:
            best = t
    assert best is not None, (n, cap, mult)
    return best


_MESH = pl.DeviceIdType.MESH
_ANY = pl.BlockSpec(memory_space=pl.ANY)
SEMS_PER_EXCHANGE = N_DEV - 1


class _Gather:
    def __init__(self, src, into=None, row0=0):
        self.src, self.into, self.row0 = src, into, row0
        self.out_shape = jax.ShapeDtypeStruct((N_DEV,) + src.shape if into is None else into.shape, src.dtype)

    def copies(self, x_ref, whole_ref, send_sems, recv_sems, local_sem):
        out_ref = whole_ref if self.into is None else whole_ref.at[:, pl.ds(self.row0, self.src.shape[0])]
        x, y, cc = lax.axis_index("x"), lax.axis_index("y"), lax.axis_index("c")
        me, sibling = (x, y, cc), (x, y, 1 - cc)
        chips = [(1 - x, y), (x, 1 - y), (1 - x, 1 - y)]

        def rows(px, py, pc):
            return out_ref.at[4 * px + 2 * py + pc]

        def copy(k, block, to, src=None):
            return pltpu.make_async_remote_copy(
                src_ref=rows(*block) if src is None else src, dst_ref=rows(*block),
                send_sem=send_sems.at[k], recv_sem=recv_sems.at[k], device_id=to, device_id_type=_MESH)

        return dict(
            mine=pltpu.make_async_copy(x_ref, rows(*me), local_sem),
            first=[copy(0, me, sibling, src=x_ref)] + [copy(1 + k, me, (*chip, cc), src=x_ref)
                                                       for k, chip in enumerate(chips)],
            passed=[copy(4 + k, (*chip, cc), sibling) for k, chip in enumerate(chips)],
            over_ici=[copy(1 + k, (*chip, cc), me) for k, chip in enumerate(chips)],
            from_sibling=[copy(0, sibling, me)] + [copy(4 + k, (*chip, 1 - cc), me) for k, chip in enumerate(chips)])

    def start(self, *refs):
        cps = self.copies(*refs)
        cps["mine"].start()
        for cp in cps["first"]:
            cp.start()

    def finish(self, *refs):
        cps = self.copies(*refs)
        for arrived, onward in zip(cps["over_ici"], cps["passed"]):
            arrived.wait_recv()
            onward.start()
        for arrived in cps["from_sibling"]:
            arrived.wait_recv()
        for cp in cps["first"] + cps["passed"]:
            cp.wait_send()
        cps["mine"].wait()


class _Scatter:
    def __init__(self, src, into=None, row0=0):
        self.src, self.into, self.row0 = src, into, row0
        whole = src if into is None else into
        self.out_shape = jax.ShapeDtypeStruct(whole.shape, whole.dtype)

    def copies(self, p_ref, whole_ref, send_sems, recv_sems, local_sem):
        x, y, cc = lax.axis_index("x"), lax.axis_index("y"), lax.axis_index("c")
        me = 4 * x + 2 * y + cc
        rows = self.src.shape[1]
        out_ref = whole_ref if self.into is None else whole_ref.at[:, pl.ds(self.row0, rows)]
        sends, arrivals = [], []
        for k in range(1, N_DEV):
            px = (1 - x) if k & 4 else x
            py = (1 - y) if k & 2 else y
            pc = (1 - cc) if k & 1 else cc
            peer = 4 * px + 2 * py + pc
            kw = dict(send_sem=send_sems.at[k - 1], recv_sem=recv_sems.at[k - 1], device_id=(px, py, pc),
                      device_id_type=_MESH)
            sends.append(pltpu.make_async_remote_copy(src_ref=p_ref.at[peer], dst_ref=out_ref.at[me], **kw))
            arrivals.append(pltpu.make_async_remote_copy(src_ref=p_ref.at[me], dst_ref=out_ref.at[peer], **kw))
        return dict(mine=pltpu.make_async_copy(p_ref.at[me], out_ref.at[me], local_sem), sends=sends,
                    arrivals=arrivals)

    def start(self, *refs):
        cps = self.copies(*refs)
        cps["mine"].start()
        for cp in cps["sends"]:
            cp.start()

    def finish(self, *refs):
        cps = self.copies(*refs)
        for cp in cps["arrivals"]:
            cp.wait_recv()
        for cp in cps["sends"]:
            cp.wait_send()
        cps["mine"].wait()


class _SemView:
    def __init__(self, ref, lo):
        self.ref, self.lo = ref, lo

    @property
    def at(self):
        return self

    def __getitem__(self, k):
        return self.ref.at[self.lo + k]


def _call(body, *, name, grid, in_specs, out_specs, out_shape, args, scratch_shapes=(), parallel=True, exchanges=()):
    n_axes = len(grid)
    if not exchanges:
        sem = ("parallel" if parallel else "arbitrary",) * n_axes
        return pl.pallas_call(
            body, name=name, grid=grid, in_specs=in_specs, out_specs=out_specs, out_shape=out_shape,
            scratch_shapes=scratch_shapes, compiler_params=pltpu.CompilerParams(dimension_semantics=sem))(*args)
    single = not isinstance(out_shape, (list, tuple))
    out_specs = [out_specs] if single else list(out_specs)
    out_shape = [out_shape] if single else list(out_shape)
    n_in, n_out, n_scr, n_x = len(in_specs), len(out_specs), len(scratch_shapes), len(exchanges)
    landing = [(e, ex.into) for e, ex in enumerate(exchanges) if ex.into is not None]
    aliases = {n_in + n_x + pos: n_out + e for pos, (e, _) in enumerate(landing)}

    def wrapped(*refs):
        ins, refs = refs[:n_in], refs[n_in:]
        x_in, refs = refs[:n_x], refs[n_x + len(landing):]
        outs, refs = refs[:n_out], refs[n_out:]
        x_out, refs = refs[:n_x], refs[n_x:]
        scr, (send_sems, recv_sems, local_sems) = refs[:n_scr], refs[n_scr:]
        ids = [pl.program_id(k) for k in range(n_axes)]
        first = functools.reduce(jnp.logical_and, [i == 0 for i in ids])
        last = functools.reduce(jnp.logical_and, [i == g - 1 for i, g in zip(ids, grid)])

        def sems(e):
            lo = e * SEMS_PER_EXCHANGE
            return _SemView(send_sems, lo), _SemView(recv_sems, lo), local_sems.at[e]

        @pl.when(first)
        def _():
            for e, ex in enumerate(exchanges):
                ex.start(x_in[e], x_out[e], *sems(e))

        body(*ins, *outs, *scr)

        @pl.when(last)
        def _():
            for e, ex in enumerate(exchanges):
                ex.finish(x_in[e], x_out[e], *sems(e))

    res = pl.pallas_call(
        wrapped, name=name, grid=grid, in_specs=list(in_specs) + [_ANY] * (n_x + len(landing)),
        out_specs=out_specs + [_ANY] * n_x, out_shape=out_shape + [ex.out_shape for ex in exchanges],
        input_output_aliases=aliases,
        scratch_shapes=list(scratch_shapes) + [pltpu.SemaphoreType.DMA((n_x * SEMS_PER_EXCHANGE,)),
                                               pltpu.SemaphoreType.DMA((n_x * SEMS_PER_EXCHANGE,)),
                                               pltpu.SemaphoreType.DMA((n_x,))],
        compiler_params=pltpu.CompilerParams(dimension_semantics=("arbitrary",) * n_axes),
    )(*args, *[ex.src for ex in exchanges], *[buf for _, buf in landing])
    outs, x_outs = res[:n_out], res[n_out:]
    return (outs[0] if single else outs), x_outs


def _exchange_only(exchanges, *, name):
    def body():
        pass

    return _call(body, name=name, grid=(1,), in_specs=[], out_specs=[], out_shape=[], args=[],
                 exchanges=exchanges)[1]


def _shift_dn(x, d):
    rolled = pltpu.roll(x, d, 0)
    if x.shape[0] <= SUBLANE or d >= SUBLANE:
        row = lax.broadcasted_iota(jnp.int32, x.shape, 0)
        return jnp.where(row >= d, rolled, 0.0)
    row = lax.broadcasted_iota(jnp.int32, (SUBLANE, x.shape[1]), 0)
    return jnp.concatenate([jnp.where(row >= d, rolled[:SUBLANE], 0.0), rolled[SUBLANE:]], axis=0)


def _shift_up(x, d):
    n = x.shape[0]
    rolled = pltpu.roll(x, n - d, 0)
    if n <= SUBLANE or d >= SUBLANE:
        row = lax.broadcasted_iota(jnp.int32, x.shape, 0)
        return jnp.where(row < n - d, rolled, 0.0)
    row = lax.broadcasted_iota(jnp.int32, (SUBLANE, x.shape[1]), 0)
    return jnp.concatenate([rolled[:n - SUBLANE], jnp.where(row < SUBLANE - d, rolled[n - SUBLANE:], 0.0)], axis=0)


def _gelu(x):
    return 0.5 * x * (1.0 + jnp.tanh(GELU_K * (x + GELU_C * x * x * x)))


def _gelu_grad(x):
    t = jnp.tanh(GELU_K * (x + GELU_C * x * x * x))
    return 0.5 * (1.0 + t) + 0.5 * x * (1.0 - t * t) * (GELU_K * (1.0 + 3.0 * GELU_C * x * x))


def _sigmoid(x):
    return 0.5 + 0.5 * jnp.tanh(0.5 * x)


def _conv3(x, w_ref):
    return w_ref[0:1, :] * _shift_dn(x, 2) + w_ref[1:2, :] * _shift_dn(x, 1) + w_ref[2:3, :] * x


def _dot(a, b, dims):
    return lax.dot_general(a.astype(BF16), b.astype(BF16), (dims, ((), ())), preferred_element_type=F32)


_NN = ((1,), (0,))
_NT = ((1,), (1,))
_TN = ((0,), (0,))


def _tiles(name, m, n, default):
    tm, tn = TILES.get(name, default)
    return math.gcd(tm, m), math.gcd(tn, n)


def _mm(a, b, mode, *, name, out_dtype=F32, add=None, norm=None, norm_bwd=None, tm_cap=512, tn_cap=1536,
        exchanges=()):
    halves = None
    if mode == "tn":
        r, m = a.shape
        n = b.shape[-1] * (2 if b.ndim == 3 else 1)
        tm, tn = _tiles(name, m, n, (_pick_tile(m, 256, LANE), _pick_tile(n, tn_cap, LANE)))
        if b.ndim == 3:
            per_half = b.shape[-1] // tn
            b_spec = pl.BlockSpec((None, r, tn), lambda i, j: (j // per_half, 0, j % per_half))
        else:
            b_spec = pl.BlockSpec((r, tn), lambda i, j: (0, j))
        in_specs = [pl.BlockSpec((r, tm), lambda i, j: (0, i)), b_spec]
        dims = _TN
    elif mode == "nt" and a.ndim == 3:
        _, m, halves = a.shape
        n = b.shape[0]
        tm, tn = _tiles(name, m, n, (_pick_tile(m, tm_cap, SUBLANE), _pick_tile(n, tn_cap, LANE)))
        in_specs = [pl.BlockSpec((2, tm, halves), lambda i, j: (0, i, 0)),
                    pl.BlockSpec((tn, 2 * halves), lambda i, j: (j, 0))]
        dims = _NT
    elif mode == "nn":
        m, k = a.shape
        n = b.shape[1]
        tm, tn = _tiles(name, m, n, (_pick_tile(m, tm_cap, SUBLANE), _pick_tile(n, tn_cap, LANE)))
        in_specs = [pl.BlockSpec((tm, k), lambda i, j: (i, 0)), pl.BlockSpec((k, tn), lambda i, j: (0, j))]
        dims = _NN
    else:
        m, k = a.shape
        n = b.shape[0]
        tm, tn = _tiles(name, m, n, (_pick_tile(m, tm_cap, SUBLANE), _pick_tile(n, tn_cap, LANE)))
        in_specs = [pl.BlockSpec((tm, k), lambda i, j: (i, 0)), pl.BlockSpec((tn, k), lambda i, j: (j, 0))]
        dims = _NT
    assert m % tm == 0 and n % tn == 0, (name, m, n, tm, tn)
    args = [a, b]
    tile = pl.BlockSpec((tm, tn), lambda i, j: (i, j))
    if add is not None:
        in_specs.append(tile)
        args.append(add)
    out_specs, out_shape = tile, jax.ShapeDtypeStruct((m, n), out_dtype)
    if norm is not None:
        gains, layer = norm
        assert tn == n
        in_specs.append(pl.BlockSpec((None, 1, n), lambda i, j: (layer, 0, 0)))
        args.append(gains.reshape(gains.shape[0], 1, n))
        out_specs, out_shape = [tile, tile], [out_shape, jax.ShapeDtypeStruct((m, n), BF16)]
    if norm_bwd is not None:
        x_in, gains, layer, res = norm_bwd
        assert tn == n and add is None and norm is None
        vec = pl.BlockSpec((None, 1, n), lambda i, j: (layer, 0, 0))
        in_specs += [tile, vec, tile]
        args += [x_in, gains.reshape(gains.shape[0], 1, n), res]
        out_specs = [tile, tile, pl.BlockSpec((1, n), lambda i, j: (0, 0))]
        out_shape = [jax.ShapeDtypeStruct((m, n), F32), jax.ShapeDtypeStruct((m, n), BF16),
                     jax.ShapeDtypeStruct((1, n), F32)]

    def body(*refs):
        if halves is None:
            acc = _dot(refs[0][...], refs[1][...], dims)
        else:
            acc = (_dot(refs[0][0], refs[1][:, :halves], dims) + _dot(refs[0][1], refs[1][:, halves:], dims))
        if add is not None:
            acc = acc + refs[2][...]
        if norm_bwd is not None:
            x_ref, g_ref, res_ref, dx_ref, dxb_ref, dg_ref = refs[2:]

            @pl.when(pl.program_id(0) == 0)
            def _():
                dg_ref[...] = jnp.zeros_like(dg_ref)
            dx, xn = _rms_bwd_rows(acc, x_ref[...], g_ref[...])
            dx = dx + res_ref[...]
            dx_ref[...] = dx
            dxb_ref[...] = dx.astype(BF16)
            dg_ref[...] += jnp.sum(acc * xn, axis=0, keepdims=True)
        elif norm is None:
            refs[-1][...] = acc.astype(out_dtype)
        else:
            refs[-2][...] = acc.astype(out_dtype)
            r = lax.rsqrt(jnp.mean(acc * acc, axis=-1, keepdims=True) + EPS)
            refs[-1][...] = (acc * r * refs[-3][...]).astype(BF16)

    return _call(body, name=name, grid=(m // tm, n // tn), in_specs=in_specs, out_specs=out_specs,
                 out_shape=out_shape, args=args, parallel=norm_bwd is None, exchanges=exchanges)


def _rms_fwd(x, g, *, name):
    l, d = x.shape
    tl = _pick_tile(l, 512, SUBLANE)

    def body(x_ref, g_ref, h_ref):
        xv = x_ref[...]
        r = lax.rsqrt(jnp.mean(xv * xv, axis=-1, keepdims=True) + EPS)
        h_ref[...] = (xv * r * g_ref[...]).astype(BF16)

    return pl.pallas_call(
        body, name=name, grid=(l // tl,),
        in_specs=[pl.BlockSpec((tl, d), lambda i: (i, 0)), pl.BlockSpec((1, d), lambda i: (0, 0))],
        out_specs=pl.BlockSpec((tl, d), lambda i: (i, 0)),
        out_shape=jax.ShapeDtypeStruct((l, d), BF16), compiler_params=_PAR)(x, g)


def _rms_bwd_rows(dh, xv, g):
    r = lax.rsqrt(jnp.mean(xv * xv, axis=-1, keepdims=True) + EPS)
    a = dh * g
    m = jnp.mean(a * xv, axis=-1, keepdims=True)
    return r * a - xv * (r * r * r) * m, xv * r


def _loss_head(x, g, tgt, *, name):
    l, d = x.shape
    tl = _pick_tile(l, 512, SUBLANE)

    def body(x_ref, g_ref, t_ref, loss_ref, dx_ref, dxb_ref, dg_ref):
        @pl.when(pl.program_id(0) == 0)
        def _():
            dg_ref[...] = jnp.zeros_like(dg_ref)
            loss_ref[...] = jnp.zeros_like(loss_ref)
        xv = x_ref[...]
        gv = g_ref[...]
        r = lax.rsqrt(jnp.mean(xv * xv, axis=-1, keepdims=True) + EPS)
        err = xv * r * gv - t_ref[...]
        row_loss = jnp.sum(err * err, axis=-1, keepdims=True) * (0.5 / d)
        loss_ref[...] += jnp.sum(row_loss, axis=0, keepdims=True)
        dy = err * (1.0 / d)
        dx, xn = _rms_bwd_rows(dy, xv, gv)
        dx_ref[...] = dx
        dxb_ref[...] = dx.astype(BF16)
        dg_ref[...] += jnp.sum(dy * xn, axis=0, keepdims=True)

    row = pl.BlockSpec((tl, d), lambda i: (i, 0))
    vec = pl.BlockSpec((1, d), lambda i: (0, 0))
    one = pl.BlockSpec((1, 1), lambda i: (0, 0))
    return pl.pallas_call(
        body, name=name, grid=(l // tl,), in_specs=[row, vec, row], out_specs=[one, row, row, vec],
        out_shape=[jax.ShapeDtypeStruct((1, 1), F32), jax.ShapeDtypeStruct((l, d), F32),
                   jax.ShapeDtypeStruct((l, d), BF16), jax.ShapeDtypeStruct((1, d), F32)],
        compiler_params=_ARB)(x, g, tgt)


def _ffn_act(up, cw, cb, *, name, exchanges=()):
    l = up.shape[0]
    nb = D_FF // LANE

    def body(ug_ref, uv_ref, wg_ref, wv_ref, bg_ref, bv_ref, o_ref, gv_ref):
        gc = _conv3(ug_ref[...], wg_ref) + bg_ref[...]
        vc = _conv3(uv_ref[...], wv_ref) + bv_ref[...]
        gv_ref[0] = gc
        gv_ref[1] = vc
        o_ref[...] = (gc * _sigmoid(gc) * vc).astype(BF16)

    col = lambda off: pl.BlockSpec((l, LANE), lambda j: (0, j + off))
    w3 = lambda off: pl.BlockSpec((CONV_WIDTH, LANE), lambda j: (0, j + off))
    b1 = lambda off: pl.BlockSpec((1, LANE), lambda j: (0, j + off))
    return _call(body, name=name, grid=(nb,), in_specs=[col(0), col(nb), w3(0), w3(nb), b1(0), b1(nb)],
                 out_specs=[col(0), pl.BlockSpec((2, l, LANE), lambda j: (0, 0, j))],
                 out_shape=[jax.ShapeDtypeStruct((l, D_FF), BF16), jax.ShapeDtypeStruct((2, l, D_FF), F32)],
                 args=[up, up, cw, cw, cb, cb], exchanges=exchanges)


def _ffn_act_bwd(up, gv, dact, cw, *, name, exchanges=()):
    l = up.shape[0]
    nb = D_FF // LANE

    def half_bwd(k, dc, x, w_ref, dup_ref, dcw_ref, dcb_ref):
        d1, d2 = _shift_up(dc, 1), _shift_up(dc, 2)
        dcb_ref[k] = jnp.sum(dc, axis=0, keepdims=True)
        dcw_ref[k] = jnp.concatenate([jnp.sum(d2 * x, axis=0, keepdims=True),
                                      jnp.sum(d1 * x, axis=0, keepdims=True),
                                      jnp.sum(dc * x, axis=0, keepdims=True)], axis=0)
        dup_ref[k] = (w_ref[2:3, :] * dc + w_ref[1:2, :] * d1 + w_ref[0:1, :] * d2).astype(BF16)

    def body(ug_ref, uv_ref, gv_ref, da_ref, wg_ref, wv_ref, dup_ref, dcw_ref, dcb_ref):
        gc, vc, da = gv_ref[0], gv_ref[1], da_ref[...]
        sg = _sigmoid(gc)
        half_bwd(0, da * vc * (sg * (1.0 + gc * (1.0 - sg))), ug_ref[...], wg_ref, dup_ref, dcw_ref, dcb_ref)
        half_bwd(1, da * (gc * sg), uv_ref[...], wv_ref, dup_ref, dcw_ref, dcb_ref)

    col = lambda off: pl.BlockSpec((l, LANE), lambda j: (0, j + off))
    w3 = lambda off: pl.BlockSpec((CONV_WIDTH, LANE), lambda j: (0, j + off))
    both = lambda rows: pl.BlockSpec((2, rows, LANE), lambda j: (0, 0, j))
    res = _call(
        body, name=name, grid=(nb,),
        in_specs=[col(0), col(nb), both(l), col(0), w3(0), w3(nb)],
        out_specs=[both(l), both(CONV_WIDTH), both(1)],
        out_shape=[jax.ShapeDtypeStruct((2, l, D_FF), BF16), jax.ShapeDtypeStruct((2, CONV_WIDTH, D_FF), F32),
                   jax.ShapeDtypeStruct((2, 1, D_FF), F32)],
        args=[up, up, gv, dact, cw, cw], exchanges=exchanges)
    (dup, dcw, dcb), moved = res if exchanges else (res, None)
    outs = [dup, jnp.concatenate([dcw[0], dcw[1]], axis=1), jnp.concatenate([dcb[0], dcb[1]], axis=1)]
    return (outs, moved) if exchanges else outs


def _sconv_fwd(proj, cw, *, name):
    l = proj.shape[0]
    nb = D_HALF // LANE

    def body(xa_ref, ba_ref, ca_ref, w_ref, o_ref):
        o_ref[...] = (ba_ref[...] * _conv3(ca_ref[...] * xa_ref[...], w_ref)).astype(BF16)

    col = lambda off: pl.BlockSpec((l, LANE), lambda j: (0, j + off))
    return pl.pallas_call(
        body, name=name, grid=(nb,),
        in_specs=[col(0), col(nb), col(2 * nb), pl.BlockSpec((CONV_WIDTH, LANE), lambda j: (0, j))],
        out_specs=col(0), out_shape=jax.ShapeDtypeStruct((l, 2 * D_HALF), BF16),
        compiler_params=_PAR)(proj, proj, proj, cw)


def _sconv_bwd(proj, dcat, cw, *, name):
    l = proj.shape[0]
    nb = D_HALF // LANE

    def body(xa_ref, ba_ref, ca_ref, dy_ref, w_ref, dxa_ref, dba_ref, dca_ref, dw_ref):
        xa, ba, ca, dy = xa_ref[...], ba_ref[...], ca_ref[...], dy_ref[...]
        q = ca * xa
        dba_ref[...] = (dy * _conv3(q, w_ref)).astype(BF16)
        dconv = dy * ba
        d1, d2 = _shift_up(dconv, 1), _shift_up(dconv, 2)
        dw_ref[...] = jnp.concatenate([jnp.sum(d2 * q, axis=0, keepdims=True), jnp.sum(d1 * q, axis=0, keepdims=True),
                                       jnp.sum(dconv * q, axis=0, keepdims=True)], axis=0)
        dq = w_ref[2:3, :] * dconv + w_ref[1:2, :] * d1 + w_ref[0:1, :] * d2
        dxa_ref[...] = (dq * ca).astype(BF16)
        dca_ref[...] = (dq * xa).astype(BF16)

    col = lambda off: pl.BlockSpec((l, LANE), lambda j: (0, j + off))
    w3 = pl.BlockSpec((CONV_WIDTH, LANE), lambda j: (0, j))
    piece = jax.ShapeDtypeStruct((l, D_HALF), BF16)
    return pl.pallas_call(
        body, name=name, grid=(nb,),
        in_specs=[col(0), col(nb), col(2 * nb), col(0), w3],
        out_specs=[col(0), col(0), col(0), w3],
        out_shape=[piece, piece, piece, jax.ShapeDtypeStruct((CONV_WIDTH, D_HALF), F32)],
        compiler_params=_PAR)(proj, proj, proj, dcat, cw)


def _s5_prep(log_step, a_re, a_im, b_re, b_im):
    step = jnp.exp(log_step)[:, None]
    mag = jnp.exp(a_re * step)
    lr = mag * jnp.cos(a_im * step)
    li = mag * jnp.sin(a_im * step)
    nr = lr - 1.0
    den = a_re * a_re + a_im * a_im
    qr = (nr * a_re + li * a_im) / den
    qi = (li * a_re - nr * a_im) / den
    br = qr[..., None] * b_re - qi[..., None] * b_im
    bi = qr[..., None] * b_im + qi[..., None] * b_re
    return lr, li, br, bi


def _block_diag(m):
    nb, ng, r, c = m.shape
    eye = jnp.eye(ng, dtype=m.dtype)
    return jnp.einsum("bgrc,gh->bgrhc", m, eye).reshape(nb, ng * r, ng * c)


def _block_diag_extract(w, r, c):
    nb = w.shape[0]
    ng = w.shape[1] // r
    w5 = w.reshape(nb, ng, r, ng, c)
    return jnp.einsum("bgrhc,gh->bgrc", w5, jnp.eye(ng, dtype=w.dtype))


def _s5_mats(br, bi, c_re, c_im):
    g8 = N_SSM_GROUPS // S5_LANE_BLOCKS
    to_blk = lambda m: m.reshape(S5_LANE_BLOCKS, g8, m.shape[1], m.shape[2])
    wb = jnp.concatenate([_block_diag(to_blk(jnp.swapaxes(br, 1, 2))),
                          _block_diag(to_blk(jnp.swapaxes(bi, 1, 2)))], axis=2)
    wc = jnp.concatenate([_block_diag(to_blk(jnp.swapaxes(c_re, 1, 2))),
                          _block_diag(to_blk(jnp.swapaxes(-c_im, 1, 2)))], axis=1)
    return wb, wc


def _s5_mats_bwd(dwb, dwc):
    g, p, h = N_SSM_GROUPS, SSM_STATE, SSM_GROUP
    half = S5_STATE_LANES
    dbr = jnp.swapaxes(_block_diag_extract(dwb[:, :, :half], h, p).reshape(g, h, p), 1, 2)
    dbi = jnp.swapaxes(_block_diag_extract(dwb[:, :, half:], h, p).reshape(g, h, p), 1, 2)
    dcr = jnp.swapaxes(_block_diag_extract(dwc[:, :half, :], p, h).reshape(g, p, h), 1, 2)
    dci = -jnp.swapaxes(_block_diag_extract(dwc[:, half:, :], p, h).reshape(g, p, h), 1, 2)
    return dbr, dbi, dcr, dci


def _s5_scan_consts(log_step, a_re, a_im, reverse):
    step = jnp.exp(log_step)[:, None]
    xr = (a_re * step).reshape(S5_LANE_BLOCKS, 1, S5_STATE_LANES)
    xi = (a_im * step).reshape(S5_LANE_BLOCKS, 1, S5_STATE_LANES)
    if reverse:
        xi = -xi
    row = jnp.arange(SUBLANE, dtype=F32).reshape(1, SUBLANE, 1)

    def power(n):
        mag = jnp.exp(n * xr)
        return jnp.concatenate([mag * jnp.cos(n * xi), mag * jnp.sin(n * xi)], axis=-1)

    kinds = []
    for d in (1, 2, 4):
        keep = (row <= SUBLANE - 1 - d) if reverse else (row >= d)
        kinds.append(jnp.where(keep, power(jnp.full_like(row, float(d))), 0.0))
    kinds.append(power((SUBLANE - row) if reverse else (row + 1.0)))
    return jnp.stack(kinds, axis=1)


def _scan_rows(s_ref, sc_ref, carry_ref, n_rows, reverse):
    n_grp = n_rows // SUBLANE
    n_col = S5_STATE_LANES // LANE
    half = S5_STATE_LANES

    def step(i, carry):
        grp = (n_grp - 1 - i) if reverse else i
        r0 = pl.multiple_of(grp * SUBLANE, SUBLANE)
        out = []
        for cb in range(n_col):
            lo, hi = cb * LANE, half + cb * LANE
            re = s_ref[pl.ds(r0, SUBLANE), lo:lo + LANE]
            im = s_ref[pl.ds(r0, SUBLANE), hi:hi + LANE]
            for k, d in enumerate((1, 2, 4)):
                sh = (SUBLANE - d) if reverse else d
                rr, ri = pltpu.roll(re, sh, 0), pltpu.roll(im, sh, 0)
                ar, ai = sc_ref[k, :, lo:lo + LANE], sc_ref[k, :, hi:hi + LANE]
                re, im = re + (ar * rr - ai * ri), im + (ar * ri + ai * rr)
            pr, pi = sc_ref[3, :, lo:lo + LANE], sc_ref[3, :, hi:hi + LANE]
            cr, ci = carry[2 * cb], carry[2 * cb + 1]
            re, im = re + (pr * cr - pi * ci), im + (pr * ci + pi * cr)
            s_ref[pl.ds(r0, SUBLANE), lo:lo + LANE] = re
            s_ref[pl.ds(r0, SUBLANE), hi:hi + LANE] = im
            edge = 0 if reverse else SUBLANE - 1
            out.append(jnp.broadcast_to(re[edge:edge + 1, :], (SUBLANE, LANE)))
            out.append(jnp.broadcast_to(im[edge:edge + 1, :], (SUBLANE, LANE)))
        return tuple(out)

    init = []
    for cb in range(n_col):
        init.append(carry_ref[:, cb * LANE:(cb + 1) * LANE])
        init.append(carry_ref[:, half + cb * LANE:half + (cb + 1) * LANE])
    fin = lax.fori_loop(0, n_grp, step, tuple(init), unroll=2)
    for cb in range(n_col):
        carry_ref[:, cb * LANE:(cb + 1) * LANE] = fin[2 * cb]
        carry_ref[:, half + cb * LANE:half + (cb + 1) * LANE] = fin[2 * cb + 1]


def _s5_fwd(proj, wb, wc, d_skip, sc, *, name, exchanges=()):
    l = proj.shape[0]
    tt = _pick_tile(l, S5_TIME_CHUNK, SUBLANE)
    u_off = (proj.shape[1] - D_HALF) // LANE
    w2 = 2 * S5_STATE_LANES

    def body(u_ref, wb_ref, wc_ref, d_ref, sc_ref, s_ref, y_ref, carry_ref):
        @pl.when(pl.program_id(1) == 0)
        def _():
            carry_ref[...] = jnp.zeros_like(carry_ref)
        u = u_ref[...]
        s_ref[...] = _dot(u, wb_ref[0], _NN)
        _scan_rows(s_ref, sc_ref.at[0], carry_ref, tt, False)
        y_ref[...] = _dot(s_ref[...], wc_ref[0], _NN) + d_ref[...] * u

    return _call(
        body, name=name, grid=(S5_LANE_BLOCKS, l // tt),
        in_specs=[pl.BlockSpec((tt, LANE), lambda b, t: (t, b + u_off)),
                  pl.BlockSpec((1, LANE, w2), lambda b, t: (b, 0, 0)),
                  pl.BlockSpec((1, w2, LANE), lambda b, t: (b, 0, 0)),
                  pl.BlockSpec((1, LANE), lambda b, t: (0, b)),
                  pl.BlockSpec((1, 4, SUBLANE, w2), lambda b, t: (b, 0, 0, 0))],
        out_specs=[pl.BlockSpec((tt, w2), lambda b, t: (t, b)), pl.BlockSpec((tt, LANE), lambda b, t: (t, b))],
        out_shape=[jax.ShapeDtypeStruct((l, S5_LANE_BLOCKS * w2), F32), jax.ShapeDtypeStruct((l, D_HALF), F32)],
        scratch_shapes=[pltpu.VMEM((SUBLANE, w2), F32)],
        args=[proj, wb, wc, d_skip, sc], parallel=False, exchanges=exchanges)


def _s5_bwd(proj, dy, states, wb, wc, d_skip, sc_rev, *, name, exchanges=()):
    l = proj.shape[0]
    tt = _pick_tile(l, S5_TIME_CHUNK, SUBLANE)
    nt = l // tt
    u_off = (proj.shape[1] - D_HALF) // LANE
    w2 = 2 * S5_STATE_LANES
    half = S5_STATE_LANES
    grp_per_chunk = tt // SUBLANE

    def body(u_ref, dy_ref, s_ref, halo_ref, wb_ref, wc_ref, d_ref, sc_ref,
             du_ref, dwb_ref, dwc_ref, dlam_ref, dd_ref, g_scr, carry_ref):
        t = pl.program_id(1)

        @pl.when(t == 0)
        def _():
            carry_ref[...] = jnp.zeros_like(carry_ref)
            dwb_ref[...] = jnp.zeros_like(dwb_ref)
            dwc_ref[...] = jnp.zeros_like(dwc_ref)
            dlam_ref[...] = jnp.zeros_like(dlam_ref)
            dd_ref[...] = jnp.zeros_like(dd_ref)

        u = u_ref[...]
        dyv = dy_ref[...]
        g_scr[...] = _dot(dyv, wc_ref[0], _NT)
        _scan_rows(g_scr, sc_ref.at[0], carry_ref, tt, True)
        gv = g_scr[...]
        du_ref[...] = (_dot(gv, wb_ref[0], _NT) + d_ref[...] * dyv).astype(BF16)
        dwb_ref[0] += _dot(u, gv, _TN)
        sv = s_ref[...]
        dwc_ref[0] += _dot(sv, dyv, _TN)
        dd_ref[...] += jnp.sum(dyv * u, axis=0, keepdims=True)
        first_chunk = t == nt - 1
        halo = jnp.where(first_chunk, 0.0, halo_ref[SUBLANE - 1:SUBLANE, :])
        row = lax.broadcasted_iota(jnp.int32, sv.shape, 0)
        sp = jnp.where(row == 0, jnp.broadcast_to(halo, sv.shape), pltpu.roll(sv, 1, 0))
        gr, gi = gv[:, :half], gv[:, half:]
        sr, si = sp[:, :half], sp[:, half:]
        dlr = jnp.sum(gr * sr + gi * si, axis=0, keepdims=True)
        dli = jnp.sum(gi * sr - gr * si, axis=0, keepdims=True)
        dlam_ref[0] += jnp.concatenate([dlr, dli], axis=1)

    rev = lambda t: nt - 1 - t
    return _call(
        body, name=name, grid=(S5_LANE_BLOCKS, nt),
        in_specs=[pl.BlockSpec((tt, LANE), lambda b, t: (rev(t), b + u_off)),
                  pl.BlockSpec((tt, LANE), lambda b, t: (rev(t), b)),
                  pl.BlockSpec((tt, w2), lambda b, t: (rev(t), b)),
                  pl.BlockSpec((SUBLANE, w2), lambda b, t: (jnp.maximum(rev(t) * grp_per_chunk - 1, 0), b)),
                  pl.BlockSpec((1, LANE, w2), lambda b, t: (b, 0, 0)),
                  pl.BlockSpec((1, w2, LANE), lambda b, t: (b, 0, 0)),
                  pl.BlockSpec((1, LANE), lambda b, t: (0, b)),
                  pl.BlockSpec((1, 4, SUBLANE, w2), lambda b, t: (b, 0, 0, 0))],
        out_specs=[pl.BlockSpec((tt, LANE), lambda b, t: (rev(t), b)),
                   pl.BlockSpec((1, LANE, w2), lambda b, t: (b, 0, 0)),
                   pl.BlockSpec((1, w2, LANE), lambda b, t: (b, 0, 0)),
                   pl.BlockSpec((1, 1, w2), lambda b, t: (b, 0, 0)),
                   pl.BlockSpec((1, LANE), lambda b, t: (0, b))],
        out_shape=[jax.ShapeDtypeStruct((l, D_HALF), BF16),
                   jax.ShapeDtypeStruct((S5_LANE_BLOCKS, LANE, w2), F32),
                   jax.ShapeDtypeStruct((S5_LANE_BLOCKS, w2, LANE), F32),
                   jax.ShapeDtypeStruct((S5_LANE_BLOCKS, 1, w2), F32),
                   jax.ShapeDtypeStruct((1, D_HALF), F32)],
        scratch_shapes=[pltpu.VMEM((tt, w2), F32), pltpu.VMEM((SUBLANE, w2), F32)],
        args=[proj, dy, states, states, wb, wc, d_skip, sc_rev], parallel=False, exchanges=exchanges)


def _glu_fwd(ypre, wg, bg, cat, *, name):
    l, d = ypre.shape
    tl = _pick_tile(l, 512, SUBLANE)

    def body(y_ref, w_ref, b_ref, cat_ref, o_ref):
        yg = _gelu(y_ref[...])
        o_ref[...] = (yg * _sigmoid(_dot(yg, w_ref[...], _NN) + b_ref[...])).astype(BF16)

    row = pl.BlockSpec((tl, d), lambda i: (i, 0))
    return pl.pallas_call(
        body, name=name, grid=(l // tl,),
        in_specs=[row, pl.BlockSpec((d, d), lambda i: (0, 0)), pl.BlockSpec((1, d), lambda i: (0, 0)), _ANY],
        out_specs=pl.BlockSpec((tl, d), lambda i: (i, 1)), out_shape=jax.ShapeDtypeStruct(cat.shape, cat.dtype),
        input_output_aliases={3: 0}, compiler_params=_PAR)(ypre, wg, bg, cat)


def _glu_bwd(dcat, ypre, wg, bg, *, name):
    l, d = ypre.shape
    tl = _pick_tile(l, 512, SUBLANE)

    def body(dy_ref, y_ref, w_ref, b_ref, dyp_ref, dw_ref, db_ref):
        @pl.when(pl.program_id(0) == 0)
        def _():
            dw_ref[...] = jnp.zeros_like(dw_ref)
            db_ref[...] = jnp.zeros_like(db_ref)
        yp = y_ref[...]
        dyb = dy_ref[...]
        yg = _gelu(yp)
        sg = _sigmoid(_dot(yg, w_ref[...], _NN) + b_ref[...])
        dz = dyb * yg * sg * (1.0 - sg)
        dyg = dyb * sg + _dot(dz, w_ref[...], _NT)
        dyp_ref[...] = dyg * _gelu_grad(yp)
        dw_ref[...] += _dot(yg, dz, _TN)
        db_ref[...] += jnp.sum(dz, axis=0, keepdims=True)

    row = pl.BlockSpec((tl, d), lambda i: (i, 0))
    mat = pl.BlockSpec((d, d), lambda i: (0, 0))
    vec = pl.BlockSpec((1, d), lambda i: (0, 0))
    return pl.pallas_call(
        body, name=name, grid=(l // tl,),
        in_specs=[pl.BlockSpec((tl, d), lambda i: (i, 1)), row, mat, vec], out_specs=[row, mat, vec],
        out_shape=[jax.ShapeDtypeStruct((l, d), F32), jax.ShapeDtypeStruct((d, d), F32),
                   jax.ShapeDtypeStruct((1, d), F32)],
        compiler_params=_ARB)(dcat, ypre, wg, bg)


def _window_sum(x, w, trailing):
    s, d = x, 1
    while d < w:
        s = s + (_shift_dn(s, d) if trailing else _shift_up(s, d))
        d *= 2
    return s


def _window_count(shape, w):
    row = lax.broadcasted_iota(jnp.int32, shape, 0)
    return jnp.minimum(row + 1, w).astype(F32)


def _pool_fwd(proj, pw, scale, *, name):
    l = proj.shape[0]
    ng = len(POOL_WINDOWS)

    def body(z_ref, w_ref, sc_ref, o_ref):
        z = z_ref[...]
        for k, w in enumerate(POOL_WINDOWS):
            @pl.when(pl.program_id(0) == k)
            def _():
                pooled = _window_sum(z, w, True) / _window_count(z.shape, w) - z
                o_ref[...] = (_dot(pooled, w_ref[0], _NN) * sc_ref[...]).astype(BF16)

    col = pl.BlockSpec((l, LANE), lambda g: (0, g))
    return pl.pallas_call(
        body, name=name, grid=(ng,),
        in_specs=[col, pl.BlockSpec((1, LANE, LANE), lambda g: (g, 0, 0)), pl.BlockSpec((1, LANE), lambda g: (0, g))],
        out_specs=col, out_shape=jax.ShapeDtypeStruct((l, 2 * D_HALF), BF16), compiler_params=_PAR)(proj, pw, scale)


def _pool_bwd(proj, dcat, pw, scale, *, name):
    l = proj.shape[0]
    ng = len(POOL_WINDOWS)

    def body(z_ref, dy_ref, w_ref, sc_ref, dz_ref, dw_ref, dsc_ref):
        z = z_ref[...]
        dy = dy_ref[...]
        for k, w in enumerate(POOL_WINDOWS):
            @pl.when(pl.program_id(0) == k)
            def _():
                cnt = _window_count(z.shape, w)
                pooled = _window_sum(z, w, True) / cnt - z
                ypre = _dot(pooled, w_ref[0], _NN)
                dsc_ref[...] = jnp.sum(dy * ypre, axis=0, keepdims=True)
                dyp = dy * sc_ref[...]
                dw_ref[0] = _dot(pooled, dyp, _TN)
                dpool = _dot(dyp, w_ref[0], _NT)
                dz_ref[...] = (_window_sum(dpool / cnt, w, False) - dpool).astype(BF16)

    col = pl.BlockSpec((l, LANE), lambda g: (0, g))
    mat = pl.BlockSpec((1, LANE, LANE), lambda g: (g, 0, 0))
    vec = pl.BlockSpec((1, LANE), lambda g: (0, g))
    return pl.pallas_call(
        body, name=name, grid=(ng,), in_specs=[col, col, mat, vec], out_specs=[col, mat, vec],
        out_shape=[jax.ShapeDtypeStruct((l, D_HALF), BF16), jax.ShapeDtypeStruct((ng, LANE, LANE), F32),
                   jax.ShapeDtypeStruct((1, D_HALF), F32)],
        compiler_params=_PAR)(proj, dcat, pw, scale)


def _tril_mask():
    r = lax.broadcasted_iota(jnp.int32, (CHUNK, CHUNK), 0)
    c = lax.broadcasted_iota(jnp.int32, (CHUNK, CHUNK), 1)
    return r >= c


def _sgu_fwd(proj, ng, sw, sb_t, cat, *, name):
    l = proj.shape[0]
    tl = _pick_tile(l, 512, CHUNK)

    def body(su_ref, sv_ref, g_ref, w_ref, b_ref, cat_ref, o_ref):
        su = _gelu(su_ref[...])
        sv = _gelu(sv_ref[...])
        r = lax.rsqrt(jnp.mean(sv * sv, axis=-1, keepdims=True) + EPS)
        v = sv * r * g_ref[...]
        mask = _tril_mask()
        for h in range(SGU_HEADS):
            wm = jnp.where(mask, w_ref[h], 0.0)
            cs = slice(h * LANE, (h + 1) * LANE)
            for n in range(tl // CHUNK):
                rs = slice(n * CHUNK, (n + 1) * CHUNK)
                mixed = _dot(wm, v[rs, cs], _NN) + b_ref[:, h:h + 1]
                o_ref[rs, cs] = (su[rs, cs] * mixed).astype(BF16)

    blk = lambda c: pl.BlockSpec((tl, D_HALF), lambda i: (i, c))
    return pl.pallas_call(
        body, name=name, grid=(l // tl,),
        in_specs=[blk(1), blk(2), pl.BlockSpec((1, D_HALF), lambda i: (0, 0)),
                  pl.BlockSpec((SGU_HEADS, CHUNK, CHUNK), lambda i: (0, 0, 0)),
                  pl.BlockSpec((CHUNK, SGU_HEADS), lambda i: (0, 0)), _ANY],
        out_specs=blk(1), out_shape=jax.ShapeDtypeStruct(cat.shape, cat.dtype), input_output_aliases={5: 0},
        compiler_params=_PAR)(proj, proj, ng, sw, sb_t, cat)


def _sgu_bwd(proj, dcat, ng, sw, sb_t, *, name):
    l = proj.shape[0]
    tl = _pick_tile(l, 512, CHUNK)

    def body(su_ref, sv_ref, dy_ref, g_ref, w_ref, b_ref, dsu_ref, dsv_ref, dw_ref, dbm_ref, dng_ref, dv_scr):
        @pl.when(pl.program_id(0) == 0)
        def _():
            dw_ref[...] = jnp.zeros_like(dw_ref)
            dbm_ref[...] = jnp.zeros_like(dbm_ref)
            dng_ref[...] = jnp.zeros_like(dng_ref)
        su_pre = su_ref[...]
        sv_pre = sv_ref[...]
        su = _gelu(su_pre)
        sv = _gelu(sv_pre)
        gsu = _gelu_grad(su_pre)
        gv = g_ref[...]
        r = lax.rsqrt(jnp.mean(sv * sv, axis=-1, keepdims=True) + EPS)
        v = sv * r * gv
        dy = dy_ref[...]
        mask = _tril_mask()
        for h in range(SGU_HEADS):
            wm = jnp.where(mask, w_ref[h], 0.0)
            cs = slice(h * LANE, (h + 1) * LANE)
            dw_acc = jnp.zeros((CHUNK, CHUNK), F32)
            db_acc = jnp.zeros((CHUNK, LANE), F32)
            for n in range(tl // CHUNK):
                rs = slice(n * CHUNK, (n + 1) * CHUNK)
                vb = v[rs, cs]
                mixed = _dot(wm, vb, _NN) + b_ref[:, h:h + 1]
                dyb = dy[rs, cs]
                dsu_ref[rs, cs] = (dyb * mixed * gsu[rs, cs]).astype(BF16)
                dmix = dyb * su[rs, cs]
                db_acc = db_acc + dmix
                dw_acc = dw_acc + _dot(dmix, vb, _NT)
                dv_scr[rs, cs] = _dot(wm, dmix, _TN)
            dw_ref[h] += jnp.where(mask, dw_acc, 0.0)
            dbm_ref[h] += db_acc
        dv = dv_scr[...]
        a = dv * gv
        m = jnp.mean(a * sv, axis=-1, keepdims=True)
        dsv = r * a - sv * (r * r * r) * m
        dng_ref[...] += jnp.sum(dv * sv * r, axis=0, keepdims=True)
        dsv_ref[...] = (dsv * _gelu_grad(sv_pre)).astype(BF16)

    blk = lambda c: pl.BlockSpec((tl, D_HALF), lambda i: (i, c))
    mats = pl.BlockSpec((SGU_HEADS, CHUNK, CHUNK), lambda i: (0, 0, 0))
    vec = pl.BlockSpec((1, D_HALF), lambda i: (0, 0))
    piece = jax.ShapeDtypeStruct((l, D_HALF), BF16)
    mshape = jax.ShapeDtypeStruct((SGU_HEADS, CHUNK, CHUNK), F32)
    return pl.pallas_call(
        body, name=name, grid=(l // tl,),
        in_specs=[blk(1), blk(2), blk(1), vec, mats, pl.BlockSpec((CHUNK, SGU_HEADS), lambda i: (0, 0))],
        out_specs=[blk(0), blk(0), mats, mats, vec],
        out_shape=[piece, piece, mshape, mshape, jax.ShapeDtypeStruct((1, D_HALF), F32)],
        scratch_shapes=[pltpu.VMEM((tl, D_HALF), F32)],
        compiler_params=_ARB)(proj, proj, dcat, ng, sw, sb_t)


def _s5_params(w):
    prep_args = (w["ssm_log_step"], w["ssm_a_re"], w["ssm_a_im"], w["ssm_b_re"], w["ssm_b_im"])
    (lr, li, br, bi), prep_vjp = jax.vjp(jax.vmap(_s5_prep), *prep_args)
    wb, wc = jax.vmap(_s5_mats)(br, bi, w["ssm_c_re"], w["ssm_c_im"])
    consts = lambda reverse: jax.vmap(functools.partial(_s5_scan_consts, reverse=reverse))(*prep_args[:3])
    return dict(wb=wb.astype(BF16), wc=wc.astype(BF16), d=w["ssm_d"][:, None, :], sc=consts(False),
                sc_rev=consts(True), prep_vjp=prep_vjp)


def _s5_param_grads(s5, dwb, dwc, dlam, dd):
    dbr, dbi, dcr, dci = jax.vmap(_s5_mats_bwd)(dwb, dwc)
    n = dlam.shape[0]
    dlr = dlam[:, :, 0, :S5_STATE_LANES].reshape(n, N_SSM_GROUPS, SSM_STATE)
    dli = dlam[:, :, 0, S5_STATE_LANES:].reshape(n, N_SSM_GROUPS, SSM_STATE)
    dls, dar, dai, db_re, db_im = s5["prep_vjp"]((dlr, dli, dbr, dbi))
    return dict(ssm_log_step=dls, ssm_a_re=dar, ssm_a_im=dai, ssm_b_re=db_re, ssm_b_im=db_im, ssm_c_re=dcr,
                ssm_c_im=dci, ssm_d=dd[:, 0, :])


TILES = {
    "mm_up": (4096, 512), "mm_up_dw": (1024, 1408), "mm_down_dx": (1024, 2816),
    "mm_down": (512, 1024), "mm_down_dw": (256, 1024), "mm_even_in": (2048, 1024), "mm_odd_in": (2048, 768),
    "mm_mix_out": (1024, 1024), "mm_mix_out_dx": (2048, 1024), "mm_mix_out_dw": (1024, 512),
    "mm_even_in_dw": (1024, 512), "mm_odd_in_dw": (1024, 512),
}


def _layer_weights(i):
    j = i // 2
    mixer = [("even_w_in", j), ("even_w_out", j), ("ssm_glu_w", j)] if i % 2 == 0 else [("odd_w_in", j),
                                                                                         ("odd_w_out", j)]
    return dict(w_in=mixer[0], w_out=mixer[1], glu=mixer[2:], up=("ffn_w_up", i), down=("ffn_w_down", i))


class _LocalWeights:
    def __init__(self, w):
        self.w, self.grads = w, {}

    def carried_by(self, stage, i):
        return []

    def delivered(self, stage, i, outs):
        pass

    def weight(self, key):
        return self.w[key[0]][key[1]]

    def grad(self, key, dw):
        self.grads[key] = dw


class _ShardedWeights:
    def __init__(self, shards):
        self.shards = shards
        self.full, self.halves, self.pending, self.scattered = {}, {}, {}, {}

    def start(self, others):
        keys = [_layer_weights(0)["w_in"]]
        outs = _exchange_only(others + [self._gather(k) for k in keys], name="ag_first")
        self._take(keys, outs[len(others):])
        return outs[:len(others)]

    def _gather(self, key):
        shard = self.shards[key[0]][key[1]]
        if len(key) == 2:
            return _Gather(shard)
        rows = shard.shape[0] // 2
        buf = self.halves.get(key[:2])
        if buf is None:
            buf = lax.empty((N_DEV,) + shard.shape, shard.dtype)
        return _Gather(shard[key[2] * rows:(key[2] + 1) * rows], into=buf, row0=key[2] * rows)

    def _take(self, keys, outs):
        for key, got in zip(keys, outs):
            if len(key) == 3 and key[:2] not in self.halves:
                self.halves[key[:2]] = got
                continue
            if BIG[key[0]] == 2:
                self.full[key[:2]] = jnp.swapaxes(got, 0, 1).reshape(got.shape[1], -1)
            else:
                self.full[key[:2]] = got.reshape(-1, got.shape[2])

    def _plan(self, stage, i):
        cur = _layer_weights(i)
        nxt = _layer_weights(i + 1) if i + 1 < DEPTH else None
        has_scan = lambda k: k % 2 == 0
        none = ([], [])
        up_half = lambda h: ([(*nxt["up"], h)], []) if nxt and not has_scan(i + 1) else none
        return {
            "mm_in": ([cur["w_out"], *cur["glu"]], []) if i == 0 else none,
            "s5_fwd": ([cur["up"]], []),
            "mm_up": ([cur["down"]] + ([nxt["w_in"], nxt["w_out"], *nxt["glu"]] if nxt else []), []),
            "ffn_act": up_half(0),
            "mm_down": up_half(1),
            "ffn_act_bwd": ([], [cur["down"]] + ([] if not nxt else [nxt["w_in"]] if not has_scan(i + 1)
                                                 else [nxt["w_out"], *nxt["glu"]])),
            "mm_up_dx": none if not nxt else ([], [(*nxt["up"], 0)]) if not has_scan(i + 1) else ([], [nxt["w_in"]]),
            "mm_up_dw": ([], [(*nxt["up"], 1)]) if nxt and not has_scan(i + 1) else none,
            "s5_bwd": ([], [cur["up"]] + ([nxt["w_out"]] if nxt else [])),
            "mm_in_dw": ([], [cur["w_out"], *cur["glu"]]) if i == 0 else none,
            "mm_in_dx": ([], [cur["w_in"]]) if i == 0 else none,
        }[stage]

    def _scatter(self, key):
        name, layer = key[:2]
        layers, rows, cols = self.shards[name].shape
        if len(key) == 3:
            src = self.pending[key[:2]] if key[2] == 0 else self.pending.pop(key[:2])
            src = src[:, key[2] * (rows // 2):(key[2] + 1) * (rows // 2)]
            row0 = layer * rows + key[2] * (rows // 2)
        else:
            src, row0 = self.pending.pop(key), layer * rows
        if name not in self.scattered:
            self.scattered[name] = lax.empty((N_DEV, layers * rows, cols), src.dtype)
        return _Scatter(src, into=self.scattered[name], row0=row0)

    def carried_by(self, stage, i):
        gather, scatter = self._plan(stage, i)
        return [self._gather(k) for k in gather] + [self._scatter(k) for k in scatter]

    def delivered(self, stage, i, outs):
        gather, scatter = self._plan(stage, i)
        self._take(gather, outs[:len(gather)])
        for key, buf in zip(scatter, outs[len(gather):]):
            self.scattered[key[0]] = buf

    def weight(self, key):
        return self.full[key]

    def grad(self, key, dw):
        self.pending[key] = _to_dest_major(dw, BIG[key[0]] - 1).astype(BF16)

    def finish(self, carrier, others):
        keys = list(self.pending)
        res, outs = carrier(others + [self._scatter(k) for k in keys])
        for (name, _), buf in zip(keys, outs[len(others):]):
            self.scattered[name] = buf
        return res, outs[:len(others)]


def _device_step(x, tgt, w, comm):
    saved = []
    s5 = _s5_params(w)
    h = _rms_fwd(x, w["norm_mix_g"][0:1], name="rms_fwd")
    for i in range(DEPTH):
        j = i // 2
        lw = _layer_weights(i)
        if i % 2 == 0:
            proj = _carry(comm, "mm_in", i, _mm, h, comm.weight(lw["w_in"]), "nn", name="mm_even_in")
            ya = _sconv_fwd(proj, w["even_conv_w"][j], name="sconv_fwd")
            states, ypre = _carry(comm, "s5_fwd", i, _s5_fwd, proj, s5["wb"][j], s5["wc"][j], s5["d"][j],
                                  s5["sc"][j], name="s5_fwd")
            cat = _glu_fwd(ypre, comm.weight(lw["glu"][0]), w["ssm_glu_b"][j][None, :], ya, name="glu_fwd")
            mix = (states, ypre)
        else:
            proj = _carry(comm, "mm_in", i, _mm, h, comm.weight(lw["w_in"]), "nn", name="mm_odd_in")
            yc = _pool_fwd(proj, w["pool_w"][j], w["pool_scale"][j][None, :], name="pool_fwd")
            sb_t = jnp.transpose(w["sgu_b"][j])
            cat = _sgu_fwd(proj, w["sgu_norm_g"][j][None, :], w["sgu_w"][j], sb_t, yc, name="sgu_fwd")
            mix = (sb_t,)
        x1, h2 = _mm(cat, comm.weight(lw["w_out"]), "nn", add=x, norm=(w["norm_ffn_g"], i), name="mm_mix_out")
        up = _carry(comm, "mm_up", i, _mm, h2, comm.weight(lw["up"]), "nn", name="mm_up")
        act, gv = _carry(comm, "ffn_act", i, _ffn_act, up, w["ffn_conv_w"][i], w["ffn_conv_b"][i:i + 1],
                         name="ffn_act")
        if i + 1 < DEPTH:
            x2, h_next = _carry(comm, "mm_down", i, _mm, act, comm.weight(lw["down"]), "nn", add=x1,
                                norm=(w["norm_mix_g"], i + 1), name="mm_down")
        else:
            x2, h_next = _mm(act, comm.weight(lw["down"]), "nn", add=x1, name="mm_down_last"), None
        saved.append((x, h, proj, cat, x1, h2, up, gv, act, mix))
        x, h = x2, h_next

    loss, dx, dxb, dgf = _loss_head(x, w["norm_final_g"][None, :], tgt, name="loss_head")
    per_layer = {}
    s5_grads = []

    def put(name, idx, val):
        per_layer.setdefault(name, {})[idx] = val

    for i in reversed(range(DEPTH)):
        j = i // 2
        lw = _layer_weights(i)
        x0, h, proj, cat, x1, h2, up, gv, act, mix = saved[i]
        dact = _mm(dxb, comm.weight(lw["down"]), "nt", out_dtype=BF16, name="mm_down_dx")
        comm.grad(lw["down"], _mm(act, dxb, "tn", out_dtype=BF16, name="mm_down_dw"))
        dup, dcw, dcb = _carry(comm, "ffn_act_bwd", i, _ffn_act_bwd, up, gv, dact, w["ffn_conv_w"][i],
                               name="ffn_act_bwd")
        put("ffn_conv_w", i, dcw)
        put("ffn_conv_b", i, dcb[0])
        dx1, dx1b, dg2 = _carry(comm, "mm_up_dx", i, _mm, dup, comm.weight(lw["up"]), "nt", tm_cap=512, tn_cap=D_MODEL,
                                norm_bwd=(x1, w["norm_ffn_g"], i, dx), name="mm_up_dx")
        comm.grad(lw["up"], _carry(comm, "mm_up_dw", i, _mm, h2, dup, "tn", out_dtype=BF16,
                                   name="mm_up_dw"))
        put("norm_ffn_g", i, dg2[0])
        dcat = _mm(dx1b, comm.weight(lw["w_out"]), "nt", name="mm_mix_out_dx")
        comm.grad(lw["w_out"], _mm(cat, dx1b, "tn", out_dtype=BF16, name="mm_mix_out_dw"))
        if i % 2 == 0:
            states, ypre = mix
            dxa, dba, dca, dcw_a = _sconv_bwd(proj, dcat, w["even_conv_w"][j], name="sconv_bwd")
            put("even_conv_w", j, dcw_a)
            dypre, dwg, dbg = _glu_bwd(dcat, ypre, comm.weight(lw["glu"][0]), w["ssm_glu_b"][j][None, :],
                                       name="glu_bwd")
            comm.grad(lw["glu"][0], dwg)
            put("ssm_glu_b", j, dbg[0])
            du, dwb, dwc, dlam, dd = _carry(comm, "s5_bwd", i, _s5_bwd, proj, dypre, states, s5["wb"][j], s5["wc"][j],
                                            s5["d"][j], s5["sc_rev"][j], name="s5_bwd")
            s5_grads.insert(0, (dwb, dwc, dlam, dd))
            dproj = jnp.concatenate([dxa, dba, dca, du], axis=1)
            in_name = "mm_even_in"
        else:
            (sb_t,) = mix
            dz, dpw, dps = _pool_bwd(proj, dcat, w["pool_w"][j], w["pool_scale"][j][None, :], name="pool_bwd")
            put("pool_w", j, dpw)
            put("pool_scale", j, dps[0])
            dsu, dsv, dsw, dbm, dng = _sgu_bwd(proj, dcat, w["sgu_norm_g"][j][None, :], w["sgu_w"][j], sb_t,
                                               name="sgu_bwd")
            put("sgu_w", j, dsw)
            put("sgu_b", j, jnp.sum(dbm, axis=-1))
            put("sgu_norm_g", j, dng[0])
            dproj = jnp.concatenate([dz, dsu, dsv], axis=1)
            in_name = "mm_odd_in"
        comm.grad(lw["w_in"], _carry(comm, "mm_in_dw", i, _mm, h, dproj, "tn", out_dtype=BF16, name=in_name + "_dw"))
        dx, dxb, dg1 = _carry(comm, "mm_in_dx", i, _mm, dproj, comm.weight(lw["w_in"]), "nt", tn_cap=D_MODEL,
                              norm_bwd=(x0, w["norm_mix_g"], i, dx1), name=in_name + "_dx")
        put("norm_mix_g", i, dg1[0])

    grads = {nm: [vals[k] for k in sorted(vals)] for nm, vals in per_layer.items()}
    grads.update({nm: jnp.stack(grads[nm]) for nm in SMALL})
    grads["norm_final_g"] = dgf[0]
    grads.update(_s5_param_grads(s5, *[jnp.stack(parts) for parts in zip(*s5_grads)]))
    return loss, dx, grads


def _carry(comm, stage, i, fn, *args, **kwargs):
    exchanges = comm.carried_by(stage, i)
    if not exchanges:
        return fn(*args, **kwargs)
    out, moved = fn(*args, exchanges=exchanges, **kwargs)
    comm.delivered(stage, i, moved)
    return out


def _sum_parts(parts, *, name):
    g, r, c = parts.shape
    tr = _pick_tile(r, max(16, EXCHANGE_BLOCK_ELEMS // c), 16)

    def body(p_ref, o_ref):
        acc = p_ref[0].astype(F32)
        for k in range(1, g):
            acc = acc + p_ref[k].astype(F32)
        o_ref[...] = acc

    return pl.pallas_call(
        body, name=name, grid=(r // tr,), in_specs=[pl.BlockSpec((g, tr, c), lambda i: (0, i, 0))],
        out_specs=pl.BlockSpec((tr, c), lambda i: (i, 0)), out_shape=jax.ShapeDtypeStruct((r, c), F32),
        compiler_params=_PAR)(parts)


def _adamw(w, m, v, g_parts, *, name, exchanges=()):
    r, c = w.shape
    g = g_parts.shape[0]
    tc = _pick_tile(c, 8192, LANE)
    tr = _pick_tile(r, max(16, (1 << 18) // tc), 16)
    c1 = 1.0 - ADAM_B1 ** ADAM_STEP
    c2 = 1.0 - ADAM_B2 ** ADAM_STEP

    def body(w_ref, m_ref, v_ref, g_ref, go_ref, d_ref, mo_ref, vo_ref):
        grad = g_ref[0].astype(F32)
        for k in range(1, g):
            grad = grad + g_ref[k].astype(F32)
        m_new = ADAM_B1 * m_ref[...] + (1.0 - ADAM_B1) * grad
        v_new = ADAM_B2 * v_ref[...] + (1.0 - ADAM_B2) * (grad * grad)
        go_ref[...] = grad
        mo_ref[...] = m_new
        vo_ref[...] = v_new
        d_ref[...] = -ADAM_LR * ((m_new / c1) / (jnp.sqrt(v_new / c2) + ADAM_EPS) + ADAM_WD * w_ref[...])

    blk = pl.BlockSpec((tr, tc), lambda i, j: (i, j))
    out = jax.ShapeDtypeStruct((r, c), F32)
    return _call(body, name=name, grid=(r // tr, c // tc),
                 in_specs=[blk, blk, blk, pl.BlockSpec((g, tr, tc), lambda i, j: (0, i, j))],
                 out_specs=[blk, blk, blk, blk], out_shape=[out, out, out, out], args=[w, m, v, g_parts],
                 exchanges=exchanges)


WEIGHT_NAMES = ['norm_mix_g', 'even_w_in', 'even_conv_w', 'ssm_log_step', 'ssm_a_re', 'ssm_a_im', 'ssm_b_re',
                'ssm_b_im', 'ssm_c_re', 'ssm_c_im', 'ssm_d', 'ssm_glu_w', 'ssm_glu_b', 'even_w_out', 'odd_w_in',
                'pool_w', 'pool_scale', 'sgu_norm_g', 'sgu_w', 'sgu_b', 'odd_w_out', 'norm_ffn_g', 'ffn_w_up',
                'ffn_conv_w', 'ffn_conv_b', 'ffn_w_down', 'norm_final_g']
BIG = {'even_w_in': 2, 'ssm_glu_w': 1, 'even_w_out': 1, 'odd_w_in': 2, 'odd_w_out': 1, 'ffn_w_up': 2,
       'ffn_w_down': 1}
SMALL = {'even_conv_w': 2, 'pool_scale': 1, 'sgu_norm_g': 1, 'ffn_conv_w': 2}
SMALL_ROWS = 16


def _pad_to(n, q):
    return -(-n // q) * q


def _pack(arrays, dtype, rows, lead=()):
    flat = [a.reshape(lead + (-1,)).astype(dtype) for a in arrays]
    n = sum(f.shape[-1] for f in flat)
    pad = _pad_to(n, rows * LANE) - n
    if pad:
        flat.append(jnp.zeros(lead + (pad,), dtype))
    return jnp.concatenate(flat, axis=-1).reshape(lead + (rows, -1))


def _unpack(buf, shapes, lead=()):
    flat = buf.reshape(lead + (-1,))
    out, off = [], 0
    for shp in shapes:
        n = math.prod(shp)
        out.append(flat[..., off:off + n].reshape(lead + tuple(shp)))
        off += n
    return out


def _to_dest_major(full, axis):
    shp = full.shape
    split = full.reshape(shp[:axis] + (N_DEV, shp[axis] // N_DEV) + shp[axis + 1:])
    return jnp.moveaxis(split, axis, 0)


def _from_dest_major(blocks, axis):
    moved = jnp.moveaxis(blocks, 0, axis)
    shp = moved.shape
    return moved.reshape(shp[:axis] + (shp[axis] * shp[axis + 1],) + shp[axis + 2:])


def _rows_2d(a):
    return a.reshape(-1, a.shape[-1])


def kernel(x, norm_mix_g, even_w_in, even_conv_w, ssm_log_step, ssm_a_re, ssm_a_im, ssm_b_re, ssm_b_im, ssm_c_re, ssm_c_im, ssm_d, ssm_glu_w, ssm_glu_b, even_w_out, odd_w_in, pool_w, pool_scale, sgu_norm_g, sgu_w, sgu_b, odd_w_out, norm_ffn_g, ffn_w_up, ffn_conv_w, ffn_conv_b, ffn_w_down, norm_final_g, loss_target, m_norm_mix_g, m_even_w_in, m_even_conv_w, m_ssm_log_step, m_ssm_a_re, m_ssm_a_im, m_ssm_b_re, m_ssm_b_im, m_ssm_c_re, m_ssm_c_im, m_ssm_d, m_ssm_glu_w, m_ssm_glu_b, m_even_w_out, m_odd_w_in, m_pool_w, m_pool_scale, m_sgu_norm_g, m_sgu_w, m_sgu_b, m_odd_w_out, m_norm_ffn_g, m_ffn_w_up, m_ffn_conv_w, m_ffn_conv_b, m_ffn_w_down, m_norm_final_g, v_norm_mix_g, v_even_w_in, v_even_conv_w, v_ssm_log_step, v_ssm_a_re, v_ssm_a_im, v_ssm_b_re, v_ssm_b_im, v_ssm_c_re, v_ssm_c_im, v_ssm_d, v_ssm_glu_w, v_ssm_glu_b, v_even_w_out, v_odd_w_in, v_pool_w, v_pool_scale, v_sgu_norm_g, v_sgu_w, v_sgu_b, v_odd_w_out, v_norm_ffn_g, v_ffn_w_up, v_ffn_conv_w, v_ffn_conv_b, v_ffn_w_down, v_norm_final_g):
    given = dict(locals())
    wts = {n: given[n] for n in WEIGHT_NAMES}
    mom = {n: given["m_" + n] for n in WEIGHT_NAMES}
    var = {n: given["v_" + n] for n in WEIGHT_NAMES}
    repl = [n for n in WEIGHT_NAMES if n not in BIG and n not in SMALL]

    small_shapes = [wts[n].shape for n in SMALL]
    comm = _ShardedWeights({n: wts[n].astype(BF16) for n in BIG})
    (small_all,) = comm.start([_Gather(_pack([wts[n] for n in SMALL], F32, SMALL_ROWS))])
    full = {n: wts[n] for n in repl}
    for n, blocks in zip(SMALL, _unpack(small_all, small_shapes, lead=(N_DEV,))):
        full[n] = _from_dest_major(blocks, SMALL[n])

    loss, dx, grads = _device_step(x[0], loss_target[0], full, comm)

    repl_shapes = [wts[n].shape for n in repl]
    pieces = [p.reshape(-1) for n in repl for p in (grads[n] if isinstance(grads[n], list) else [grads[n]])]
    repl_flat = jnp.concatenate(pieces + [loss.reshape(-1)])
    n_repl = repl_flat.shape[0]
    chunk = _pad_to(-(-n_repl // N_DEV), SMALL_ROWS * LANE)
    repl_flat = jnp.pad(repl_flat, (0, N_DEV * chunk - n_repl))
    small_part = _pack([_to_dest_major(grads[n], SMALL[n]) for n in SMALL], F32, SMALL_ROWS, lead=(N_DEV,))
    small_cols = small_part.shape[2]
    small_scatter = _Scatter(
        jnp.concatenate([small_part, repl_flat.reshape(N_DEV, SMALL_ROWS, chunk // SMALL_ROWS)], axis=2))

    _, (small_rs,) = comm.finish(lambda exchanges: (None, _exchange_only(exchanges, name="rs_last")), [small_scatter])
    small_sum = _sum_parts(small_rs, name="rs_sum_small")
    (repl_all,) = _exchange_only([_Gather(small_sum[:, small_cols:])], name="ag_repl")
    repl_sum = repl_all.reshape(-1)
    total_loss = repl_sum[n_repl - 1]

    out = {}
    for n in BIG:
        res = _adamw(_rows_2d(wts[n]), _rows_2d(mom[n]), _rows_2d(var[n]), comm.scattered[n], name="adamw_" + n)
        out[n] = [r.reshape(wts[n].shape) for r in res]

    def small_vec(shard_part, repl_part):
        flat = jnp.concatenate([shard_part.reshape(-1), repl_part])
        return flat.reshape(SMALL_ROWS, -1)

    def small_tree(tree):
        tail = jnp.concatenate([tree[n].reshape(-1) for n in repl])
        tail = jnp.pad(tail, (0, N_DEV * chunk - tail.shape[0]))
        return small_vec(_pack([tree[n] for n in SMALL], F32, SMALL_ROWS), tail)

    res = _adamw(small_tree(wts), small_tree(mom), small_tree(var), small_vec(small_sum[:, :small_cols], repl_sum)[None],
                 name="adamw_small")
    n_small = SMALL_ROWS * small_cols
    for k, r in enumerate(res):
        flat = r.reshape(-1)
        shard = _unpack(flat[:n_small], small_shapes)
        rest = _unpack(flat[n_small:], repl_shapes)
        for n, val in zip(SMALL, shard):
            out.setdefault(n, [None] * 4)[k] = val
        for n, val in zip(repl, rest):
            out.setdefault(n, [None] * 4)[k] = val

    grad_x = dx[None]
    return (total_loss, grad_x, *[out[n][0] for n in WEIGHT_NAMES], *[out[n][1] for n in WEIGHT_NAMES],
            *[out[n][2] for n in WEIGHT_NAMES], *[out[n][3] for n in WEIGHT_NAMES])
```
